```python
import jax, jax.numpy as jnp
from jax import lax
import numpy as np

D_MODEL = 1024
BATCH = 8
SEQ = 4096
DEPTH = 1

CHUNK = 64
POOL_WINDOWS = (2, 4, 8, 16)
N_POOL_GROUPS = len(POOL_WINDOWS)
D_POOL = D_MODEL
POOL_GROUP = D_POOL // N_POOL_GROUPS
SGU_BLOCK = 128
N_SGU_HEADS = 4
D_SGU = D_MODEL
SGU_HEAD = D_SGU // N_SGU_HEADS
D_IN = D_POOL + 2 * D_SGU + D_POOL + D_SGU
D_FF = 4 * D_MODEL
EPS = 1e-6

kernel_name = "hybrid_pool_sgu_gated_block"


def rms_norm(x, g):
    xf = x.astype(jnp.float32)
    y = xf * lax.rsqrt(jnp.mean(xf * xf, axis=-1, keepdims=True) + EPS)
    return (y * g.astype(jnp.float32)).astype(x.dtype)


def layer_norm(x, g, b):
    xf = x.astype(jnp.float32)
    mu = jnp.mean(xf, axis=-1, keepdims=True)
    xc = xf - mu
    y = xc * lax.rsqrt(jnp.mean(xc * xc, axis=-1, keepdims=True) + EPS)
    return (y * g.astype(jnp.float32) + b.astype(jnp.float32)).astype(x.dtype)


def multiscale_pool(p):
    s_len = p.shape[1]
    pf = p.astype(jnp.float32)
    csum = jnp.cumsum(pf, axis=1)
    pos1 = jnp.arange(1, s_len + 1)
    outs = []
    for gi, w in enumerate(POOL_WINDOWS):
        sl = slice(gi * POOL_GROUP, (gi + 1) * POOL_GROUP)
        cg = csum[..., sl]
        prev = jnp.pad(cg, ((0, 0), (w, 0), (0, 0)))[:, :s_len]
        cnt = jnp.minimum(pos1, w).astype(jnp.float32)[None, :, None]
        outs.append((cg - prev) / cnt - pf[..., sl])
    return jnp.stack(outs, axis=2).astype(p.dtype)


def chunk_causal_block_mask():
    pos = jnp.arange(SGU_BLOCK)
    return (pos[:, None] // CHUNK) >= (pos[None, :] // CHUNK)


def _fwd_setup_inputs(seed: int = 0) -> dict:
    key = jax.random.key(seed)
    ks = jax.random.split(key, 20)
    f32 = jnp.float32
    nrm = lambda k, shape, s: jax.random.normal(k, shape, f32) * s
    return {
        "x": jax.random.normal(ks[0], (BATCH, SEQ, D_MODEL), f32),
        "norm1_pre_g": 1.0 + nrm(ks[1], (D_MODEL,), 0.05),
        "w_in": nrm(ks[2], (D_MODEL, D_IN), D_MODEL ** -0.5),
        "b_in": nrm(ks[3], (D_IN,), 0.02),
        "w_pool": nrm(ks[4], (N_POOL_GROUPS, POOL_GROUP, POOL_GROUP), POOL_GROUP ** -0.5),
        "pool_scale": 1.0 + nrm(ks[5], (D_POOL,), 0.1),
        "sgu_ln_g": 1.0 + nrm(ks[6], (D_SGU,), 0.05),
        "sgu_ln_b": nrm(ks[7], (D_SGU,), 0.02),
        "w_spatial": nrm(ks[8], (N_SGU_HEADS, SGU_BLOCK, SGU_BLOCK), SGU_BLOCK ** -0.5),
        "b_spatial": 1.0 + nrm(ks[9], (N_SGU_HEADS, SGU_BLOCK), 0.02),
        "w_sgu_proj": nrm(ks[10], (N_SGU_HEADS, SGU_HEAD, SGU_HEAD), SGU_HEAD ** -0.5),
        "w_out": nrm(ks[11], (D_MODEL, D_MODEL), D_MODEL ** -0.5),
        "norm1_post_g": 1.0 + nrm(ks[12], (D_MODEL,), 0.05),
        "norm2_pre_g": 1.0 + nrm(ks[13], (D_MODEL,), 0.05),
        "w_ff1": nrm(ks[14], (D_MODEL, D_FF), D_MODEL ** -0.5),
        "w_ff2": nrm(ks[15], (D_FF, D_MODEL), D_FF ** -0.5),
        "norm2_post_g": 1.0 + nrm(ks[16], (D_MODEL,), 0.05),
    }


def _fwd_reference(x, norm1_pre_g, w_in, b_in, w_pool, pool_scale, sgu_ln_g, sgu_ln_b,
              w_spatial, b_spatial, w_sgu_proj, w_out, norm1_post_g, norm2_pre_g,
              w_ff1, w_ff2, norm2_post_g):
    bsz, s_len, _ = x.shape
    n_blk = s_len // SGU_BLOCK
    h = x
    for _ in range(DEPTH):
        xn = rms_norm(h, norm1_pre_g)
        z = jnp.einsum('bsd,de->bse', xn, w_in) + b_in
        o = 0
        z_pool = z[..., o:o + D_POOL]; o += D_POOL
        z_u = z[..., o:o + D_SGU]; o += D_SGU
        z_v = z[..., o:o + D_SGU]; o += D_SGU
        z_ga = z[..., o:o + D_POOL]; o += D_POOL
        z_gb = z[..., o:o + D_SGU]

        pooled = multiscale_pool(z_pool)
        a = jnp.einsum('bsgc,gcd->bsgd', pooled, w_pool).reshape(bsz, s_len, D_POOL)
        a = a * pool_scale

        u = jax.nn.gelu(z_u)
        v = layer_norm(jax.nn.gelu(z_v), sgu_ln_g, sgu_ln_b)
        vb = v.reshape(bsz, n_blk, SGU_BLOCK, N_SGU_HEADS, SGU_HEAD)
        ws = jnp.where(chunk_causal_block_mask()[None], w_spatial, 0.0).astype(v.dtype)
        sv = jnp.einsum('hij,bnjhc->bnihc', ws, vb) + b_spatial.T[None, None, :, :, None]
        gated = u.reshape(bsz, n_blk, SGU_BLOCK, N_SGU_HEADS, SGU_HEAD) * sv
        bbr = jnp.einsum('bnihc,hcd->bnihd', gated, w_sgu_proj).reshape(bsz, s_len, D_SGU)

        merged = jax.nn.sigmoid(z_ga) * a + jax.nn.sigmoid(z_gb) * bbr
        y = jnp.einsum('bsd,de->bse', merged, w_out)
        h = h + rms_norm(y, norm1_post_g)

        hn = rms_norm(h, norm2_pre_g)
        f = jnp.square(jax.nn.relu(jnp.einsum('bsd,df->bsf', hn, w_ff1)))
        f = jnp.einsum('bsf,fd->bsd', f, w_ff2)
        h = h + rms_norm(f, norm2_post_g)
    return h


import jax as _jax
import jax.numpy as _jnp

TWIN_FORMAT = 'train_step'
FWD_PARAMS = ['x', 'norm1_pre_g', 'w_in', 'b_in', 'w_pool', 'pool_scale', 'sgu_ln_g', 'sgu_ln_b', 'w_spatial', 'b_spatial', 'w_sgu_proj', 'w_out', 'norm1_post_g', 'norm2_pre_g', 'w_ff1', 'w_ff2', 'norm2_post_g']
TWIN_WEIGHTS = ['norm1_pre_g', 'w_in', 'b_in', 'w_pool', 'pool_scale', 'sgu_ln_g', 'sgu_ln_b', 'w_spatial', 'b_spatial', 'w_sgu_proj', 'w_out', 'norm1_post_g', 'norm2_pre_g', 'w_ff1', 'w_ff2', 'norm2_post_g']
TWIN_DIFF_INPUT = 'x'
TWIN_INPUTS = ['x', 'norm1_pre_g', 'w_in', 'b_in', 'w_pool', 'pool_scale', 'sgu_ln_g', 'sgu_ln_b', 'w_spatial', 'b_spatial', 'w_sgu_proj', 'w_out', 'norm1_post_g', 'norm2_pre_g', 'w_ff1', 'w_ff2', 'norm2_post_g', 'loss_target', 'm_norm1_pre_g', 'm_w_in', 'm_b_in', 'm_w_pool', 'm_pool_scale', 'm_sgu_ln_g', 'm_sgu_ln_b', 'm_w_spatial', 'm_b_spatial', 'm_w_sgu_proj', 'm_w_out', 'm_norm1_post_g', 'm_norm2_pre_g', 'm_w_ff1', 'm_w_ff2', 'm_norm2_post_g', 'v_norm1_pre_g', 'v_w_in', 'v_b_in', 'v_w_pool', 'v_pool_scale', 'v_sgu_ln_g', 'v_sgu_ln_b', 'v_w_spatial', 'v_b_spatial', 'v_w_sgu_proj', 'v_w_out', 'v_norm1_post_g', 'v_norm2_pre_g', 'v_w_ff1', 'v_w_ff2', 'v_norm2_post_g']
TWIN_OUTPUTS = ['loss', 'grad_x', 'grad_norm1_pre_g', 'grad_w_in', 'grad_b_in', 'grad_w_pool', 'grad_pool_scale', 'grad_sgu_ln_g', 'grad_sgu_ln_b', 'grad_w_spatial', 'grad_b_spatial', 'grad_w_sgu_proj', 'grad_w_out', 'grad_norm1_post_g', 'grad_norm2_pre_g', 'grad_w_ff1', 'grad_w_ff2', 'grad_norm2_post_g', 'delta_norm1_pre_g', 'delta_w_in', 'delta_b_in', 'delta_w_pool', 'delta_pool_scale', 'delta_sgu_ln_g', 'delta_sgu_ln_b', 'delta_w_spatial', 'delta_b_spatial', 'delta_w_sgu_proj', 'delta_w_out', 'delta_norm1_post_g', 'delta_norm2_pre_g', 'delta_w_ff1', 'delta_w_ff2', 'delta_norm2_post_g', 'new_m_norm1_pre_g', 'new_m_w_in', 'new_m_b_in', 'new_m_w_pool', 'new_m_pool_scale', 'new_m_sgu_ln_g', 'new_m_sgu_ln_b', 'new_m_w_spatial', 'new_m_b_spatial', 'new_m_w_sgu_proj', 'new_m_w_out', 'new_m_norm1_post_g', 'new_m_norm2_pre_g', 'new_m_w_ff1', 'new_m_w_ff2', 'new_m_norm2_post_g', 'new_v_norm1_pre_g', 'new_v_w_in', 'new_v_b_in', 'new_v_w_pool', 'new_v_pool_scale', 'new_v_sgu_ln_g', 'new_v_sgu_ln_b', 'new_v_w_spatial', 'new_v_b_spatial', 'new_v_w_sgu_proj', 'new_v_w_out', 'new_v_norm1_post_g', 'new_v_norm2_pre_g', 'new_v_w_ff1', 'new_v_w_ff2', 'new_v_norm2_post_g']
TWIN_LEAF_KINDS = {'loss': 'loss', 'grad_x': 'grad_x', 'grad_norm1_pre_g': 'grad_w', 'grad_w_in': 'grad_w', 'grad_b_in': 'grad_w', 'grad_w_pool': 'grad_w', 'grad_pool_scale': 'grad_w', 'grad_sgu_ln_g': 'grad_w', 'grad_sgu_ln_b': 'grad_w', 'grad_w_spatial': 'grad_w', 'grad_b_spatial': 'grad_w', 'grad_w_sgu_proj': 'grad_w', 'grad_w_out': 'grad_w', 'grad_norm1_post_g': 'grad_w', 'grad_norm2_pre_g': 'grad_w', 'grad_w_ff1': 'grad_w', 'grad_w_ff2': 'grad_w', 'grad_norm2_post_g': 'grad_w', 'delta_norm1_pre_g': 'delta_w', 'delta_w_in': 'delta_w', 'delta_b_in': 'delta_w', 'delta_w_pool': 'delta_w', 'delta_pool_scale': 'delta_w', 'delta_sgu_ln_g': 'delta_w', 'delta_sgu_ln_b': 'delta_w', 'delta_w_spatial': 'delta_w', 'delta_b_spatial': 'delta_w', 'delta_w_sgu_proj': 'delta_w', 'delta_w_out': 'delta_w', 'delta_norm1_post_g': 'delta_w', 'delta_norm2_pre_g': 'delta_w', 'delta_w_ff1': 'delta_w', 'delta_w_ff2': 'delta_w', 'delta_norm2_post_g': 'delta_w', 'new_m_norm1_pre_g': 'new_m', 'new_m_w_in': 'new_m', 'new_m_b_in': 'new_m', 'new_m_w_pool': 'new_m', 'new_m_pool_scale': 'new_m', 'new_m_sgu_ln_g': 'new_m', 'new_m_sgu_ln_b': 'new_m', 'new_m_w_spatial': 'new_m', 'new_m_b_spatial': 'new_m', 'new_m_w_sgu_proj': 'new_m', 'new_m_w_out': 'new_m', 'new_m_norm1_post_g': 'new_m', 'new_m_norm2_pre_g': 'new_m', 'new_m_w_ff1': 'new_m', 'new_m_w_ff2': 'new_m', 'new_m_norm2_post_g': 'new_m', 'new_v_norm1_pre_g': 'new_v', 'new_v_w_in': 'new_v', 'new_v_b_in': 'new_v', 'new_v_w_pool': 'new_v', 'new_v_pool_scale': 'new_v', 'new_v_sgu_ln_g': 'new_v', 'new_v_sgu_ln_b': 'new_v', 'new_v_w_spatial': 'new_v', 'new_v_b_spatial': 'new_v', 'new_v_w_sgu_proj': 'new_v', 'new_v_w_out': 'new_v', 'new_v_norm1_post_g': 'new_v', 'new_v_norm2_pre_g': 'new_v', 'new_v_w_ff1': 'new_v', 'new_v_w_ff2': 'new_v', 'new_v_norm2_post_g': 'new_v'}


def _forward(args):
    return _fwd_reference(*[args[k] for k in FWD_PARAMS])


def _output_shape():
    out = _jax.eval_shape(lambda: _forward(_fwd_setup_inputs(0)))
    return out.shape, out.dtype

N_MICROBATCH = 1
ADAM_LR = 0.001
ADAM_B1 = 0.9
ADAM_B2 = 0.999
ADAM_EPS = 1e-08
ADAM_WD = 0.01
ADAM_STEP = 10
PER_EXAMPLE_BATCH_AXIS = {'x': 0, 'loss_target': 0}
SHARED_INPUTS = []
_WEIGHT_DTYPES = {'norm1_pre_g': _jnp.float32, 'w_in': _jnp.float32, 'b_in': _jnp.float32, 'w_pool': _jnp.float32, 'pool_scale': _jnp.float32, 'sgu_ln_g': _jnp.float32, 'sgu_ln_b': _jnp.float32, 'w_spatial': _jnp.float32, 'b_spatial': _jnp.float32, 'w_sgu_proj': _jnp.float32, 'w_out': _jnp.float32, 'norm1_post_g': _jnp.float32, 'norm2_pre_g': _jnp.float32, 'w_ff1': _jnp.float32, 'w_ff2': _jnp.float32, 'norm2_post_g': _jnp.float32}
MOMENT_SCALE = {'norm1_pre_g': 7.140185e-01, 'w_in': 3.242743e-01, 'b_in': 2.223271e+00, 'w_pool': 5.994737e-01, 'pool_scale': 7.160588e-01, 'sgu_ln_g': 2.203826e-01, 'sgu_ln_b': 2.133838e-01, 'w_spatial': 2.587374e-01, 'b_spatial': 3.342111e-01, 'w_sgu_proj': 2.777513e+00, 'w_out': 2.746371e+00, 'norm1_post_g': 3.229999e+01, 'norm2_pre_g': 1.072142e+00, 'w_ff1': 5.203809e-01, 'w_ff2': 2.786802e+00, 'norm2_post_g': 3.319762e+01}


def _to_microbatches(a, axis):
    t = _jnp.moveaxis(a, axis, 0)
    t = t.reshape((N_MICROBATCH, t.shape[0] // N_MICROBATCH) + t.shape[1:])
    return _jnp.moveaxis(t, 1, axis + 1)


def setup_inputs(seed: int = 0) -> dict:
    inp = _fwd_setup_inputs(seed)
    key = _jax.random.fold_in(_jax.random.key(seed), 7919)
    shape, _ = _output_shape()
    out = dict(inp)
    out["loss_target"] = _jax.random.normal(_jax.random.fold_in(key, 0), shape, _jnp.float32)
    for i, name in enumerate(TWIN_WEIGHTS):
        w = inp[name].astype(_jnp.float32)
        if MOMENT_SCALE is None:
            s = _jnp.sqrt(_jnp.mean(_jnp.square(w)) + 1e-30)
        else:
            s = MOMENT_SCALE[name]
        km, kv = _jax.random.split(_jax.random.fold_in(key, i + 1))
        out[name] = w
        out["m_" + name] = s * _jax.random.normal(km, w.shape, _jnp.float32)
        out["v_" + name] = (s * s) * _jax.random.uniform(kv, w.shape, _jnp.float32, 0.5, 1.5)
    if N_MICROBATCH > 1:
        for name, axis in PER_EXAMPLE_BATCH_AXIS.items():
            out[name] = _to_microbatches(out[name], axis)
    return {'x': out['x'], 'norm1_pre_g': out['norm1_pre_g'], 'w_in': out['w_in'], 'b_in': out['b_in'], 'w_pool': out['w_pool'], 'pool_scale': out['pool_scale'], 'sgu_ln_g': out['sgu_ln_g'], 'sgu_ln_b': out['sgu_ln_b'], 'w_spatial': out['w_spatial'], 'b_spatial': out['b_spatial'], 'w_sgu_proj': out['w_sgu_proj'], 'w_out': out['w_out'], 'norm1_post_g': out['norm1_post_g'], 'norm2_pre_g': out['norm2_pre_g'], 'w_ff1': out['w_ff1'], 'w_ff2': out['w_ff2'], 'norm2_post_g': out['norm2_post_g'], 'loss_target': out['loss_target'], 'm_norm1_pre_g': out['m_norm1_pre_g'], 'm_w_in': out['m_w_in'], 'm_b_in': out['m_b_in'], 'm_w_pool': out['m_w_pool'], 'm_pool_scale': out['m_pool_scale'], 'm_sgu_ln_g': out['m_sgu_ln_g'], 'm_sgu_ln_b': out['m_sgu_ln_b'], 'm_w_spatial': out['m_w_spatial'], 'm_b_spatial': out['m_b_spatial'], 'm_w_sgu_proj': out['m_w_sgu_proj'], 'm_w_out': out['m_w_out'], 'm_norm1_post_g': out['m_norm1_post_g'], 'm_norm2_pre_g': out['m_norm2_pre_g'], 'm_w_ff1': out['m_w_ff1'], 'm_w_ff2': out['m_w_ff2'], 'm_norm2_post_g': out['m_norm2_post_g'], 'v_norm1_pre_g': out['v_norm1_pre_g'], 'v_w_in': out['v_w_in'], 'v_b_in': out['v_b_in'], 'v_w_pool': out['v_w_pool'], 'v_pool_scale': out['v_pool_scale'], 'v_sgu_ln_g': out['v_sgu_ln_g'], 'v_sgu_ln_b': out['v_sgu_ln_b'], 'v_w_spatial': out['v_w_spatial'], 'v_b_spatial': out['v_b_spatial'], 'v_w_sgu_proj': out['v_w_sgu_proj'], 'v_w_out': out['v_w_out'], 'v_norm1_post_g': out['v_norm1_post_g'], 'v_norm2_pre_g': out['v_norm2_pre_g'], 'v_w_ff1': out['v_w_ff1'], 'v_w_ff2': out['v_w_ff2'], 'v_norm2_post_g': out['v_norm2_post_g']}


def _loss(weights, diff, rest, loss_target):
    with _jax.named_scope("forward"):
        args = {**rest, TWIN_DIFF_INPUT: diff, **{k: w.astype(_WEIGHT_DTYPES[k]) for k, w in weights.items()}}
        y = _forward(args)
    with _jax.named_scope("loss_head"):
        err = _jnp.square(y.astype(_jnp.float32) - loss_target)
        return 0.5 * _jnp.sum(_jnp.mean(err, axis=-1)) if err.ndim else 0.5 * err


def _adamw(w, g, m, v):
    m = ADAM_B1 * m + (1.0 - ADAM_B1) * g
    v = ADAM_B2 * v + (1.0 - ADAM_B2) * _jnp.square(g)
    m_hat = m / (1.0 - ADAM_B1 ** ADAM_STEP)
    v_hat = v / (1.0 - ADAM_B2 ** ADAM_STEP)
    delta = -ADAM_LR * (m_hat / (_jnp.sqrt(v_hat) + ADAM_EPS) + ADAM_WD * w)
    return delta, m, v


def reference(x, norm1_pre_g, w_in, b_in, w_pool, pool_scale, sgu_ln_g, sgu_ln_b, w_spatial, b_spatial, w_sgu_proj, w_out, norm1_post_g, norm2_pre_g, w_ff1, w_ff2, norm2_post_g, loss_target, m_norm1_pre_g, m_w_in, m_b_in, m_w_pool, m_pool_scale, m_sgu_ln_g, m_sgu_ln_b, m_w_spatial, m_b_spatial, m_w_sgu_proj, m_w_out, m_norm1_post_g, m_norm2_pre_g, m_w_ff1, m_w_ff2, m_norm2_post_g, v_norm1_pre_g, v_w_in, v_b_in, v_w_pool, v_pool_scale, v_sgu_ln_g, v_sgu_ln_b, v_w_spatial, v_b_spatial, v_w_sgu_proj, v_w_out, v_norm1_post_g, v_norm2_pre_g, v_w_ff1, v_w_ff2, v_norm2_post_g):
    given = dict(x=x, norm1_pre_g=norm1_pre_g, w_in=w_in, b_in=b_in, w_pool=w_pool, pool_scale=pool_scale, sgu_ln_g=sgu_ln_g, sgu_ln_b=sgu_ln_b, w_spatial=w_spatial, b_spatial=b_spatial, w_sgu_proj=w_sgu_proj, w_out=w_out, norm1_post_g=norm1_post_g, norm2_pre_g=norm2_pre_g, w_ff1=w_ff1, w_ff2=w_ff2, norm2_post_g=norm2_post_g, loss_target=loss_target, m_norm1_pre_g=m_norm1_pre_g, m_w_in=m_w_in, m_b_in=m_b_in, m_w_pool=m_w_pool, m_pool_scale=m_pool_scale, m_sgu_ln_g=m_sgu_ln_g, m_sgu_ln_b=m_sgu_ln_b, m_w_spatial=m_w_spatial, m_b_spatial=m_b_spatial, m_w_sgu_proj=m_w_sgu_proj, m_w_out=m_w_out, m_norm1_post_g=m_norm1_post_g, m_norm2_pre_g=m_norm2_pre_g, m_w_ff1=m_w_ff1, m_w_ff2=m_w_ff2, m_norm2_post_g=m_norm2_post_g, v_norm1_pre_g=v_norm1_pre_g, v_w_in=v_w_in, v_b_in=v_b_in, v_w_pool=v_w_pool, v_pool_scale=v_pool_scale, v_sgu_ln_g=v_sgu_ln_g, v_sgu_ln_b=v_sgu_ln_b, v_w_spatial=v_w_spatial, v_b_spatial=v_b_spatial, v_w_sgu_proj=v_w_sgu_proj, v_w_out=v_w_out, v_norm1_post_g=v_norm1_post_g, v_norm2_pre_g=v_norm2_pre_g, v_w_ff1=v_w_ff1, v_w_ff2=v_w_ff2, v_norm2_post_g=v_norm2_post_g)
    weights = {n: given[n] for n in TWIN_WEIGHTS}
    shared = {n: given[n] for n in SHARED_INPUTS}
    per_example = {n: given[n] for n in ['x']}
    grad_fn = _jax.value_and_grad(_loss, argnums=(0, 1))

    def one_microbatch(ex, loss_target):
        ex = dict(ex)
        diff = ex.pop(TWIN_DIFF_INPUT)
        return grad_fn(weights, diff, {**shared, **ex}, loss_target)

    if N_MICROBATCH == 1:
        loss, (grad_w, grad_x) = one_microbatch(per_example, given["loss_target"])
    else:
        def body(carry, xs):
            loss_sum, grad_sum = carry
            l_k, (gw_k, gx_k) = one_microbatch(xs[0], xs[1])
            with _jax.named_scope("update"):
                return (loss_sum + l_k, _jax.tree.map(_jnp.add, grad_sum, gw_k)), gx_k

        init = (_jnp.zeros((), _jnp.float32), _jax.tree.map(_jnp.zeros_like, weights))
        (loss, grad_w), grad_x = _jax.lax.scan(body, init, (per_example, given["loss_target"]))
    with _jax.named_scope("update"):
        delta_w, new_m, new_v = {}, {}, {}
        for n in TWIN_WEIGHTS:
            delta_w[n], new_m[n], new_v[n] = _adamw(weights[n], grad_w[n], given["m_" + n], given["v_" + n])
    return (loss, grad_x, *[grad_w[n] for n in TWIN_WEIGHTS], *[delta_w[n] for n in TWIN_WEIGHTS],
            *[new_m[n] for n in TWIN_WEIGHTS], *[new_v[n] for n in TWIN_WEIGHTS])
```

```python
import functools
import math

import jax
import jax.numpy as jnp
from jax import lax
from jax.experimental import pallas as pl
from jax.experimental.pallas import tpu as pltpu

F32, BF16 = jnp.float32, jnp.bfloat16
MESH = pl.DeviceIdType.MESH

D_MODEL = 1024
D_IN = 5120
D_FF = 4096
N_DEV = 8
N_CHIPS = 4
WINDOWS = (2, 4, 8, 16)
N_GROUPS = 4
GROUP = 256
HALO = 16
SGU_BLOCK = 128
N_HEADS = 4
HEAD = 256
CHUNK = 64
EPS = 1e-6
IN_SHARD = D_IN // N_DEV
FF_SHARD = D_FF // N_DEV
OUT_SHARD = D_MODEL // N_DEV
PG_SHARD = GROUP // N_DEV

ADAM_LR, ADAM_B1, ADAM_B2, ADAM_EPS, ADAM_WD, ADAM_STEP = 0.001, 0.9, 0.999, 1e-08, 0.01, 10

VMEM_LIMIT = 56 * 1024 * 1024
TM = 256
TM_BWD = 128
GELU_C0 = math.sqrt(2.0 / math.pi)
GELU_C1 = 0.044715


def _dot(a, b):
    return jnp.dot(a, b, preferred_element_type=F32)


def _dot_nt(a, b):
    return lax.dot_general(a, b, (((1,), (1,)), ((), ())), preferred_element_type=F32)


def _dot_tn(a, b):
    return lax.dot_general(a, b, (((0,), (0,)), ((), ())), preferred_element_type=F32)


def _gelu(x):
    t = jnp.tanh(GELU_C0 * (x + GELU_C1 * (x * x * x)))
    return 0.5 * x * (1.0 + t), t


def _gelu_grad(x, t):
    return 0.5 * (1.0 + t) + 0.5 * x * (1.0 - t * t) * (GELU_C0 * (1.0 + 3.0 * GELU_C1 * (x * x)))


def _sigmoid(x):
    return 1.0 / (1.0 + jnp.exp(-x))


def _mean(x):
    return jnp.mean(x, axis=-1, keepdims=True)


def _colsum(x):
    return jnp.sum(x, axis=0, keepdims=True)


def _const_spec(shape):
    nd = len(shape)
    return pl.BlockSpec(shape, lambda *_: (0,) * nd, pipeline_mode=pl.Buffered(1))


def _acc_spec(shape):
    nd = len(shape)
    return pl.BlockSpec(shape, lambda *_: (0,) * nd)


def _masked_ws(ws_ref):
    ri = lax.broadcasted_iota(jnp.int32, (SGU_BLOCK, SGU_BLOCK), 0) // CHUNK
    ci = lax.broadcasted_iota(jnp.int32, (SGU_BLOCK, SGU_BLOCK), 1) // CHUNK
    return [jnp.where(ri >= ci, ws_ref[h], 0.0).astype(BF16) for h in range(N_HEADS)]


def _pool_fwd(pbuf, tile_idx, tm):
    pos = lax.broadcasted_iota(jnp.int32, (tm, 1), 0) + tile_idx * tm + 1
    pooled = []
    for g, w in enumerate(WINDOWS):
        e = pbuf[:, g * GROUP:(g + 1) * GROUP]
        s, sh = e, 1
        while sh < w:
            s = s + pltpu.roll(s, sh, 0)
            sh *= 2
        inv = 1.0 / jnp.minimum(pos, w).astype(F32)
        pooled.append(s[HALO:] * inv - e[HALO:])
    return pooled


def _sgu_fwd(zu, zv, lng, lnb, wsm, bsp_ref, tm):
    u, tu = _gelu(zu)
    gv, tv = _gelu(zv)
    xc = gv - _mean(gv)
    rln = lax.rsqrt(_mean(xc * xc) + EPS)
    xhat = xc * rln
    vb = (xhat * lng + lnb).astype(BF16)
    sv_heads = []
    for h in range(N_HEADS):
        rows = []
        for n in range(tm // SGU_BLOCK):
            blk = vb[n * SGU_BLOCK:(n + 1) * SGU_BLOCK, h * HEAD:(h + 1) * HEAD]
            rows.append(_dot(wsm[h], blk) + bsp_ref[:, h:h + 1])
        sv_heads.append(jnp.concatenate(rows, axis=0))
    return u, tu, tv, xhat, rln, vb, sv_heads


def _fwd_mix(x, g1pre, win_g, b_in, wpool, pool_scale, lng, lnb, ws, bsp_t, wproj, wout, g1post):
    t_len = x.shape[0]
    tm = TM
    nt = t_len // tm

    def body(x_ref, g1_ref, win_ref, bin_ref, wpool_ref, ps_ref, lng_ref, lnb_ref, ws_ref, bsp_ref, wproj_ref,
             wout_ref, g1post_ref, z_ref, xn_ref, y_ref, h1_ref, pbuf):
        i = pl.program_id(0)
        xv = x_ref[...]
        r1 = lax.rsqrt(_mean(xv * xv) + EPS)
        xnb = (xv * r1 * g1_ref[...]).astype(BF16)
        xn_ref[...] = xnb
        for k in range(N_DEV):
            cols = slice(k * IN_SHARD, (k + 1) * IN_SHARD)
            z_ref[:, cols] = _dot(xnb, win_ref[k]) + bin_ref[:, cols]

        @pl.when(i == 0)
        def _():
            pbuf[0:HALO, :] = jnp.zeros((HALO, D_MODEL), F32)

        pbuf[HALO:, :] = z_ref[:, 0:D_MODEL]
        pooled = _pool_fwd(pbuf, i, tm)
        pbuf[0:HALO, :] = pbuf[tm:tm + HALO, :]
        a = jnp.concatenate([_dot(pooled[g].astype(BF16), wpool_ref[g]) for g in range(N_GROUPS)], axis=1)
        a = a * ps_ref[...]

        wsm = _masked_ws(ws_ref)
        u, _, _, _, _, _, sv_heads = _sgu_fwd(z_ref[:, D_MODEL:2 * D_MODEL], z_ref[:, 2 * D_MODEL:3 * D_MODEL],
                                              lng_ref[...], lnb_ref[...], wsm, bsp_ref, tm)
        bbr = jnp.concatenate(
            [_dot((u[:, h * HEAD:(h + 1) * HEAD] * sv_heads[h]).astype(BF16), wproj_ref[h]) for h in range(N_HEADS)],
            axis=1)
        merged = _sigmoid(z_ref[:, 3 * D_MODEL:4 * D_MODEL]) * a + _sigmoid(z_ref[:, 4 * D_MODEL:5 * D_MODEL]) * bbr
        yv = _dot(merged.astype(BF16), wout_ref[...])
        y_ref[...] = yv
        ry = lax.rsqrt(_mean(yv * yv) + EPS)
        h1_ref[...] = xv + yv * ry * g1post_ref[...]

    tok = lambda w: pl.BlockSpec((tm, w), lambda i: (i, 0))
    return pl.pallas_call(
        body, name="fwd_mix", grid=(nt,),
        in_specs=[tok(D_MODEL), _const_spec((1, D_MODEL)), _const_spec((N_DEV, D_MODEL, IN_SHARD)),
                  _const_spec((1, D_IN)), _const_spec((N_GROUPS, GROUP, GROUP)), _const_spec((1, D_MODEL)),
                  _const_spec((1, D_MODEL)), _const_spec((1, D_MODEL)),
                  _const_spec((N_HEADS, SGU_BLOCK, SGU_BLOCK)), _const_spec((SGU_BLOCK, N_HEADS)),
                  _const_spec((N_HEADS, HEAD, HEAD)), _const_spec((D_MODEL, D_MODEL)), _const_spec((1, D_MODEL))],
        out_specs=[tok(D_IN), tok(D_MODEL), tok(D_MODEL), tok(D_MODEL)],
        out_shape=[jax.ShapeDtypeStruct((t_len, D_IN), F32), jax.ShapeDtypeStruct((t_len, D_MODEL), BF16),
                   jax.ShapeDtypeStruct((t_len, D_MODEL), F32), jax.ShapeDtypeStruct((t_len, D_MODEL), F32)],
        scratch_shapes=[pltpu.VMEM((tm + HALO, D_MODEL), F32)],
        compiler_params=pltpu.CompilerParams(dimension_semantics=("arbitrary",), vmem_limit_bytes=VMEM_LIMIT),
    )(x, g1pre, win_g, b_in, wpool, pool_scale, lng, lnb, ws, bsp_t, wproj, wout, g1post)


def _mlp(h1, target, g2pre, g2post, w1_g, w2):
    t_len = h1.shape[0]
    tm = TM
    nt = t_len // tm

    def body(h1_ref, tgt_ref, g2pre_ref, g2post_ref, w1_ref, w2_ref,
             hn_ref, f_ref, df1_ref, df2_ref, dh1_ref, dg2post_ref, dg2pre_ref, loss_ref, f1_scr):
        i = pl.program_id(0)

        @pl.when(i == 0)
        def _():
            dg2post_ref[...] = jnp.zeros_like(dg2post_ref)
            dg2pre_ref[...] = jnp.zeros_like(dg2pre_ref)
            loss_ref[...] = jnp.zeros_like(loss_ref)

        h = h1_ref[...]
        r2 = lax.rsqrt(_mean(h * h) + EPS)
        nh = h * r2
        hnb = (nh * g2pre_ref[...]).astype(BF16)
        hn_ref[...] = hnb
        for k in range(N_DEV):
            f1_scr[:, k * FF_SHARD:(k + 1) * FF_SHARD] = _dot(hnb, w1_ref[k])
        r = jnp.maximum(f1_scr[...], 0.0)
        fb = (r * r).astype(BF16)
        f_ref[...] = fb
        f2 = _dot(fb, w2_ref[...])
        rf = lax.rsqrt(_mean(f2 * f2) + EPS)
        nf = f2 * rf
        diff = h + nf * g2post_ref[...] - tgt_ref[...]
        loss_ref[...] += (0.5 / D_MODEL) * jnp.sum(diff * diff)
        dout = diff * (1.0 / D_MODEL)
        dg2post_ref[...] += _colsum(dout * nf)
        dn = dout * g2post_ref[...]
        df2b = (rf * (dn - nf * _mean(dn * nf))).astype(BF16)
        df2_ref[...] = df2b
        df = _dot_nt(df2b, w2_ref[...])
        df1b = (df * (2.0 * jnp.maximum(f1_scr[...], 0.0))).astype(BF16)
        df1_ref[...] = df1b
        dhn = _dot_nt(df1b[:, 0:FF_SHARD], w1_ref[0])
        for k in range(1, N_DEV):
            dhn = dhn + _dot_nt(df1b[:, k * FF_SHARD:(k + 1) * FF_SHARD], w1_ref[k])
        dg2pre_ref[...] += _colsum(dhn * nh)
        dnh = dhn * g2pre_ref[...]
        dh1_ref[...] = dout + r2 * (dnh - nh * _mean(dnh * nh))

    tok = lambda w: pl.BlockSpec((tm, w), lambda i: (i, 0))
    return pl.pallas_call(
        body, name="mlp_fwd_bwd", grid=(nt,),
        in_specs=[tok(D_MODEL), tok(D_MODEL), _const_spec((1, D_MODEL)), _const_spec((1, D_MODEL)),
                  _const_spec((N_DEV, D_MODEL, FF_SHARD)), _const_spec((D_FF, D_MODEL))],
        out_specs=[tok(D_MODEL), tok(D_FF), tok(D_FF), tok(D_MODEL), tok(D_MODEL),
                   _acc_spec((1, D_MODEL)), _acc_spec((1, D_MODEL)), _acc_spec((1, 128))],
        out_shape=[jax.ShapeDtypeStruct((t_len, D_MODEL), BF16), jax.ShapeDtypeStruct((t_len, D_FF), BF16),
                   jax.ShapeDtypeStruct((t_len, D_FF), BF16), jax.ShapeDtypeStruct((t_len, D_MODEL), BF16),
                   jax.ShapeDtypeStruct((t_len, D_MODEL), F32), jax.ShapeDtypeStruct((1, D_MODEL), F32),
                   jax.ShapeDtypeStruct((1, D_MODEL), F32), jax.ShapeDtypeStruct((1, 128), F32)],
        scratch_shapes=[pltpu.VMEM((tm, D_FF), F32)],
        compiler_params=pltpu.CompilerParams(dimension_semantics=("arbitrary",), vmem_limit_bytes=VMEM_LIMIT),
    )(h1, target, g2pre, g2post, w1_g, w2)


def _bwd_mix(dh1, x, y, z, g1pre, win_g, wpool, pool_scale, lng, lnb, ws, bsp_t, wproj, wout, g1post):
    t_len = x.shape[0]
    tm = TM_BWD
    nt = t_len // tm
    nblk = tm // SGU_BLOCK

    def body(dh1_ref, x_ref, y_ref, z_ref, zh_ref, g1_ref, win_ref, wpool_ref, ps_ref, lng_ref, lnb_ref, ws_ref,
             bsp_ref, wproj_ref, wout_ref, g1post_ref,
             dx_ref, dz_ref, mg_ref, dy_ref, dwpool_ref, dwproj_ref, dws_ref, dbsp_ref, dg1post_ref, dps_ref,
             dlng_ref, dlnb_ref, dg1pre_ref, dbin_ref, pbuf, qbuf):
        i = pl.program_id(0)
        ti = nt - 1 - i

        @pl.when(i == 0)
        def _():
            for ref in (dwpool_ref, dwproj_ref, dws_ref, dbsp_ref, dg1post_ref, dps_ref, dlng_ref, dlnb_ref,
                        dg1pre_ref, dbin_ref):
                ref[...] = jnp.zeros_like(ref)
            qbuf[tm:tm + HALO, :] = jnp.zeros((HALO, D_MODEL), F32)

        pbuf[0:HALO, :] = jnp.where(ti > 0, zh_ref[...], 0.0)
        pbuf[HALO:, :] = z_ref[:, 0:D_MODEL]
        pooled = _pool_fwd(pbuf, ti, tm)
        pooled_b = [p.astype(BF16) for p in pooled]
        a_raw = jnp.concatenate([_dot(pooled_b[g], wpool_ref[g]) for g in range(N_GROUPS)], axis=1)
        wsm = _masked_ws(ws_ref)
        zu = z_ref[:, D_MODEL:2 * D_MODEL]
        zv = z_ref[:, 2 * D_MODEL:3 * D_MODEL]
        u, tu, tv, xhat, rln, vb, sv_heads = _sgu_fwd(zu, zv, lng_ref[...], lnb_ref[...], wsm, bsp_ref, tm)
        gated_b = [(u[:, h * HEAD:(h + 1) * HEAD] * sv_heads[h]).astype(BF16) for h in range(N_HEADS)]
        bbr = jnp.concatenate([_dot(gated_b[h], wproj_ref[h]) for h in range(N_HEADS)], axis=1)
        sa = _sigmoid(z_ref[:, 3 * D_MODEL:4 * D_MODEL])
        sb = _sigmoid(z_ref[:, 4 * D_MODEL:5 * D_MODEL])
        a = a_raw * ps_ref[...]
        mg_ref[...] = (sa * a + sb * bbr).astype(BF16)

        dh = dh1_ref[...]
        yv = y_ref[...]
        ry = lax.rsqrt(_mean(yv * yv) + EPS)
        ny = yv * ry
        dg1post_ref[...] += _colsum(dh * ny)
        dn = dh * g1post_ref[...]
        dyb = (ry * (dn - ny * _mean(dn * ny))).astype(BF16)
        dy_ref[...] = dyb
        dmg = _dot_nt(dyb, wout_ref[...])

        da = dmg * sa
        dbbr = dmg * sb
        dzga = dmg * a * sa * (1.0 - sa)
        dzgb = dmg * bbr * sb * (1.0 - sb)
        dz_ref[:, 3 * D_MODEL:4 * D_MODEL] = dzga.astype(BF16)
        dz_ref[:, 4 * D_MODEL:5 * D_MODEL] = dzgb.astype(BF16)
        dbin_ref[:, 3 * D_MODEL:4 * D_MODEL] += _colsum(dzga)
        dbin_ref[:, 4 * D_MODEL:5 * D_MODEL] += _colsum(dzgb)

        dps_ref[...] += _colsum(da * a_raw)
        da_raw_b = (da * ps_ref[...]).astype(BF16)
        pos = lax.broadcasted_iota(jnp.int32, (tm, 1), 0) + ti * tm + 1
        dpooled = []
        for g, w in enumerate(WINDOWS):
            cols = slice(g * GROUP, (g + 1) * GROUP)
            dwpool_ref[g] += _dot_tn(pooled_b[g], da_raw_b[:, cols])
            dp = _dot_nt(da_raw_b[:, cols], wpool_ref[g])
            dpooled.append(dp)
            qbuf[0:tm, cols] = dp * (1.0 / jnp.minimum(pos, w).astype(F32))
        n_ext = tm + HALO
        dzp = []
        for g, w in enumerate(WINDOWS):
            e = qbuf[:, g * GROUP:(g + 1) * GROUP]
            s, sh = e, 1
            while sh < w:
                s = s + pltpu.roll(s, n_ext - sh, 0)
                sh *= 2
            dzp.append(s[0:tm] - dpooled[g])
        qbuf[tm:tm + HALO, :] = qbuf[0:HALO, :]
        dzp = jnp.concatenate(dzp, axis=1)
        dz_ref[:, 0:D_MODEL] = dzp.astype(BF16)
        dbin_ref[:, 0:D_MODEL] += _colsum(dzp)

        dv_heads = []
        du_heads = []
        for h in range(N_HEADS):
            cols = slice(h * HEAD, (h + 1) * HEAD)
            dbbr_b = dbbr[:, cols].astype(BF16)
            dwproj_ref[h] += _dot_tn(gated_b[h], dbbr_b)
            dgated = _dot_nt(dbbr_b, wproj_ref[h])
            du_heads.append(dgated * sv_heads[h])
            dsv = dgated * u[:, cols]
            dsv_b = dsv.astype(BF16)
            rows = []
            for n in range(nblk):
                blk = slice(n * SGU_BLOCK, (n + 1) * SGU_BLOCK)
                rows.append(_dot_tn(wsm[h], dsv_b[blk]))
                dws_ref[h] += _dot_nt(dsv_b[blk], vb[blk, cols])
                dbsp_ref[:, h:h + 1] += jnp.sum(dsv[blk], axis=1, keepdims=True)
            dv_heads.append(jnp.concatenate(rows, axis=0))
        dzu = jnp.concatenate(du_heads, axis=1) * _gelu_grad(zu, tu)
        dz_ref[:, D_MODEL:2 * D_MODEL] = dzu.astype(BF16)
        dbin_ref[:, D_MODEL:2 * D_MODEL] += _colsum(dzu)
        dv = jnp.concatenate(dv_heads, axis=1)
        dlng_ref[...] += _colsum(dv * xhat)
        dlnb_ref[...] += _colsum(dv)
        dxh = dv * lng_ref[...]
        dgv = rln * (dxh - _mean(dxh) - xhat * _mean(dxh * xhat))
        dzv = dgv * _gelu_grad(zv, tv)
        dz_ref[:, 2 * D_MODEL:3 * D_MODEL] = dzv.astype(BF16)
        dbin_ref[:, 2 * D_MODEL:3 * D_MODEL] += _colsum(dzv)

        dxn = _dot_nt(dz_ref[:, 0:IN_SHARD], win_ref[0])
        for k in range(1, N_DEV):
            dxn = dxn + _dot_nt(dz_ref[:, k * IN_SHARD:(k + 1) * IN_SHARD], win_ref[k])
        xv = x_ref[...]
        r1 = lax.rsqrt(_mean(xv * xv) + EPS)
        nx = xv * r1
        dg1pre_ref[...] += _colsum(dxn * nx)
        dnx = dxn * g1_ref[...]
        dx_ref[...] = r1 * (dnx - nx * _mean(dnx * nx)) + dh

        @pl.when(i == nt - 1)
        def _():
            ri = lax.broadcasted_iota(jnp.int32, (SGU_BLOCK, SGU_BLOCK), 0) // CHUNK
            ci = lax.broadcasted_iota(jnp.int32, (SGU_BLOCK, SGU_BLOCK), 1) // CHUNK
            for h in range(N_HEADS):
                dws_ref[h] = jnp.where(ri >= ci, dws_ref[h], 0.0)

    tok = lambda w: pl.BlockSpec((tm, w), lambda i: (nt - 1 - i, 0))
    halo = pl.BlockSpec((HALO, D_MODEL), lambda i: (jnp.maximum((nt - 1 - i) * (tm // HALO) - 1, 0), 0))
    return pl.pallas_call(
        body, name="bwd_mix", grid=(nt,),
        in_specs=[tok(D_MODEL), tok(D_MODEL), tok(D_MODEL), tok(D_IN), halo, _const_spec((1, D_MODEL)),
                  _const_spec((N_DEV, D_MODEL, IN_SHARD)), _const_spec((N_GROUPS, GROUP, GROUP)),
                  _const_spec((1, D_MODEL)), _const_spec((1, D_MODEL)), _const_spec((1, D_MODEL)),
                  _const_spec((N_HEADS, SGU_BLOCK, SGU_BLOCK)), _const_spec((SGU_BLOCK, N_HEADS)),
                  _const_spec((N_HEADS, HEAD, HEAD)), _const_spec((D_MODEL, D_MODEL)), _const_spec((1, D_MODEL))],
        out_specs=[tok(D_MODEL), tok(D_IN), tok(D_MODEL), tok(D_MODEL),
                   _acc_spec((N_GROUPS, GROUP, GROUP)), _acc_spec((N_HEADS, HEAD, HEAD)),
                   _acc_spec((N_HEADS, SGU_BLOCK, SGU_BLOCK)), _acc_spec((SGU_BLOCK, N_HEADS)),
                   _acc_spec((1, D_MODEL)), _acc_spec((1, D_MODEL)), _acc_spec((1, D_MODEL)), _acc_spec((1, D_MODEL)),
                   _acc_spec((1, D_MODEL)), _acc_spec((1, D_IN))],
        out_shape=[jax.ShapeDtypeStruct((t_len, D_MODEL), F32), jax.ShapeDtypeStruct((t_len, D_IN), BF16),
                   jax.ShapeDtypeStruct((t_len, D_MODEL), BF16), jax.ShapeDtypeStruct((t_len, D_MODEL), BF16),
                   jax.ShapeDtypeStruct((N_GROUPS, GROUP, GROUP), F32), jax.ShapeDtypeStruct((N_HEADS, HEAD, HEAD), F32),
                   jax.ShapeDtypeStruct((N_HEADS, SGU_BLOCK, SGU_BLOCK), F32),
                   jax.ShapeDtypeStruct((SGU_BLOCK, N_HEADS), F32),
                   jax.ShapeDtypeStruct((1, D_MODEL), F32), jax.ShapeDtypeStruct((1, D_MODEL), F32),
                   jax.ShapeDtypeStruct((1, D_MODEL), F32), jax.ShapeDtypeStruct((1, D_MODEL), F32),
                   jax.ShapeDtypeStruct((1, D_MODEL), F32), jax.ShapeDtypeStruct((1, D_IN), F32)],
        scratch_shapes=[pltpu.VMEM((tm + HALO, D_MODEL), F32), pltpu.VMEM((tm + HALO, D_MODEL), F32)],
        compiler_params=pltpu.CompilerParams(dimension_semantics=("arbitrary",), vmem_limit_bytes=VMEM_LIMIT),
    )(dh1, x, y, z, z, g1pre, win_g, wpool, pool_scale, lng, lnb, ws, bsp_t, wproj, wout, g1post)


def _owner_of_slot(s):
    return 4 * ((s // 2) % 2) + 2 * (s % 2) + s // 4


def _wgrad(name, a, b, shard, a_sliced, tk=1024):
    t_len = a.shape[0]
    tk = min(tk, t_len)
    nk = t_len // tk
    am = shard if a_sliced else a.shape[1]
    bn = b.shape[1] if a_sliced else shard

    def body(a_ref, b_ref, o_ref, acc):
        k = pl.program_id(1)

        @pl.when(k == 0)
        def _():
            acc[...] = jnp.zeros_like(acc)

        acc[...] += _dot_tn(a_ref[...], b_ref[...])

        @pl.when(k == nk - 1)
        def _():
            o_ref[...] = acc[...].astype(BF16)

    a_map = (lambda s, k: (k, _owner_of_slot(s))) if a_sliced else (lambda s, k: (k, 0))
    b_map = (lambda s, k: (k, 0)) if a_sliced else (lambda s, k: (k, _owner_of_slot(s)))
    return pl.pallas_call(
        body, name=name, grid=(N_DEV, nk),
        in_specs=[pl.BlockSpec((tk, am), a_map), pl.BlockSpec((tk, bn), b_map)],
        out_specs=pl.BlockSpec((None, am, bn), lambda s, k: (s, 0, 0)),
        out_shape=jax.ShapeDtypeStruct((N_DEV, am, bn), BF16),
        scratch_shapes=[pltpu.VMEM((am, bn), F32)],
        compiler_params=pltpu.CompilerParams(dimension_semantics=("arbitrary", "arbitrary"),
                                             vmem_limit_bytes=VMEM_LIMIT),
    )(a, b)


def _coords():
    return lax.axis_index("x"), lax.axis_index("y"), lax.axis_index("c")


_ANY = pl.BlockSpec(memory_space=pl.ANY)


def _all_gather(shards):
    n = len(shards)

    def body(*refs):
        src, dst = refs[:n], refs[n:2 * n]
        send_sems, recv_sems, local_sems = refs[2 * n:]
        x, y, c = _coords()
        me, sibling = (x, y, c), (x, y, 1 - c)
        chips = [(1 - x, y), (x, 1 - y), (1 - x, 1 - y)]

        def slot(j, dev):
            return dst[j].at[4 * dev[0] + 2 * dev[1] + dev[2]]

        def copy(j, k, block, to, from_shard=False):
            return pltpu.make_async_remote_copy(
                src_ref=src[j] if from_shard else slot(j, block), dst_ref=slot(j, block),
                send_sem=send_sems.at[j, k], recv_sem=recv_sems.at[j, k], device_id=to, device_id_type=MESH)

        started = []
        local = []
        for j in range(n):
            for k, to in enumerate([(*chip, c) for chip in chips]):
                cp = copy(j, 1 + k, me, to, from_shard=True)
                cp.start()
                started.append(cp)
        for j in range(n):
            cp = copy(j, 0, me, sibling, from_shard=True)
            cp.start()
            started.append(cp)
            lc = pltpu.make_async_copy(src[j], slot(j, me), local_sems.at[j])
            lc.start()
            local.append(lc)
        for j in range(n):
            for k, chip in enumerate(chips):
                copy(j, 1 + k, (*chip, c), me).wait_recv()
                cp = copy(j, 4 + k, (*chip, c), sibling)
                cp.start()
                started.append(cp)
        for j in range(n):
            copy(j, 0, sibling, me).wait_recv()
            for k, chip in enumerate(chips):
                copy(j, 4 + k, (*chip, 1 - c), me).wait_recv()
        for cp in started:
            cp.wait_send()
        for lc in local:
            lc.wait()

    return pl.pallas_call(
        body, name="all_gather_weights",
        in_specs=[_ANY] * n, out_specs=[_ANY] * n,
        out_shape=[jax.ShapeDtypeStruct((N_DEV,) + s.shape, s.dtype) for s in shards],
        scratch_shapes=[pltpu.SemaphoreType.DMA((n, 7)), pltpu.SemaphoreType.DMA((n, 7)),
                        pltpu.SemaphoreType.DMA((n,))],
    )(*shards)


def _pair_exchange(bigs, smalls):
    nb, n = len(bigs), len(bigs) + len(smalls)

    def body(*refs):
        src, dst = refs[:n], refs[n:2 * n]
        send_sems, recv_sems = refs[2 * n:]
        x, y, c = _coords()
        copies = []
        for j in range(n):
            s = src[j].at[pl.ds(4 * (1 - c), 4)] if j < nb else src[j]
            cp = pltpu.make_async_remote_copy(src_ref=s, dst_ref=dst[j], send_sem=send_sems.at[j],
                                              recv_sem=recv_sems.at[j], device_id=(x, y, 1 - c), device_id_type=MESH)
            cp.start()
            copies.append(cp)
        for cp in copies:
            cp.wait()

    return pl.pallas_call(
        body, name="pair_exchange",
        in_specs=[_ANY] * n, out_specs=[_ANY] * n,
        out_shape=[jax.ShapeDtypeStruct((4,) + b.shape[1:], b.dtype) for b in bigs]
        + [jax.ShapeDtypeStruct(s.shape, s.dtype) for s in smalls],
        scratch_shapes=[pltpu.SemaphoreType.DMA((n,)), pltpu.SemaphoreType.DMA((n,))],
    )(*bigs, *smalls)


def _chip_exchange(bigs, smalls):
    nb, n = len(bigs), len(bigs) + len(smalls)

    def body(*refs):
        src, dst = refs[:n], refs[n:2 * n]
        send_sems, recv_sems, local_sems = refs[2 * n:]
        x, y, c = _coords()
        q_me = 2 * x + y
        peers = [(1 - x, y, c), (x, 1 - y, c), (1 - x, 1 - y, c)]

        def piece(j, q):
            return src[j].at[q] if j < nb else src[j]

        started = []
        for j in range(n):
            for r, peer in enumerate(peers):
                cp = pltpu.make_async_remote_copy(
                    src_ref=piece(j, 2 * peer[0] + peer[1]), dst_ref=dst[j].at[q_me],
                    send_sem=send_sems.at[j, r], recv_sem=recv_sems.at[j, r], device_id=peer, device_id_type=MESH)
                cp.start()
                started.append(cp)
        local = []
        for j in range(n):
            lc = pltpu.make_async_copy(piece(j, q_me), dst[j].at[q_me], local_sems.at[j])
            lc.start()
            local.append(lc)
        for j in range(n):
            for r, peer in enumerate(peers):
                q_peer = 2 * peer[0] + peer[1]
                pltpu.make_async_remote_copy(
                    src_ref=piece(j, q_peer), dst_ref=dst[j].at[q_peer],
                    send_sem=send_sems.at[j, r], recv_sem=recv_sems.at[j, r], device_id=peer,
                    device_id_type=MESH).wait_recv()
        for cp in started:
            cp.wait_send()
        for lc in local:
            lc.wait()

    return pl.pallas_call(
        body, name="chip_exchange",
        in_specs=[_ANY] * n, out_specs=[_ANY] * n,
        out_shape=[jax.ShapeDtypeStruct(b.shape, b.dtype) for b in bigs]
        + [jax.ShapeDtypeStruct((N_CHIPS,) + s.shape, s.dtype) for s in smalls],
        scratch_shapes=[pltpu.SemaphoreType.DMA((n, 3)), pltpu.SemaphoreType.DMA((n, 3)),
                        pltpu.SemaphoreType.DMA((n,))],
    )(*bigs, *smalls)


def _pair_sum_big(name, c_arr, mine, theirs):
    _, rows, cols = theirs.shape
    tr = min(rows, 256)

    def body(c_ref, a_ref, b_ref, o_ref):
        del c_ref
        o_ref[...] = (a_ref[...].astype(F32) + b_ref[...].astype(F32)).astype(BF16)

    return pl.pallas_call(
        body, name=name,
        grid_spec=pltpu.PrefetchScalarGridSpec(
            num_scalar_prefetch=1, grid=(N_CHIPS, rows // tr),
            in_specs=[pl.BlockSpec((None, tr, cols), lambda q, r, c_ref: (4 * c_ref[0] + q, r, 0)),
                      pl.BlockSpec((None, tr, cols), lambda q, r, c_ref: (q, r, 0))],
            out_specs=pl.BlockSpec((None, tr, cols), lambda q, r, c_ref: (q, r, 0))),
        out_shape=jax.ShapeDtypeStruct(theirs.shape, BF16),
        compiler_params=pltpu.CompilerParams(dimension_semantics=("arbitrary", "arbitrary")),
    )(c_arr, mine, theirs)


def _pair_sum_small(mine, theirs):
    n = len(mine)

    def body(*refs):
        for j in range(n):
            refs[2 * n + j][...] = refs[j][...] + refs[n + j][...]

    return pl.pallas_call(
        body, name="pair_sum_small",
        out_shape=[jax.ShapeDtypeStruct(m.shape, m.dtype) for m in mine],
    )(*mine, *theirs)


def _adamw_math(w, g, m, v):
    m = ADAM_B1 * m + (1.0 - ADAM_B1) * g
    v = ADAM_B2 * v + (1.0 - ADAM_B2) * (g * g)
    m_hat = m / (1.0 - ADAM_B1 ** ADAM_STEP)
    v_hat = v / (1.0 - ADAM_B2 ** ADAM_STEP)
    delta = -ADAM_LR * (m_hat / (jnp.sqrt(v_hat) + ADAM_EPS) + ADAM_WD * w)
    return delta, m, v


def _adamw_big(name, chip_sums, w, m, v):
    rows, cols = w.shape
    tr = min(rows, 256)

    def body(t_ref, w_ref, m_ref, v_ref, g_out, d_out, m_out, v_out):
        g = t_ref[0].astype(F32)
        for q in range(1, N_CHIPS):
            g = g + t_ref[q].astype(F32)
        d, mn, vn = _adamw_math(w_ref[...], g, m_ref[...], v_ref[...])
        g_out[...] = g
        d_out[...] = d
        m_out[...] = mn
        v_out[...] = vn

    blk = pl.BlockSpec((tr, cols), lambda r: (r, 0))
    return pl.pallas_call(
        body, name=name, grid=(rows // tr,),
        in_specs=[pl.BlockSpec((N_CHIPS, tr, cols), lambda r: (0, r, 0)), blk, blk, blk],
        out_specs=[blk] * 4,
        out_shape=[jax.ShapeDtypeStruct((rows, cols), F32)] * 4,
        compiler_params=pltpu.CompilerParams(dimension_semantics=("arbitrary",)),
    )(chip_sums, w, m, v)


SMALL_ROWS = ("loss", "norm1_pre_g", "pool_scale", "sgu_ln_g", "sgu_ln_b", "norm1_post_g", "norm2_pre_g",
              "norm2_post_g")


def _adamw_small(u_rows, u_bin, u_ws, u_bs, params):
    n = len(params)

    def body(*refs):
        urow_ref, ubin_ref, uws_ref, ubs_ref = refs[:4]
        wmv = refs[4:4 + 3 * n]
        loss_ref = refs[4 + 3 * n]
        outs = refs[5 + 3 * n:]

        def total(ref, idx):
            g = ref[(0,) + idx]
            for q in range(1, N_CHIPS):
                g = g + ref[(q,) + idx]
            return g

        loss_ref[...] = total(urow_ref, (slice(0, 1), slice(None)))
        for p in range(n):
            if p < 7:
                g = total(urow_ref, (slice(p + 1, p + 2), slice(None)))
            else:
                g = total((ubin_ref, uws_ref, ubs_ref)[p - 7], (slice(None), slice(None)))
            d, mn, vn = _adamw_math(wmv[3 * p][...], g, wmv[3 * p + 1][...], wmv[3 * p + 2][...])
            outs[4 * p][...] = g
            outs[4 * p + 1][...] = d
            outs[4 * p + 2][...] = mn
            outs[4 * p + 3][...] = vn

    flat = [a for p in params for a in p]
    out_shape = [jax.ShapeDtypeStruct((1, D_MODEL), F32)]
    for w, _, _ in params:
        out_shape += [jax.ShapeDtypeStruct(w.shape, F32)] * 4
    return pl.pallas_call(body, name="adamw_small", out_shape=out_shape)(u_rows, u_bin, u_ws, u_bs, *flat)


def _slots_of_rows(full):
    owners = [_owner_of_slot(s) for s in range(N_DEV)]
    parts = full.reshape(N_GROUPS, N_DEV, PG_SHARD, GROUP)
    return jnp.stack([parts[:, o] for o in owners]).reshape(N_DEV, N_GROUPS * PG_SHARD, GROUP).astype(BF16)


def kernel(x, norm1_pre_g, w_in, b_in, w_pool, pool_scale, sgu_ln_g, sgu_ln_b, w_spatial, b_spatial, w_sgu_proj, w_out, norm1_post_g, norm2_pre_g, w_ff1, w_ff2, norm2_post_g, loss_target, m_norm1_pre_g, m_w_in, m_b_in, m_w_pool, m_pool_scale, m_sgu_ln_g, m_sgu_ln_b, m_w_spatial, m_b_spatial, m_w_sgu_proj, m_w_out, m_norm1_post_g, m_norm2_pre_g, m_w_ff1, m_w_ff2, m_norm2_post_g, v_norm1_pre_g, v_w_in, v_b_in, v_w_pool, v_pool_scale, v_sgu_ln_g, v_sgu_ln_b, v_w_spatial, v_b_spatial, v_w_sgu_proj, v_w_out, v_norm1_post_g, v_norm2_pre_g, v_w_ff1, v_w_ff2, v_norm2_post_g):
    t_len = x.shape[1]
    row = lambda a: a.reshape(1, -1)
    x2 = x.reshape(t_len, D_MODEL)
    tgt2 = loss_target.reshape(t_len, D_MODEL)
    pg2 = lambda a: a.reshape(N_GROUPS * PG_SHARD, GROUP)

    g_in, g_pool, g_proj, g_out, g_ff1, g_ff2 = _all_gather(
        [w_in.astype(BF16), pg2(w_pool).astype(BF16), pg2(w_sgu_proj).astype(BF16), w_out.astype(BF16),
         w_ff1.astype(BF16), w_ff2.astype(BF16)])
    regroup = lambda g: g.reshape(N_DEV, N_GROUPS, PG_SHARD, GROUP).transpose(1, 0, 2, 3).reshape(N_GROUPS, GROUP, GROUP)
    wpool_f, wproj_f = regroup(g_pool), regroup(g_proj)
    wout_f = g_out.reshape(D_MODEL, D_MODEL)
    w2_f = g_ff2.reshape(D_FF, D_MODEL)
    bsp_t = b_spatial.T

    z, xn_b, y, h1 = _fwd_mix(x2, row(norm1_pre_g), g_in, row(b_in), wpool_f, row(pool_scale), row(sgu_ln_g),
                              row(sgu_ln_b), w_spatial, bsp_t, wproj_f, wout_f, row(norm1_post_g))
    hn_b, f_b, df1_b, df2_b, dh1, dg2post, dg2pre, loss_p = _mlp(h1, tgt2, row(norm2_pre_g), row(norm2_post_g),
                                                               g_ff1, w2_f)
    p_ff2 = _wgrad("wgrad_ff2", f_b, df2_b, FF_SHARD, True)
    p_ff1 = _wgrad("wgrad_ff1", hn_b, df1_b, FF_SHARD, False)
    (dx, dz_b, mg_b, dy_b, dwpool, dwproj, dws, dbsp_t, dg1post, dps, dlng, dlnb, dg1pre, dbin) = _bwd_mix(
        dh1, x2, y, z, row(norm1_pre_g), g_in, wpool_f, row(pool_scale), row(sgu_ln_g), row(sgu_ln_b), w_spatial,
        bsp_t, wproj_f, wout_f, row(norm1_post_g))
    p_out = _wgrad("wgrad_out", mg_b, dy_b, OUT_SHARD, True)
    p_in = _wgrad("wgrad_in", xn_b, dz_b, IN_SHARD, False)
    p_pool, p_proj = _slots_of_rows(dwpool), _slots_of_rows(dwproj)

    rows = jnp.concatenate([jnp.broadcast_to(loss_p[:, 0:1], (1, D_MODEL)), dg1pre, dps, dlng, dlnb, dg1post, dg2pre,
                            dg2post], axis=0)
    bigs = [p_in, p_pool, p_proj, p_out, p_ff1, p_ff2]
    smalls = [rows, dbin, dws.reshape(N_HEADS * SGU_BLOCK, SGU_BLOCK), dbsp_t.T]
    got = _pair_exchange(bigs, smalls)
    c_arr = lax.axis_index("c").astype(jnp.int32).reshape(1)
    names = ("in", "pool", "proj", "out", "ff1", "ff2")
    chip_bigs = [_pair_sum_big("pair_sum_" + nm, c_arr, b, r) for nm, b, r in zip(names, bigs, got[:6])]
    chip_smalls = _pair_sum_small(smalls, got[6:])
    summed = _chip_exchange(chip_bigs, chip_smalls)

    big_params = [(w_in, m_w_in, v_w_in), (pg2(w_pool), pg2(m_w_pool), pg2(v_w_pool)),
                  (pg2(w_sgu_proj), pg2(m_w_sgu_proj), pg2(v_w_sgu_proj)), (w_out, m_w_out, v_w_out),
                  (w_ff1, m_w_ff1, v_w_ff1), (w_ff2, m_w_ff2, v_w_ff2)]
    big = {nm: _adamw_big("adamw_" + nm, t, *p) for nm, t, p in zip(names, summed[:6], big_params)}
    ws2 = lambda a: a.reshape(N_HEADS * SGU_BLOCK, SGU_BLOCK)
    small_params = [(row(norm1_pre_g), row(m_norm1_pre_g), row(v_norm1_pre_g)),
                    (row(pool_scale), row(m_pool_scale), row(v_pool_scale)),
                    (row(sgu_ln_g), row(m_sgu_ln_g), row(v_sgu_ln_g)),
                    (row(sgu_ln_b), row(m_sgu_ln_b), row(v_sgu_ln_b)),
                    (row(norm1_post_g), row(m_norm1_post_g), row(v_norm1_post_g)),
                    (row(norm2_pre_g), row(m_norm2_pre_g), row(v_norm2_pre_g)),
                    (row(norm2_post_g), row(m_norm2_post_g), row(v_norm2_post_g)),
                    (row(b_in), row(m_b_in), row(v_b_in)),
                    (ws2(w_spatial), ws2(m_w_spatial), ws2(v_w_spatial)),
                    (b_spatial, m_b_spatial, v_b_spatial)]
    small_out = _adamw_small(*summed[6:], small_params)
    loss = small_out[0][0, 0]
    small_names = SMALL_ROWS[1:] + ("b_in", "w_spatial", "b_spatial")
    small = {nm: small_out[1 + 4 * p:5 + 4 * p] for p, nm in enumerate(small_names)}

    shapes = {"norm1_pre_g": norm1_pre_g.shape, "w_in": w_in.shape, "b_in": b_in.shape, "w_pool": w_pool.shape,
              "pool_scale": pool_scale.shape, "sgu_ln_g": sgu_ln_g.shape, "sgu_ln_b": sgu_ln_b.shape,
              "w_spatial": w_spatial.shape, "b_spatial": b_spatial.shape, "w_sgu_proj": w_sgu_proj.shape,
              "w_out": w_out.shape, "norm1_post_g": norm1_post_g.shape, "norm2_pre_g": norm2_pre_g.shape,
              "w_ff1": w_ff1.shape, "w_ff2": w_ff2.shape, "norm2_post_g": norm2_post_g.shape}
    source = {"w_in": big["in"], "w_pool": big["pool"], "w_sgu_proj": big["proj"], "w_out": big["out"],
              "w_ff1": big["ff1"], "w_ff2": big["ff2"], **small}
    order = list(shapes)
    outs = [loss, dx.reshape(x.shape)]
    for kind in range(4):
        outs += [source[nm][kind].reshape(shapes[nm]) for nm in order]
    return tuple(outs)
```

```python
import functools
import math

import jax
import jax.numpy as jnp
from jax import lax
from jax.experimental import pallas as pl
from jax.experimental.pallas import tpu as pltpu

F32, BF16 = jnp.float32, jnp.bfloat16
MESH = pl.DeviceIdType.MESH

D_MODEL = 1024
D_IN = 5120
D_FF = 4096
N_DEV = 8
N_CHIPS = 4
WINDOWS = (2, 4, 8, 16)
N_GROUPS = 4
GROUP = 256
HALO = 16
SGU_BLOCK = 128
N_HEADS = 4
HEAD = 256
CHUNK = 64
EPS = 1e-6
IN_SHARD = D_IN // N_DEV
FF_SHARD = D_FF // N_DEV
OUT_SHARD = D_MODEL // N_DEV
PG_SHARD = GROUP // N_DEV

ADAM_LR, ADAM_B1, ADAM_B2, ADAM_EPS, ADAM_WD, ADAM_STEP = 0.001, 0.9, 0.999, 1e-08, 0.01, 10

VMEM_LIMIT = 56 * 1024 * 1024
TM = 256
TM_BWD = 256
TM_IN = 512
GELU_C0 = math.sqrt(2.0 / math.pi)
GELU_C1 = 0.044715


def _dot(a, b):
    return jnp.dot(a, b, preferred_element_type=F32)


def _dot_nt(a, b):
    return lax.dot_general(a, b, (((1,), (1,)), ((), ())), preferred_element_type=F32)


def _dot_tn(a, b):
    return lax.dot_general(a, b, (((0,), (0,)), ((), ())), preferred_element_type=F32)


def _gelu(x):
    t = jnp.tanh(GELU_C0 * (x + GELU_C1 * (x * x * x)))
    return 0.5 * x * (1.0 + t), t


def _gelu_grad(x, t):
    return 0.5 * (1.0 + t) + 0.5 * x * (1.0 - t * t) * (GELU_C0 * (1.0 + 3.0 * GELU_C1 * (x * x)))


def _sigmoid(x):
    return 1.0 / (1.0 + jnp.exp(-x))


def _mean(x):
    return jnp.mean(x, axis=-1, keepdims=True)


def _colsum(x):
    return jnp.sum(x, axis=0, keepdims=True)


def _const_spec(shape):
    nd = len(shape)
    return pl.BlockSpec(shape, lambda *_: (0,) * nd, pipeline_mode=pl.Buffered(1))


def _acc_spec(shape):
    nd = len(shape)
    return pl.BlockSpec(shape, lambda *_: (0,) * nd)


def _masked_ws(ws_ref):
    ri = lax.broadcasted_iota(jnp.int32, (SGU_BLOCK, SGU_BLOCK), 0) // CHUNK
    ci = lax.broadcasted_iota(jnp.int32, (SGU_BLOCK, SGU_BLOCK), 1) // CHUNK
    return [jnp.where(ri >= ci, ws_ref[h], 0.0).astype(BF16) for h in range(N_HEADS)]


def _pool_fwd(pbuf, tile_idx, tm):
    pos = lax.broadcasted_iota(jnp.int32, (tm, 1), 0) + tile_idx * tm + 1
    pooled = []
    for g, w in enumerate(WINDOWS):
        e = pbuf[:, g * GROUP:(g + 1) * GROUP]
        s, sh = e, 1
        while sh < w:
            s = s + pltpu.roll(s, sh, 0)
            sh *= 2
        inv = 1.0 / jnp.minimum(pos, w).astype(F32)
        pooled.append(s[HALO:] * inv - e[HALO:])
    return pooled


def _sgu_fwd(zu, zv, lng, lnb, wsm, bsp_ref, tm):
    u, tu = _gelu(zu)
    gv, tv = _gelu(zv)
    xc = gv - _mean(gv)
    rln = lax.rsqrt(_mean(xc * xc) + EPS)
    xhat = xc * rln
    vb = (xhat * lng + lnb).astype(BF16)
    sv_heads = []
    for h in range(N_HEADS):
        rows = []
        for n in range(tm // SGU_BLOCK):
            blk = vb[n * SGU_BLOCK:(n + 1) * SGU_BLOCK, h * HEAD:(h + 1) * HEAD]
            rows.append(_dot(wsm[h], blk) + bsp_ref[:, h:h + 1])
        sv_heads.append(jnp.concatenate(rows, axis=0))
    return u, tu, tv, xhat, rln, vb, sv_heads


def _fwd_mix(after, x, g1pre, win_g, b_in, wpool, pool_scale, lng, lnb, ws, bsp_t, wproj, wout, g1post):
    t_len = x.shape[0]
    tm = TM
    nt = t_len // tm

    def body(after_ref, x_ref, g1_ref, win_ref, bin_ref, wpool_ref, ps_ref, lng_ref, lnb_ref, ws_ref, bsp_ref,
             wproj_ref, wout_ref, g1post_ref, z_ref, xn_ref, y_ref, h1_ref, pbuf):
        del after_ref
        i = pl.program_id(0)
        xv = x_ref[...]
        r1 = lax.rsqrt(_mean(xv * xv) + EPS)
        xnb = (xv * r1 * g1_ref[...]).astype(BF16)
        xn_ref[...] = xnb
        for k in range(N_DEV):
            cols = slice(k * IN_SHARD, (k + 1) * IN_SHARD)
            z_ref[:, cols] = _dot(xnb, win_ref[k]) + bin_ref[:, cols]

        @pl.when(i == 0)
        def _():
            pbuf[0:HALO, :] = jnp.zeros((HALO, D_MODEL), F32)

        pbuf[HALO:, :] = z_ref[:, 0:D_MODEL]
        pooled = _pool_fwd(pbuf, i, tm)
        pbuf[0:HALO, :] = pbuf[tm:tm + HALO, :]
        a = jnp.concatenate([_dot(pooled[g].astype(BF16), wpool_ref[g]) for g in range(N_GROUPS)], axis=1)
        a = a * ps_ref[...]

        wsm = _masked_ws(ws_ref)
        u, _, _, _, _, _, sv_heads = _sgu_fwd(z_ref[:, D_MODEL:2 * D_MODEL], z_ref[:, 2 * D_MODEL:3 * D_MODEL],
                                              lng_ref[...], lnb_ref[...], wsm, bsp_ref, tm)
        bbr = jnp.concatenate(
            [_dot((u[:, h * HEAD:(h + 1) * HEAD] * sv_heads[h]).astype(BF16), wproj_ref[h]) for h in range(N_HEADS)],
            axis=1)
        merged = _sigmoid(z_ref[:, 3 * D_MODEL:4 * D_MODEL]) * a + _sigmoid(z_ref[:, 4 * D_MODEL:5 * D_MODEL]) * bbr
        yv = _dot(merged.astype(BF16), wout_ref[...])
        y_ref[...] = yv
        ry = lax.rsqrt(_mean(yv * yv) + EPS)
        h1_ref[...] = xv + yv * ry * g1post_ref[...]

    tok = lambda w: pl.BlockSpec((tm, w), lambda i: (i, 0))
    return pl.pallas_call(
        body, name="fwd_mix", grid=(nt,),
        in_specs=[_ANY, tok(D_MODEL), _const_spec((1, D_MODEL)), _const_spec((N_DEV, D_MODEL, IN_SHARD)),
                  _const_spec((1, D_IN)), _const_spec((N_GROUPS, GROUP, GROUP)), _const_spec((1, D_MODEL)),
                  _const_spec((1, D_MODEL)), _const_spec((1, D_MODEL)),
                  _const_spec((N_HEADS, SGU_BLOCK, SGU_BLOCK)), _const_spec((SGU_BLOCK, N_HEADS)),
                  _const_spec((N_HEADS, HEAD, HEAD)), _const_spec((D_MODEL, D_MODEL)), _const_spec((1, D_MODEL))],
        out_specs=[tok(D_IN), tok(D_MODEL), tok(D_MODEL), tok(D_MODEL)],
        out_shape=[jax.ShapeDtypeStruct((t_len, D_IN), F32), jax.ShapeDtypeStruct((t_len, D_MODEL), BF16),
                   jax.ShapeDtypeStruct((t_len, D_MODEL), F32), jax.ShapeDtypeStruct((t_len, D_MODEL), F32)],
        scratch_shapes=[pltpu.VMEM((tm + HALO, D_MODEL), F32)],
        compiler_params=pltpu.CompilerParams(dimension_semantics=("arbitrary",), vmem_limit_bytes=VMEM_LIMIT),
    )(after, x, g1pre, win_g, b_in, wpool, pool_scale, lng, lnb, ws, bsp_t, wproj, wout, g1post)


def _mlp(h1, target, g2pre, g2post, w1_g, w2):
    t_len = h1.shape[0]
    tm = TM
    nt = t_len // tm

    def body(h1_ref, tgt_ref, g2pre_ref, g2post_ref, w1_ref, w2_ref,
             hn_ref, f_ref, df1_ref, df2_ref, dh1_ref, dg2post_ref, dg2pre_ref, loss_ref, f1_scr):
        i = pl.program_id(0)

        @pl.when(i == 0)
        def _():
            dg2post_ref[...] = jnp.zeros_like(dg2post_ref)
            dg2pre_ref[...] = jnp.zeros_like(dg2pre_ref)
            loss_ref[...] = jnp.zeros_like(loss_ref)

        h = h1_ref[...]
        r2 = lax.rsqrt(_mean(h * h) + EPS)
        nh = h * r2
        hnb = (nh * g2pre_ref[...]).astype(BF16)
        hn_ref[...] = hnb
        for k in range(N_DEV):
            f1_scr[:, k * FF_SHARD:(k + 1) * FF_SHARD] = _dot(hnb, w1_ref[k])
        r = jnp.maximum(f1_scr[...], 0.0)
        fb = (r * r).astype(BF16)
        f_ref[...] = fb
        f2 = _dot(fb, w2_ref[...])
        rf = lax.rsqrt(_mean(f2 * f2) + EPS)
        nf = f2 * rf
        diff = h + nf * g2post_ref[...] - tgt_ref[...]
        loss_ref[...] += (0.5 / D_MODEL) * jnp.sum(diff * diff)
        dout = diff * (1.0 / D_MODEL)
        dg2post_ref[...] += _colsum(dout * nf)
        dn = dout * g2post_ref[...]
        df2b = (rf * (dn - nf * _mean(dn * nf))).astype(BF16)
        df2_ref[...] = df2b
        df = _dot_nt(df2b, w2_ref[...])
        df1b = (df * (2.0 * jnp.maximum(f1_scr[...], 0.0))).astype(BF16)
        df1_ref[...] = df1b
        dhn = _dot_nt(df1b[:, 0:FF_SHARD], w1_ref[0])
        for k in range(1, N_DEV):
            dhn = dhn + _dot_nt(df1b[:, k * FF_SHARD:(k + 1) * FF_SHARD], w1_ref[k])
        dg2pre_ref[...] += _colsum(dhn * nh)
        dnh = dhn * g2pre_ref[...]
        dh1_ref[...] = dout + r2 * (dnh - nh * _mean(dnh * nh))

    tok = lambda w: pl.BlockSpec((tm, w), lambda i: (i, 0))
    return pl.pallas_call(
        body, name="mlp_fwd_bwd", grid=(nt,),
        in_specs=[tok(D_MODEL), tok(D_MODEL), _const_spec((1, D_MODEL)), _const_spec((1, D_MODEL)),
                  _const_spec((N_DEV, D_MODEL, FF_SHARD)), _const_spec((D_FF, D_MODEL))],
        out_specs=[tok(D_MODEL), tok(D_FF), tok(D_FF), tok(D_MODEL), tok(D_MODEL),
                   _acc_spec((1, D_MODEL)), _acc_spec((1, D_MODEL)), _acc_spec((1, 128))],
        out_shape=[jax.ShapeDtypeStruct((t_len, D_MODEL), BF16), jax.ShapeDtypeStruct((t_len, D_FF), BF16),
                   jax.ShapeDtypeStruct((t_len, D_FF), BF16), jax.ShapeDtypeStruct((t_len, D_MODEL), BF16),
                   jax.ShapeDtypeStruct((t_len, D_MODEL), F32), jax.ShapeDtypeStruct((1, D_MODEL), F32),
                   jax.ShapeDtypeStruct((1, D_MODEL), F32), jax.ShapeDtypeStruct((1, 128), F32)],
        scratch_shapes=[pltpu.VMEM((tm, D_FF), F32)],
        compiler_params=pltpu.CompilerParams(dimension_semantics=("arbitrary",), vmem_limit_bytes=VMEM_LIMIT),
    )(h1, target, g2pre, g2post, w1_g, w2)


def _bwd_mix(after, dh1, y, z, wpool, pool_scale, lng, lnb, ws, bsp_t, wproj, wout, g1post):
    t_len = y.shape[0]
    tm = TM_BWD
    nt = t_len // tm
    nblk = tm // SGU_BLOCK

    def body(after_ref, dh1_ref, y_ref, z_ref, zh_ref, wpool_ref, ps_ref, lng_ref, lnb_ref, ws_ref,
             bsp_ref, wproj_ref, wout_ref, g1post_ref,
             dz_ref, mg_ref, dy_ref, dwpool_ref, dwproj_ref, dws_ref, dbsp_ref, dg1post_ref, dps_ref,
             dlng_ref, dlnb_ref, dbin_ref, pbuf, qbuf):
        del after_ref
        i = pl.program_id(0)
        ti = nt - 1 - i

        @pl.when(i == 0)
        def _():
            for ref in (dwpool_ref, dwproj_ref, dws_ref, dbsp_ref, dg1post_ref, dps_ref, dlng_ref, dlnb_ref,
                        dbin_ref):
                ref[...] = jnp.zeros_like(ref)
            qbuf[tm:tm + HALO, :] = jnp.zeros((HALO, D_MODEL), F32)

        pbuf[0:HALO, :] = jnp.where(ti > 0, zh_ref[...], 0.0)
        pbuf[HALO:, :] = z_ref[:, 0:D_MODEL]
        pooled = _pool_fwd(pbuf, ti, tm)
        pooled_b = [p.astype(BF16) for p in pooled]
        a_raw = jnp.concatenate([_dot(pooled_b[g], wpool_ref[g]) for g in range(N_GROUPS)], axis=1)
        wsm = _masked_ws(ws_ref)
        zu = z_ref[:, D_MODEL:2 * D_MODEL]
        zv = z_ref[:, 2 * D_MODEL:3 * D_MODEL]
        u, tu, tv, xhat, rln, vb, sv_heads = _sgu_fwd(zu, zv, lng_ref[...], lnb_ref[...], wsm, bsp_ref, tm)
        gated_b = [(u[:, h * HEAD:(h + 1) * HEAD] * sv_heads[h]).astype(BF16) for h in range(N_HEADS)]
        bbr = jnp.concatenate([_dot(gated_b[h], wproj_ref[h]) for h in range(N_HEADS)], axis=1)
        sa = _sigmoid(z_ref[:, 3 * D_MODEL:4 * D_MODEL])
        sb = _sigmoid(z_ref[:, 4 * D_MODEL:5 * D_MODEL])
        a = a_raw * ps_ref[...]
        mg_ref[...] = (sa * a + sb * bbr).astype(BF16)

        dh = dh1_ref[...]
        yv = y_ref[...]
        ry = lax.rsqrt(_mean(yv * yv) + EPS)
        ny = yv * ry
        dg1post_ref[...] += _colsum(dh * ny)
        dn = dh * g1post_ref[...]
        dyb = (ry * (dn - ny * _mean(dn * ny))).astype(BF16)
        dy_ref[...] = dyb
        dmg = _dot_nt(dyb, wout_ref[...])

        da = dmg * sa
        dbbr = dmg * sb
        dzga = dmg * a * sa * (1.0 - sa)
        dzgb = dmg * bbr * sb * (1.0 - sb)
        dz_ref[:, 3 * D_MODEL:4 * D_MODEL] = dzga.astype(BF16)
        dz_ref[:, 4 * D_MODEL:5 * D_MODEL] = dzgb.astype(BF16)
        dbin_ref[:, 3 * D_MODEL:4 * D_MODEL] += _colsum(dzga)
        dbin_ref[:, 4 * D_MODEL:5 * D_MODEL] += _colsum(dzgb)

        dps_ref[...] += _colsum(da * a_raw)
        da_raw_b = (da * ps_ref[...]).astype(BF16)
        pos = lax.broadcasted_iota(jnp.int32, (tm, 1), 0) + ti * tm + 1
        dpooled = []
        for g, w in enumerate(WINDOWS):
            cols = slice(g * GROUP, (g + 1) * GROUP)
            dwpool_ref[g] += _dot_tn(pooled_b[g], da_raw_b[:, cols])
            dp = _dot_nt(da_raw_b[:, cols], wpool_ref[g])
            dpooled.append(dp)
            qbuf[0:tm, cols] = dp * (1.0 / jnp.minimum(pos, w).astype(F32))
        n_ext = tm + HALO
        dzp = []
        for g, w in enumerate(WINDOWS):
            e = qbuf[:, g * GROUP:(g + 1) * GROUP]
            s, sh = e, 1
            while sh < w:
                s = s + pltpu.roll(s, n_ext - sh, 0)
                sh *= 2
            dzp.append(s[0:tm] - dpooled[g])
        qbuf[tm:tm + HALO, :] = qbuf[0:HALO, :]
        dzp = jnp.concatenate(dzp, axis=1)
        dz_ref[:, 0:D_MODEL] = dzp.astype(BF16)
        dbin_ref[:, 0:D_MODEL] += _colsum(dzp)

        dv_heads = []
        du_heads = []
        for h in range(N_HEADS):
            cols = slice(h * HEAD, (h + 1) * HEAD)
            dbbr_b = dbbr[:, cols].astype(BF16)
            dwproj_ref[h] += _dot_tn(gated_b[h], dbbr_b)
            dgated = _dot_nt(dbbr_b, wproj_ref[h])
            du_heads.append(dgated * sv_heads[h])
            dsv = dgated * u[:, cols]
            dsv_b = dsv.astype(BF16)
            rows = []
            for n in range(nblk):
                blk = slice(n * SGU_BLOCK, (n + 1) * SGU_BLOCK)
                rows.append(_dot_tn(wsm[h], dsv_b[blk]))
                dws_ref[h] += _dot_nt(dsv_b[blk], vb[blk, cols])
                dbsp_ref[:, h:h + 1] += jnp.sum(dsv[blk], axis=1, keepdims=True)
            dv_heads.append(jnp.concatenate(rows, axis=0))
        dzu = jnp.concatenate(du_heads, axis=1) * _gelu_grad(zu, tu)
        dz_ref[:, D_MODEL:2 * D_MODEL] = dzu.astype(BF16)
        dbin_ref[:, D_MODEL:2 * D_MODEL] += _colsum(dzu)
        dv = jnp.concatenate(dv_heads, axis=1)
        dlng_ref[...] += _colsum(dv * xhat)
        dlnb_ref[...] += _colsum(dv)
        dxh = dv * lng_ref[...]
        dgv = rln * (dxh - _mean(dxh) - xhat * _mean(dxh * xhat))
        dzv = dgv * _gelu_grad(zv, tv)
        dz_ref[:, 2 * D_MODEL:3 * D_MODEL] = dzv.astype(BF16)
        dbin_ref[:, 2 * D_MODEL:3 * D_MODEL] += _colsum(dzv)

        @pl.when(i == nt - 1)
        def _():
            ri = lax.broadcasted_iota(jnp.int32, (SGU_BLOCK, SGU_BLOCK), 0) // CHUNK
            ci = lax.broadcasted_iota(jnp.int32, (SGU_BLOCK, SGU_BLOCK), 1) // CHUNK
            for h in range(N_HEADS):
                dws_ref[h] = jnp.where(ri >= ci, dws_ref[h], 0.0)

    tok = lambda w: pl.BlockSpec((tm, w), lambda i: (nt - 1 - i, 0))
    halo = pl.BlockSpec((HALO, D_MODEL), lambda i: (jnp.maximum((nt - 1 - i) * (tm // HALO) - 1, 0), 0))
    return pl.pallas_call(
        body, name="bwd_mix", grid=(nt,),
        in_specs=[_ANY, tok(D_MODEL), tok(D_MODEL), tok(D_IN), halo, _const_spec((N_GROUPS, GROUP, GROUP)),
                  _const_spec((1, D_MODEL)), _const_spec((1, D_MODEL)), _const_spec((1, D_MODEL)),
                  _const_spec((N_HEADS, SGU_BLOCK, SGU_BLOCK)), _const_spec((SGU_BLOCK, N_HEADS)),
                  _const_spec((N_HEADS, HEAD, HEAD)), _const_spec((D_MODEL, D_MODEL)), _const_spec((1, D_MODEL))],
        out_specs=[tok(D_IN), tok(D_MODEL), tok(D_MODEL),
                   _acc_spec((N_GROUPS, GROUP, GROUP)), _acc_spec((N_HEADS, HEAD, HEAD)),
                   _acc_spec((N_HEADS, SGU_BLOCK, SGU_BLOCK)), _acc_spec((SGU_BLOCK, N_HEADS)),
                   _acc_spec((1, D_MODEL)), _acc_spec((1, D_MODEL)), _acc_spec((1, D_MODEL)), _acc_spec((1, D_MODEL)),
                   _acc_spec((1, D_IN))],
        out_shape=[jax.ShapeDtypeStruct((t_len, D_IN), BF16),
                   jax.ShapeDtypeStruct((t_len, D_MODEL), BF16), jax.ShapeDtypeStruct((t_len, D_MODEL), BF16),
                   jax.ShapeDtypeStruct((N_GROUPS, GROUP, GROUP), F32), jax.ShapeDtypeStruct((N_HEADS, HEAD, HEAD), F32),
                   jax.ShapeDtypeStruct((N_HEADS, SGU_BLOCK, SGU_BLOCK), F32),
                   jax.ShapeDtypeStruct((SGU_BLOCK, N_HEADS), F32),
                   jax.ShapeDtypeStruct((1, D_MODEL), F32), jax.ShapeDtypeStruct((1, D_MODEL), F32),
                   jax.ShapeDtypeStruct((1, D_MODEL), F32), jax.ShapeDtypeStruct((1, D_MODEL), F32),
                   jax.ShapeDtypeStruct((1, D_IN), F32)],
        scratch_shapes=[pltpu.VMEM((tm + HALO, D_MODEL), F32), pltpu.VMEM((tm + HALO, D_MODEL), F32)],
        compiler_params=pltpu.CompilerParams(dimension_semantics=("arbitrary",), vmem_limit_bytes=VMEM_LIMIT),
    )(after, dh1, y, z, z, wpool, pool_scale, lng, lnb, ws, bsp_t, wproj, wout, g1post)


def _bwd_in(dz_b, x, dh1, g1pre, win_g):
    t_len = x.shape[0]
    tm = TM_IN
    nt = t_len // tm

    def body(dz_ref, x_ref, dh1_ref, g1_ref, win_ref, dx_ref, dg1pre_ref):
        @pl.when(pl.program_id(0) == 0)
        def _():
            dg1pre_ref[...] = jnp.zeros_like(dg1pre_ref)

        dxn = _dot_nt(dz_ref[:, 0:IN_SHARD], win_ref[0])
        for k in range(1, N_DEV):
            dxn = dxn + _dot_nt(dz_ref[:, k * IN_SHARD:(k + 1) * IN_SHARD], win_ref[k])
        xv = x_ref[...]
        r1 = lax.rsqrt(_mean(xv * xv) + EPS)
        nx = xv * r1
        dg1pre_ref[...] += _colsum(dxn * nx)
        dnx = dxn * g1_ref[...]
        dx_ref[...] = r1 * (dnx - nx * _mean(dnx * nx)) + dh1_ref[...]

    tok = lambda w: pl.BlockSpec((tm, w), lambda i: (i, 0))
    return pl.pallas_call(
        body, name="bwd_in", grid=(nt,),
        in_specs=[tok(D_IN), tok(D_MODEL), tok(D_MODEL), _const_spec((1, D_MODEL)),
                  _const_spec((N_DEV, D_MODEL, IN_SHARD))],
        out_specs=[tok(D_MODEL), _acc_spec((1, D_MODEL))],
        out_shape=[jax.ShapeDtypeStruct((t_len, D_MODEL), F32), jax.ShapeDtypeStruct((1, D_MODEL), F32)],
        compiler_params=pltpu.CompilerParams(dimension_semantics=("arbitrary",), vmem_limit_bytes=VMEM_LIMIT),
    )(dz_b, x, dh1, g1pre, win_g)


def _owner_of_slot(s):
    return 4 * ((s // 2) % 2) + 2 * (s % 2) + s // 4


def _wgrad(name, a, b, shard, a_sliced):
    t_len = a.shape[0]
    am = shard if a_sliced else a.shape[1]
    bn = b.shape[1] if a_sliced else shard

    def body(a_ref, b_ref, o_ref):
        o_ref[...] = _dot_tn(a_ref[...], b_ref[...]).astype(BF16)

    sliced = lambda w: pl.BlockSpec((t_len, w), lambda s: (0, _owner_of_slot(s)))
    whole = lambda w: pl.BlockSpec((t_len, w), lambda s: (0, 0), pipeline_mode=pl.Buffered(1))
    return pl.pallas_call(
        body, name=name, grid=(N_DEV,),
        in_specs=[sliced(am) if a_sliced else whole(am), whole(bn) if a_sliced else sliced(bn)],
        out_specs=pl.BlockSpec((None, am, bn), lambda s: (s, 0, 0)),
        out_shape=jax.ShapeDtypeStruct((N_DEV, am, bn), BF16),
        compiler_params=pltpu.CompilerParams(dimension_semantics=("arbitrary",), vmem_limit_bytes=VMEM_LIMIT),
    )(a, b)


def _coords():
    return lax.axis_index("x"), lax.axis_index("y"), lax.axis_index("c")


_ANY = pl.BlockSpec(memory_space=pl.ANY)


def _all_gather(shards):
    n = len(shards)

    def body(*refs):
        src, dst = refs[:n], refs[n:2 * n]
        send_sems, recv_sems, local_sems = refs[2 * n:]
        x, y, c = _coords()
        me, sibling = (x, y, c), (x, y, 1 - c)
        chips = [(1 - x, y), (x, 1 - y), (1 - x, 1 - y)]

        def slot(j, dev):
            return dst[j].at[4 * dev[0] + 2 * dev[1] + dev[2]]

        def copy(j, k, block, to, from_shard=False):
            return pltpu.make_async_remote_copy(
                src_ref=src[j] if from_shard else slot(j, block), dst_ref=slot(j, block),
                send_sem=send_sems.at[j, k], recv_sem=recv_sems.at[j, k], device_id=to, device_id_type=MESH)

        started = []
        local = []
        for j in range(n):
            for k, to in enumerate([(*chip, c) for chip in chips]):
                cp = copy(j, 1 + k, me, to, from_shard=True)
                cp.start()
                started.append(cp)
        for j in range(n):
            cp = copy(j, 0, me, sibling, from_shard=True)
            cp.start()
            started.append(cp)
            lc = pltpu.make_async_copy(src[j], slot(j, me), local_sems.at[j])
            lc.start()
            local.append(lc)
        for j in range(n):
            for k, chip in enumerate(chips):
                copy(j, 1 + k, (*chip, c), me).wait_recv()
                cp = copy(j, 4 + k, (*chip, c), sibling)
                cp.start()
                started.append(cp)
        for j in range(n):
            copy(j, 0, sibling, me).wait_recv()
            for k, chip in enumerate(chips):
                copy(j, 4 + k, (*chip, 1 - c), me).wait_recv()
        for cp in started:
            cp.wait_send()
        for lc in local:
            lc.wait()

    return pl.pallas_call(
        body, name="all_gather_weights",
        in_specs=[_ANY] * n, out_specs=[_ANY] * n,
        out_shape=[jax.ShapeDtypeStruct((N_DEV,) + s.shape, s.dtype) for s in shards],
        scratch_shapes=[pltpu.SemaphoreType.DMA((n, 7)), pltpu.SemaphoreType.DMA((n, 7)),
                        pltpu.SemaphoreType.DMA((n,))],
    )(*shards)


def _pair_exchange(name, bigs, smalls):
    nb, n = len(bigs), len(bigs) + len(smalls)

    def body(*refs):
        src, dst = refs[:n], refs[n:2 * n]
        send_sems, recv_sems = refs[2 * n:]
        x, y, c = _coords()
        copies = []
        for j in range(n):
            s = src[j].at[pl.ds(4 * (1 - c), 4)] if j < nb else src[j]
            cp = pltpu.make_async_remote_copy(src_ref=s, dst_ref=dst[j], send_sem=send_sems.at[j],
                                              recv_sem=recv_sems.at[j], device_id=(x, y, 1 - c), device_id_type=MESH)
            cp.start()
            copies.append(cp)
        for cp in copies:
            cp.wait()

    return pl.pallas_call(
        body, name=name,
        in_specs=[_ANY] * n, out_specs=[_ANY] * n,
        out_shape=[jax.ShapeDtypeStruct((4,) + b.shape[1:], b.dtype) for b in bigs]
        + [jax.ShapeDtypeStruct(s.shape, s.dtype) for s in smalls],
        scratch_shapes=[pltpu.SemaphoreType.DMA((n,)), pltpu.SemaphoreType.DMA((n,))],
    )(*bigs, *smalls)


def _chip_exchange(bigs, smalls):
    nb, n = len(bigs), len(bigs) + len(smalls)

    def body(*refs):
        src, dst = refs[:n], refs[n:2 * n]
        send_sems, recv_sems, local_sems = refs[2 * n:]
        x, y, c = _coords()
        q_me = 2 * x + y
        peers = [(1 - x, y, c), (x, 1 - y, c), (1 - x, 1 - y, c)]

        def piece(j, q):
            return src[j].at[q] if j < nb else src[j]

        started = []
        for j in range(n):
            for r, peer in enumerate(peers):
                cp = pltpu.make_async_remote_copy(
                    src_ref=piece(j, 2 * peer[0] + peer[1]), dst_ref=dst[j].at[q_me],
                    send_sem=send_sems.at[j, r], recv_sem=recv_sems.at[j, r], device_id=peer, device_id_type=MESH)
                cp.start()
                started.append(cp)
        local = []
        for j in range(n):
            lc = pltpu.make_async_copy(piece(j, q_me), dst[j].at[q_me], local_sems.at[j])
            lc.start()
            local.append(lc)
        for j in range(n):
            for r, peer in enumerate(peers):
                q_peer = 2 * peer[0] + peer[1]
                pltpu.make_async_remote_copy(
                    src_ref=piece(j, q_peer), dst_ref=dst[j].at[q_peer],
                    send_sem=send_sems.at[j, r], recv_sem=recv_sems.at[j, r], device_id=peer,
                    device_id_type=MESH).wait_recv()
        for cp in started:
            cp.wait_send()
        for lc in local:
            lc.wait()

    return pl.pallas_call(
        body, name="chip_exchange",
        in_specs=[_ANY] * n, out_specs=[_ANY] * n,
        out_shape=[jax.ShapeDtypeStruct(b.shape, b.dtype) for b in bigs]
        + [jax.ShapeDtypeStruct((N_CHIPS,) + s.shape, s.dtype) for s in smalls],
        scratch_shapes=[pltpu.SemaphoreType.DMA((n, 3)), pltpu.SemaphoreType.DMA((n, 3)),
                        pltpu.SemaphoreType.DMA((n,))],
    )(*bigs, *smalls)


_HBM = pl.BlockSpec(memory_space=pltpu.HBM)
_SEM = pl.BlockSpec(memory_space=pltpu.SEMAPHORE)
_VMEM = pl.BlockSpec(memory_space=pltpu.VMEM)
_EFFECT = pltpu.SideEffectType.DATAFLOW_SIDE_EFFECTING
_TOKEN = jax.ShapeDtypeStruct((8, 128), F32)


def _in_hbm(a):
    return pltpu.with_memory_space_constraint(a, pltpu.HBM)


def _split_call(name, body, n_sems_out, arrays, sems_in=(), after=None):
    na, ns = len(arrays), len(sems_in)
    has_after = after is not None

    def kernel_body(*refs):
        arr = refs[:na]
        s_in = refs[na:na + ns]
        outs = refs[na + ns + has_after:]
        body(arr, s_in, outs[:n_sems_out])
        outs[-1][...] = jnp.zeros((8, 128), F32)

    out_shape = ([pltpu.SemaphoreType.DMA(())] * n_sems_out + [pltpu.HBM(a.shape, a.dtype) for a in arrays] + [_TOKEN])
    res = pl.pallas_call(
        kernel_body, name=name, out_shape=out_shape,
        in_specs=[_HBM] * na + [_SEM] * ns + [_ANY] * has_after,
        out_specs=[_SEM] * n_sems_out + [_HBM] * na + [_VMEM],
        input_output_aliases={i: n_sems_out + i for i in range(na)},
        compiler_params=pltpu.CompilerParams(has_side_effects=_EFFECT),
    )(*[_in_hbm(a) for a in arrays], *sems_in, *([after] if has_after else []))
    return list(res[:n_sems_out]), list(res[n_sems_out:n_sems_out + na]), res[-1]


def _wait_bytes_of(ref, send_sem, recv_sem, peer, send=True, recv=True):
    cp = pltpu.make_async_remote_copy(src_ref=ref, dst_ref=ref, send_sem=send_sem, recv_sem=recv_sem,
                                      device_id=peer, device_id_type=MESH)
    if send:
        cp.wait_send()
    if recv:
        cp.wait_recv()


def _gather_behind(shards, after, work):
    n = len(shards)
    x, y, c = _coords()
    me = 4 * x + 2 * y + c
    lands = [lax.dynamic_update_slice(lax.empty((N_DEV,) + s.shape, s.dtype), s[None], (me,) + (0,) * s.ndim)
             for s in shards]

    def start(arr, _, sems):
        x, y, c = _coords()
        me = 4 * x + 2 * y + c
        for j in range(n):
            for chip in [(1 - x, y), (x, 1 - y), (1 - x, 1 - y)]:
                pltpu.make_async_remote_copy(src_ref=arr[j], dst_ref=arr[n + j].at[me], send_sem=sems[j],
                                             recv_sem=sems[n + j], device_id=(*chip, c), device_id_type=MESH).start()
        for j in range(n):
            pltpu.make_async_remote_copy(src_ref=arr[j], dst_ref=arr[n + j].at[me], send_sem=sems[2 * n + j],
                                         recv_sem=sems[3 * n + j], device_id=(x, y, 1 - c),
                                         device_id_type=MESH).start()

    def middle(arr, s_in, sems):
        x, y, c = _coords()
        sibling = (x, y, 1 - c)
        for j in range(n):
            _wait_bytes_of(arr[n + j].at[pl.ds(0, 3)], s_in[j], s_in[n + j], sibling)
        for j in range(n):
            for chip in [(1 - x, y), (x, 1 - y), (1 - x, 1 - y)]:
                slot = arr[n + j].at[4 * chip[0] + 2 * chip[1] + c]
                pltpu.make_async_remote_copy(src_ref=slot, dst_ref=slot, send_sem=sems[j], recv_sem=sems[n + j],
                                             device_id=sibling, device_id_type=MESH).start()

    def finish(arr, s_in, _):
        x, y, c = _coords()
        sibling = (x, y, 1 - c)
        for j in range(n):
            _wait_bytes_of(arr[n + j].at[pl.ds(0, 1)], s_in[j], s_in[n + j], sibling)
            _wait_bytes_of(arr[n + j].at[pl.ds(0, 3)], s_in[2 * n + j], s_in[3 * n + j], sibling)

    sems, arrays, token = _split_call("gather_ff_start", start, 4 * n, list(shards) + lands, after=after)
    result = work(token)
    fwd_sems, arrays, token = _split_call("gather_ff_middle", middle, 2 * n, arrays, sems_in=sems[:2 * n],
                                          after=result[0])
    _, arrays, _ = _split_call("gather_ff_finish", finish, 0, arrays, sems_in=sems[2 * n:] + fwd_sems, after=token)
    return arrays[n:], result


def _chip_exchange_behind(chip_sums, work):
    n = len(chip_sums)
    x, y, _ = _coords()
    q_me = 2 * x + y
    lands = [lax.dynamic_update_slice(lax.empty(s.shape, s.dtype), lax.dynamic_slice_in_dim(s, q_me, 1),
                                      (q_me,) + (0,) * (s.ndim - 1)) for s in chip_sums]

    def start(arr, _, sems):
        x, y, c = _coords()
        q_me = 2 * x + y
        for j in range(n):
            for peer in [(1 - x, y, c), (x, 1 - y, c), (1 - x, 1 - y, c)]:
                pltpu.make_async_remote_copy(src_ref=arr[j].at[2 * peer[0] + peer[1]], dst_ref=arr[n + j].at[q_me],
                                             send_sem=sems[j], recv_sem=sems[n + j], device_id=peer,
                                             device_id_type=MESH).start()

    def finish(arr, s_in, _):
        x, y, c = _coords()
        for j in range(n):
            _wait_bytes_of(arr[n + j].at[pl.ds(0, 3)], s_in[j], s_in[n + j], (x, y, 1 - c))

    sems, arrays, token = _split_call("chip_exchange_ff_start", start, 2 * n, list(chip_sums) + lands)
    result = work(token)
    _, arrays, _ = _split_call("chip_exchange_ff_finish", finish, 0, arrays, sems_in=sems, after=result[0])
    return arrays[n:], result


def _pair_sum_big(name, c_arr, mine, theirs):
    _, rows, cols = theirs.shape
    tr = rows

    def body(c_ref, a_ref, b_ref, o_ref):
        del c_ref
        o_ref[...] = (a_ref[...].astype(F32) + b_ref[...].astype(F32)).astype(BF16)

    return pl.pallas_call(
        body, name=name,
        grid_spec=pltpu.PrefetchScalarGridSpec(
            num_scalar_prefetch=1, grid=(N_CHIPS, rows // tr),
            in_specs=[pl.BlockSpec((None, tr, cols), lambda q, r, c_ref: (4 * c_ref[0] + q, r, 0)),
                      pl.BlockSpec((None, tr, cols), lambda q, r, c_ref: (q, r, 0))],
            out_specs=pl.BlockSpec((None, tr, cols), lambda q, r, c_ref: (q, r, 0))),
        out_shape=jax.ShapeDtypeStruct(theirs.shape, BF16),
        compiler_params=pltpu.CompilerParams(dimension_semantics=("arbitrary", "arbitrary")),
    )(c_arr, mine, theirs)


def _pair_sum_small(mine, theirs):
    n = len(mine)

    def body(*refs):
        for j in range(n):
            refs[2 * n + j][...] = refs[j][...] + refs[n + j][...]

    return pl.pallas_call(
        body, name="pair_sum_small",
        out_shape=[jax.ShapeDtypeStruct(m.shape, m.dtype) for m in mine],
    )(*mine, *theirs)


def _adamw_math(w, g, m, v):
    m = ADAM_B1 * m + (1.0 - ADAM_B1) * g
    v = ADAM_B2 * v + (1.0 - ADAM_B2) * (g * g)
    m_hat = m / (1.0 - ADAM_B1 ** ADAM_STEP)
    v_hat = v / (1.0 - ADAM_B2 ** ADAM_STEP)
    delta = -ADAM_LR * (m_hat / (jnp.sqrt(v_hat) + ADAM_EPS) + ADAM_WD * w)
    return delta, m, v


def _adamw_big(name, chip_sums, w, m, v):
    rows, cols = w.shape
    tr = min(rows, 512)

    def body(t_ref, w_ref, m_ref, v_ref, g_out, d_out, m_out, v_out):
        g = t_ref[0].astype(F32)
        for q in range(1, N_CHIPS):
            g = g + t_ref[q].astype(F32)
        d, mn, vn = _adamw_math(w_ref[...], g, m_ref[...], v_ref[...])
        g_out[...] = g
        d_out[...] = d
        m_out[...] = mn
        v_out[...] = vn

    blk = pl.BlockSpec((tr, cols), lambda r: (r, 0))
    return pl.pallas_call(
        body, name=name, grid=(rows // tr,),
        in_specs=[pl.BlockSpec((N_CHIPS, tr, cols), lambda r: (0, r, 0)), blk, blk, blk],
        out_specs=[blk] * 4,
        out_shape=[jax.ShapeDtypeStruct((rows, cols), F32)] * 4,
        compiler_params=pltpu.CompilerParams(dimension_semantics=("arbitrary",), vmem_limit_bytes=VMEM_LIMIT),
    )(chip_sums, w, m, v)


SMALL_ROWS = ("loss", "norm1_pre_g", "pool_scale", "sgu_ln_g", "sgu_ln_b", "norm1_post_g", "norm2_pre_g",
              "norm2_post_g")


def _adamw_small(u_rows, u_bin, u_ws, u_bs, params):
    n = len(params)

    def body(*refs):
        urow_ref, ubin_ref, uws_ref, ubs_ref = refs[:4]
        wmv = refs[4:4 + 3 * n]
        loss_ref = refs[4 + 3 * n]
        outs = refs[5 + 3 * n:]

        def total(ref, idx):
            g = ref[(0,) + idx]
            for q in range(1, N_CHIPS):
                g = g + ref[(q,) + idx]
            return g

        loss_ref[...] = total(urow_ref, (slice(0, 1), slice(None)))
        for p in range(n):
            if p < 7:
                g = total(urow_ref, (slice(p + 1, p + 2), slice(None)))
            else:
                g = total((ubin_ref, uws_ref, ubs_ref)[p - 7], (slice(None), slice(None)))
            d, mn, vn = _adamw_math(wmv[3 * p][...], g, wmv[3 * p + 1][...], wmv[3 * p + 2][...])
            outs[4 * p][...] = g
            outs[4 * p + 1][...] = d
            outs[4 * p + 2][...] = mn
            outs[4 * p + 3][...] = vn

    flat = [a for p in params for a in p]
    out_shape = [jax.ShapeDtypeStruct((1, D_MODEL), F32)]
    for w, _, _ in params:
        out_shape += [jax.ShapeDtypeStruct(w.shape, F32)] * 4
    return pl.pallas_call(body, name="adamw_small", out_shape=out_shape)(u_rows, u_bin, u_ws, u_bs, *flat)


def _slots_of_rows(full):
    owners = [_owner_of_slot(s) for s in range(N_DEV)]
    parts = full.reshape(N_GROUPS, N_DEV, PG_SHARD, GROUP)
    return jnp.stack([parts[:, o] for o in owners]).reshape(N_DEV, N_GROUPS * PG_SHARD, GROUP).astype(BF16)


def kernel(x, norm1_pre_g, w_in, b_in, w_pool, pool_scale, sgu_ln_g, sgu_ln_b, w_spatial, b_spatial, w_sgu_proj, w_out, norm1_post_g, norm2_pre_g, w_ff1, w_ff2, norm2_post_g, loss_target, m_norm1_pre_g, m_w_in, m_b_in, m_w_pool, m_pool_scale, m_sgu_ln_g, m_sgu_ln_b, m_w_spatial, m_b_spatial, m_w_sgu_proj, m_w_out, m_norm1_post_g, m_norm2_pre_g, m_w_ff1, m_w_ff2, m_norm2_post_g, v_norm1_pre_g, v_w_in, v_b_in, v_w_pool, v_pool_scale, v_sgu_ln_g, v_sgu_ln_b, v_w_spatial, v_b_spatial, v_w_sgu_proj, v_w_out, v_norm1_post_g, v_norm2_pre_g, v_w_ff1, v_w_ff2, v_norm2_post_g):
    t_len = x.shape[1]
    row = lambda a: a.reshape(1, -1)
    x2 = x.reshape(t_len, D_MODEL)
    tgt2 = loss_target.reshape(t_len, D_MODEL)
    pg2 = lambda a: a.reshape(N_GROUPS * PG_SHARD, GROUP)

    g_in, g_pool, g_proj, g_out = _all_gather(
        [w_in.astype(BF16), pg2(w_pool).astype(BF16), pg2(w_sgu_proj).astype(BF16), w_out.astype(BF16)])
    regroup = lambda g: g.reshape(N_DEV, N_GROUPS, PG_SHARD, GROUP).transpose(1, 0, 2, 3).reshape(N_GROUPS, GROUP, GROUP)
    wpool_f, wproj_f = regroup(g_pool), regroup(g_proj)
    wout_f = g_out.reshape(D_MODEL, D_MODEL)
    bsp_t = b_spatial.T

    def forward(token):
        z, xn_b, y, h1 = _fwd_mix(token, x2, row(norm1_pre_g), g_in, row(b_in), wpool_f, row(pool_scale),
                                  row(sgu_ln_g), row(sgu_ln_b), w_spatial, bsp_t, wproj_f, wout_f, row(norm1_post_g))
        return h1, z, xn_b, y

    (g_ff1, g_ff2), (h1, z, xn_b, y) = _gather_behind([w_ff1.astype(BF16), w_ff2.astype(BF16)], g_in, forward)
    w2_f = g_ff2.reshape(D_FF, D_MODEL)
    hn_b, f_b, df1_b, df2_b, dh1, dg2post, dg2pre, loss_p = _mlp(h1, tgt2, row(norm2_pre_g), row(norm2_post_g),
                                                               g_ff1, w2_f)
    p_ff2 = _wgrad("wgrad_ff2", f_b, df2_b, FF_SHARD, True)
    p_ff1 = _wgrad("wgrad_ff1", hn_b, df1_b, FF_SHARD, False)
    c_arr = lax.axis_index("c").astype(jnp.int32).reshape(1)
    ff_parts = [p_ff1, p_ff2]
    got_ff = _pair_exchange("pair_exchange_ff", ff_parts, [])
    chip_ff = [_pair_sum_big("pair_sum_" + nm, c_arr, b, r) for nm, b, r in zip(("ff1", "ff2"), ff_parts, got_ff)]

    def backward(token):
        (dz_b, mg_b, dy_b, dwpool, dwproj, dws, dbsp_t, dg1post, dps, dlng, dlnb, dbin) = _bwd_mix(
            token, dh1, y, z, wpool_f, row(pool_scale), row(sgu_ln_g), row(sgu_ln_b), w_spatial, bsp_t, wproj_f,
            wout_f, row(norm1_post_g))
        dx, dg1pre = _bwd_in(dz_b, x2, dh1, row(norm1_pre_g), g_in)
        return dx, dg1pre, dz_b, mg_b, dy_b, dwpool, dwproj, dws, dbsp_t, dg1post, dps, dlng, dlnb, dbin

    summed_ff, (dx, dg1pre, dz_b, mg_b, dy_b, dwpool, dwproj, dws, dbsp_t, dg1post, dps, dlng, dlnb,
                dbin) = _chip_exchange_behind(chip_ff, backward)
    p_out = _wgrad("wgrad_out", mg_b, dy_b, OUT_SHARD, True)
    p_in = _wgrad("wgrad_in", xn_b, dz_b, IN_SHARD, False)
    p_pool, p_proj = _slots_of_rows(dwpool), _slots_of_rows(dwproj)

    rows = jnp.concatenate([jnp.broadcast_to(loss_p[:, 0:1], (1, D_MODEL)), dg1pre, dps, dlng, dlnb, dg1post, dg2pre,
                            dg2post], axis=0)
    bigs = [p_in, p_pool, p_proj, p_out]
    smalls = [rows, dbin, dws.reshape(N_HEADS * SGU_BLOCK, SGU_BLOCK), dbsp_t.T]
    got = _pair_exchange("pair_exchange_mix", bigs, smalls)
    names = ("in", "pool", "proj", "out")
    chip_bigs = [_pair_sum_big("pair_sum_" + nm, c_arr, b, r) for nm, b, r in zip(names, bigs, got[:4])]
    chip_smalls = _pair_sum_small(smalls, got[4:])
    summed = _chip_exchange(chip_bigs, chip_smalls)

    names = names + ("ff1", "ff2")
    chip_sums = list(summed[:4]) + list(summed_ff)
    big_params = [(w_in, m_w_in, v_w_in), (pg2(w_pool), pg2(m_w_pool), pg2(v_w_pool)),
                  (pg2(w_sgu_proj), pg2(m_w_sgu_proj), pg2(v_w_sgu_proj)), (w_out, m_w_out, v_w_out),
                  (w_ff1, m_w_ff1, v_w_ff1), (w_ff2, m_w_ff2, v_w_ff2)]
    big = {nm: _adamw_big("adamw_" + nm, t, *p) for nm, t, p in zip(names, chip_sums, big_params)}
    ws2 = lambda a: a.reshape(N_HEADS * SGU_BLOCK, SGU_BLOCK)
    small_params = [(row(norm1_pre_g), row(m_norm1_pre_g), row(v_norm1_pre_g)),
                    (row(pool_scale), row(m_pool_scale), row(v_pool_scale)),
                    (row(sgu_ln_g), row(m_sgu_ln_g), row(v_sgu_ln_g)),
                    (row(sgu_ln_b), row(m_sgu_ln_b), row(v_sgu_ln_b)),
                    (row(norm1_post_g), row(m_norm1_post_g), row(v_norm1_post_g)),
                    (row(norm2_pre_g), row(m_norm2_pre_g), row(v_norm2_pre_g)),
                    (row(norm2_post_g), row(m_norm2_post_g), row(v_norm2_post_g)),
                    (row(b_in), row(m_b_in), row(v_b_in)),
                    (ws2(w_spatial), ws2(m_w_spatial), ws2(v_w_spatial)),
                    (b_spatial, m_b_spatial, v_b_spatial)]
    small_out = _adamw_small(*summed[4:], small_params)
    loss = small_out[0][0, 0]
    small_names = SMALL_ROWS[1:] + ("b_in", "w_spatial", "b_spatial")
    small = {nm: small_out[1 + 4 * p:5 + 4 * p] for p, nm in enumerate(small_names)}

    shapes = {"norm1_pre_g": norm1_pre_g.shape, "w_in": w_in.shape, "b_in": b_in.shape, "w_pool": w_pool.shape,
              "pool_scale": pool_scale.shape, "sgu_ln_g": sgu_ln_g.shape, "sgu_ln_b": sgu_ln_b.shape,
              "w_spatial": w_spatial.shape, "b_spatial": b_spatial.shape, "w_sgu_proj": w_sgu_proj.shape,
              "w_out": w_out.shape, "norm1_post_g": norm1_post_g.shape, "norm2_pre_g": norm2_pre_g.shape,
              "w_ff1": w_ff1.shape, "w_ff2": w_ff2.shape, "norm2_post_g": norm2_post_g.shape}
    source = {"w_in": big["in"], "w_pool": big["pool"], "w_sgu_proj": big["proj"], "w_out": big["out"],
              "w_ff1": big["ff1"], "w_ff2": big["ff2"], **small}
    order = list(shapes)
    outs = [loss, dx.reshape(x.shape)]
    for kind in range(4):
        outs += [source[nm][kind].reshape(shapes[nm]) for nm in order]
    return tuple(outs)
```

```python
import functools
import math

import jax
import jax.numpy as jnp
from jax import lax
from jax.experimental import pallas as pl
from jax.experimental.pallas import tpu as pltpu

F32, BF16 = jnp.float32, jnp.bfloat16
MESH = pl.DeviceIdType.MESH

D_MODEL = 1024
D_IN = 5120
D_FF = 4096
N_DEV = 8
N_CHIPS = 4
WINDOWS = (2, 4, 8, 16)
N_GROUPS = 4
GROUP = 256
HALO = 16
SGU_BLOCK = 128
N_HEADS = 4
HEAD = 256
CHUNK = 64
EPS = 1e-6
IN_SHARD = D_IN // N_DEV
FF_SHARD = D_FF // N_DEV
OUT_SHARD = D_MODEL // N_DEV
PG_SHARD = GROUP // N_DEV

ADAM_LR, ADAM_B1, ADAM_B2, ADAM_EPS, ADAM_WD, ADAM_STEP = 0.001, 0.9, 0.999, 1e-08, 0.01, 10

VMEM_LIMIT = 56 * 1024 * 1024
TM = 256
TM_BWD = 256
TM_IN = 512
GELU_C0 = math.sqrt(2.0 / math.pi)
GELU_C1 = 0.044715


def _dot(a, b):
    return jnp.dot(a, b, preferred_element_type=F32)


def _dot_nt(a, b):
    return lax.dot_general(a, b, (((1,), (1,)), ((), ())), preferred_element_type=F32)


def _dot_tn(a, b):
    return lax.dot_general(a, b, (((0,), (0,)), ((), ())), preferred_element_type=F32)


def _gelu(x):
    t = jnp.tanh(GELU_C0 * (x + GELU_C1 * (x * x * x)))
    return 0.5 * x * (1.0 + t), t


def _gelu_grad(x, t):
    return 0.5 * (1.0 + t) + 0.5 * x * (1.0 - t * t) * (GELU_C0 * (1.0 + 3.0 * GELU_C1 * (x * x)))


def _sigmoid(x):
    return 1.0 / (1.0 + jnp.exp(-x))


def _mean(x):
    return jnp.mean(x, axis=-1, keepdims=True)


def _colsum(x):
    return jnp.sum(x, axis=0, keepdims=True)


def _const_spec(shape):
    nd = len(shape)
    return pl.BlockSpec(shape, lambda *_: (0,) * nd, pipeline_mode=pl.Buffered(1))


def _acc_spec(shape):
    nd = len(shape)
    return pl.BlockSpec(shape, lambda *_: (0,) * nd)


def _masked_ws(ws_ref):
    ri = lax.broadcasted_iota(jnp.int32, (SGU_BLOCK, SGU_BLOCK), 0) // CHUNK
    ci = lax.broadcasted_iota(jnp.int32, (SGU_BLOCK, SGU_BLOCK), 1) // CHUNK
    return [jnp.where(ri >= ci, ws_ref[h], 0.0).astype(BF16) for h in range(N_HEADS)]


def _pool_fwd(pbuf, tile_idx, tm):
    pos = lax.broadcasted_iota(jnp.int32, (tm, 1), 0) + tile_idx * tm + 1
    pooled = []
    for g, w in enumerate(WINDOWS):
        e = pbuf[:, g * GROUP:(g + 1) * GROUP]
        s, sh = e, 1
        while sh < w:
            s = s + pltpu.roll(s, sh, 0)
            sh *= 2
        inv = 1.0 / jnp.minimum(pos, w).astype(F32)
        pooled.append(s[HALO:] * inv - e[HALO:])
    return pooled


def _sgu_fwd(zu, zv, lng, lnb, wsm, bsp_ref, tm):
    u, tu = _gelu(zu)
    gv, tv = _gelu(zv)
    xc = gv - _mean(gv)
    rln = lax.rsqrt(_mean(xc * xc) + EPS)
    xhat = xc * rln
    vb = (xhat * lng + lnb).astype(BF16)
    sv_heads = []
    for h in range(N_HEADS):
        rows = []
        for n in range(tm // SGU_BLOCK):
            blk = vb[n * SGU_BLOCK:(n + 1) * SGU_BLOCK, h * HEAD:(h + 1) * HEAD]
            rows.append(_dot(wsm[h], blk) + bsp_ref[:, h:h + 1])
        sv_heads.append(jnp.concatenate(rows, axis=0))
    return u, tu, tv, xhat, rln, vb, sv_heads


def _fwd_mix(after, x, g1pre, win_g, b_in, wpool, pool_scale, lng, lnb, ws, bsp_t, wproj, wout, g1post):
    t_len = x.shape[0]
    tm = TM
    nt = t_len // tm

    def body(after_ref, x_ref, g1_ref, win_ref, bin_ref, wpool_ref, ps_ref, lng_ref, lnb_ref, ws_ref, bsp_ref,
             wproj_ref, wout_ref, g1post_ref, z_ref, xn_ref, y_ref, h1_ref, pbuf):
        del after_ref
        i = pl.program_id(0)
        xv = x_ref[...]
        r1 = lax.rsqrt(_mean(xv * xv) + EPS)
        xnb = (xv * r1 * g1_ref[...]).astype(BF16)
        xn_ref[...] = xnb
        for k in range(N_DEV):
            cols = slice(k * IN_SHARD, (k + 1) * IN_SHARD)
            z_ref[:, cols] = _dot(xnb, win_ref[k]) + bin_ref[:, cols]

        @pl.when(i == 0)
        def _():
            pbuf[0:HALO, :] = jnp.zeros((HALO, D_MODEL), F32)

        pbuf[HALO:, :] = z_ref[:, 0:D_MODEL]
        pooled = _pool_fwd(pbuf, i, tm)
        pbuf[0:HALO, :] = pbuf[tm:tm + HALO, :]
        a = jnp.concatenate([_dot(pooled[g].astype(BF16), wpool_ref[g]) for g in range(N_GROUPS)], axis=1)
        a = a * ps_ref[...]

        wsm = _masked_ws(ws_ref)
        u, _, _, _, _, _, sv_heads = _sgu_fwd(z_ref[:, D_MODEL:2 * D_MODEL], z_ref[:, 2 * D_MODEL:3 * D_MODEL],
                                              lng_ref[...], lnb_ref[...], wsm, bsp_ref, tm)
        bbr = jnp.concatenate(
            [_dot((u[:, h * HEAD:(h + 1) * HEAD] * sv_heads[h]).astype(BF16), wproj_ref[h]) for h in range(N_HEADS)],
            axis=1)
        merged = _sigmoid(z_ref[:, 3 * D_MODEL:4 * D_MODEL]) * a + _sigmoid(z_ref[:, 4 * D_MODEL:5 * D_MODEL]) * bbr
        yv = _dot(merged.astype(BF16), wout_ref[...])
        y_ref[...] = yv
        ry = lax.rsqrt(_mean(yv * yv) + EPS)
        h1_ref[...] = xv + yv * ry * g1post_ref[...]

    tok = lambda w: pl.BlockSpec((tm, w), lambda i: (i, 0))
    return pl.pallas_call(
        body, name="fwd_mix", grid=(nt,),
        in_specs=[_ANY, tok(D_MODEL), _const_spec((1, D_MODEL)), _const_spec((N_DEV, D_MODEL, IN_SHARD)),
                  _const_spec((1, D_IN)), _const_spec((N_GROUPS, GROUP, GROUP)), _const_spec((1, D_MODEL)),
                  _const_spec((1, D_MODEL)), _const_spec((1, D_MODEL)),
                  _const_spec((N_HEADS, SGU_BLOCK, SGU_BLOCK)), _const_spec((SGU_BLOCK, N_HEADS)),
                  _const_spec((N_HEADS, HEAD, HEAD)), _const_spec((D_MODEL, D_MODEL)), _const_spec((1, D_MODEL))],
        out_specs=[tok(D_IN), tok(D_MODEL), tok(D_MODEL), tok(D_MODEL)],
        out_shape=[jax.ShapeDtypeStruct((t_len, D_IN), F32), jax.ShapeDtypeStruct((t_len, D_MODEL), BF16),
                   jax.ShapeDtypeStruct((t_len, D_MODEL), F32), jax.ShapeDtypeStruct((t_len, D_MODEL), F32)],
        scratch_shapes=[pltpu.VMEM((tm + HALO, D_MODEL), F32)],
        compiler_params=pltpu.CompilerParams(dimension_semantics=("arbitrary",), vmem_limit_bytes=VMEM_LIMIT),
    )(after, x, g1pre, win_g, b_in, wpool, pool_scale, lng, lnb, ws, bsp_t, wproj, wout, g1post)


def _mlp(h1, target, g2pre, g2post, w1_g, w2):
    t_len = h1.shape[0]
    tm = TM
    nt = t_len // tm

    def body(h1_ref, tgt_ref, g2pre_ref, g2post_ref, w1_ref, w2_ref,
             hn_ref, f_ref, df1_ref, df2_ref, dh1_ref, dg2post_ref, dg2pre_ref, loss_ref, f1_scr):
        i = pl.program_id(0)

        @pl.when(i == 0)
        def _():
            dg2post_ref[...] = jnp.zeros_like(dg2post_ref)
            dg2pre_ref[...] = jnp.zeros_like(dg2pre_ref)
            loss_ref[...] = jnp.zeros_like(loss_ref)

        h = h1_ref[...]
        r2 = lax.rsqrt(_mean(h * h) + EPS)
        nh = h * r2
        hnb = (nh * g2pre_ref[...]).astype(BF16)
        hn_ref[...] = hnb
        for k in range(N_DEV):
            f1_scr[:, k * FF_SHARD:(k + 1) * FF_SHARD] = _dot(hnb, w1_ref[k])
        r = jnp.maximum(f1_scr[...], 0.0)
        fb = (r * r).astype(BF16)
        f_ref[...] = fb
        f2 = _dot(fb, w2_ref[...])
        rf = lax.rsqrt(_mean(f2 * f2) + EPS)
        nf = f2 * rf
        diff = h + nf * g2post_ref[...] - tgt_ref[...]
        loss_ref[...] += (0.5 / D_MODEL) * jnp.sum(diff * diff)
        dout = diff * (1.0 / D_MODEL)
        dg2post_ref[...] += _colsum(dout * nf)
        dn = dout * g2post_ref[...]
        df2b = (rf * (dn - nf * _mean(dn * nf))).astype(BF16)
        df2_ref[...] = df2b
        df = _dot_nt(df2b, w2_ref[...])
        df1b = (df * (2.0 * jnp.maximum(f1_scr[...], 0.0))).astype(BF16)
        df1_ref[...] = df1b
        dhn = _dot_nt(df1b[:, 0:FF_SHARD], w1_ref[0])
        for k in range(1, N_DEV):
            dhn = dhn + _dot_nt(df1b[:, k * FF_SHARD:(k + 1) * FF_SHARD], w1_ref[k])
        dg2pre_ref[...] += _colsum(dhn * nh)
        dnh = dhn * g2pre_ref[...]
        dh1_ref[...] = dout + r2 * (dnh - nh * _mean(dnh * nh))

    tok = lambda w: pl.BlockSpec((tm, w), lambda i: (i, 0))
    return pl.pallas_call(
        body, name="mlp_fwd_bwd", grid=(nt,),
        in_specs=[tok(D_MODEL), tok(D_MODEL), _const_spec((1, D_MODEL)), _const_spec((1, D_MODEL)),
                  _const_spec((N_DEV, D_MODEL, FF_SHARD)), _const_spec((D_FF, D_MODEL))],
        out_specs=[tok(D_MODEL), tok(D_FF), tok(D_FF), tok(D_MODEL), tok(D_MODEL),
                   _acc_spec((1, D_MODEL)), _acc_spec((1, D_MODEL)), _acc_spec((1, 128))],
        out_shape=[jax.ShapeDtypeStruct((t_len, D_MODEL), BF16), jax.ShapeDtypeStruct((t_len, D_FF), BF16),
                   jax.ShapeDtypeStruct((t_len, D_FF), BF16), jax.ShapeDtypeStruct((t_len, D_MODEL), BF16),
                   jax.ShapeDtypeStruct((t_len, D_MODEL), F32), jax.ShapeDtypeStruct((1, D_MODEL), F32),
                   jax.ShapeDtypeStruct((1, D_MODEL), F32), jax.ShapeDtypeStruct((1, 128), F32)],
        scratch_shapes=[pltpu.VMEM((tm, D_FF), F32)],
        compiler_params=pltpu.CompilerParams(dimension_semantics=("arbitrary",), vmem_limit_bytes=VMEM_LIMIT),
    )(h1, target, g2pre, g2post, w1_g, w2)


def _bwd_mix(after, dh1, y, z, wpool, pool_scale, lng, lnb, ws, bsp_t, wproj, wout, g1post):
    t_len = y.shape[0]
    tm = TM_BWD
    nt = t_len // tm
    nblk = tm // SGU_BLOCK

    def body(after_ref, dh1_ref, y_ref, z_ref, zh_ref, wpool_ref, ps_ref, lng_ref, lnb_ref, ws_ref,
             bsp_ref, wproj_ref, wout_ref, g1post_ref,
             dz_ref, mg_ref, dy_ref, dwpool_ref, dwproj_ref, dws_ref, dbsp_ref, dg1post_ref, dps_ref,
             dlng_ref, dlnb_ref, dbin_ref, pbuf, qbuf):
        del after_ref
        i = pl.program_id(0)
        ti = nt - 1 - i

        @pl.when(i == 0)
        def _():
            for ref in (dwpool_ref, dwproj_ref, dws_ref, dbsp_ref, dg1post_ref, dps_ref, dlng_ref, dlnb_ref,
                        dbin_ref):
                ref[...] = jnp.zeros_like(ref)
            qbuf[tm:tm + HALO, :] = jnp.zeros((HALO, D_MODEL), F32)

        pbuf[0:HALO, :] = jnp.where(ti > 0, zh_ref[...], 0.0)
        pbuf[HALO:, :] = z_ref[:, 0:D_MODEL]
        pooled = _pool_fwd(pbuf, ti, tm)
        pooled_b = [p.astype(BF16) for p in pooled]
        a_raw = jnp.concatenate([_dot(pooled_b[g], wpool_ref[g]) for g in range(N_GROUPS)], axis=1)
        wsm = _masked_ws(ws_ref)
        zu = z_ref[:, D_MODEL:2 * D_MODEL]
        zv = z_ref[:, 2 * D_MODEL:3 * D_MODEL]
        u, tu, tv, xhat, rln, vb, sv_heads = _sgu_fwd(zu, zv, lng_ref[...], lnb_ref[...], wsm, bsp_ref, tm)
        gated_b = [(u[:, h * HEAD:(h + 1) * HEAD] * sv_heads[h]).astype(BF16) for h in range(N_HEADS)]
        bbr = jnp.concatenate([_dot(gated_b[h], wproj_ref[h]) for h in range(N_HEADS)], axis=1)
        sa = _sigmoid(z_ref[:, 3 * D_MODEL:4 * D_MODEL])
        sb = _sigmoid(z_ref[:, 4 * D_MODEL:5 * D_MODEL])
        a = a_raw * ps_ref[...]
        mg_ref[...] = (sa * a + sb * bbr).astype(BF16)

        dh = dh1_ref[...]
        yv = y_ref[...]
        ry = lax.rsqrt(_mean(yv * yv) + EPS)
        ny = yv * ry
        dg1post_ref[...] += _colsum(dh * ny)
        dn = dh * g1post_ref[...]
        dyb = (ry * (dn - ny * _mean(dn * ny))).astype(BF16)
        dy_ref[...] = dyb
        dmg = _dot_nt(dyb, wout_ref[...])

        da = dmg * sa
        dbbr = dmg * sb
        dzga = dmg * a * sa * (1.0 - sa)
        dzgb = dmg * bbr * sb * (1.0 - sb)
        dz_ref[:, 3 * D_MODEL:4 * D_MODEL] = dzga.astype(BF16)
        dz_ref[:, 4 * D_MODEL:5 * D_MODEL] = dzgb.astype(BF16)
        dbin_ref[:, 3 * D_MODEL:4 * D_MODEL] += _colsum(dzga)
        dbin_ref[:, 4 * D_MODEL:5 * D_MODEL] += _colsum(dzgb)

        dps_ref[...] += _colsum(da * a_raw)
        da_raw_b = (da * ps_ref[...]).astype(BF16)
        pos = lax.broadcasted_iota(jnp.int32, (tm, 1), 0) + ti * tm + 1
        dpooled = []
        for g, w in enumerate(WINDOWS):
            cols = slice(g * GROUP, (g + 1) * GROUP)
            dwpool_ref[g] += _dot_tn(pooled_b[g], da_raw_b[:, cols])
            dp = _dot_nt(da_raw_b[:, cols], wpool_ref[g])
            dpooled.append(dp)
            qbuf[0:tm, cols] = dp * (1.0 / jnp.minimum(pos, w).astype(F32))
        n_ext = tm + HALO
        dzp = []
        for g, w in enumerate(WINDOWS):
            e = qbuf[:, g * GROUP:(g + 1) * GROUP]
            s, sh = e, 1
            while sh < w:
                s = s + pltpu.roll(s, n_ext - sh, 0)
                sh *= 2
            dzp.append(s[0:tm] - dpooled[g])
        qbuf[tm:tm + HALO, :] = qbuf[0:HALO, :]
        dzp = jnp.concatenate(dzp, axis=1)
        dz_ref[:, 0:D_MODEL] = dzp.astype(BF16)
        dbin_ref[:, 0:D_MODEL] += _colsum(dzp)

        dv_heads = []
        du_heads = []
        for h in range(N_HEADS):
            cols = slice(h * HEAD, (h + 1) * HEAD)
            dbbr_b = dbbr[:, cols].astype(BF16)
            dwproj_ref[h] += _dot_tn(gated_b[h], dbbr_b)
            dgated = _dot_nt(dbbr_b, wproj_ref[h])
            du_heads.append(dgated * sv_heads[h])
            dsv = dgated * u[:, cols]
            dsv_b = dsv.astype(BF16)
            rows = []
            for n in range(nblk):
                blk = slice(n * SGU_BLOCK, (n + 1) * SGU_BLOCK)
                rows.append(_dot_tn(wsm[h], dsv_b[blk]))
                dws_ref[h] += _dot_nt(dsv_b[blk], vb[blk, cols])
                dbsp_ref[:, h:h + 1] += jnp.sum(dsv[blk], axis=1, keepdims=True)
            dv_heads.append(jnp.concatenate(rows, axis=0))
        dzu = jnp.concatenate(du_heads, axis=1) * _gelu_grad(zu, tu)
        dz_ref[:, D_MODEL:2 * D_MODEL] = dzu.astype(BF16)
        dbin_ref[:, D_MODEL:2 * D_MODEL] += _colsum(dzu)
        dv = jnp.concatenate(dv_heads, axis=1)
        dlng_ref[...] += _colsum(dv * xhat)
        dlnb_ref[...] += _colsum(dv)
        dxh = dv * lng_ref[...]
        dgv = rln * (dxh - _mean(dxh) - xhat * _mean(dxh * xhat))
        dzv = dgv * _gelu_grad(zv, tv)
        dz_ref[:, 2 * D_MODEL:3 * D_MODEL] = dzv.astype(BF16)
        dbin_ref[:, 2 * D_MODEL:3 * D_MODEL] += _colsum(dzv)

        @pl.when(i == nt - 1)
        def _():
            ri = lax.broadcasted_iota(jnp.int32, (SGU_BLOCK, SGU_BLOCK), 0) // CHUNK
            ci = lax.broadcasted_iota(jnp.int32, (SGU_BLOCK, SGU_BLOCK), 1) // CHUNK
            for h in range(N_HEADS):
                dws_ref[h] = jnp.where(ri >= ci, dws_ref[h], 0.0)

    tok = lambda w: pl.BlockSpec((tm, w), lambda i: (nt - 1 - i, 0))
    halo = pl.BlockSpec((HALO, D_MODEL), lambda i: (jnp.maximum((nt - 1 - i) * (tm // HALO) - 1, 0), 0))
    return pl.pallas_call(
        body, name="bwd_mix", grid=(nt,),
        in_specs=[_ANY, tok(D_MODEL), tok(D_MODEL), tok(D_IN), halo, _const_spec((N_GROUPS, GROUP, GROUP)),
                  _const_spec((1, D_MODEL)), _const_spec((1, D_MODEL)), _const_spec((1, D_MODEL)),
                  _const_spec((N_HEADS, SGU_BLOCK, SGU_BLOCK)), _const_spec((SGU_BLOCK, N_HEADS)),
                  _const_spec((N_HEADS, HEAD, HEAD)), _const_spec((D_MODEL, D_MODEL)), _const_spec((1, D_MODEL))],
        out_specs=[tok(D_IN), tok(D_MODEL), tok(D_MODEL),
                   _acc_spec((N_GROUPS, GROUP, GROUP)), _acc_spec((N_HEADS, HEAD, HEAD)),
                   _acc_spec((N_HEADS, SGU_BLOCK, SGU_BLOCK)), _acc_spec((SGU_BLOCK, N_HEADS)),
                   _acc_spec((1, D_MODEL)), _acc_spec((1, D_MODEL)), _acc_spec((1, D_MODEL)), _acc_spec((1, D_MODEL)),
                   _acc_spec((1, D_IN))],
        out_shape=[jax.ShapeDtypeStruct((t_len, D_IN), BF16),
                   jax.ShapeDtypeStruct((t_len, D_MODEL), BF16), jax.ShapeDtypeStruct((t_len, D_MODEL), BF16),
                   jax.ShapeDtypeStruct((N_GROUPS, GROUP, GROUP), F32), jax.ShapeDtypeStruct((N_HEADS, HEAD, HEAD), F32),
                   jax.ShapeDtypeStruct((N_HEADS, SGU_BLOCK, SGU_BLOCK), F32),
                   jax.ShapeDtypeStruct((SGU_BLOCK, N_HEADS), F32),
                   jax.ShapeDtypeStruct((1, D_MODEL), F32), jax.ShapeDtypeStruct((1, D_MODEL), F32),
                   jax.ShapeDtypeStruct((1, D_MODEL), F32), jax.ShapeDtypeStruct((1, D_MODEL), F32),
                   jax.ShapeDtypeStruct((1, D_IN), F32)],
        scratch_shapes=[pltpu.VMEM((tm + HALO, D_MODEL), F32), pltpu.VMEM((tm + HALO, D_MODEL), F32)],
        compiler_params=pltpu.CompilerParams(dimension_semantics=("arbitrary",), vmem_limit_bytes=VMEM_LIMIT),
    )(after, dh1, y, z, z, wpool, pool_scale, lng, lnb, ws, bsp_t, wproj, wout, g1post)


def _bwd_in(after, dz_b, x, dh1, g1pre, win_g):
    t_len = x.shape[0]
    tm = TM_IN
    nt = t_len // tm

    def body(after_ref, dz_ref, x_ref, dh1_ref, g1_ref, win_ref, dx_ref, dg1pre_ref):
        del after_ref

        @pl.when(pl.program_id(0) == 0)
        def _():
            dg1pre_ref[...] = jnp.zeros_like(dg1pre_ref)

        dxn = _dot_nt(dz_ref[:, 0:IN_SHARD], win_ref[0])
        for k in range(1, N_DEV):
            dxn = dxn + _dot_nt(dz_ref[:, k * IN_SHARD:(k + 1) * IN_SHARD], win_ref[k])
        xv = x_ref[...]
        r1 = lax.rsqrt(_mean(xv * xv) + EPS)
        nx = xv * r1
        dg1pre_ref[...] += _colsum(dxn * nx)
        dnx = dxn * g1_ref[...]
        dx_ref[...] = r1 * (dnx - nx * _mean(dnx * nx)) + dh1_ref[...]

    tok = lambda w: pl.BlockSpec((tm, w), lambda i: (i, 0))
    return pl.pallas_call(
        body, name="bwd_in", grid=(nt,),
        in_specs=[_ANY, tok(D_IN), tok(D_MODEL), tok(D_MODEL), _const_spec((1, D_MODEL)),
                  _const_spec((N_DEV, D_MODEL, IN_SHARD))],
        out_specs=[tok(D_MODEL), _acc_spec((1, D_MODEL))],
        out_shape=[jax.ShapeDtypeStruct((t_len, D_MODEL), F32), jax.ShapeDtypeStruct((1, D_MODEL), F32)],
        compiler_params=pltpu.CompilerParams(dimension_semantics=("arbitrary",), vmem_limit_bytes=VMEM_LIMIT),
    )(after, dz_b, x, dh1, g1pre, win_g)


def _owner_of_slot(s):
    return 4 * ((s // 2) % 2) + 2 * (s % 2) + s // 4


def _wgrad(name, a, b, shard, a_sliced):
    t_len = a.shape[0]
    am = shard if a_sliced else a.shape[1]
    bn = b.shape[1] if a_sliced else shard

    def body(a_ref, b_ref, o_ref):
        o_ref[...] = _dot_tn(a_ref[...], b_ref[...]).astype(BF16)

    sliced = lambda w: pl.BlockSpec((t_len, w), lambda s: (0, _owner_of_slot(s)))
    whole = lambda w: pl.BlockSpec((t_len, w), lambda s: (0, 0), pipeline_mode=pl.Buffered(1))
    return pl.pallas_call(
        body, name=name, grid=(N_DEV,),
        in_specs=[sliced(am) if a_sliced else whole(am), whole(bn) if a_sliced else sliced(bn)],
        out_specs=pl.BlockSpec((None, am, bn), lambda s: (s, 0, 0)),
        out_shape=jax.ShapeDtypeStruct((N_DEV, am, bn), BF16),
        compiler_params=pltpu.CompilerParams(dimension_semantics=("arbitrary",), vmem_limit_bytes=VMEM_LIMIT),
    )(a, b)


def _coords():
    return lax.axis_index("x"), lax.axis_index("y"), lax.axis_index("c")


_ANY = pl.BlockSpec(memory_space=pl.ANY)


def _all_gather(shards):
    n = len(shards)

    def body(*refs):
        src, dst = refs[:n], refs[n:2 * n]
        send_sems, recv_sems, local_sems = refs[2 * n:]
        x, y, c = _coords()
        me, sibling = (x, y, c), (x, y, 1 - c)
        chips = [(1 - x, y), (x, 1 - y), (1 - x, 1 - y)]

        def slot(j, dev):
            return dst[j].at[4 * dev[0] + 2 * dev[1] + dev[2]]

        def copy(j, k, block, to, from_shard=False):
            return pltpu.make_async_remote_copy(
                src_ref=src[j] if from_shard else slot(j, block), dst_ref=slot(j, block),
                send_sem=send_sems.at[j, k], recv_sem=recv_sems.at[j, k], device_id=to, device_id_type=MESH)

        started = []
        local = []
        for j in range(n):
            for k, to in enumerate([(*chip, c) for chip in chips]):
                cp = copy(j, 1 + k, me, to, from_shard=True)
                cp.start()
                started.append(cp)
        for j in range(n):
            cp = copy(j, 0, me, sibling, from_shard=True)
            cp.start()
            started.append(cp)
            lc = pltpu.make_async_copy(src[j], slot(j, me), local_sems.at[j])
            lc.start()
            local.append(lc)
        for j in range(n):
            for k, chip in enumerate(chips):
                copy(j, 1 + k, (*chip, c), me).wait_recv()
                cp = copy(j, 4 + k, (*chip, c), sibling)
                cp.start()
                started.append(cp)
        for j in range(n):
            copy(j, 0, sibling, me).wait_recv()
            for k, chip in enumerate(chips):
                copy(j, 4 + k, (*chip, 1 - c), me).wait_recv()
        for cp in started:
            cp.wait_send()
        for lc in local:
            lc.wait()

    return pl.pallas_call(
        body, name="all_gather_weights",
        in_specs=[_ANY] * n, out_specs=[_ANY] * n,
        out_shape=[jax.ShapeDtypeStruct((N_DEV,) + s.shape, s.dtype) for s in shards],
        scratch_shapes=[pltpu.SemaphoreType.DMA((n, 7)), pltpu.SemaphoreType.DMA((n, 7)),
                        pltpu.SemaphoreType.DMA((n,))],
    )(*shards)


def _pair_exchange(name, bigs, smalls):
    nb, n = len(bigs), len(bigs) + len(smalls)

    def body(*refs):
        src, dst = refs[:n], refs[n:2 * n]
        send_sems, recv_sems = refs[2 * n:]
        x, y, c = _coords()
        copies = []
        for j in range(n):
            s = src[j].at[pl.ds(4 * (1 - c), 4)] if j < nb else src[j]
            cp = pltpu.make_async_remote_copy(src_ref=s, dst_ref=dst[j], send_sem=send_sems.at[j],
                                              recv_sem=recv_sems.at[j], device_id=(x, y, 1 - c), device_id_type=MESH)
            cp.start()
            copies.append(cp)
        for cp in copies:
            cp.wait()

    return pl.pallas_call(
        body, name=name,
        in_specs=[_ANY] * n, out_specs=[_ANY] * n,
        out_shape=[jax.ShapeDtypeStruct((4,) + b.shape[1:], b.dtype) for b in bigs]
        + [jax.ShapeDtypeStruct(s.shape, s.dtype) for s in smalls],
        scratch_shapes=[pltpu.SemaphoreType.DMA((n,)), pltpu.SemaphoreType.DMA((n,))],
    )(*bigs, *smalls)


def _chip_exchange(bigs, smalls):
    nb, n = len(bigs), len(bigs) + len(smalls)

    def body(*refs):
        src, dst = refs[:n], refs[n:2 * n]
        send_sems, recv_sems, local_sems = refs[2 * n:]
        x, y, c = _coords()
        q_me = 2 * x + y
        peers = [(1 - x, y, c), (x, 1 - y, c), (1 - x, 1 - y, c)]

        def piece(j, q):
            return src[j].at[q] if j < nb else src[j]

        started = []
        for j in range(n):
            for r, peer in enumerate(peers):
                cp = pltpu.make_async_remote_copy(
                    src_ref=piece(j, 2 * peer[0] + peer[1]), dst_ref=dst[j].at[q_me],
                    send_sem=send_sems.at[j, r], recv_sem=recv_sems.at[j, r], device_id=peer, device_id_type=MESH)
                cp.start()
                started.append(cp)
        local = []
        for j in range(n):
            lc = pltpu.make_async_copy(piece(j, q_me), dst[j].at[q_me], local_sems.at[j])
            lc.start()
            local.append(lc)
        for j in range(n):
            for r, peer in enumerate(peers):
                q_peer = 2 * peer[0] + peer[1]
                pltpu.make_async_remote_copy(
                    src_ref=piece(j, q_peer), dst_ref=dst[j].at[q_peer],
                    send_sem=send_sems.at[j, r], recv_sem=recv_sems.at[j, r], device_id=peer,
                    device_id_type=MESH).wait_recv()
        for cp in started:
            cp.wait_send()
        for lc in local:
            lc.wait()

    return pl.pallas_call(
        body, name="chip_exchange",
        in_specs=[_ANY] * n, out_specs=[_ANY] * n,
        out_shape=[jax.ShapeDtypeStruct(b.shape, b.dtype) for b in bigs]
        + [jax.ShapeDtypeStruct((N_CHIPS,) + s.shape, s.dtype) for s in smalls],
        scratch_shapes=[pltpu.SemaphoreType.DMA((n, 3)), pltpu.SemaphoreType.DMA((n, 3)),
                        pltpu.SemaphoreType.DMA((n,))],
    )(*bigs, *smalls)


_HBM = pl.BlockSpec(memory_space=pltpu.HBM)
_SEM = pl.BlockSpec(memory_space=pltpu.SEMAPHORE)
_VMEM = pl.BlockSpec(memory_space=pltpu.VMEM)
_EFFECT = pltpu.SideEffectType.DATAFLOW_SIDE_EFFECTING
_TOKEN = jax.ShapeDtypeStruct((8, 128), F32)


def _in_hbm(a):
    return pltpu.with_memory_space_constraint(a, pltpu.HBM)


def _split_call(name, body, n_sems_out, arrays, sems_in=(), after=None):
    na, ns = len(arrays), len(sems_in)
    has_after = after is not None

    def kernel_body(*refs):
        arr = refs[:na]
        s_in = refs[na:na + ns]
        outs = refs[na + ns + has_after:]
        body(arr, s_in, outs[:n_sems_out])
        outs[-1][...] = jnp.zeros((8, 128), F32)

    out_shape = ([pltpu.SemaphoreType.DMA(())] * n_sems_out + [pltpu.HBM(a.shape, a.dtype) for a in arrays] + [_TOKEN])
    res = pl.pallas_call(
        kernel_body, name=name, out_shape=out_shape,
        in_specs=[_HBM] * na + [_SEM] * ns + [_ANY] * has_after,
        out_specs=[_SEM] * n_sems_out + [_HBM] * na + [_VMEM],
        input_output_aliases={i: n_sems_out + i for i in range(na)},
        compiler_params=pltpu.CompilerParams(has_side_effects=_EFFECT),
    )(*[_in_hbm(a) for a in arrays], *sems_in, *([after] if has_after else []))
    return list(res[:n_sems_out]), list(res[n_sems_out:n_sems_out + na]), res[-1]


def _wait_bytes_of(ref, send_sem, recv_sem, peer, send=True, recv=True):
    cp = pltpu.make_async_remote_copy(src_ref=ref, dst_ref=ref, send_sem=send_sem, recv_sem=recv_sem,
                                      device_id=peer, device_id_type=MESH)
    if send:
        cp.wait_send()
    if recv:
        cp.wait_recv()


def _gather_behind(shards, after, work):
    n = len(shards)
    x, y, c = _coords()
    me = 4 * x + 2 * y + c
    lands = [lax.dynamic_update_slice(lax.empty((N_DEV,) + s.shape, s.dtype), s[None], (me,) + (0,) * s.ndim)
             for s in shards]

    def start(arr, _, sems):
        x, y, c = _coords()
        me = 4 * x + 2 * y + c
        for j in range(n):
            for chip in [(1 - x, y), (x, 1 - y), (1 - x, 1 - y)]:
                pltpu.make_async_remote_copy(src_ref=arr[j], dst_ref=arr[n + j].at[me], send_sem=sems[j],
                                             recv_sem=sems[n + j], device_id=(*chip, c), device_id_type=MESH).start()
        for j in range(n):
            pltpu.make_async_remote_copy(src_ref=arr[j], dst_ref=arr[n + j].at[me], send_sem=sems[2 * n + j],
                                         recv_sem=sems[3 * n + j], device_id=(x, y, 1 - c),
                                         device_id_type=MESH).start()

    def middle(arr, s_in, sems):
        x, y, c = _coords()
        sibling = (x, y, 1 - c)
        for j in range(n):
            _wait_bytes_of(arr[n + j].at[pl.ds(0, 3)], s_in[j], s_in[n + j], sibling)
        for j in range(n):
            for chip in [(1 - x, y), (x, 1 - y), (1 - x, 1 - y)]:
                slot = arr[n + j].at[4 * chip[0] + 2 * chip[1] + c]
                pltpu.make_async_remote_copy(src_ref=slot, dst_ref=slot, send_sem=sems[j], recv_sem=sems[n + j],
                                             device_id=sibling, device_id_type=MESH).start()

    def finish(arr, s_in, _):
        x, y, c = _coords()
        sibling = (x, y, 1 - c)
        for j in range(n):
            _wait_bytes_of(arr[n + j].at[pl.ds(0, 1)], s_in[j], s_in[n + j], sibling)
            _wait_bytes_of(arr[n + j].at[pl.ds(0, 3)], s_in[2 * n + j], s_in[3 * n + j], sibling)

    sems, arrays, token = _split_call("gather_ff_start", start, 4 * n, list(shards) + lands, after=after)
    result = work(token)
    fwd_sems, arrays, token = _split_call("gather_ff_middle", middle, 2 * n, arrays, sems_in=sems[:2 * n],
                                          after=result[0])
    _, arrays, _ = _split_call("gather_ff_finish", finish, 0, arrays, sems_in=sems[2 * n:] + fwd_sems, after=token)
    return arrays[n:], result


def _chip_exchange_behind(tag, bigs, smalls, work):
    nb, n = len(bigs), len(bigs) + len(smalls)
    x, y, _ = _coords()
    q_me = 2 * x + y
    lands = [lax.dynamic_update_slice(lax.empty(s.shape, s.dtype), lax.dynamic_slice_in_dim(s, q_me, 1),
                                      (q_me,) + (0,) * (s.ndim - 1)) for s in bigs]
    lands += [lax.dynamic_update_slice(lax.empty((N_CHIPS,) + s.shape, s.dtype), s[None], (q_me,) + (0,) * s.ndim)
              for s in smalls]

    def start(arr, _, sems):
        x, y, c = _coords()
        q_me = 2 * x + y
        for j in range(n):
            for peer in [(1 - x, y, c), (x, 1 - y, c), (1 - x, 1 - y, c)]:
                piece = arr[j].at[2 * peer[0] + peer[1]] if j < nb else arr[j]
                pltpu.make_async_remote_copy(src_ref=piece, dst_ref=arr[n + j].at[q_me],
                                             send_sem=sems[j], recv_sem=sems[n + j], device_id=peer,
                                             device_id_type=MESH).start()

    def finish(arr, s_in, _):
        x, y, c = _coords()
        for j in range(n):
            _wait_bytes_of(arr[n + j].at[pl.ds(0, 3)], s_in[j], s_in[n + j], (x, y, 1 - c))

    sems, arrays, token = _split_call("chip_exchange_%s_start" % tag, start, 2 * n, list(bigs) + list(smalls) + lands)
    result = work(token)
    _, arrays, _ = _split_call("chip_exchange_%s_finish" % tag, finish, 0, arrays, sems_in=sems, after=result[0])
    return arrays[n:], result


def _pair_sum_big(name, c_arr, mine, theirs):
    _, rows, cols = theirs.shape
    tr = rows

    def body(c_ref, a_ref, b_ref, o_ref):
        del c_ref
        o_ref[...] = (a_ref[...].astype(F32) + b_ref[...].astype(F32)).astype(BF16)

    return pl.pallas_call(
        body, name=name,
        grid_spec=pltpu.PrefetchScalarGridSpec(
            num_scalar_prefetch=1, grid=(N_CHIPS, rows // tr),
            in_specs=[pl.BlockSpec((None, tr, cols), lambda q, r, c_ref: (4 * c_ref[0] + q, r, 0)),
                      pl.BlockSpec((None, tr, cols), lambda q, r, c_ref: (q, r, 0))],
            out_specs=pl.BlockSpec((None, tr, cols), lambda q, r, c_ref: (q, r, 0))),
        out_shape=jax.ShapeDtypeStruct(theirs.shape, BF16),
        compiler_params=pltpu.CompilerParams(dimension_semantics=("arbitrary", "arbitrary")),
    )(c_arr, mine, theirs)


def _pair_sum_small(mine, theirs):
    n = len(mine)

    def body(*refs):
        for j in range(n):
            refs[2 * n + j][...] = refs[j][...] + refs[n + j][...]

    return pl.pallas_call(
        body, name="pair_sum_small",
        out_shape=[jax.ShapeDtypeStruct(m.shape, m.dtype) for m in mine],
    )(*mine, *theirs)


def _adamw_math(w, g, m, v):
    m = ADAM_B1 * m + (1.0 - ADAM_B1) * g
    v = ADAM_B2 * v + (1.0 - ADAM_B2) * (g * g)
    m_hat = m / (1.0 - ADAM_B1 ** ADAM_STEP)
    v_hat = v / (1.0 - ADAM_B2 ** ADAM_STEP)
    delta = -ADAM_LR * (m_hat / (jnp.sqrt(v_hat) + ADAM_EPS) + ADAM_WD * w)
    return delta, m, v


def _adamw_big(name, after, chip_sums, w, m, v):
    rows, cols = w.shape
    tr = min(rows, 512)

    def body(after_ref, t_ref, w_ref, m_ref, v_ref, g_out, d_out, m_out, v_out):
        del after_ref
        g = t_ref[0].astype(F32)
        for q in range(1, N_CHIPS):
            g = g + t_ref[q].astype(F32)
        d, mn, vn = _adamw_math(w_ref[...], g, m_ref[...], v_ref[...])
        g_out[...] = g
        d_out[...] = d
        m_out[...] = mn
        v_out[...] = vn

    blk = pl.BlockSpec((tr, cols), lambda r: (r, 0))
    return pl.pallas_call(
        body, name=name, grid=(rows // tr,),
        in_specs=[_ANY, pl.BlockSpec((N_CHIPS, tr, cols), lambda r: (0, r, 0)), blk, blk, blk],
        out_specs=[blk] * 4,
        out_shape=[jax.ShapeDtypeStruct((rows, cols), F32)] * 4,
        compiler_params=pltpu.CompilerParams(dimension_semantics=("arbitrary",), vmem_limit_bytes=VMEM_LIMIT),
    )(after, chip_sums, w, m, v)


SMALL_ROWS = ("loss", "norm1_pre_g", "pool_scale", "sgu_ln_g", "sgu_ln_b", "norm1_post_g", "norm2_pre_g",
              "norm2_post_g")


def _adamw_small(u_rows, u_bin, u_ws, u_bs, params):
    n = len(params)

    def body(*refs):
        urow_ref, ubin_ref, uws_ref, ubs_ref = refs[:4]
        wmv = refs[4:4 + 3 * n]
        loss_ref = refs[4 + 3 * n]
        outs = refs[5 + 3 * n:]

        def total(ref, idx):
            g = ref[(0,) + idx]
            for q in range(1, N_CHIPS):
                g = g + ref[(q,) + idx]
            return g

        loss_ref[...] = total(urow_ref, (slice(0, 1), slice(None)))
        for p in range(n):
            if p < 7:
                g = total(urow_ref, (slice(p + 1, p + 2), slice(None)))
            else:
                g = total((ubin_ref, uws_ref, ubs_ref)[p - 7], (slice(None), slice(None)))
            d, mn, vn = _adamw_math(wmv[3 * p][...], g, wmv[3 * p + 1][...], wmv[3 * p + 2][...])
            outs[4 * p][...] = g
            outs[4 * p + 1][...] = d
            outs[4 * p + 2][...] = mn
            outs[4 * p + 3][...] = vn

    flat = [a for p in params for a in p]
    out_shape = [jax.ShapeDtypeStruct((1, D_MODEL), F32)]
    for w, _, _ in params:
        out_shape += [jax.ShapeDtypeStruct(w.shape, F32)] * 4
    return pl.pallas_call(body, name="adamw_small", out_shape=out_shape)(u_rows, u_bin, u_ws, u_bs, *flat)


def _slots_of_rows(full):
    owners = [_owner_of_slot(s) for s in range(N_DEV)]
    parts = full.reshape(N_GROUPS, N_DEV, PG_SHARD, GROUP)
    return jnp.stack([parts[:, o] for o in owners]).reshape(N_DEV, N_GROUPS * PG_SHARD, GROUP).astype(BF16)


def kernel(x, norm1_pre_g, w_in, b_in, w_pool, pool_scale, sgu_ln_g, sgu_ln_b, w_spatial, b_spatial, w_sgu_proj, w_out, norm1_post_g, norm2_pre_g, w_ff1, w_ff2, norm2_post_g, loss_target, m_norm1_pre_g, m_w_in, m_b_in, m_w_pool, m_pool_scale, m_sgu_ln_g, m_sgu_ln_b, m_w_spatial, m_b_spatial, m_w_sgu_proj, m_w_out, m_norm1_post_g, m_norm2_pre_g, m_w_ff1, m_w_ff2, m_norm2_post_g, v_norm1_pre_g, v_w_in, v_b_in, v_w_pool, v_pool_scale, v_sgu_ln_g, v_sgu_ln_b, v_w_spatial, v_b_spatial, v_w_sgu_proj, v_w_out, v_norm1_post_g, v_norm2_pre_g, v_w_ff1, v_w_ff2, v_norm2_post_g):
    t_len = x.shape[1]
    row = lambda a: a.reshape(1, -1)
    x2 = x.reshape(t_len, D_MODEL)
    tgt2 = loss_target.reshape(t_len, D_MODEL)
    pg2 = lambda a: a.reshape(N_GROUPS * PG_SHARD, GROUP)

    g_in, g_pool, g_proj, g_out = _all_gather(
        [w_in.astype(BF16), pg2(w_pool).astype(BF16), pg2(w_sgu_proj).astype(BF16), w_out.astype(BF16)])
    regroup = lambda g: g.reshape(N_DEV, N_GROUPS, PG_SHARD, GROUP).transpose(1, 0, 2, 3).reshape(N_GROUPS, GROUP, GROUP)
    wpool_f, wproj_f = regroup(g_pool), regroup(g_proj)
    wout_f = g_out.reshape(D_MODEL, D_MODEL)
    bsp_t = b_spatial.T

    def forward(token):
        z, xn_b, y, h1 = _fwd_mix(token, x2, row(norm1_pre_g), g_in, row(b_in), wpool_f, row(pool_scale),
                                  row(sgu_ln_g), row(sgu_ln_b), w_spatial, bsp_t, wproj_f, wout_f, row(norm1_post_g))
        return h1, z, xn_b, y

    (g_ff1, g_ff2), (h1, z, xn_b, y) = _gather_behind([w_ff1.astype(BF16), w_ff2.astype(BF16)], g_in, forward)
    w2_f = g_ff2.reshape(D_FF, D_MODEL)
    hn_b, f_b, df1_b, df2_b, dh1, dg2post, dg2pre, loss_p = _mlp(h1, tgt2, row(norm2_pre_g), row(norm2_post_g),
                                                               g_ff1, w2_f)
    p_ff2 = _wgrad("wgrad_ff2", f_b, df2_b, FF_SHARD, True)
    p_ff1 = _wgrad("wgrad_ff1", hn_b, df1_b, FF_SHARD, False)
    c_arr = lax.axis_index("c").astype(jnp.int32).reshape(1)
    ff_parts = [p_ff1, p_ff2]
    got_ff = _pair_exchange("pair_exchange_ff", ff_parts, [])
    chip_ff = [_pair_sum_big("pair_sum_" + nm, c_arr, b, r) for nm, b, r in zip(("ff1", "ff2"), ff_parts, got_ff)]

    def backward_mix(token):
        return _bwd_mix(token, dh1, y, z, wpool_f, row(pool_scale), row(sgu_ln_g), row(sgu_ln_b), w_spatial, bsp_t,
                        wproj_f, wout_f, row(norm1_post_g))

    summed_ff, (dz_b, mg_b, dy_b, dwpool, dwproj, dws, dbsp_t, dg1post, dps, dlng, dlnb,
                dbin) = _chip_exchange_behind("ff", chip_ff, [], backward_mix)

    p_in = _wgrad("wgrad_in", xn_b, dz_b, IN_SHARD, False)
    got_in = _pair_exchange("pair_exchange_in", [p_in], [])
    chip_in = [_pair_sum_big("pair_sum_in", c_arr, p_in, got_in[0])]

    def backward_rest(token):
        dx, dg1pre = _bwd_in(token, dz_b, x2, dh1, row(norm1_pre_g), g_in)
        p_out = _wgrad("wgrad_out", mg_b, dy_b, OUT_SHARD, True)
        bigs = [_slots_of_rows(dwpool), _slots_of_rows(dwproj), p_out]
        rows = jnp.concatenate([jnp.broadcast_to(loss_p[:, 0:1], (1, D_MODEL)), dg1pre, dps, dlng, dlnb, dg1post,
                                dg2pre, dg2post], axis=0)
        smalls = [rows, dbin, dws.reshape(N_HEADS * SGU_BLOCK, SGU_BLOCK), dbsp_t.T]
        got = _pair_exchange("pair_exchange_mix", bigs, smalls)
        chip_bigs = [_pair_sum_big("pair_sum_" + nm, c_arr, b, r)
                     for nm, b, r in zip(("pool", "proj", "out"), bigs, got[:3])]
        return dx, chip_bigs, _pair_sum_small(smalls, got[3:])

    summed_in, (dx, chip_bigs, chip_smalls) = _chip_exchange_behind("in", chip_in, [], backward_rest)

    def update_large(token):
        res = [_adamw_big("adamw_" + nm, token, t, *p) for nm, t, p in
               (("in", summed_in[0], (w_in, m_w_in, v_w_in)), ("ff1", summed_ff[0], (w_ff1, m_w_ff1, v_w_ff1)),
                ("ff2", summed_ff[1], (w_ff2, m_w_ff2, v_w_ff2)))]
        return res[-1][0], res

    summed, (_, large) = _chip_exchange_behind("mix", chip_bigs, chip_smalls, update_large)
    big = {"in": large[0], "ff1": large[1], "ff2": large[2]}
    for nm, t, p in (("pool", summed[0], (pg2(w_pool), pg2(m_w_pool), pg2(v_w_pool))),
                     ("proj", summed[1], (pg2(w_sgu_proj), pg2(m_w_sgu_proj), pg2(v_w_sgu_proj))),
                     ("out", summed[2], (w_out, m_w_out, v_w_out))):
        big[nm] = _adamw_big("adamw_" + nm, t, t, *p)
    ws2 = lambda a: a.reshape(N_HEADS * SGU_BLOCK, SGU_BLOCK)
    small_params = [(row(norm1_pre_g), row(m_norm1_pre_g), row(v_norm1_pre_g)),
                    (row(pool_scale), row(m_pool_scale), row(v_pool_scale)),
                    (row(sgu_ln_g), row(m_sgu_ln_g), row(v_sgu_ln_g)),
                    (row(sgu_ln_b), row(m_sgu_ln_b), row(v_sgu_ln_b)),
                    (row(norm1_post_g), row(m_norm1_post_g), row(v_norm1_post_g)),
                    (row(norm2_pre_g), row(m_norm2_pre_g), row(v_norm2_pre_g)),
                    (row(norm2_post_g), row(m_norm2_post_g), row(v_norm2_post_g)),
                    (row(b_in), row(m_b_in), row(v_b_in)),
                    (ws2(w_spatial), ws2(m_w_spatial), ws2(v_w_spatial)),
                    (b_spatial, m_b_spatial, v_b_spatial)]
    small_out = _adamw_small(*summed[3:], small_params)
    loss = small_out[0][0, 0]
    small_names = SMALL_ROWS[1:] + ("b_in", "w_spatial", "b_spatial")
    small = {nm: small_out[1 + 4 * p:5 + 4 * p] for p, nm in enumerate(small_names)}

    shapes = {"norm1_pre_g": norm1_pre_g.shape, "w_in": w_in.shape, "b_in": b_in.shape, "w_pool": w_pool.shape,
              "pool_scale": pool_scale.shape, "sgu_ln_g": sgu_ln_g.shape, "sgu_ln_b": sgu_ln_b.shape,
              "w_spatial": w_spatial.shape, "b_spatial": b_spatial.shape, "w_sgu_proj": w_sgu_proj.shape,
              "w_out": w_out.shape, "norm1_post_g": norm1_post_g.shape, "norm2_pre_g": norm2_pre_g.shape,
              "w_ff1": w_ff1.shape, "w_ff2": w_ff2.shape, "norm2_post_g": norm2_post_g.shape}
    source = {"w_in": big["in"], "w_pool": big["pool"], "w_sgu_proj": big["proj"], "w_out": big["out"],
              "w_ff1": big["ff1"], "w_ff2": big["ff2"], **small}
    order = list(shapes)
    outs = [loss, dx.reshape(x.shape)]
    for kind in range(4):
        outs += [source[nm][kind].reshape(shapes[nm]) for nm in order]
    return tuple(outs)
```

```python
import functools
import math

import jax
import jax.numpy as jnp
from jax import lax
from jax.experimental import pallas as pl
from jax.experimental.pallas import tpu as pltpu

F32, BF16 = jnp.float32, jnp.bfloat16
MESH = pl.DeviceIdType.MESH

D_MODEL = 1024
D_IN = 5120
D_FF = 4096
N_DEV = 8
N_CHIPS = 4
WINDOWS = (2, 4, 8, 16)
N_GROUPS = 4
GROUP = 256
HALO = 16
SGU_BLOCK = 128
N_HEADS = 4
HEAD = 256
CHUNK = 64
EPS = 1e-6
IN_SHARD = D_IN // N_DEV
FF_SHARD = D_FF // N_DEV
OUT_SHARD = D_MODEL // N_DEV
PG_SHARD = GROUP // N_DEV

ADAM_LR, ADAM_B1, ADAM_B2, ADAM_EPS, ADAM_WD, ADAM_STEP = 0.001, 0.9, 0.999, 1e-08, 0.01, 10

VMEM_LIMIT = 56 * 1024 * 1024
TM = 256
TM_BWD = 256
TM_IN = 512
GELU_C0 = math.sqrt(2.0 / math.pi)
GELU_C1 = 0.044715


def _dot(a, b):
    return jnp.dot(a, b, preferred_element_type=F32)


def _dot_nt(a, b):
    return lax.dot_general(a, b, (((1,), (1,)), ((), ())), preferred_element_type=F32)


def _dot_tn(a, b):
    return lax.dot_general(a, b, (((0,), (0,)), ((), ())), preferred_element_type=F32)


def _gelu(x):
    t = jnp.tanh(GELU_C0 * (x + GELU_C1 * (x * x * x)))
    return 0.5 * x * (1.0 + t), t


def _gelu_grad(x, t):
    return 0.5 * (1.0 + t) + 0.5 * x * (1.0 - t * t) * (GELU_C0 * (1.0 + 3.0 * GELU_C1 * (x * x)))


def _sigmoid(x):
    return 1.0 / (1.0 + jnp.exp(-x))


def _mean(x):
    return jnp.mean(x, axis=-1, keepdims=True)


def _colsum(x):
    return jnp.sum(x, axis=0, keepdims=True)


def _const_spec(shape):
    nd = len(shape)
    return pl.BlockSpec(shape, lambda *_: (0,) * nd, pipeline_mode=pl.Buffered(1))


def _acc_spec(shape):
    nd = len(shape)
    return pl.BlockSpec(shape, lambda *_: (0,) * nd)


def _masked_ws(ws_ref):
    ri = lax.broadcasted_iota(jnp.int32, (SGU_BLOCK, SGU_BLOCK), 0) // CHUNK
    ci = lax.broadcasted_iota(jnp.int32, (SGU_BLOCK, SGU_BLOCK), 1) // CHUNK
    return [jnp.where(ri >= ci, ws_ref[h], 0.0).astype(BF16) for h in range(N_HEADS)]


def _pool_fwd(pbuf, tile_idx, tm):
    pos = lax.broadcasted_iota(jnp.int32, (tm, 1), 0) + tile_idx * tm + 1
    pooled = []
    for g, w in enumerate(WINDOWS):
        e = pbuf[:, g * GROUP:(g + 1) * GROUP]
        s, sh = e, 1
        while sh < w:
            s = s + pltpu.roll(s, sh, 0)
            sh *= 2
        inv = 1.0 / jnp.minimum(pos, w).astype(F32)
        pooled.append(s[HALO:] * inv - e[HALO:])
    return pooled


def _sgu_fwd(zu, zv, lng, lnb, wsm, bsp_ref, tm):
    u, tu = _gelu(zu)
    gv, tv = _gelu(zv)
    xc = gv - _mean(gv)
    rln = lax.rsqrt(_mean(xc * xc) + EPS)
    xhat = xc * rln
    vb = (xhat * lng + lnb).astype(BF16)
    sv_heads = []
    for h in range(N_HEADS):
        rows = []
        for n in range(tm // SGU_BLOCK):
            blk = vb[n * SGU_BLOCK:(n + 1) * SGU_BLOCK, h * HEAD:(h + 1) * HEAD]
            rows.append(_dot(wsm[h], blk) + bsp_ref[:, h:h + 1])
        sv_heads.append(jnp.concatenate(rows, axis=0))
    return u, tu, tv, xhat, rln, vb, sv_heads


def _fwd_mix(after, x, g1pre, win_g, b_in, wpool, pool_scale, lng, lnb, ws, bsp_t, wproj, wout, g1post):
    t_len = x.shape[0]
    tm = TM
    nt = t_len // tm

    def body(after_ref, xa_ref, xb_ref, g1_ref, win_ref, bin_ref, wpool_ref, ps_ref, lng_ref, lnb_ref, ws_ref, bsp_ref,
             wproj_ref, wout_ref, g1post_ref, z_ref, xn_ref, y_ref, h1_ref, zcur, znext, pbuf):
        del after_ref
        s = pl.program_id(0)

        @pl.when(s == 0)
        def _():
            znext[...] = jnp.zeros((tm, D_IN), F32)
            pbuf[...] = jnp.zeros((tm + HALO, D_MODEL), F32)

        zcur[...] = znext[...]
        xv = xa_ref[...]
        r1 = lax.rsqrt(_mean(xv * xv) + EPS)
        xnb = (xv * r1 * g1_ref[...]).astype(BF16)
        xn_ref[...] = xnb

        def project(k):
            cols = slice(k * IN_SHARD, (k + 1) * IN_SHARD)
            zk = _dot(xnb, win_ref[k]) + bin_ref[:, cols]
            z_ref[:, cols] = zk
            znext[:, cols] = zk

        project(0)
        pbuf[0:HALO, :] = jnp.where(s <= 1, 0.0, pbuf[0:HALO, :])
        pbuf[HALO:, :] = zcur[:, 0:D_MODEL]
        pooled = _pool_fwd(pbuf, jnp.maximum(s - 1, 0), tm)
        pbuf[0:HALO, :] = pbuf[tm:tm + HALO, :]
        a = jnp.concatenate([_dot(pooled[g].astype(BF16), wpool_ref[g]) for g in range(N_GROUPS)], axis=1)
        a = a * ps_ref[...]
        project(1)
        u, _ = _gelu(zcur[:, D_MODEL:2 * D_MODEL])
        project(2)
        gv, _ = _gelu(zcur[:, 2 * D_MODEL:3 * D_MODEL])
        xc = gv - _mean(gv)
        vb = (xc * lax.rsqrt(_mean(xc * xc) + EPS) * lng_ref[...] + lnb_ref[...]).astype(BF16)
        wsm = _masked_ws(ws_ref)
        bbr = []
        for h in range(N_HEADS):
            project(3 + h)
            sv = jnp.concatenate(
                [_dot(wsm[h], vb[n * SGU_BLOCK:(n + 1) * SGU_BLOCK, h * HEAD:(h + 1) * HEAD]) + bsp_ref[:, h:h + 1]
                 for n in range(tm // SGU_BLOCK)], axis=0)
            bbr.append(_dot((u[:, h * HEAD:(h + 1) * HEAD] * sv).astype(BF16), wproj_ref[h]))
        bbr = jnp.concatenate(bbr, axis=1)
        project(7)
        sa = _sigmoid(zcur[:, 3 * D_MODEL:4 * D_MODEL])
        sb = _sigmoid(zcur[:, 4 * D_MODEL:5 * D_MODEL])
        yv = _dot((sa * a + sb * bbr).astype(BF16), wout_ref[...])
        y_ref[...] = yv
        ry = lax.rsqrt(_mean(yv * yv) + EPS)
        h1_ref[...] = xb_ref[...] + yv * ry * g1post_ref[...]

    proj = lambda w: pl.BlockSpec((tm, w), lambda s: (jnp.minimum(s, nt - 1), 0))
    mix = lambda w: pl.BlockSpec((tm, w), lambda s: (jnp.maximum(s - 1, 0), 0))
    return pl.pallas_call(
        body, name="fwd_mix", grid=(nt + 1,),
        in_specs=[_ANY, proj(D_MODEL), mix(D_MODEL), _const_spec((1, D_MODEL)),
                  _const_spec((N_DEV, D_MODEL, IN_SHARD)),
                  _const_spec((1, D_IN)), _const_spec((N_GROUPS, GROUP, GROUP)), _const_spec((1, D_MODEL)),
                  _const_spec((1, D_MODEL)), _const_spec((1, D_MODEL)),
                  _const_spec((N_HEADS, SGU_BLOCK, SGU_BLOCK)), _const_spec((SGU_BLOCK, N_HEADS)),
                  _const_spec((N_HEADS, HEAD, HEAD)), _const_spec((D_MODEL, D_MODEL)), _const_spec((1, D_MODEL))],
        out_specs=[proj(D_IN), proj(D_MODEL), mix(D_MODEL), mix(D_MODEL)],
        out_shape=[jax.ShapeDtypeStruct((t_len, D_IN), F32), jax.ShapeDtypeStruct((t_len, D_MODEL), BF16),
                   jax.ShapeDtypeStruct((t_len, D_MODEL), F32), jax.ShapeDtypeStruct((t_len, D_MODEL), F32)],
        scratch_shapes=[pltpu.VMEM((tm, D_IN), F32), pltpu.VMEM((tm, D_IN), F32),
                        pltpu.VMEM((tm + HALO, D_MODEL), F32)],
        compiler_params=pltpu.CompilerParams(dimension_semantics=("arbitrary",), vmem_limit_bytes=VMEM_LIMIT),
    )(after, x, x, g1pre, win_g, b_in, wpool, pool_scale, lng, lnb, ws, bsp_t, wproj, wout, g1post)


def _mlp(h1, target, g2pre, g2post, w1_g, w2):
    t_len = h1.shape[0]
    tm = TM
    nt = t_len // tm

    def body(h1_ref, tgt_ref, g2pre_ref, g2post_ref, w1_ref, w2_ref,
             hn_ref, f_ref, df1_ref, df2_ref, dh1_ref, dg2post_ref, dg2pre_ref, loss_ref, f1_scr):
        i = pl.program_id(0)

        @pl.when(i == 0)
        def _():
            dg2post_ref[...] = jnp.zeros_like(dg2post_ref)
            dg2pre_ref[...] = jnp.zeros_like(dg2pre_ref)
            loss_ref[...] = jnp.zeros_like(loss_ref)

        h = h1_ref[...]
        r2 = lax.rsqrt(_mean(h * h) + EPS)
        nh = h * r2
        hnb = (nh * g2pre_ref[...]).astype(BF16)
        hn_ref[...] = hnb
        for k in range(N_DEV):
            f1_scr[:, k * FF_SHARD:(k + 1) * FF_SHARD] = _dot(hnb, w1_ref[k])
        r = jnp.maximum(f1_scr[...], 0.0)
        fb = (r * r).astype(BF16)
        f_ref[...] = fb
        f2 = _dot(fb, w2_ref[...])
        rf = lax.rsqrt(_mean(f2 * f2) + EPS)
        nf = f2 * rf
        diff = h + nf * g2post_ref[...] - tgt_ref[...]
        loss_ref[...] += (0.5 / D_MODEL) * jnp.sum(diff * diff)
        dout = diff * (1.0 / D_MODEL)
        dg2post_ref[...] += _colsum(dout * nf)
        dn = dout * g2post_ref[...]
        df2b = (rf * (dn - nf * _mean(dn * nf))).astype(BF16)
        df2_ref[...] = df2b
        df = _dot_nt(df2b, w2_ref[...])
        df1b = (df * (2.0 * jnp.maximum(f1_scr[...], 0.0))).astype(BF16)
        df1_ref[...] = df1b
        dhn = _dot_nt(df1b[:, 0:FF_SHARD], w1_ref[0])
        for k in range(1, N_DEV):
            dhn = dhn + _dot_nt(df1b[:, k * FF_SHARD:(k + 1) * FF_SHARD], w1_ref[k])
        dg2pre_ref[...] += _colsum(dhn * nh)
        dnh = dhn * g2pre_ref[...]
        dh1_ref[...] = dout + r2 * (dnh - nh * _mean(dnh * nh))

    tok = lambda w: pl.BlockSpec((tm, w), lambda i: (i, 0))
    return pl.pallas_call(
        body, name="mlp_fwd_bwd", grid=(nt,),
        in_specs=[tok(D_MODEL), tok(D_MODEL), _const_spec((1, D_MODEL)), _const_spec((1, D_MODEL)),
                  _const_spec((N_DEV, D_MODEL, FF_SHARD)), _const_spec((D_FF, D_MODEL))],
        out_specs=[tok(D_MODEL), tok(D_FF), tok(D_FF), tok(D_MODEL), tok(D_MODEL),
                   _acc_spec((1, D_MODEL)), _acc_spec((1, D_MODEL)), _acc_spec((1, 128))],
        out_shape=[jax.ShapeDtypeStruct((t_len, D_MODEL), BF16), jax.ShapeDtypeStruct((t_len, D_FF), BF16),
                   jax.ShapeDtypeStruct((t_len, D_FF), BF16), jax.ShapeDtypeStruct((t_len, D_MODEL), BF16),
                   jax.ShapeDtypeStruct((t_len, D_MODEL), F32), jax.ShapeDtypeStruct((1, D_MODEL), F32),
                   jax.ShapeDtypeStruct((1, D_MODEL), F32), jax.ShapeDtypeStruct((1, 128), F32)],
        scratch_shapes=[pltpu.VMEM((tm, D_FF), F32)],
        compiler_params=pltpu.CompilerParams(dimension_semantics=("arbitrary",), vmem_limit_bytes=VMEM_LIMIT),
    )(h1, target, g2pre, g2post, w1_g, w2)


def _bwd_mix(after, dh1, y, z, wpool, pool_scale, lng, lnb, ws, bsp_t, wproj, wout, g1post):
    t_len = y.shape[0]
    tm = TM_BWD
    nt = t_len // tm
    nblk = tm // SGU_BLOCK

    def body(after_ref, dh1_ref, y_ref, z_ref, zh_ref, wpool_ref, ps_ref, lng_ref, lnb_ref, ws_ref,
             bsp_ref, wproj_ref, wout_ref, g1post_ref,
             dz_ref, mg_ref, dy_ref, dwpool_ref, dwproj_ref, dws_ref, dbsp_ref, dg1post_ref, dps_ref,
             dlng_ref, dlnb_ref, dbin_ref, pbuf, qbuf):
        del after_ref
        i = pl.program_id(0)
        ti = nt - 1 - i

        @pl.when(i == 0)
        def _():
            for ref in (dwpool_ref, dwproj_ref, dws_ref, dbsp_ref, dg1post_ref, dps_ref, dlng_ref, dlnb_ref,
                        dbin_ref):
                ref[...] = jnp.zeros_like(ref)
            qbuf[tm:tm + HALO, :] = jnp.zeros((HALO, D_MODEL), F32)

        pbuf[0:HALO, :] = jnp.where(ti > 0, zh_ref[...], 0.0)
        pbuf[HALO:, :] = z_ref[:, 0:D_MODEL]
        pooled = _pool_fwd(pbuf, ti, tm)
        pooled_b = [p.astype(BF16) for p in pooled]
        a_raw = jnp.concatenate([_dot(pooled_b[g], wpool_ref[g]) for g in range(N_GROUPS)], axis=1)
        wsm = _masked_ws(ws_ref)
        zu = z_ref[:, D_MODEL:2 * D_MODEL]
        zv = z_ref[:, 2 * D_MODEL:3 * D_MODEL]
        u, tu, tv, xhat, rln, vb, sv_heads = _sgu_fwd(zu, zv, lng_ref[...], lnb_ref[...], wsm, bsp_ref, tm)
        gated_b = [(u[:, h * HEAD:(h + 1) * HEAD] * sv_heads[h]).astype(BF16) for h in range(N_HEADS)]
        bbr = jnp.concatenate([_dot(gated_b[h], wproj_ref[h]) for h in range(N_HEADS)], axis=1)
        sa = _sigmoid(z_ref[:, 3 * D_MODEL:4 * D_MODEL])
        sb = _sigmoid(z_ref[:, 4 * D_MODEL:5 * D_MODEL])
        a = a_raw * ps_ref[...]
        mg_ref[...] = (sa * a + sb * bbr).astype(BF16)

        dh = dh1_ref[...]
        yv = y_ref[...]
        ry = lax.rsqrt(_mean(yv * yv) + EPS)
        ny = yv * ry
        dg1post_ref[...] += _colsum(dh * ny)
        dn = dh * g1post_ref[...]
        dyb = (ry * (dn - ny * _mean(dn * ny))).astype(BF16)
        dy_ref[...] = dyb
        dmg = _dot_nt(dyb, wout_ref[...])

        da = dmg * sa
        dbbr = dmg * sb
        dzga = dmg * a * sa * (1.0 - sa)
        dzgb = dmg * bbr * sb * (1.0 - sb)
        dz_ref[:, 3 * D_MODEL:4 * D_MODEL] = dzga.astype(BF16)
        dz_ref[:, 4 * D_MODEL:5 * D_MODEL] = dzgb.astype(BF16)
        dbin_ref[:, 3 * D_MODEL:4 * D_MODEL] += _colsum(dzga)
        dbin_ref[:, 4 * D_MODEL:5 * D_MODEL] += _colsum(dzgb)

        dps_ref[...] += _colsum(da * a_raw)
        da_raw_b = (da * ps_ref[...]).astype(BF16)
        pos = lax.broadcasted_iota(jnp.int32, (tm, 1), 0) + ti * tm + 1
        dpooled = []
        for g, w in enumerate(WINDOWS):
            cols = slice(g * GROUP, (g + 1) * GROUP)
            dwpool_ref[g] += _dot_tn(pooled_b[g], da_raw_b[:, cols])
            dp = _dot_nt(da_raw_b[:, cols], wpool_ref[g])
            dpooled.append(dp)
            qbuf[0:tm, cols] = dp * (1.0 / jnp.minimum(pos, w).astype(F32))
        n_ext = tm + HALO
        dzp = []
        for g, w in enumerate(WINDOWS):
            e = qbuf[:, g * GROUP:(g + 1) * GROUP]
            s, sh = e, 1
            while sh < w:
                s = s + pltpu.roll(s, n_ext - sh, 0)
                sh *= 2
            dzp.append(s[0:tm] - dpooled[g])
        qbuf[tm:tm + HALO, :] = qbuf[0:HALO, :]
        dzp = jnp.concatenate(dzp, axis=1)
        dz_ref[:, 0:D_MODEL] = dzp.astype(BF16)
        dbin_ref[:, 0:D_MODEL] += _colsum(dzp)

        dv_heads = []
        du_heads = []
        for h in range(N_HEADS):
            cols = slice(h * HEAD, (h + 1) * HEAD)
            dbbr_b = dbbr[:, cols].astype(BF16)
            dwproj_ref[h] += _dot_tn(gated_b[h], dbbr_b)
            dgated = _dot_nt(dbbr_b, wproj_ref[h])
            du_heads.append(dgated * sv_heads[h])
            dsv = dgated * u[:, cols]
            dsv_b = dsv.astype(BF16)
            rows = []
            for n in range(nblk):
                blk = slice(n * SGU_BLOCK, (n + 1) * SGU_BLOCK)
                rows.append(_dot_tn(wsm[h], dsv_b[blk]))
                dws_ref[h] += _dot_nt(dsv_b[blk], vb[blk, cols])
                dbsp_ref[:, h:h + 1] += jnp.sum(dsv[blk], axis=1, keepdims=True)
            dv_heads.append(jnp.concatenate(rows, axis=0))
        dzu = jnp.concatenate(du_heads, axis=1) * _gelu_grad(zu, tu)
        dz_ref[:, D_MODEL:2 * D_MODEL] = dzu.astype(BF16)
        dbin_ref[:, D_MODEL:2 * D_MODEL] += _colsum(dzu)
        dv = jnp.concatenate(dv_heads, axis=1)
        dlng_ref[...] += _colsum(dv * xhat)
        dlnb_ref[...] += _colsum(dv)
        dxh = dv * lng_ref[...]
        dgv = rln * (dxh - _mean(dxh) - xhat * _mean(dxh * xhat))
        dzv = dgv * _gelu_grad(zv, tv)
        dz_ref[:, 2 * D_MODEL:3 * D_MODEL] = dzv.astype(BF16)
        dbin_ref[:, 2 * D_MODEL:3 * D_MODEL] += _colsum(dzv)

        @pl.when(i == nt - 1)
        def _():
            ri = lax.broadcasted_iota(jnp.int32, (SGU_BLOCK, SGU_BLOCK), 0) // CHUNK
            ci = lax.broadcasted_iota(jnp.int32, (SGU_BLOCK, SGU_BLOCK), 1) // CHUNK
            for h in range(N_HEADS):
                dws_ref[h] = jnp.where(ri >= ci, dws_ref[h], 0.0)

    tok = lambda w: pl.BlockSpec((tm, w), lambda i: (nt - 1 - i, 0))
    halo = pl.BlockSpec((HALO, D_MODEL), lambda i: (jnp.maximum((nt - 1 - i) * (tm // HALO) - 1, 0), 0))
    return pl.pallas_call(
        body, name="bwd_mix", grid=(nt,),
        in_specs=[_ANY, tok(D_MODEL), tok(D_MODEL), tok(D_IN), halo, _const_spec((N_GROUPS, GROUP, GROUP)),
                  _const_spec((1, D_MODEL)), _const_spec((1, D_MODEL)), _const_spec((1, D_MODEL)),
                  _const_spec((N_HEADS, SGU_BLOCK, SGU_BLOCK)), _const_spec((SGU_BLOCK, N_HEADS)),
                  _const_spec((N_HEADS, HEAD, HEAD)), _const_spec((D_MODEL, D_MODEL)), _const_spec((1, D_MODEL))],
        out_specs=[tok(D_IN), tok(D_MODEL), tok(D_MODEL),
                   _acc_spec((N_GROUPS, GROUP, GROUP)), _acc_spec((N_HEADS, HEAD, HEAD)),
                   _acc_spec((N_HEADS, SGU_BLOCK, SGU_BLOCK)), _acc_spec((SGU_BLOCK, N_HEADS)),
                   _acc_spec((1, D_MODEL)), _acc_spec((1, D_MODEL)), _acc_spec((1, D_MODEL)), _acc_spec((1, D_MODEL)),
                   _acc_spec((1, D_IN))],
        out_shape=[jax.ShapeDtypeStruct((t_len, D_IN), BF16),
                   jax.ShapeDtypeStruct((t_len, D_MODEL), BF16), jax.ShapeDtypeStruct((t_len, D_MODEL), BF16),
                   jax.ShapeDtypeStruct((N_GROUPS, GROUP, GROUP), F32), jax.ShapeDtypeStruct((N_HEADS, HEAD, HEAD), F32),
                   jax.ShapeDtypeStruct((N_HEADS, SGU_BLOCK, SGU_BLOCK), F32),
                   jax.ShapeDtypeStruct((SGU_BLOCK, N_HEADS), F32),
                   jax.ShapeDtypeStruct((1, D_MODEL), F32), jax.ShapeDtypeStruct((1, D_MODEL), F32),
                   jax.ShapeDtypeStruct((1, D_MODEL), F32), jax.ShapeDtypeStruct((1, D_MODEL), F32),
                   jax.ShapeDtypeStruct((1, D_IN), F32)],
        scratch_shapes=[pltpu.VMEM((tm + HALO, D_MODEL), F32), pltpu.VMEM((tm + HALO, D_MODEL), F32)],
        compiler_params=pltpu.CompilerParams(dimension_semantics=("arbitrary",), vmem_limit_bytes=VMEM_LIMIT),
    )(after, dh1, y, z, z, wpool, pool_scale, lng, lnb, ws, bsp_t, wproj, wout, g1post)


def _bwd_in(after, dz_b, x, dh1, g1pre, win_g):
    t_len = x.shape[0]
    tm = TM_IN
    nt = t_len // tm

    def body(after_ref, dz_ref, x_ref, dh1_ref, g1_ref, win_ref, dx_ref, dg1pre_ref):
        del after_ref

        @pl.when(pl.program_id(0) == 0)
        def _():
            dg1pre_ref[...] = jnp.zeros_like(dg1pre_ref)

        dxn = _dot_nt(dz_ref[:, 0:IN_SHARD], win_ref[0])
        for k in range(1, N_DEV):
            dxn = dxn + _dot_nt(dz_ref[:, k * IN_SHARD:(k + 1) * IN_SHARD], win_ref[k])
        xv = x_ref[...]
        r1 = lax.rsqrt(_mean(xv * xv) + EPS)
        nx = xv * r1
        dg1pre_ref[...] += _colsum(dxn * nx)
        dnx = dxn * g1_ref[...]
        dx_ref[...] = r1 * (dnx - nx * _mean(dnx * nx)) + dh1_ref[...]

    tok = lambda w: pl.BlockSpec((tm, w), lambda i: (i, 0))
    return pl.pallas_call(
        body, name="bwd_in", grid=(nt,),
        in_specs=[_ANY, tok(D_IN), tok(D_MODEL), tok(D_MODEL), _const_spec((1, D_MODEL)),
                  _const_spec((N_DEV, D_MODEL, IN_SHARD))],
        out_specs=[tok(D_MODEL), _acc_spec((1, D_MODEL))],
        out_shape=[jax.ShapeDtypeStruct((t_len, D_MODEL), F32), jax.ShapeDtypeStruct((1, D_MODEL), F32)],
        compiler_params=pltpu.CompilerParams(dimension_semantics=("arbitrary",), vmem_limit_bytes=VMEM_LIMIT),
    )(after, dz_b, x, dh1, g1pre, win_g)


def _owner_of_slot(s):
    return 4 * ((s // 2) % 2) + 2 * (s % 2) + s // 4


def _wgrad(name, a, b, shard, a_sliced):
    t_len = a.shape[0]
    am = shard if a_sliced else a.shape[1]
    bn = b.shape[1] if a_sliced else shard

    def body(a_ref, b_ref, o_ref):
        o_ref[...] = _dot_tn(a_ref[...], b_ref[...]).astype(BF16)

    sliced = lambda w: pl.BlockSpec((t_len, w), lambda s: (0, _owner_of_slot(s)))
    whole = lambda w: pl.BlockSpec((t_len, w), lambda s: (0, 0), pipeline_mode=pl.Buffered(1))
    return pl.pallas_call(
        body, name=name, grid=(N_DEV,),
        in_specs=[sliced(am) if a_sliced else whole(am), whole(bn) if a_sliced else sliced(bn)],
        out_specs=pl.BlockSpec((None, am, bn), lambda s: (s, 0, 0)),
        out_shape=jax.ShapeDtypeStruct((N_DEV, am, bn), BF16),
        compiler_params=pltpu.CompilerParams(dimension_semantics=("arbitrary",), vmem_limit_bytes=VMEM_LIMIT),
    )(a, b)


def _coords():
    return lax.axis_index("x"), lax.axis_index("y"), lax.axis_index("c")


_ANY = pl.BlockSpec(memory_space=pl.ANY)


def _all_gather(shards):
    n = len(shards)

    def body(*refs):
        src, dst = refs[:n], refs[n:2 * n]
        send_sems, recv_sems, local_sems = refs[2 * n:]
        x, y, c = _coords()
        me, sibling = (x, y, c), (x, y, 1 - c)
        chips = [(1 - x, y), (x, 1 - y), (1 - x, 1 - y)]

        def slot(j, dev):
            return dst[j].at[4 * dev[0] + 2 * dev[1] + dev[2]]

        def copy(j, k, block, to, from_shard=False):
            return pltpu.make_async_remote_copy(
                src_ref=src[j] if from_shard else slot(j, block), dst_ref=slot(j, block),
                send_sem=send_sems.at[j, k], recv_sem=recv_sems.at[j, k], device_id=to, device_id_type=MESH)

        started = []
        local = []
        for j in range(n):
            for k, to in enumerate([(*chip, c) for chip in chips]):
                cp = copy(j, 1 + k, me, to, from_shard=True)
                cp.start()
                started.append(cp)
        for j in range(n):
            cp = copy(j, 0, me, sibling, from_shard=True)
            cp.start()
            started.append(cp)
            lc = pltpu.make_async_copy(src[j], slot(j, me), local_sems.at[j])
            lc.start()
            local.append(lc)
        for j in range(n):
            for k, chip in enumerate(chips):
                copy(j, 1 + k, (*chip, c), me).wait_recv()
                cp = copy(j, 4 + k, (*chip, c), sibling)
                cp.start()
                started.append(cp)
        for j in range(n):
            copy(j, 0, sibling, me).wait_recv()
            for k, chip in enumerate(chips):
                copy(j, 4 + k, (*chip, 1 - c), me).wait_recv()
        for cp in started:
            cp.wait_send()
        for lc in local:
            lc.wait()

    return pl.pallas_call(
        body, name="all_gather_weights",
        in_specs=[_ANY] * n, out_specs=[_ANY] * n,
        out_shape=[jax.ShapeDtypeStruct((N_DEV,) + s.shape, s.dtype) for s in shards],
        scratch_shapes=[pltpu.SemaphoreType.DMA((n, 7)), pltpu.SemaphoreType.DMA((n, 7)),
                        pltpu.SemaphoreType.DMA((n,))],
    )(*shards)


def _pair_exchange(name, bigs, smalls):
    nb, n = len(bigs), len(bigs) + len(smalls)

    def body(*refs):
        src, dst = refs[:n], refs[n:2 * n]
        send_sems, recv_sems = refs[2 * n:]
        x, y, c = _coords()
        copies = []
        for j in range(n):
            s = src[j].at[pl.ds(4 * (1 - c), 4)] if j < nb else src[j]
            cp = pltpu.make_async_remote_copy(src_ref=s, dst_ref=dst[j], send_sem=send_sems.at[j],
                                              recv_sem=recv_sems.at[j], device_id=(x, y, 1 - c), device_id_type=MESH)
            cp.start()
            copies.append(cp)
        for cp in copies:
            cp.wait()

    return pl.pallas_call(
        body, name=name,
        in_specs=[_ANY] * n, out_specs=[_ANY] * n,
        out_shape=[jax.ShapeDtypeStruct((4,) + b.shape[1:], b.dtype) for b in bigs]
        + [jax.ShapeDtypeStruct(s.shape, s.dtype) for s in smalls],
        scratch_shapes=[pltpu.SemaphoreType.DMA((n,)), pltpu.SemaphoreType.DMA((n,))],
    )(*bigs, *smalls)


def _chip_exchange(bigs, smalls):
    nb, n = len(bigs), len(bigs) + len(smalls)

    def body(*refs):
        src, dst = refs[:n], refs[n:2 * n]
        send_sems, recv_sems, local_sems = refs[2 * n:]
        x, y, c = _coords()
        q_me = 2 * x + y
        peers = [(1 - x, y, c), (x, 1 - y, c), (1 - x, 1 - y, c)]

        def piece(j, q):
            return src[j].at[q] if j < nb else src[j]

        started = []
        for j in range(n):
            for r, peer in enumerate(peers):
                cp = pltpu.make_async_remote_copy(
                    src_ref=piece(j, 2 * peer[0] + peer[1]), dst_ref=dst[j].at[q_me],
                    send_sem=send_sems.at[j, r], recv_sem=recv_sems.at[j, r], device_id=peer, device_id_type=MESH)
                cp.start()
                started.append(cp)
        local = []
        for j in range(n):
            lc = pltpu.make_async_copy(piece(j, q_me), dst[j].at[q_me], local_sems.at[j])
            lc.start()
            local.append(lc)
        for j in range(n):
            for r, peer in enumerate(peers):
                q_peer = 2 * peer[0] + peer[1]
                pltpu.make_async_remote_copy(
                    src_ref=piece(j, q_peer), dst_ref=dst[j].at[q_peer],
                    send_sem=send_sems.at[j, r], recv_sem=recv_sems.at[j, r], device_id=peer,
                    device_id_type=MESH).wait_recv()
        for cp in started:
            cp.wait_send()
        for lc in local:
            lc.wait()

    return pl.pallas_call(
        body, name="chip_exchange",
        in_specs=[_ANY] * n, out_specs=[_ANY] * n,
        out_shape=[jax.ShapeDtypeStruct(b.shape, b.dtype) for b in bigs]
        + [jax.ShapeDtypeStruct((N_CHIPS,) + s.shape, s.dtype) for s in smalls],
        scratch_shapes=[pltpu.SemaphoreType.DMA((n, 3)), pltpu.SemaphoreType.DMA((n, 3)),
                        pltpu.SemaphoreType.DMA((n,))],
    )(*bigs, *smalls)


_HBM = pl.BlockSpec(memory_space=pltpu.HBM)
_SEM = pl.BlockSpec(memory_space=pltpu.SEMAPHORE)
_VMEM = pl.BlockSpec(memory_space=pltpu.VMEM)
_EFFECT = pltpu.SideEffectType.DATAFLOW_SIDE_EFFECTING
_TOKEN = jax.ShapeDtypeStruct((8, 128), F32)


def _in_hbm(a):
    return pltpu.with_memory_space_constraint(a, pltpu.HBM)


def _split_call(name, body, n_sems_out, arrays, sems_in=(), after=None):
    na, ns = len(arrays), len(sems_in)
    has_after = after is not None

    def kernel_body(*refs):
        arr = refs[:na]
        s_in = refs[na:na + ns]
        outs = refs[na + ns + has_after:]
        body(arr, s_in, outs[:n_sems_out])
        outs[-1][...] = jnp.zeros((8, 128), F32)

    out_shape = ([pltpu.SemaphoreType.DMA(())] * n_sems_out + [pltpu.HBM(a.shape, a.dtype) for a in arrays] + [_TOKEN])
    res = pl.pallas_call(
        kernel_body, name=name, out_shape=out_shape,
        in_specs=[_HBM] * na + [_SEM] * ns + [_ANY] * has_after,
        out_specs=[_SEM] * n_sems_out + [_HBM] * na + [_VMEM],
        input_output_aliases={i: n_sems_out + i for i in range(na)},
        compiler_params=pltpu.CompilerParams(has_side_effects=_EFFECT),
    )(*[_in_hbm(a) for a in arrays], *sems_in, *([after] if has_after else []))
    return list(res[:n_sems_out]), list(res[n_sems_out:n_sems_out + na]), res[-1]


def _wait_bytes_of(ref, send_sem, recv_sem, peer, send=True, recv=True):
    cp = pltpu.make_async_remote_copy(src_ref=ref, dst_ref=ref, send_sem=send_sem, recv_sem=recv_sem,
                                      device_id=peer, device_id_type=MESH)
    if send:
        cp.wait_send()
    if recv:
        cp.wait_recv()


def _gather_behind(shards, after, work):
    n = len(shards)
    x, y, c = _coords()
    me = 4 * x + 2 * y + c
    lands = [lax.dynamic_update_slice(lax.empty((N_DEV,) + s.shape, s.dtype), s[None], (me,) + (0,) * s.ndim)
             for s in shards]

    def start(arr, _, sems):
        x, y, c = _coords()
        me = 4 * x + 2 * y + c
        for j in range(n):
            for chip in [(1 - x, y), (x, 1 - y), (1 - x, 1 - y)]:
                pltpu.make_async_remote_copy(src_ref=arr[j], dst_ref=arr[n + j].at[me], send_sem=sems[j],
                                             recv_sem=sems[n + j], device_id=(*chip, c), device_id_type=MESH).start()
        for j in range(n):
            pltpu.make_async_remote_copy(src_ref=arr[j], dst_ref=arr[n + j].at[me], send_sem=sems[2 * n + j],
                                         recv_sem=sems[3 * n + j], device_id=(x, y, 1 - c),
                                         device_id_type=MESH).start()

    def middle(arr, s_in, sems):
        x, y, c = _coords()
        sibling = (x, y, 1 - c)
        for j in range(n):
            _wait_bytes_of(arr[n + j].at[pl.ds(0, 3)], s_in[j], s_in[n + j], sibling)
        for j in range(n):
            for chip in [(1 - x, y), (x, 1 - y), (1 - x, 1 - y)]:
                slot = arr[n + j].at[4 * chip[0] + 2 * chip[1] + c]
                pltpu.make_async_remote_copy(src_ref=slot, dst_ref=slot, send_sem=sems[j], recv_sem=sems[n + j],
                                             device_id=sibling, device_id_type=MESH).start()

    def finish(arr, s_in, _):
        x, y, c = _coords()
        sibling = (x, y, 1 - c)
        for j in range(n):
            _wait_bytes_of(arr[n + j].at[pl.ds(0, 1)], s_in[j], s_in[n + j], sibling)
            _wait_bytes_of(arr[n + j].at[pl.ds(0, 3)], s_in[2 * n + j], s_in[3 * n + j], sibling)

    sems, arrays, token = _split_call("gather_ff_start", start, 4 * n, list(shards) + lands, after=after)
    result = work(token)
    fwd_sems, arrays, token = _split_call("gather_ff_middle", middle, 2 * n, arrays, sems_in=sems[:2 * n],
                                          after=result[0])
    _, arrays, _ = _split_call("gather_ff_finish", finish, 0, arrays, sems_in=sems[2 * n:] + fwd_sems, after=token)
    return arrays[n:], result


def _chip_exchange_behind(tag, bigs, smalls, work):
    nb, n = len(bigs), len(bigs) + len(smalls)
    x, y, _ = _coords()
    q_me = 2 * x + y
    lands = [lax.dynamic_update_slice(lax.empty(s.shape, s.dtype), lax.dynamic_slice_in_dim(s, q_me, 1),
                                      (q_me,) + (0,) * (s.ndim - 1)) for s in bigs]
    lands += [lax.dynamic_update_slice(lax.empty((N_CHIPS,) + s.shape, s.dtype), s[None], (q_me,) + (0,) * s.ndim)
              for s in smalls]

    def start(arr, _, sems):
        x, y, c = _coords()
        q_me = 2 * x + y
        for j in range(n):
            for peer in [(1 - x, y, c), (x, 1 - y, c), (1 - x, 1 - y, c)]:
                piece = arr[j].at[2 * peer[0] + peer[1]] if j < nb else arr[j]
                pltpu.make_async_remote_copy(src_ref=piece, dst_ref=arr[n + j].at[q_me],
                                             send_sem=sems[j], recv_sem=sems[n + j], device_id=peer,
                                             device_id_type=MESH).start()

    def finish(arr, s_in, _):
        x, y, c = _coords()
        for j in range(n):
            _wait_bytes_of(arr[n + j].at[pl.ds(0, 3)], s_in[j], s_in[n + j], (x, y, 1 - c))

    sems, arrays, token = _split_call("chip_exchange_%s_start" % tag, start, 2 * n, list(bigs) + list(smalls) + lands)
    result = work(token)
    _, arrays, _ = _split_call("chip_exchange_%s_finish" % tag, finish, 0, arrays, sems_in=sems, after=result[0])
    return arrays[n:], result


def _pair_sum_big(name, c_arr, mine, theirs):
    _, rows, cols = theirs.shape
    tr = rows

    def body(c_ref, a_ref, b_ref, o_ref):
        del c_ref
        o_ref[...] = (a_ref[...].astype(F32) + b_ref[...].astype(F32)).astype(BF16)

    return pl.pallas_call(
        body, name=name,
        grid_spec=pltpu.PrefetchScalarGridSpec(
            num_scalar_prefetch=1, grid=(N_CHIPS, rows // tr),
            in_specs=[pl.BlockSpec((None, tr, cols), lambda q, r, c_ref: (4 * c_ref[0] + q, r, 0)),
                      pl.BlockSpec((None, tr, cols), lambda q, r, c_ref: (q, r, 0))],
            out_specs=pl.BlockSpec((None, tr, cols), lambda q, r, c_ref: (q, r, 0))),
        out_shape=jax.ShapeDtypeStruct(theirs.shape, BF16),
        compiler_params=pltpu.CompilerParams(dimension_semantics=("arbitrary", "arbitrary")),
    )(c_arr, mine, theirs)


def _pair_sum_small(name, mine, theirs):
    n = len(mine)

    def body(*refs):
        for j in range(n):
            refs[2 * n + j][...] = refs[j][...] + refs[n + j][...]

    return pl.pallas_call(
        body, name=name,
        out_shape=[jax.ShapeDtypeStruct(m.shape, m.dtype) for m in mine],
    )(*mine, *theirs)


def _adamw_math(w, g, m, v):
    m = ADAM_B1 * m + (1.0 - ADAM_B1) * g
    v = ADAM_B2 * v + (1.0 - ADAM_B2) * (g * g)
    m_hat = m / (1.0 - ADAM_B1 ** ADAM_STEP)
    v_hat = v / (1.0 - ADAM_B2 ** ADAM_STEP)
    delta = -ADAM_LR * (m_hat / (jnp.sqrt(v_hat) + ADAM_EPS) + ADAM_WD * w)
    return delta, m, v


def _adamw_big(name, after, chip_sums, w, m, v):
    rows, cols = w.shape
    tr = min(rows, 512)

    def body(after_ref, t_ref, w_ref, m_ref, v_ref, g_out, d_out, m_out, v_out):
        del after_ref
        g = t_ref[0].astype(F32)
        for q in range(1, N_CHIPS):
            g = g + t_ref[q].astype(F32)
        d, mn, vn = _adamw_math(w_ref[...], g, m_ref[...], v_ref[...])
        g_out[...] = g
        d_out[...] = d
        m_out[...] = mn
        v_out[...] = vn

    blk = pl.BlockSpec((tr, cols), lambda r: (r, 0))
    return pl.pallas_call(
        body, name=name, grid=(rows // tr,),
        in_specs=[_ANY, pl.BlockSpec((N_CHIPS, tr, cols), lambda r: (0, r, 0)), blk, blk, blk],
        out_specs=[blk] * 4,
        out_shape=[jax.ShapeDtypeStruct((rows, cols), F32)] * 4,
        compiler_params=pltpu.CompilerParams(dimension_semantics=("arbitrary",), vmem_limit_bytes=VMEM_LIMIT),
    )(after, chip_sums, w, m, v)


SMALL_ROWS = ("loss", "norm1_pre_g", "pool_scale", "sgu_ln_g", "sgu_ln_b", "norm1_post_g", "norm2_pre_g",
              "norm2_post_g")


def _adamw_small(u_rows, u_bin, u_ws, u_bs, params):
    n = len(params)

    def body(*refs):
        urow_ref, ubin_ref, uws_ref, ubs_ref = refs[:4]
        wmv = refs[4:4 + 3 * n]
        loss_ref = refs[4 + 3 * n]
        outs = refs[5 + 3 * n:]

        def total(ref, idx):
            g = ref[(0,) + idx]
            for q in range(1, N_CHIPS):
                g = g + ref[(q,) + idx]
            return g

        loss_ref[...] = total(urow_ref, (slice(0, 1), slice(None)))
        for p in range(n):
            if p < 7:
                g = total(urow_ref, (slice(p + 1, p + 2), slice(None)))
            else:
                g = total((ubin_ref, uws_ref, ubs_ref)[p - 7], (slice(None), slice(None)))
            d, mn, vn = _adamw_math(wmv[3 * p][...], g, wmv[3 * p + 1][...], wmv[3 * p + 2][...])
            outs[4 * p][...] = g
            outs[4 * p + 1][...] = d
            outs[4 * p + 2][...] = mn
            outs[4 * p + 3][...] = vn

    flat = [a for p in params for a in p]
    out_shape = [jax.ShapeDtypeStruct((1, D_MODEL), F32)]
    for w, _, _ in params:
        out_shape += [jax.ShapeDtypeStruct(w.shape, F32)] * 4
    return pl.pallas_call(body, name="adamw_small", out_shape=out_shape)(u_rows, u_bin, u_ws, u_bs, *flat)


def _slots_of_rows(full):
    owners = [_owner_of_slot(s) for s in range(N_DEV)]
    parts = full.reshape(N_GROUPS, N_DEV, PG_SHARD, GROUP)
    return jnp.stack([parts[:, o] for o in owners]).reshape(N_DEV, N_GROUPS * PG_SHARD, GROUP).astype(BF16)


def kernel(x, norm1_pre_g, w_in, b_in, w_pool, pool_scale, sgu_ln_g, sgu_ln_b, w_spatial, b_spatial, w_sgu_proj, w_out, norm1_post_g, norm2_pre_g, w_ff1, w_ff2, norm2_post_g, loss_target, m_norm1_pre_g, m_w_in, m_b_in, m_w_pool, m_pool_scale, m_sgu_ln_g, m_sgu_ln_b, m_w_spatial, m_b_spatial, m_w_sgu_proj, m_w_out, m_norm1_post_g, m_norm2_pre_g, m_w_ff1, m_w_ff2, m_norm2_post_g, v_norm1_pre_g, v_w_in, v_b_in, v_w_pool, v_pool_scale, v_sgu_ln_g, v_sgu_ln_b, v_w_spatial, v_b_spatial, v_w_sgu_proj, v_w_out, v_norm1_post_g, v_norm2_pre_g, v_w_ff1, v_w_ff2, v_norm2_post_g):
    t_len = x.shape[1]
    row = lambda a: a.reshape(1, -1)
    x2 = x.reshape(t_len, D_MODEL)
    tgt2 = loss_target.reshape(t_len, D_MODEL)
    pg2 = lambda a: a.reshape(N_GROUPS * PG_SHARD, GROUP)

    g_in, g_pool, g_proj, g_out = _all_gather(
        [w_in.astype(BF16), pg2(w_pool).astype(BF16), pg2(w_sgu_proj).astype(BF16), w_out.astype(BF16)])
    regroup = lambda g: g.reshape(N_DEV, N_GROUPS, PG_SHARD, GROUP).transpose(1, 0, 2, 3).reshape(N_GROUPS, GROUP, GROUP)
    wpool_f, wproj_f = regroup(g_pool), regroup(g_proj)
    wout_f = g_out.reshape(D_MODEL, D_MODEL)
    bsp_t = b_spatial.T

    def forward(token):
        z, xn_b, y, h1 = _fwd_mix(token, x2, row(norm1_pre_g), g_in, row(b_in), wpool_f, row(pool_scale),
                                  row(sgu_ln_g), row(sgu_ln_b), w_spatial, bsp_t, wproj_f, wout_f, row(norm1_post_g))
        return h1, z, xn_b, y

    (g_ff1, g_ff2), (h1, z, xn_b, y) = _gather_behind([w_ff1.astype(BF16), w_ff2.astype(BF16)], g_in, forward)
    w2_f = g_ff2.reshape(D_FF, D_MODEL)
    hn_b, f_b, df1_b, df2_b, dh1, dg2post, dg2pre, loss_p = _mlp(h1, tgt2, row(norm2_pre_g), row(norm2_post_g),
                                                               g_ff1, w2_f)
    p_ff2 = _wgrad("wgrad_ff2", f_b, df2_b, FF_SHARD, True)
    p_ff1 = _wgrad("wgrad_ff1", hn_b, df1_b, FF_SHARD, False)
    c_arr = lax.axis_index("c").astype(jnp.int32).reshape(1)
    ff_parts = [p_ff1, p_ff2]
    got_ff = _pair_exchange("pair_exchange_ff", ff_parts, [])
    chip_ff = [_pair_sum_big("pair_sum_" + nm, c_arr, b, r) for nm, b, r in zip(("ff1", "ff2"), ff_parts, got_ff)]

    def backward_mix(token):
        return _bwd_mix(token, dh1, y, z, wpool_f, row(pool_scale), row(sgu_ln_g), row(sgu_ln_b), w_spatial, bsp_t,
                        wproj_f, wout_f, row(norm1_post_g))

    summed_ff, (dz_b, mg_b, dy_b, dwpool, dwproj, dws, dbsp_t, dg1post, dps, dlng, dlnb,
                dbin) = _chip_exchange_behind("ff", chip_ff, [], backward_mix)

    p_in = _wgrad("wgrad_in", xn_b, dz_b, IN_SHARD, False)
    bigs = [p_in, _slots_of_rows(dwpool), _slots_of_rows(dwproj)]
    rows = jnp.concatenate([jnp.broadcast_to(loss_p[:, 0:1], (1, D_MODEL)), dps, dlng, dlnb, dg1post, dg2pre, dg2post],
                           axis=0)
    smalls = [rows, dbin, dws.reshape(N_HEADS * SGU_BLOCK, SGU_BLOCK), dbsp_t.T]
    got = _pair_exchange("pair_exchange_in", bigs, smalls)
    chip_bigs = [_pair_sum_big("pair_sum_" + nm, c_arr, b, r) for nm, b, r in zip(("in", "pool", "proj"), bigs, got[:3])]
    chip_smalls = _pair_sum_small("pair_sum_small", smalls, got[3:])

    def backward_rest(token):
        dx, dg1pre = _bwd_in(token, dz_b, x2, dh1, row(norm1_pre_g), g_in)
        p_out = _wgrad("wgrad_out", mg_b, dy_b, OUT_SHARD, True)
        got_out = _pair_exchange("pair_exchange_out", [p_out], [dg1pre])
        chip_out = _pair_sum_big("pair_sum_out", c_arr, p_out, got_out[0])
        return dx, chip_out, _pair_sum_small("pair_sum_g1pre", [dg1pre], got_out[1:])[0]

    summed_in, (dx, chip_out, chip_g1pre) = _chip_exchange_behind("in", chip_bigs, chip_smalls, backward_rest)

    def update_large(token):
        res = [_adamw_big("adamw_" + nm, token, t, *p) for nm, t, p in
               (("in", summed_in[0], (w_in, m_w_in, v_w_in)), ("ff1", summed_ff[0], (w_ff1, m_w_ff1, v_w_ff1)),
                ("ff2", summed_ff[1], (w_ff2, m_w_ff2, v_w_ff2)))]
        return res[-1][0], res

    summed_out, (_, large) = _chip_exchange_behind("out", [chip_out], [chip_g1pre], update_large)
    big = {"in": large[0], "ff1": large[1], "ff2": large[2]}
    for nm, t, p in (("pool", summed_in[1], (pg2(w_pool), pg2(m_w_pool), pg2(v_w_pool))),
                     ("proj", summed_in[2], (pg2(w_sgu_proj), pg2(m_w_sgu_proj), pg2(v_w_sgu_proj))),
                     ("out", summed_out[0], (w_out, m_w_out, v_w_out))):
        big[nm] = _adamw_big("adamw_" + nm, t, t, *p)
    u_rows = jnp.concatenate([summed_in[3][:, 0:1], summed_out[1], summed_in[3][:, 1:]], axis=1)
    ws2 = lambda a: a.reshape(N_HEADS * SGU_BLOCK, SGU_BLOCK)
    small_params = [(row(norm1_pre_g), row(m_norm1_pre_g), row(v_norm1_pre_g)),
                    (row(pool_scale), row(m_pool_scale), row(v_pool_scale)),
                    (row(sgu_ln_g), row(m_sgu_ln_g), row(v_sgu_ln_g)),
                    (row(sgu_ln_b), row(m_sgu_ln_b), row(v_sgu_ln_b)),
                    (row(norm1_post_g), row(m_norm1_post_g), row(v_norm1_post_g)),
                    (row(norm2_pre_g), row(m_norm2_pre_g), row(v_norm2_pre_g)),
                    (row(norm2_post_g), row(m_norm2_post_g), row(v_norm2_post_g)),
                    (row(b_in), row(m_b_in), row(v_b_in)),
                    (ws2(w_spatial), ws2(m_w_spatial), ws2(v_w_spatial)),
                    (b_spatial, m_b_spatial, v_b_spatial)]
    small_out = _adamw_small(u_rows, *summed_in[4:], small_params)
    loss = small_out[0][0, 0]
    small_names = SMALL_ROWS[1:] + ("b_in", "w_spatial", "b_spatial")
    small = {nm: small_out[1 + 4 * p:5 + 4 * p] for p, nm in enumerate(small_names)}

    shapes = {"norm1_pre_g": norm1_pre_g.shape, "w_in": w_in.shape, "b_in": b_in.shape, "w_pool": w_pool.shape,
              "pool_scale": pool_scale.shape, "sgu_ln_g": sgu_ln_g.shape, "sgu_ln_b": sgu_ln_b.shape,
              "w_spatial": w_spatial.shape, "b_spatial": b_spatial.shape, "w_sgu_proj": w_sgu_proj.shape,
              "w_out": w_out.shape, "norm1_post_g": norm1_post_g.shape, "norm2_pre_g": norm2_pre_g.shape,
              "w_ff1": w_ff1.shape, "w_ff2": w_ff2.shape, "norm2_post_g": norm2_post_g.shape}
    source = {"w_in": big["in"], "w_pool": big["pool"], "w_sgu_proj": big["proj"], "w_out": big["out"],
              "w_ff1": big["ff1"], "w_ff2": big["ff2"], **small}
    order = list(shapes)
    outs = [loss, dx.reshape(x.shape)]
    for kind in range(4):
        outs += [source[nm][kind].reshape(shapes[nm]) for nm in order]
    return tuple(outs)
```

```python
import functools
import math

import jax
import jax.numpy as jnp
from jax import lax
from jax.experimental import pallas as pl
from jax.experimental.pallas import tpu as pltpu

F32, BF16 = jnp.float32, jnp.bfloat16
MESH = pl.DeviceIdType.MESH

D_MODEL = 1024
D_IN = 5120
D_FF = 4096
N_DEV = 8
N_CHIPS = 4
WINDOWS = (2, 4, 8, 16)
N_GROUPS = 4
GROUP = 256
HALO = 16
SGU_BLOCK = 128
N_HEADS = 4
HEAD = 256
CHUNK = 64
EPS = 1e-6
IN_SHARD = D_IN // N_DEV
FF_SHARD = D_FF // N_DEV
OUT_SHARD = D_MODEL // N_DEV
PG_SHARD = GROUP // N_DEV

ADAM_LR, ADAM_B1, ADAM_B2, ADAM_EPS, ADAM_WD, ADAM_STEP = 0.001, 0.9, 0.999, 1e-08, 0.01, 10

VMEM_LIMIT = 56 * 1024 * 1024
TM = 256
TM_BWD = 256
TM_IN = 512
GELU_C0 = math.sqrt(2.0 / math.pi)
GELU_C1 = 0.044715


def _dot(a, b):
    return jnp.dot(a, b, preferred_element_type=F32)


def _dot_nt(a, b):
    return lax.dot_general(a, b, (((1,), (1,)), ((), ())), preferred_element_type=F32)


def _dot_tn(a, b):
    return lax.dot_general(a, b, (((0,), (0,)), ((), ())), preferred_element_type=F32)


def _gelu(x):
    t = jnp.tanh(GELU_C0 * (x + GELU_C1 * (x * x * x)))
    return 0.5 * x * (1.0 + t), t


def _gelu_grad(x, t):
    return 0.5 * (1.0 + t) + 0.5 * x * (1.0 - t * t) * (GELU_C0 * (1.0 + 3.0 * GELU_C1 * (x * x)))


def _sigmoid(x):
    return 1.0 / (1.0 + jnp.exp(-x))


def _mean(x):
    return jnp.mean(x, axis=-1, keepdims=True)


def _colsum(x):
    return jnp.sum(x, axis=0, keepdims=True)


def _const_spec(shape):
    nd = len(shape)
    return pl.BlockSpec(shape, lambda *_: (0,) * nd, pipeline_mode=pl.Buffered(1))


def _acc_spec(shape):
    nd = len(shape)
    return pl.BlockSpec(shape, lambda *_: (0,) * nd)


def _masked_ws(ws_ref):
    ri = lax.broadcasted_iota(jnp.int32, (SGU_BLOCK, SGU_BLOCK), 0) // CHUNK
    ci = lax.broadcasted_iota(jnp.int32, (SGU_BLOCK, SGU_BLOCK), 1) // CHUNK
    return [jnp.where(ri >= ci, ws_ref[h], 0.0).astype(BF16) for h in range(N_HEADS)]


def _pool_fwd(pbuf, tile_idx, tm):
    pos = lax.broadcasted_iota(jnp.int32, (tm, 1), 0) + tile_idx * tm + 1
    pooled = []
    for g, w in enumerate(WINDOWS):
        e = pbuf[:, g * GROUP:(g + 1) * GROUP]
        s, sh = e, 1
        while sh < w:
            s = s + pltpu.roll(s, sh, 0)
            sh *= 2
        inv = 1.0 / jnp.minimum(pos, w).astype(F32)
        pooled.append(s[HALO:] * inv - e[HALO:])
    return pooled


def _sgu_fwd(zu, zv, lng, lnb, wsm, bsp_ref, tm):
    u, tu = _gelu(zu)
    gv, tv = _gelu(zv)
    xc = gv - _mean(gv)
    rln = lax.rsqrt(_mean(xc * xc) + EPS)
    xhat = xc * rln
    vb = (xhat * lng + lnb).astype(BF16)
    sv_heads = []
    for h in range(N_HEADS):
        rows = []
        for n in range(tm // SGU_BLOCK):
            blk = vb[n * SGU_BLOCK:(n + 1) * SGU_BLOCK, h * HEAD:(h + 1) * HEAD]
            rows.append(_dot(wsm[h], blk) + bsp_ref[:, h:h + 1])
        sv_heads.append(jnp.concatenate(rows, axis=0))
    return u, tu, tv, xhat, rln, vb, sv_heads


def _fwd_mix(after, x, g1pre, win_g, b_in, wpool, pool_scale, lng, lnb, ws, bsp_t, wproj, wout, g1post):
    t_len = x.shape[0]
    tm = TM
    nt = t_len // tm

    def body(after_ref, xa_ref, xb_ref, g1_ref, win_ref, bin_ref, wpool_ref, ps_ref, lng_ref, lnb_ref, ws_ref, bsp_ref,
             wproj_ref, wout_ref, g1post_ref, z_ref, xn_ref, y_ref, h1_ref, zcur, znext, pbuf):
        del after_ref
        s = pl.program_id(0)

        @pl.when(s == 0)
        def _():
            znext[...] = jnp.zeros((tm, D_IN), F32)
            pbuf[...] = jnp.zeros((tm + HALO, D_MODEL), F32)

        zcur[...] = znext[...]
        xv = xa_ref[...]
        r1 = lax.rsqrt(_mean(xv * xv) + EPS)
        xnb = (xv * r1 * g1_ref[...]).astype(BF16)
        xn_ref[...] = xnb.T

        def project(k):
            cols = slice(k * IN_SHARD, (k + 1) * IN_SHARD)
            zk = _dot(xnb, win_ref[k]) + bin_ref[:, cols]
            z_ref[:, cols] = zk
            znext[:, cols] = zk

        project(0)
        pbuf[0:HALO, :] = jnp.where(s <= 1, 0.0, pbuf[0:HALO, :])
        pbuf[HALO:, :] = zcur[:, 0:D_MODEL]
        pooled = _pool_fwd(pbuf, jnp.maximum(s - 1, 0), tm)
        pbuf[0:HALO, :] = pbuf[tm:tm + HALO, :]
        a = jnp.concatenate([_dot(pooled[g].astype(BF16), wpool_ref[g]) for g in range(N_GROUPS)], axis=1)
        a = a * ps_ref[...]
        project(1)
        u, _ = _gelu(zcur[:, D_MODEL:2 * D_MODEL])
        project(2)
        gv, _ = _gelu(zcur[:, 2 * D_MODEL:3 * D_MODEL])
        xc = gv - _mean(gv)
        vb = (xc * lax.rsqrt(_mean(xc * xc) + EPS) * lng_ref[...] + lnb_ref[...]).astype(BF16)
        wsm = _masked_ws(ws_ref)
        bbr = []
        for h in range(N_HEADS):
            project(3 + h)
            sv = jnp.concatenate(
                [_dot(wsm[h], vb[n * SGU_BLOCK:(n + 1) * SGU_BLOCK, h * HEAD:(h + 1) * HEAD]) + bsp_ref[:, h:h + 1]
                 for n in range(tm // SGU_BLOCK)], axis=0)
            bbr.append(_dot((u[:, h * HEAD:(h + 1) * HEAD] * sv).astype(BF16), wproj_ref[h]))
        bbr = jnp.concatenate(bbr, axis=1)
        project(7)
        sa = _sigmoid(zcur[:, 3 * D_MODEL:4 * D_MODEL])
        sb = _sigmoid(zcur[:, 4 * D_MODEL:5 * D_MODEL])
        yv = _dot((sa * a + sb * bbr).astype(BF16), wout_ref[...])
        y_ref[...] = yv
        ry = lax.rsqrt(_mean(yv * yv) + EPS)
        h1_ref[...] = xb_ref[...] + yv * ry * g1post_ref[...]

    proj = lambda w: pl.BlockSpec((tm, w), lambda s: (jnp.minimum(s, nt - 1), 0))
    mix = lambda w: pl.BlockSpec((tm, w), lambda s: (jnp.maximum(s - 1, 0), 0))
    return pl.pallas_call(
        body, name="fwd_mix", grid=(nt + 1,),
        in_specs=[_ANY, proj(D_MODEL), mix(D_MODEL), _const_spec((1, D_MODEL)),
                  _const_spec((N_DEV, D_MODEL, IN_SHARD)),
                  _const_spec((1, D_IN)), _const_spec((N_GROUPS, GROUP, GROUP)), _const_spec((1, D_MODEL)),
                  _const_spec((1, D_MODEL)), _const_spec((1, D_MODEL)),
                  _const_spec((N_HEADS, SGU_BLOCK, SGU_BLOCK)), _const_spec((SGU_BLOCK, N_HEADS)),
                  _const_spec((N_HEADS, HEAD, HEAD)), _const_spec((D_MODEL, D_MODEL)), _const_spec((1, D_MODEL))],
        out_specs=[proj(D_IN), pl.BlockSpec((D_MODEL, tm), lambda s: (0, jnp.minimum(s, nt - 1))), mix(D_MODEL),
                   mix(D_MODEL)],
        out_shape=[jax.ShapeDtypeStruct((t_len, D_IN), F32), jax.ShapeDtypeStruct((D_MODEL, t_len), BF16),
                   jax.ShapeDtypeStruct((t_len, D_MODEL), F32), jax.ShapeDtypeStruct((t_len, D_MODEL), F32)],
        scratch_shapes=[pltpu.VMEM((tm, D_IN), F32), pltpu.VMEM((tm, D_IN), F32),
                        pltpu.VMEM((tm + HALO, D_MODEL), F32)],
        compiler_params=pltpu.CompilerParams(dimension_semantics=("arbitrary",), vmem_limit_bytes=VMEM_LIMIT),
    )(after, x, x, g1pre, win_g, b_in, wpool, pool_scale, lng, lnb, ws, bsp_t, wproj, wout, g1post)


def _mlp(h1, target, g2pre, g2post, w1_g, w2):
    t_len = h1.shape[0]
    tm = TM
    nt = t_len // tm

    def body(h1_ref, tgt_ref, g2pre_ref, g2post_ref, w1_ref, w2_ref,
             hn_ref, f_ref, df1_ref, df2_ref, dh1_ref, dg2post_ref, dg2pre_ref, loss_ref, f1_scr):
        i = pl.program_id(0)

        @pl.when(i == 0)
        def _():
            dg2post_ref[...] = jnp.zeros_like(dg2post_ref)
            dg2pre_ref[...] = jnp.zeros_like(dg2pre_ref)
            loss_ref[...] = jnp.zeros_like(loss_ref)

        h = h1_ref[...]
        r2 = lax.rsqrt(_mean(h * h) + EPS)
        nh = h * r2
        hnb = (nh * g2pre_ref[...]).astype(BF16)
        hn_ref[...] = hnb.T
        for k in range(N_DEV):
            f1_scr[:, k * FF_SHARD:(k + 1) * FF_SHARD] = _dot(hnb, w1_ref[k])
        r = jnp.maximum(f1_scr[...], 0.0)
        fb = (r * r).astype(BF16)
        f_ref[...] = fb.T
        f2 = _dot(fb, w2_ref[...])
        rf = lax.rsqrt(_mean(f2 * f2) + EPS)
        nf = f2 * rf
        diff = h + nf * g2post_ref[...] - tgt_ref[...]
        loss_ref[...] += (0.5 / D_MODEL) * jnp.sum(diff * diff)
        dout = diff * (1.0 / D_MODEL)
        dg2post_ref[...] += _colsum(dout * nf)
        dn = dout * g2post_ref[...]
        df2b = (rf * (dn - nf * _mean(dn * nf))).astype(BF16)
        df2_ref[...] = df2b
        df = _dot_nt(df2b, w2_ref[...])
        df1b = (df * (2.0 * jnp.maximum(f1_scr[...], 0.0))).astype(BF16)
        df1_ref[...] = df1b
        dhn = _dot_nt(df1b[:, 0:FF_SHARD], w1_ref[0])
        for k in range(1, N_DEV):
            dhn = dhn + _dot_nt(df1b[:, k * FF_SHARD:(k + 1) * FF_SHARD], w1_ref[k])
        dg2pre_ref[...] += _colsum(dhn * nh)
        dnh = dhn * g2pre_ref[...]
        dh1_ref[...] = dout + r2 * (dnh - nh * _mean(dnh * nh))

    tok = lambda w: pl.BlockSpec((tm, w), lambda i: (i, 0))
    return pl.pallas_call(
        body, name="mlp_fwd_bwd", grid=(nt,),
        in_specs=[tok(D_MODEL), tok(D_MODEL), _const_spec((1, D_MODEL)), _const_spec((1, D_MODEL)),
                  _const_spec((N_DEV, D_MODEL, FF_SHARD)), _const_spec((D_FF, D_MODEL))],
        out_specs=[pl.BlockSpec((D_MODEL, tm), lambda i: (0, i)), pl.BlockSpec((D_FF, tm), lambda i: (0, i)),
                   tok(D_FF), tok(D_MODEL), tok(D_MODEL),
                   _acc_spec((1, D_MODEL)), _acc_spec((1, D_MODEL)), _acc_spec((1, 128))],
        out_shape=[jax.ShapeDtypeStruct((D_MODEL, t_len), BF16), jax.ShapeDtypeStruct((D_FF, t_len), BF16),
                   jax.ShapeDtypeStruct((t_len, D_FF), BF16), jax.ShapeDtypeStruct((t_len, D_MODEL), BF16),
                   jax.ShapeDtypeStruct((t_len, D_MODEL), F32), jax.ShapeDtypeStruct((1, D_MODEL), F32),
                   jax.ShapeDtypeStruct((1, D_MODEL), F32), jax.ShapeDtypeStruct((1, 128), F32)],
        scratch_shapes=[pltpu.VMEM((tm, D_FF), F32)],
        compiler_params=pltpu.CompilerParams(dimension_semantics=("arbitrary",), vmem_limit_bytes=VMEM_LIMIT),
    )(h1, target, g2pre, g2post, w1_g, w2)


def _bwd_mix(after, dh1, y, z, wpool, pool_scale, lng, lnb, ws, bsp_t, wproj, wout, g1post):
    t_len = y.shape[0]
    tm = TM_BWD
    nt = t_len // tm
    nblk = tm // SGU_BLOCK

    def body(after_ref, dh1_ref, y_ref, z_ref, zh_ref, wpool_ref, ps_ref, lng_ref, lnb_ref, ws_ref,
             bsp_ref, wproj_ref, wout_ref, g1post_ref,
             dz_ref, mg_ref, dy_ref, dwpool_ref, dwproj_ref, dws_ref, dbsp_ref, dg1post_ref, dps_ref,
             dlng_ref, dlnb_ref, dbin_ref, pbuf, qbuf):
        del after_ref
        i = pl.program_id(0)
        ti = nt - 1 - i

        @pl.when(i == 0)
        def _():
            for ref in (dwpool_ref, dwproj_ref, dws_ref, dbsp_ref, dg1post_ref, dps_ref, dlng_ref, dlnb_ref,
                        dbin_ref):
                ref[...] = jnp.zeros_like(ref)
            qbuf[tm:tm + HALO, :] = jnp.zeros((HALO, D_MODEL), F32)

        pbuf[0:HALO, :] = jnp.where(ti > 0, zh_ref[...], 0.0)
        pbuf[HALO:, :] = z_ref[:, 0:D_MODEL]
        pooled = _pool_fwd(pbuf, ti, tm)
        pooled_b = [p.astype(BF16) for p in pooled]
        a_raw = jnp.concatenate([_dot(pooled_b[g], wpool_ref[g]) for g in range(N_GROUPS)], axis=1)
        wsm = _masked_ws(ws_ref)
        zu = z_ref[:, D_MODEL:2 * D_MODEL]
        zv = z_ref[:, 2 * D_MODEL:3 * D_MODEL]
        u, tu, tv, xhat, rln, vb, sv_heads = _sgu_fwd(zu, zv, lng_ref[...], lnb_ref[...], wsm, bsp_ref, tm)
        gated_b = [(u[:, h * HEAD:(h + 1) * HEAD] * sv_heads[h]).astype(BF16) for h in range(N_HEADS)]
        bbr = jnp.concatenate([_dot(gated_b[h], wproj_ref[h]) for h in range(N_HEADS)], axis=1)
        sa = _sigmoid(z_ref[:, 3 * D_MODEL:4 * D_MODEL])
        sb = _sigmoid(z_ref[:, 4 * D_MODEL:5 * D_MODEL])
        a = a_raw * ps_ref[...]
        mg_ref[...] = (sa * a + sb * bbr).astype(BF16).T

        dh = dh1_ref[...]
        yv = y_ref[...]
        ry = lax.rsqrt(_mean(yv * yv) + EPS)
        ny = yv * ry
        dg1post_ref[...] += _colsum(dh * ny)
        dn = dh * g1post_ref[...]
        dyb = (ry * (dn - ny * _mean(dn * ny))).astype(BF16)
        dy_ref[...] = dyb
        dmg = _dot_nt(dyb, wout_ref[...])

        da = dmg * sa
        dbbr = dmg * sb
        dzga = dmg * a * sa * (1.0 - sa)
        dzgb = dmg * bbr * sb * (1.0 - sb)
        dz_ref[:, 3 * D_MODEL:4 * D_MODEL] = dzga.astype(BF16)
        dz_ref[:, 4 * D_MODEL:5 * D_MODEL] = dzgb.astype(BF16)
        dbin_ref[:, 3 * D_MODEL:4 * D_MODEL] += _colsum(dzga)
        dbin_ref[:, 4 * D_MODEL:5 * D_MODEL] += _colsum(dzgb)

        dps_ref[...] += _colsum(da * a_raw)
        da_raw_b = (da * ps_ref[...]).astype(BF16)
        pos = lax.broadcasted_iota(jnp.int32, (tm, 1), 0) + ti * tm + 1
        dpooled = []
        for g, w in enumerate(WINDOWS):
            cols = slice(g * GROUP, (g + 1) * GROUP)
            dwpool_ref[g] += _dot_tn(pooled_b[g], da_raw_b[:, cols])
            dp = _dot_nt(da_raw_b[:, cols], wpool_ref[g])
            dpooled.append(dp)
            qbuf[0:tm, cols] = dp * (1.0 / jnp.minimum(pos, w).astype(F32))
        n_ext = tm + HALO
        dzp = []
        for g, w in enumerate(WINDOWS):
            e = qbuf[:, g * GROUP:(g + 1) * GROUP]
            s, sh = e, 1
            while sh < w:
                s = s + pltpu.roll(s, n_ext - sh, 0)
                sh *= 2
            dzp.append(s[0:tm] - dpooled[g])
        qbuf[tm:tm + HALO, :] = qbuf[0:HALO, :]
        dzp = jnp.concatenate(dzp, axis=1)
        dz_ref[:, 0:D_MODEL] = dzp.astype(BF16)
        dbin_ref[:, 0:D_MODEL] += _colsum(dzp)

        dv_heads = []
        du_heads = []
        for h in range(N_HEADS):
            cols = slice(h * HEAD, (h + 1) * HEAD)
            dbbr_b = dbbr[:, cols].astype(BF16)
            dwproj_ref[h] += _dot_tn(gated_b[h], dbbr_b)
            dgated = _dot_nt(dbbr_b, wproj_ref[h])
            du_heads.append(dgated * sv_heads[h])
            dsv = dgated * u[:, cols]
            dsv_b = dsv.astype(BF16)
            rows = []
            for n in range(nblk):
                blk = slice(n * SGU_BLOCK, (n + 1) * SGU_BLOCK)
                rows.append(_dot_tn(wsm[h], dsv_b[blk]))
                dws_ref[h] += _dot_nt(dsv_b[blk], vb[blk, cols])
                dbsp_ref[:, h:h + 1] += jnp.sum(dsv[blk], axis=1, keepdims=True)
            dv_heads.append(jnp.concatenate(rows, axis=0))
        dzu = jnp.concatenate(du_heads, axis=1) * _gelu_grad(zu, tu)
        dz_ref[:, D_MODEL:2 * D_MODEL] = dzu.astype(BF16)
        dbin_ref[:, D_MODEL:2 * D_MODEL] += _colsum(dzu)
        dv = jnp.concatenate(dv_heads, axis=1)
        dlng_ref[...] += _colsum(dv * xhat)
        dlnb_ref[...] += _colsum(dv)
        dxh = dv * lng_ref[...]
        dgv = rln * (dxh - _mean(dxh) - xhat * _mean(dxh * xhat))
        dzv = dgv * _gelu_grad(zv, tv)
        dz_ref[:, 2 * D_MODEL:3 * D_MODEL] = dzv.astype(BF16)
        dbin_ref[:, 2 * D_MODEL:3 * D_MODEL] += _colsum(dzv)

        @pl.when(i == nt - 1)
        def _():
            ri = lax.broadcasted_iota(jnp.int32, (SGU_BLOCK, SGU_BLOCK), 0) // CHUNK
            ci = lax.broadcasted_iota(jnp.int32, (SGU_BLOCK, SGU_BLOCK), 1) // CHUNK
            for h in range(N_HEADS):
                dws_ref[h] = jnp.where(ri >= ci, dws_ref[h], 0.0)

    tok = lambda w: pl.BlockSpec((tm, w), lambda i: (nt - 1 - i, 0))
    halo = pl.BlockSpec((HALO, D_MODEL), lambda i: (jnp.maximum((nt - 1 - i) * (tm // HALO) - 1, 0), 0))
    return pl.pallas_call(
        body, name="bwd_mix", grid=(nt,),
        in_specs=[_ANY, tok(D_MODEL), tok(D_MODEL), tok(D_IN), halo, _const_spec((N_GROUPS, GROUP, GROUP)),
                  _const_spec((1, D_MODEL)), _const_spec((1, D_MODEL)), _const_spec((1, D_MODEL)),
                  _const_spec((N_HEADS, SGU_BLOCK, SGU_BLOCK)), _const_spec((SGU_BLOCK, N_HEADS)),
                  _const_spec((N_HEADS, HEAD, HEAD)), _const_spec((D_MODEL, D_MODEL)), _const_spec((1, D_MODEL))],
        out_specs=[tok(D_IN), pl.BlockSpec((D_MODEL, tm), lambda i: (0, nt - 1 - i)), tok(D_MODEL),
                   _acc_spec((N_GROUPS, GROUP, GROUP)), _acc_spec((N_HEADS, HEAD, HEAD)),
                   _acc_spec((N_HEADS, SGU_BLOCK, SGU_BLOCK)), _acc_spec((SGU_BLOCK, N_HEADS)),
                   _acc_spec((1, D_MODEL)), _acc_spec((1, D_MODEL)), _acc_spec((1, D_MODEL)), _acc_spec((1, D_MODEL)),
                   _acc_spec((1, D_IN))],
        out_shape=[jax.ShapeDtypeStruct((t_len, D_IN), BF16),
                   jax.ShapeDtypeStruct((D_MODEL, t_len), BF16), jax.ShapeDtypeStruct((t_len, D_MODEL), BF16),
                   jax.ShapeDtypeStruct((N_GROUPS, GROUP, GROUP), F32), jax.ShapeDtypeStruct((N_HEADS, HEAD, HEAD), F32),
                   jax.ShapeDtypeStruct((N_HEADS, SGU_BLOCK, SGU_BLOCK), F32),
                   jax.ShapeDtypeStruct((SGU_BLOCK, N_HEADS), F32),
                   jax.ShapeDtypeStruct((1, D_MODEL), F32), jax.ShapeDtypeStruct((1, D_MODEL), F32),
                   jax.ShapeDtypeStruct((1, D_MODEL), F32), jax.ShapeDtypeStruct((1, D_MODEL), F32),
                   jax.ShapeDtypeStruct((1, D_IN), F32)],
        scratch_shapes=[pltpu.VMEM((tm + HALO, D_MODEL), F32), pltpu.VMEM((tm + HALO, D_MODEL), F32)],
        compiler_params=pltpu.CompilerParams(dimension_semantics=("arbitrary",), vmem_limit_bytes=VMEM_LIMIT),
    )(after, dh1, y, z, z, wpool, pool_scale, lng, lnb, ws, bsp_t, wproj, wout, g1post)


def _bwd_in(after, dz_b, x, dh1, g1pre, win_g):
    t_len = x.shape[0]
    tm = TM_IN
    nt = t_len // tm

    def body(after_ref, dz_ref, x_ref, dh1_ref, g1_ref, win_ref, dx_ref, dg1pre_ref):
        del after_ref

        @pl.when(pl.program_id(0) == 0)
        def _():
            dg1pre_ref[...] = jnp.zeros_like(dg1pre_ref)

        dxn = _dot_nt(dz_ref[:, 0:IN_SHARD], win_ref[0])
        for k in range(1, N_DEV):
            dxn = dxn + _dot_nt(dz_ref[:, k * IN_SHARD:(k + 1) * IN_SHARD], win_ref[k])
        xv = x_ref[...]
        r1 = lax.rsqrt(_mean(xv * xv) + EPS)
        nx = xv * r1
        dg1pre_ref[...] += _colsum(dxn * nx)
        dnx = dxn * g1_ref[...]
        dx_ref[...] = r1 * (dnx - nx * _mean(dnx * nx)) + dh1_ref[...]

    tok = lambda w: pl.BlockSpec((tm, w), lambda i: (i, 0))
    return pl.pallas_call(
        body, name="bwd_in", grid=(nt,),
        in_specs=[_ANY, tok(D_IN), tok(D_MODEL), tok(D_MODEL), _const_spec((1, D_MODEL)),
                  _const_spec((N_DEV, D_MODEL, IN_SHARD))],
        out_specs=[tok(D_MODEL), _acc_spec((1, D_MODEL))],
        out_shape=[jax.ShapeDtypeStruct((t_len, D_MODEL), F32), jax.ShapeDtypeStruct((1, D_MODEL), F32)],
        compiler_params=pltpu.CompilerParams(dimension_semantics=("arbitrary",), vmem_limit_bytes=VMEM_LIMIT),
    )(after, dz_b, x, dh1, g1pre, win_g)


def _owner_of_slot(s):
    return 4 * ((s // 2) % 2) + 2 * (s % 2) + s // 4


def _wgrad(name, a_t, b, shard, a_sliced):
    t_len = b.shape[0]
    am = shard if a_sliced else a_t.shape[0]
    bn = b.shape[1] if a_sliced else shard

    def body(a_ref, b_ref, o_ref):
        o_ref[...] = _dot(a_ref[...], b_ref[...]).astype(BF16)

    if a_sliced:
        in_specs = [pl.BlockSpec((am, t_len), lambda s: (_owner_of_slot(s), 0)),
                    pl.BlockSpec((t_len, bn), lambda s: (0, 0), pipeline_mode=pl.Buffered(1))]
    else:
        in_specs = [pl.BlockSpec((am, t_len), lambda s: (0, 0), pipeline_mode=pl.Buffered(1)),
                    pl.BlockSpec((t_len, bn), lambda s: (0, _owner_of_slot(s)))]
    return pl.pallas_call(
        body, name=name, grid=(N_DEV,), in_specs=in_specs,
        out_specs=pl.BlockSpec((None, am, bn), lambda s: (s, 0, 0)),
        out_shape=jax.ShapeDtypeStruct((N_DEV, am, bn), BF16),
        compiler_params=pltpu.CompilerParams(dimension_semantics=("arbitrary",), vmem_limit_bytes=VMEM_LIMIT),
    )(a_t, b)


def _coords():
    return lax.axis_index("x"), lax.axis_index("y"), lax.axis_index("c")


_ANY = pl.BlockSpec(memory_space=pl.ANY)


def _all_gather(shards):
    n = len(shards)

    def body(*refs):
        src, dst = refs[:n], refs[n:2 * n]
        send_sems, recv_sems, local_sems = refs[2 * n:]
        x, y, c = _coords()
        me, sibling = (x, y, c), (x, y, 1 - c)
        chips = [(1 - x, y), (x, 1 - y), (1 - x, 1 - y)]

        def slot(j, dev):
            return dst[j].at[4 * dev[0] + 2 * dev[1] + dev[2]]

        def copy(j, k, block, to, from_shard=False):
            return pltpu.make_async_remote_copy(
                src_ref=src[j] if from_shard else slot(j, block), dst_ref=slot(j, block),
                send_sem=send_sems.at[j, k], recv_sem=recv_sems.at[j, k], device_id=to, device_id_type=MESH)

        started = []
        local = []
        for j in range(n):
            for k, to in enumerate([(*chip, c) for chip in chips]):
                cp = copy(j, 1 + k, me, to, from_shard=True)
                cp.start()
                started.append(cp)
        for j in range(n):
            cp = copy(j, 0, me, sibling, from_shard=True)
            cp.start()
            started.append(cp)
            lc = pltpu.make_async_copy(src[j], slot(j, me), local_sems.at[j])
            lc.start()
            local.append(lc)
        for j in range(n):
            for k, chip in enumerate(chips):
                copy(j, 1 + k, (*chip, c), me).wait_recv()
                cp = copy(j, 4 + k, (*chip, c), sibling)
                cp.start()
                started.append(cp)
        for j in range(n):
            copy(j, 0, sibling, me).wait_recv()
            for k, chip in enumerate(chips):
                copy(j, 4 + k, (*chip, 1 - c), me).wait_recv()
        for cp in started:
            cp.wait_send()
        for lc in local:
            lc.wait()

    return pl.pallas_call(
        body, name="all_gather_weights",
        in_specs=[_ANY] * n, out_specs=[_ANY] * n,
        out_shape=[jax.ShapeDtypeStruct((N_DEV,) + s.shape, s.dtype) for s in shards],
        scratch_shapes=[pltpu.SemaphoreType.DMA((n, 7)), pltpu.SemaphoreType.DMA((n, 7)),
                        pltpu.SemaphoreType.DMA((n,))],
    )(*shards)


def _pair_exchange(name, bigs, smalls):
    nb, n = len(bigs), len(bigs) + len(smalls)

    def body(*refs):
        src, dst = refs[:n], refs[n:2 * n]
        send_sems, recv_sems = refs[2 * n:]
        x, y, c = _coords()
        copies = []
        for j in range(n):
            s = src[j].at[pl.ds(4 * (1 - c), 4)] if j < nb else src[j]
            cp = pltpu.make_async_remote_copy(src_ref=s, dst_ref=dst[j], send_sem=send_sems.at[j],
                                              recv_sem=recv_sems.at[j], device_id=(x, y, 1 - c), device_id_type=MESH)
            cp.start()
            copies.append(cp)
        for cp in copies:
            cp.wait()

    return pl.pallas_call(
        body, name=name,
        in_specs=[_ANY] * n, out_specs=[_ANY] * n,
        out_shape=[jax.ShapeDtypeStruct((4,) + b.shape[1:], b.dtype) for b in bigs]
        + [jax.ShapeDtypeStruct(s.shape, s.dtype) for s in smalls],
        scratch_shapes=[pltpu.SemaphoreType.DMA((n,)), pltpu.SemaphoreType.DMA((n,))],
    )(*bigs, *smalls)


def _chip_exchange(bigs, smalls):
    nb, n = len(bigs), len(bigs) + len(smalls)

    def body(*refs):
        src, dst = refs[:n], refs[n:2 * n]
        send_sems, recv_sems, local_sems = refs[2 * n:]
        x, y, c = _coords()
        q_me = 2 * x + y
        peers = [(1 - x, y, c), (x, 1 - y, c), (1 - x, 1 - y, c)]

        def piece(j, q):
            return src[j].at[q] if j < nb else src[j]

        started = []
        for j in range(n):
            for r, peer in enumerate(peers):
                cp = pltpu.make_async_remote_copy(
                    src_ref=piece(j, 2 * peer[0] + peer[1]), dst_ref=dst[j].at[q_me],
                    send_sem=send_sems.at[j, r], recv_sem=recv_sems.at[j, r], device_id=peer, device_id_type=MESH)
                cp.start()
                started.append(cp)
        local = []
        for j in range(n):
            lc = pltpu.make_async_copy(piece(j, q_me), dst[j].at[q_me], local_sems.at[j])
            lc.start()
            local.append(lc)
        for j in range(n):
            for r, peer in enumerate(peers):
                q_peer = 2 * peer[0] + peer[1]
                pltpu.make_async_remote_copy(
                    src_ref=piece(j, q_peer), dst_ref=dst[j].at[q_peer],
                    send_sem=send_sems.at[j, r], recv_sem=recv_sems.at[j, r], device_id=peer,
                    device_id_type=MESH).wait_recv()
        for cp in started:
            cp.wait_send()
        for lc in local:
            lc.wait()

    return pl.pallas_call(
        body, name="chip_exchange",
        in_specs=[_ANY] * n, out_specs=[_ANY] * n,
        out_shape=[jax.ShapeDtypeStruct(b.shape, b.dtype) for b in bigs]
        + [jax.ShapeDtypeStruct((N_CHIPS,) + s.shape, s.dtype) for s in smalls],
        scratch_shapes=[pltpu.SemaphoreType.DMA((n, 3)), pltpu.SemaphoreType.DMA((n, 3)),
                        pltpu.SemaphoreType.DMA((n,))],
    )(*bigs, *smalls)


_HBM = pl.BlockSpec(memory_space=pltpu.HBM)
_SEM = pl.BlockSpec(memory_space=pltpu.SEMAPHORE)
_VMEM = pl.BlockSpec(memory_space=pltpu.VMEM)
_EFFECT = pltpu.SideEffectType.DATAFLOW_SIDE_EFFECTING
_TOKEN = jax.ShapeDtypeStruct((8, 128), F32)


def _in_hbm(a):
    return pltpu.with_memory_space_constraint(a, pltpu.HBM)


def _split_call(name, body, n_sems_out, arrays, sems_in=(), after=None):
    na, ns = len(arrays), len(sems_in)
    has_after = after is not None

    def kernel_body(*refs):
        arr = refs[:na]
        s_in = refs[na:na + ns]
        outs = refs[na + ns + has_after:]
        body(arr, s_in, outs[:n_sems_out])
        outs[-1][...] = jnp.zeros((8, 128), F32)

    out_shape = ([pltpu.SemaphoreType.DMA(())] * n_sems_out + [pltpu.HBM(a.shape, a.dtype) for a in arrays] + [_TOKEN])
    res = pl.pallas_call(
        kernel_body, name=name, out_shape=out_shape,
        in_specs=[_HBM] * na + [_SEM] * ns + [_ANY] * has_after,
        out_specs=[_SEM] * n_sems_out + [_HBM] * na + [_VMEM],
        input_output_aliases={i: n_sems_out + i for i in range(na)},
        compiler_params=pltpu.CompilerParams(has_side_effects=_EFFECT),
    )(*[_in_hbm(a) for a in arrays], *sems_in, *([after] if has_after else []))
    return list(res[:n_sems_out]), list(res[n_sems_out:n_sems_out + na]), res[-1]


def _wait_bytes_of(ref, send_sem, recv_sem, peer, send=True, recv=True):
    cp = pltpu.make_async_remote_copy(src_ref=ref, dst_ref=ref, send_sem=send_sem, recv_sem=recv_sem,
                                      device_id=peer, device_id_type=MESH)
    if send:
        cp.wait_send()
    if recv:
        cp.wait_recv()


def _gather_behind(shards, after, work):
    n = len(shards)
    lands = [lax.empty((N_DEV,) + s.shape, s.dtype) for s in shards]

    def own_slot(arr, j, sem):
        x, y, c = _coords()
        return pltpu.make_async_copy(arr[j], arr[n + j].at[4 * x + 2 * y + c], sem)

    def start(arr, _, sems):
        x, y, c = _coords()
        me = 4 * x + 2 * y + c
        for j in range(n):
            for chip in [(1 - x, y), (x, 1 - y), (1 - x, 1 - y)]:
                pltpu.make_async_remote_copy(src_ref=arr[j], dst_ref=arr[n + j].at[me], send_sem=sems[j],
                                             recv_sem=sems[n + j], device_id=(*chip, c), device_id_type=MESH).start()
        for j in range(n):
            pltpu.make_async_remote_copy(src_ref=arr[j], dst_ref=arr[n + j].at[me], send_sem=sems[2 * n + j],
                                         recv_sem=sems[3 * n + j], device_id=(x, y, 1 - c),
                                         device_id_type=MESH).start()
            own_slot(arr, j, sems[4 * n + j]).start()

    def middle(arr, s_in, sems):
        x, y, c = _coords()
        sibling = (x, y, 1 - c)
        for j in range(n):
            _wait_bytes_of(arr[n + j].at[pl.ds(0, 3)], s_in[j], s_in[n + j], sibling)
        for j in range(n):
            for chip in [(1 - x, y), (x, 1 - y), (1 - x, 1 - y)]:
                slot = arr[n + j].at[4 * chip[0] + 2 * chip[1] + c]
                pltpu.make_async_remote_copy(src_ref=slot, dst_ref=slot, send_sem=sems[j], recv_sem=sems[n + j],
                                             device_id=sibling, device_id_type=MESH).start()

    def finish(arr, s_in, _):
        x, y, c = _coords()
        sibling = (x, y, 1 - c)
        for j in range(n):
            _wait_bytes_of(arr[n + j].at[pl.ds(0, 1)], s_in[j], s_in[n + j], sibling)
            own_slot(arr, j, s_in[2 * n + j]).wait()
            _wait_bytes_of(arr[n + j].at[pl.ds(0, 3)], s_in[3 * n + j], s_in[4 * n + j], sibling)

    sems, arrays, token = _split_call("gather_ff_start", start, 5 * n, list(shards) + lands, after=after)
    result = work(token)
    fwd_sems, arrays, token = _split_call("gather_ff_middle", middle, 2 * n, arrays, sems_in=sems[:2 * n],
                                          after=result[0])
    _, arrays, _ = _split_call("gather_ff_finish", finish, 0, arrays, sems_in=sems[2 * n:] + fwd_sems, after=token)
    return arrays[n:], result


def _chip_exchange_behind(tag, bigs, smalls, work):
    nb, n = len(bigs), len(bigs) + len(smalls)
    lands = [lax.empty(s.shape, s.dtype) for s in bigs] + [lax.empty((N_CHIPS,) + s.shape, s.dtype) for s in smalls]

    def own_slot(arr, j, sem):
        x, y, _ = _coords()
        q_me = 2 * x + y
        return pltpu.make_async_copy(arr[j].at[q_me] if j < nb else arr[j], arr[n + j].at[q_me], sem)

    def start(arr, _, sems):
        x, y, c = _coords()
        q_me = 2 * x + y
        for j in range(n):
            for peer in [(1 - x, y, c), (x, 1 - y, c), (1 - x, 1 - y, c)]:
                piece = arr[j].at[2 * peer[0] + peer[1]] if j < nb else arr[j]
                pltpu.make_async_remote_copy(src_ref=piece, dst_ref=arr[n + j].at[q_me],
                                             send_sem=sems[j], recv_sem=sems[n + j], device_id=peer,
                                             device_id_type=MESH).start()
            own_slot(arr, j, sems[2 * n + j]).start()

    def finish(arr, s_in, _):
        x, y, c = _coords()
        for j in range(n):
            _wait_bytes_of(arr[n + j].at[pl.ds(0, 3)], s_in[j], s_in[n + j], (x, y, 1 - c))
            own_slot(arr, j, s_in[2 * n + j]).wait()

    sems, arrays, token = _split_call("chip_exchange_%s_start" % tag, start, 3 * n, list(bigs) + list(smalls) + lands)
    result = work(token)
    _, arrays, _ = _split_call("chip_exchange_%s_finish" % tag, finish, 0, arrays, sems_in=sems, after=result[0])
    return arrays[n:], result


def _pair_sum_big(name, c_arr, mine, theirs):
    _, rows, cols = theirs.shape
    tr = rows

    def body(c_ref, a_ref, b_ref, o_ref):
        del c_ref
        o_ref[...] = (a_ref[...].astype(F32) + b_ref[...].astype(F32)).astype(BF16)

    return pl.pallas_call(
        body, name=name,
        grid_spec=pltpu.PrefetchScalarGridSpec(
            num_scalar_prefetch=1, grid=(N_CHIPS, rows // tr),
            in_specs=[pl.BlockSpec((None, tr, cols), lambda q, r, c_ref: (4 * c_ref[0] + q, r, 0)),
                      pl.BlockSpec((None, tr, cols), lambda q, r, c_ref: (q, r, 0))],
            out_specs=pl.BlockSpec((None, tr, cols), lambda q, r, c_ref: (q, r, 0))),
        out_shape=jax.ShapeDtypeStruct(theirs.shape, BF16),
        compiler_params=pltpu.CompilerParams(dimension_semantics=("arbitrary", "arbitrary")),
    )(c_arr, mine, theirs)


def _pair_sum_small(name, mine, theirs):
    n = len(mine)

    def body(*refs):
        for j in range(n):
            refs[2 * n + j][...] = refs[j][...] + refs[n + j][...]

    return pl.pallas_call(
        body, name=name,
        out_shape=[jax.ShapeDtypeStruct(m.shape, m.dtype) for m in mine],
    )(*mine, *theirs)


def _adamw_math(w, g, m, v):
    m = ADAM_B1 * m + (1.0 - ADAM_B1) * g
    v = ADAM_B2 * v + (1.0 - ADAM_B2) * (g * g)
    m_hat = m / (1.0 - ADAM_B1 ** ADAM_STEP)
    v_hat = v / (1.0 - ADAM_B2 ** ADAM_STEP)
    delta = -ADAM_LR * (m_hat / (jnp.sqrt(v_hat) + ADAM_EPS) + ADAM_WD * w)
    return delta, m, v


def _adamw_big(name, after, chip_sums, w, m, v):
    rows, cols = w.shape
    tr = min(rows, 512)

    def body(after_ref, t_ref, w_ref, m_ref, v_ref, g_out, d_out, m_out, v_out):
        del after_ref
        g = t_ref[0].astype(F32)
        for q in range(1, N_CHIPS):
            g = g + t_ref[q].astype(F32)
        d, mn, vn = _adamw_math(w_ref[...], g, m_ref[...], v_ref[...])
        g_out[...] = g
        d_out[...] = d
        m_out[...] = mn
        v_out[...] = vn

    blk = pl.BlockSpec((tr, cols), lambda r: (r, 0))
    return pl.pallas_call(
        body, name=name, grid=(rows // tr,),
        in_specs=[_ANY, pl.BlockSpec((N_CHIPS, tr, cols), lambda r: (0, r, 0)), blk, blk, blk],
        out_specs=[blk] * 4,
        out_shape=[jax.ShapeDtypeStruct((rows, cols), F32)] * 4,
        compiler_params=pltpu.CompilerParams(dimension_semantics=("arbitrary",), vmem_limit_bytes=VMEM_LIMIT),
    )(after, chip_sums, w, m, v)


SMALL_ROWS = ("loss", "norm1_pre_g", "pool_scale", "sgu_ln_g", "sgu_ln_b", "norm1_post_g", "norm2_pre_g",
              "norm2_post_g")


def _adamw_small(u_rows, u_bin, u_ws, u_bs, params):
    n = len(params)

    def body(*refs):
        urow_ref, ubin_ref, uws_ref, ubs_ref = refs[:4]
        wmv = refs[4:4 + 3 * n]
        loss_ref = refs[4 + 3 * n]
        outs = refs[5 + 3 * n:]

        def total(ref, idx):
            g = ref[(0,) + idx]
            for q in range(1, N_CHIPS):
                g = g + ref[(q,) + idx]
            return g

        loss_ref[...] = total(urow_ref, (slice(0, 1), slice(None)))
        for p in range(n):
            if p < 7:
                g = total(urow_ref, (slice(p + 1, p + 2), slice(None)))
            else:
                g = total((ubin_ref, uws_ref, ubs_ref)[p - 7], (slice(None), slice(None)))
            d, mn, vn = _adamw_math(wmv[3 * p][...], g, wmv[3 * p + 1][...], wmv[3 * p + 2][...])
            outs[4 * p][...] = g
            outs[4 * p + 1][...] = d
            outs[4 * p + 2][...] = mn
            outs[4 * p + 3][...] = vn

    flat = [a for p in params for a in p]
    out_shape = [jax.ShapeDtypeStruct((1, D_MODEL), F32)]
    for w, _, _ in params:
        out_shape += [jax.ShapeDtypeStruct(w.shape, F32)] * 4
    return pl.pallas_call(body, name="adamw_small", out_shape=out_shape)(u_rows, u_bin, u_ws, u_bs, *flat)


def _slots_of_rows(full):
    owners = [_owner_of_slot(s) for s in range(N_DEV)]
    parts = full.reshape(N_GROUPS, N_DEV, PG_SHARD, GROUP)
    return jnp.stack([parts[:, o] for o in owners]).reshape(N_DEV, N_GROUPS * PG_SHARD, GROUP).astype(BF16)


def kernel(x, norm1_pre_g, w_in, b_in, w_pool, pool_scale, sgu_ln_g, sgu_ln_b, w_spatial, b_spatial, w_sgu_proj, w_out, norm1_post_g, norm2_pre_g, w_ff1, w_ff2, norm2_post_g, loss_target, m_norm1_pre_g, m_w_in, m_b_in, m_w_pool, m_pool_scale, m_sgu_ln_g, m_sgu_ln_b, m_w_spatial, m_b_spatial, m_w_sgu_proj, m_w_out, m_norm1_post_g, m_norm2_pre_g, m_w_ff1, m_w_ff2, m_norm2_post_g, v_norm1_pre_g, v_w_in, v_b_in, v_w_pool, v_pool_scale, v_sgu_ln_g, v_sgu_ln_b, v_w_spatial, v_b_spatial, v_w_sgu_proj, v_w_out, v_norm1_post_g, v_norm2_pre_g, v_w_ff1, v_w_ff2, v_norm2_post_g):
    t_len = x.shape[1]
    row = lambda a: a.reshape(1, -1)
    x2 = x.reshape(t_len, D_MODEL)
    tgt2 = loss_target.reshape(t_len, D_MODEL)
    pg2 = lambda a: a.reshape(N_GROUPS * PG_SHARD, GROUP)

    g_in, g_pool, g_proj, g_out = _all_gather(
        [w_in.astype(BF16), pg2(w_pool).astype(BF16), pg2(w_sgu_proj).astype(BF16), w_out.astype(BF16)])
    regroup = lambda g: g.reshape(N_DEV, N_GROUPS, PG_SHARD, GROUP).transpose(1, 0, 2, 3).reshape(N_GROUPS, GROUP, GROUP)
    wpool_f, wproj_f = regroup(g_pool), regroup(g_proj)
    wout_f = g_out.reshape(D_MODEL, D_MODEL)
    bsp_t = b_spatial.T

    def forward(token):
        z, xn_b, y, h1 = _fwd_mix(token, x2, row(norm1_pre_g), g_in, row(b_in), wpool_f, row(pool_scale),
                                  row(sgu_ln_g), row(sgu_ln_b), w_spatial, bsp_t, wproj_f, wout_f, row(norm1_post_g))
        return h1, z, xn_b, y

    (g_ff1, g_ff2), (h1, z, xn_b, y) = _gather_behind([w_ff1.astype(BF16), w_ff2.astype(BF16)], g_in, forward)
    w2_f = g_ff2.reshape(D_FF, D_MODEL)
    hn_b, f_b, df1_b, df2_b, dh1, dg2post, dg2pre, loss_p = _mlp(h1, tgt2, row(norm2_pre_g), row(norm2_post_g),
                                                               g_ff1, w2_f)
    p_ff2 = _wgrad("wgrad_ff2", f_b, df2_b, FF_SHARD, True)
    p_ff1 = _wgrad("wgrad_ff1", hn_b, df1_b, FF_SHARD, False)
    c_arr = lax.axis_index("c").astype(jnp.int32).reshape(1)
    ff_parts = [p_ff1, p_ff2]
    got_ff = _pair_exchange("pair_exchange_ff", ff_parts, [])
    chip_ff = [_pair_sum_big("pair_sum_" + nm, c_arr, b, r) for nm, b, r in zip(("ff1", "ff2"), ff_parts, got_ff)]

    def backward_mix(token):
        return _bwd_mix(token, dh1, y, z, wpool_f, row(pool_scale), row(sgu_ln_g), row(sgu_ln_b), w_spatial, bsp_t,
                        wproj_f, wout_f, row(norm1_post_g))

    summed_ff, (dz_b, mg_b, dy_b, dwpool, dwproj, dws, dbsp_t, dg1post, dps, dlng, dlnb,
                dbin) = _chip_exchange_behind("ff", chip_ff, [], backward_mix)

    p_in = _wgrad("wgrad_in", xn_b, dz_b, IN_SHARD, False)
    bigs = [p_in, _slots_of_rows(dwpool), _slots_of_rows(dwproj)]
    rows = jnp.concatenate([jnp.broadcast_to(loss_p[:, 0:1], (1, D_MODEL)), dps, dlng, dlnb, dg1post, dg2pre, dg2post],
                           axis=0)
    smalls = [rows, dbin, dws.reshape(N_HEADS * SGU_BLOCK, SGU_BLOCK), dbsp_t.T]
    got = _pair_exchange("pair_exchange_in", bigs, smalls)
    chip_bigs = [_pair_sum_big("pair_sum_" + nm, c_arr, b, r) for nm, b, r in zip(("in", "pool", "proj"), bigs, got[:3])]
    chip_smalls = _pair_sum_small("pair_sum_small", smalls, got[3:])

    def backward_rest(token):
        dx, dg1pre = _bwd_in(token, dz_b, x2, dh1, row(norm1_pre_g), g_in)
        p_out = _wgrad("wgrad_out", mg_b, dy_b, OUT_SHARD, True)
        got_out = _pair_exchange("pair_exchange_out", [p_out], [dg1pre])
        chip_out = _pair_sum_big("pair_sum_out", c_arr, p_out, got_out[0])
        return dx, chip_out, _pair_sum_small("pair_sum_g1pre", [dg1pre], got_out[1:])[0]

    summed_in, (dx, chip_out, chip_g1pre) = _chip_exchange_behind("in", chip_bigs, chip_smalls, backward_rest)

    def update_large(token):
        res = [_adamw_big("adamw_" + nm, token, t, *p) for nm, t, p in
               (("in", summed_in[0], (w_in, m_w_in, v_w_in)), ("ff1", summed_ff[0], (w_ff1, m_w_ff1, v_w_ff1)),
                ("ff2", summed_ff[1], (w_ff2, m_w_ff2, v_w_ff2)))]
        return res[-1][0], res

    summed_out, (_, large) = _chip_exchange_behind("out", [chip_out], [chip_g1pre], update_large)
    big = {"in": large[0], "ff1": large[1], "ff2": large[2]}
    for nm, t, p in (("pool", summed_in[1], (pg2(w_pool), pg2(m_w_pool), pg2(v_w_pool))),
                     ("proj", summed_in[2], (pg2(w_sgu_proj), pg2(m_w_sgu_proj), pg2(v_w_sgu_proj))),
                     ("out", summed_out[0], (w_out, m_w_out, v_w_out))):
        big[nm] = _adamw_big("adamw_" + nm, t, t, *p)
    u_rows = jnp.concatenate([summed_in[3][:, 0:1], summed_out[1], summed_in[3][:, 1:]], axis=1)
    ws2 = lambda a: a.reshape(N_HEADS * SGU_BLOCK, SGU_BLOCK)
    small_params = [(row(norm1_pre_g), row(m_norm1_pre_g), row(v_norm1_pre_g)),
                    (row(pool_scale), row(m_pool_scale), row(v_pool_scale)),
                    (row(sgu_ln_g), row(m_sgu_ln_g), row(v_sgu_ln_g)),
                    (row(sgu_ln_b), row(m_sgu_ln_b), row(v_sgu_ln_b)),
                    (row(norm1_post_g), row(m_norm1_post_g), row(v_norm1_post_g)),
                    (row(norm2_pre_g), row(m_norm2_pre_g), row(v_norm2_pre_g)),
                    (row(norm2_post_g), row(m_norm2_post_g), row(v_norm2_post_g)),
                    (row(b_in), row(m_b_in), row(v_b_in)),
                    (ws2(w_spatial), ws2(m_w_spatial), ws2(v_w_spatial)),
                    (b_spatial, m_b_spatial, v_b_spatial)]
    small_out = _adamw_small(u_rows, *summed_in[4:], small_params)
    loss = small_out[0][0, 0]
    small_names = SMALL_ROWS[1:] + ("b_in", "w_spatial", "b_spatial")
    small = {nm: small_out[1 + 4 * p:5 + 4 * p] for p, nm in enumerate(small_names)}

    shapes = {"norm1_pre_g": norm1_pre_g.shape, "w_in": w_in.shape, "b_in": b_in.shape, "w_pool": w_pool.shape,
              "pool_scale": pool_scale.shape, "sgu_ln_g": sgu_ln_g.shape, "sgu_ln_b": sgu_ln_b.shape,
              "w_spatial": w_spatial.shape, "b_spatial": b_spatial.shape, "w_sgu_proj": w_sgu_proj.shape,
              "w_out": w_out.shape, "norm1_post_g": norm1_post_g.shape, "norm2_pre_g": norm2_pre_g.shape,
              "w_ff1": w_ff1.shape, "w_ff2": w_ff2.shape, "norm2_post_g": norm2_post_g.shape}
    source = {"w_in": big["in"], "w_pool": big["pool"], "w_sgu_proj": big["proj"], "w_out": big["out"],
              "w_ff1": big["ff1"], "w_ff2": big["ff2"], **small}
    order = list(shapes)
    outs = [loss, dx.reshape(x.shape)]
    for kind in range(4):
        outs += [source[nm][kind].reshape(shapes[nm]) for nm in order]
    return tuple(outs)
```

```python
import functools
import math

import jax
import jax.numpy as jnp
from jax import lax
from jax.experimental import pallas as pl
from jax.experimental.pallas import tpu as pltpu

F32, BF16 = jnp.float32, jnp.bfloat16
MESH = pl.DeviceIdType.MESH

D_MODEL = 1024
D_IN = 5120
D_FF = 4096
N_DEV = 8
N_CHIPS = 4
WINDOWS = (2, 4, 8, 16)
N_GROUPS = 4
GROUP = 256
HALO = 16
SGU_BLOCK = 128
N_HEADS = 4
HEAD = 256
CHUNK = 64
EPS = 1e-6
IN_SHARD = D_IN // N_DEV
FF_SHARD = D_FF // N_DEV
OUT_SHARD = D_MODEL // N_DEV
PG_SHARD = GROUP // N_DEV

ADAM_LR, ADAM_B1, ADAM_B2, ADAM_EPS, ADAM_WD, ADAM_STEP = 0.001, 0.9, 0.999, 1e-08, 0.01, 10

VMEM_LIMIT = 56 * 1024 * 1024
TM = 256
TM_BWD = 256
TM_IN = 512
PROJ_COLS = 512
GELU_C0 = math.sqrt(2.0 / math.pi)
GELU_C1 = 0.044715


def _dot(a, b):
    return jnp.dot(a, b, preferred_element_type=F32)


def _dot_nt(a, b):
    return lax.dot_general(a, b, (((1,), (1,)), ((), ())), preferred_element_type=F32)


def _dot_tn(a, b):
    return lax.dot_general(a, b, (((0,), (0,)), ((), ())), preferred_element_type=F32)


def _gelu(x):
    t = jnp.tanh(GELU_C0 * (x + GELU_C1 * (x * x * x)))
    return 0.5 * x * (1.0 + t), t


def _gelu_grad(x, t):
    return 0.5 * (1.0 + t) + 0.5 * x * (1.0 - t * t) * (GELU_C0 * (1.0 + 3.0 * GELU_C1 * (x * x)))


def _sigmoid(x):
    return 1.0 / (1.0 + jnp.exp(-x))


def _mean(x):
    return jnp.mean(x, axis=-1, keepdims=True)


def _colsum(x):
    return jnp.sum(x, axis=0, keepdims=True)


def _const_spec(shape):
    nd = len(shape)
    return pl.BlockSpec(shape, lambda *_: (0,) * nd, pipeline_mode=pl.Buffered(1))


def _acc_spec(shape):
    nd = len(shape)
    return pl.BlockSpec(shape, lambda *_: (0,) * nd)


def _masked_ws(ws_ref):
    ri = lax.broadcasted_iota(jnp.int32, (SGU_BLOCK, SGU_BLOCK), 0) // CHUNK
    ci = lax.broadcasted_iota(jnp.int32, (SGU_BLOCK, SGU_BLOCK), 1) // CHUNK
    return [jnp.where(ri >= ci, ws_ref[h], 0.0).astype(BF16) for h in range(N_HEADS)]


def _pool_fwd(pbuf, tile_idx, tm):
    pos = lax.broadcasted_iota(jnp.int32, (tm, 1), 0) + tile_idx * tm + 1
    pooled = []
    for g, w in enumerate(WINDOWS):
        e = pbuf[:, g * GROUP:(g + 1) * GROUP]
        s, sh = e, 1
        while sh < w:
            s = s + pltpu.roll(s, sh, 0)
            sh *= 2
        inv = 1.0 / jnp.minimum(pos, w).astype(F32)
        pooled.append(s[HALO:] * inv - e[HALO:])
    return pooled


def _sgu_fwd(zu, zv, lng, lnb, wsm, bsp_ref, tm):
    u, tu = _gelu(zu)
    gv, tv = _gelu(zv)
    xc = gv - _mean(gv)
    rln = lax.rsqrt(_mean(xc * xc) + EPS)
    xhat = xc * rln
    vb = (xhat * lng + lnb).astype(BF16)
    sv_heads = []
    for h in range(N_HEADS):
        rows = []
        for n in range(tm // SGU_BLOCK):
            blk = vb[n * SGU_BLOCK:(n + 1) * SGU_BLOCK, h * HEAD:(h + 1) * HEAD]
            rows.append(_dot(wsm[h], blk) + bsp_ref[:, h:h + 1])
        sv_heads.append(jnp.concatenate(rows, axis=0))
    return u, tu, tv, xhat, rln, vb, sv_heads


def _fwd_mix(after, x, g1pre, win_g, b_in, wpool, pool_scale, lng, lnb, ws, bsp_t, wproj, wout, g1post):
    t_len = x.shape[0]
    tm = TM
    nt = t_len // tm

    def body(after_ref, xa_ref, xb_ref, g1_ref, win_ref, bin_ref, wpool_ref, ps_ref, lng_ref, lnb_ref, ws_ref, bsp_ref,
             wproj_ref, wout_ref, g1post_ref, z_ref, xn_ref, y_ref, h1_ref, zcur, znext, pbuf):
        del after_ref
        s = pl.program_id(0)

        @pl.when(s == 0)
        def _():
            znext[...] = jnp.zeros((tm, D_IN), F32)
            pbuf[...] = jnp.zeros((tm + HALO, D_MODEL), F32)

        zcur[...] = znext[...]
        xv = xa_ref[...]
        r1 = lax.rsqrt(_mean(xv * xv) + EPS)
        xnb = (xv * r1 * g1_ref[...]).astype(BF16)
        xn_ref[...] = xnb.T

        def project(p):
            cols = slice(p * PROJ_COLS, (p + 1) * PROJ_COLS)
            zp = _dot(xnb, win_ref[:, cols]) + bin_ref[:, cols]
            z_ref[:, cols] = zp
            znext[:, cols] = zp

        project(0)
        pbuf[0:HALO, :] = jnp.where(s <= 1, 0.0, pbuf[0:HALO, :])
        pbuf[HALO:, :] = zcur[:, 0:D_MODEL]
        pooled = _pool_fwd(pbuf, jnp.maximum(s - 1, 0), tm)
        pbuf[0:HALO, :] = pbuf[tm:tm + HALO, :]
        a = jnp.concatenate([_dot(pooled[g].astype(BF16), wpool_ref[g]) for g in range(N_GROUPS)], axis=1)
        a = a * ps_ref[...]
        project(1)
        u, _ = _gelu(zcur[:, D_MODEL:2 * D_MODEL])
        project(2)
        gv, _ = _gelu(zcur[:, 2 * D_MODEL:3 * D_MODEL])
        xc = gv - _mean(gv)
        vb = (xc * lax.rsqrt(_mean(xc * xc) + EPS) * lng_ref[...] + lnb_ref[...]).astype(BF16)
        wsm = _masked_ws(ws_ref)
        bbr = []
        for h in range(N_HEADS):
            project(3 + h)
            sv = jnp.concatenate(
                [_dot(wsm[h], vb[n * SGU_BLOCK:(n + 1) * SGU_BLOCK, h * HEAD:(h + 1) * HEAD]) + bsp_ref[:, h:h + 1]
                 for n in range(tm // SGU_BLOCK)], axis=0)
            bbr.append(_dot((u[:, h * HEAD:(h + 1) * HEAD] * sv).astype(BF16), wproj_ref[h]))
        bbr = jnp.concatenate(bbr, axis=1)
        project(7)
        sa = _sigmoid(zcur[:, 3 * D_MODEL:4 * D_MODEL])
        sb = _sigmoid(zcur[:, 4 * D_MODEL:5 * D_MODEL])
        project(8)
        yv = _dot((sa * a + sb * bbr).astype(BF16), wout_ref[...])
        y_ref[...] = yv
        project(9)
        ry = lax.rsqrt(_mean(yv * yv) + EPS)
        h1_ref[...] = xb_ref[...] + yv * ry * g1post_ref[...]

    proj = lambda w: pl.BlockSpec((tm, w), lambda s: (jnp.minimum(s, nt - 1), 0))
    mix = lambda w: pl.BlockSpec((tm, w), lambda s: (jnp.maximum(s - 1, 0), 0))
    return pl.pallas_call(
        body, name="fwd_mix", grid=(nt + 1,),
        in_specs=[_ANY, proj(D_MODEL), mix(D_MODEL), _const_spec((1, D_MODEL)),
                  _const_spec((D_MODEL, D_IN)),
                  _const_spec((1, D_IN)), _const_spec((N_GROUPS, GROUP, GROUP)), _const_spec((1, D_MODEL)),
                  _const_spec((1, D_MODEL)), _const_spec((1, D_MODEL)),
                  _const_spec((N_HEADS, SGU_BLOCK, SGU_BLOCK)), _const_spec((SGU_BLOCK, N_HEADS)),
                  _const_spec((N_HEADS, HEAD, HEAD)), _const_spec((D_MODEL, D_MODEL)), _const_spec((1, D_MODEL))],
        out_specs=[proj(D_IN), pl.BlockSpec((D_MODEL, tm), lambda s: (0, jnp.minimum(s, nt - 1))), mix(D_MODEL),
                   mix(D_MODEL)],
        out_shape=[jax.ShapeDtypeStruct((t_len, D_IN), F32), jax.ShapeDtypeStruct((D_MODEL, t_len), BF16),
                   jax.ShapeDtypeStruct((t_len, D_MODEL), F32), jax.ShapeDtypeStruct((t_len, D_MODEL), F32)],
        scratch_shapes=[pltpu.VMEM((tm, D_IN), F32), pltpu.VMEM((tm, D_IN), F32),
                        pltpu.VMEM((tm + HALO, D_MODEL), F32)],
        compiler_params=pltpu.CompilerParams(dimension_semantics=("arbitrary",), vmem_limit_bytes=VMEM_LIMIT),
    )(after, x, x, g1pre, win_g, b_in, wpool, pool_scale, lng, lnb, ws, bsp_t, wproj, wout, g1post)


def _mlp(h1, target, g2pre, g2post, w1_g, w2):
    t_len = h1.shape[0]
    tm = TM
    nt = t_len // tm

    def body(h1_ref, tgt_ref, g2pre_ref, g2post_ref, w1_ref, w2_ref,
             hn_ref, f_ref, df1_ref, df2_ref, dh1_ref, dg2post_ref, dg2pre_ref, loss_ref, f1_scr):
        i = pl.program_id(0)

        @pl.when(i == 0)
        def _():
            dg2post_ref[...] = jnp.zeros_like(dg2post_ref)
            dg2pre_ref[...] = jnp.zeros_like(dg2pre_ref)
            loss_ref[...] = jnp.zeros_like(loss_ref)

        h = h1_ref[...]
        r2 = lax.rsqrt(_mean(h * h) + EPS)
        nh = h * r2
        hnb = (nh * g2pre_ref[...]).astype(BF16)
        hn_ref[...] = hnb.T
        for k in range(N_DEV):
            f1_scr[:, k * FF_SHARD:(k + 1) * FF_SHARD] = _dot(hnb, w1_ref[k])
        r = jnp.maximum(f1_scr[...], 0.0)
        fb = (r * r).astype(BF16)
        f_ref[...] = fb.T
        f2 = _dot(fb, w2_ref[...])
        rf = lax.rsqrt(_mean(f2 * f2) + EPS)
        nf = f2 * rf
        diff = h + nf * g2post_ref[...] - tgt_ref[...]
        loss_ref[...] += (0.5 / D_MODEL) * jnp.sum(diff * diff)
        dout = diff * (1.0 / D_MODEL)
        dg2post_ref[...] += _colsum(dout * nf)
        dn = dout * g2post_ref[...]
        df2b = (rf * (dn - nf * _mean(dn * nf))).astype(BF16)
        df2_ref[...] = df2b
        df = _dot_nt(df2b, w2_ref[...])
        df1b = (df * (2.0 * jnp.maximum(f1_scr[...], 0.0))).astype(BF16)
        df1_ref[...] = df1b
        dhn = _dot_nt(df1b[:, 0:FF_SHARD], w1_ref[0])
        for k in range(1, N_DEV):
            dhn = dhn + _dot_nt(df1b[:, k * FF_SHARD:(k + 1) * FF_SHARD], w1_ref[k])
        dg2pre_ref[...] += _colsum(dhn * nh)
        dnh = dhn * g2pre_ref[...]
        dh1_ref[...] = dout + r2 * (dnh - nh * _mean(dnh * nh))

    tok = lambda w: pl.BlockSpec((tm, w), lambda i: (i, 0))
    return pl.pallas_call(
        body, name="mlp_fwd_bwd", grid=(nt,),
        in_specs=[tok(D_MODEL), tok(D_MODEL), _const_spec((1, D_MODEL)), _const_spec((1, D_MODEL)),
                  _const_spec((N_DEV, D_MODEL, FF_SHARD)), _const_spec((D_FF, D_MODEL))],
        out_specs=[pl.BlockSpec((D_MODEL, tm), lambda i: (0, i)), pl.BlockSpec((D_FF, tm), lambda i: (0, i)),
                   tok(D_FF), tok(D_MODEL), tok(D_MODEL),
                   _acc_spec((1, D_MODEL)), _acc_spec((1, D_MODEL)), _acc_spec((1, 128))],
        out_shape=[jax.ShapeDtypeStruct((D_MODEL, t_len), BF16), jax.ShapeDtypeStruct((D_FF, t_len), BF16),
                   jax.ShapeDtypeStruct((t_len, D_FF), BF16), jax.ShapeDtypeStruct((t_len, D_MODEL), BF16),
                   jax.ShapeDtypeStruct((t_len, D_MODEL), F32), jax.ShapeDtypeStruct((1, D_MODEL), F32),
                   jax.ShapeDtypeStruct((1, D_MODEL), F32), jax.ShapeDtypeStruct((1, 128), F32)],
        scratch_shapes=[pltpu.VMEM((tm, D_FF), F32)],
        compiler_params=pltpu.CompilerParams(dimension_semantics=("arbitrary",), vmem_limit_bytes=VMEM_LIMIT),
    )(h1, target, g2pre, g2post, w1_g, w2)


def _bwd_mix(after, dh1, y, z, wpool, pool_scale, lng, lnb, ws, bsp_t, wproj, wout, g1post):
    t_len = y.shape[0]
    tm = TM_BWD
    nt = t_len // tm
    nblk = tm // SGU_BLOCK

    def body(after_ref, dh1_ref, y_ref, z_ref, zh_ref, wpool_ref, ps_ref, lng_ref, lnb_ref, ws_ref,
             bsp_ref, wproj_ref, wout_ref, g1post_ref,
             dz_ref, mg_ref, dy_ref, dwpool_ref, dwproj_ref, dws_ref, dbsp_ref, dg1post_ref, dps_ref,
             dlng_ref, dlnb_ref, dbin_ref, pbuf, qbuf):
        del after_ref
        i = pl.program_id(0)
        ti = nt - 1 - i

        @pl.when(i == 0)
        def _():
            for ref in (dwpool_ref, dwproj_ref, dws_ref, dbsp_ref, dg1post_ref, dps_ref, dlng_ref, dlnb_ref,
                        dbin_ref):
                ref[...] = jnp.zeros_like(ref)
            qbuf[tm:tm + HALO, :] = jnp.zeros((HALO, D_MODEL), F32)

        pbuf[0:HALO, :] = jnp.where(ti > 0, zh_ref[...], 0.0)
        pbuf[HALO:, :] = z_ref[:, 0:D_MODEL]
        pooled = _pool_fwd(pbuf, ti, tm)
        pooled_b = [p.astype(BF16) for p in pooled]
        a_raw = jnp.concatenate([_dot(pooled_b[g], wpool_ref[g]) for g in range(N_GROUPS)], axis=1)
        wsm = _masked_ws(ws_ref)
        zu = z_ref[:, D_MODEL:2 * D_MODEL]
        zv = z_ref[:, 2 * D_MODEL:3 * D_MODEL]
        u, tu, tv, xhat, rln, vb, sv_heads = _sgu_fwd(zu, zv, lng_ref[...], lnb_ref[...], wsm, bsp_ref, tm)
        gated_b = [(u[:, h * HEAD:(h + 1) * HEAD] * sv_heads[h]).astype(BF16) for h in range(N_HEADS)]
        bbr = jnp.concatenate([_dot(gated_b[h], wproj_ref[h]) for h in range(N_HEADS)], axis=1)
        sa = _sigmoid(z_ref[:, 3 * D_MODEL:4 * D_MODEL])
        sb = _sigmoid(z_ref[:, 4 * D_MODEL:5 * D_MODEL])
        a = a_raw * ps_ref[...]
        mg_ref[...] = (sa * a + sb * bbr).astype(BF16).T

        dh = dh1_ref[...]
        yv = y_ref[...]
        ry = lax.rsqrt(_mean(yv * yv) + EPS)
        ny = yv * ry
        dg1post_ref[...] += _colsum(dh * ny)
        dn = dh * g1post_ref[...]
        dyb = (ry * (dn - ny * _mean(dn * ny))).astype(BF16)
        dy_ref[...] = dyb
        dmg = _dot_nt(dyb, wout_ref[...])

        da = dmg * sa
        dbbr = dmg * sb
        dzga = dmg * a * sa * (1.0 - sa)
        dzgb = dmg * bbr * sb * (1.0 - sb)
        dz_ref[:, 3 * D_MODEL:4 * D_MODEL] = dzga.astype(BF16)
        dz_ref[:, 4 * D_MODEL:5 * D_MODEL] = dzgb.astype(BF16)
        dbin_ref[:, 3 * D_MODEL:4 * D_MODEL] += _colsum(dzga)
        dbin_ref[:, 4 * D_MODEL:5 * D_MODEL] += _colsum(dzgb)

        dps_ref[...] += _colsum(da * a_raw)
        da_raw_b = (da * ps_ref[...]).astype(BF16)
        pos = lax.broadcasted_iota(jnp.int32, (tm, 1), 0) + ti * tm + 1
        dpooled = []
        for g, w in enumerate(WINDOWS):
            cols = slice(g * GROUP, (g + 1) * GROUP)
            dwpool_ref[g] += _dot_tn(pooled_b[g], da_raw_b[:, cols])
            dp = _dot_nt(da_raw_b[:, cols], wpool_ref[g])
            dpooled.append(dp)
            qbuf[0:tm, cols] = dp * (1.0 / jnp.minimum(pos, w).astype(F32))
        n_ext = tm + HALO
        dzp = []
        for g, w in enumerate(WINDOWS):
            e = qbuf[:, g * GROUP:(g + 1) * GROUP]
            s, sh = e, 1
            while sh < w:
                s = s + pltpu.roll(s, n_ext - sh, 0)
                sh *= 2
            dzp.append(s[0:tm] - dpooled[g])
        qbuf[tm:tm + HALO, :] = qbuf[0:HALO, :]
        dzp = jnp.concatenate(dzp, axis=1)
        dz_ref[:, 0:D_MODEL] = dzp.astype(BF16)
        dbin_ref[:, 0:D_MODEL] += _colsum(dzp)

        dv_heads = []
        du_heads = []
        for h in range(N_HEADS):
            cols = slice(h * HEAD, (h + 1) * HEAD)
            dbbr_b = dbbr[:, cols].astype(BF16)
            dwproj_ref[h] += _dot_tn(gated_b[h], dbbr_b)
            dgated = _dot_nt(dbbr_b, wproj_ref[h])
            du_heads.append(dgated * sv_heads[h])
            dsv = dgated * u[:, cols]
            dsv_b = dsv.astype(BF16)
            rows = []
            for n in range(nblk):
                blk = slice(n * SGU_BLOCK, (n + 1) * SGU_BLOCK)
                rows.append(_dot_tn(wsm[h], dsv_b[blk]))
                dws_ref[h] += _dot_nt(dsv_b[blk], vb[blk, cols])
                dbsp_ref[:, h:h + 1] += jnp.sum(dsv[blk], axis=1, keepdims=True)
            dv_heads.append(jnp.concatenate(rows, axis=0))
        dzu = jnp.concatenate(du_heads, axis=1) * _gelu_grad(zu, tu)
        dz_ref[:, D_MODEL:2 * D_MODEL] = dzu.astype(BF16)
        dbin_ref[:, D_MODEL:2 * D_MODEL] += _colsum(dzu)
        dv = jnp.concatenate(dv_heads, axis=1)
        dlng_ref[...] += _colsum(dv * xhat)
        dlnb_ref[...] += _colsum(dv)
        dxh = dv * lng_ref[...]
        dgv = rln * (dxh - _mean(dxh) - xhat * _mean(dxh * xhat))
        dzv = dgv * _gelu_grad(zv, tv)
        dz_ref[:, 2 * D_MODEL:3 * D_MODEL] = dzv.astype(BF16)
        dbin_ref[:, 2 * D_MODEL:3 * D_MODEL] += _colsum(dzv)

        @pl.when(i == nt - 1)
        def _():
            ri = lax.broadcasted_iota(jnp.int32, (SGU_BLOCK, SGU_BLOCK), 0) // CHUNK
            ci = lax.broadcasted_iota(jnp.int32, (SGU_BLOCK, SGU_BLOCK), 1) // CHUNK
            for h in range(N_HEADS):
                dws_ref[h] = jnp.where(ri >= ci, dws_ref[h], 0.0)

    tok = lambda w: pl.BlockSpec((tm, w), lambda i: (nt - 1 - i, 0))
    halo = pl.BlockSpec((HALO, D_MODEL), lambda i: (jnp.maximum((nt - 1 - i) * (tm // HALO) - 1, 0), 0))
    return pl.pallas_call(
        body, name="bwd_mix", grid=(nt,),
        in_specs=[_ANY, tok(D_MODEL), tok(D_MODEL), tok(D_IN), halo, _const_spec((N_GROUPS, GROUP, GROUP)),
                  _const_spec((1, D_MODEL)), _const_spec((1, D_MODEL)), _const_spec((1, D_MODEL)),
                  _const_spec((N_HEADS, SGU_BLOCK, SGU_BLOCK)), _const_spec((SGU_BLOCK, N_HEADS)),
                  _const_spec((N_HEADS, HEAD, HEAD)), _const_spec((D_MODEL, D_MODEL)), _const_spec((1, D_MODEL))],
        out_specs=[tok(D_IN), pl.BlockSpec((D_MODEL, tm), lambda i: (0, nt - 1 - i)), tok(D_MODEL),
                   _acc_spec((N_GROUPS, GROUP, GROUP)), _acc_spec((N_HEADS, HEAD, HEAD)),
                   _acc_spec((N_HEADS, SGU_BLOCK, SGU_BLOCK)), _acc_spec((SGU_BLOCK, N_HEADS)),
                   _acc_spec((1, D_MODEL)), _acc_spec((1, D_MODEL)), _acc_spec((1, D_MODEL)), _acc_spec((1, D_MODEL)),
                   _acc_spec((1, D_IN))],
        out_shape=[jax.ShapeDtypeStruct((t_len, D_IN), BF16),
                   jax.ShapeDtypeStruct((D_MODEL, t_len), BF16), jax.ShapeDtypeStruct((t_len, D_MODEL), BF16),
                   jax.ShapeDtypeStruct((N_GROUPS, GROUP, GROUP), F32), jax.ShapeDtypeStruct((N_HEADS, HEAD, HEAD), F32),
                   jax.ShapeDtypeStruct((N_HEADS, SGU_BLOCK, SGU_BLOCK), F32),
                   jax.ShapeDtypeStruct((SGU_BLOCK, N_HEADS), F32),
                   jax.ShapeDtypeStruct((1, D_MODEL), F32), jax.ShapeDtypeStruct((1, D_MODEL), F32),
                   jax.ShapeDtypeStruct((1, D_MODEL), F32), jax.ShapeDtypeStruct((1, D_MODEL), F32),
                   jax.ShapeDtypeStruct((1, D_IN), F32)],
        scratch_shapes=[pltpu.VMEM((tm + HALO, D_MODEL), F32), pltpu.VMEM((tm + HALO, D_MODEL), F32)],
        compiler_params=pltpu.CompilerParams(dimension_semantics=("arbitrary",), vmem_limit_bytes=VMEM_LIMIT),
    )(after, dh1, y, z, z, wpool, pool_scale, lng, lnb, ws, bsp_t, wproj, wout, g1post)


def _bwd_in(after, dz_b, x, dh1, g1pre, win_g):
    t_len = x.shape[0]
    tm = TM_IN
    nt = t_len // tm

    def body(after_ref, dz_ref, x_ref, dh1_ref, g1_ref, win_ref, dx_ref, dg1pre_ref):
        del after_ref

        @pl.when(pl.program_id(0) == 0)
        def _():
            dg1pre_ref[...] = jnp.zeros_like(dg1pre_ref)

        dxn = _dot_nt(dz_ref[...], win_ref[...])
        xv = x_ref[...]
        r1 = lax.rsqrt(_mean(xv * xv) + EPS)
        nx = xv * r1
        dg1pre_ref[...] += _colsum(dxn * nx)
        dnx = dxn * g1_ref[...]
        dx_ref[...] = r1 * (dnx - nx * _mean(dnx * nx)) + dh1_ref[...]

    tok = lambda w: pl.BlockSpec((tm, w), lambda i: (i, 0))
    return pl.pallas_call(
        body, name="bwd_in", grid=(nt,),
        in_specs=[_ANY, tok(D_IN), tok(D_MODEL), tok(D_MODEL), _const_spec((1, D_MODEL)),
                  _const_spec((D_MODEL, D_IN))],
        out_specs=[tok(D_MODEL), _acc_spec((1, D_MODEL))],
        out_shape=[jax.ShapeDtypeStruct((t_len, D_MODEL), F32), jax.ShapeDtypeStruct((1, D_MODEL), F32)],
        compiler_params=pltpu.CompilerParams(dimension_semantics=("arbitrary",), vmem_limit_bytes=VMEM_LIMIT),
    )(after, dz_b, x, dh1, g1pre, win_g)


def _owner_of_slot(s):
    return 4 * ((s // 2) % 2) + 2 * (s % 2) + s // 4


def _wgrad(name, a_t, b, shard, a_sliced):
    t_len = b.shape[0]
    am = shard if a_sliced else a_t.shape[0]
    bn = b.shape[1] if a_sliced else shard

    def body(a_ref, b_ref, o_ref):
        o_ref[...] = _dot(a_ref[...], b_ref[...]).astype(BF16)

    if a_sliced:
        in_specs = [pl.BlockSpec((am, t_len), lambda s: (_owner_of_slot(s), 0)),
                    pl.BlockSpec((t_len, bn), lambda s: (0, 0), pipeline_mode=pl.Buffered(1))]
    else:
        in_specs = [pl.BlockSpec((am, t_len), lambda s: (0, 0), pipeline_mode=pl.Buffered(1)),
                    pl.BlockSpec((t_len, bn), lambda s: (0, _owner_of_slot(s)))]
    return pl.pallas_call(
        body, name=name, grid=(N_DEV,), in_specs=in_specs,
        out_specs=pl.BlockSpec((None, am, bn), lambda s: (s, 0, 0)),
        out_shape=jax.ShapeDtypeStruct((N_DEV, am, bn), BF16),
        compiler_params=pltpu.CompilerParams(dimension_semantics=("arbitrary",), vmem_limit_bytes=VMEM_LIMIT),
    )(a_t, b)


def _wgrad_in(xn_t, dz_b):
    t_len = dz_b.shape[0]

    def body(a_ref, b_ref, o_ref):
        res = _dot(a_ref[...], b_ref[...])
        o_ref[0] = res[:, 0:IN_SHARD].astype(BF16)
        o_ref[1] = res[:, IN_SHARD:2 * IN_SHARD].astype(BF16)

    out = pl.pallas_call(
        body, name="wgrad_in", grid=(N_CHIPS,),
        in_specs=[pl.BlockSpec((D_MODEL, t_len), lambda q: (0, 0), pipeline_mode=pl.Buffered(1)),
                  pl.BlockSpec((t_len, 2 * IN_SHARD), lambda q: (0, q))],
        out_specs=pl.BlockSpec((2, None, D_MODEL, IN_SHARD), lambda q: (0, q, 0, 0)),
        out_shape=jax.ShapeDtypeStruct((2, N_CHIPS, D_MODEL, IN_SHARD), BF16),
        compiler_params=pltpu.CompilerParams(dimension_semantics=("arbitrary",), vmem_limit_bytes=VMEM_LIMIT),
    )(xn_t, dz_b)
    return out.reshape(N_DEV, D_MODEL, IN_SHARD)


def _coords():
    return lax.axis_index("x"), lax.axis_index("y"), lax.axis_index("c")


_ANY = pl.BlockSpec(memory_space=pl.ANY)


def _all_gather(shards):
    n = len(shards)

    def body(*refs):
        src, dst = refs[:n], refs[n:2 * n]
        send_sems, recv_sems, local_sems = refs[2 * n:]
        x, y, c = _coords()
        me, sibling = (x, y, c), (x, y, 1 - c)
        chips = [(1 - x, y), (x, 1 - y), (1 - x, 1 - y)]

        def slot(j, dev):
            return dst[j].at[4 * dev[0] + 2 * dev[1] + dev[2]]

        def copy(j, k, block, to, from_shard=False):
            return pltpu.make_async_remote_copy(
                src_ref=src[j] if from_shard else slot(j, block), dst_ref=slot(j, block),
                send_sem=send_sems.at[j, k], recv_sem=recv_sems.at[j, k], device_id=to, device_id_type=MESH)

        started = []
        local = []
        for j in range(n):
            for k, to in enumerate([(*chip, c) for chip in chips]):
                cp = copy(j, 1 + k, me, to, from_shard=True)
                cp.start()
                started.append(cp)
        for j in range(n):
            cp = copy(j, 0, me, sibling, from_shard=True)
            cp.start()
            started.append(cp)
            lc = pltpu.make_async_copy(src[j], slot(j, me), local_sems.at[j])
            lc.start()
            local.append(lc)
        for j in range(n):
            for k, chip in enumerate(chips):
                copy(j, 1 + k, (*chip, c), me).wait_recv()
                cp = copy(j, 4 + k, (*chip, c), sibling)
                cp.start()
                started.append(cp)
        for j in range(n):
            copy(j, 0, sibling, me).wait_recv()
            for k, chip in enumerate(chips):
                copy(j, 4 + k, (*chip, 1 - c), me).wait_recv()
        for cp in started:
            cp.wait_send()
        for lc in local:
            lc.wait()

    return pl.pallas_call(
        body, name="all_gather_weights",
        in_specs=[_ANY] * n, out_specs=[_ANY] * n,
        out_shape=[jax.ShapeDtypeStruct((N_DEV,) + s.shape, s.dtype) for s in shards],
        scratch_shapes=[pltpu.SemaphoreType.DMA((n, 7)), pltpu.SemaphoreType.DMA((n, 7)),
                        pltpu.SemaphoreType.DMA((n,))],
    )(*shards)


def _pair_exchange(name, bigs, smalls):
    nb, n = len(bigs), len(bigs) + len(smalls)

    def body(*refs):
        src, dst = refs[:n], refs[n:2 * n]
        send_sems, recv_sems = refs[2 * n:]
        x, y, c = _coords()
        copies = []
        for j in range(n):
            s = src[j].at[pl.ds(4 * (1 - c), 4)] if j < nb else src[j]
            cp = pltpu.make_async_remote_copy(src_ref=s, dst_ref=dst[j], send_sem=send_sems.at[j],
                                              recv_sem=recv_sems.at[j], device_id=(x, y, 1 - c), device_id_type=MESH)
            cp.start()
            copies.append(cp)
        for cp in copies:
            cp.wait()

    return pl.pallas_call(
        body, name=name,
        in_specs=[_ANY] * n, out_specs=[_ANY] * n,
        out_shape=[jax.ShapeDtypeStruct((4,) + b.shape[1:], b.dtype) for b in bigs]
        + [jax.ShapeDtypeStruct(s.shape, s.dtype) for s in smalls],
        scratch_shapes=[pltpu.SemaphoreType.DMA((n,)), pltpu.SemaphoreType.DMA((n,))],
    )(*bigs, *smalls)


def _chip_exchange(bigs, smalls):
    nb, n = len(bigs), len(bigs) + len(smalls)

    def body(*refs):
        src, dst = refs[:n], refs[n:2 * n]
        send_sems, recv_sems, local_sems = refs[2 * n:]
        x, y, c = _coords()
        q_me = 2 * x + y
        peers = [(1 - x, y, c), (x, 1 - y, c), (1 - x, 1 - y, c)]

        def piece(j, q):
            return src[j].at[q] if j < nb else src[j]

        started = []
        for j in range(n):
            for r, peer in enumerate(peers):
                cp = pltpu.make_async_remote_copy(
                    src_ref=piece(j, 2 * peer[0] + peer[1]), dst_ref=dst[j].at[q_me],
                    send_sem=send_sems.at[j, r], recv_sem=recv_sems.at[j, r], device_id=peer, device_id_type=MESH)
                cp.start()
                started.append(cp)
        local = []
        for j in range(n):
            lc = pltpu.make_async_copy(piece(j, q_me), dst[j].at[q_me], local_sems.at[j])
            lc.start()
            local.append(lc)
        for j in range(n):
            for r, peer in enumerate(peers):
                q_peer = 2 * peer[0] + peer[1]
                pltpu.make_async_remote_copy(
                    src_ref=piece(j, q_peer), dst_ref=dst[j].at[q_peer],
                    send_sem=send_sems.at[j, r], recv_sem=recv_sems.at[j, r], device_id=peer,
                    device_id_type=MESH).wait_recv()
        for cp in started:
            cp.wait_send()
        for lc in local:
            lc.wait()

    return pl.pallas_call(
        body, name="chip_exchange",
        in_specs=[_ANY] * n, out_specs=[_ANY] * n,
        out_shape=[jax.ShapeDtypeStruct(b.shape, b.dtype) for b in bigs]
        + [jax.ShapeDtypeStruct((N_CHIPS,) + s.shape, s.dtype) for s in smalls],
        scratch_shapes=[pltpu.SemaphoreType.DMA((n, 3)), pltpu.SemaphoreType.DMA((n, 3)),
                        pltpu.SemaphoreType.DMA((n,))],
    )(*bigs, *smalls)


_HBM = pl.BlockSpec(memory_space=pltpu.HBM)
_SEM = pl.BlockSpec(memory_space=pltpu.SEMAPHORE)
_VMEM = pl.BlockSpec(memory_space=pltpu.VMEM)
_EFFECT = pltpu.SideEffectType.DATAFLOW_SIDE_EFFECTING
_TOKEN = jax.ShapeDtypeStruct((8, 128), F32)


def _in_hbm(a):
    return pltpu.with_memory_space_constraint(a, pltpu.HBM)


def _split_call(name, body, n_sems_out, arrays, sems_in=(), after=None):
    na, ns = len(arrays), len(sems_in)
    has_after = after is not None

    def kernel_body(*refs):
        arr = refs[:na]
        s_in = refs[na:na + ns]
        outs = refs[na + ns + has_after:]
        body(arr, s_in, outs[:n_sems_out])
        outs[-1][...] = jnp.zeros((8, 128), F32)

    out_shape = ([pltpu.SemaphoreType.DMA(())] * n_sems_out + [pltpu.HBM(a.shape, a.dtype) for a in arrays] + [_TOKEN])
    res = pl.pallas_call(
        kernel_body, name=name, out_shape=out_shape,
        in_specs=[_HBM] * na + [_SEM] * ns + [_ANY] * has_after,
        out_specs=[_SEM] * n_sems_out + [_HBM] * na + [_VMEM],
        input_output_aliases={i: n_sems_out + i for i in range(na)},
        compiler_params=pltpu.CompilerParams(has_side_effects=_EFFECT),
    )(*[_in_hbm(a) for a in arrays], *sems_in, *([after] if has_after else []))
    return list(res[:n_sems_out]), list(res[n_sems_out:n_sems_out + na]), res[-1]


def _wait_bytes_of(ref, send_sem, recv_sem, peer, send=True, recv=True):
    cp = pltpu.make_async_remote_copy(src_ref=ref, dst_ref=ref, send_sem=send_sem, recv_sem=recv_sem,
                                      device_id=peer, device_id_type=MESH)
    if send:
        cp.wait_send()
    if recv:
        cp.wait_recv()


def _gather_behind(shards, after, work):
    n = len(shards)
    lands = [lax.empty((N_DEV,) + s.shape, s.dtype) for s in shards]

    def own_slot(arr, j, sem):
        x, y, c = _coords()
        return pltpu.make_async_copy(arr[j], arr[n + j].at[4 * x + 2 * y + c], sem)

    def start(arr, _, sems):
        x, y, c = _coords()
        me = 4 * x + 2 * y + c
        for j in range(n):
            for chip in [(1 - x, y), (x, 1 - y), (1 - x, 1 - y)]:
                pltpu.make_async_remote_copy(src_ref=arr[j], dst_ref=arr[n + j].at[me], send_sem=sems[j],
                                             recv_sem=sems[n + j], device_id=(*chip, c), device_id_type=MESH).start()
        for j in range(n):
            pltpu.make_async_remote_copy(src_ref=arr[j], dst_ref=arr[n + j].at[me], send_sem=sems[2 * n + j],
                                         recv_sem=sems[3 * n + j], device_id=(x, y, 1 - c),
                                         device_id_type=MESH).start()
            own_slot(arr, j, sems[4 * n + j]).start()

    def middle(arr, s_in, sems):
        x, y, c = _coords()
        sibling = (x, y, 1 - c)
        for j in range(n):
            _wait_bytes_of(arr[n + j].at[pl.ds(0, 3)], s_in[j], s_in[n + j], sibling)
        for j in range(n):
            for chip in [(1 - x, y), (x, 1 - y), (1 - x, 1 - y)]:
                slot = arr[n + j].at[4 * chip[0] + 2 * chip[1] + c]
                pltpu.make_async_remote_copy(src_ref=slot, dst_ref=slot, send_sem=sems[j], recv_sem=sems[n + j],
                                             device_id=sibling, device_id_type=MESH).start()

    def finish(arr, s_in, _):
        x, y, c = _coords()
        sibling = (x, y, 1 - c)
        for j in range(n):
            _wait_bytes_of(arr[n + j].at[pl.ds(0, 1)], s_in[j], s_in[n + j], sibling)
            own_slot(arr, j, s_in[2 * n + j]).wait()
            _wait_bytes_of(arr[n + j].at[pl.ds(0, 3)], s_in[3 * n + j], s_in[4 * n + j], sibling)

    sems, arrays, token = _split_call("gather_ff_start", start, 5 * n, list(shards) + lands, after=after)
    result = work(token)
    fwd_sems, arrays, token = _split_call("gather_ff_middle", middle, 2 * n, arrays, sems_in=sems[:2 * n],
                                          after=result[0])
    _, arrays, _ = _split_call("gather_ff_finish", finish, 0, arrays, sems_in=sems[2 * n:] + fwd_sems, after=token)
    return arrays[n:], result


def _chip_exchange_behind(tag, bigs, smalls, work):
    nb, n = len(bigs), len(bigs) + len(smalls)
    lands = [lax.empty(s.shape, s.dtype) for s in bigs] + [lax.empty((N_CHIPS,) + s.shape, s.dtype) for s in smalls]

    def own_slot(arr, j, sem):
        x, y, _ = _coords()
        q_me = 2 * x + y
        return pltpu.make_async_copy(arr[j].at[q_me] if j < nb else arr[j], arr[n + j].at[q_me], sem)

    def start(arr, _, sems):
        x, y, c = _coords()
        q_me = 2 * x + y
        for j in range(n):
            for peer in [(1 - x, y, c), (x, 1 - y, c), (1 - x, 1 - y, c)]:
                piece = arr[j].at[2 * peer[0] + peer[1]] if j < nb else arr[j]
                pltpu.make_async_remote_copy(src_ref=piece, dst_ref=arr[n + j].at[q_me],
                                             send_sem=sems[j], recv_sem=sems[n + j], device_id=peer,
                                             device_id_type=MESH).start()
            own_slot(arr, j, sems[2 * n + j]).start()

    def finish(arr, s_in, _):
        x, y, c = _coords()
        for j in range(n):
            _wait_bytes_of(arr[n + j].at[pl.ds(0, 3)], s_in[j], s_in[n + j], (x, y, 1 - c))
            own_slot(arr, j, s_in[2 * n + j]).wait()

    sems, arrays, token = _split_call("chip_exchange_%s_start" % tag, start, 3 * n, list(bigs) + list(smalls) + lands)
    result = work(token)
    _, arrays, _ = _split_call("chip_exchange_%s_finish" % tag, finish, 0, arrays, sems_in=sems, after=result[0])
    return arrays[n:], result


def _pair_sum_big(name, c_arr, mine, theirs):
    _, rows, cols = theirs.shape
    tr = rows

    def body(c_ref, a_ref, b_ref, o_ref):
        del c_ref
        o_ref[...] = (a_ref[...].astype(F32) + b_ref[...].astype(F32)).astype(BF16)

    return pl.pallas_call(
        body, name=name,
        grid_spec=pltpu.PrefetchScalarGridSpec(
            num_scalar_prefetch=1, grid=(N_CHIPS, rows // tr),
            in_specs=[pl.BlockSpec((None, tr, cols), lambda q, r, c_ref: (4 * c_ref[0] + q, r, 0)),
                      pl.BlockSpec((None, tr, cols), lambda q, r, c_ref: (q, r, 0))],
            out_specs=pl.BlockSpec((None, tr, cols), lambda q, r, c_ref: (q, r, 0))),
        out_shape=jax.ShapeDtypeStruct(theirs.shape, BF16),
        compiler_params=pltpu.CompilerParams(dimension_semantics=("arbitrary", "arbitrary")),
    )(c_arr, mine, theirs)


def _pair_sum_small(name, mine, theirs):
    n = len(mine)

    def body(*refs):
        for j in range(n):
            refs[2 * n + j][...] = refs[j][...] + refs[n + j][...]

    return pl.pallas_call(
        body, name=name,
        out_shape=[jax.ShapeDtypeStruct(m.shape, m.dtype) for m in mine],
    )(*mine, *theirs)


def _adamw_math(w, g, m, v):
    m = ADAM_B1 * m + (1.0 - ADAM_B1) * g
    v = ADAM_B2 * v + (1.0 - ADAM_B2) * (g * g)
    m_hat = m / (1.0 - ADAM_B1 ** ADAM_STEP)
    v_hat = v / (1.0 - ADAM_B2 ** ADAM_STEP)
    delta = -ADAM_LR * (m_hat / (jnp.sqrt(v_hat) + ADAM_EPS) + ADAM_WD * w)
    return delta, m, v


def _adamw_big(name, after, chip_sums, w, m, v):
    rows, cols = w.shape
    tr = min(rows, 512)

    def body(after_ref, t_ref, w_ref, m_ref, v_ref, g_out, d_out, m_out, v_out):
        del after_ref
        g = t_ref[0].astype(F32)
        for q in range(1, N_CHIPS):
            g = g + t_ref[q].astype(F32)
        d, mn, vn = _adamw_math(w_ref[...], g, m_ref[...], v_ref[...])
        g_out[...] = g
        d_out[...] = d
        m_out[...] = mn
        v_out[...] = vn

    blk = pl.BlockSpec((tr, cols), lambda r: (r, 0))
    return pl.pallas_call(
        body, name=name, grid=(rows // tr,),
        in_specs=[_ANY, pl.BlockSpec((N_CHIPS, tr, cols), lambda r: (0, r, 0)), blk, blk, blk],
        out_specs=[blk] * 4,
        out_shape=[jax.ShapeDtypeStruct((rows, cols), F32)] * 4,
        compiler_params=pltpu.CompilerParams(dimension_semantics=("arbitrary",), vmem_limit_bytes=VMEM_LIMIT),
    )(after, chip_sums, w, m, v)


SMALL_ROWS = ("loss", "norm1_pre_g", "pool_scale", "sgu_ln_g", "sgu_ln_b", "norm1_post_g", "norm2_pre_g",
              "norm2_post_g")


def _adamw_small(u_rows, u_bin, u_ws, u_bs, params):
    n = len(params)

    def body(*refs):
        urow_ref, ubin_ref, uws_ref, ubs_ref = refs[:4]
        wmv = refs[4:4 + 3 * n]
        loss_ref = refs[4 + 3 * n]
        outs = refs[5 + 3 * n:]

        def total(ref, idx):
            g = ref[(0,) + idx]
            for q in range(1, N_CHIPS):
                g = g + ref[(q,) + idx]
            return g

        loss_ref[...] = total(urow_ref, (slice(0, 1), slice(None)))
        for p in range(n):
            if p < 7:
                g = total(urow_ref, (slice(p + 1, p + 2), slice(None)))
            else:
                g = total((ubin_ref, uws_ref, ubs_ref)[p - 7], (slice(None), slice(None)))
            d, mn, vn = _adamw_math(wmv[3 * p][...], g, wmv[3 * p + 1][...], wmv[3 * p + 2][...])
            outs[4 * p][...] = g
            outs[4 * p + 1][...] = d
            outs[4 * p + 2][...] = mn
            outs[4 * p + 3][...] = vn

    flat = [a for p in params for a in p]
    out_shape = [jax.ShapeDtypeStruct((1, D_MODEL), F32)]
    for w, _, _ in params:
        out_shape += [jax.ShapeDtypeStruct(w.shape, F32)] * 4
    return pl.pallas_call(body, name="adamw_small", out_shape=out_shape)(u_rows, u_bin, u_ws, u_bs, *flat)


def _slots_of_rows(full):
    owners = [_owner_of_slot(s) for s in range(N_DEV)]
    parts = full.reshape(N_GROUPS, N_DEV, PG_SHARD, GROUP)
    return jnp.stack([parts[:, o] for o in owners]).reshape(N_DEV, N_GROUPS * PG_SHARD, GROUP).astype(BF16)


def kernel(x, norm1_pre_g, w_in, b_in, w_pool, pool_scale, sgu_ln_g, sgu_ln_b, w_spatial, b_spatial, w_sgu_proj, w_out, norm1_post_g, norm2_pre_g, w_ff1, w_ff2, norm2_post_g, loss_target, m_norm1_pre_g, m_w_in, m_b_in, m_w_pool, m_pool_scale, m_sgu_ln_g, m_sgu_ln_b, m_w_spatial, m_b_spatial, m_w_sgu_proj, m_w_out, m_norm1_post_g, m_norm2_pre_g, m_w_ff1, m_w_ff2, m_norm2_post_g, v_norm1_pre_g, v_w_in, v_b_in, v_w_pool, v_pool_scale, v_sgu_ln_g, v_sgu_ln_b, v_w_spatial, v_b_spatial, v_w_sgu_proj, v_w_out, v_norm1_post_g, v_norm2_pre_g, v_w_ff1, v_w_ff2, v_norm2_post_g):
    t_len = x.shape[1]
    row = lambda a: a.reshape(1, -1)
    x2 = x.reshape(t_len, D_MODEL)
    tgt2 = loss_target.reshape(t_len, D_MODEL)
    pg2 = lambda a: a.reshape(N_GROUPS * PG_SHARD, GROUP)

    g_in, g_pool, g_proj, g_out = _all_gather(
        [w_in.astype(BF16), pg2(w_pool).astype(BF16), pg2(w_sgu_proj).astype(BF16), w_out.astype(BF16)])
    regroup = lambda g: g.reshape(N_DEV, N_GROUPS, PG_SHARD, GROUP).transpose(1, 0, 2, 3).reshape(N_GROUPS, GROUP, GROUP)
    wpool_f, wproj_f = regroup(g_pool), regroup(g_proj)
    wout_f = g_out.reshape(D_MODEL, D_MODEL)
    win_f = g_in.transpose(1, 0, 2).reshape(D_MODEL, D_IN)
    bsp_t = b_spatial.T

    def forward(token):
        z, xn_b, y, h1 = _fwd_mix(token, x2, row(norm1_pre_g), win_f, row(b_in), wpool_f, row(pool_scale),
                                  row(sgu_ln_g), row(sgu_ln_b), w_spatial, bsp_t, wproj_f, wout_f, row(norm1_post_g))
        return h1, z, xn_b, y

    (g_ff1, g_ff2), (h1, z, xn_b, y) = _gather_behind([w_ff1.astype(BF16), w_ff2.astype(BF16)], g_in, forward)
    w2_f = g_ff2.reshape(D_FF, D_MODEL)
    hn_b, f_b, df1_b, df2_b, dh1, dg2post, dg2pre, loss_p = _mlp(h1, tgt2, row(norm2_pre_g), row(norm2_post_g),
                                                               g_ff1, w2_f)
    p_ff2 = _wgrad("wgrad_ff2", f_b, df2_b, FF_SHARD, True)
    p_ff1 = _wgrad("wgrad_ff1", hn_b, df1_b, FF_SHARD, False)
    c_arr = lax.axis_index("c").astype(jnp.int32).reshape(1)
    ff_parts = [p_ff1, p_ff2]
    got_ff = _pair_exchange("pair_exchange_ff", ff_parts, [])
    chip_ff = [_pair_sum_big("pair_sum_" + nm, c_arr, b, r) for nm, b, r in zip(("ff1", "ff2"), ff_parts, got_ff)]

    def backward_mix(token):
        return _bwd_mix(token, dh1, y, z, wpool_f, row(pool_scale), row(sgu_ln_g), row(sgu_ln_b), w_spatial, bsp_t,
                        wproj_f, wout_f, row(norm1_post_g))

    summed_ff, (dz_b, mg_b, dy_b, dwpool, dwproj, dws, dbsp_t, dg1post, dps, dlng, dlnb,
                dbin) = _chip_exchange_behind("ff", chip_ff, [], backward_mix)

    p_in = _wgrad_in(xn_b, dz_b)
    bigs = [p_in, _slots_of_rows(dwpool), _slots_of_rows(dwproj)]
    rows = jnp.concatenate([jnp.broadcast_to(loss_p[:, 0:1], (1, D_MODEL)), dps, dlng, dlnb, dg1post, dg2pre, dg2post],
                           axis=0)
    smalls = [rows, dbin, dws.reshape(N_HEADS * SGU_BLOCK, SGU_BLOCK), dbsp_t.T]
    got = _pair_exchange("pair_exchange_in", bigs, smalls)
    chip_bigs = [_pair_sum_big("pair_sum_" + nm, c_arr, b, r) for nm, b, r in zip(("in", "pool", "proj"), bigs, got[:3])]
    chip_smalls = _pair_sum_small("pair_sum_small", smalls, got[3:])

    def backward_rest(token):
        dx, dg1pre = _bwd_in(token, dz_b, x2, dh1, row(norm1_pre_g), win_f)
        p_out = _wgrad("wgrad_out", mg_b, dy_b, OUT_SHARD, True)
        got_out = _pair_exchange("pair_exchange_out", [p_out], [dg1pre])
        chip_out = _pair_sum_big("pair_sum_out", c_arr, p_out, got_out[0])
        return dx, chip_out, _pair_sum_small("pair_sum_g1pre", [dg1pre], got_out[1:])[0]

    summed_in, (dx, chip_out, chip_g1pre) = _chip_exchange_behind("in", chip_bigs, chip_smalls, backward_rest)

    def update_large(token):
        res = [_adamw_big("adamw_" + nm, token, t, *p) for nm, t, p in
               (("in", summed_in[0], (w_in, m_w_in, v_w_in)), ("ff1", summed_ff[0], (w_ff1, m_w_ff1, v_w_ff1)),
                ("ff2", summed_ff[1], (w_ff2, m_w_ff2, v_w_ff2)))]
        return res[-1][0], res

    summed_out, (_, large) = _chip_exchange_behind("out", [chip_out], [chip_g1pre], update_large)
    big = {"in": large[0], "ff1": large[1], "ff2": large[2]}
    for nm, t, p in (("pool", summed_in[1], (pg2(w_pool), pg2(m_w_pool), pg2(v_w_pool))),
                     ("proj", summed_in[2], (pg2(w_sgu_proj), pg2(m_w_sgu_proj), pg2(v_w_sgu_proj))),
                     ("out", summed_out[0], (w_out, m_w_out, v_w_out))):
        big[nm] = _adamw_big("adamw_" + nm, t, t, *p)
    u_rows = jnp.concatenate([summed_in[3][:, 0:1], summed_out[1], summed_in[3][:, 1:]], axis=1)
    ws2 = lambda a: a.reshape(N_HEADS * SGU_BLOCK, SGU_BLOCK)
    small_params = [(row(norm1_pre_g), row(m_norm1_pre_g), row(v_norm1_pre_g)),
                    (row(pool_scale), row(m_pool_scale), row(v_pool_scale)),
                    (row(sgu_ln_g), row(m_sgu_ln_g), row(v_sgu_ln_g)),
                    (row(sgu_ln_b), row(m_sgu_ln_b), row(v_sgu_ln_b)),
                    (row(norm1_post_g), row(m_norm1_post_g), row(v_norm1_post_g)),
                    (row(norm2_pre_g), row(m_norm2_pre_g), row(v_norm2_pre_g)),
                    (row(norm2_post_g), row(m_norm2_post_g), row(v_norm2_post_g)),
                    (row(b_in), row(m_b_in), row(v_b_in)),
                    (ws2(w_spatial), ws2(m_w_spatial), ws2(v_w_spatial)),
                    (b_spatial, m_b_spatial, v_b_spatial)]
    small_out = _adamw_small(u_rows, *summed_in[4:], small_params)
    loss = small_out[0][0, 0]
    small_names = SMALL_ROWS[1:] + ("b_in", "w_spatial", "b_spatial")
    small = {nm: small_out[1 + 4 * p:5 + 4 * p] for p, nm in enumerate(small_names)}

    shapes = {"norm1_pre_g": norm1_pre_g.shape, "w_in": w_in.shape, "b_in": b_in.shape, "w_pool": w_pool.shape,
              "pool_scale": pool_scale.shape, "sgu_ln_g": sgu_ln_g.shape, "sgu_ln_b": sgu_ln_b.shape,
              "w_spatial": w_spatial.shape, "b_spatial": b_spatial.shape, "w_sgu_proj": w_sgu_proj.shape,
              "w_out": w_out.shape, "norm1_post_g": norm1_post_g.shape, "norm2_pre_g": norm2_pre_g.shape,
              "w_ff1": w_ff1.shape, "w_ff2": w_ff2.shape, "norm2_post_g": norm2_post_g.shape}
    source = {"w_in": big["in"], "w_pool": big["pool"], "w_sgu_proj": big["proj"], "w_out": big["out"],
              "w_ff1": big["ff1"], "w_ff2": big["ff2"], **small}
    order = list(shapes)
    outs = [loss, dx.reshape(x.shape)]
    for kind in range(4):
        outs += [source[nm][kind].reshape(shapes[nm]) for nm in order]
    return tuple(outs)
```

```python
import functools
import math

import jax
import jax.numpy as jnp
from jax import lax
from jax.experimental import pallas as pl
from jax.experimental.pallas import tpu as pltpu

F32, BF16 = jnp.float32, jnp.bfloat16
MESH = pl.DeviceIdType.MESH

D_MODEL = 1024
D_IN = 5120
D_FF = 4096
N_DEV = 8
N_CHIPS = 4
WINDOWS = (2, 4, 8, 16)
N_GROUPS = 4
GROUP = 256
HALO = 16
SGU_BLOCK = 128
N_HEADS = 4
HEAD = 256
CHUNK = 64
EPS = 1e-6
IN_SHARD = D_IN // N_DEV
FF_SHARD = D_FF // N_DEV
OUT_SHARD = D_MODEL // N_DEV
PG_SHARD = GROUP // N_DEV

ADAM_LR, ADAM_B1, ADAM_B2, ADAM_EPS, ADAM_WD, ADAM_STEP = 0.001, 0.9, 0.999, 1e-08, 0.01, 10

VMEM_LIMIT = 56 * 1024 * 1024
TM = 256
TM_BWD = 256
TM_IN = 512
PROJ_COLS = 512
GELU_C0 = math.sqrt(2.0 / math.pi)
GELU_C1 = 0.044715


def _dot(a, b):
    return jnp.dot(a, b, preferred_element_type=F32)


def _dot_nt(a, b):
    return lax.dot_general(a, b, (((1,), (1,)), ((), ())), preferred_element_type=F32)


def _dot_tn(a, b):
    return lax.dot_general(a, b, (((0,), (0,)), ((), ())), preferred_element_type=F32)


def _gelu(x):
    t = jnp.tanh(GELU_C0 * (x + GELU_C1 * (x * x * x)))
    return 0.5 * x * (1.0 + t), t


def _gelu_grad(x, t):
    return 0.5 * (1.0 + t) + 0.5 * x * (1.0 - t * t) * (GELU_C0 * (1.0 + 3.0 * GELU_C1 * (x * x)))


def _sigmoid(x):
    return 1.0 / (1.0 + jnp.exp(-x))


def _mean(x):
    return jnp.mean(x, axis=-1, keepdims=True)


def _colsum(x):
    return jnp.sum(x, axis=0, keepdims=True)


def _const_spec(shape):
    nd = len(shape)
    return pl.BlockSpec(shape, lambda *_: (0,) * nd, pipeline_mode=pl.Buffered(1))


def _acc_spec(shape):
    nd = len(shape)
    return pl.BlockSpec(shape, lambda *_: (0,) * nd)


def _masked_ws(ws_ref):
    ri = lax.broadcasted_iota(jnp.int32, (SGU_BLOCK, SGU_BLOCK), 0) // CHUNK
    ci = lax.broadcasted_iota(jnp.int32, (SGU_BLOCK, SGU_BLOCK), 1) // CHUNK
    return [jnp.where(ri >= ci, ws_ref[h], 0.0).astype(BF16) for h in range(N_HEADS)]


def _pool_fwd(pbuf, tile_idx, tm):
    pos = lax.broadcasted_iota(jnp.int32, (tm, 1), 0) + tile_idx * tm + 1
    pooled = []
    for g, w in enumerate(WINDOWS):
        e = pbuf[:, g * GROUP:(g + 1) * GROUP]
        s, sh = e, 1
        while sh < w:
            s = s + pltpu.roll(s, sh, 0)
            sh *= 2
        inv = 1.0 / jnp.minimum(pos, w).astype(F32)
        pooled.append(s[HALO:] * inv - e[HALO:])
    return pooled


def _sgu_fwd(zu, zv, lng, lnb, wsm, bsp_ref, tm):
    u, tu = _gelu(zu)
    gv, tv = _gelu(zv)
    xc = gv - _mean(gv)
    rln = lax.rsqrt(_mean(xc * xc) + EPS)
    xhat = xc * rln
    vb = (xhat * lng + lnb).astype(BF16)
    sv_heads = []
    for h in range(N_HEADS):
        rows = []
        for n in range(tm // SGU_BLOCK):
            blk = vb[n * SGU_BLOCK:(n + 1) * SGU_BLOCK, h * HEAD:(h + 1) * HEAD]
            rows.append(_dot(wsm[h], blk) + bsp_ref[:, h:h + 1])
        sv_heads.append(jnp.concatenate(rows, axis=0))
    return u, tu, tv, xhat, rln, vb, sv_heads


def _fwd_mix(after, x, g1pre, win_g, b_in, wpool, pool_scale, lng, lnb, ws, bsp_t, wproj, wout, g1post):
    t_len = x.shape[0]
    tm = TM
    nt = t_len // tm

    def body(after_ref, xa_ref, xb_ref, g1_ref, win_ref, bin_ref, wpool_ref, ps_ref, lng_ref, lnb_ref, ws_ref, bsp_ref,
             wproj_ref, wout_ref, g1post_ref, z_ref, xn_ref, y_ref, h1_ref, zcur, znext, pbuf):
        del after_ref
        s = pl.program_id(0)

        @pl.when(s == 0)
        def _():
            znext[...] = jnp.zeros((tm, D_IN), F32)
            pbuf[...] = jnp.zeros((tm + HALO, D_MODEL), F32)

        zcur[...] = znext[...]
        xv = xa_ref[...]
        r1 = lax.rsqrt(_mean(xv * xv) + EPS)
        xnb = (xv * r1 * g1_ref[...]).astype(BF16)
        xn_ref[...] = xnb.T

        def project(p):
            cols = slice(p * PROJ_COLS, (p + 1) * PROJ_COLS)
            zp = _dot(xnb, win_ref[:, cols]) + bin_ref[:, cols]
            z_ref[:, cols] = zp
            znext[:, cols] = zp

        project(0)
        pbuf[0:HALO, :] = jnp.where(s <= 1, 0.0, pbuf[0:HALO, :])
        pbuf[HALO:, :] = zcur[:, 0:D_MODEL]
        pooled = _pool_fwd(pbuf, jnp.maximum(s - 1, 0), tm)
        pbuf[0:HALO, :] = pbuf[tm:tm + HALO, :]
        a = jnp.concatenate([_dot(pooled[g].astype(BF16), wpool_ref[g]) for g in range(N_GROUPS)], axis=1)
        a = a * ps_ref[...]
        project(1)
        u, _ = _gelu(zcur[:, D_MODEL:2 * D_MODEL])
        project(2)
        gv, _ = _gelu(zcur[:, 2 * D_MODEL:3 * D_MODEL])
        xc = gv - _mean(gv)
        vb = (xc * lax.rsqrt(_mean(xc * xc) + EPS) * lng_ref[...] + lnb_ref[...]).astype(BF16)
        wsm = _masked_ws(ws_ref)
        bbr = []
        for h in range(N_HEADS):
            project(3 + h)
            sv = jnp.concatenate(
                [_dot(wsm[h], vb[n * SGU_BLOCK:(n + 1) * SGU_BLOCK, h * HEAD:(h + 1) * HEAD]) + bsp_ref[:, h:h + 1]
                 for n in range(tm // SGU_BLOCK)], axis=0)
            bbr.append(_dot((u[:, h * HEAD:(h + 1) * HEAD] * sv).astype(BF16), wproj_ref[h]))
        bbr = jnp.concatenate(bbr, axis=1)
        project(7)
        sa = _sigmoid(zcur[:, 3 * D_MODEL:4 * D_MODEL])
        sb = _sigmoid(zcur[:, 4 * D_MODEL:5 * D_MODEL])
        project(8)
        yv = _dot((sa * a + sb * bbr).astype(BF16), wout_ref[...])
        y_ref[...] = yv
        project(9)
        ry = lax.rsqrt(_mean(yv * yv) + EPS)
        h1_ref[...] = xb_ref[...] + yv * ry * g1post_ref[...]

    proj = lambda w: pl.BlockSpec((tm, w), lambda s: (jnp.minimum(s, nt - 1), 0))
    mix = lambda w: pl.BlockSpec((tm, w), lambda s: (jnp.maximum(s - 1, 0), 0))
    return pl.pallas_call(
        body, name="fwd_mix", grid=(nt + 1,),
        in_specs=[_ANY, proj(D_MODEL), mix(D_MODEL), _const_spec((1, D_MODEL)),
                  _const_spec((D_MODEL, D_IN)),
                  _const_spec((1, D_IN)), _const_spec((N_GROUPS, GROUP, GROUP)), _const_spec((1, D_MODEL)),
                  _const_spec((1, D_MODEL)), _const_spec((1, D_MODEL)),
                  _const_spec((N_HEADS, SGU_BLOCK, SGU_BLOCK)), _const_spec((SGU_BLOCK, N_HEADS)),
                  _const_spec((N_HEADS, HEAD, HEAD)), _const_spec((D_MODEL, D_MODEL)), _const_spec((1, D_MODEL))],
        out_specs=[proj(D_IN), pl.BlockSpec((D_MODEL, tm), lambda s: (0, jnp.minimum(s, nt - 1))), mix(D_MODEL),
                   mix(D_MODEL)],
        out_shape=[jax.ShapeDtypeStruct((t_len, D_IN), F32), jax.ShapeDtypeStruct((D_MODEL, t_len), BF16),
                   jax.ShapeDtypeStruct((t_len, D_MODEL), F32), jax.ShapeDtypeStruct((t_len, D_MODEL), F32)],
        scratch_shapes=[pltpu.VMEM((tm, D_IN), F32), pltpu.VMEM((tm, D_IN), F32),
                        pltpu.VMEM((tm + HALO, D_MODEL), F32)],
        compiler_params=pltpu.CompilerParams(dimension_semantics=("arbitrary",), vmem_limit_bytes=VMEM_LIMIT),
    )(after, x, x, g1pre, win_g, b_in, wpool, pool_scale, lng, lnb, ws, bsp_t, wproj, wout, g1post)


def _mlp(h1, target, g2pre, g2post, w1_g, w2):
    t_len = h1.shape[0]
    tm = TM
    nt = t_len // tm

    def body(h1_ref, tgt_ref, g2pre_ref, g2post_ref, w1_ref, w2_ref,
             hn_ref, f_ref, df1_ref, df2_ref, dh1_ref, dg2post_ref, dg2pre_ref, loss_ref, f1_scr):
        i = pl.program_id(0)

        @pl.when(i == 0)
        def _():
            dg2post_ref[...] = jnp.zeros_like(dg2post_ref)
            dg2pre_ref[...] = jnp.zeros_like(dg2pre_ref)
            loss_ref[...] = jnp.zeros_like(loss_ref)

        h = h1_ref[...]
        r2 = lax.rsqrt(_mean(h * h) + EPS)
        nh = h * r2
        hnb = (nh * g2pre_ref[...]).astype(BF16)
        hn_ref[...] = hnb.T
        for k in range(N_DEV):
            f1_scr[:, k * FF_SHARD:(k + 1) * FF_SHARD] = _dot(hnb, w1_ref[k])
        r = jnp.maximum(f1_scr[...], 0.0)
        fb = (r * r).astype(BF16)
        f_ref[...] = fb.T
        f2 = _dot(fb, w2_ref[...])
        rf = lax.rsqrt(_mean(f2 * f2) + EPS)
        nf = f2 * rf
        diff = h + nf * g2post_ref[...] - tgt_ref[...]
        loss_ref[...] += (0.5 / D_MODEL) * jnp.sum(diff * diff)
        dout = diff * (1.0 / D_MODEL)
        dg2post_ref[...] += _colsum(dout * nf)
        dn = dout * g2post_ref[...]
        df2b = (rf * (dn - nf * _mean(dn * nf))).astype(BF16)
        df2_ref[...] = df2b
        df = _dot_nt(df2b, w2_ref[...])
        df1b = (df * (2.0 * jnp.maximum(f1_scr[...], 0.0))).astype(BF16)
        df1_ref[...] = df1b
        dhn = _dot_nt(df1b[:, 0:FF_SHARD], w1_ref[0])
        for k in range(1, N_DEV):
            dhn = dhn + _dot_nt(df1b[:, k * FF_SHARD:(k + 1) * FF_SHARD], w1_ref[k])
        dg2pre_ref[...] += _colsum(dhn * nh)
        dnh = dhn * g2pre_ref[...]
        dh1_ref[...] = dout + r2 * (dnh - nh * _mean(dnh * nh))

    tok = lambda w: pl.BlockSpec((tm, w), lambda i: (i, 0))
    return pl.pallas_call(
        body, name="mlp_fwd_bwd", grid=(nt,),
        in_specs=[tok(D_MODEL), tok(D_MODEL), _const_spec((1, D_MODEL)), _const_spec((1, D_MODEL)),
                  _const_spec((N_DEV, D_MODEL, FF_SHARD)), _const_spec((D_FF, D_MODEL))],
        out_specs=[pl.BlockSpec((D_MODEL, tm), lambda i: (0, i)), pl.BlockSpec((D_FF, tm), lambda i: (0, i)),
                   tok(D_FF), tok(D_MODEL), tok(D_MODEL),
                   _acc_spec((1, D_MODEL)), _acc_spec((1, D_MODEL)), _acc_spec((1, 128))],
        out_shape=[jax.ShapeDtypeStruct((D_MODEL, t_len), BF16), jax.ShapeDtypeStruct((D_FF, t_len), BF16),
                   jax.ShapeDtypeStruct((t_len, D_FF), BF16), jax.ShapeDtypeStruct((t_len, D_MODEL), BF16),
                   jax.ShapeDtypeStruct((t_len, D_MODEL), F32), jax.ShapeDtypeStruct((1, D_MODEL), F32),
                   jax.ShapeDtypeStruct((1, D_MODEL), F32), jax.ShapeDtypeStruct((1, 128), F32)],
        scratch_shapes=[pltpu.VMEM((tm, D_FF), F32)],
        compiler_params=pltpu.CompilerParams(dimension_semantics=("arbitrary",), vmem_limit_bytes=VMEM_LIMIT),
    )(h1, target, g2pre, g2post, w1_g, w2)


def _bwd_mix(after, dh1, y, z, wpool, pool_scale, lng, lnb, ws, bsp_t, wproj, wout, g1post):
    t_len = y.shape[0]
    tm = TM_BWD
    nt = t_len // tm
    nblk = tm // SGU_BLOCK

    def body(after_ref, dh1_ref, y_ref, z_ref, zh_ref, wpool_ref, ps_ref, lng_ref, lnb_ref, ws_ref,
             bsp_ref, wproj_ref, wout_ref, g1post_ref,
             dz_ref, mg_ref, dy_ref, dwpool_ref, dwproj_ref, dws_ref, dbsp_ref, dg1post_ref, dps_ref,
             dlng_ref, dlnb_ref, dbin_ref, pbuf, qbuf):
        del after_ref
        i = pl.program_id(0)
        ti = nt - 1 - i

        @pl.when(i == 0)
        def _():
            for ref in (dwpool_ref, dwproj_ref, dws_ref, dbsp_ref, dg1post_ref, dps_ref, dlng_ref, dlnb_ref,
                        dbin_ref):
                ref[...] = jnp.zeros_like(ref)
            qbuf[tm:tm + HALO, :] = jnp.zeros((HALO, D_MODEL), F32)

        pbuf[0:HALO, :] = jnp.where(ti > 0, zh_ref[...], 0.0)
        pbuf[HALO:, :] = z_ref[:, 0:D_MODEL]
        pooled = _pool_fwd(pbuf, ti, tm)
        pooled_b = [p.astype(BF16) for p in pooled]
        a_raw = jnp.concatenate([_dot(pooled_b[g], wpool_ref[g]) for g in range(N_GROUPS)], axis=1)
        wsm = _masked_ws(ws_ref)
        zu = z_ref[:, D_MODEL:2 * D_MODEL]
        zv = z_ref[:, 2 * D_MODEL:3 * D_MODEL]
        u, tu, tv, xhat, rln, vb, sv_heads = _sgu_fwd(zu, zv, lng_ref[...], lnb_ref[...], wsm, bsp_ref, tm)
        gated_b = [(u[:, h * HEAD:(h + 1) * HEAD] * sv_heads[h]).astype(BF16) for h in range(N_HEADS)]
        bbr = jnp.concatenate([_dot(gated_b[h], wproj_ref[h]) for h in range(N_HEADS)], axis=1)
        sa = _sigmoid(z_ref[:, 3 * D_MODEL:4 * D_MODEL])
        sb = _sigmoid(z_ref[:, 4 * D_MODEL:5 * D_MODEL])
        a = a_raw * ps_ref[...]
        mg_ref[...] = (sa * a + sb * bbr).astype(BF16).T

        dh = dh1_ref[...]
        yv = y_ref[...]
        ry = lax.rsqrt(_mean(yv * yv) + EPS)
        ny = yv * ry
        dg1post_ref[...] += _colsum(dh * ny)
        dn = dh * g1post_ref[...]
        dyb = (ry * (dn - ny * _mean(dn * ny))).astype(BF16)
        dy_ref[...] = dyb
        dmg = _dot_nt(dyb, wout_ref[...])

        da = dmg * sa
        dbbr = dmg * sb
        dzga = dmg * a * sa * (1.0 - sa)
        dzgb = dmg * bbr * sb * (1.0 - sb)
        dz_ref[:, 3 * D_MODEL:4 * D_MODEL] = dzga.astype(BF16)
        dz_ref[:, 4 * D_MODEL:5 * D_MODEL] = dzgb.astype(BF16)
        dbin_ref[:, 3 * D_MODEL:4 * D_MODEL] += _colsum(dzga)
        dbin_ref[:, 4 * D_MODEL:5 * D_MODEL] += _colsum(dzgb)

        dps_ref[...] += _colsum(da * a_raw)
        da_raw_b = (da * ps_ref[...]).astype(BF16)
        pos = lax.broadcasted_iota(jnp.int32, (tm, 1), 0) + ti * tm + 1
        dpooled = []
        for g, w in enumerate(WINDOWS):
            cols = slice(g * GROUP, (g + 1) * GROUP)
            dwpool_ref[g] += _dot_tn(pooled_b[g], da_raw_b[:, cols])
            dp = _dot_nt(da_raw_b[:, cols], wpool_ref[g])
            dpooled.append(dp)
            qbuf[0:tm, cols] = dp * (1.0 / jnp.minimum(pos, w).astype(F32))
        n_ext = tm + HALO
        dzp = []
        for g, w in enumerate(WINDOWS):
            e = qbuf[:, g * GROUP:(g + 1) * GROUP]
            s, sh = e, 1
            while sh < w:
                s = s + pltpu.roll(s, n_ext - sh, 0)
                sh *= 2
            dzp.append(s[0:tm] - dpooled[g])
        qbuf[tm:tm + HALO, :] = qbuf[0:HALO, :]
        dzp = jnp.concatenate(dzp, axis=1)
        dz_ref[:, 0:D_MODEL] = dzp.astype(BF16)
        dbin_ref[:, 0:D_MODEL] += _colsum(dzp)

        dv_heads = []
        du_heads = []
        for h in range(N_HEADS):
            cols = slice(h * HEAD, (h + 1) * HEAD)
            dbbr_b = dbbr[:, cols].astype(BF16)
            dwproj_ref[h] += _dot_tn(gated_b[h], dbbr_b)
            dgated = _dot_nt(dbbr_b, wproj_ref[h])
            du_heads.append(dgated * sv_heads[h])
            dsv = dgated * u[:, cols]
            dsv_b = dsv.astype(BF16)
            rows = []
            for n in range(nblk):
                blk = slice(n * SGU_BLOCK, (n + 1) * SGU_BLOCK)
                rows.append(_dot_tn(wsm[h], dsv_b[blk]))
                dws_ref[h] += _dot_nt(dsv_b[blk], vb[blk, cols])
                dbsp_ref[:, h:h + 1] += jnp.sum(dsv[blk], axis=1, keepdims=True)
            dv_heads.append(jnp.concatenate(rows, axis=0))
        dzu = jnp.concatenate(du_heads, axis=1) * _gelu_grad(zu, tu)
        dz_ref[:, D_MODEL:2 * D_MODEL] = dzu.astype(BF16)
        dbin_ref[:, D_MODEL:2 * D_MODEL] += _colsum(dzu)
        dv = jnp.concatenate(dv_heads, axis=1)
        dlng_ref[...] += _colsum(dv * xhat)
        dlnb_ref[...] += _colsum(dv)
        dxh = dv * lng_ref[...]
        dgv = rln * (dxh - _mean(dxh) - xhat * _mean(dxh * xhat))
        dzv = dgv * _gelu_grad(zv, tv)
        dz_ref[:, 2 * D_MODEL:3 * D_MODEL] = dzv.astype(BF16)
        dbin_ref[:, 2 * D_MODEL:3 * D_MODEL] += _colsum(dzv)

        @pl.when(i == nt - 1)
        def _():
            ri = lax.broadcasted_iota(jnp.int32, (SGU_BLOCK, SGU_BLOCK), 0) // CHUNK
            ci = lax.broadcasted_iota(jnp.int32, (SGU_BLOCK, SGU_BLOCK), 1) // CHUNK
            for h in range(N_HEADS):
                dws_ref[h] = jnp.where(ri >= ci, dws_ref[h], 0.0)

    tok = lambda w: pl.BlockSpec((tm, w), lambda i: (nt - 1 - i, 0))
    halo = pl.BlockSpec((HALO, D_MODEL), lambda i: (jnp.maximum((nt - 1 - i) * (tm // HALO) - 1, 0), 0))
    return pl.pallas_call(
        body, name="bwd_mix", grid=(nt,),
        in_specs=[_ANY, tok(D_MODEL), tok(D_MODEL), tok(D_IN), halo, _const_spec((N_GROUPS, GROUP, GROUP)),
                  _const_spec((1, D_MODEL)), _const_spec((1, D_MODEL)), _const_spec((1, D_MODEL)),
                  _const_spec((N_HEADS, SGU_BLOCK, SGU_BLOCK)), _const_spec((SGU_BLOCK, N_HEADS)),
                  _const_spec((N_HEADS, HEAD, HEAD)), _const_spec((D_MODEL, D_MODEL)), _const_spec((1, D_MODEL))],
        out_specs=[tok(D_IN), pl.BlockSpec((D_MODEL, tm), lambda i: (0, nt - 1 - i)), tok(D_MODEL),
                   _acc_spec((N_GROUPS, GROUP, GROUP)), _acc_spec((N_HEADS, HEAD, HEAD)),
                   _acc_spec((N_HEADS, SGU_BLOCK, SGU_BLOCK)), _acc_spec((SGU_BLOCK, N_HEADS)),
                   _acc_spec((1, D_MODEL)), _acc_spec((1, D_MODEL)), _acc_spec((1, D_MODEL)), _acc_spec((1, D_MODEL)),
                   _acc_spec((1, D_IN))],
        out_shape=[jax.ShapeDtypeStruct((t_len, D_IN), BF16),
                   jax.ShapeDtypeStruct((D_MODEL, t_len), BF16), jax.ShapeDtypeStruct((t_len, D_MODEL), BF16),
                   jax.ShapeDtypeStruct((N_GROUPS, GROUP, GROUP), F32), jax.ShapeDtypeStruct((N_HEADS, HEAD, HEAD), F32),
                   jax.ShapeDtypeStruct((N_HEADS, SGU_BLOCK, SGU_BLOCK), F32),
                   jax.ShapeDtypeStruct((SGU_BLOCK, N_HEADS), F32),
                   jax.ShapeDtypeStruct((1, D_MODEL), F32), jax.ShapeDtypeStruct((1, D_MODEL), F32),
                   jax.ShapeDtypeStruct((1, D_MODEL), F32), jax.ShapeDtypeStruct((1, D_MODEL), F32),
                   jax.ShapeDtypeStruct((1, D_IN), F32)],
        scratch_shapes=[pltpu.VMEM((tm + HALO, D_MODEL), F32), pltpu.VMEM((tm + HALO, D_MODEL), F32)],
        compiler_params=pltpu.CompilerParams(dimension_semantics=("arbitrary",), vmem_limit_bytes=VMEM_LIMIT),
    )(after, dh1, y, z, z, wpool, pool_scale, lng, lnb, ws, bsp_t, wproj, wout, g1post)


def _bwd_in(after, dz_b, x, dh1, g1pre, win_g):
    t_len = x.shape[0]
    tm = TM_IN
    nt = t_len // tm

    def body(after_ref, dz_ref, x_ref, dh1_ref, g1_ref, win_ref, dx_ref, dg1pre_ref):
        del after_ref

        @pl.when(pl.program_id(0) == 0)
        def _():
            dg1pre_ref[...] = jnp.zeros_like(dg1pre_ref)

        dxn = _dot_nt(dz_ref[...], win_ref[...])
        xv = x_ref[...]
        r1 = lax.rsqrt(_mean(xv * xv) + EPS)
        nx = xv * r1
        dg1pre_ref[...] += _colsum(dxn * nx)
        dnx = dxn * g1_ref[...]
        dx_ref[...] = r1 * (dnx - nx * _mean(dnx * nx)) + dh1_ref[...]

    tok = lambda w: pl.BlockSpec((tm, w), lambda i: (i, 0))
    return pl.pallas_call(
        body, name="bwd_in", grid=(nt,),
        in_specs=[_ANY, tok(D_IN), tok(D_MODEL), tok(D_MODEL), _const_spec((1, D_MODEL)),
                  _const_spec((D_MODEL, D_IN))],
        out_specs=[tok(D_MODEL), _acc_spec((1, D_MODEL))],
        out_shape=[jax.ShapeDtypeStruct((t_len, D_MODEL), F32), jax.ShapeDtypeStruct((1, D_MODEL), F32)],
        compiler_params=pltpu.CompilerParams(dimension_semantics=("arbitrary",), vmem_limit_bytes=VMEM_LIMIT),
    )(after, dz_b, x, dh1, g1pre, win_g)


def _owner_of_slot(s):
    return 4 * ((s // 2) % 2) + 2 * (s % 2) + s // 4


def _wgrad(name, a_t, b, shard, a_sliced):
    t_len = b.shape[0]
    am = shard if a_sliced else a_t.shape[0]
    bn = b.shape[1] if a_sliced else shard

    def body(a_ref, b_ref, o_ref):
        o_ref[...] = _dot(a_ref[...], b_ref[...]).astype(BF16)

    if a_sliced:
        in_specs = [pl.BlockSpec((am, t_len), lambda s: (_owner_of_slot(s), 0)),
                    pl.BlockSpec((t_len, bn), lambda s: (0, 0), pipeline_mode=pl.Buffered(1))]
    else:
        in_specs = [pl.BlockSpec((am, t_len), lambda s: (0, 0), pipeline_mode=pl.Buffered(1)),
                    pl.BlockSpec((t_len, bn), lambda s: (0, _owner_of_slot(s)))]
    return pl.pallas_call(
        body, name=name, grid=(N_DEV,), in_specs=in_specs,
        out_specs=pl.BlockSpec((None, am, bn), lambda s: (s, 0, 0)),
        out_shape=jax.ShapeDtypeStruct((N_DEV, am, bn), BF16),
        compiler_params=pltpu.CompilerParams(dimension_semantics=("arbitrary",), vmem_limit_bytes=VMEM_LIMIT),
    )(a_t, b)


def _wgrad_in(xn_t, dz_b):
    t_len = dz_b.shape[0]

    def body(a_ref, b_ref, o_ref):
        res = _dot(a_ref[...], b_ref[...])
        o_ref[0] = res[:, 0:IN_SHARD].astype(BF16)
        o_ref[1] = res[:, IN_SHARD:2 * IN_SHARD].astype(BF16)

    out = pl.pallas_call(
        body, name="wgrad_in", grid=(N_CHIPS,),
        in_specs=[pl.BlockSpec((D_MODEL, t_len), lambda q: (0, 0), pipeline_mode=pl.Buffered(1)),
                  pl.BlockSpec((t_len, 2 * IN_SHARD), lambda q: (0, q))],
        out_specs=pl.BlockSpec((2, None, D_MODEL, IN_SHARD), lambda q: (0, q, 0, 0)),
        out_shape=jax.ShapeDtypeStruct((2, N_CHIPS, D_MODEL, IN_SHARD), BF16),
        compiler_params=pltpu.CompilerParams(dimension_semantics=("arbitrary",), vmem_limit_bytes=VMEM_LIMIT),
    )(xn_t, dz_b)
    return out.reshape(N_DEV, D_MODEL, IN_SHARD)


def _coords():
    return lax.axis_index("x"), lax.axis_index("y"), lax.axis_index("c")


_ANY = pl.BlockSpec(memory_space=pl.ANY)


def _all_gather(shards, by_columns=()):
    n = len(shards)

    def body(*refs):
        src, dst = refs[:n], refs[n:2 * n]
        send_sems, recv_sems, local_sems = refs[2 * n:]
        x, y, c = _coords()
        me, sibling = (x, y, c), (x, y, 1 - c)
        chips = [(1 - x, y), (x, 1 - y), (1 - x, 1 - y)]

        def slot(j, dev):
            idx = 4 * dev[0] + 2 * dev[1] + dev[2]
            if j in by_columns:
                cols = shards[j].shape[1]
                return dst[j].at[:, pl.ds(pl.multiple_of(idx * cols, 128), cols)]
            return dst[j].at[idx]

        def copy(j, k, block, to, from_shard=False):
            return pltpu.make_async_remote_copy(
                src_ref=src[j] if from_shard else slot(j, block), dst_ref=slot(j, block),
                send_sem=send_sems.at[j, k], recv_sem=recv_sems.at[j, k], device_id=to, device_id_type=MESH)

        started = []
        local = []
        for j in range(n):
            for k, to in enumerate([(*chip, c) for chip in chips]):
                cp = copy(j, 1 + k, me, to, from_shard=True)
                cp.start()
                started.append(cp)
        for j in range(n):
            cp = copy(j, 0, me, sibling, from_shard=True)
            cp.start()
            started.append(cp)
            lc = pltpu.make_async_copy(src[j], slot(j, me), local_sems.at[j])
            lc.start()
            local.append(lc)
        for j in range(n):
            for k, chip in enumerate(chips):
                copy(j, 1 + k, (*chip, c), me).wait_recv()
                cp = copy(j, 4 + k, (*chip, c), sibling)
                cp.start()
                started.append(cp)
        for j in range(n):
            copy(j, 0, sibling, me).wait_recv()
            for k, chip in enumerate(chips):
                copy(j, 4 + k, (*chip, 1 - c), me).wait_recv()
        for cp in started:
            cp.wait_send()
        for lc in local:
            lc.wait()

    return pl.pallas_call(
        body, name="all_gather_weights",
        in_specs=[_ANY] * n, out_specs=[_ANY] * n,
        out_shape=[jax.ShapeDtypeStruct((s.shape[0], N_DEV * s.shape[1]) if j in by_columns else (N_DEV,) + s.shape,
                                        s.dtype) for j, s in enumerate(shards)],
        scratch_shapes=[pltpu.SemaphoreType.DMA((n, 7)), pltpu.SemaphoreType.DMA((n, 7)),
                        pltpu.SemaphoreType.DMA((n,))],
    )(*shards)


def _pair_exchange(name, bigs, smalls):
    nb, n = len(bigs), len(bigs) + len(smalls)

    def body(*refs):
        src, dst = refs[:n], refs[n:2 * n]
        send_sems, recv_sems = refs[2 * n:]
        x, y, c = _coords()
        copies = []
        for j in range(n):
            s = src[j].at[pl.ds(4 * (1 - c), 4)] if j < nb else src[j]
            cp = pltpu.make_async_remote_copy(src_ref=s, dst_ref=dst[j], send_sem=send_sems.at[j],
                                              recv_sem=recv_sems.at[j], device_id=(x, y, 1 - c), device_id_type=MESH)
            cp.start()
            copies.append(cp)
        for cp in copies:
            cp.wait()

    return pl.pallas_call(
        body, name=name,
        in_specs=[_ANY] * n, out_specs=[_ANY] * n,
        out_shape=[jax.ShapeDtypeStruct((4,) + b.shape[1:], b.dtype) for b in bigs]
        + [jax.ShapeDtypeStruct(s.shape, s.dtype) for s in smalls],
        scratch_shapes=[pltpu.SemaphoreType.DMA((n,)), pltpu.SemaphoreType.DMA((n,))],
    )(*bigs, *smalls)


def _chip_exchange(bigs, smalls):
    nb, n = len(bigs), len(bigs) + len(smalls)

    def body(*refs):
        src, dst = refs[:n], refs[n:2 * n]
        send_sems, recv_sems, local_sems = refs[2 * n:]
        x, y, c = _coords()
        q_me = 2 * x + y
        peers = [(1 - x, y, c), (x, 1 - y, c), (1 - x, 1 - y, c)]

        def piece(j, q):
            return src[j].at[q] if j < nb else src[j]

        started = []
        for j in range(n):
            for r, peer in enumerate(peers):
                cp = pltpu.make_async_remote_copy(
                    src_ref=piece(j, 2 * peer[0] + peer[1]), dst_ref=dst[j].at[q_me],
                    send_sem=send_sems.at[j, r], recv_sem=recv_sems.at[j, r], device_id=peer, device_id_type=MESH)
                cp.start()
                started.append(cp)
        local = []
        for j in range(n):
            lc = pltpu.make_async_copy(piece(j, q_me), dst[j].at[q_me], local_sems.at[j])
            lc.start()
            local.append(lc)
        for j in range(n):
            for r, peer in enumerate(peers):
                q_peer = 2 * peer[0] + peer[1]
                pltpu.make_async_remote_copy(
                    src_ref=piece(j, q_peer), dst_ref=dst[j].at[q_peer],
                    send_sem=send_sems.at[j, r], recv_sem=recv_sems.at[j, r], device_id=peer,
                    device_id_type=MESH).wait_recv()
        for cp in started:
            cp.wait_send()
        for lc in local:
            lc.wait()

    return pl.pallas_call(
        body, name="chip_exchange",
        in_specs=[_ANY] * n, out_specs=[_ANY] * n,
        out_shape=[jax.ShapeDtypeStruct(b.shape, b.dtype) for b in bigs]
        + [jax.ShapeDtypeStruct((N_CHIPS,) + s.shape, s.dtype) for s in smalls],
        scratch_shapes=[pltpu.SemaphoreType.DMA((n, 3)), pltpu.SemaphoreType.DMA((n, 3)),
                        pltpu.SemaphoreType.DMA((n,))],
    )(*bigs, *smalls)


_HBM = pl.BlockSpec(memory_space=pltpu.HBM)
_SEM = pl.BlockSpec(memory_space=pltpu.SEMAPHORE)
_VMEM = pl.BlockSpec(memory_space=pltpu.VMEM)
_EFFECT = pltpu.SideEffectType.DATAFLOW_SIDE_EFFECTING
_TOKEN = jax.ShapeDtypeStruct((8, 128), F32)


def _in_hbm(a):
    return pltpu.with_memory_space_constraint(a, pltpu.HBM)


def _split_call(name, body, n_sems_out, arrays, sems_in=(), after=None):
    na, ns = len(arrays), len(sems_in)
    has_after = after is not None

    def kernel_body(*refs):
        arr = refs[:na]
        s_in = refs[na:na + ns]
        outs = refs[na + ns + has_after:]
        body(arr, s_in, outs[:n_sems_out])
        outs[-1][...] = jnp.zeros((8, 128), F32)

    out_shape = ([pltpu.SemaphoreType.DMA(())] * n_sems_out + [pltpu.HBM(a.shape, a.dtype) for a in arrays] + [_TOKEN])
    res = pl.pallas_call(
        kernel_body, name=name, out_shape=out_shape,
        in_specs=[_HBM] * na + [_SEM] * ns + [_ANY] * has_after,
        out_specs=[_SEM] * n_sems_out + [_HBM] * na + [_VMEM],
        input_output_aliases={i: n_sems_out + i for i in range(na)},
        compiler_params=pltpu.CompilerParams(has_side_effects=_EFFECT),
    )(*[_in_hbm(a) for a in arrays], *sems_in, *([after] if has_after else []))
    return list(res[:n_sems_out]), list(res[n_sems_out:n_sems_out + na]), res[-1]


def _wait_bytes_of(ref, send_sem, recv_sem, peer, send=True, recv=True):
    cp = pltpu.make_async_remote_copy(src_ref=ref, dst_ref=ref, send_sem=send_sem, recv_sem=recv_sem,
                                      device_id=peer, device_id_type=MESH)
    if send:
        cp.wait_send()
    if recv:
        cp.wait_recv()


def _gather_behind(shards, after, work):
    n = len(shards)
    lands = [lax.empty((N_DEV,) + s.shape, s.dtype) for s in shards]

    def own_slot(arr, j, sem):
        x, y, c = _coords()
        return pltpu.make_async_copy(arr[j], arr[n + j].at[4 * x + 2 * y + c], sem)

    def start(arr, _, sems):
        x, y, c = _coords()
        me = 4 * x + 2 * y + c
        for j in range(n):
            for chip in [(1 - x, y), (x, 1 - y), (1 - x, 1 - y)]:
                pltpu.make_async_remote_copy(src_ref=arr[j], dst_ref=arr[n + j].at[me], send_sem=sems[j],
                                             recv_sem=sems[n + j], device_id=(*chip, c), device_id_type=MESH).start()
        for j in range(n):
            pltpu.make_async_remote_copy(src_ref=arr[j], dst_ref=arr[n + j].at[me], send_sem=sems[2 * n + j],
                                         recv_sem=sems[3 * n + j], device_id=(x, y, 1 - c),
                                         device_id_type=MESH).start()
            own_slot(arr, j, sems[4 * n + j]).start()

    def middle(arr, s_in, sems):
        x, y, c = _coords()
        sibling = (x, y, 1 - c)
        for j in range(n):
            _wait_bytes_of(arr[n + j].at[pl.ds(0, 3)], s_in[j], s_in[n + j], sibling)
        for j in range(n):
            for chip in [(1 - x, y), (x, 1 - y), (1 - x, 1 - y)]:
                slot = arr[n + j].at[4 * chip[0] + 2 * chip[1] + c]
                pltpu.make_async_remote_copy(src_ref=slot, dst_ref=slot, send_sem=sems[j], recv_sem=sems[n + j],
                                             device_id=sibling, device_id_type=MESH).start()

    def finish(arr, s_in, _):
        x, y, c = _coords()
        sibling = (x, y, 1 - c)
        for j in range(n):
            _wait_bytes_of(arr[n + j].at[pl.ds(0, 1)], s_in[j], s_in[n + j], sibling)
            own_slot(arr, j, s_in[2 * n + j]).wait()
            _wait_bytes_of(arr[n + j].at[pl.ds(0, 3)], s_in[3 * n + j], s_in[4 * n + j], sibling)

    sems, arrays, token = _split_call("gather_ff_start", start, 5 * n, list(shards) + lands, after=after)
    result = work(token)
    fwd_sems, arrays, token = _split_call("gather_ff_middle", middle, 2 * n, arrays, sems_in=sems[:2 * n],
                                          after=result[0])
    _, arrays, _ = _split_call("gather_ff_finish", finish, 0, arrays, sems_in=sems[2 * n:] + fwd_sems, after=token)
    return arrays[n:], result


def _chip_exchange_behind(tag, bigs, smalls, work):
    nb, n = len(bigs), len(bigs) + len(smalls)
    lands = [lax.empty(s.shape, s.dtype) for s in bigs] + [lax.empty((N_CHIPS,) + s.shape, s.dtype) for s in smalls]

    def own_slot(arr, j, sem):
        x, y, _ = _coords()
        q_me = 2 * x + y
        return pltpu.make_async_copy(arr[j].at[q_me] if j < nb else arr[j], arr[n + j].at[q_me], sem)

    def start(arr, _, sems):
        x, y, c = _coords()
        q_me = 2 * x + y
        for j in range(n):
            for peer in [(1 - x, y, c), (x, 1 - y, c), (1 - x, 1 - y, c)]:
                piece = arr[j].at[2 * peer[0] + peer[1]] if j < nb else arr[j]
                pltpu.make_async_remote_copy(src_ref=piece, dst_ref=arr[n + j].at[q_me],
                                             send_sem=sems[j], recv_sem=sems[n + j], device_id=peer,
                                             device_id_type=MESH).start()
            own_slot(arr, j, sems[2 * n + j]).start()

    def finish(arr, s_in, _):
        x, y, c = _coords()
        for j in range(n):
            _wait_bytes_of(arr[n + j].at[pl.ds(0, 3)], s_in[j], s_in[n + j], (x, y, 1 - c))
            own_slot(arr, j, s_in[2 * n + j]).wait()

    sems, arrays, token = _split_call("chip_exchange_%s_start" % tag, start, 3 * n, list(bigs) + list(smalls) + lands)
    result = work(token)
    _, arrays, _ = _split_call("chip_exchange_%s_finish" % tag, finish, 0, arrays, sems_in=sems, after=result[0])
    return arrays[n:], result


def _pair_sum_big(name, c_arr, mine, theirs):
    _, rows, cols = theirs.shape
    tr = rows

    def body(c_ref, a_ref, b_ref, o_ref):
        del c_ref
        o_ref[...] = (a_ref[...].astype(F32) + b_ref[...].astype(F32)).astype(BF16)

    return pl.pallas_call(
        body, name=name,
        grid_spec=pltpu.PrefetchScalarGridSpec(
            num_scalar_prefetch=1, grid=(N_CHIPS, rows // tr),
            in_specs=[pl.BlockSpec((None, tr, cols), lambda q, r, c_ref: (4 * c_ref[0] + q, r, 0)),
                      pl.BlockSpec((None, tr, cols), lambda q, r, c_ref: (q, r, 0))],
            out_specs=pl.BlockSpec((None, tr, cols), lambda q, r, c_ref: (q, r, 0))),
        out_shape=jax.ShapeDtypeStruct(theirs.shape, BF16),
        compiler_params=pltpu.CompilerParams(dimension_semantics=("arbitrary", "arbitrary")),
    )(c_arr, mine, theirs)


def _pair_sum_small(name, mine, theirs):
    n = len(mine)

    def body(*refs):
        for j in range(n):
            refs[2 * n + j][...] = refs[j][...] + refs[n + j][...]

    return pl.pallas_call(
        body, name=name,
        out_shape=[jax.ShapeDtypeStruct(m.shape, m.dtype) for m in mine],
    )(*mine, *theirs)


def _adamw_math(w, g, m, v):
    m = ADAM_B1 * m + (1.0 - ADAM_B1) * g
    v = ADAM_B2 * v + (1.0 - ADAM_B2) * (g * g)
    m_hat = m / (1.0 - ADAM_B1 ** ADAM_STEP)
    v_hat = v / (1.0 - ADAM_B2 ** ADAM_STEP)
    delta = -ADAM_LR * (m_hat / (jnp.sqrt(v_hat) + ADAM_EPS) + ADAM_WD * w)
    return delta, m, v


def _adamw_big(name, after, chip_sums, w, m, v):
    rows, cols = w.shape
    tr = min(rows, 512)

    def body(after_ref, t_ref, w_ref, m_ref, v_ref, g_out, d_out, m_out, v_out):
        del after_ref
        g = t_ref[0].astype(F32)
        for q in range(1, N_CHIPS):
            g = g + t_ref[q].astype(F32)
        d, mn, vn = _adamw_math(w_ref[...], g, m_ref[...], v_ref[...])
        g_out[...] = g
        d_out[...] = d
        m_out[...] = mn
        v_out[...] = vn

    blk = pl.BlockSpec((tr, cols), lambda r: (r, 0))
    return pl.pallas_call(
        body, name=name, grid=(rows // tr,),
        in_specs=[_ANY, pl.BlockSpec((N_CHIPS, tr, cols), lambda r: (0, r, 0)), blk, blk, blk],
        out_specs=[blk] * 4,
        out_shape=[jax.ShapeDtypeStruct((rows, cols), F32)] * 4,
        compiler_params=pltpu.CompilerParams(dimension_semantics=("arbitrary",), vmem_limit_bytes=VMEM_LIMIT),
    )(after, chip_sums, w, m, v)


SMALL_ROWS = ("loss", "norm1_pre_g", "pool_scale", "sgu_ln_g", "sgu_ln_b", "norm1_post_g", "norm2_pre_g",
              "norm2_post_g")


def _adamw_small(u_rows, u_bin, u_ws, u_bs, params):
    n = len(params)

    def body(*refs):
        urow_ref, ubin_ref, uws_ref, ubs_ref = refs[:4]
        wmv = refs[4:4 + 3 * n]
        loss_ref = refs[4 + 3 * n]
        outs = refs[5 + 3 * n:]

        def total(ref, idx):
            g = ref[(0,) + idx]
            for q in range(1, N_CHIPS):
                g = g + ref[(q,) + idx]
            return g

        loss_ref[...] = total(urow_ref, (slice(0, 1), slice(None)))
        for p in range(n):
            if p < 7:
                g = total(urow_ref, (slice(p + 1, p + 2), slice(None)))
            else:
                g = total((ubin_ref, uws_ref, ubs_ref)[p - 7], (slice(None), slice(None)))
            d, mn, vn = _adamw_math(wmv[3 * p][...], g, wmv[3 * p + 1][...], wmv[3 * p + 2][...])
            outs[4 * p][...] = g
            outs[4 * p + 1][...] = d
            outs[4 * p + 2][...] = mn
            outs[4 * p + 3][...] = vn

    flat = [a for p in params for a in p]
    out_shape = [jax.ShapeDtypeStruct((1, D_MODEL), F32)]
    for w, _, _ in params:
        out_shape += [jax.ShapeDtypeStruct(w.shape, F32)] * 4
    return pl.pallas_call(body, name="adamw_small", out_shape=out_shape)(u_rows, u_bin, u_ws, u_bs, *flat)


def _slots_of_rows(full):
    owners = [_owner_of_slot(s) for s in range(N_DEV)]
    parts = full.reshape(N_GROUPS, N_DEV, PG_SHARD, GROUP)
    return jnp.stack([parts[:, o] for o in owners]).reshape(N_DEV, N_GROUPS * PG_SHARD, GROUP).astype(BF16)


def kernel(x, norm1_pre_g, w_in, b_in, w_pool, pool_scale, sgu_ln_g, sgu_ln_b, w_spatial, b_spatial, w_sgu_proj, w_out, norm1_post_g, norm2_pre_g, w_ff1, w_ff2, norm2_post_g, loss_target, m_norm1_pre_g, m_w_in, m_b_in, m_w_pool, m_pool_scale, m_sgu_ln_g, m_sgu_ln_b, m_w_spatial, m_b_spatial, m_w_sgu_proj, m_w_out, m_norm1_post_g, m_norm2_pre_g, m_w_ff1, m_w_ff2, m_norm2_post_g, v_norm1_pre_g, v_w_in, v_b_in, v_w_pool, v_pool_scale, v_sgu_ln_g, v_sgu_ln_b, v_w_spatial, v_b_spatial, v_w_sgu_proj, v_w_out, v_norm1_post_g, v_norm2_pre_g, v_w_ff1, v_w_ff2, v_norm2_post_g):
    t_len = x.shape[1]
    row = lambda a: a.reshape(1, -1)
    x2 = x.reshape(t_len, D_MODEL)
    tgt2 = loss_target.reshape(t_len, D_MODEL)
    pg2 = lambda a: a.reshape(N_GROUPS * PG_SHARD, GROUP)

    win_f, g_pool, g_proj, g_out = _all_gather(
        [w_in.astype(BF16), pg2(w_pool).astype(BF16), pg2(w_sgu_proj).astype(BF16), w_out.astype(BF16)],
        by_columns=(0,))
    regroup = lambda g: g.reshape(N_DEV, N_GROUPS, PG_SHARD, GROUP).transpose(1, 0, 2, 3).reshape(N_GROUPS, GROUP, GROUP)
    wpool_f, wproj_f = regroup(g_pool), regroup(g_proj)
    wout_f = g_out.reshape(D_MODEL, D_MODEL)
    bsp_t = b_spatial.T

    def forward(token):
        z, xn_b, y, h1 = _fwd_mix(token, x2, row(norm1_pre_g), win_f, row(b_in), wpool_f, row(pool_scale),
                                  row(sgu_ln_g), row(sgu_ln_b), w_spatial, bsp_t, wproj_f, wout_f, row(norm1_post_g))
        return h1, z, xn_b, y

    (g_ff1, g_ff2), (h1, z, xn_b, y) = _gather_behind([w_ff1.astype(BF16), w_ff2.astype(BF16)], win_f, forward)
    w2_f = g_ff2.reshape(D_FF, D_MODEL)
    hn_b, f_b, df1_b, df2_b, dh1, dg2post, dg2pre, loss_p = _mlp(h1, tgt2, row(norm2_pre_g), row(norm2_post_g),
                                                               g_ff1, w2_f)
    p_ff2 = _wgrad("wgrad_ff2", f_b, df2_b, FF_SHARD, True)
    p_ff1 = _wgrad("wgrad_ff1", hn_b, df1_b, FF_SHARD, False)
    c_arr = lax.axis_index("c").astype(jnp.int32).reshape(1)
    ff_parts = [p_ff1, p_ff2]
    got_ff = _pair_exchange("pair_exchange_ff", ff_parts, [])
    chip_ff = [_pair_sum_big("pair_sum_" + nm, c_arr, b, r) for nm, b, r in zip(("ff1", "ff2"), ff_parts, got_ff)]

    def backward_mix(token):
        return _bwd_mix(token, dh1, y, z, wpool_f, row(pool_scale), row(sgu_ln_g), row(sgu_ln_b), w_spatial, bsp_t,
                        wproj_f, wout_f, row(norm1_post_g))

    summed_ff, (dz_b, mg_b, dy_b, dwpool, dwproj, dws, dbsp_t, dg1post, dps, dlng, dlnb,
                dbin) = _chip_exchange_behind("ff", chip_ff, [], backward_mix)

    p_in = _wgrad_in(xn_b, dz_b)
    bigs = [p_in, _slots_of_rows(dwpool), _slots_of_rows(dwproj)]
    rows = jnp.concatenate([jnp.broadcast_to(loss_p[:, 0:1], (1, D_MODEL)), dps, dlng, dlnb, dg1post, dg2pre, dg2post],
                           axis=0)
    smalls = [rows, dbin, dws.reshape(N_HEADS * SGU_BLOCK, SGU_BLOCK), dbsp_t.T]
    got = _pair_exchange("pair_exchange_in", bigs, smalls)
    chip_bigs = [_pair_sum_big("pair_sum_" + nm, c_arr, b, r) for nm, b, r in zip(("in", "pool", "proj"), bigs, got[:3])]
    chip_smalls = _pair_sum_small("pair_sum_small", smalls, got[3:])

    def backward_rest(token):
        dx, dg1pre = _bwd_in(token, dz_b, x2, dh1, row(norm1_pre_g), win_f)
        p_out = _wgrad("wgrad_out", mg_b, dy_b, OUT_SHARD, True)
        got_out = _pair_exchange("pair_exchange_out", [p_out], [dg1pre])
        chip_out = _pair_sum_big("pair_sum_out", c_arr, p_out, got_out[0])
        upd_ff = [_adamw_big("adamw_" + nm, p_out, t, *p) for nm, t, p in
                  (("ff1", summed_ff[0], (w_ff1, m_w_ff1, v_w_ff1)), ("ff2", summed_ff[1], (w_ff2, m_w_ff2, v_w_ff2)))]
        return upd_ff[1][0], dx, chip_out, _pair_sum_small("pair_sum_g1pre", [dg1pre], got_out[1:])[0], upd_ff

    summed_in, (_, dx, chip_out, chip_g1pre, upd_ff) = _chip_exchange_behind("in", chip_bigs, chip_smalls,
                                                                             backward_rest)

    def update_in(token):
        res = _adamw_big("adamw_in", token, summed_in[0], w_in, m_w_in, v_w_in)
        return res[0], res

    summed_out, (_, upd_in) = _chip_exchange_behind("out", [chip_out], [chip_g1pre], update_in)
    big = {"in": upd_in, "ff1": upd_ff[0], "ff2": upd_ff[1]}
    for nm, t, p in (("pool", summed_in[1], (pg2(w_pool), pg2(m_w_pool), pg2(v_w_pool))),
                     ("proj", summed_in[2], (pg2(w_sgu_proj), pg2(m_w_sgu_proj), pg2(v_w_sgu_proj))),
                     ("out", summed_out[0], (w_out, m_w_out, v_w_out))):
        big[nm] = _adamw_big("adamw_" + nm, t, t, *p)
    u_rows = jnp.concatenate([summed_in[3][:, 0:1], summed_out[1], summed_in[3][:, 1:]], axis=1)
    ws2 = lambda a: a.reshape(N_HEADS * SGU_BLOCK, SGU_BLOCK)
    small_params = [(row(norm1_pre_g), row(m_norm1_pre_g), row(v_norm1_pre_g)),
                    (row(pool_scale), row(m_pool_scale), row(v_pool_scale)),
                    (row(sgu_ln_g), row(m_sgu_ln_g), row(v_sgu_ln_g)),
                    (row(sgu_ln_b), row(m_sgu_ln_b), row(v_sgu_ln_b)),
                    (row(norm1_post_g), row(m_norm1_post_g), row(v_norm1_post_g)),
                    (row(norm2_pre_g), row(m_norm2_pre_g), row(v_norm2_pre_g)),
                    (row(norm2_post_g), row(m_norm2_post_g), row(v_norm2_post_g)),
                    (row(b_in), row(m_b_in), row(v_b_in)),
                    (ws2(w_spatial), ws2(m_w_spatial), ws2(v_w_spatial)),
                    (b_spatial, m_b_spatial, v_b_spatial)]
    small_out = _adamw_small(u_rows, *summed_in[4:], small_params)
    loss = small_out[0][0, 0]
    small_names = SMALL_ROWS[1:] + ("b_in", "w_spatial", "b_spatial")
    small = {nm: small_out[1 + 4 * p:5 + 4 * p] for p, nm in enumerate(small_names)}

    shapes = {"norm1_pre_g": norm1_pre_g.shape, "w_in": w_in.shape, "b_in": b_in.shape, "w_pool": w_pool.shape,
              "pool_scale": pool_scale.shape, "sgu_ln_g": sgu_ln_g.shape, "sgu_ln_b": sgu_ln_b.shape,
              "w_spatial": w_spatial.shape, "b_spatial": b_spatial.shape, "w_sgu_proj": w_sgu_proj.shape,
              "w_out": w_out.shape, "norm1_post_g": norm1_post_g.shape, "norm2_pre_g": norm2_pre_g.shape,
              "w_ff1": w_ff1.shape, "w_ff2": w_ff2.shape, "norm2_post_g": norm2_post_g.shape}
    source = {"w_in": big["in"], "w_pool": big["pool"], "w_sgu_proj": big["proj"], "w_out": big["out"],
              "w_ff1": big["ff1"], "w_ff2": big["ff2"], **small}
    order = list(shapes)
    outs = [loss, dx.reshape(x.shape)]
    for kind in range(4):
        outs += [source[nm][kind].reshape(shapes[nm]) for nm in order]
    return tuple(outs)
```

```python
import functools
import math

import jax
import jax.numpy as jnp
from jax import lax
from jax.experimental import pallas as pl
from jax.experimental.pallas import tpu as pltpu

F32, BF16 = jnp.float32, jnp.bfloat16
MESH = pl.DeviceIdType.MESH

D_MODEL = 1024
D_IN = 5120
D_FF = 4096
N_DEV = 8
N_CHIPS = 4
WINDOWS = (2, 4, 8, 16)
N_GROUPS = 4
GROUP = 256
HALO = 16
SGU_BLOCK = 128
N_HEADS = 4
HEAD = 256
CHUNK = 64
EPS = 1e-6
IN_SHARD = D_IN // N_DEV
FF_SHARD = D_FF // N_DEV
OUT_SHARD = D_MODEL // N_DEV
PG_SHARD = GROUP // N_DEV

ADAM_LR, ADAM_B1, ADAM_B2, ADAM_EPS, ADAM_WD, ADAM_STEP = 0.001, 0.9, 0.999, 1e-08, 0.01, 10

VMEM_LIMIT = 56 * 1024 * 1024
TM = 256
TM_BWD = 256
TM_IN = 512
PROJ_COLS = 512
GELU_C0 = math.sqrt(2.0 / math.pi)
GELU_C1 = 0.044715


def _dot(a, b):
    return jnp.dot(a, b, preferred_element_type=F32)


def _dot_nt(a, b):
    return lax.dot_general(a, b, (((1,), (1,)), ((), ())), preferred_element_type=F32)


def _dot_tn(a, b):
    return lax.dot_general(a, b, (((0,), (0,)), ((), ())), preferred_element_type=F32)


def _gelu(x):
    t = jnp.tanh(GELU_C0 * (x + GELU_C1 * (x * x * x)))
    return 0.5 * x * (1.0 + t), t


def _gelu_grad(x, t):
    return 0.5 * (1.0 + t) + 0.5 * x * (1.0 - t * t) * (GELU_C0 * (1.0 + 3.0 * GELU_C1 * (x * x)))


def _sigmoid(x):
    return 1.0 / (1.0 + jnp.exp(-x))


def _mean(x):
    return jnp.mean(x, axis=-1, keepdims=True)


def _colsum(x):
    return jnp.sum(x, axis=0, keepdims=True)


def _const_spec(shape):
    nd = len(shape)
    return pl.BlockSpec(shape, lambda *_: (0,) * nd, pipeline_mode=pl.Buffered(1))


def _acc_spec(shape):
    nd = len(shape)
    return pl.BlockSpec(shape, lambda *_: (0,) * nd)


def _masked_ws(ws_ref):
    ri = lax.broadcasted_iota(jnp.int32, (SGU_BLOCK, SGU_BLOCK), 0) // CHUNK
    ci = lax.broadcasted_iota(jnp.int32, (SGU_BLOCK, SGU_BLOCK), 1) // CHUNK
    return [jnp.where(ri >= ci, ws_ref[h], 0.0).astype(BF16) for h in range(N_HEADS)]


def _pool_fwd(pbuf, tile_idx, tm):
    pos = lax.broadcasted_iota(jnp.int32, (tm, 1), 0) + tile_idx * tm + 1
    pooled = []
    for g, w in enumerate(WINDOWS):
        e = pbuf[:, g * GROUP:(g + 1) * GROUP]
        s, sh = e, 1
        while sh < w:
            s = s + pltpu.roll(s, sh, 0)
            sh *= 2
        inv = 1.0 / jnp.minimum(pos, w).astype(F32)
        pooled.append(s[HALO:] * inv - e[HALO:])
    return pooled


def _sgu_fwd(zu, zv, lng, lnb, wsm, bsp_ref, tm):
    u, tu = _gelu(zu)
    gv, tv = _gelu(zv)
    xc = gv - _mean(gv)
    rln = lax.rsqrt(_mean(xc * xc) + EPS)
    xhat = xc * rln
    vb = (xhat * lng + lnb).astype(BF16)
    sv_heads = []
    for h in range(N_HEADS):
        rows = []
        for n in range(tm // SGU_BLOCK):
            blk = vb[n * SGU_BLOCK:(n + 1) * SGU_BLOCK, h * HEAD:(h + 1) * HEAD]
            rows.append(_dot(wsm[h], blk) + bsp_ref[:, h:h + 1])
        sv_heads.append(jnp.concatenate(rows, axis=0))
    return u, tu, tv, xhat, rln, vb, sv_heads


def _prenorm(after, x, g1pre):
    t_len = x.shape[0]
    tm = TM_IN

    def body(after_ref, x_ref, g_ref, xn_ref, xnt_ref):
        del after_ref
        xv = x_ref[...]
        xnb = (xv * lax.rsqrt(_mean(xv * xv) + EPS) * g_ref[...]).astype(BF16)
        xn_ref[...] = xnb
        xnt_ref[...] = xnb.T

    return pl.pallas_call(
        body, name="prenorm", grid=(t_len // tm,),
        in_specs=[_ANY, pl.BlockSpec((tm, D_MODEL), lambda i: (i, 0)), _const_spec((1, D_MODEL))],
        out_specs=[pl.BlockSpec((tm, D_MODEL), lambda i: (i, 0)), pl.BlockSpec((D_MODEL, tm), lambda i: (0, i))],
        out_shape=[jax.ShapeDtypeStruct((t_len, D_MODEL), BF16), jax.ShapeDtypeStruct((D_MODEL, t_len), BF16)],
        compiler_params=pltpu.CompilerParams(dimension_semantics=("arbitrary",)),
    )(after, x, g1pre)


def _fwd_mix(after, xn, x, win_g, b_in, wpool, pool_scale, lng, lnb, ws, bsp_t, wproj, wout, g1post):
    t_len = x.shape[0]
    tm = TM
    nt = t_len // tm

    def body(after_ref, xn_ref, xb_ref, win_ref, bin_ref, wpool_ref, ps_ref, lng_ref, lnb_ref, ws_ref, bsp_ref,
             wproj_ref, wout_ref, g1post_ref, z_ref, y_ref, h1_ref, u_ref, gpu_ref, xhat_ref, gpv_ref, sa_ref,
             sb_ref, zcur, znext, pbuf):
        del after_ref
        s = pl.program_id(0)

        @pl.when(s == 0)
        def _():
            znext[...] = jnp.zeros((tm, D_IN), F32)
            pbuf[...] = jnp.zeros((tm + HALO, D_MODEL), F32)

        zcur[...] = znext[...]
        xnb = xn_ref[...]

        def project(p):
            cols = slice(p * PROJ_COLS, (p + 1) * PROJ_COLS)
            zp = _dot(xnb, win_ref[:, cols]) + bin_ref[:, cols]
            if (p + 1) * PROJ_COLS <= D_MODEL:
                z_ref[:, cols] = zp
            znext[:, cols] = zp

        project(0)
        pbuf[0:HALO, :] = jnp.where(s <= 1, 0.0, pbuf[0:HALO, :])
        pbuf[HALO:, :] = zcur[:, 0:D_MODEL]
        pooled = _pool_fwd(pbuf, jnp.maximum(s - 1, 0), tm)
        pbuf[0:HALO, :] = pbuf[tm:tm + HALO, :]
        a = jnp.concatenate([_dot(pooled[g].astype(BF16), wpool_ref[g]) for g in range(N_GROUPS)], axis=1)
        a = a * ps_ref[...]
        project(1)
        zu = zcur[:, D_MODEL:2 * D_MODEL]
        u, tu = _gelu(zu)
        u_ref[...] = u.astype(BF16)
        gpu_ref[...] = _gelu_grad(zu, tu).astype(BF16)
        project(2)
        zv = zcur[:, 2 * D_MODEL:3 * D_MODEL]
        gv, tv = _gelu(zv)
        xc = gv - _mean(gv)
        rln = lax.rsqrt(_mean(xc * xc) + EPS)
        xhat = xc * rln
        xhat_ref[...] = xhat.astype(BF16)
        gpv_ref[...] = (_gelu_grad(zv, tv) * rln).astype(BF16)
        vb = (xhat * lng_ref[...] + lnb_ref[...]).astype(BF16)
        wsm = _masked_ws(ws_ref)
        bbr = []
        for h in range(N_HEADS):
            project(3 + h)
            sv = jnp.concatenate(
                [_dot(wsm[h], vb[n * SGU_BLOCK:(n + 1) * SGU_BLOCK, h * HEAD:(h + 1) * HEAD]) + bsp_ref[:, h:h + 1]
                 for n in range(tm // SGU_BLOCK)], axis=0)
            bbr.append(_dot((u[:, h * HEAD:(h + 1) * HEAD] * sv).astype(BF16), wproj_ref[h]))
        bbr = jnp.concatenate(bbr, axis=1)
        project(7)
        sa = _sigmoid(zcur[:, 3 * D_MODEL:4 * D_MODEL])
        sb = _sigmoid(zcur[:, 4 * D_MODEL:5 * D_MODEL])
        sa_ref[...] = sa.astype(BF16)
        sb_ref[...] = sb.astype(BF16)
        project(8)
        yv = _dot((sa * a + sb * bbr).astype(BF16), wout_ref[...])
        y_ref[...] = yv
        project(9)
        ry = lax.rsqrt(_mean(yv * yv) + EPS)
        h1_ref[...] = xb_ref[...] + yv * ry * g1post_ref[...]

    proj = lambda w: pl.BlockSpec((tm, w), lambda s: (jnp.minimum(s, nt - 1), 0))
    mix = lambda w: pl.BlockSpec((tm, w), lambda s: (jnp.maximum(s - 1, 0), 0))
    return pl.pallas_call(
        body, name="fwd_mix", grid=(nt + 1,),
        in_specs=[_ANY, proj(D_MODEL), mix(D_MODEL),
                  _const_spec((D_MODEL, D_IN)),
                  _const_spec((1, D_IN)), _const_spec((N_GROUPS, GROUP, GROUP)), _const_spec((1, D_MODEL)),
                  _const_spec((1, D_MODEL)), _const_spec((1, D_MODEL)),
                  _const_spec((N_HEADS, SGU_BLOCK, SGU_BLOCK)), _const_spec((SGU_BLOCK, N_HEADS)),
                  _const_spec((N_HEADS, HEAD, HEAD)), _const_spec((D_MODEL, D_MODEL)), _const_spec((1, D_MODEL))],
        out_specs=[proj(D_MODEL), mix(D_MODEL), mix(D_MODEL)] + [mix(D_MODEL)] * 6,
        out_shape=[jax.ShapeDtypeStruct((t_len, D_MODEL), F32),
                   jax.ShapeDtypeStruct((t_len, D_MODEL), F32), jax.ShapeDtypeStruct((t_len, D_MODEL), F32)]
        + [jax.ShapeDtypeStruct((t_len, D_MODEL), BF16)] * 6,
        scratch_shapes=[pltpu.VMEM((tm, D_IN), F32), pltpu.VMEM((tm, D_IN), F32),
                        pltpu.VMEM((tm + HALO, D_MODEL), F32)],
        compiler_params=pltpu.CompilerParams(dimension_semantics=("arbitrary",), vmem_limit_bytes=VMEM_LIMIT),
    )(after, xn, x, win_g, b_in, wpool, pool_scale, lng, lnb, ws, bsp_t, wproj, wout, g1post)


def _mlp(h1, target, g2pre, g2post, w1_g, w2):
    t_len = h1.shape[0]
    tm = TM
    nt = t_len // tm

    def body(h1_ref, tgt_ref, g2pre_ref, g2post_ref, w1_ref, w2_ref,
             hn_ref, f_ref, df1_ref, df2_ref, dh1_ref, dg2post_ref, dg2pre_ref, loss_ref, f1_scr):
        i = pl.program_id(0)

        @pl.when(i == 0)
        def _():
            dg2post_ref[...] = jnp.zeros_like(dg2post_ref)
            dg2pre_ref[...] = jnp.zeros_like(dg2pre_ref)
            loss_ref[...] = jnp.zeros_like(loss_ref)

        h = h1_ref[...]
        r2 = lax.rsqrt(_mean(h * h) + EPS)
        nh = h * r2
        hnb = (nh * g2pre_ref[...]).astype(BF16)
        hn_ref[...] = hnb.T
        for k in range(N_DEV):
            f1_scr[:, k * FF_SHARD:(k + 1) * FF_SHARD] = _dot(hnb, w1_ref[k])
        r = jnp.maximum(f1_scr[...], 0.0)
        fb = (r * r).astype(BF16)
        f_ref[...] = fb.T
        f2 = _dot(fb, w2_ref[...])
        rf = lax.rsqrt(_mean(f2 * f2) + EPS)
        nf = f2 * rf
        diff = h + nf * g2post_ref[...] - tgt_ref[...]
        loss_ref[...] += (0.5 / D_MODEL) * jnp.sum(diff * diff)
        dout = diff * (1.0 / D_MODEL)
        dg2post_ref[...] += _colsum(dout * nf)
        dn = dout * g2post_ref[...]
        df2b = (rf * (dn - nf * _mean(dn * nf))).astype(BF16)
        df2_ref[...] = df2b
        df = _dot_nt(df2b, w2_ref[...])
        df1b = (df * (2.0 * jnp.maximum(f1_scr[...], 0.0))).astype(BF16)
        df1_ref[...] = df1b
        dhn = _dot_nt(df1b[:, 0:FF_SHARD], w1_ref[0])
        for k in range(1, N_DEV):
            dhn = dhn + _dot_nt(df1b[:, k * FF_SHARD:(k + 1) * FF_SHARD], w1_ref[k])
        dg2pre_ref[...] += _colsum(dhn * nh)
        dnh = dhn * g2pre_ref[...]
        dh1_ref[...] = dout + r2 * (dnh - nh * _mean(dnh * nh))

    tok = lambda w: pl.BlockSpec((tm, w), lambda i: (i, 0))
    return pl.pallas_call(
        body, name="mlp_fwd_bwd", grid=(nt,),
        in_specs=[tok(D_MODEL), tok(D_MODEL), _const_spec((1, D_MODEL)), _const_spec((1, D_MODEL)),
                  _const_spec((N_DEV, D_MODEL, FF_SHARD)), _const_spec((D_FF, D_MODEL))],
        out_specs=[pl.BlockSpec((D_MODEL, tm), lambda i: (0, i)), pl.BlockSpec((D_FF, tm), lambda i: (0, i)),
                   tok(D_FF), tok(D_MODEL), tok(D_MODEL),
                   _acc_spec((1, D_MODEL)), _acc_spec((1, D_MODEL)), _acc_spec((1, 128))],
        out_shape=[jax.ShapeDtypeStruct((D_MODEL, t_len), BF16), jax.ShapeDtypeStruct((D_FF, t_len), BF16),
                   jax.ShapeDtypeStruct((t_len, D_FF), BF16), jax.ShapeDtypeStruct((t_len, D_MODEL), BF16),
                   jax.ShapeDtypeStruct((t_len, D_MODEL), F32), jax.ShapeDtypeStruct((1, D_MODEL), F32),
                   jax.ShapeDtypeStruct((1, D_MODEL), F32), jax.ShapeDtypeStruct((1, 128), F32)],
        scratch_shapes=[pltpu.VMEM((tm, D_FF), F32)],
        compiler_params=pltpu.CompilerParams(dimension_semantics=("arbitrary",), vmem_limit_bytes=VMEM_LIMIT),
    )(h1, target, g2pre, g2post, w1_g, w2)


def _bwd_mix(after, dh1, y, z, saved, wpool, pool_scale, lng, lnb, ws, bsp_t, wproj, wout, g1post):
    t_len = y.shape[0]
    tm = TM_BWD
    nt = t_len // tm
    nblk = tm // SGU_BLOCK

    def body(after_ref, dh1_ref, y_ref, z_ref, zh_ref, u_ref, gpu_ref, xhat_ref, gpv_ref, sa_ref, sb_ref,
             wpool_ref, ps_ref, lng_ref, lnb_ref, ws_ref,
             bsp_ref, wproj_ref, wout_ref, g1post_ref,
             dz_ref, mg_ref, dy_ref, dwpool_ref, dwproj_ref, dws_ref, dbsp_ref, dg1post_ref, dps_ref,
             dlng_ref, dlnb_ref, dbin_ref, pbuf, qbuf):
        del after_ref
        i = pl.program_id(0)
        ti = nt - 1 - i

        @pl.when(i == 0)
        def _():
            for ref in (dwpool_ref, dwproj_ref, dws_ref, dbsp_ref, dg1post_ref, dps_ref, dlng_ref, dlnb_ref,
                        dbin_ref):
                ref[...] = jnp.zeros_like(ref)
            qbuf[tm:tm + HALO, :] = jnp.zeros((HALO, D_MODEL), F32)

        pbuf[0:HALO, :] = jnp.where(ti > 0, zh_ref[...], 0.0)
        pbuf[HALO:, :] = z_ref[...]
        pooled = _pool_fwd(pbuf, ti, tm)
        pooled_b = [p.astype(BF16) for p in pooled]
        a_raw = jnp.concatenate([_dot(pooled_b[g], wpool_ref[g]) for g in range(N_GROUPS)], axis=1)
        wsm = _masked_ws(ws_ref)
        u = u_ref[...].astype(F32)
        xhat = xhat_ref[...].astype(F32)
        vb = (xhat * lng_ref[...] + lnb_ref[...]).astype(BF16)
        sv_heads = []
        for h in range(N_HEADS):
            sv_heads.append(jnp.concatenate(
                [_dot(wsm[h], vb[n * SGU_BLOCK:(n + 1) * SGU_BLOCK, h * HEAD:(h + 1) * HEAD]) + bsp_ref[:, h:h + 1]
                 for n in range(nblk)], axis=0))
        gated_b = [(u[:, h * HEAD:(h + 1) * HEAD] * sv_heads[h]).astype(BF16) for h in range(N_HEADS)]
        bbr = jnp.concatenate([_dot(gated_b[h], wproj_ref[h]) for h in range(N_HEADS)], axis=1)
        sa = sa_ref[...].astype(F32)
        sb = sb_ref[...].astype(F32)
        a = a_raw * ps_ref[...]
        mg_ref[...] = (sa * a + sb * bbr).astype(BF16).T

        dh = dh1_ref[...]
        yv = y_ref[...]
        ry = lax.rsqrt(_mean(yv * yv) + EPS)
        ny = yv * ry
        dg1post_ref[...] += _colsum(dh * ny)
        dn = dh * g1post_ref[...]
        dyb = (ry * (dn - ny * _mean(dn * ny))).astype(BF16)
        dy_ref[...] = dyb
        dmg = _dot_nt(dyb, wout_ref[...])

        da = dmg * sa
        dbbr = dmg * sb
        dzga = dmg * a * sa * (1.0 - sa)
        dzgb = dmg * bbr * sb * (1.0 - sb)
        dz_ref[:, 3 * D_MODEL:4 * D_MODEL] = dzga.astype(BF16)
        dz_ref[:, 4 * D_MODEL:5 * D_MODEL] = dzgb.astype(BF16)
        dbin_ref[:, 3 * D_MODEL:4 * D_MODEL] += _colsum(dzga)
        dbin_ref[:, 4 * D_MODEL:5 * D_MODEL] += _colsum(dzgb)

        dps_ref[...] += _colsum(da * a_raw)
        da_raw_b = (da * ps_ref[...]).astype(BF16)
        pos = lax.broadcasted_iota(jnp.int32, (tm, 1), 0) + ti * tm + 1
        dpooled = []
        for g, w in enumerate(WINDOWS):
            cols = slice(g * GROUP, (g + 1) * GROUP)
            dwpool_ref[g] += _dot_tn(pooled_b[g], da_raw_b[:, cols])
            dp = _dot_nt(da_raw_b[:, cols], wpool_ref[g])
            dpooled.append(dp)
            qbuf[0:tm, cols] = dp * (1.0 / jnp.minimum(pos, w).astype(F32))
        n_ext = tm + HALO
        dzp = []
        for g, w in enumerate(WINDOWS):
            e = qbuf[:, g * GROUP:(g + 1) * GROUP]
            s, sh = e, 1
            while sh < w:
                s = s + pltpu.roll(s, n_ext - sh, 0)
                sh *= 2
            dzp.append(s[0:tm] - dpooled[g])
        qbuf[tm:tm + HALO, :] = qbuf[0:HALO, :]
        dzp = jnp.concatenate(dzp, axis=1)
        dz_ref[:, 0:D_MODEL] = dzp.astype(BF16)
        dbin_ref[:, 0:D_MODEL] += _colsum(dzp)

        dv_heads = []
        du_heads = []
        for h in range(N_HEADS):
            cols = slice(h * HEAD, (h + 1) * HEAD)
            dbbr_b = dbbr[:, cols].astype(BF16)
            dwproj_ref[h] += _dot_tn(gated_b[h], dbbr_b)
            dgated = _dot_nt(dbbr_b, wproj_ref[h])
            du_heads.append(dgated * sv_heads[h])
            dsv = dgated * u[:, cols]
            dsv_b = dsv.astype(BF16)
            rows = []
            for n in range(nblk):
                blk = slice(n * SGU_BLOCK, (n + 1) * SGU_BLOCK)
                rows.append(_dot_tn(wsm[h], dsv_b[blk]))
                dws_ref[h] += _dot_nt(dsv_b[blk], vb[blk, cols])
                dbsp_ref[:, h:h + 1] += jnp.sum(dsv[blk], axis=1, keepdims=True)
            dv_heads.append(jnp.concatenate(rows, axis=0))
        dzu = jnp.concatenate(du_heads, axis=1) * gpu_ref[...].astype(F32)
        dz_ref[:, D_MODEL:2 * D_MODEL] = dzu.astype(BF16)
        dbin_ref[:, D_MODEL:2 * D_MODEL] += _colsum(dzu)
        dv = jnp.concatenate(dv_heads, axis=1)
        dlng_ref[...] += _colsum(dv * xhat)
        dlnb_ref[...] += _colsum(dv)
        dxh = dv * lng_ref[...]
        dzv = (dxh - _mean(dxh) - xhat * _mean(dxh * xhat)) * gpv_ref[...].astype(F32)
        dz_ref[:, 2 * D_MODEL:3 * D_MODEL] = dzv.astype(BF16)
        dbin_ref[:, 2 * D_MODEL:3 * D_MODEL] += _colsum(dzv)

        @pl.when(i == nt - 1)
        def _():
            ri = lax.broadcasted_iota(jnp.int32, (SGU_BLOCK, SGU_BLOCK), 0) // CHUNK
            ci = lax.broadcasted_iota(jnp.int32, (SGU_BLOCK, SGU_BLOCK), 1) // CHUNK
            for h in range(N_HEADS):
                dws_ref[h] = jnp.where(ri >= ci, dws_ref[h], 0.0)

    tok = lambda w: pl.BlockSpec((tm, w), lambda i: (nt - 1 - i, 0))
    halo = pl.BlockSpec((HALO, D_MODEL), lambda i: (jnp.maximum((nt - 1 - i) * (tm // HALO) - 1, 0), 0))
    return pl.pallas_call(
        body, name="bwd_mix", grid=(nt,),
        in_specs=[_ANY, tok(D_MODEL), tok(D_MODEL), tok(D_MODEL), halo] + [tok(D_MODEL)] * 6
        + [_const_spec((N_GROUPS, GROUP, GROUP)),
                  _const_spec((1, D_MODEL)), _const_spec((1, D_MODEL)), _const_spec((1, D_MODEL)),
                  _const_spec((N_HEADS, SGU_BLOCK, SGU_BLOCK)), _const_spec((SGU_BLOCK, N_HEADS)),
                  _const_spec((N_HEADS, HEAD, HEAD)), _const_spec((D_MODEL, D_MODEL)), _const_spec((1, D_MODEL))],
        out_specs=[tok(D_IN), pl.BlockSpec((D_MODEL, tm), lambda i: (0, nt - 1 - i)), tok(D_MODEL),
                   _acc_spec((N_GROUPS, GROUP, GROUP)), _acc_spec((N_HEADS, HEAD, HEAD)),
                   _acc_spec((N_HEADS, SGU_BLOCK, SGU_BLOCK)), _acc_spec((SGU_BLOCK, N_HEADS)),
                   _acc_spec((1, D_MODEL)), _acc_spec((1, D_MODEL)), _acc_spec((1, D_MODEL)), _acc_spec((1, D_MODEL)),
                   _acc_spec((1, D_IN))],
        out_shape=[jax.ShapeDtypeStruct((t_len, D_IN), BF16),
                   jax.ShapeDtypeStruct((D_MODEL, t_len), BF16), jax.ShapeDtypeStruct((t_len, D_MODEL), BF16),
                   jax.ShapeDtypeStruct((N_GROUPS, GROUP, GROUP), F32), jax.ShapeDtypeStruct((N_HEADS, HEAD, HEAD), F32),
                   jax.ShapeDtypeStruct((N_HEADS, SGU_BLOCK, SGU_BLOCK), F32),
                   jax.ShapeDtypeStruct((SGU_BLOCK, N_HEADS), F32),
                   jax.ShapeDtypeStruct((1, D_MODEL), F32), jax.ShapeDtypeStruct((1, D_MODEL), F32),
                   jax.ShapeDtypeStruct((1, D_MODEL), F32), jax.ShapeDtypeStruct((1, D_MODEL), F32),
                   jax.ShapeDtypeStruct((1, D_IN), F32)],
        scratch_shapes=[pltpu.VMEM((tm + HALO, D_MODEL), F32), pltpu.VMEM((tm + HALO, D_MODEL), F32)],
        compiler_params=pltpu.CompilerParams(dimension_semantics=("arbitrary",), vmem_limit_bytes=VMEM_LIMIT),
    )(after, dh1, y, z, z, *saved, wpool, pool_scale, lng, lnb, ws, bsp_t, wproj, wout, g1post)


def _bwd_in(after, dz_b, x, dh1, g1pre, win_g):
    t_len = x.shape[0]
    tm = TM_IN
    nt = t_len // tm

    def body(after_ref, dz_ref, x_ref, dh1_ref, g1_ref, win_ref, dx_ref, dg1pre_ref):
        del after_ref

        @pl.when(pl.program_id(0) == 0)
        def _():
            dg1pre_ref[...] = jnp.zeros_like(dg1pre_ref)

        dxn = _dot_nt(dz_ref[...], win_ref[...])
        xv = x_ref[...]
        r1 = lax.rsqrt(_mean(xv * xv) + EPS)
        nx = xv * r1
        dg1pre_ref[...] += _colsum(dxn * nx)
        dnx = dxn * g1_ref[...]
        dx_ref[...] = r1 * (dnx - nx * _mean(dnx * nx)) + dh1_ref[...]

    tok = lambda w: pl.BlockSpec((tm, w), lambda i: (i, 0))
    return pl.pallas_call(
        body, name="bwd_in", grid=(nt,),
        in_specs=[_ANY, tok(D_IN), tok(D_MODEL), tok(D_MODEL), _const_spec((1, D_MODEL)),
                  _const_spec((D_MODEL, D_IN))],
        out_specs=[tok(D_MODEL), _acc_spec((1, D_MODEL))],
        out_shape=[jax.ShapeDtypeStruct((t_len, D_MODEL), F32), jax.ShapeDtypeStruct((1, D_MODEL), F32)],
        compiler_params=pltpu.CompilerParams(dimension_semantics=("arbitrary",), vmem_limit_bytes=VMEM_LIMIT),
    )(after, dz_b, x, dh1, g1pre, win_g)


def _owner_of_slot(s):
    return 4 * ((s // 2) % 2) + 2 * (s % 2) + s // 4


def _wgrad(name, a_t, b, shard, a_sliced):
    t_len = b.shape[0]
    am = shard if a_sliced else a_t.shape[0]
    bn = b.shape[1] if a_sliced else shard

    def body(a_ref, b_ref, o_ref):
        o_ref[...] = _dot(a_ref[...], b_ref[...]).astype(BF16)

    if a_sliced:
        in_specs = [pl.BlockSpec((am, t_len), lambda s: (_owner_of_slot(s), 0)),
                    pl.BlockSpec((t_len, bn), lambda s: (0, 0), pipeline_mode=pl.Buffered(1))]
    else:
        in_specs = [pl.BlockSpec((am, t_len), lambda s: (0, 0), pipeline_mode=pl.Buffered(1)),
                    pl.BlockSpec((t_len, bn), lambda s: (0, _owner_of_slot(s)))]
    return pl.pallas_call(
        body, name=name, grid=(N_DEV,), in_specs=in_specs,
        out_specs=pl.BlockSpec((None, am, bn), lambda s: (s, 0, 0)),
        out_shape=jax.ShapeDtypeStruct((N_DEV, am, bn), BF16),
        compiler_params=pltpu.CompilerParams(dimension_semantics=("arbitrary",), vmem_limit_bytes=VMEM_LIMIT),
    )(a_t, b)


def _wgrad_in(xn_t, dz_b):
    t_len = dz_b.shape[0]

    def body(a_ref, b_ref, o_ref):
        res = _dot(a_ref[...], b_ref[...])
        o_ref[0] = res[:, 0:IN_SHARD].astype(BF16)
        o_ref[1] = res[:, IN_SHARD:2 * IN_SHARD].astype(BF16)

    out = pl.pallas_call(
        body, name="wgrad_in", grid=(N_CHIPS,),
        in_specs=[pl.BlockSpec((D_MODEL, t_len), lambda q: (0, 0), pipeline_mode=pl.Buffered(1)),
                  pl.BlockSpec((t_len, 2 * IN_SHARD), lambda q: (0, q))],
        out_specs=pl.BlockSpec((2, None, D_MODEL, IN_SHARD), lambda q: (0, q, 0, 0)),
        out_shape=jax.ShapeDtypeStruct((2, N_CHIPS, D_MODEL, IN_SHARD), BF16),
        compiler_params=pltpu.CompilerParams(dimension_semantics=("arbitrary",), vmem_limit_bytes=VMEM_LIMIT),
    )(xn_t, dz_b)
    return out.reshape(N_DEV, D_MODEL, IN_SHARD)


def _coords():
    return lax.axis_index("x"), lax.axis_index("y"), lax.axis_index("c")


_ANY = pl.BlockSpec(memory_space=pl.ANY)


def _all_gather(shards, by_columns=()):
    n = len(shards)

    def body(*refs):
        src, dst = refs[:n], refs[n:2 * n]
        send_sems, recv_sems, local_sems = refs[2 * n:]
        x, y, c = _coords()
        me, sibling = (x, y, c), (x, y, 1 - c)
        chips = [(1 - x, y), (x, 1 - y), (1 - x, 1 - y)]

        def slot(j, dev):
            idx = 4 * dev[0] + 2 * dev[1] + dev[2]
            if j in by_columns:
                cols = shards[j].shape[1]
                return dst[j].at[:, pl.ds(pl.multiple_of(idx * cols, 128), cols)]
            return dst[j].at[idx]

        def copy(j, k, block, to, from_shard=False):
            return pltpu.make_async_remote_copy(
                src_ref=src[j] if from_shard else slot(j, block), dst_ref=slot(j, block),
                send_sem=send_sems.at[j, k], recv_sem=recv_sems.at[j, k], device_id=to, device_id_type=MESH)

        started = []
        local = []
        for j in range(n):
            for k, to in enumerate([(*chip, c) for chip in chips]):
                cp = copy(j, 1 + k, me, to, from_shard=True)
                cp.start()
                started.append(cp)
        for j in range(n):
            cp = copy(j, 0, me, sibling, from_shard=True)
            cp.start()
            started.append(cp)
            lc = pltpu.make_async_copy(src[j], slot(j, me), local_sems.at[j])
            lc.start()
            local.append(lc)
        for j in range(n):
            for k, chip in enumerate(chips):
                copy(j, 1 + k, (*chip, c), me).wait_recv()
                cp = copy(j, 4 + k, (*chip, c), sibling)
                cp.start()
                started.append(cp)
        for j in range(n):
            copy(j, 0, sibling, me).wait_recv()
            for k, chip in enumerate(chips):
                copy(j, 4 + k, (*chip, 1 - c), me).wait_recv()
        for cp in started:
            cp.wait_send()
        for lc in local:
            lc.wait()

    return pl.pallas_call(
        body, name="all_gather_weights",
        in_specs=[_ANY] * n, out_specs=[_ANY] * n,
        out_shape=[jax.ShapeDtypeStruct((s.shape[0], N_DEV * s.shape[1]) if j in by_columns else (N_DEV,) + s.shape,
                                        s.dtype) for j, s in enumerate(shards)],
        scratch_shapes=[pltpu.SemaphoreType.DMA((n, 7)), pltpu.SemaphoreType.DMA((n, 7)),
                        pltpu.SemaphoreType.DMA((n,))],
    )(*shards)


def _pair_exchange(name, bigs, smalls):
    nb, n = len(bigs), len(bigs) + len(smalls)

    def body(*refs):
        src, dst = refs[:n], refs[n:2 * n]
        send_sems, recv_sems = refs[2 * n:]
        x, y, c = _coords()
        copies = []
        for j in range(n):
            s = src[j].at[pl.ds(4 * (1 - c), 4)] if j < nb else src[j]
            cp = pltpu.make_async_remote_copy(src_ref=s, dst_ref=dst[j], send_sem=send_sems.at[j],
                                              recv_sem=recv_sems.at[j], device_id=(x, y, 1 - c), device_id_type=MESH)
            cp.start()
            copies.append(cp)
        for cp in copies:
            cp.wait()

    return pl.pallas_call(
        body, name=name,
        in_specs=[_ANY] * n, out_specs=[_ANY] * n,
        out_shape=[jax.ShapeDtypeStruct((4,) + b.shape[1:], b.dtype) for b in bigs]
        + [jax.ShapeDtypeStruct(s.shape, s.dtype) for s in smalls],
        scratch_shapes=[pltpu.SemaphoreType.DMA((n,)), pltpu.SemaphoreType.DMA((n,))],
    )(*bigs, *smalls)


def _chip_exchange(bigs, smalls):
    nb, n = len(bigs), len(bigs) + len(smalls)

    def body(*refs):
        src, dst = refs[:n], refs[n:2 * n]
        send_sems, recv_sems, local_sems = refs[2 * n:]
        x, y, c = _coords()
        q_me = 2 * x + y
        peers = [(1 - x, y, c), (x, 1 - y, c), (1 - x, 1 - y, c)]

        def piece(j, q):
            return src[j].at[q] if j < nb else src[j]

        started = []
        for j in range(n):
            for r, peer in enumerate(peers):
                cp = pltpu.make_async_remote_copy(
                    src_ref=piece(j, 2 * peer[0] + peer[1]), dst_ref=dst[j].at[q_me],
                    send_sem=send_sems.at[j, r], recv_sem=recv_sems.at[j, r], device_id=peer, device_id_type=MESH)
                cp.start()
                started.append(cp)
        local = []
        for j in range(n):
            lc = pltpu.make_async_copy(piece(j, q_me), dst[j].at[q_me], local_sems.at[j])
            lc.start()
            local.append(lc)
        for j in range(n):
            for r, peer in enumerate(peers):
                q_peer = 2 * peer[0] + peer[1]
                pltpu.make_async_remote_copy(
                    src_ref=piece(j, q_peer), dst_ref=dst[j].at[q_peer],
                    send_sem=send_sems.at[j, r], recv_sem=recv_sems.at[j, r], device_id=peer,
                    device_id_type=MESH).wait_recv()
        for cp in started:
            cp.wait_send()
        for lc in local:
            lc.wait()

    return pl.pallas_call(
        body, name="chip_exchange",
        in_specs=[_ANY] * n, out_specs=[_ANY] * n,
        out_shape=[jax.ShapeDtypeStruct(b.shape, b.dtype) for b in bigs]
        + [jax.ShapeDtypeStruct((N_CHIPS,) + s.shape, s.dtype) for s in smalls],
        scratch_shapes=[pltpu.SemaphoreType.DMA((n, 3)), pltpu.SemaphoreType.DMA((n, 3)),
                        pltpu.SemaphoreType.DMA((n,))],
    )(*bigs, *smalls)


_HBM = pl.BlockSpec(memory_space=pltpu.HBM)
_SEM = pl.BlockSpec(memory_space=pltpu.SEMAPHORE)
_VMEM = pl.BlockSpec(memory_space=pltpu.VMEM)
_EFFECT = pltpu.SideEffectType.DATAFLOW_SIDE_EFFECTING
_TOKEN = jax.ShapeDtypeStruct((8, 128), F32)


def _in_hbm(a):
    return pltpu.with_memory_space_constraint(a, pltpu.HBM)


def _split_call(name, body, n_sems_out, arrays, sems_in=(), after=None):
    na, ns = len(arrays), len(sems_in)
    has_after = after is not None

    def kernel_body(*refs):
        arr = refs[:na]
        s_in = refs[na:na + ns]
        outs = refs[na + ns + has_after:]
        body(arr, s_in, outs[:n_sems_out])
        outs[-1][...] = jnp.zeros((8, 128), F32)

    out_shape = ([pltpu.SemaphoreType.DMA(())] * n_sems_out + [pltpu.HBM(a.shape, a.dtype) for a in arrays] + [_TOKEN])
    res = pl.pallas_call(
        kernel_body, name=name, out_shape=out_shape,
        in_specs=[_HBM] * na + [_SEM] * ns + [_ANY] * has_after,
        out_specs=[_SEM] * n_sems_out + [_HBM] * na + [_VMEM],
        input_output_aliases={i: n_sems_out + i for i in range(na)},
        compiler_params=pltpu.CompilerParams(has_side_effects=_EFFECT),
    )(*[_in_hbm(a) for a in arrays], *sems_in, *([after] if has_after else []))
    return list(res[:n_sems_out]), list(res[n_sems_out:n_sems_out + na]), res[-1]


def _wait_bytes_of(ref, send_sem, recv_sem, peer, send=True, recv=True):
    cp = pltpu.make_async_remote_copy(src_ref=ref, dst_ref=ref, send_sem=send_sem, recv_sem=recv_sem,
                                      device_id=peer, device_id_type=MESH)
    if send:
        cp.wait_send()
    if recv:
        cp.wait_recv()


def _gather_behind(tag, shards, after, work, by_columns=()):
    n = len(shards)
    lands = [lax.empty((s.shape[0], N_DEV * s.shape[1]) if j in by_columns else (N_DEV,) + s.shape, s.dtype)
             for j, s in enumerate(shards)]

    def slots(arr, j, first, count=1):
        if j in by_columns:
            cols = shards[j].shape[1]
            return arr[n + j].at[:, pl.ds(pl.multiple_of(first * cols, 128), count * cols)]
        return arr[n + j].at[first] if count == 1 else arr[n + j].at[pl.ds(first, count)]

    def own_slot(arr, j, sem):
        x, y, c = _coords()
        return pltpu.make_async_copy(arr[j], slots(arr, j, 4 * x + 2 * y + c), sem)

    def start(arr, _, sems):
        x, y, c = _coords()
        me = 4 * x + 2 * y + c
        for j in range(n):
            for chip in [(1 - x, y), (x, 1 - y), (1 - x, 1 - y)]:
                pltpu.make_async_remote_copy(src_ref=arr[j], dst_ref=slots(arr, j, me), send_sem=sems[j],
                                             recv_sem=sems[n + j], device_id=(*chip, c), device_id_type=MESH).start()
        for j in range(n):
            pltpu.make_async_remote_copy(src_ref=arr[j], dst_ref=slots(arr, j, me), send_sem=sems[2 * n + j],
                                         recv_sem=sems[3 * n + j], device_id=(x, y, 1 - c),
                                         device_id_type=MESH).start()
            own_slot(arr, j, sems[4 * n + j]).start()

    def middle(arr, s_in, sems):
        x, y, c = _coords()
        sibling = (x, y, 1 - c)
        for j in range(n):
            _wait_bytes_of(slots(arr, j, 0, 3), s_in[j], s_in[n + j], sibling)
        for j in range(n):
            for chip in [(1 - x, y), (x, 1 - y), (1 - x, 1 - y)]:
                slot = slots(arr, j, 4 * chip[0] + 2 * chip[1] + c)
                pltpu.make_async_remote_copy(src_ref=slot, dst_ref=slot, send_sem=sems[j], recv_sem=sems[n + j],
                                             device_id=sibling, device_id_type=MESH).start()

    def finish(arr, s_in, _):
        x, y, c = _coords()
        sibling = (x, y, 1 - c)
        for j in range(n):
            _wait_bytes_of(slots(arr, j, 0, 1), s_in[j], s_in[n + j], sibling)
            own_slot(arr, j, s_in[2 * n + j]).wait()
            _wait_bytes_of(slots(arr, j, 0, 3), s_in[3 * n + j], s_in[4 * n + j], sibling)

    sems, arrays, token = _split_call("gather_%s_start" % tag, start, 5 * n, list(shards) + lands, after=after)
    result = work(token)
    fwd_sems, arrays, token = _split_call("gather_%s_middle" % tag, middle, 2 * n, arrays, sems_in=sems[:2 * n],
                                          after=result[0])
    _, arrays, _ = _split_call("gather_%s_finish" % tag, finish, 0, arrays, sems_in=sems[2 * n:] + fwd_sems,
                               after=token)
    return arrays[n:], result


def _chip_exchange_behind(tag, bigs, smalls, work):
    nb, n = len(bigs), len(bigs) + len(smalls)
    lands = [lax.empty(s.shape, s.dtype) for s in bigs] + [lax.empty((N_CHIPS,) + s.shape, s.dtype) for s in smalls]

    def own_slot(arr, j, sem):
        x, y, _ = _coords()
        q_me = 2 * x + y
        return pltpu.make_async_copy(arr[j].at[q_me] if j < nb else arr[j], arr[n + j].at[q_me], sem)

    def start(arr, _, sems):
        x, y, c = _coords()
        q_me = 2 * x + y
        for j in range(n):
            for peer in [(1 - x, y, c), (x, 1 - y, c), (1 - x, 1 - y, c)]:
                piece = arr[j].at[2 * peer[0] + peer[1]] if j < nb else arr[j]
                pltpu.make_async_remote_copy(src_ref=piece, dst_ref=arr[n + j].at[q_me],
                                             send_sem=sems[j], recv_sem=sems[n + j], device_id=peer,
                                             device_id_type=MESH).start()
            own_slot(arr, j, sems[2 * n + j]).start()

    def finish(arr, s_in, _):
        x, y, c = _coords()
        for j in range(n):
            _wait_bytes_of(arr[n + j].at[pl.ds(0, 3)], s_in[j], s_in[n + j], (x, y, 1 - c))
            own_slot(arr, j, s_in[2 * n + j]).wait()

    sems, arrays, token = _split_call("chip_exchange_%s_start" % tag, start, 3 * n, list(bigs) + list(smalls) + lands)
    result = work(token)
    _, arrays, _ = _split_call("chip_exchange_%s_finish" % tag, finish, 0, arrays, sems_in=sems, after=result[0])
    return arrays[n:], result


def _pair_sum_big(name, c_arr, mine, theirs):
    _, rows, cols = theirs.shape
    tr = rows

    def body(c_ref, a_ref, b_ref, o_ref):
        del c_ref
        o_ref[...] = (a_ref[...].astype(F32) + b_ref[...].astype(F32)).astype(BF16)

    return pl.pallas_call(
        body, name=name,
        grid_spec=pltpu.PrefetchScalarGridSpec(
            num_scalar_prefetch=1, grid=(N_CHIPS, rows // tr),
            in_specs=[pl.BlockSpec((None, tr, cols), lambda q, r, c_ref: (4 * c_ref[0] + q, r, 0)),
                      pl.BlockSpec((None, tr, cols), lambda q, r, c_ref: (q, r, 0))],
            out_specs=pl.BlockSpec((None, tr, cols), lambda q, r, c_ref: (q, r, 0))),
        out_shape=jax.ShapeDtypeStruct(theirs.shape, BF16),
        compiler_params=pltpu.CompilerParams(dimension_semantics=("arbitrary", "arbitrary")),
    )(c_arr, mine, theirs)


def _pair_sum_small(name, mine, theirs):
    n = len(mine)

    def body(*refs):
        for j in range(n):
            refs[2 * n + j][...] = refs[j][...] + refs[n + j][...]

    return pl.pallas_call(
        body, name=name,
        out_shape=[jax.ShapeDtypeStruct(m.shape, m.dtype) for m in mine],
    )(*mine, *theirs)


def _adamw_math(w, g, m, v):
    m = ADAM_B1 * m + (1.0 - ADAM_B1) * g
    v = ADAM_B2 * v + (1.0 - ADAM_B2) * (g * g)
    m_hat = m / (1.0 - ADAM_B1 ** ADAM_STEP)
    v_hat = v / (1.0 - ADAM_B2 ** ADAM_STEP)
    delta = -ADAM_LR * (m_hat / (jnp.sqrt(v_hat) + ADAM_EPS) + ADAM_WD * w)
    return delta, m, v


def _adamw_big(name, after, chip_sums, w, m, v):
    rows, cols = w.shape
    tr = min(rows, 512)

    def body(after_ref, t_ref, w_ref, m_ref, v_ref, g_out, d_out, m_out, v_out):
        del after_ref
        g = t_ref[0].astype(F32)
        for q in range(1, N_CHIPS):
            g = g + t_ref[q].astype(F32)
        d, mn, vn = _adamw_math(w_ref[...], g, m_ref[...], v_ref[...])
        g_out[...] = g
        d_out[...] = d
        m_out[...] = mn
        v_out[...] = vn

    blk = pl.BlockSpec((tr, cols), lambda r: (r, 0))
    return pl.pallas_call(
        body, name=name, grid=(rows // tr,),
        in_specs=[_ANY, pl.BlockSpec((N_CHIPS, tr, cols), lambda r: (0, r, 0)), blk, blk, blk],
        out_specs=[blk] * 4,
        out_shape=[jax.ShapeDtypeStruct((rows, cols), F32)] * 4,
        compiler_params=pltpu.CompilerParams(dimension_semantics=("arbitrary",), vmem_limit_bytes=VMEM_LIMIT),
    )(after, chip_sums, w, m, v)


SMALL_ROWS = ("loss", "norm1_pre_g", "pool_scale", "sgu_ln_g", "sgu_ln_b", "norm1_post_g", "norm2_pre_g",
              "norm2_post_g")


def _adamw_small(u_rows, u_bin, u_ws, u_bs, params):
    n = len(params)

    def body(*refs):
        urow_ref, ubin_ref, uws_ref, ubs_ref = refs[:4]
        wmv = refs[4:4 + 3 * n]
        loss_ref = refs[4 + 3 * n]
        outs = refs[5 + 3 * n:]

        def total(ref, idx):
            g = ref[(0,) + idx]
            for q in range(1, N_CHIPS):
                g = g + ref[(q,) + idx]
            return g

        loss_ref[...] = total(urow_ref, (slice(0, 1), slice(None)))
        for p in range(n):
            if p < 7:
                g = total(urow_ref, (slice(p + 1, p + 2), slice(None)))
            else:
                g = total((ubin_ref, uws_ref, ubs_ref)[p - 7], (slice(None), slice(None)))
            d, mn, vn = _adamw_math(wmv[3 * p][...], g, wmv[3 * p + 1][...], wmv[3 * p + 2][...])
            outs[4 * p][...] = g
            outs[4 * p + 1][...] = d
            outs[4 * p + 2][...] = mn
            outs[4 * p + 3][...] = vn

    flat = [a for p in params for a in p]
    out_shape = [jax.ShapeDtypeStruct((1, D_MODEL), F32)]
    for w, _, _ in params:
        out_shape += [jax.ShapeDtypeStruct(w.shape, F32)] * 4
    return pl.pallas_call(body, name="adamw_small", out_shape=out_shape)(u_rows, u_bin, u_ws, u_bs, *flat)


def _slots_of_rows(full):
    owners = [_owner_of_slot(s) for s in range(N_DEV)]
    parts = full.reshape(N_GROUPS, N_DEV, PG_SHARD, GROUP)
    return jnp.stack([parts[:, o] for o in owners]).reshape(N_DEV, N_GROUPS * PG_SHARD, GROUP).astype(BF16)


def kernel(x, norm1_pre_g, w_in, b_in, w_pool, pool_scale, sgu_ln_g, sgu_ln_b, w_spatial, b_spatial, w_sgu_proj, w_out, norm1_post_g, norm2_pre_g, w_ff1, w_ff2, norm2_post_g, loss_target, m_norm1_pre_g, m_w_in, m_b_in, m_w_pool, m_pool_scale, m_sgu_ln_g, m_sgu_ln_b, m_w_spatial, m_b_spatial, m_w_sgu_proj, m_w_out, m_norm1_post_g, m_norm2_pre_g, m_w_ff1, m_w_ff2, m_norm2_post_g, v_norm1_pre_g, v_w_in, v_b_in, v_w_pool, v_pool_scale, v_sgu_ln_g, v_sgu_ln_b, v_w_spatial, v_b_spatial, v_w_sgu_proj, v_w_out, v_norm1_post_g, v_norm2_pre_g, v_w_ff1, v_w_ff2, v_norm2_post_g):
    t_len = x.shape[1]
    row = lambda a: a.reshape(1, -1)
    x2 = x.reshape(t_len, D_MODEL)
    tgt2 = loss_target.reshape(t_len, D_MODEL)
    pg2 = lambda a: a.reshape(N_GROUPS * PG_SHARD, GROUP)

    def prework(token):
        return _prenorm(token, x2, row(norm1_pre_g))

    (win_f, g_pool, g_proj, g_out), (xn, xn_b) = _gather_behind(
        "mix", [w_in.astype(BF16), pg2(w_pool).astype(BF16), pg2(w_sgu_proj).astype(BF16), w_out.astype(BF16)],
        None, prework, by_columns=(0,))
    regroup = lambda g: g.reshape(N_DEV, N_GROUPS, PG_SHARD, GROUP).transpose(1, 0, 2, 3).reshape(N_GROUPS, GROUP, GROUP)
    wpool_f, wproj_f = regroup(g_pool), regroup(g_proj)
    wout_f = g_out.reshape(D_MODEL, D_MODEL)
    bsp_t = b_spatial.T

    def forward(token):
        z, y, h1, *saved = _fwd_mix(token, xn, x2, win_f, row(b_in), wpool_f, row(pool_scale),
                                    row(sgu_ln_g), row(sgu_ln_b), w_spatial, bsp_t, wproj_f, wout_f,
                                    row(norm1_post_g))
        return h1, z, y, saved

    (g_ff1, g_ff2), (h1, z, y, saved) = _gather_behind("ff", [w_ff1.astype(BF16), w_ff2.astype(BF16)], win_f,
                                                       forward)
    w2_f = g_ff2.reshape(D_FF, D_MODEL)
    hn_b, f_b, df1_b, df2_b, dh1, dg2post, dg2pre, loss_p = _mlp(h1, tgt2, row(norm2_pre_g), row(norm2_post_g),
                                                               g_ff1, w2_f)
    p_ff2 = _wgrad("wgrad_ff2", f_b, df2_b, FF_SHARD, True)
    p_ff1 = _wgrad("wgrad_ff1", hn_b, df1_b, FF_SHARD, False)
    c_arr = lax.axis_index("c").astype(jnp.int32).reshape(1)
    ff_parts = [p_ff1, p_ff2]
    got_ff = _pair_exchange("pair_exchange_ff", ff_parts, [])
    chip_ff = [_pair_sum_big("pair_sum_" + nm, c_arr, b, r) for nm, b, r in zip(("ff1", "ff2"), ff_parts, got_ff)]

    def backward_mix(token):
        return _bwd_mix(token, dh1, y, z, saved, wpool_f, row(pool_scale), row(sgu_ln_g), row(sgu_ln_b), w_spatial,
                        bsp_t, wproj_f, wout_f, row(norm1_post_g))

    summed_ff, (dz_b, mg_b, dy_b, dwpool, dwproj, dws, dbsp_t, dg1post, dps, dlng, dlnb,
                dbin) = _chip_exchange_behind("ff", chip_ff, [], backward_mix)

    p_in = _wgrad_in(xn_b, dz_b)
    bigs = [p_in, _slots_of_rows(dwpool), _slots_of_rows(dwproj)]
    rows = jnp.concatenate([jnp.broadcast_to(loss_p[:, 0:1], (1, D_MODEL)), dps, dlng, dlnb, dg1post, dg2pre, dg2post],
                           axis=0)
    smalls = [rows, dbin, dws.reshape(N_HEADS * SGU_BLOCK, SGU_BLOCK), dbsp_t.T]
    got = _pair_exchange("pair_exchange_in", bigs, smalls)
    chip_bigs = [_pair_sum_big("pair_sum_" + nm, c_arr, b, r) for nm, b, r in zip(("in", "pool", "proj"), bigs, got[:3])]
    chip_smalls = _pair_sum_small("pair_sum_small", smalls, got[3:])

    def backward_rest(token):
        dx, dg1pre = _bwd_in(token, dz_b, x2, dh1, row(norm1_pre_g), win_f)
        p_out = _wgrad("wgrad_out", mg_b, dy_b, OUT_SHARD, True)
        got_out = _pair_exchange("pair_exchange_out", [p_out], [dg1pre])
        chip_out = _pair_sum_big("pair_sum_out", c_arr, p_out, got_out[0])
        upd_ff = [_adamw_big("adamw_" + nm, p_out, t, *p) for nm, t, p in
                  (("ff1", summed_ff[0], (w_ff1, m_w_ff1, v_w_ff1)), ("ff2", summed_ff[1], (w_ff2, m_w_ff2, v_w_ff2)))]
        return upd_ff[1][0], dx, chip_out, _pair_sum_small("pair_sum_g1pre", [dg1pre], got_out[1:])[0], upd_ff

    summed_in, (_, dx, chip_out, chip_g1pre, upd_ff) = _chip_exchange_behind("in", chip_bigs, chip_smalls,
                                                                             backward_rest)

    def update_in(token):
        res = _adamw_big("adamw_in", token, summed_in[0], w_in, m_w_in, v_w_in)
        return res[0], res

    summed_out, (_, upd_in) = _chip_exchange_behind("out", [chip_out], [chip_g1pre], update_in)
    big = {"in": upd_in, "ff1": upd_ff[0], "ff2": upd_ff[1]}
    for nm, t, p in (("pool", summed_in[1], (pg2(w_pool), pg2(m_w_pool), pg2(v_w_pool))),
                     ("proj", summed_in[2], (pg2(w_sgu_proj), pg2(m_w_sgu_proj), pg2(v_w_sgu_proj))),
                     ("out", summed_out[0], (w_out, m_w_out, v_w_out))):
        big[nm] = _adamw_big("adamw_" + nm, t, t, *p)
    u_rows = jnp.concatenate([summed_in[3][:, 0:1], summed_out[1], summed_in[3][:, 1:]], axis=1)
    ws2 = lambda a: a.reshape(N_HEADS * SGU_BLOCK, SGU_BLOCK)
    small_params = [(row(norm1_pre_g), row(m_norm1_pre_g), row(v_norm1_pre_g)),
                    (row(pool_scale), row(m_pool_scale), row(v_pool_scale)),
                    (row(sgu_ln_g), row(m_sgu_ln_g), row(v_sgu_ln_g)),
                    (row(sgu_ln_b), row(m_sgu_ln_b), row(v_sgu_ln_b)),
                    (row(norm1_post_g), row(m_norm1_post_g), row(v_norm1_post_g)),
                    (row(norm2_pre_g), row(m_norm2_pre_g), row(v_norm2_pre_g)),
                    (row(norm2_post_g), row(m_norm2_post_g), row(v_norm2_post_g)),
                    (row(b_in), row(m_b_in), row(v_b_in)),
                    (ws2(w_spatial), ws2(m_w_spatial), ws2(v_w_spatial)),
                    (b_spatial, m_b_spatial, v_b_spatial)]
    small_out = _adamw_small(u_rows, *summed_in[4:], small_params)
    loss = small_out[0][0, 0]
    small_names = SMALL_ROWS[1:] + ("b_in", "w_spatial", "b_spatial")
    small = {nm: small_out[1 + 4 * p:5 + 4 * p] for p, nm in enumerate(small_names)}

    shapes = {"norm1_pre_g": norm1_pre_g.shape, "w_in": w_in.shape, "b_in": b_in.shape, "w_pool": w_pool.shape,
              "pool_scale": pool_scale.shape, "sgu_ln_g": sgu_ln_g.shape, "sgu_ln_b": sgu_ln_b.shape,
              "w_spatial": w_spatial.shape, "b_spatial": b_spatial.shape, "w_sgu_proj": w_sgu_proj.shape,
              "w_out": w_out.shape, "norm1_post_g": norm1_post_g.shape, "norm2_pre_g": norm2_pre_g.shape,
              "w_ff1": w_ff1.shape, "w_ff2": w_ff2.shape, "norm2_post_g": norm2_post_g.shape}
    source = {"w_in": big["in"], "w_pool": big["pool"], "w_sgu_proj": big["proj"], "w_out": big["out"],
              "w_ff1": big["ff1"], "w_ff2": big["ff2"], **small}
    order = list(shapes)
    outs = [loss, dx.reshape(x.shape)]
    for kind in range(4):
        outs += [source[nm][kind].reshape(shapes[nm]) for nm in order]
    return tuple(outs)
```

```python
import functools
import math

import jax
import jax.numpy as jnp
from jax import lax
from jax.experimental import pallas as pl
from jax.experimental.pallas import tpu as pltpu

F32, BF16 = jnp.float32, jnp.bfloat16
MESH = pl.DeviceIdType.MESH

D_MODEL = 1024
D_IN = 5120
D_FF = 4096
N_DEV = 8
N_CHIPS = 4
WINDOWS = (2, 4, 8, 16)
N_GROUPS = 4
GROUP = 256
HALO = 16
SGU_BLOCK = 128
N_HEADS = 4
HEAD = 256
CHUNK = 64
EPS = 1e-6
IN_SHARD = D_IN // N_DEV
FF_SHARD = D_FF // N_DEV
OUT_SHARD = D_MODEL // N_DEV
PG_SHARD = GROUP // N_DEV

ADAM_LR, ADAM_B1, ADAM_B2, ADAM_EPS, ADAM_WD, ADAM_STEP = 0.001, 0.9, 0.999, 1e-08, 0.01, 10

VMEM_LIMIT = 56 * 1024 * 1024
TM = 256
TM_BWD = 256
TM_IN = 512
PROJ_COLS = 512
GELU_C0 = math.sqrt(2.0 / math.pi)
GELU_C1 = 0.044715


def _dot(a, b):
    return jnp.dot(a, b, preferred_element_type=F32)


def _dot_nt(a, b):
    return lax.dot_general(a, b, (((1,), (1,)), ((), ())), preferred_element_type=F32)


def _dot_tn(a, b):
    return lax.dot_general(a, b, (((0,), (0,)), ((), ())), preferred_element_type=F32)


def _gelu(x):
    t = jnp.tanh(GELU_C0 * (x + GELU_C1 * (x * x * x)))
    return 0.5 * x * (1.0 + t), t


def _gelu_grad(x, t):
    return 0.5 * (1.0 + t) + 0.5 * x * (1.0 - t * t) * (GELU_C0 * (1.0 + 3.0 * GELU_C1 * (x * x)))


def _sigmoid(x):
    return 1.0 / (1.0 + jnp.exp(-x))


def _mean(x):
    return jnp.mean(x, axis=-1, keepdims=True)


def _colsum(x):
    return jnp.sum(x, axis=0, keepdims=True)


def _const_spec(shape):
    nd = len(shape)
    return pl.BlockSpec(shape, lambda *_: (0,) * nd, pipeline_mode=pl.Buffered(1))


def _acc_spec(shape):
    nd = len(shape)
    return pl.BlockSpec(shape, lambda *_: (0,) * nd)


def _masked_ws(ws_ref):
    ri = lax.broadcasted_iota(jnp.int32, (SGU_BLOCK, SGU_BLOCK), 0) // CHUNK
    ci = lax.broadcasted_iota(jnp.int32, (SGU_BLOCK, SGU_BLOCK), 1) // CHUNK
    return [jnp.where(ri >= ci, ws_ref[h], 0.0).astype(BF16) for h in range(N_HEADS)]


def _pool_fwd(pbuf, tile_idx, tm):
    pos = lax.broadcasted_iota(jnp.int32, (tm, 1), 0) + tile_idx * tm + 1
    pooled = []
    for g, w in enumerate(WINDOWS):
        e = pbuf[:, g * GROUP:(g + 1) * GROUP]
        s, sh = e, 1
        while sh < w:
            s = s + pltpu.roll(s, sh, 0)
            sh *= 2
        inv = 1.0 / jnp.minimum(pos, w).astype(F32)
        pooled.append(s[HALO:] * inv - e[HALO:])
    return pooled


def _sgu_fwd(zu, zv, lng, lnb, wsm, bsp_ref, tm):
    u, tu = _gelu(zu)
    gv, tv = _gelu(zv)
    xc = gv - _mean(gv)
    rln = lax.rsqrt(_mean(xc * xc) + EPS)
    xhat = xc * rln
    vb = (xhat * lng + lnb).astype(BF16)
    sv_heads = []
    for h in range(N_HEADS):
        rows = []
        for n in range(tm // SGU_BLOCK):
            blk = vb[n * SGU_BLOCK:(n + 1) * SGU_BLOCK, h * HEAD:(h + 1) * HEAD]
            rows.append(_dot(wsm[h], blk) + bsp_ref[:, h:h + 1])
        sv_heads.append(jnp.concatenate(rows, axis=0))
    return u, tu, tv, xhat, rln, vb, sv_heads


def _prenorm(after, x, g1pre):
    t_len = x.shape[0]
    tm = TM_IN

    def body(after_ref, x_ref, g_ref, xn_ref, xnt_ref):
        del after_ref
        xv = x_ref[...]
        xnb = (xv * lax.rsqrt(_mean(xv * xv) + EPS) * g_ref[...]).astype(BF16)
        xn_ref[...] = xnb
        xnt_ref[...] = xnb.T

    return pl.pallas_call(
        body, name="prenorm", grid=(t_len // tm,),
        in_specs=[_ANY, pl.BlockSpec((tm, D_MODEL), lambda i: (i, 0)), _const_spec((1, D_MODEL))],
        out_specs=[pl.BlockSpec((tm, D_MODEL), lambda i: (i, 0)), pl.BlockSpec((D_MODEL, tm), lambda i: (0, i))],
        out_shape=[jax.ShapeDtypeStruct((t_len, D_MODEL), BF16), jax.ShapeDtypeStruct((D_MODEL, t_len), BF16)],
        compiler_params=pltpu.CompilerParams(dimension_semantics=("arbitrary",)),
    )(after, x, g1pre)


def _fwd_mix(after, xn, x, win_g, b_in, wpool, pool_scale, lng, lnb, ws, bsp_t, wproj, wout, g1post):
    t_len = x.shape[0]
    tm = TM
    nt = t_len // tm

    def body(after_ref, xn_ref, xb_ref, win_ref, bin_ref, wpool_ref, ps_ref, lng_ref, lnb_ref, ws_ref, bsp_ref,
             wproj_ref, wout_ref, g1post_ref, z_ref, y_ref, h1_ref, u_ref, gpu_ref, xhat_ref, gpv_ref, sa_ref,
             sb_ref, zcur, znext, pbuf):
        del after_ref
        s = pl.program_id(0)

        @pl.when(s == 0)
        def _():
            znext[...] = jnp.zeros((tm, D_IN), F32)
            pbuf[...] = jnp.zeros((tm + HALO, D_MODEL), F32)

        zcur[...] = znext[...]
        xnb = xn_ref[...]

        def project(p):
            cols = slice(p * PROJ_COLS, (p + 1) * PROJ_COLS)
            zp = _dot(xnb, win_ref[:, cols]) + bin_ref[:, cols]
            if (p + 1) * PROJ_COLS <= D_MODEL:
                z_ref[:, cols] = zp
            znext[:, cols] = zp

        project(0)
        pbuf[0:HALO, :] = jnp.where(s <= 1, 0.0, pbuf[0:HALO, :])
        pbuf[HALO:, :] = zcur[:, 0:D_MODEL]
        pooled = _pool_fwd(pbuf, jnp.maximum(s - 1, 0), tm)
        pbuf[0:HALO, :] = pbuf[tm:tm + HALO, :]
        a = jnp.concatenate([_dot(pooled[g].astype(BF16), wpool_ref[g]) for g in range(N_GROUPS)], axis=1)
        a = a * ps_ref[...]
        project(1)
        zu = zcur[:, D_MODEL:2 * D_MODEL]
        u, tu = _gelu(zu)
        u_ref[...] = u.astype(BF16)
        gpu_ref[...] = _gelu_grad(zu, tu).astype(BF16)
        project(2)
        zv = zcur[:, 2 * D_MODEL:3 * D_MODEL]
        gv, tv = _gelu(zv)
        xc = gv - _mean(gv)
        rln = lax.rsqrt(_mean(xc * xc) + EPS)
        xhat = xc * rln
        xhat_ref[...] = xhat.astype(BF16)
        gpv_ref[...] = (_gelu_grad(zv, tv) * rln).astype(BF16)
        vb = (xhat * lng_ref[...] + lnb_ref[...]).astype(BF16)
        wsm = _masked_ws(ws_ref)
        bbr = []
        for h in range(N_HEADS):
            project(3 + h)
            sv = jnp.concatenate(
                [_dot(wsm[h], vb[n * SGU_BLOCK:(n + 1) * SGU_BLOCK, h * HEAD:(h + 1) * HEAD]) + bsp_ref[:, h:h + 1]
                 for n in range(tm // SGU_BLOCK)], axis=0)
            bbr.append(_dot((u[:, h * HEAD:(h + 1) * HEAD] * sv).astype(BF16), wproj_ref[h]))
        bbr = jnp.concatenate(bbr, axis=1)
        project(7)
        sa = _sigmoid(zcur[:, 3 * D_MODEL:4 * D_MODEL])
        sb = _sigmoid(zcur[:, 4 * D_MODEL:5 * D_MODEL])
        sa_ref[...] = sa.astype(BF16)
        sb_ref[...] = sb.astype(BF16)
        project(8)
        yv = _dot((sa * a + sb * bbr).astype(BF16), wout_ref[...])
        y_ref[...] = yv
        project(9)
        ry = lax.rsqrt(_mean(yv * yv) + EPS)
        h1_ref[...] = xb_ref[...] + yv * ry * g1post_ref[...]

    proj = lambda w: pl.BlockSpec((tm, w), lambda s: (jnp.minimum(s, nt - 1), 0))
    mix = lambda w: pl.BlockSpec((tm, w), lambda s: (jnp.maximum(s - 1, 0), 0))
    return pl.pallas_call(
        body, name="fwd_mix", grid=(nt + 1,),
        in_specs=[_ANY, proj(D_MODEL), mix(D_MODEL),
                  _const_spec((D_MODEL, D_IN)),
                  _const_spec((1, D_IN)), _const_spec((N_GROUPS, GROUP, GROUP)), _const_spec((1, D_MODEL)),
                  _const_spec((1, D_MODEL)), _const_spec((1, D_MODEL)),
                  _const_spec((N_HEADS, SGU_BLOCK, SGU_BLOCK)), _const_spec((SGU_BLOCK, N_HEADS)),
                  _const_spec((N_HEADS, HEAD, HEAD)), _const_spec((D_MODEL, D_MODEL)), _const_spec((1, D_MODEL))],
        out_specs=[proj(D_MODEL), mix(D_MODEL), mix(D_MODEL)] + [mix(D_MODEL)] * 6,
        out_shape=[jax.ShapeDtypeStruct((t_len, D_MODEL), F32),
                   jax.ShapeDtypeStruct((t_len, D_MODEL), F32), jax.ShapeDtypeStruct((t_len, D_MODEL), F32)]
        + [jax.ShapeDtypeStruct((t_len, D_MODEL), BF16)] * 6,
        scratch_shapes=[pltpu.VMEM((tm, D_IN), F32), pltpu.VMEM((tm, D_IN), F32),
                        pltpu.VMEM((tm + HALO, D_MODEL), F32)],
        compiler_params=pltpu.CompilerParams(dimension_semantics=("arbitrary",), vmem_limit_bytes=VMEM_LIMIT),
    )(after, xn, x, win_g, b_in, wpool, pool_scale, lng, lnb, ws, bsp_t, wproj, wout, g1post)


def _mlp(h1, target, g2pre, g2post, w1_g, w2):
    t_len = h1.shape[0]
    tm = TM
    nt = t_len // tm

    def body(h1_ref, tgt_ref, g2pre_ref, g2post_ref, w1_ref, w2_ref,
             hn_ref, f_ref, df1_ref, df2_ref, dh1_ref, dg2post_ref, dg2pre_ref, loss_ref, f1_scr):
        i = pl.program_id(0)

        @pl.when(i == 0)
        def _():
            dg2post_ref[...] = jnp.zeros_like(dg2post_ref)
            dg2pre_ref[...] = jnp.zeros_like(dg2pre_ref)
            loss_ref[...] = jnp.zeros_like(loss_ref)

        h = h1_ref[...]
        r2 = lax.rsqrt(_mean(h * h) + EPS)
        nh = h * r2
        hnb = (nh * g2pre_ref[...]).astype(BF16)
        hn_ref[...] = hnb.T
        for k in range(N_DEV):
            f1_scr[:, k * FF_SHARD:(k + 1) * FF_SHARD] = _dot(hnb, w1_ref[k])
        r = jnp.maximum(f1_scr[...], 0.0)
        fb = (r * r).astype(BF16)
        f_ref[...] = fb.T
        f2 = _dot(fb, w2_ref[...])
        rf = lax.rsqrt(_mean(f2 * f2) + EPS)
        nf = f2 * rf
        diff = h + nf * g2post_ref[...] - tgt_ref[...]
        loss_ref[...] += (0.5 / D_MODEL) * jnp.sum(diff * diff)
        dout = diff * (1.0 / D_MODEL)
        dg2post_ref[...] += _colsum(dout * nf)
        dn = dout * g2post_ref[...]
        df2b = (rf * (dn - nf * _mean(dn * nf))).astype(BF16)
        df2_ref[...] = df2b
        df = _dot_nt(df2b, w2_ref[...])
        df1b = (df * (2.0 * jnp.maximum(f1_scr[...], 0.0))).astype(BF16)
        df1_ref[...] = df1b
        dhn = _dot_nt(df1b[:, 0:FF_SHARD], w1_ref[0])
        for k in range(1, N_DEV):
            dhn = dhn + _dot_nt(df1b[:, k * FF_SHARD:(k + 1) * FF_SHARD], w1_ref[k])
        dg2pre_ref[...] += _colsum(dhn * nh)
        dnh = dhn * g2pre_ref[...]
        dh1_ref[...] = dout + r2 * (dnh - nh * _mean(dnh * nh))

    tok = lambda w: pl.BlockSpec((tm, w), lambda i: (i, 0))
    return pl.pallas_call(
        body, name="mlp_fwd_bwd", grid=(nt,),
        in_specs=[tok(D_MODEL), tok(D_MODEL), _const_spec((1, D_MODEL)), _const_spec((1, D_MODEL)),
                  _const_spec((N_DEV, D_MODEL, FF_SHARD)), _const_spec((D_FF, D_MODEL))],
        out_specs=[pl.BlockSpec((D_MODEL, tm), lambda i: (0, i)), pl.BlockSpec((D_FF, tm), lambda i: (0, i)),
                   tok(D_FF), tok(D_MODEL), tok(D_MODEL),
                   _acc_spec((1, D_MODEL)), _acc_spec((1, D_MODEL)), _acc_spec((1, 128))],
        out_shape=[jax.ShapeDtypeStruct((D_MODEL, t_len), BF16), jax.ShapeDtypeStruct((D_FF, t_len), BF16),
                   jax.ShapeDtypeStruct((t_len, D_FF), BF16), jax.ShapeDtypeStruct((t_len, D_MODEL), BF16),
                   jax.ShapeDtypeStruct((t_len, D_MODEL), F32), jax.ShapeDtypeStruct((1, D_MODEL), F32),
                   jax.ShapeDtypeStruct((1, D_MODEL), F32), jax.ShapeDtypeStruct((1, 128), F32)],
        scratch_shapes=[pltpu.VMEM((tm, D_FF), F32)],
        compiler_params=pltpu.CompilerParams(dimension_semantics=("arbitrary",), vmem_limit_bytes=VMEM_LIMIT),
    )(h1, target, g2pre, g2post, w1_g, w2)


def _bwd_mix(after, dh1, y, z, saved, wpool, pool_scale, lng, lnb, ws, bsp_t, wproj, wout, g1post):
    t_len = y.shape[0]
    tm = TM_BWD
    nt = t_len // tm
    nblk = tm // SGU_BLOCK

    def body(after_ref, dh1_ref, y_ref, z_ref, zh_ref, u_ref, gpu_ref, xhat_ref, gpv_ref, sa_ref, sb_ref,
             wpool_ref, ps_ref, lng_ref, lnb_ref, ws_ref,
             bsp_ref, wproj_ref, wout_ref, g1post_ref,
             dz_ref, mg_ref, dy_ref, dwpool_ref, dwproj_ref, dws_ref, dbsp_ref, dg1post_ref, dps_ref,
             dlng_ref, dlnb_ref, dbin_ref, pbuf, qbuf):
        del after_ref
        i = pl.program_id(0)
        ti = nt - 1 - i

        @pl.when(i == 0)
        def _():
            for ref in (dwpool_ref, dwproj_ref, dws_ref, dbsp_ref, dg1post_ref, dps_ref, dlng_ref, dlnb_ref,
                        dbin_ref):
                ref[...] = jnp.zeros_like(ref)
            qbuf[tm:tm + HALO, :] = jnp.zeros((HALO, D_MODEL), F32)

        pbuf[0:HALO, :] = jnp.where(ti > 0, zh_ref[...], 0.0)
        pbuf[HALO:, :] = z_ref[...]
        pooled = _pool_fwd(pbuf, ti, tm)
        pooled_b = [p.astype(BF16) for p in pooled]
        a_raw = jnp.concatenate([_dot(pooled_b[g], wpool_ref[g]) for g in range(N_GROUPS)], axis=1)
        wsm = _masked_ws(ws_ref)
        u = u_ref[...].astype(F32)
        xhat = xhat_ref[...].astype(F32)
        vb = (xhat * lng_ref[...] + lnb_ref[...]).astype(BF16)
        sv_heads = []
        for h in range(N_HEADS):
            sv_heads.append(jnp.concatenate(
                [_dot(wsm[h], vb[n * SGU_BLOCK:(n + 1) * SGU_BLOCK, h * HEAD:(h + 1) * HEAD]) + bsp_ref[:, h:h + 1]
                 for n in range(nblk)], axis=0))
        gated_b = [(u[:, h * HEAD:(h + 1) * HEAD] * sv_heads[h]).astype(BF16) for h in range(N_HEADS)]
        bbr = jnp.concatenate([_dot(gated_b[h], wproj_ref[h]) for h in range(N_HEADS)], axis=1)
        sa = sa_ref[...].astype(F32)
        sb = sb_ref[...].astype(F32)
        a = a_raw * ps_ref[...]
        mg_ref[...] = (sa * a + sb * bbr).astype(BF16).T

        dh = dh1_ref[...]
        yv = y_ref[...]
        ry = lax.rsqrt(_mean(yv * yv) + EPS)
        ny = yv * ry
        dg1post_ref[...] += _colsum(dh * ny)
        dn = dh * g1post_ref[...]
        dyb = (ry * (dn - ny * _mean(dn * ny))).astype(BF16)
        dy_ref[...] = dyb
        dmg = _dot_nt(dyb, wout_ref[...])

        da = dmg * sa
        dbbr = dmg * sb
        dzga = dmg * a * sa * (1.0 - sa)
        dzgb = dmg * bbr * sb * (1.0 - sb)
        dz_ref[:, 3 * D_MODEL:4 * D_MODEL] = dzga.astype(BF16)
        dz_ref[:, 4 * D_MODEL:5 * D_MODEL] = dzgb.astype(BF16)
        dbin_ref[:, 3 * D_MODEL:4 * D_MODEL] += _colsum(dzga)
        dbin_ref[:, 4 * D_MODEL:5 * D_MODEL] += _colsum(dzgb)

        dps_ref[...] += _colsum(da * a_raw)
        da_raw_b = (da * ps_ref[...]).astype(BF16)
        pos = lax.broadcasted_iota(jnp.int32, (tm, 1), 0) + ti * tm + 1
        dpooled = []
        for g, w in enumerate(WINDOWS):
            cols = slice(g * GROUP, (g + 1) * GROUP)
            dwpool_ref[g] += _dot_tn(pooled_b[g], da_raw_b[:, cols])
            dp = _dot_nt(da_raw_b[:, cols], wpool_ref[g])
            dpooled.append(dp)
            qbuf[0:tm, cols] = dp * (1.0 / jnp.minimum(pos, w).astype(F32))
        n_ext = tm + HALO
        dzp = []
        for g, w in enumerate(WINDOWS):
            e = qbuf[:, g * GROUP:(g + 1) * GROUP]
            s, sh = e, 1
            while sh < w:
                s = s + pltpu.roll(s, n_ext - sh, 0)
                sh *= 2
            dzp.append(s[0:tm] - dpooled[g])
        qbuf[tm:tm + HALO, :] = qbuf[0:HALO, :]
        dzp = jnp.concatenate(dzp, axis=1)
        dz_ref[:, 0:D_MODEL] = dzp.astype(BF16)
        dbin_ref[:, 0:D_MODEL] += _colsum(dzp)

        dv_heads = []
        du_heads = []
        for h in range(N_HEADS):
            cols = slice(h * HEAD, (h + 1) * HEAD)
            dbbr_b = dbbr[:, cols].astype(BF16)
            dwproj_ref[h] += _dot_tn(gated_b[h], dbbr_b)
            dgated = _dot_nt(dbbr_b, wproj_ref[h])
            du_heads.append(dgated * sv_heads[h])
            dsv = dgated * u[:, cols]
            dsv_b = dsv.astype(BF16)
            rows = []
            for n in range(nblk):
                blk = slice(n * SGU_BLOCK, (n + 1) * SGU_BLOCK)
                rows.append(_dot_tn(wsm[h], dsv_b[blk]))
                dws_ref[h] += _dot_nt(dsv_b[blk], vb[blk, cols])
                dbsp_ref[:, h:h + 1] += jnp.sum(dsv[blk], axis=1, keepdims=True)
            dv_heads.append(jnp.concatenate(rows, axis=0))
        dzu = jnp.concatenate(du_heads, axis=1) * gpu_ref[...].astype(F32)
        dz_ref[:, D_MODEL:2 * D_MODEL] = dzu.astype(BF16)
        dbin_ref[:, D_MODEL:2 * D_MODEL] += _colsum(dzu)
        dv = jnp.concatenate(dv_heads, axis=1)
        dlng_ref[...] += _colsum(dv * xhat)
        dlnb_ref[...] += _colsum(dv)
        dxh = dv * lng_ref[...]
        dzv = (dxh - _mean(dxh) - xhat * _mean(dxh * xhat)) * gpv_ref[...].astype(F32)
        dz_ref[:, 2 * D_MODEL:3 * D_MODEL] = dzv.astype(BF16)
        dbin_ref[:, 2 * D_MODEL:3 * D_MODEL] += _colsum(dzv)

        @pl.when(i == nt - 1)
        def _():
            ri = lax.broadcasted_iota(jnp.int32, (SGU_BLOCK, SGU_BLOCK), 0) // CHUNK
            ci = lax.broadcasted_iota(jnp.int32, (SGU_BLOCK, SGU_BLOCK), 1) // CHUNK
            for h in range(N_HEADS):
                dws_ref[h] = jnp.where(ri >= ci, dws_ref[h], 0.0)

    tok = lambda w: pl.BlockSpec((tm, w), lambda i: (nt - 1 - i, 0))
    halo = pl.BlockSpec((HALO, D_MODEL), lambda i: (jnp.maximum((nt - 1 - i) * (tm // HALO) - 1, 0), 0))
    return pl.pallas_call(
        body, name="bwd_mix", grid=(nt,),
        in_specs=[_ANY, tok(D_MODEL), tok(D_MODEL), tok(D_MODEL), halo] + [tok(D_MODEL)] * 6
        + [_const_spec((N_GROUPS, GROUP, GROUP)),
                  _const_spec((1, D_MODEL)), _const_spec((1, D_MODEL)), _const_spec((1, D_MODEL)),
                  _const_spec((N_HEADS, SGU_BLOCK, SGU_BLOCK)), _const_spec((SGU_BLOCK, N_HEADS)),
                  _const_spec((N_HEADS, HEAD, HEAD)), _const_spec((D_MODEL, D_MODEL)), _const_spec((1, D_MODEL))],
        out_specs=[tok(D_IN), pl.BlockSpec((D_MODEL, tm), lambda i: (0, nt - 1 - i)), tok(D_MODEL),
                   _acc_spec((N_GROUPS, GROUP, GROUP)), _acc_spec((N_HEADS, HEAD, HEAD)),
                   _acc_spec((N_HEADS, SGU_BLOCK, SGU_BLOCK)), _acc_spec((SGU_BLOCK, N_HEADS)),
                   _acc_spec((1, D_MODEL)), _acc_spec((1, D_MODEL)), _acc_spec((1, D_MODEL)), _acc_spec((1, D_MODEL)),
                   _acc_spec((1, D_IN))],
        out_shape=[jax.ShapeDtypeStruct((t_len, D_IN), BF16),
                   jax.ShapeDtypeStruct((D_MODEL, t_len), BF16), jax.ShapeDtypeStruct((t_len, D_MODEL), BF16),
                   jax.ShapeDtypeStruct((N_GROUPS, GROUP, GROUP), F32), jax.ShapeDtypeStruct((N_HEADS, HEAD, HEAD), F32),
                   jax.ShapeDtypeStruct((N_HEADS, SGU_BLOCK, SGU_BLOCK), F32),
                   jax.ShapeDtypeStruct((SGU_BLOCK, N_HEADS), F32),
                   jax.ShapeDtypeStruct((1, D_MODEL), F32), jax.ShapeDtypeStruct((1, D_MODEL), F32),
                   jax.ShapeDtypeStruct((1, D_MODEL), F32), jax.ShapeDtypeStruct((1, D_MODEL), F32),
                   jax.ShapeDtypeStruct((1, D_IN), F32)],
        scratch_shapes=[pltpu.VMEM((tm + HALO, D_MODEL), F32), pltpu.VMEM((tm + HALO, D_MODEL), F32)],
        compiler_params=pltpu.CompilerParams(dimension_semantics=("arbitrary",), vmem_limit_bytes=VMEM_LIMIT),
    )(after, dh1, y, z, z, *saved, wpool, pool_scale, lng, lnb, ws, bsp_t, wproj, wout, g1post)


def _bwd_in(after, dz_b, x, dh1, g1pre, win_g):
    t_len = x.shape[0]
    tm = TM_IN
    nt = t_len // tm

    def body(after_ref, dz_ref, x_ref, dh1_ref, g1_ref, win_ref, dx_ref, dg1pre_ref):
        del after_ref

        @pl.when(pl.program_id(0) == 0)
        def _():
            dg1pre_ref[...] = jnp.zeros_like(dg1pre_ref)

        dxn = _dot_nt(dz_ref[...], win_ref[...])
        xv = x_ref[...]
        r1 = lax.rsqrt(_mean(xv * xv) + EPS)
        nx = xv * r1
        dg1pre_ref[...] += _colsum(dxn * nx)
        dnx = dxn * g1_ref[...]
        dx_ref[...] = r1 * (dnx - nx * _mean(dnx * nx)) + dh1_ref[...]

    tok = lambda w: pl.BlockSpec((tm, w), lambda i: (i, 0))
    return pl.pallas_call(
        body, name="bwd_in", grid=(nt,),
        in_specs=[_ANY, tok(D_IN), tok(D_MODEL), tok(D_MODEL), _const_spec((1, D_MODEL)),
                  _const_spec((D_MODEL, D_IN))],
        out_specs=[tok(D_MODEL), _acc_spec((1, D_MODEL))],
        out_shape=[jax.ShapeDtypeStruct((t_len, D_MODEL), F32), jax.ShapeDtypeStruct((1, D_MODEL), F32)],
        compiler_params=pltpu.CompilerParams(dimension_semantics=("arbitrary",), vmem_limit_bytes=VMEM_LIMIT),
    )(after, dz_b, x, dh1, g1pre, win_g)


def _owner_of_slot(s):
    return 4 * ((s // 2) % 2) + 2 * (s % 2) + s // 4


def _wgrad(name, a_t, b, shard, a_sliced):
    t_len = b.shape[0]
    am = shard if a_sliced else a_t.shape[0]
    bn = b.shape[1] if a_sliced else shard

    def body(a_ref, b_ref, o_ref):
        o_ref[...] = _dot(a_ref[...], b_ref[...]).astype(BF16)

    if a_sliced:
        in_specs = [pl.BlockSpec((am, t_len), lambda s: (_owner_of_slot(s), 0)),
                    pl.BlockSpec((t_len, bn), lambda s: (0, 0), pipeline_mode=pl.Buffered(1))]
    else:
        in_specs = [pl.BlockSpec((am, t_len), lambda s: (0, 0), pipeline_mode=pl.Buffered(1)),
                    pl.BlockSpec((t_len, bn), lambda s: (0, _owner_of_slot(s)))]
    return pl.pallas_call(
        body, name=name, grid=(N_DEV,), in_specs=in_specs,
        out_specs=pl.BlockSpec((None, am, bn), lambda s: (s, 0, 0)),
        out_shape=jax.ShapeDtypeStruct((N_DEV, am, bn), BF16),
        compiler_params=pltpu.CompilerParams(dimension_semantics=("arbitrary",), vmem_limit_bytes=VMEM_LIMIT),
    )(a_t, b)


def _wgrad_in(xn_t, dz_b):
    t_len = dz_b.shape[0]

    def body(a_ref, b_ref, o_ref):
        res = _dot(a_ref[...], b_ref[...])
        o_ref[0] = res[:, 0:IN_SHARD].astype(BF16)
        o_ref[1] = res[:, IN_SHARD:2 * IN_SHARD].astype(BF16)

    out = pl.pallas_call(
        body, name="wgrad_in", grid=(N_CHIPS,),
        in_specs=[pl.BlockSpec((D_MODEL, t_len), lambda q: (0, 0), pipeline_mode=pl.Buffered(1)),
                  pl.BlockSpec((t_len, 2 * IN_SHARD), lambda q: (0, q))],
        out_specs=pl.BlockSpec((2, None, D_MODEL, IN_SHARD), lambda q: (0, q, 0, 0)),
        out_shape=jax.ShapeDtypeStruct((2, N_CHIPS, D_MODEL, IN_SHARD), BF16),
        compiler_params=pltpu.CompilerParams(dimension_semantics=("arbitrary",), vmem_limit_bytes=VMEM_LIMIT),
    )(xn_t, dz_b)
    return out.reshape(N_DEV, D_MODEL, IN_SHARD)


def _coords():
    return lax.axis_index("x"), lax.axis_index("y"), lax.axis_index("c")


_ANY = pl.BlockSpec(memory_space=pl.ANY)


def _all_gather(shards, by_columns=()):
    n = len(shards)

    def body(*refs):
        src, dst = refs[:n], refs[n:2 * n]
        send_sems, recv_sems, local_sems = refs[2 * n:]
        x, y, c = _coords()
        me, sibling = (x, y, c), (x, y, 1 - c)
        chips = [(1 - x, y), (x, 1 - y), (1 - x, 1 - y)]

        def slot(j, dev):
            idx = 4 * dev[0] + 2 * dev[1] + dev[2]
            if j in by_columns:
                cols = shards[j].shape[1]
                return dst[j].at[:, pl.ds(pl.multiple_of(idx * cols, 128), cols)]
            return dst[j].at[idx]

        def copy(j, k, block, to, from_shard=False):
            return pltpu.make_async_remote_copy(
                src_ref=src[j] if from_shard else slot(j, block), dst_ref=slot(j, block),
                send_sem=send_sems.at[j, k], recv_sem=recv_sems.at[j, k], device_id=to, device_id_type=MESH)

        started = []
        local = []
        for j in range(n):
            for k, to in enumerate([(*chip, c) for chip in chips]):
                cp = copy(j, 1 + k, me, to, from_shard=True)
                cp.start()
                started.append(cp)
        for j in range(n):
            cp = copy(j, 0, me, sibling, from_shard=True)
            cp.start()
            started.append(cp)
            lc = pltpu.make_async_copy(src[j], slot(j, me), local_sems.at[j])
            lc.start()
            local.append(lc)
        for j in range(n):
            for k, chip in enumerate(chips):
                copy(j, 1 + k, (*chip, c), me).wait_recv()
                cp = copy(j, 4 + k, (*chip, c), sibling)
                cp.start()
                started.append(cp)
        for j in range(n):
            copy(j, 0, sibling, me).wait_recv()
            for k, chip in enumerate(chips):
                copy(j, 4 + k, (*chip, 1 - c), me).wait_recv()
        for cp in started:
            cp.wait_send()
        for lc in local:
            lc.wait()

    return pl.pallas_call(
        body, name="all_gather_weights",
        in_specs=[_ANY] * n, out_specs=[_ANY] * n,
        out_shape=[jax.ShapeDtypeStruct((s.shape[0], N_DEV * s.shape[1]) if j in by_columns else (N_DEV,) + s.shape,
                                        s.dtype) for j, s in enumerate(shards)],
        scratch_shapes=[pltpu.SemaphoreType.DMA((n, 7)), pltpu.SemaphoreType.DMA((n, 7)),
                        pltpu.SemaphoreType.DMA((n,))],
    )(*shards)


def _pair_exchange(name, bigs, smalls):
    nb, n = len(bigs), len(bigs) + len(smalls)

    def body(*refs):
        src, dst = refs[:n], refs[n:2 * n]
        send_sems, recv_sems = refs[2 * n:]
        x, y, c = _coords()
        copies = []
        for j in range(n):
            s = src[j].at[pl.ds(4 * (1 - c), 4)] if j < nb else src[j]
            cp = pltpu.make_async_remote_copy(src_ref=s, dst_ref=dst[j], send_sem=send_sems.at[j],
                                              recv_sem=recv_sems.at[j], device_id=(x, y, 1 - c), device_id_type=MESH)
            cp.start()
            copies.append(cp)
        for cp in copies:
            cp.wait()

    return pl.pallas_call(
        body, name=name,
        in_specs=[_ANY] * n, out_specs=[_ANY] * n,
        out_shape=[jax.ShapeDtypeStruct((4,) + b.shape[1:], b.dtype) for b in bigs]
        + [jax.ShapeDtypeStruct(s.shape, s.dtype) for s in smalls],
        scratch_shapes=[pltpu.SemaphoreType.DMA((n,)), pltpu.SemaphoreType.DMA((n,))],
    )(*bigs, *smalls)


def _chip_exchange(bigs, smalls):
    nb, n = len(bigs), len(bigs) + len(smalls)

    def body(*refs):
        src, dst = refs[:n], refs[n:2 * n]
        send_sems, recv_sems, local_sems = refs[2 * n:]
        x, y, c = _coords()
        q_me = 2 * x + y
        peers = [(1 - x, y, c), (x, 1 - y, c), (1 - x, 1 - y, c)]

        def piece(j, q):
            return src[j].at[q] if j < nb else src[j]

        started = []
        for j in range(n):
            for r, peer in enumerate(peers):
                cp = pltpu.make_async_remote_copy(
                    src_ref=piece(j, 2 * peer[0] + peer[1]), dst_ref=dst[j].at[q_me],
                    send_sem=send_sems.at[j, r], recv_sem=recv_sems.at[j, r], device_id=peer, device_id_type=MESH)
                cp.start()
                started.append(cp)
        local = []
        for j in range(n):
            lc = pltpu.make_async_copy(piece(j, q_me), dst[j].at[q_me], local_sems.at[j])
            lc.start()
            local.append(lc)
        for j in range(n):
            for r, peer in enumerate(peers):
                q_peer = 2 * peer[0] + peer[1]
                pltpu.make_async_remote_copy(
                    src_ref=piece(j, q_peer), dst_ref=dst[j].at[q_peer],
                    send_sem=send_sems.at[j, r], recv_sem=recv_sems.at[j, r], device_id=peer,
                    device_id_type=MESH).wait_recv()
        for cp in started:
            cp.wait_send()
        for lc in local:
            lc.wait()

    return pl.pallas_call(
        body, name="chip_exchange",
        in_specs=[_ANY] * n, out_specs=[_ANY] * n,
        out_shape=[jax.ShapeDtypeStruct(b.shape, b.dtype) for b in bigs]
        + [jax.ShapeDtypeStruct((N_CHIPS,) + s.shape, s.dtype) for s in smalls],
        scratch_shapes=[pltpu.SemaphoreType.DMA((n, 3)), pltpu.SemaphoreType.DMA((n, 3)),
                        pltpu.SemaphoreType.DMA((n,))],
    )(*bigs, *smalls)


_HBM = pl.BlockSpec(memory_space=pltpu.HBM)
_SEM = pl.BlockSpec(memory_space=pltpu.SEMAPHORE)
_VMEM = pl.BlockSpec(memory_space=pltpu.VMEM)
_EFFECT = pltpu.SideEffectType.DATAFLOW_SIDE_EFFECTING
_TOKEN = jax.ShapeDtypeStruct((8, 128), F32)


def _in_hbm(a):
    return pltpu.with_memory_space_constraint(a, pltpu.HBM)


def _split_call(name, body, n_sems_out, arrays, sems_in=(), after=None):
    na, ns = len(arrays), len(sems_in)
    has_after = after is not None

    def kernel_body(*refs):
        arr = refs[:na]
        s_in = refs[na:na + ns]
        outs = refs[na + ns + has_after:]
        body(arr, s_in, outs[:n_sems_out])
        outs[-1][...] = jnp.zeros((8, 128), F32)

    out_shape = ([pltpu.SemaphoreType.DMA(())] * n_sems_out + [pltpu.HBM(a.shape, a.dtype) for a in arrays] + [_TOKEN])
    res = pl.pallas_call(
        kernel_body, name=name, out_shape=out_shape,
        in_specs=[_HBM] * na + [_SEM] * ns + [_ANY] * has_after,
        out_specs=[_SEM] * n_sems_out + [_HBM] * na + [_VMEM],
        input_output_aliases={i: n_sems_out + i for i in range(na)},
        compiler_params=pltpu.CompilerParams(has_side_effects=_EFFECT),
    )(*[_in_hbm(a) for a in arrays], *sems_in, *([after] if has_after else []))
    return list(res[:n_sems_out]), list(res[n_sems_out:n_sems_out + na]), res[-1]


def _wait_bytes_of(ref, send_sem, recv_sem, peer, send=True, recv=True):
    cp = pltpu.make_async_remote_copy(src_ref=ref, dst_ref=ref, send_sem=send_sem, recv_sem=recv_sem,
                                      device_id=peer, device_id_type=MESH)
    if send:
        cp.wait_send()
    if recv:
        cp.wait_recv()


def _gather_behind(tag, shards, after, work, by_columns=()):
    n = len(shards)
    lands = [lax.empty((s.shape[0], N_DEV * s.shape[1]) if j in by_columns else (N_DEV,) + s.shape, s.dtype)
             for j, s in enumerate(shards)]

    def slots(arr, j, first, count=1):
        if j in by_columns:
            cols = shards[j].shape[1]
            return arr[n + j].at[:, pl.ds(pl.multiple_of(first * cols, 128), count * cols)]
        return arr[n + j].at[first] if count == 1 else arr[n + j].at[pl.ds(first, count)]

    def own_slot(arr, j, sem):
        x, y, c = _coords()
        return pltpu.make_async_copy(arr[j], slots(arr, j, 4 * x + 2 * y + c), sem)

    def start(arr, _, sems):
        x, y, c = _coords()
        me = 4 * x + 2 * y + c
        for j in range(n):
            for chip in [(1 - x, y), (x, 1 - y), (1 - x, 1 - y)]:
                pltpu.make_async_remote_copy(src_ref=arr[j], dst_ref=slots(arr, j, me), send_sem=sems[j],
                                             recv_sem=sems[n + j], device_id=(*chip, c), device_id_type=MESH).start()
        for j in range(n):
            pltpu.make_async_remote_copy(src_ref=arr[j], dst_ref=slots(arr, j, me), send_sem=sems[2 * n + j],
                                         recv_sem=sems[3 * n + j], device_id=(x, y, 1 - c),
                                         device_id_type=MESH).start()
            own_slot(arr, j, sems[4 * n + j]).start()

    def middle(arr, s_in, sems):
        x, y, c = _coords()
        sibling = (x, y, 1 - c)
        for j in range(n):
            _wait_bytes_of(slots(arr, j, 0, 3), s_in[j], s_in[n + j], sibling)
            for chip in [(1 - x, y), (x, 1 - y), (1 - x, 1 - y)]:
                slot = slots(arr, j, 4 * chip[0] + 2 * chip[1] + c)
                pltpu.make_async_remote_copy(src_ref=slot, dst_ref=slot, send_sem=sems[j], recv_sem=sems[n + j],
                                             device_id=sibling, device_id_type=MESH).start()

    def finish(arr, s_in, _):
        x, y, c = _coords()
        sibling = (x, y, 1 - c)
        for j in range(n):
            _wait_bytes_of(slots(arr, j, 0, 1), s_in[j], s_in[n + j], sibling)
            own_slot(arr, j, s_in[2 * n + j]).wait()
            _wait_bytes_of(slots(arr, j, 0, 3), s_in[3 * n + j], s_in[4 * n + j], sibling)

    sems, arrays, token = _split_call("gather_%s_start" % tag, start, 5 * n, list(shards) + lands, after=after)
    result = work(token)
    fwd_sems, arrays, token = _split_call("gather_%s_middle" % tag, middle, 2 * n, arrays, sems_in=sems[:2 * n],
                                          after=result[0])
    _, arrays, _ = _split_call("gather_%s_finish" % tag, finish, 0, arrays, sems_in=sems[2 * n:] + fwd_sems,
                               after=token)
    return arrays[n:], result


def _chip_exchange_behind(tag, bigs, smalls, work):
    nb, n = len(bigs), len(bigs) + len(smalls)
    lands = [lax.empty(s.shape, s.dtype) for s in bigs] + [lax.empty((N_CHIPS,) + s.shape, s.dtype) for s in smalls]

    def own_slot(arr, j, sem):
        x, y, _ = _coords()
        q_me = 2 * x + y
        return pltpu.make_async_copy(arr[j].at[q_me] if j < nb else arr[j], arr[n + j].at[q_me], sem)

    def start(arr, _, sems):
        x, y, c = _coords()
        q_me = 2 * x + y
        for j in range(n):
            for peer in [(1 - x, y, c), (x, 1 - y, c), (1 - x, 1 - y, c)]:
                piece = arr[j].at[2 * peer[0] + peer[1]] if j < nb else arr[j]
                pltpu.make_async_remote_copy(src_ref=piece, dst_ref=arr[n + j].at[q_me],
                                             send_sem=sems[j], recv_sem=sems[n + j], device_id=peer,
                                             device_id_type=MESH).start()
            own_slot(arr, j, sems[2 * n + j]).start()

    def finish(arr, s_in, _):
        x, y, c = _coords()
        for j in range(n):
            _wait_bytes_of(arr[n + j].at[pl.ds(0, 3)], s_in[j], s_in[n + j], (x, y, 1 - c))
            own_slot(arr, j, s_in[2 * n + j]).wait()

    sems, arrays, token = _split_call("chip_exchange_%s_start" % tag, start, 3 * n, list(bigs) + list(smalls) + lands)
    result = work(token)
    _, arrays, _ = _split_call("chip_exchange_%s_finish" % tag, finish, 0, arrays, sems_in=sems, after=result[0])
    return arrays[n:], result


def _pair_sum_big(name, c_arr, mine, theirs):
    _, rows, cols = theirs.shape
    tr = rows

    def body(c_ref, a_ref, b_ref, o_ref):
        del c_ref
        o_ref[...] = (a_ref[...].astype(F32) + b_ref[...].astype(F32)).astype(BF16)

    return pl.pallas_call(
        body, name=name,
        grid_spec=pltpu.PrefetchScalarGridSpec(
            num_scalar_prefetch=1, grid=(N_CHIPS, rows // tr),
            in_specs=[pl.BlockSpec((None, tr, cols), lambda q, r, c_ref: (4 * c_ref[0] + q, r, 0)),
                      pl.BlockSpec((None, tr, cols), lambda q, r, c_ref: (q, r, 0))],
            out_specs=pl.BlockSpec((None, tr, cols), lambda q, r, c_ref: (q, r, 0))),
        out_shape=jax.ShapeDtypeStruct(theirs.shape, BF16),
        compiler_params=pltpu.CompilerParams(dimension_semantics=("arbitrary", "arbitrary")),
    )(c_arr, mine, theirs)


def _pair_sum_small(name, mine, theirs):
    n = len(mine)

    def body(*refs):
        for j in range(n):
            refs[2 * n + j][...] = refs[j][...] + refs[n + j][...]

    return pl.pallas_call(
        body, name=name,
        out_shape=[jax.ShapeDtypeStruct(m.shape, m.dtype) for m in mine],
    )(*mine, *theirs)


def _adamw_math(w, g, m, v):
    m = ADAM_B1 * m + (1.0 - ADAM_B1) * g
    v = ADAM_B2 * v + (1.0 - ADAM_B2) * (g * g)
    m_hat = m / (1.0 - ADAM_B1 ** ADAM_STEP)
    v_hat = v / (1.0 - ADAM_B2 ** ADAM_STEP)
    delta = -ADAM_LR * (m_hat / (jnp.sqrt(v_hat) + ADAM_EPS) + ADAM_WD * w)
    return delta, m, v


def _adamw_big(name, after, chip_sums, w, m, v):
    rows, cols = w.shape
    tr = min(rows, 512)

    def body(after_ref, t_ref, w_ref, m_ref, v_ref, g_out, d_out, m_out, v_out):
        del after_ref
        g = t_ref[0].astype(F32)
        for q in range(1, N_CHIPS):
            g = g + t_ref[q].astype(F32)
        d, mn, vn = _adamw_math(w_ref[...], g, m_ref[...], v_ref[...])
        g_out[...] = g
        d_out[...] = d
        m_out[...] = mn
        v_out[...] = vn

    blk = pl.BlockSpec((tr, cols), lambda r: (r, 0))
    return pl.pallas_call(
        body, name=name, grid=(rows // tr,),
        in_specs=[_ANY, pl.BlockSpec((N_CHIPS, tr, cols), lambda r: (0, r, 0)), blk, blk, blk],
        out_specs=[blk] * 4,
        out_shape=[jax.ShapeDtypeStruct((rows, cols), F32)] * 4,
        compiler_params=pltpu.CompilerParams(dimension_semantics=("arbitrary",), vmem_limit_bytes=VMEM_LIMIT),
    )(after, chip_sums, w, m, v)


SMALL_ROWS = ("loss", "norm1_pre_g", "pool_scale", "sgu_ln_g", "sgu_ln_b", "norm1_post_g", "norm2_pre_g",
              "norm2_post_g")


def _adamw_small(u_rows, u_bin, u_ws, u_bs, params):
    n = len(params)

    def body(*refs):
        urow_ref, ubin_ref, uws_ref, ubs_ref = refs[:4]
        wmv = refs[4:4 + 3 * n]
        loss_ref = refs[4 + 3 * n]
        outs = refs[5 + 3 * n:]

        def total(ref, idx):
            g = ref[(0,) + idx]
            for q in range(1, N_CHIPS):
                g = g + ref[(q,) + idx]
            return g

        loss_ref[...] = total(urow_ref, (slice(0, 1), slice(None)))
        for p in range(n):
            if p < 7:
                g = total(urow_ref, (slice(p + 1, p + 2), slice(None)))
            else:
                g = total((ubin_ref, uws_ref, ubs_ref)[p - 7], (slice(None), slice(None)))
            d, mn, vn = _adamw_math(wmv[3 * p][...], g, wmv[3 * p + 1][...], wmv[3 * p + 2][...])
            outs[4 * p][...] = g
            outs[4 * p + 1][...] = d
            outs[4 * p + 2][...] = mn
            outs[4 * p + 3][...] = vn

    flat = [a for p in params for a in p]
    out_shape = [jax.ShapeDtypeStruct((1, D_MODEL), F32)]
    for w, _, _ in params:
        out_shape += [jax.ShapeDtypeStruct(w.shape, F32)] * 4
    return pl.pallas_call(body, name="adamw_small", out_shape=out_shape)(u_rows, u_bin, u_ws, u_bs, *flat)


def _slots_of_rows(full):
    owners = [_owner_of_slot(s) for s in range(N_DEV)]
    parts = full.reshape(N_GROUPS, N_DEV, PG_SHARD, GROUP)
    return jnp.stack([parts[:, o] for o in owners]).reshape(N_DEV, N_GROUPS * PG_SHARD, GROUP).astype(BF16)


def kernel(x, norm1_pre_g, w_in, b_in, w_pool, pool_scale, sgu_ln_g, sgu_ln_b, w_spatial, b_spatial, w_sgu_proj, w_out, norm1_post_g, norm2_pre_g, w_ff1, w_ff2, norm2_post_g, loss_target, m_norm1_pre_g, m_w_in, m_b_in, m_w_pool, m_pool_scale, m_sgu_ln_g, m_sgu_ln_b, m_w_spatial, m_b_spatial, m_w_sgu_proj, m_w_out, m_norm1_post_g, m_norm2_pre_g, m_w_ff1, m_w_ff2, m_norm2_post_g, v_norm1_pre_g, v_w_in, v_b_in, v_w_pool, v_pool_scale, v_sgu_ln_g, v_sgu_ln_b, v_w_spatial, v_b_spatial, v_w_sgu_proj, v_w_out, v_norm1_post_g, v_norm2_pre_g, v_w_ff1, v_w_ff2, v_norm2_post_g):
    t_len = x.shape[1]
    row = lambda a: a.reshape(1, -1)
    x2 = x.reshape(t_len, D_MODEL)
    tgt2 = loss_target.reshape(t_len, D_MODEL)
    pg2 = lambda a: a.reshape(N_GROUPS * PG_SHARD, GROUP)

    def prework(token):
        return _prenorm(token, x2, row(norm1_pre_g))

    (win_f, g_pool, g_proj, g_out), (xn, xn_b) = _gather_behind(
        "mix", [w_in.astype(BF16), pg2(w_pool).astype(BF16), pg2(w_sgu_proj).astype(BF16), w_out.astype(BF16)],
        None, prework, by_columns=(0,))
    regroup = lambda g: g.reshape(N_DEV, N_GROUPS, PG_SHARD, GROUP).transpose(1, 0, 2, 3).reshape(N_GROUPS, GROUP, GROUP)
    wpool_f, wproj_f = regroup(g_pool), regroup(g_proj)
    wout_f = g_out.reshape(D_MODEL, D_MODEL)
    bsp_t = b_spatial.T

    def forward(token):
        z, y, h1, *saved = _fwd_mix(token, xn, x2, win_f, row(b_in), wpool_f, row(pool_scale),
                                    row(sgu_ln_g), row(sgu_ln_b), w_spatial, bsp_t, wproj_f, wout_f,
                                    row(norm1_post_g))
        return h1, z, y, saved

    (g_ff1, g_ff2), (h1, z, y, saved) = _gather_behind("ff", [w_ff1.astype(BF16), w_ff2.astype(BF16)], win_f,
                                                       forward)
    w2_f = g_ff2.reshape(D_FF, D_MODEL)
    hn_b, f_b, df1_b, df2_b, dh1, dg2post, dg2pre, loss_p = _mlp(h1, tgt2, row(norm2_pre_g), row(norm2_post_g),
                                                               g_ff1, w2_f)
    p_ff2 = _wgrad("wgrad_ff2", f_b, df2_b, FF_SHARD, True)
    p_ff1 = _wgrad("wgrad_ff1", hn_b, df1_b, FF_SHARD, False)
    c_arr = lax.axis_index("c").astype(jnp.int32).reshape(1)
    ff_parts = [p_ff1, p_ff2]
    got_ff = _pair_exchange("pair_exchange_ff", ff_parts, [])
    chip_ff = [_pair_sum_big("pair_sum_" + nm, c_arr, b, r) for nm, b, r in zip(("ff1", "ff2"), ff_parts, got_ff)]

    def backward_mix(token):
        return _bwd_mix(token, dh1, y, z, saved, wpool_f, row(pool_scale), row(sgu_ln_g), row(sgu_ln_b), w_spatial,
                        bsp_t, wproj_f, wout_f, row(norm1_post_g))

    summed_ff, (dz_b, mg_b, dy_b, dwpool, dwproj, dws, dbsp_t, dg1post, dps, dlng, dlnb,
                dbin) = _chip_exchange_behind("ff", chip_ff, [], backward_mix)

    p_in = _wgrad_in(xn_b, dz_b)
    bigs = [p_in, _slots_of_rows(dwpool), _slots_of_rows(dwproj)]
    rows = jnp.concatenate([jnp.broadcast_to(loss_p[:, 0:1], (1, D_MODEL)), dps, dlng, dlnb, dg1post, dg2pre, dg2post],
                           axis=0)
    smalls = [rows, dbin, dws.reshape(N_HEADS * SGU_BLOCK, SGU_BLOCK), dbsp_t.T]
    got = _pair_exchange("pair_exchange_in", bigs, smalls)
    chip_bigs = [_pair_sum_big("pair_sum_" + nm, c_arr, b, r) for nm, b, r in zip(("in", "pool", "proj"), bigs, got[:3])]
    chip_smalls = _pair_sum_small("pair_sum_small", smalls, got[3:])

    def backward_rest(token):
        dx, dg1pre = _bwd_in(token, dz_b, x2, dh1, row(norm1_pre_g), win_f)
        p_out = _wgrad("wgrad_out", mg_b, dy_b, OUT_SHARD, True)
        got_out = _pair_exchange("pair_exchange_out", [p_out], [dg1pre])
        chip_out = _pair_sum_big("pair_sum_out", c_arr, p_out, got_out[0])
        upd_ff = [_adamw_big("adamw_" + nm, p_out, t, *p) for nm, t, p in
                  (("ff1", summed_ff[0], (w_ff1, m_w_ff1, v_w_ff1)), ("ff2", summed_ff[1], (w_ff2, m_w_ff2, v_w_ff2)))]
        return upd_ff[1][0], dx, chip_out, _pair_sum_small("pair_sum_g1pre", [dg1pre], got_out[1:])[0], upd_ff

    summed_in, (_, dx, chip_out, chip_g1pre, upd_ff) = _chip_exchange_behind("in", chip_bigs, chip_smalls,
                                                                             backward_rest)

    def update_in(token):
        res = _adamw_big("adamw_in", token, summed_in[0], w_in, m_w_in, v_w_in)
        return res[0], res

    summed_out, (_, upd_in) = _chip_exchange_behind("out", [chip_out], [chip_g1pre], update_in)
    big = {"in": upd_in, "ff1": upd_ff[0], "ff2": upd_ff[1]}
    for nm, t, p in (("pool", summed_in[1], (pg2(w_pool), pg2(m_w_pool), pg2(v_w_pool))),
                     ("proj", summed_in[2], (pg2(w_sgu_proj), pg2(m_w_sgu_proj), pg2(v_w_sgu_proj))),
                     ("out", summed_out[0], (w_out, m_w_out, v_w_out))):
        big[nm] = _adamw_big("adamw_" + nm, t, t, *p)
    u_rows = jnp.concatenate([summed_in[3][:, 0:1], summed_out[1], summed_in[3][:, 1:]], axis=1)
    ws2 = lambda a: a.reshape(N_HEADS * SGU_BLOCK, SGU_BLOCK)
    small_params = [(row(norm1_pre_g), row(m_norm1_pre_g), row(v_norm1_pre_g)),
                    (row(pool_scale), row(m_pool_scale), row(v_pool_scale)),
                    (row(sgu_ln_g), row(m_sgu_ln_g), row(v_sgu_ln_g)),
                    (row(sgu_ln_b), row(m_sgu_ln_b), row(v_sgu_ln_b)),
                    (row(norm1_post_g), row(m_norm1_post_g), row(v_norm1_post_g)),
                    (row(norm2_pre_g), row(m_norm2_pre_g), row(v_norm2_pre_g)),
                    (row(norm2_post_g), row(m_norm2_post_g), row(v_norm2_post_g)),
                    (row(b_in), row(m_b_in), row(v_b_in)),
                    (ws2(w_spatial), ws2(m_w_spatial), ws2(v_w_spatial)),
                    (b_spatial, m_b_spatial, v_b_spatial)]
    small_out = _adamw_small(u_rows, *summed_in[4:], small_params)
    loss = small_out[0][0, 0]
    small_names = SMALL_ROWS[1:] + ("b_in", "w_spatial", "b_spatial")
    small = {nm: small_out[1 + 4 * p:5 + 4 * p] for p, nm in enumerate(small_names)}

    shapes = {"norm1_pre_g": norm1_pre_g.shape, "w_in": w_in.shape, "b_in": b_in.shape, "w_pool": w_pool.shape,
              "pool_scale": pool_scale.shape, "sgu_ln_g": sgu_ln_g.shape, "sgu_ln_b": sgu_ln_b.shape,
              "w_spatial": w_spatial.shape, "b_spatial": b_spatial.shape, "w_sgu_proj": w_sgu_proj.shape,
              "w_out": w_out.shape, "norm1_post_g": norm1_post_g.shape, "norm2_pre_g": norm2_pre_g.shape,
              "w_ff1": w_ff1.shape, "w_ff2": w_ff2.shape, "norm2_post_g": norm2_post_g.shape}
    source = {"w_in": big["in"], "w_pool": big["pool"], "w_sgu_proj": big["proj"], "w_out": big["out"],
              "w_ff1": big["ff1"], "w_ff2": big["ff2"], **small}
    order = list(shapes)
    outs = [loss, dx.reshape(x.shape)]
    for kind in range(4):
        outs += [source[nm][kind].reshape(shapes[nm]) for nm in order]
    return tuple(outs)
```

```python
import functools
import math

import jax
import jax.numpy as jnp
from jax import lax
from jax.experimental import pallas as pl
from jax.experimental.pallas import tpu as pltpu

F32, BF16 = jnp.float32, jnp.bfloat16
MESH = pl.DeviceIdType.MESH

D_MODEL = 1024
D_IN = 5120
D_FF = 4096
N_DEV = 8
N_CHIPS = 4
WINDOWS = (2, 4, 8, 16)
N_GROUPS = 4
GROUP = 256
HALO = 16
SGU_BLOCK = 128
N_HEADS = 4
HEAD = 256
CHUNK = 64
EPS = 1e-6
IN_SHARD = D_IN // N_DEV
FF_SHARD = D_FF // N_DEV
OUT_SHARD = D_MODEL // N_DEV
PG_SHARD = GROUP // N_DEV

ADAM_LR, ADAM_B1, ADAM_B2, ADAM_EPS, ADAM_WD, ADAM_STEP = 0.001, 0.9, 0.999, 1e-08, 0.01, 10

VMEM_LIMIT = 56 * 1024 * 1024
TM = 256
TM_BWD = 256
TM_IN = 512
PROJ_COLS = 512
GELU_C0 = math.sqrt(2.0 / math.pi)
GELU_C1 = 0.044715


def _dot(a, b):
    return jnp.dot(a, b, preferred_element_type=F32)


def _dot_nt(a, b):
    return lax.dot_general(a, b, (((1,), (1,)), ((), ())), preferred_element_type=F32)


def _dot_tn(a, b):
    return lax.dot_general(a, b, (((0,), (0,)), ((), ())), preferred_element_type=F32)


def _gelu(x):
    t = jnp.tanh(GELU_C0 * (x + GELU_C1 * (x * x * x)))
    return 0.5 * x * (1.0 + t), t


def _gelu_grad(x, t):
    return 0.5 * (1.0 + t) + 0.5 * x * (1.0 - t * t) * (GELU_C0 * (1.0 + 3.0 * GELU_C1 * (x * x)))


def _sigmoid(x):
    return 1.0 / (1.0 + jnp.exp(-x))


def _mean(x):
    return jnp.mean(x, axis=-1, keepdims=True)


def _colsum(x):
    return jnp.sum(x, axis=0, keepdims=True)


def _const_spec(shape):
    nd = len(shape)
    return pl.BlockSpec(shape, lambda *_: (0,) * nd, pipeline_mode=pl.Buffered(1))


def _acc_spec(shape):
    nd = len(shape)
    return pl.BlockSpec(shape, lambda *_: (0,) * nd)


def _masked_ws(ws_ref):
    ri = lax.broadcasted_iota(jnp.int32, (SGU_BLOCK, SGU_BLOCK), 0) // CHUNK
    ci = lax.broadcasted_iota(jnp.int32, (SGU_BLOCK, SGU_BLOCK), 1) // CHUNK
    return [jnp.where(ri >= ci, ws_ref[h], 0.0).astype(BF16) for h in range(N_HEADS)]


def _pool_fwd(pbuf, tile_idx, tm):
    pos = lax.broadcasted_iota(jnp.int32, (tm, 1), 0) + tile_idx * tm + 1
    pooled = []
    for g, w in enumerate(WINDOWS):
        e = pbuf[:, g * GROUP:(g + 1) * GROUP]
        s, sh = e, 1
        while sh < w:
            s = s + pltpu.roll(s, sh, 0)
            sh *= 2
        inv = 1.0 / jnp.minimum(pos, w).astype(F32)
        pooled.append(s[HALO:] * inv - e[HALO:])
    return pooled


def _spatial_head(ws_h, vb, bsp_ref, h, nblk):
    return jnp.concatenate(
        [_dot(ws_h, vb[n * SGU_BLOCK:(n + 1) * SGU_BLOCK, h * HEAD:(h + 1) * HEAD]) + bsp_ref[:, h:h + 1]
         for n in range(nblk)], axis=0)


def _prenorm(after, x, g1pre):
    t_len = x.shape[0]
    tm = TM_IN

    def body(after_ref, x_ref, g_ref, xn_ref, xnt_ref):
        del after_ref
        xv = x_ref[...]
        xnb = (xv * lax.rsqrt(_mean(xv * xv) + EPS) * g_ref[...]).astype(BF16)
        xn_ref[...] = xnb
        xnt_ref[...] = xnb.T

    return pl.pallas_call(
        body, name="prenorm", grid=(t_len // tm,),
        in_specs=[_ANY, pl.BlockSpec((tm, D_MODEL), lambda i: (i, 0)), _const_spec((1, D_MODEL))],
        out_specs=[pl.BlockSpec((tm, D_MODEL), lambda i: (i, 0)), pl.BlockSpec((D_MODEL, tm), lambda i: (0, i))],
        out_shape=[jax.ShapeDtypeStruct((t_len, D_MODEL), BF16), jax.ShapeDtypeStruct((D_MODEL, t_len), BF16)],
        compiler_params=pltpu.CompilerParams(dimension_semantics=("arbitrary",)),
    )(after, x, g1pre)


def _fwd_mix(after, xn, x, win_g, b_in, wpool, pool_scale, lng, lnb, ws, bsp_t, wproj, wout, g1post):
    t_len = x.shape[0]
    tm = TM
    nt = t_len // tm

    def body(after_ref, xn_ref, xb_ref, win_ref, bin_ref, wpool_ref, ps_ref, lng_ref, lnb_ref, ws_ref, bsp_ref,
             wproj_ref, wout_ref, g1post_ref, z_ref, y_ref, h1_ref, u_ref, gpu_ref, xhat_ref, gpv_ref, sa_ref,
             sb_ref, zcur, znext, pbuf):
        del after_ref
        s = pl.program_id(0)

        @pl.when(s == 0)
        def _():
            znext[...] = jnp.zeros((tm, D_IN), F32)
            pbuf[...] = jnp.zeros((tm + HALO, D_MODEL), F32)

        zcur[...] = znext[...]
        xnb = xn_ref[...]

        def project(p):
            cols = slice(p * PROJ_COLS, (p + 1) * PROJ_COLS)
            zp = _dot(xnb, win_ref[:, cols]) + bin_ref[:, cols]
            if (p + 1) * PROJ_COLS <= D_MODEL:
                z_ref[:, cols] = zp
            znext[:, cols] = zp

        project(0)
        pbuf[0:HALO, :] = jnp.where(s <= 1, 0.0, pbuf[0:HALO, :])
        pbuf[HALO:, :] = zcur[:, 0:D_MODEL]
        pooled = _pool_fwd(pbuf, jnp.maximum(s - 1, 0), tm)
        pbuf[0:HALO, :] = pbuf[tm:tm + HALO, :]
        a = jnp.concatenate([_dot(pooled[g].astype(BF16), wpool_ref[g]) for g in range(N_GROUPS)], axis=1)
        a = a * ps_ref[...]
        project(1)
        zu = zcur[:, D_MODEL:2 * D_MODEL]
        u, tu = _gelu(zu)
        u_ref[...] = u.astype(BF16)
        gpu_ref[...] = _gelu_grad(zu, tu).astype(BF16)
        project(2)
        zv = zcur[:, 2 * D_MODEL:3 * D_MODEL]
        gv, tv = _gelu(zv)
        xc = gv - _mean(gv)
        rln = lax.rsqrt(_mean(xc * xc) + EPS)
        xhat = xc * rln
        xhat_ref[...] = xhat.astype(BF16)
        gpv_ref[...] = (_gelu_grad(zv, tv) * rln).astype(BF16)
        vb = (xhat * lng_ref[...] + lnb_ref[...]).astype(BF16)
        wsm = _masked_ws(ws_ref)
        bbr = []
        for h in range(N_HEADS):
            project(3 + h)
            sv = _spatial_head(wsm[h], vb, bsp_ref, h, tm // SGU_BLOCK)
            bbr.append(_dot((u[:, h * HEAD:(h + 1) * HEAD] * sv).astype(BF16), wproj_ref[h]))
        bbr = jnp.concatenate(bbr, axis=1)
        project(7)
        sa = _sigmoid(zcur[:, 3 * D_MODEL:4 * D_MODEL])
        sb = _sigmoid(zcur[:, 4 * D_MODEL:5 * D_MODEL])
        sa_ref[...] = sa.astype(BF16)
        sb_ref[...] = sb.astype(BF16)
        project(8)
        yv = _dot((sa * a + sb * bbr).astype(BF16), wout_ref[...])
        y_ref[...] = yv
        project(9)
        ry = lax.rsqrt(_mean(yv * yv) + EPS)
        h1_ref[...] = xb_ref[...] + yv * ry * g1post_ref[...]

    proj = lambda w: pl.BlockSpec((tm, w), lambda s: (jnp.minimum(s, nt - 1), 0))
    mix = lambda w: pl.BlockSpec((tm, w), lambda s: (jnp.maximum(s - 1, 0), 0))
    return pl.pallas_call(
        body, name="fwd_mix", grid=(nt + 1,),
        in_specs=[_ANY, proj(D_MODEL), mix(D_MODEL),
                  _const_spec((D_MODEL, D_IN)),
                  _const_spec((1, D_IN)), _const_spec((N_GROUPS, GROUP, GROUP)), _const_spec((1, D_MODEL)),
                  _const_spec((1, D_MODEL)), _const_spec((1, D_MODEL)),
                  _const_spec((N_HEADS, SGU_BLOCK, SGU_BLOCK)), _const_spec((SGU_BLOCK, N_HEADS)),
                  _const_spec((N_HEADS, HEAD, HEAD)), _const_spec((D_MODEL, D_MODEL)), _const_spec((1, D_MODEL))],
        out_specs=[proj(D_MODEL), mix(D_MODEL), mix(D_MODEL)] + [mix(D_MODEL)] * 6,
        out_shape=[jax.ShapeDtypeStruct((t_len, D_MODEL), F32),
                   jax.ShapeDtypeStruct((t_len, D_MODEL), F32), jax.ShapeDtypeStruct((t_len, D_MODEL), F32)]
        + [jax.ShapeDtypeStruct((t_len, D_MODEL), BF16)] * 6,
        scratch_shapes=[pltpu.VMEM((tm, D_IN), F32), pltpu.VMEM((tm, D_IN), F32),
                        pltpu.VMEM((tm + HALO, D_MODEL), F32)],
        compiler_params=pltpu.CompilerParams(dimension_semantics=("arbitrary",), vmem_limit_bytes=VMEM_LIMIT),
    )(after, xn, x, win_g, b_in, wpool, pool_scale, lng, lnb, ws, bsp_t, wproj, wout, g1post)


def _mlp(h1, target, g2pre, g2post, w1_g, w2):
    t_len = h1.shape[0]
    tm = TM
    nt = t_len // tm

    def body(h1_ref, tgt_ref, g2pre_ref, g2post_ref, w1_ref, w2_ref,
             hn_ref, f_ref, df1_ref, df2_ref, dh1_ref, dg2post_ref, dg2pre_ref, loss_ref, f1_scr):
        i = pl.program_id(0)

        @pl.when(i == 0)
        def _():
            dg2post_ref[...] = jnp.zeros_like(dg2post_ref)
            dg2pre_ref[...] = jnp.zeros_like(dg2pre_ref)
            loss_ref[...] = jnp.zeros_like(loss_ref)

        h = h1_ref[...]
        r2 = lax.rsqrt(_mean(h * h) + EPS)
        nh = h * r2
        hnb = (nh * g2pre_ref[...]).astype(BF16)
        hn_ref[...] = hnb.T
        for k in range(N_DEV):
            f1_scr[:, k * FF_SHARD:(k + 1) * FF_SHARD] = _dot(hnb, w1_ref[:, k * FF_SHARD:(k + 1) * FF_SHARD])
        r = jnp.maximum(f1_scr[...], 0.0)
        fb = (r * r).astype(BF16)
        f_ref[...] = fb.T
        f2 = _dot(fb, w2_ref[...])
        rf = lax.rsqrt(_mean(f2 * f2) + EPS)
        nf = f2 * rf
        diff = h + nf * g2post_ref[...] - tgt_ref[...]
        loss_ref[...] += (0.5 / D_MODEL) * jnp.sum(diff * diff)
        dout = diff * (1.0 / D_MODEL)
        dg2post_ref[...] += _colsum(dout * nf)
        dn = dout * g2post_ref[...]
        df2b = (rf * (dn - nf * _mean(dn * nf))).astype(BF16)
        df2_ref[...] = df2b
        df = _dot_nt(df2b, w2_ref[...])
        df1b = (df * (2.0 * jnp.maximum(f1_scr[...], 0.0))).astype(BF16)
        df1_ref[...] = df1b
        dhn = _dot_nt(df1b, w1_ref[...])
        dg2pre_ref[...] += _colsum(dhn * nh)
        dnh = dhn * g2pre_ref[...]
        dh1_ref[...] = dout + r2 * (dnh - nh * _mean(dnh * nh))

    tok = lambda w: pl.BlockSpec((tm, w), lambda i: (i, 0))
    return pl.pallas_call(
        body, name="mlp_fwd_bwd", grid=(nt,),
        in_specs=[tok(D_MODEL), tok(D_MODEL), _const_spec((1, D_MODEL)), _const_spec((1, D_MODEL)),
                  _const_spec((D_MODEL, D_FF)), _const_spec((D_FF, D_MODEL))],
        out_specs=[pl.BlockSpec((D_MODEL, tm), lambda i: (0, i)), pl.BlockSpec((D_FF, tm), lambda i: (0, i)),
                   tok(D_FF), tok(D_MODEL), tok(D_MODEL),
                   _acc_spec((1, D_MODEL)), _acc_spec((1, D_MODEL)), _acc_spec((1, 128))],
        out_shape=[jax.ShapeDtypeStruct((D_MODEL, t_len), BF16), jax.ShapeDtypeStruct((D_FF, t_len), BF16),
                   jax.ShapeDtypeStruct((t_len, D_FF), BF16), jax.ShapeDtypeStruct((t_len, D_MODEL), BF16),
                   jax.ShapeDtypeStruct((t_len, D_MODEL), F32), jax.ShapeDtypeStruct((1, D_MODEL), F32),
                   jax.ShapeDtypeStruct((1, D_MODEL), F32), jax.ShapeDtypeStruct((1, 128), F32)],
        scratch_shapes=[pltpu.VMEM((tm, D_FF), F32)],
        compiler_params=pltpu.CompilerParams(dimension_semantics=("arbitrary",), vmem_limit_bytes=VMEM_LIMIT),
    )(h1, target, g2pre, g2post, w1_g, w2)


def _bwd_mix(after, dh1, y, z, saved, wpool, pool_scale, lng, lnb, ws, bsp_t, wproj, wout, g1post):
    t_len = y.shape[0]
    tm = TM_BWD
    nt = t_len // tm
    nblk = tm // SGU_BLOCK

    def body(after_ref, dh1_ref, y_ref, z_ref, zh_ref, u_ref, gpu_ref, xhat_ref, gpv_ref, sa_ref, sb_ref,
             wpool_ref, ps_ref, lng_ref, lnb_ref, ws_ref,
             bsp_ref, wproj_ref, wout_ref, g1post_ref,
             dz_ref, mg_ref, dy_ref, dwpool_ref, dwproj_ref, dws_ref, dbsp_ref, dg1post_ref, dps_ref,
             dlng_ref, dlnb_ref, dbin_ref, pbuf, qbuf):
        del after_ref
        i = pl.program_id(0)
        ti = nt - 1 - i

        @pl.when(i == 0)
        def _():
            for ref in (dwpool_ref, dwproj_ref, dws_ref, dbsp_ref, dg1post_ref, dps_ref, dlng_ref, dlnb_ref,
                        dbin_ref):
                ref[...] = jnp.zeros_like(ref)
            qbuf[tm:tm + HALO, :] = jnp.zeros((HALO, D_MODEL), F32)

        pbuf[0:HALO, :] = jnp.where(ti > 0, zh_ref[...], 0.0)
        pbuf[HALO:, :] = z_ref[...]
        pooled = _pool_fwd(pbuf, ti, tm)
        pooled_b = [p.astype(BF16) for p in pooled]
        a_raw = jnp.concatenate([_dot(pooled_b[g], wpool_ref[g]) for g in range(N_GROUPS)], axis=1)
        wsm = _masked_ws(ws_ref)
        u = u_ref[...].astype(F32)
        xhat = xhat_ref[...].astype(F32)
        vb = (xhat * lng_ref[...] + lnb_ref[...]).astype(BF16)
        sv_heads = [_spatial_head(wsm[h], vb, bsp_ref, h, nblk) for h in range(N_HEADS)]
        gated_b = [(u[:, h * HEAD:(h + 1) * HEAD] * sv_heads[h]).astype(BF16) for h in range(N_HEADS)]
        bbr = jnp.concatenate([_dot(gated_b[h], wproj_ref[h]) for h in range(N_HEADS)], axis=1)
        sa = sa_ref[...].astype(F32)
        sb = sb_ref[...].astype(F32)
        a = a_raw * ps_ref[...]
        mg_ref[...] = (sa * a + sb * bbr).astype(BF16).T

        dh = dh1_ref[...]
        yv = y_ref[...]
        ry = lax.rsqrt(_mean(yv * yv) + EPS)
        ny = yv * ry
        dg1post_ref[...] += _colsum(dh * ny)
        dn = dh * g1post_ref[...]
        dyb = (ry * (dn - ny * _mean(dn * ny))).astype(BF16)
        dy_ref[...] = dyb
        dmg = _dot_nt(dyb, wout_ref[...])

        da = dmg * sa
        dbbr = dmg * sb
        dzga = dmg * a * sa * (1.0 - sa)
        dzgb = dmg * bbr * sb * (1.0 - sb)
        dz_ref[:, 3 * D_MODEL:4 * D_MODEL] = dzga.astype(BF16)
        dz_ref[:, 4 * D_MODEL:5 * D_MODEL] = dzgb.astype(BF16)
        dbin_ref[:, 3 * D_MODEL:4 * D_MODEL] += _colsum(dzga)
        dbin_ref[:, 4 * D_MODEL:5 * D_MODEL] += _colsum(dzgb)

        dps_ref[...] += _colsum(da * a_raw)
        da_raw_b = (da * ps_ref[...]).astype(BF16)
        pos = lax.broadcasted_iota(jnp.int32, (tm, 1), 0) + ti * tm + 1
        dpooled = []
        for g, w in enumerate(WINDOWS):
            cols = slice(g * GROUP, (g + 1) * GROUP)
            dwpool_ref[g] += _dot_tn(pooled_b[g], da_raw_b[:, cols])
            dp = _dot_nt(da_raw_b[:, cols], wpool_ref[g])
            dpooled.append(dp)
            qbuf[0:tm, cols] = dp * (1.0 / jnp.minimum(pos, w).astype(F32))
        n_ext = tm + HALO
        dzp = []
        for g, w in enumerate(WINDOWS):
            e = qbuf[:, g * GROUP:(g + 1) * GROUP]
            s, sh = e, 1
            while sh < w:
                s = s + pltpu.roll(s, n_ext - sh, 0)
                sh *= 2
            dzp.append(s[0:tm] - dpooled[g])
        qbuf[tm:tm + HALO, :] = qbuf[0:HALO, :]
        dzp = jnp.concatenate(dzp, axis=1)
        dz_ref[:, 0:D_MODEL] = dzp.astype(BF16)
        dbin_ref[:, 0:D_MODEL] += _colsum(dzp)

        dv_heads = []
        du_heads = []
        for h in range(N_HEADS):
            cols = slice(h * HEAD, (h + 1) * HEAD)
            dbbr_b = dbbr[:, cols].astype(BF16)
            dwproj_ref[h] += _dot_tn(gated_b[h], dbbr_b)
            dgated = _dot_nt(dbbr_b, wproj_ref[h])
            du_heads.append(dgated * sv_heads[h])
            dsv = dgated * u[:, cols]
            dsv_b = dsv.astype(BF16)
            rows = []
            for n in range(nblk):
                blk = slice(n * SGU_BLOCK, (n + 1) * SGU_BLOCK)
                rows.append(_dot_tn(wsm[h], dsv_b[blk]))
                dws_ref[h] += _dot_nt(dsv_b[blk], vb[blk, cols])
                dbsp_ref[:, h:h + 1] += jnp.sum(dsv[blk], axis=1, keepdims=True)
            dv_heads.append(jnp.concatenate(rows, axis=0))
        dzu = jnp.concatenate(du_heads, axis=1) * gpu_ref[...].astype(F32)
        dz_ref[:, D_MODEL:2 * D_MODEL] = dzu.astype(BF16)
        dbin_ref[:, D_MODEL:2 * D_MODEL] += _colsum(dzu)
        dv = jnp.concatenate(dv_heads, axis=1)
        dlng_ref[...] += _colsum(dv * xhat)
        dlnb_ref[...] += _colsum(dv)
        dxh = dv * lng_ref[...]
        dzv = (dxh - _mean(dxh) - xhat * _mean(dxh * xhat)) * gpv_ref[...].astype(F32)
        dz_ref[:, 2 * D_MODEL:3 * D_MODEL] = dzv.astype(BF16)
        dbin_ref[:, 2 * D_MODEL:3 * D_MODEL] += _colsum(dzv)

        @pl.when(i == nt - 1)
        def _():
            ri = lax.broadcasted_iota(jnp.int32, (SGU_BLOCK, SGU_BLOCK), 0) // CHUNK
            ci = lax.broadcasted_iota(jnp.int32, (SGU_BLOCK, SGU_BLOCK), 1) // CHUNK
            for h in range(N_HEADS):
                dws_ref[h] = jnp.where(ri >= ci, dws_ref[h], 0.0)

    tok = lambda w: pl.BlockSpec((tm, w), lambda i: (nt - 1 - i, 0))
    halo = pl.BlockSpec((HALO, D_MODEL), lambda i: (jnp.maximum((nt - 1 - i) * (tm // HALO) - 1, 0), 0))
    return pl.pallas_call(
        body, name="bwd_mix", grid=(nt,),
        in_specs=[_ANY, tok(D_MODEL), tok(D_MODEL), tok(D_MODEL), halo] + [tok(D_MODEL)] * 6
        + [_const_spec((N_GROUPS, GROUP, GROUP)),
                  _const_spec((1, D_MODEL)), _const_spec((1, D_MODEL)), _const_spec((1, D_MODEL)),
                  _const_spec((N_HEADS, SGU_BLOCK, SGU_BLOCK)), _const_spec((SGU_BLOCK, N_HEADS)),
                  _const_spec((N_HEADS, HEAD, HEAD)), _const_spec((D_MODEL, D_MODEL)), _const_spec((1, D_MODEL))],
        out_specs=[tok(D_IN), pl.BlockSpec((D_MODEL, tm), lambda i: (0, nt - 1 - i)), tok(D_MODEL),
                   _acc_spec((N_GROUPS, GROUP, GROUP)), _acc_spec((N_HEADS, HEAD, HEAD)),
                   _acc_spec((N_HEADS, SGU_BLOCK, SGU_BLOCK)), _acc_spec((SGU_BLOCK, N_HEADS)),
                   _acc_spec((1, D_MODEL)), _acc_spec((1, D_MODEL)), _acc_spec((1, D_MODEL)), _acc_spec((1, D_MODEL)),
                   _acc_spec((1, D_IN))],
        out_shape=[jax.ShapeDtypeStruct((t_len, D_IN), BF16),
                   jax.ShapeDtypeStruct((D_MODEL, t_len), BF16), jax.ShapeDtypeStruct((t_len, D_MODEL), BF16),
                   jax.ShapeDtypeStruct((N_GROUPS, GROUP, GROUP), F32), jax.ShapeDtypeStruct((N_HEADS, HEAD, HEAD), F32),
                   jax.ShapeDtypeStruct((N_HEADS, SGU_BLOCK, SGU_BLOCK), F32),
                   jax.ShapeDtypeStruct((SGU_BLOCK, N_HEADS), F32),
                   jax.ShapeDtypeStruct((1, D_MODEL), F32), jax.ShapeDtypeStruct((1, D_MODEL), F32),
                   jax.ShapeDtypeStruct((1, D_MODEL), F32), jax.ShapeDtypeStruct((1, D_MODEL), F32),
                   jax.ShapeDtypeStruct((1, D_IN), F32)],
        scratch_shapes=[pltpu.VMEM((tm + HALO, D_MODEL), F32), pltpu.VMEM((tm + HALO, D_MODEL), F32)],
        compiler_params=pltpu.CompilerParams(dimension_semantics=("arbitrary",), vmem_limit_bytes=VMEM_LIMIT),
    )(after, dh1, y, z, z, *saved, wpool, pool_scale, lng, lnb, ws, bsp_t, wproj, wout, g1post)


def _bwd_in(after, dz_b, x, dh1, g1pre, win_g):
    t_len = x.shape[0]
    tm = TM_IN
    nt = t_len // tm

    def body(after_ref, dz_ref, x_ref, dh1_ref, g1_ref, win_ref, dx_ref, dg1pre_ref):
        del after_ref

        @pl.when(pl.program_id(0) == 0)
        def _():
            dg1pre_ref[...] = jnp.zeros_like(dg1pre_ref)

        dxn = _dot_nt(dz_ref[...], win_ref[...])
        xv = x_ref[...]
        r1 = lax.rsqrt(_mean(xv * xv) + EPS)
        nx = xv * r1
        dg1pre_ref[...] += _colsum(dxn * nx)
        dnx = dxn * g1_ref[...]
        dx_ref[...] = r1 * (dnx - nx * _mean(dnx * nx)) + dh1_ref[...]

    tok = lambda w: pl.BlockSpec((tm, w), lambda i: (i, 0))
    return pl.pallas_call(
        body, name="bwd_in", grid=(nt,),
        in_specs=[_ANY, tok(D_IN), tok(D_MODEL), tok(D_MODEL), _const_spec((1, D_MODEL)),
                  _const_spec((D_MODEL, D_IN))],
        out_specs=[tok(D_MODEL), _acc_spec((1, D_MODEL))],
        out_shape=[jax.ShapeDtypeStruct((t_len, D_MODEL), F32), jax.ShapeDtypeStruct((1, D_MODEL), F32)],
        compiler_params=pltpu.CompilerParams(dimension_semantics=("arbitrary",), vmem_limit_bytes=VMEM_LIMIT),
    )(after, dz_b, x, dh1, g1pre, win_g)


def _owner_of_slot(s):
    return 4 * ((s // 2) % 2) + 2 * (s % 2) + s // 4


def _wgrad(name, a_t, b, shard, a_sliced):
    t_len = b.shape[0]
    am = shard if a_sliced else a_t.shape[0]
    bn = b.shape[1] if a_sliced else shard

    def body(a_ref, b_ref, o_ref):
        o_ref[...] = _dot(a_ref[...], b_ref[...]).astype(BF16)

    if a_sliced:
        in_specs = [pl.BlockSpec((am, t_len), lambda s: (_owner_of_slot(s), 0)),
                    pl.BlockSpec((t_len, bn), lambda s: (0, 0), pipeline_mode=pl.Buffered(1))]
    else:
        in_specs = [pl.BlockSpec((am, t_len), lambda s: (0, 0), pipeline_mode=pl.Buffered(1)),
                    pl.BlockSpec((t_len, bn), lambda s: (0, _owner_of_slot(s)))]
    return pl.pallas_call(
        body, name=name, grid=(N_DEV,), in_specs=in_specs,
        out_specs=pl.BlockSpec((None, am, bn), lambda s: (s, 0, 0)),
        out_shape=jax.ShapeDtypeStruct((N_DEV, am, bn), BF16),
        compiler_params=pltpu.CompilerParams(dimension_semantics=("arbitrary",), vmem_limit_bytes=VMEM_LIMIT),
    )(a_t, b)


def _wgrad_in(xn_t, dz_b):
    t_len = dz_b.shape[0]

    def body(a_ref, b_ref, o_ref):
        res = _dot(a_ref[...], b_ref[...])
        o_ref[0] = res[:, 0:IN_SHARD].astype(BF16)
        o_ref[1] = res[:, IN_SHARD:2 * IN_SHARD].astype(BF16)

    out = pl.pallas_call(
        body, name="wgrad_in", grid=(N_CHIPS,),
        in_specs=[pl.BlockSpec((D_MODEL, t_len), lambda q: (0, 0), pipeline_mode=pl.Buffered(1)),
                  pl.BlockSpec((t_len, 2 * IN_SHARD), lambda q: (0, q))],
        out_specs=pl.BlockSpec((2, None, D_MODEL, IN_SHARD), lambda q: (0, q, 0, 0)),
        out_shape=jax.ShapeDtypeStruct((2, N_CHIPS, D_MODEL, IN_SHARD), BF16),
        compiler_params=pltpu.CompilerParams(dimension_semantics=("arbitrary",), vmem_limit_bytes=VMEM_LIMIT),
    )(xn_t, dz_b)
    return out.reshape(N_DEV, D_MODEL, IN_SHARD)


def _coords():
    return lax.axis_index("x"), lax.axis_index("y"), lax.axis_index("c")


_ANY = pl.BlockSpec(memory_space=pl.ANY)


def _pair_exchange(name, bigs, smalls):
    nb, n = len(bigs), len(bigs) + len(smalls)

    def body(*refs):
        src, dst = refs[:n], refs[n:2 * n]
        send_sems, recv_sems = refs[2 * n:]
        x, y, c = _coords()
        copies = []
        for j in range(n):
            s = src[j].at[pl.ds(4 * (1 - c), 4)] if j < nb else src[j]
            cp = pltpu.make_async_remote_copy(src_ref=s, dst_ref=dst[j], send_sem=send_sems.at[j],
                                              recv_sem=recv_sems.at[j], device_id=(x, y, 1 - c), device_id_type=MESH)
            cp.start()
            copies.append(cp)
        for cp in copies:
            cp.wait()

    return pl.pallas_call(
        body, name=name,
        in_specs=[_ANY] * n, out_specs=[_ANY] * n,
        out_shape=[jax.ShapeDtypeStruct((4,) + b.shape[1:], b.dtype) for b in bigs]
        + [jax.ShapeDtypeStruct(s.shape, s.dtype) for s in smalls],
        scratch_shapes=[pltpu.SemaphoreType.DMA((n,)), pltpu.SemaphoreType.DMA((n,))],
    )(*bigs, *smalls)


_HBM = pl.BlockSpec(memory_space=pltpu.HBM)
_SEM = pl.BlockSpec(memory_space=pltpu.SEMAPHORE)
_VMEM = pl.BlockSpec(memory_space=pltpu.VMEM)
_EFFECT = pltpu.SideEffectType.DATAFLOW_SIDE_EFFECTING
_TOKEN = jax.ShapeDtypeStruct((8, 128), F32)


def _in_hbm(a):
    return pltpu.with_memory_space_constraint(a, pltpu.HBM)


def _split_call(name, body, n_sems_out, arrays, sems_in=(), after=None):
    na, ns = len(arrays), len(sems_in)
    has_after = after is not None

    def kernel_body(*refs):
        arr = refs[:na]
        s_in = refs[na:na + ns]
        outs = refs[na + ns + has_after:]
        body(arr, s_in, outs[:n_sems_out])
        outs[-1][...] = jnp.zeros((8, 128), F32)

    out_shape = ([pltpu.SemaphoreType.DMA(())] * n_sems_out + [pltpu.HBM(a.shape, a.dtype) for a in arrays] + [_TOKEN])
    res = pl.pallas_call(
        kernel_body, name=name, out_shape=out_shape,
        in_specs=[_HBM] * na + [_SEM] * ns + [_ANY] * has_after,
        out_specs=[_SEM] * n_sems_out + [_HBM] * na + [_VMEM],
        input_output_aliases={i: n_sems_out + i for i in range(na)},
        compiler_params=pltpu.CompilerParams(has_side_effects=_EFFECT),
    )(*[_in_hbm(a) for a in arrays], *sems_in, *([after] if has_after else []))
    return list(res[:n_sems_out]), list(res[n_sems_out:n_sems_out + na]), res[-1]


def _wait_bytes_of(ref, send_sem, recv_sem, peer, send=True, recv=True):
    cp = pltpu.make_async_remote_copy(src_ref=ref, dst_ref=ref, send_sem=send_sem, recv_sem=recv_sem,
                                      device_id=peer, device_id_type=MESH)
    if send:
        cp.wait_send()
    if recv:
        cp.wait_recv()


def _gather_behind(tag, shards, after, work, by_columns=()):
    n = len(shards)
    lands = [lax.empty((s.shape[0], N_DEV * s.shape[1]) if j in by_columns else (N_DEV,) + s.shape, s.dtype)
             for j, s in enumerate(shards)]

    def slots(arr, j, first, count=1):
        if j in by_columns:
            cols = shards[j].shape[1]
            return arr[n + j].at[:, pl.ds(pl.multiple_of(first * cols, 128), count * cols)]
        return arr[n + j].at[first] if count == 1 else arr[n + j].at[pl.ds(first, count)]

    def own_slot(arr, j, sem):
        x, y, c = _coords()
        return pltpu.make_async_copy(arr[j], slots(arr, j, 4 * x + 2 * y + c), sem)

    def start(arr, _, sems):
        x, y, c = _coords()
        me = 4 * x + 2 * y + c
        for j in range(n):
            for chip in [(1 - x, y), (x, 1 - y), (1 - x, 1 - y)]:
                pltpu.make_async_remote_copy(src_ref=arr[j], dst_ref=slots(arr, j, me), send_sem=sems[j],
                                             recv_sem=sems[n + j], device_id=(*chip, c), device_id_type=MESH).start()
        for j in range(n):
            pltpu.make_async_remote_copy(src_ref=arr[j], dst_ref=slots(arr, j, me), send_sem=sems[2 * n + j],
                                         recv_sem=sems[3 * n + j], device_id=(x, y, 1 - c),
                                         device_id_type=MESH).start()
            own_slot(arr, j, sems[4 * n + j]).start()

    def middle(arr, s_in, sems):
        x, y, c = _coords()
        sibling = (x, y, 1 - c)
        for j in range(n):
            _wait_bytes_of(slots(arr, j, 0, 3), s_in[j], s_in[n + j], sibling)
            for chip in [(1 - x, y), (x, 1 - y), (1 - x, 1 - y)]:
                slot = slots(arr, j, 4 * chip[0] + 2 * chip[1] + c)
                pltpu.make_async_remote_copy(src_ref=slot, dst_ref=slot, send_sem=sems[j], recv_sem=sems[n + j],
                                             device_id=sibling, device_id_type=MESH).start()

    def finish(arr, s_in, _):
        x, y, c = _coords()
        sibling = (x, y, 1 - c)
        for j in range(n):
            _wait_bytes_of(slots(arr, j, 0, 1), s_in[j], s_in[n + j], sibling)
            own_slot(arr, j, s_in[2 * n + j]).wait()
            _wait_bytes_of(slots(arr, j, 0, 3), s_in[3 * n + j], s_in[4 * n + j], sibling)

    sems, arrays, token = _split_call("gather_%s_start" % tag, start, 5 * n, list(shards) + lands, after=after)
    result = work(token)
    fwd_sems, arrays, token = _split_call("gather_%s_middle" % tag, middle, 2 * n, arrays, sems_in=sems[:2 * n],
                                          after=result[0])
    _, arrays, _ = _split_call("gather_%s_finish" % tag, finish, 0, arrays, sems_in=sems[2 * n:] + fwd_sems,
                               after=token)
    return arrays[n:], result


def _chip_exchange_behind(tag, bigs, smalls, work):
    nb, n = len(bigs), len(bigs) + len(smalls)
    lands = [lax.empty(s.shape, s.dtype) for s in bigs] + [lax.empty((N_CHIPS,) + s.shape, s.dtype) for s in smalls]

    def own_slot(arr, j, sem):
        x, y, _ = _coords()
        q_me = 2 * x + y
        return pltpu.make_async_copy(arr[j].at[q_me] if j < nb else arr[j], arr[n + j].at[q_me], sem)

    def start(arr, _, sems):
        x, y, c = _coords()
        q_me = 2 * x + y
        for j in range(n):
            for peer in [(1 - x, y, c), (x, 1 - y, c), (1 - x, 1 - y, c)]:
                piece = arr[j].at[2 * peer[0] + peer[1]] if j < nb else arr[j]
                pltpu.make_async_remote_copy(src_ref=piece, dst_ref=arr[n + j].at[q_me],
                                             send_sem=sems[j], recv_sem=sems[n + j], device_id=peer,
                                             device_id_type=MESH).start()
            own_slot(arr, j, sems[2 * n + j]).start()

    def finish(arr, s_in, _):
        x, y, c = _coords()
        for j in range(n):
            _wait_bytes_of(arr[n + j].at[pl.ds(0, 3)], s_in[j], s_in[n + j], (x, y, 1 - c))
            own_slot(arr, j, s_in[2 * n + j]).wait()

    sems, arrays, token = _split_call("chip_exchange_%s_start" % tag, start, 3 * n, list(bigs) + list(smalls) + lands)
    result = work(token)
    _, arrays, _ = _split_call("chip_exchange_%s_finish" % tag, finish, 0, arrays, sems_in=sems, after=result[0])
    return arrays[n:], result


def _pair_sum_big(name, c_arr, mine, theirs):
    _, rows, cols = theirs.shape
    tr = rows

    def body(c_ref, a_ref, b_ref, o_ref):
        del c_ref
        o_ref[...] = (a_ref[...].astype(F32) + b_ref[...].astype(F32)).astype(BF16)

    return pl.pallas_call(
        body, name=name,
        grid_spec=pltpu.PrefetchScalarGridSpec(
            num_scalar_prefetch=1, grid=(N_CHIPS, rows // tr),
            in_specs=[pl.BlockSpec((None, tr, cols), lambda q, r, c_ref: (4 * c_ref[0] + q, r, 0)),
                      pl.BlockSpec((None, tr, cols), lambda q, r, c_ref: (q, r, 0))],
            out_specs=pl.BlockSpec((None, tr, cols), lambda q, r, c_ref: (q, r, 0))),
        out_shape=jax.ShapeDtypeStruct(theirs.shape, BF16),
        compiler_params=pltpu.CompilerParams(dimension_semantics=("arbitrary", "arbitrary")),
    )(c_arr, mine, theirs)


def _pair_sum_small(name, mine, theirs):
    n = len(mine)

    def body(*refs):
        for j in range(n):
            refs[2 * n + j][...] = refs[j][...] + refs[n + j][...]

    return pl.pallas_call(
        body, name=name,
        out_shape=[jax.ShapeDtypeStruct(m.shape, m.dtype) for m in mine],
    )(*mine, *theirs)


def _adamw_math(w, g, m, v):
    m = ADAM_B1 * m + (1.0 - ADAM_B1) * g
    v = ADAM_B2 * v + (1.0 - ADAM_B2) * (g * g)
    m_hat = m / (1.0 - ADAM_B1 ** ADAM_STEP)
    v_hat = v / (1.0 - ADAM_B2 ** ADAM_STEP)
    delta = -ADAM_LR * (m_hat / (jnp.sqrt(v_hat) + ADAM_EPS) + ADAM_WD * w)
    return delta, m, v


def _adamw_big(name, after, chip_sums, w, m, v):
    rows, cols = w.shape
    tr = min(rows, 512)

    def body(after_ref, t_ref, w_ref, m_ref, v_ref, g_out, d_out, m_out, v_out):
        del after_ref
        g = t_ref[0].astype(F32)
        for q in range(1, N_CHIPS):
            g = g + t_ref[q].astype(F32)
        d, mn, vn = _adamw_math(w_ref[...], g, m_ref[...], v_ref[...])
        g_out[...] = g
        d_out[...] = d
        m_out[...] = mn
        v_out[...] = vn

    blk = pl.BlockSpec((tr, cols), lambda r: (r, 0))
    return pl.pallas_call(
        body, name=name, grid=(rows // tr,),
        in_specs=[_ANY, pl.BlockSpec((N_CHIPS, tr, cols), lambda r: (0, r, 0)), blk, blk, blk],
        out_specs=[blk] * 4,
        out_shape=[jax.ShapeDtypeStruct((rows, cols), F32)] * 4,
        compiler_params=pltpu.CompilerParams(dimension_semantics=("arbitrary",), vmem_limit_bytes=VMEM_LIMIT),
    )(after, chip_sums, w, m, v)


SMALL_ROWS = ("loss", "norm1_pre_g", "pool_scale", "sgu_ln_g", "sgu_ln_b", "norm1_post_g", "norm2_pre_g",
              "norm2_post_g")


N_ROW_PARAMS = len(SMALL_ROWS) - 1


def _adamw_small(u_rows, u_others, params):
    n, n_oth = len(params), len(u_others)

    def body(*refs):
        urow_ref, others = refs[0], refs[1:1 + n_oth]
        wmv = refs[1 + n_oth:1 + n_oth + 3 * n]
        loss_ref = refs[1 + n_oth + 3 * n]
        outs = refs[2 + n_oth + 3 * n:]

        def total(ref, idx):
            g = ref[(0,) + idx].astype(F32)
            for q in range(1, N_CHIPS):
                g = g + ref[(q,) + idx].astype(F32)
            return g

        loss_ref[...] = total(urow_ref, (slice(0, 1), slice(None)))
        for p in range(n):
            if p < N_ROW_PARAMS:
                g = total(urow_ref, (slice(p + 1, p + 2), slice(None)))
            else:
                g = total(others[p - N_ROW_PARAMS], (slice(None), slice(None)))
            d, mn, vn = _adamw_math(wmv[3 * p][...], g, wmv[3 * p + 1][...], wmv[3 * p + 2][...])
            outs[4 * p][...] = g
            outs[4 * p + 1][...] = d
            outs[4 * p + 2][...] = mn
            outs[4 * p + 3][...] = vn

    flat = [a for p in params for a in p]
    out_shape = [jax.ShapeDtypeStruct((1, D_MODEL), F32)]
    for w, _, _ in params:
        out_shape += [jax.ShapeDtypeStruct(w.shape, F32)] * 4
    return pl.pallas_call(body, name="adamw_small", out_shape=out_shape)(u_rows, *u_others, *flat)


def _slots_of_rows(full):
    owners = [_owner_of_slot(s) for s in range(N_DEV)]
    parts = full.reshape(N_GROUPS, N_DEV, PG_SHARD, GROUP)
    return jnp.stack([parts[:, o] for o in owners]).reshape(N_DEV, N_GROUPS * PG_SHARD, GROUP).astype(BF16)


def kernel(x, norm1_pre_g, w_in, b_in, w_pool, pool_scale, sgu_ln_g, sgu_ln_b, w_spatial, b_spatial, w_sgu_proj, w_out, norm1_post_g, norm2_pre_g, w_ff1, w_ff2, norm2_post_g, loss_target, m_norm1_pre_g, m_w_in, m_b_in, m_w_pool, m_pool_scale, m_sgu_ln_g, m_sgu_ln_b, m_w_spatial, m_b_spatial, m_w_sgu_proj, m_w_out, m_norm1_post_g, m_norm2_pre_g, m_w_ff1, m_w_ff2, m_norm2_post_g, v_norm1_pre_g, v_w_in, v_b_in, v_w_pool, v_pool_scale, v_sgu_ln_g, v_sgu_ln_b, v_w_spatial, v_b_spatial, v_w_sgu_proj, v_w_out, v_norm1_post_g, v_norm2_pre_g, v_w_ff1, v_w_ff2, v_norm2_post_g):
    t_len = x.shape[1]
    row = lambda a: a.reshape(1, -1)
    x2 = x.reshape(t_len, D_MODEL)
    tgt2 = loss_target.reshape(t_len, D_MODEL)
    pg2 = lambda a: a.reshape(N_GROUPS * PG_SHARD, GROUP)

    def prework(token):
        return _prenorm(token, x2, row(norm1_pre_g))

    (win_f, g_pool, g_proj, g_out), (xn, xn_b) = _gather_behind(
        "mix", [w_in.astype(BF16), pg2(w_pool).astype(BF16), pg2(w_sgu_proj).astype(BF16), w_out.astype(BF16)],
        None, prework, by_columns=(0,))
    regroup = lambda g: g.reshape(N_DEV, N_GROUPS, PG_SHARD, GROUP).transpose(1, 0, 2, 3).reshape(N_GROUPS, GROUP, GROUP)
    wpool_f, wproj_f = regroup(g_pool), regroup(g_proj)
    wout_f = g_out.reshape(D_MODEL, D_MODEL)
    bsp_t = b_spatial.T

    def forward(token):
        z, y, h1, *saved = _fwd_mix(token, xn, x2, win_f, row(b_in), wpool_f, row(pool_scale),
                                    row(sgu_ln_g), row(sgu_ln_b), w_spatial, bsp_t, wproj_f, wout_f,
                                    row(norm1_post_g))
        return h1, z, y, saved

    (g_ff1, g_ff2), (h1, z, y, saved) = _gather_behind("ff", [w_ff1.astype(BF16), w_ff2.astype(BF16)], win_f,
                                                       forward, by_columns=(0,))
    w2_f = g_ff2.reshape(D_FF, D_MODEL)
    hn_b, f_b, df1_b, df2_b, dh1, dg2post, dg2pre, loss_p = _mlp(h1, tgt2, row(norm2_pre_g), row(norm2_post_g),
                                                               g_ff1, w2_f)
    p_ff2 = _wgrad("wgrad_ff2", f_b, df2_b, FF_SHARD, True)
    p_ff1 = _wgrad("wgrad_ff1", hn_b, df1_b, FF_SHARD, False)
    c_arr = lax.axis_index("c").astype(jnp.int32).reshape(1)
    ff_parts = [p_ff1, p_ff2]
    got_ff = _pair_exchange("pair_exchange_ff", ff_parts, [])
    chip_ff = [_pair_sum_big("pair_sum_" + nm, c_arr, b, r) for nm, b, r in zip(("ff1", "ff2"), ff_parts, got_ff)]

    def backward_mix(token):
        return _bwd_mix(token, dh1, y, z, saved, wpool_f, row(pool_scale), row(sgu_ln_g), row(sgu_ln_b), w_spatial,
                        bsp_t, wproj_f, wout_f, row(norm1_post_g))

    summed_ff, (dz_b, mg_b, dy_b, dwpool, dwproj, dws, dbsp_t, dg1post, dps, dlng, dlnb,
                dbin) = _chip_exchange_behind("ff", chip_ff, [], backward_mix)

    p_in = _wgrad_in(xn_b, dz_b)
    bigs = [p_in, _slots_of_rows(dwpool), _slots_of_rows(dwproj)]
    rows = jnp.concatenate([jnp.broadcast_to(loss_p[:, 0:1], (1, D_MODEL)), dps, dlng, dlnb, dg1post, dg2pre, dg2post],
                           axis=0)
    smalls = [rows, dbin, dws.reshape(N_HEADS * SGU_BLOCK, SGU_BLOCK), dbsp_t.T]
    got = _pair_exchange("pair_exchange_in", bigs, smalls)
    chip_bigs = [_pair_sum_big("pair_sum_" + nm, c_arr, b, r) for nm, b, r in zip(("in", "pool", "proj"), bigs, got[:3])]
    chip_smalls = _pair_sum_small("pair_sum_small", smalls, got[3:])

    def backward_rest(token):
        dx, dg1pre = _bwd_in(token, dz_b, x2, dh1, row(norm1_pre_g), win_f)
        p_out = _wgrad("wgrad_out", mg_b, dy_b, OUT_SHARD, True)
        got_out = _pair_exchange("pair_exchange_out", [p_out], [dg1pre])
        chip_out = _pair_sum_big("pair_sum_out", c_arr, p_out, got_out[0])
        upd_ff = [_adamw_big("adamw_" + nm, p_out, t, *p) for nm, t, p in
                  (("ff1", summed_ff[0], (w_ff1, m_w_ff1, v_w_ff1)), ("ff2", summed_ff[1], (w_ff2, m_w_ff2, v_w_ff2)))]
        return upd_ff[1][0], dx, chip_out, _pair_sum_small("pair_sum_g1pre", [dg1pre], got_out[1:])[0], upd_ff

    summed_in, (_, dx, chip_out, chip_g1pre, upd_ff) = _chip_exchange_behind("in", chip_bigs, chip_smalls,
                                                                             backward_rest)

    def update_in(token):
        res = _adamw_big("adamw_in", token, summed_in[0], w_in, m_w_in, v_w_in)
        return res[0], res

    summed_out, (_, upd_in) = _chip_exchange_behind("out", [chip_out], [chip_g1pre], update_in)
    big = {"in": upd_in, "ff1": upd_ff[0], "ff2": upd_ff[1]}
    u_rows = jnp.concatenate([summed_in[3][:, 0:1], summed_out[1], summed_in[3][:, 1:]], axis=1)
    ws2 = lambda a: a.reshape(N_HEADS * SGU_BLOCK, SGU_BLOCK)
    small_params = [(row(norm1_pre_g), row(m_norm1_pre_g), row(v_norm1_pre_g)),
                    (row(pool_scale), row(m_pool_scale), row(v_pool_scale)),
                    (row(sgu_ln_g), row(m_sgu_ln_g), row(v_sgu_ln_g)),
                    (row(sgu_ln_b), row(m_sgu_ln_b), row(v_sgu_ln_b)),
                    (row(norm1_post_g), row(m_norm1_post_g), row(v_norm1_post_g)),
                    (row(norm2_pre_g), row(m_norm2_pre_g), row(v_norm2_pre_g)),
                    (row(norm2_post_g), row(m_norm2_post_g), row(v_norm2_post_g)),
                    (row(b_in), row(m_b_in), row(v_b_in)),
                    (ws2(w_spatial), ws2(m_w_spatial), ws2(v_w_spatial)),
                    (b_spatial, m_b_spatial, v_b_spatial),
                    (pg2(w_pool), pg2(m_w_pool), pg2(v_w_pool)),
                    (pg2(w_sgu_proj), pg2(m_w_sgu_proj), pg2(v_w_sgu_proj)),
                    (w_out, m_w_out, v_w_out)]
    small_out = _adamw_small(u_rows, list(summed_in[4:]) + [summed_in[1], summed_in[2], summed_out[0]], small_params)
    loss = small_out[0][0, 0]
    small_names = SMALL_ROWS[1:] + ("b_in", "w_spatial", "b_spatial", "w_pool", "w_sgu_proj", "w_out")
    small = {nm: small_out[1 + 4 * p:5 + 4 * p] for p, nm in enumerate(small_names)}

    shapes = {"norm1_pre_g": norm1_pre_g.shape, "w_in": w_in.shape, "b_in": b_in.shape, "w_pool": w_pool.shape,
              "pool_scale": pool_scale.shape, "sgu_ln_g": sgu_ln_g.shape, "sgu_ln_b": sgu_ln_b.shape,
              "w_spatial": w_spatial.shape, "b_spatial": b_spatial.shape, "w_sgu_proj": w_sgu_proj.shape,
              "w_out": w_out.shape, "norm1_post_g": norm1_post_g.shape, "norm2_pre_g": norm2_pre_g.shape,
              "w_ff1": w_ff1.shape, "w_ff2": w_ff2.shape, "norm2_post_g": norm2_post_g.shape}
    source = {"w_in": big["in"], "w_ff1": big["ff1"], "w_ff2": big["ff2"], **small}
    order = list(shapes)
    outs = [loss, dx.reshape(x.shape)]
    for kind in range(4):
        outs += [source[nm][kind].reshape(shapes[nm]) for nm in order]
    return tuple(outs)
```

```python
import functools
import math

import jax
import jax.numpy as jnp
from jax import lax
from jax.experimental import pallas as pl
from jax.experimental.pallas import tpu as pltpu

F32, BF16 = jnp.float32, jnp.bfloat16
MESH = pl.DeviceIdType.MESH

D_MODEL = 1024
D_IN = 5120
D_FF = 4096
N_DEV = 8
N_CHIPS = 4
WINDOWS = (2, 4, 8, 16)
N_GROUPS = 4
GROUP = 256
HALO = 16
SGU_BLOCK = 128
N_HEADS = 4
HEAD = 256
CHUNK = 64
EPS = 1e-6
IN_SHARD = D_IN // N_DEV
FF_SHARD = D_FF // N_DEV
OUT_SHARD = D_MODEL // N_DEV
PG_SHARD = GROUP // N_DEV

ADAM_LR, ADAM_B1, ADAM_B2, ADAM_EPS, ADAM_WD, ADAM_STEP = 0.001, 0.9, 0.999, 1e-08, 0.01, 10

VMEM_LIMIT = 56 * 1024 * 1024
TM = 256
TM_BWD = 256
TM_IN = 512
PROJ_COLS = 512
GELU_C0 = math.sqrt(2.0 / math.pi)
GELU_C1 = 0.044715


def _dot(a, b):
    return jnp.dot(a, b, preferred_element_type=F32)


def _dot_nt(a, b):
    return lax.dot_general(a, b, (((1,), (1,)), ((), ())), preferred_element_type=F32)


def _dot_tn(a, b):
    return lax.dot_general(a, b, (((0,), (0,)), ((), ())), preferred_element_type=F32)


def _gelu(x):
    t = jnp.tanh(GELU_C0 * (x + GELU_C1 * (x * x * x)))
    return 0.5 * x * (1.0 + t), t


def _gelu_grad(x, t):
    return 0.5 * (1.0 + t) + 0.5 * x * (1.0 - t * t) * (GELU_C0 * (1.0 + 3.0 * GELU_C1 * (x * x)))


def _sigmoid(x):
    return 1.0 / (1.0 + jnp.exp(-x))


def _mean(x):
    return jnp.mean(x, axis=-1, keepdims=True)


def _colsum(x):
    return jnp.sum(x, axis=0, keepdims=True)


def _const_spec(shape):
    nd = len(shape)
    return pl.BlockSpec(shape, lambda *_: (0,) * nd, pipeline_mode=pl.Buffered(1))


def _acc_spec(shape):
    nd = len(shape)
    return pl.BlockSpec(shape, lambda *_: (0,) * nd)


def _masked_ws(ws_ref):
    ri = lax.broadcasted_iota(jnp.int32, (SGU_BLOCK, SGU_BLOCK), 0) // CHUNK
    ci = lax.broadcasted_iota(jnp.int32, (SGU_BLOCK, SGU_BLOCK), 1) // CHUNK
    return [jnp.where(ri >= ci, ws_ref[h], 0.0).astype(BF16) for h in range(N_HEADS)]


def _pool_fwd(pbuf, tile_idx, tm):
    pos = lax.broadcasted_iota(jnp.int32, (tm, 1), 0) + tile_idx * tm + 1
    pooled = []
    for g, w in enumerate(WINDOWS):
        e = pbuf[:, g * GROUP:(g + 1) * GROUP]
        s, sh = e, 1
        while sh < w:
            s = s + pltpu.roll(s, sh, 0)
            sh *= 2
        inv = 1.0 / jnp.minimum(pos, w).astype(F32)
        pooled.append(s[HALO:] * inv - e[HALO:])
    return pooled


def _spatial_head(ws_h, vb, bsp_ref, h, nblk):
    return jnp.concatenate(
        [_dot(ws_h, vb[n * SGU_BLOCK:(n + 1) * SGU_BLOCK, h * HEAD:(h + 1) * HEAD]) + bsp_ref[:, h:h + 1]
         for n in range(nblk)], axis=0)


def _prenorm(after, x, g1pre):
    t_len = x.shape[0]
    tm = TM_IN

    def body(after_ref, x_ref, g_ref, xn_ref, xnt_ref):
        del after_ref
        xv = x_ref[...]
        xnb = (xv * lax.rsqrt(_mean(xv * xv) + EPS) * g_ref[...]).astype(BF16)
        xn_ref[...] = xnb
        xnt_ref[...] = xnb.T

    return pl.pallas_call(
        body, name="prenorm", grid=(t_len // tm,),
        in_specs=[_ANY, pl.BlockSpec((tm, D_MODEL), lambda i: (i, 0)), _const_spec((1, D_MODEL))],
        out_specs=[pl.BlockSpec((tm, D_MODEL), lambda i: (i, 0)), pl.BlockSpec((D_MODEL, tm), lambda i: (0, i))],
        out_shape=[jax.ShapeDtypeStruct((t_len, D_MODEL), BF16), jax.ShapeDtypeStruct((D_MODEL, t_len), BF16)],
        compiler_params=pltpu.CompilerParams(dimension_semantics=("arbitrary",)),
    )(after, x, g1pre)


def _fwd_mix(after, xn, win_g, b_in, wpool, pool_scale, lng, lnb, ws, bsp_t, wproj, wout):
    t_len = xn.shape[0]
    tm = TM
    nt = t_len // tm

    def body(after_ref, xn_ref, win_ref, bin_ref, wpool_ref, ps_ref, lng_ref, lnb_ref, ws_ref, bsp_ref,
             wproj_ref, wout_ref, z_ref, y_ref, u_ref, gpu_ref, xhat_ref, gpv_ref, sa_ref,
             sb_ref, zcur, znext, pbuf):
        del after_ref
        s = pl.program_id(0)

        @pl.when(s == 0)
        def _():
            znext[...] = jnp.zeros((tm, D_IN), F32)
            pbuf[...] = jnp.zeros((tm + HALO, D_MODEL), F32)

        zcur[...] = znext[...]
        xnb = xn_ref[...]

        def project(p):
            cols = slice(p * PROJ_COLS, (p + 1) * PROJ_COLS)
            zp = _dot(xnb, win_ref[:, cols]) + bin_ref[:, cols]
            if (p + 1) * PROJ_COLS <= D_MODEL:
                z_ref[:, cols] = zp
            znext[:, cols] = zp

        project(0)
        pbuf[0:HALO, :] = jnp.where(s <= 1, 0.0, pbuf[0:HALO, :])
        pbuf[HALO:, :] = zcur[:, 0:D_MODEL]
        pooled = _pool_fwd(pbuf, jnp.maximum(s - 1, 0), tm)
        pbuf[0:HALO, :] = pbuf[tm:tm + HALO, :]
        a = jnp.concatenate([_dot(pooled[g].astype(BF16), wpool_ref[g]) for g in range(N_GROUPS)], axis=1)
        a = a * ps_ref[...]
        project(1)
        zu = zcur[:, D_MODEL:2 * D_MODEL]
        u, tu = _gelu(zu)
        u_ref[...] = u.astype(BF16)
        gpu_ref[...] = _gelu_grad(zu, tu).astype(BF16)
        project(2)
        zv = zcur[:, 2 * D_MODEL:3 * D_MODEL]
        gv, tv = _gelu(zv)
        xc = gv - _mean(gv)
        rln = lax.rsqrt(_mean(xc * xc) + EPS)
        xhat = xc * rln
        xhat_ref[...] = xhat.astype(BF16)
        gpv_ref[...] = (_gelu_grad(zv, tv) * rln).astype(BF16)
        vb = (xhat * lng_ref[...] + lnb_ref[...]).astype(BF16)
        wsm = _masked_ws(ws_ref)
        bbr = []
        for h in range(N_HEADS):
            project(3 + h)
            sv = _spatial_head(wsm[h], vb, bsp_ref, h, tm // SGU_BLOCK)
            bbr.append(_dot((u[:, h * HEAD:(h + 1) * HEAD] * sv).astype(BF16), wproj_ref[h]))
        bbr = jnp.concatenate(bbr, axis=1)
        project(7)
        sa = _sigmoid(zcur[:, 3 * D_MODEL:4 * D_MODEL])
        sb = _sigmoid(zcur[:, 4 * D_MODEL:5 * D_MODEL])
        sa_ref[...] = sa.astype(BF16)
        sb_ref[...] = sb.astype(BF16)
        project(8)
        y_ref[...] = _dot((sa * a + sb * bbr).astype(BF16), wout_ref[...])
        project(9)

    proj = lambda w: pl.BlockSpec((tm, w), lambda s: (jnp.minimum(s, nt - 1), 0))
    mix = lambda w: pl.BlockSpec((tm, w), lambda s: (jnp.maximum(s - 1, 0), 0))
    return pl.pallas_call(
        body, name="fwd_mix", grid=(nt + 1,),
        in_specs=[_ANY, proj(D_MODEL),
                  _const_spec((D_MODEL, D_IN)),
                  _const_spec((1, D_IN)), _const_spec((N_GROUPS, GROUP, GROUP)), _const_spec((1, D_MODEL)),
                  _const_spec((1, D_MODEL)), _const_spec((1, D_MODEL)),
                  _const_spec((N_HEADS, SGU_BLOCK, SGU_BLOCK)), _const_spec((SGU_BLOCK, N_HEADS)),
                  _const_spec((N_HEADS, HEAD, HEAD)), _const_spec((D_MODEL, D_MODEL))],
        out_specs=[proj(D_MODEL), mix(D_MODEL)] + [mix(D_MODEL)] * 6,
        out_shape=[jax.ShapeDtypeStruct((t_len, D_MODEL), F32), jax.ShapeDtypeStruct((t_len, D_MODEL), F32)]
        + [jax.ShapeDtypeStruct((t_len, D_MODEL), BF16)] * 6,
        scratch_shapes=[pltpu.VMEM((tm, D_IN), F32), pltpu.VMEM((tm, D_IN), F32),
                        pltpu.VMEM((tm + HALO, D_MODEL), F32)],
        compiler_params=pltpu.CompilerParams(dimension_semantics=("arbitrary",), vmem_limit_bytes=VMEM_LIMIT),
    )(after, xn, win_g, b_in, wpool, pool_scale, lng, lnb, ws, bsp_t, wproj, wout)


def _mlp(x, y, target, g1post, g2pre, g2post, w1_g, w2):
    t_len = x.shape[0]
    tm = TM
    nt = t_len // tm

    def body(x_ref, y_ref, tgt_ref, g1post_ref, g2pre_ref, g2post_ref, w1_ref, w2_ref,
             hn_ref, f_ref, df1_ref, df2_ref, dh1_ref, dy_ref, dg2post_ref, dg2pre_ref, dg1post_ref, loss_ref, f1_scr):
        i = pl.program_id(0)

        @pl.when(i == 0)
        def _():
            dg2post_ref[...] = jnp.zeros_like(dg2post_ref)
            dg2pre_ref[...] = jnp.zeros_like(dg2pre_ref)
            dg1post_ref[...] = jnp.zeros_like(dg1post_ref)
            loss_ref[...] = jnp.zeros_like(loss_ref)

        yv = y_ref[...]
        ry = lax.rsqrt(_mean(yv * yv) + EPS)
        ny = yv * ry
        h = x_ref[...] + ny * g1post_ref[...]
        r2 = lax.rsqrt(_mean(h * h) + EPS)
        nh = h * r2
        hnb = (nh * g2pre_ref[...]).astype(BF16)
        hn_ref[...] = hnb.T
        for k in range(N_DEV):
            f1_scr[:, k * FF_SHARD:(k + 1) * FF_SHARD] = _dot(hnb, w1_ref[:, k * FF_SHARD:(k + 1) * FF_SHARD])
        r = jnp.maximum(f1_scr[...], 0.0)
        fb = (r * r).astype(BF16)
        f_ref[...] = fb.T
        f2 = _dot(fb, w2_ref[...])
        rf = lax.rsqrt(_mean(f2 * f2) + EPS)
        nf = f2 * rf
        diff = h + nf * g2post_ref[...] - tgt_ref[...]
        loss_ref[...] += (0.5 / D_MODEL) * jnp.sum(diff * diff)
        dout = diff * (1.0 / D_MODEL)
        dg2post_ref[...] += _colsum(dout * nf)
        dn = dout * g2post_ref[...]
        df2b = (rf * (dn - nf * _mean(dn * nf))).astype(BF16)
        df2_ref[...] = df2b
        df = _dot_nt(df2b, w2_ref[...])
        df1b = (df * (2.0 * jnp.maximum(f1_scr[...], 0.0))).astype(BF16)
        df1_ref[...] = df1b
        dhn = _dot_nt(df1b, w1_ref[...])
        dg2pre_ref[...] += _colsum(dhn * nh)
        dnh = dhn * g2pre_ref[...]
        dh1 = dout + r2 * (dnh - nh * _mean(dnh * nh))
        dh1_ref[...] = dh1
        dg1post_ref[...] += _colsum(dh1 * ny)
        dny = dh1 * g1post_ref[...]
        dy_ref[...] = (ry * (dny - ny * _mean(dny * ny))).astype(BF16)

    tok = lambda w: pl.BlockSpec((tm, w), lambda i: (i, 0))
    return pl.pallas_call(
        body, name="mlp_fwd_bwd", grid=(nt,),
        in_specs=[tok(D_MODEL), tok(D_MODEL), tok(D_MODEL), _const_spec((1, D_MODEL)), _const_spec((1, D_MODEL)),
                  _const_spec((1, D_MODEL)), _const_spec((D_MODEL, D_FF)), _const_spec((D_FF, D_MODEL))],
        out_specs=[pl.BlockSpec((D_MODEL, tm), lambda i: (0, i)), pl.BlockSpec((D_FF, tm), lambda i: (0, i)),
                   tok(D_FF), tok(D_MODEL), tok(D_MODEL), tok(D_MODEL),
                   _acc_spec((1, D_MODEL)), _acc_spec((1, D_MODEL)), _acc_spec((1, D_MODEL)), _acc_spec((1, 128))],
        out_shape=[jax.ShapeDtypeStruct((D_MODEL, t_len), BF16), jax.ShapeDtypeStruct((D_FF, t_len), BF16),
                   jax.ShapeDtypeStruct((t_len, D_FF), BF16), jax.ShapeDtypeStruct((t_len, D_MODEL), BF16),
                   jax.ShapeDtypeStruct((t_len, D_MODEL), F32), jax.ShapeDtypeStruct((t_len, D_MODEL), BF16),
                   jax.ShapeDtypeStruct((1, D_MODEL), F32), jax.ShapeDtypeStruct((1, D_MODEL), F32),
                   jax.ShapeDtypeStruct((1, D_MODEL), F32), jax.ShapeDtypeStruct((1, 128), F32)],
        scratch_shapes=[pltpu.VMEM((tm, D_FF), F32)],
        compiler_params=pltpu.CompilerParams(dimension_semantics=("arbitrary",), vmem_limit_bytes=VMEM_LIMIT),
    )(x, y, target, g1post, g2pre, g2post, w1_g, w2)


def _bwd_mix(after, dy, z, saved, wpool, pool_scale, lng, lnb, ws, bsp_t, wproj, wout):
    t_len = dy.shape[0]
    tm = TM_BWD
    nt = t_len // tm
    nblk = tm // SGU_BLOCK

    def body(after_ref, dy_ref, z_ref, zh_ref, u_ref, gpu_ref, xhat_ref, gpv_ref, sa_ref, sb_ref,
             wpool_ref, ps_ref, lng_ref, lnb_ref, ws_ref,
             bsp_ref, wproj_ref, wout_ref,
             dz_ref, mg_ref, dwpool_ref, dwproj_ref, dws_ref, dbsp_ref, dps_ref,
             dlng_ref, dlnb_ref, dbin_ref, pbuf, qbuf):
        del after_ref
        i = pl.program_id(0)
        ti = nt - 1 - i

        @pl.when(i == 0)
        def _():
            for ref in (dwpool_ref, dwproj_ref, dws_ref, dbsp_ref, dps_ref, dlng_ref, dlnb_ref,
                        dbin_ref):
                ref[...] = jnp.zeros_like(ref)
            qbuf[tm:tm + HALO, :] = jnp.zeros((HALO, D_MODEL), F32)

        pbuf[0:HALO, :] = jnp.where(ti > 0, zh_ref[...], 0.0)
        pbuf[HALO:, :] = z_ref[...]
        pooled = _pool_fwd(pbuf, ti, tm)
        pooled_b = [p.astype(BF16) for p in pooled]
        a_raw = jnp.concatenate([_dot(pooled_b[g], wpool_ref[g]) for g in range(N_GROUPS)], axis=1)
        wsm = _masked_ws(ws_ref)
        u = u_ref[...].astype(F32)
        xhat = xhat_ref[...].astype(F32)
        vb = (xhat * lng_ref[...] + lnb_ref[...]).astype(BF16)
        sv_heads = [_spatial_head(wsm[h], vb, bsp_ref, h, nblk) for h in range(N_HEADS)]
        gated_b = [(u[:, h * HEAD:(h + 1) * HEAD] * sv_heads[h]).astype(BF16) for h in range(N_HEADS)]
        bbr = jnp.concatenate([_dot(gated_b[h], wproj_ref[h]) for h in range(N_HEADS)], axis=1)
        sa = sa_ref[...].astype(F32)
        sb = sb_ref[...].astype(F32)
        a = a_raw * ps_ref[...]
        mg_ref[...] = (sa * a + sb * bbr).astype(BF16).T

        dmg = _dot_nt(dy_ref[...], wout_ref[...])

        da = dmg * sa
        dbbr = dmg * sb
        dzga = dmg * a * sa * (1.0 - sa)
        dzgb = dmg * bbr * sb * (1.0 - sb)
        dz_ref[:, 3 * D_MODEL:4 * D_MODEL] = dzga.astype(BF16)
        dz_ref[:, 4 * D_MODEL:5 * D_MODEL] = dzgb.astype(BF16)
        dbin_ref[:, 3 * D_MODEL:4 * D_MODEL] += _colsum(dzga)
        dbin_ref[:, 4 * D_MODEL:5 * D_MODEL] += _colsum(dzgb)

        dps_ref[...] += _colsum(da * a_raw)
        da_raw_b = (da * ps_ref[...]).astype(BF16)
        pos = lax.broadcasted_iota(jnp.int32, (tm, 1), 0) + ti * tm + 1
        dpooled = []
        for g, w in enumerate(WINDOWS):
            cols = slice(g * GROUP, (g + 1) * GROUP)
            dwpool_ref[g] += _dot_tn(pooled_b[g], da_raw_b[:, cols])
            dp = _dot_nt(da_raw_b[:, cols], wpool_ref[g])
            dpooled.append(dp)
            qbuf[0:tm, cols] = dp * (1.0 / jnp.minimum(pos, w).astype(F32))
        n_ext = tm + HALO
        dzp = []
        for g, w in enumerate(WINDOWS):
            e = qbuf[:, g * GROUP:(g + 1) * GROUP]
            s, sh = e, 1
            while sh < w:
                s = s + pltpu.roll(s, n_ext - sh, 0)
                sh *= 2
            dzp.append(s[0:tm] - dpooled[g])
        qbuf[tm:tm + HALO, :] = qbuf[0:HALO, :]
        dzp = jnp.concatenate(dzp, axis=1)
        dz_ref[:, 0:D_MODEL] = dzp.astype(BF16)
        dbin_ref[:, 0:D_MODEL] += _colsum(dzp)

        dv_heads = []
        du_heads = []
        for h in range(N_HEADS):
            cols = slice(h * HEAD, (h + 1) * HEAD)
            dbbr_b = dbbr[:, cols].astype(BF16)
            dwproj_ref[h] += _dot_tn(gated_b[h], dbbr_b)
            dgated = _dot_nt(dbbr_b, wproj_ref[h])
            du_heads.append(dgated * sv_heads[h])
            dsv = dgated * u[:, cols]
            dsv_b = dsv.astype(BF16)
            rows = []
            for n in range(nblk):
                blk = slice(n * SGU_BLOCK, (n + 1) * SGU_BLOCK)
                rows.append(_dot_tn(wsm[h], dsv_b[blk]))
                dws_ref[h] += _dot_nt(dsv_b[blk], vb[blk, cols])
                dbsp_ref[:, h:h + 1] += jnp.sum(dsv[blk], axis=1, keepdims=True)
            dv_heads.append(jnp.concatenate(rows, axis=0))
        dzu = jnp.concatenate(du_heads, axis=1) * gpu_ref[...].astype(F32)
        dz_ref[:, D_MODEL:2 * D_MODEL] = dzu.astype(BF16)
        dbin_ref[:, D_MODEL:2 * D_MODEL] += _colsum(dzu)
        dv = jnp.concatenate(dv_heads, axis=1)
        dlng_ref[...] += _colsum(dv * xhat)
        dlnb_ref[...] += _colsum(dv)
        dxh = dv * lng_ref[...]
        dzv = (dxh - _mean(dxh) - xhat * _mean(dxh * xhat)) * gpv_ref[...].astype(F32)
        dz_ref[:, 2 * D_MODEL:3 * D_MODEL] = dzv.astype(BF16)
        dbin_ref[:, 2 * D_MODEL:3 * D_MODEL] += _colsum(dzv)

        @pl.when(i == nt - 1)
        def _():
            ri = lax.broadcasted_iota(jnp.int32, (SGU_BLOCK, SGU_BLOCK), 0) // CHUNK
            ci = lax.broadcasted_iota(jnp.int32, (SGU_BLOCK, SGU_BLOCK), 1) // CHUNK
            for h in range(N_HEADS):
                dws_ref[h] = jnp.where(ri >= ci, dws_ref[h], 0.0)

    tok = lambda w: pl.BlockSpec((tm, w), lambda i: (nt - 1 - i, 0))
    halo = pl.BlockSpec((HALO, D_MODEL), lambda i: (jnp.maximum((nt - 1 - i) * (tm // HALO) - 1, 0), 0))
    return pl.pallas_call(
        body, name="bwd_mix", grid=(nt,),
        in_specs=[_ANY, tok(D_MODEL), tok(D_MODEL), halo] + [tok(D_MODEL)] * 6
        + [_const_spec((N_GROUPS, GROUP, GROUP)),
                  _const_spec((1, D_MODEL)), _const_spec((1, D_MODEL)), _const_spec((1, D_MODEL)),
                  _const_spec((N_HEADS, SGU_BLOCK, SGU_BLOCK)), _const_spec((SGU_BLOCK, N_HEADS)),
                  _const_spec((N_HEADS, HEAD, HEAD)), _const_spec((D_MODEL, D_MODEL))],
        out_specs=[tok(D_IN), pl.BlockSpec((D_MODEL, tm), lambda i: (0, nt - 1 - i)),
                   _acc_spec((N_GROUPS, GROUP, GROUP)), _acc_spec((N_HEADS, HEAD, HEAD)),
                   _acc_spec((N_HEADS, SGU_BLOCK, SGU_BLOCK)), _acc_spec((SGU_BLOCK, N_HEADS)),
                   _acc_spec((1, D_MODEL)), _acc_spec((1, D_MODEL)), _acc_spec((1, D_MODEL)),
                   _acc_spec((1, D_IN))],
        out_shape=[jax.ShapeDtypeStruct((t_len, D_IN), BF16), jax.ShapeDtypeStruct((D_MODEL, t_len), BF16),
                   jax.ShapeDtypeStruct((N_GROUPS, GROUP, GROUP), F32), jax.ShapeDtypeStruct((N_HEADS, HEAD, HEAD), F32),
                   jax.ShapeDtypeStruct((N_HEADS, SGU_BLOCK, SGU_BLOCK), F32),
                   jax.ShapeDtypeStruct((SGU_BLOCK, N_HEADS), F32),
                   jax.ShapeDtypeStruct((1, D_MODEL), F32), jax.ShapeDtypeStruct((1, D_MODEL), F32),
                   jax.ShapeDtypeStruct((1, D_MODEL), F32), jax.ShapeDtypeStruct((1, D_IN), F32)],
        scratch_shapes=[pltpu.VMEM((tm + HALO, D_MODEL), F32), pltpu.VMEM((tm + HALO, D_MODEL), F32)],
        compiler_params=pltpu.CompilerParams(dimension_semantics=("arbitrary",), vmem_limit_bytes=VMEM_LIMIT),
    )(after, dy, z, z, *saved, wpool, pool_scale, lng, lnb, ws, bsp_t, wproj, wout)


def _bwd_in(after, dz_b, x, dh1, g1pre, win_g):
    t_len = x.shape[0]
    tm = TM_IN
    nt = t_len // tm

    def body(after_ref, dz_ref, x_ref, dh1_ref, g1_ref, win_ref, dx_ref, dg1pre_ref):
        del after_ref

        @pl.when(pl.program_id(0) == 0)
        def _():
            dg1pre_ref[...] = jnp.zeros_like(dg1pre_ref)

        dxn = _dot_nt(dz_ref[...], win_ref[...])
        xv = x_ref[...]
        r1 = lax.rsqrt(_mean(xv * xv) + EPS)
        nx = xv * r1
        dg1pre_ref[...] += _colsum(dxn * nx)
        dnx = dxn * g1_ref[...]
        dx_ref[...] = r1 * (dnx - nx * _mean(dnx * nx)) + dh1_ref[...]

    tok = lambda w: pl.BlockSpec((tm, w), lambda i: (i, 0))
    return pl.pallas_call(
        body, name="bwd_in", grid=(nt,),
        in_specs=[_ANY, tok(D_IN), tok(D_MODEL), tok(D_MODEL), _const_spec((1, D_MODEL)),
                  _const_spec((D_MODEL, D_IN))],
        out_specs=[tok(D_MODEL), _acc_spec((1, D_MODEL))],
        out_shape=[jax.ShapeDtypeStruct((t_len, D_MODEL), F32), jax.ShapeDtypeStruct((1, D_MODEL), F32)],
        compiler_params=pltpu.CompilerParams(dimension_semantics=("arbitrary",), vmem_limit_bytes=VMEM_LIMIT),
    )(after, dz_b, x, dh1, g1pre, win_g)


def _owner_of_slot(s):
    return 4 * ((s // 2) % 2) + 2 * (s % 2) + s // 4


def _wgrad(name, a_t, b, shard, a_sliced):
    t_len = b.shape[0]
    am = shard if a_sliced else a_t.shape[0]
    bn = b.shape[1] if a_sliced else shard

    def body(a_ref, b_ref, o_ref):
        o_ref[...] = _dot(a_ref[...], b_ref[...]).astype(BF16)

    if a_sliced:
        in_specs = [pl.BlockSpec((am, t_len), lambda s: (_owner_of_slot(s), 0)),
                    pl.BlockSpec((t_len, bn), lambda s: (0, 0), pipeline_mode=pl.Buffered(1))]
    else:
        in_specs = [pl.BlockSpec((am, t_len), lambda s: (0, 0), pipeline_mode=pl.Buffered(1)),
                    pl.BlockSpec((t_len, bn), lambda s: (0, _owner_of_slot(s)))]
    return pl.pallas_call(
        body, name=name, grid=(N_DEV,), in_specs=in_specs,
        out_specs=pl.BlockSpec((None, am, bn), lambda s: (s, 0, 0)),
        out_shape=jax.ShapeDtypeStruct((N_DEV, am, bn), BF16),
        compiler_params=pltpu.CompilerParams(dimension_semantics=("arbitrary",), vmem_limit_bytes=VMEM_LIMIT),
    )(a_t, b)


def _wgrad_in(xn_t, dz_b):
    t_len = dz_b.shape[0]

    def body(a_ref, b_ref, o_ref):
        res = _dot(a_ref[...], b_ref[...])
        o_ref[0] = res[:, 0:IN_SHARD].astype(BF16)
        o_ref[1] = res[:, IN_SHARD:2 * IN_SHARD].astype(BF16)

    out = pl.pallas_call(
        body, name="wgrad_in", grid=(N_CHIPS,),
        in_specs=[pl.BlockSpec((D_MODEL, t_len), lambda q: (0, 0), pipeline_mode=pl.Buffered(1)),
                  pl.BlockSpec((t_len, 2 * IN_SHARD), lambda q: (0, q))],
        out_specs=pl.BlockSpec((2, None, D_MODEL, IN_SHARD), lambda q: (0, q, 0, 0)),
        out_shape=jax.ShapeDtypeStruct((2, N_CHIPS, D_MODEL, IN_SHARD), BF16),
        compiler_params=pltpu.CompilerParams(dimension_semantics=("arbitrary",), vmem_limit_bytes=VMEM_LIMIT),
    )(xn_t, dz_b)
    return out.reshape(N_DEV, D_MODEL, IN_SHARD)


def _wgrad_out(mg_t, dy_b):
    t_len = dy_b.shape[0]

    def body(a_ref, b_ref, o_ref):
        res = _dot(a_ref[...], b_ref[...])
        o_ref[0] = res[0:OUT_SHARD].astype(BF16)
        o_ref[1] = res[OUT_SHARD:2 * OUT_SHARD].astype(BF16)

    out = pl.pallas_call(
        body, name="wgrad_out", grid=(N_CHIPS,),
        in_specs=[pl.BlockSpec((2 * OUT_SHARD, t_len), lambda q: (q, 0)),
                  pl.BlockSpec((t_len, D_MODEL), lambda q: (0, 0), pipeline_mode=pl.Buffered(1))],
        out_specs=pl.BlockSpec((2, None, OUT_SHARD, D_MODEL), lambda q: (0, q, 0, 0)),
        out_shape=jax.ShapeDtypeStruct((2, N_CHIPS, OUT_SHARD, D_MODEL), BF16),
        compiler_params=pltpu.CompilerParams(dimension_semantics=("arbitrary",), vmem_limit_bytes=VMEM_LIMIT),
    )(mg_t, dy_b)
    return out.reshape(N_DEV, OUT_SHARD, D_MODEL)


def _coords():
    return lax.axis_index("x"), lax.axis_index("y"), lax.axis_index("c")


_ANY = pl.BlockSpec(memory_space=pl.ANY)


def _pair_exchange(name, bigs, smalls):
    nb, n = len(bigs), len(bigs) + len(smalls)

    def body(*refs):
        src, dst = refs[:n], refs[n:2 * n]
        send_sems, recv_sems = refs[2 * n:]
        x, y, c = _coords()
        copies = []
        for j in range(n):
            s = src[j].at[pl.ds(4 * (1 - c), 4)] if j < nb else src[j]
            cp = pltpu.make_async_remote_copy(src_ref=s, dst_ref=dst[j], send_sem=send_sems.at[j],
                                              recv_sem=recv_sems.at[j], device_id=(x, y, 1 - c), device_id_type=MESH)
            cp.start()
            copies.append(cp)
        for cp in copies:
            cp.wait()

    return pl.pallas_call(
        body, name=name,
        in_specs=[_ANY] * n, out_specs=[_ANY] * n,
        out_shape=[jax.ShapeDtypeStruct((4,) + b.shape[1:], b.dtype) for b in bigs]
        + [jax.ShapeDtypeStruct(s.shape, s.dtype) for s in smalls],
        scratch_shapes=[pltpu.SemaphoreType.DMA((n,)), pltpu.SemaphoreType.DMA((n,))],
    )(*bigs, *smalls)


_HBM = pl.BlockSpec(memory_space=pltpu.HBM)
_SEM = pl.BlockSpec(memory_space=pltpu.SEMAPHORE)
_VMEM = pl.BlockSpec(memory_space=pltpu.VMEM)
_EFFECT = pltpu.SideEffectType.DATAFLOW_SIDE_EFFECTING
_TOKEN = jax.ShapeDtypeStruct((8, 128), F32)


def _in_hbm(a):
    return pltpu.with_memory_space_constraint(a, pltpu.HBM)


def _split_call(name, body, n_sems_out, arrays, sems_in=(), after=None):
    na, ns = len(arrays), len(sems_in)
    has_after = after is not None

    def kernel_body(*refs):
        arr = refs[:na]
        s_in = refs[na:na + ns]
        outs = refs[na + ns + has_after:]
        body(arr, s_in, outs[:n_sems_out])
        outs[-1][...] = jnp.zeros((8, 128), F32)

    out_shape = ([pltpu.SemaphoreType.DMA(())] * n_sems_out + [pltpu.HBM(a.shape, a.dtype) for a in arrays] + [_TOKEN])
    res = pl.pallas_call(
        kernel_body, name=name, out_shape=out_shape,
        in_specs=[_HBM] * na + [_SEM] * ns + [_ANY] * has_after,
        out_specs=[_SEM] * n_sems_out + [_HBM] * na + [_VMEM],
        input_output_aliases={i: n_sems_out + i for i in range(na)},
        compiler_params=pltpu.CompilerParams(has_side_effects=_EFFECT),
    )(*[_in_hbm(a) for a in arrays], *sems_in, *([after] if has_after else []))
    return list(res[:n_sems_out]), list(res[n_sems_out:n_sems_out + na]), res[-1]


def _wait_bytes_of(ref, send_sem, recv_sem, peer, send=True, recv=True):
    cp = pltpu.make_async_remote_copy(src_ref=ref, dst_ref=ref, send_sem=send_sem, recv_sem=recv_sem,
                                      device_id=peer, device_id_type=MESH)
    if send:
        cp.wait_send()
    if recv:
        cp.wait_recv()


def _gather_behind(tag, shards, after, work, by_columns=()):
    n = len(shards)
    lands = [lax.empty((s.shape[0], N_DEV * s.shape[1]) if j in by_columns else (N_DEV,) + s.shape, s.dtype)
             for j, s in enumerate(shards)]

    def slots(arr, j, first, count=1):
        if j in by_columns:
            cols = shards[j].shape[1]
            return arr[n + j].at[:, pl.ds(pl.multiple_of(first * cols, 128), count * cols)]
        return arr[n + j].at[first] if count == 1 else arr[n + j].at[pl.ds(first, count)]

    def own_slot(arr, j, sem):
        x, y, c = _coords()
        return pltpu.make_async_copy(arr[j], slots(arr, j, 4 * x + 2 * y + c), sem)

    def start(arr, _, sems):
        x, y, c = _coords()
        me = 4 * x + 2 * y + c
        for j in range(n):
            for chip in [(1 - x, y), (x, 1 - y), (1 - x, 1 - y)]:
                pltpu.make_async_remote_copy(src_ref=arr[j], dst_ref=slots(arr, j, me), send_sem=sems[j],
                                             recv_sem=sems[n + j], device_id=(*chip, c), device_id_type=MESH).start()
        for j in range(n):
            pltpu.make_async_remote_copy(src_ref=arr[j], dst_ref=slots(arr, j, me), send_sem=sems[2 * n + j],
                                         recv_sem=sems[3 * n + j], device_id=(x, y, 1 - c),
                                         device_id_type=MESH).start()
            own_slot(arr, j, sems[4 * n + j]).start()

    def middle(arr, s_in, sems):
        x, y, c = _coords()
        sibling = (x, y, 1 - c)
        for j in range(n):
            _wait_bytes_of(slots(arr, j, 0, 3), s_in[j], s_in[n + j], sibling)
            for chip in [(1 - x, y), (x, 1 - y), (1 - x, 1 - y)]:
                slot = slots(arr, j, 4 * chip[0] + 2 * chip[1] + c)
                pltpu.make_async_remote_copy(src_ref=slot, dst_ref=slot, send_sem=sems[j], recv_sem=sems[n + j],
                                             device_id=sibling, device_id_type=MESH).start()

    def finish(arr, s_in, _):
        x, y, c = _coords()
        sibling = (x, y, 1 - c)
        for j in range(n):
            _wait_bytes_of(slots(arr, j, 0, 1), s_in[j], s_in[n + j], sibling)
            own_slot(arr, j, s_in[2 * n + j]).wait()
            _wait_bytes_of(slots(arr, j, 0, 3), s_in[3 * n + j], s_in[4 * n + j], sibling)

    sems, arrays, token = _split_call("gather_%s_start" % tag, start, 5 * n, list(shards) + lands, after=after)
    result = work(token)
    fwd_sems, arrays, token = _split_call("gather_%s_middle" % tag, middle, 2 * n, arrays, sems_in=sems[:2 * n],
                                          after=result[0])
    _, arrays, _ = _split_call("gather_%s_finish" % tag, finish, 0, arrays, sems_in=sems[2 * n:] + fwd_sems,
                               after=token)
    return arrays[n:], result


def _chip_exchange_behind(tag, bigs, smalls, work):
    nb, n = len(bigs), len(bigs) + len(smalls)
    lands = [lax.empty(s.shape, s.dtype) for s in bigs] + [lax.empty((N_CHIPS,) + s.shape, s.dtype) for s in smalls]

    def own_slot(arr, j, sem):
        x, y, _ = _coords()
        q_me = 2 * x + y
        return pltpu.make_async_copy(arr[j].at[q_me] if j < nb else arr[j], arr[n + j].at[q_me], sem)

    def start(arr, _, sems):
        x, y, c = _coords()
        q_me = 2 * x + y
        for j in range(n):
            for peer in [(1 - x, y, c), (x, 1 - y, c), (1 - x, 1 - y, c)]:
                piece = arr[j].at[2 * peer[0] + peer[1]] if j < nb else arr[j]
                pltpu.make_async_remote_copy(src_ref=piece, dst_ref=arr[n + j].at[q_me],
                                             send_sem=sems[j], recv_sem=sems[n + j], device_id=peer,
                                             device_id_type=MESH).start()
            own_slot(arr, j, sems[2 * n + j]).start()

    def finish(arr, s_in, _):
        x, y, c = _coords()
        for j in range(n):
            _wait_bytes_of(arr[n + j].at[pl.ds(0, 3)], s_in[j], s_in[n + j], (x, y, 1 - c))
            own_slot(arr, j, s_in[2 * n + j]).wait()

    sems, arrays, token = _split_call("chip_exchange_%s_start" % tag, start, 3 * n, list(bigs) + list(smalls) + lands)
    result = work(token)
    _, arrays, _ = _split_call("chip_exchange_%s_finish" % tag, finish, 0, arrays, sems_in=sems, after=result[0])
    return arrays[n:], result


def _pair_sum_big(name, c_arr, mine, theirs):
    _, rows, cols = theirs.shape
    tr = rows

    def body(c_ref, a_ref, b_ref, o_ref):
        del c_ref
        o_ref[...] = (a_ref[...].astype(F32) + b_ref[...].astype(F32)).astype(BF16)

    return pl.pallas_call(
        body, name=name,
        grid_spec=pltpu.PrefetchScalarGridSpec(
            num_scalar_prefetch=1, grid=(N_CHIPS, rows // tr),
            in_specs=[pl.BlockSpec((None, tr, cols), lambda q, r, c_ref: (4 * c_ref[0] + q, r, 0)),
                      pl.BlockSpec((None, tr, cols), lambda q, r, c_ref: (q, r, 0))],
            out_specs=pl.BlockSpec((None, tr, cols), lambda q, r, c_ref: (q, r, 0))),
        out_shape=jax.ShapeDtypeStruct(theirs.shape, BF16),
        compiler_params=pltpu.CompilerParams(dimension_semantics=("arbitrary", "arbitrary")),
    )(c_arr, mine, theirs)


def _pair_sum_small(name, mine, theirs):
    n = len(mine)

    def body(*refs):
        for j in range(n):
            refs[2 * n + j][...] = refs[j][...] + refs[n + j][...]

    return pl.pallas_call(
        body, name=name,
        out_shape=[jax.ShapeDtypeStruct(m.shape, m.dtype) for m in mine],
    )(*mine, *theirs)


def _adamw_math(w, g, m, v):
    m = ADAM_B1 * m + (1.0 - ADAM_B1) * g
    v = ADAM_B2 * v + (1.0 - ADAM_B2) * (g * g)
    m_hat = m / (1.0 - ADAM_B1 ** ADAM_STEP)
    v_hat = v / (1.0 - ADAM_B2 ** ADAM_STEP)
    delta = -ADAM_LR * (m_hat / (jnp.sqrt(v_hat) + ADAM_EPS) + ADAM_WD * w)
    return delta, m, v


def _adamw_big(name, after, chip_sums, w, m, v):
    rows, cols = w.shape
    tr = min(rows, 512)

    def body(after_ref, t_ref, w_ref, m_ref, v_ref, g_out, d_out, m_out, v_out):
        del after_ref
        g = t_ref[0].astype(F32)
        for q in range(1, N_CHIPS):
            g = g + t_ref[q].astype(F32)
        d, mn, vn = _adamw_math(w_ref[...], g, m_ref[...], v_ref[...])
        g_out[...] = g
        d_out[...] = d
        m_out[...] = mn
        v_out[...] = vn

    blk = pl.BlockSpec((tr, cols), lambda r: (r, 0))
    return pl.pallas_call(
        body, name=name, grid=(rows // tr,),
        in_specs=[_ANY, pl.BlockSpec((N_CHIPS, tr, cols), lambda r: (0, r, 0)), blk, blk, blk],
        out_specs=[blk] * 4,
        out_shape=[jax.ShapeDtypeStruct((rows, cols), F32)] * 4,
        compiler_params=pltpu.CompilerParams(dimension_semantics=("arbitrary",), vmem_limit_bytes=VMEM_LIMIT),
    )(after, chip_sums, w, m, v)


SMALL_ROWS = ("loss", "norm1_pre_g", "pool_scale", "sgu_ln_g", "sgu_ln_b", "norm1_post_g", "norm2_pre_g",
              "norm2_post_g")


N_ROW_PARAMS = len(SMALL_ROWS) - 1


def _adamw_small(u_rows, u_others, params):
    n, n_oth = len(params), len(u_others)

    def body(*refs):
        urow_ref, others = refs[0], refs[1:1 + n_oth]
        wmv = refs[1 + n_oth:1 + n_oth + 3 * n]
        loss_ref = refs[1 + n_oth + 3 * n]
        outs = refs[2 + n_oth + 3 * n:]

        def total(ref, idx):
            g = ref[(0,) + idx].astype(F32)
            for q in range(1, N_CHIPS):
                g = g + ref[(q,) + idx].astype(F32)
            return g

        loss_ref[...] = total(urow_ref, (slice(0, 1), slice(None)))
        for p in range(n):
            if p < N_ROW_PARAMS:
                g = total(urow_ref, (slice(p + 1, p + 2), slice(None)))
            else:
                g = total(others[p - N_ROW_PARAMS], (slice(None), slice(None)))
            d, mn, vn = _adamw_math(wmv[3 * p][...], g, wmv[3 * p + 1][...], wmv[3 * p + 2][...])
            outs[4 * p][...] = g
            outs[4 * p + 1][...] = d
            outs[4 * p + 2][...] = mn
            outs[4 * p + 3][...] = vn

    flat = [a for p in params for a in p]
    out_shape = [jax.ShapeDtypeStruct((1, D_MODEL), F32)]
    for w, _, _ in params:
        out_shape += [jax.ShapeDtypeStruct(w.shape, F32)] * 4
    return pl.pallas_call(body, name="adamw_small", out_shape=out_shape)(u_rows, *u_others, *flat)


def _slots_of_rows(full):
    owners = [_owner_of_slot(s) for s in range(N_DEV)]
    parts = full.reshape(N_GROUPS, N_DEV, PG_SHARD, GROUP)
    return jnp.stack([parts[:, o] for o in owners]).reshape(N_DEV, N_GROUPS * PG_SHARD, GROUP).astype(BF16)


def kernel(x, norm1_pre_g, w_in, b_in, w_pool, pool_scale, sgu_ln_g, sgu_ln_b, w_spatial, b_spatial, w_sgu_proj, w_out, norm1_post_g, norm2_pre_g, w_ff1, w_ff2, norm2_post_g, loss_target, m_norm1_pre_g, m_w_in, m_b_in, m_w_pool, m_pool_scale, m_sgu_ln_g, m_sgu_ln_b, m_w_spatial, m_b_spatial, m_w_sgu_proj, m_w_out, m_norm1_post_g, m_norm2_pre_g, m_w_ff1, m_w_ff2, m_norm2_post_g, v_norm1_pre_g, v_w_in, v_b_in, v_w_pool, v_pool_scale, v_sgu_ln_g, v_sgu_ln_b, v_w_spatial, v_b_spatial, v_w_sgu_proj, v_w_out, v_norm1_post_g, v_norm2_pre_g, v_w_ff1, v_w_ff2, v_norm2_post_g):
    t_len = x.shape[1]
    row = lambda a: a.reshape(1, -1)
    x2 = x.reshape(t_len, D_MODEL)
    tgt2 = loss_target.reshape(t_len, D_MODEL)
    pg2 = lambda a: a.reshape(N_GROUPS * PG_SHARD, GROUP)

    def prework(token):
        return _prenorm(token, x2, row(norm1_pre_g))

    (win_f, g_pool, g_proj, g_out), (xn, xn_b) = _gather_behind(
        "mix", [w_in.astype(BF16), pg2(w_pool).astype(BF16), pg2(w_sgu_proj).astype(BF16), w_out.astype(BF16)],
        None, prework, by_columns=(0,))
    regroup = lambda g: g.reshape(N_DEV, N_GROUPS, PG_SHARD, GROUP).transpose(1, 0, 2, 3).reshape(N_GROUPS, GROUP, GROUP)
    wpool_f, wproj_f = regroup(g_pool), regroup(g_proj)
    wout_f = g_out.reshape(D_MODEL, D_MODEL)
    bsp_t = b_spatial.T

    def forward(token):
        z, y, *saved = _fwd_mix(token, xn, win_f, row(b_in), wpool_f, row(pool_scale),
                                row(sgu_ln_g), row(sgu_ln_b), w_spatial, bsp_t, wproj_f, wout_f)
        return y, z, saved

    (g_ff1, g_ff2), (y, z, saved) = _gather_behind("ff", [w_ff1.astype(BF16), w_ff2.astype(BF16)], win_f,
                                                   forward, by_columns=(0,))
    w2_f = g_ff2.reshape(D_FF, D_MODEL)
    hn_b, f_b, df1_b, df2_b, dh1, dy_b, dg2post, dg2pre, dg1post, loss_p = _mlp(
        x2, y, tgt2, row(norm1_post_g), row(norm2_pre_g), row(norm2_post_g), g_ff1, w2_f)
    p_ff2 = _wgrad("wgrad_ff2", f_b, df2_b, FF_SHARD, True)
    p_ff1 = _wgrad("wgrad_ff1", hn_b, df1_b, FF_SHARD, False)
    c_arr = lax.axis_index("c").astype(jnp.int32).reshape(1)
    ff_parts = [p_ff1, p_ff2]
    got_ff = _pair_exchange("pair_exchange_ff", ff_parts, [])
    chip_ff = [_pair_sum_big("pair_sum_" + nm, c_arr, b, r) for nm, b, r in zip(("ff1", "ff2"), ff_parts, got_ff)]

    def backward_mix(token):
        return _bwd_mix(token, dy_b, z, saved, wpool_f, row(pool_scale), row(sgu_ln_g), row(sgu_ln_b), w_spatial,
                        bsp_t, wproj_f, wout_f)

    summed_ff, (dz_b, mg_b, dwpool, dwproj, dws, dbsp_t, dps, dlng, dlnb,
                dbin) = _chip_exchange_behind("ff", chip_ff, [], backward_mix)

    p_in = _wgrad_in(xn_b, dz_b)
    bigs = [p_in, _slots_of_rows(dwpool), _slots_of_rows(dwproj)]
    rows = jnp.concatenate([jnp.broadcast_to(loss_p[:, 0:1], (1, D_MODEL)), dps, dlng, dlnb, dg1post, dg2pre, dg2post],
                           axis=0)
    smalls = [rows, dbin, dws.reshape(N_HEADS * SGU_BLOCK, SGU_BLOCK), dbsp_t.T]
    got = _pair_exchange("pair_exchange_in", bigs, smalls)
    chip_bigs = [_pair_sum_big("pair_sum_" + nm, c_arr, b, r) for nm, b, r in zip(("in", "pool", "proj"), bigs, got[:3])]
    chip_smalls = _pair_sum_small("pair_sum_small", smalls, got[3:])

    def backward_rest(token):
        dx, dg1pre = _bwd_in(token, dz_b, x2, dh1, row(norm1_pre_g), win_f)
        p_out = _wgrad_out(mg_b, dy_b)
        got_out = _pair_exchange("pair_exchange_out", [p_out], [dg1pre])
        chip_out = _pair_sum_big("pair_sum_out", c_arr, p_out, got_out[0])
        upd_ff = [_adamw_big("adamw_" + nm, p_out, t, *p) for nm, t, p in
                  (("ff1", summed_ff[0], (w_ff1, m_w_ff1, v_w_ff1)), ("ff2", summed_ff[1], (w_ff2, m_w_ff2, v_w_ff2)))]
        return upd_ff[1][0], dx, chip_out, _pair_sum_small("pair_sum_g1pre", [dg1pre], got_out[1:])[0], upd_ff

    summed_in, (_, dx, chip_out, chip_g1pre, upd_ff) = _chip_exchange_behind("in", chip_bigs, chip_smalls,
                                                                             backward_rest)

    def update_in(token):
        res = _adamw_big("adamw_in", token, summed_in[0], w_in, m_w_in, v_w_in)
        return res[0], res

    summed_out, (_, upd_in) = _chip_exchange_behind("out", [chip_out], [chip_g1pre], update_in)
    big = {"in": upd_in, "ff1": upd_ff[0], "ff2": upd_ff[1]}
    u_rows = jnp.concatenate([summed_in[3][:, 0:1], summed_out[1], summed_in[3][:, 1:]], axis=1)
    ws2 = lambda a: a.reshape(N_HEADS * SGU_BLOCK, SGU_BLOCK)
    small_params = [(row(norm1_pre_g), row(m_norm1_pre_g), row(v_norm1_pre_g)),
                    (row(pool_scale), row(m_pool_scale), row(v_pool_scale)),
                    (row(sgu_ln_g), row(m_sgu_ln_g), row(v_sgu_ln_g)),
                    (row(sgu_ln_b), row(m_sgu_ln_b), row(v_sgu_ln_b)),
                    (row(norm1_post_g), row(m_norm1_post_g), row(v_norm1_post_g)),
                    (row(norm2_pre_g), row(m_norm2_pre_g), row(v_norm2_pre_g)),
                    (row(norm2_post_g), row(m_norm2_post_g), row(v_norm2_post_g)),
                    (row(b_in), row(m_b_in), row(v_b_in)),
                    (ws2(w_spatial), ws2(m_w_spatial), ws2(v_w_spatial)),
                    (b_spatial, m_b_spatial, v_b_spatial),
                    (pg2(w_pool), pg2(m_w_pool), pg2(v_w_pool)),
                    (pg2(w_sgu_proj), pg2(m_w_sgu_proj), pg2(v_w_sgu_proj)),
                    (w_out, m_w_out, v_w_out)]
    small_out = _adamw_small(u_rows, list(summed_in[4:]) + [summed_in[1], summed_in[2], summed_out[0]], small_params)
    loss = small_out[0][0, 0]
    small_names = SMALL_ROWS[1:] + ("b_in", "w_spatial", "b_spatial", "w_pool", "w_sgu_proj", "w_out")
    small = {nm: small_out[1 + 4 * p:5 + 4 * p] for p, nm in enumerate(small_names)}

    shapes = {"norm1_pre_g": norm1_pre_g.shape, "w_in": w_in.shape, "b_in": b_in.shape, "w_pool": w_pool.shape,
              "pool_scale": pool_scale.shape, "sgu_ln_g": sgu_ln_g.shape, "sgu_ln_b": sgu_ln_b.shape,
              "w_spatial": w_spatial.shape, "b_spatial": b_spatial.shape, "w_sgu_proj": w_sgu_proj.shape,
              "w_out": w_out.shape, "norm1_post_g": norm1_post_g.shape, "norm2_pre_g": norm2_pre_g.shape,
              "w_ff1": w_ff1.shape, "w_ff2": w_ff2.shape, "norm2_post_g": norm2_post_g.shape}
    source = {"w_in": big["in"], "w_ff1": big["ff1"], "w_ff2": big["ff2"], **small}
    order = list(shapes)
    outs = [loss, dx.reshape(x.shape)]
    for kind in range(4):
        outs += [source[nm][kind].reshape(shapes[nm]) for nm in order]
    return tuple(outs)
```

```python
import math

import jax
import jax.numpy as jnp
from jax import lax
from jax.experimental import pallas as pl
from jax.experimental.pallas import tpu as pltpu

F32, BF16 = jnp.float32, jnp.bfloat16
MESH = pl.DeviceIdType.MESH

D_MODEL = 1024
D_IN = 5120
D_FF = 4096
N_DEV = 8
N_CHIPS = 4
WINDOWS = (2, 4, 8, 16)
N_GROUPS = 4
GROUP = 256
HALO = 16
SGU_BLOCK = 128
N_HEADS = 4
HEAD = 256
CHUNK = 64
EPS = 1e-6
IN_SHARD = D_IN // N_DEV
FF_SHARD = D_FF // N_DEV
OUT_SHARD = D_MODEL // N_DEV
PG_SHARD = GROUP // N_DEV

ADAM_LR, ADAM_B1, ADAM_B2, ADAM_EPS, ADAM_WD, ADAM_STEP = 0.001, 0.9, 0.999, 1e-08, 0.01, 10

VMEM_LIMIT = 56 * 1024 * 1024
TM = 256
TM_BWD = 256
TM_IN = 512
PROJ_COLS = 512
GELU_C0 = math.sqrt(2.0 / math.pi)
GELU_C1 = 0.044715


def _dot(a, b):
    return jnp.dot(a, b, preferred_element_type=F32)


def _dot_nt(a, b):
    return lax.dot_general(a, b, (((1,), (1,)), ((), ())), preferred_element_type=F32)


def _dot_tn(a, b):
    return lax.dot_general(a, b, (((0,), (0,)), ((), ())), preferred_element_type=F32)


def _gelu(x):
    t = jnp.tanh(GELU_C0 * (x + GELU_C1 * (x * x * x)))
    return 0.5 * x * (1.0 + t), t


def _gelu_grad(x, t):
    return 0.5 * (1.0 + t) + 0.5 * x * (1.0 - t * t) * (GELU_C0 * (1.0 + 3.0 * GELU_C1 * (x * x)))


def _sigmoid(x):
    return 1.0 / (1.0 + jnp.exp(-x))


def _mean(x):
    return jnp.mean(x, axis=-1, keepdims=True)


def _colsum(x):
    return jnp.sum(x, axis=0, keepdims=True)


def _const_spec(shape):
    nd = len(shape)
    return pl.BlockSpec(shape, lambda *_: (0,) * nd, pipeline_mode=pl.Buffered(1))


def _acc_spec(shape):
    nd = len(shape)
    return pl.BlockSpec(shape, lambda *_: (0,) * nd)


def _masked_ws(ws_ref):
    ri = lax.broadcasted_iota(jnp.int32, (SGU_BLOCK, SGU_BLOCK), 0) // CHUNK
    ci = lax.broadcasted_iota(jnp.int32, (SGU_BLOCK, SGU_BLOCK), 1) // CHUNK
    return [jnp.where(ri >= ci, ws_ref[h], 0.0).astype(BF16) for h in range(N_HEADS)]


def _pool_fwd(pbuf, tile_idx, tm):
    pos = lax.broadcasted_iota(jnp.int32, (tm, 1), 0) + tile_idx * tm + 1
    pooled = []
    for g, w in enumerate(WINDOWS):
        e = pbuf[:, g * GROUP:(g + 1) * GROUP]
        s, sh = e, 1
        while sh < w:
            s = s + pltpu.roll(s, sh, 0)
            sh *= 2
        inv = 1.0 / jnp.minimum(pos, w).astype(F32)
        pooled.append(s[HALO:] * inv - e[HALO:])
    return pooled


def _spatial_head(ws_h, vb, bsp_ref, h, nblk):
    return jnp.concatenate(
        [_dot(ws_h, vb[n * SGU_BLOCK:(n + 1) * SGU_BLOCK, h * HEAD:(h + 1) * HEAD]) + bsp_ref[:, h:h + 1]
         for n in range(nblk)], axis=0)


def _prenorm(after, x, g1pre):
    t_len = x.shape[0]
    tm = TM_IN

    def body(after_ref, x_ref, g_ref, xn_ref, xnt_ref):
        del after_ref
        xv = x_ref[...]
        xnb = (xv * lax.rsqrt(_mean(xv * xv) + EPS) * g_ref[...]).astype(BF16)
        xn_ref[...] = xnb
        xnt_ref[...] = xnb.T

    return pl.pallas_call(
        body, name="prenorm", grid=(t_len // tm,),
        in_specs=[_ANY, pl.BlockSpec((tm, D_MODEL), lambda i: (i, 0)), _const_spec((1, D_MODEL))],
        out_specs=[pl.BlockSpec((tm, D_MODEL), lambda i: (i, 0)), pl.BlockSpec((D_MODEL, tm), lambda i: (0, i))],
        out_shape=[jax.ShapeDtypeStruct((t_len, D_MODEL), BF16), jax.ShapeDtypeStruct((D_MODEL, t_len), BF16)],
        compiler_params=pltpu.CompilerParams(dimension_semantics=("arbitrary",)),
    )(after, x, g1pre)


def _fwd_mix(after, xn, x, win_g, b_in, wpool, pool_scale, lng, lnb, ws, bsp_t, wproj, wout, g1post):
    t_len = x.shape[0]
    tm = TM
    nt = t_len // tm

    def body(after_ref, xn_ref, xb_ref, win_ref, bin_ref, wpool_ref, ps_ref, lng_ref, lnb_ref, ws_ref, bsp_ref,
             wproj_ref, wout_ref, g1post_ref, z_ref, y_ref, h1_ref, u_ref, gpu_ref, xhat_ref, gpv_ref, sa_ref,
             sb_ref, zcur, znext, pbuf):
        del after_ref
        s = pl.program_id(0)

        @pl.when(s == 0)
        def _():
            znext[...] = jnp.zeros((tm, D_IN), F32)
            pbuf[...] = jnp.zeros((tm + HALO, D_MODEL), F32)

        zcur[...] = znext[...]
        xnb = xn_ref[...]

        def project(p):
            cols = slice(p * PROJ_COLS, (p + 1) * PROJ_COLS)
            zp = _dot(xnb, win_ref[:, cols]) + bin_ref[:, cols]
            if (p + 1) * PROJ_COLS <= D_MODEL:
                z_ref[:, cols] = zp
            znext[:, cols] = zp

        project(0)
        pbuf[0:HALO, :] = jnp.where(s <= 1, 0.0, pbuf[0:HALO, :])
        pbuf[HALO:, :] = zcur[:, 0:D_MODEL]
        pooled = _pool_fwd(pbuf, jnp.maximum(s - 1, 0), tm)
        pbuf[0:HALO, :] = pbuf[tm:tm + HALO, :]
        a = jnp.concatenate([_dot(pooled[g].astype(BF16), wpool_ref[g]) for g in range(N_GROUPS)], axis=1)
        a = a * ps_ref[...]
        project(1)
        zu = zcur[:, D_MODEL:2 * D_MODEL]
        u, tu = _gelu(zu)
        u_ref[...] = u.astype(BF16)
        gpu_ref[...] = _gelu_grad(zu, tu).astype(BF16)
        project(2)
        zv = zcur[:, 2 * D_MODEL:3 * D_MODEL]
        gv, tv = _gelu(zv)
        xc = gv - _mean(gv)
        rln = lax.rsqrt(_mean(xc * xc) + EPS)
        xhat = xc * rln
        xhat_ref[...] = xhat.astype(BF16)
        gpv_ref[...] = (_gelu_grad(zv, tv) * rln).astype(BF16)
        vb = (xhat * lng_ref[...] + lnb_ref[...]).astype(BF16)
        wsm = _masked_ws(ws_ref)
        bbr = []
        for h in range(N_HEADS):
            project(3 + h)
            sv = _spatial_head(wsm[h], vb, bsp_ref, h, tm // SGU_BLOCK)
            bbr.append(_dot((u[:, h * HEAD:(h + 1) * HEAD] * sv).astype(BF16), wproj_ref[h]))
        bbr = jnp.concatenate(bbr, axis=1)
        project(7)
        sa = _sigmoid(zcur[:, 3 * D_MODEL:4 * D_MODEL])
        sb = _sigmoid(zcur[:, 4 * D_MODEL:5 * D_MODEL])
        sa_ref[...] = sa.astype(BF16)
        sb_ref[...] = sb.astype(BF16)
        project(8)
        yv = _dot((sa * a + sb * bbr).astype(BF16), wout_ref[...])
        y_ref[...] = yv
        project(9)
        ry = lax.rsqrt(_mean(yv * yv) + EPS)
        h1_ref[...] = xb_ref[...] + yv * ry * g1post_ref[...]

    proj = lambda w: pl.BlockSpec((tm, w), lambda s: (jnp.minimum(s, nt - 1), 0))
    mix = lambda w: pl.BlockSpec((tm, w), lambda s: (jnp.maximum(s - 1, 0), 0))
    return pl.pallas_call(
        body, name="fwd_mix", grid=(nt + 1,),
        in_specs=[_ANY, proj(D_MODEL), mix(D_MODEL),
                  _const_spec((D_MODEL, D_IN)),
                  _const_spec((1, D_IN)), _const_spec((N_GROUPS, GROUP, GROUP)), _const_spec((1, D_MODEL)),
                  _const_spec((1, D_MODEL)), _const_spec((1, D_MODEL)),
                  _const_spec((N_HEADS, SGU_BLOCK, SGU_BLOCK)), _const_spec((SGU_BLOCK, N_HEADS)),
                  _const_spec((N_HEADS, HEAD, HEAD)), _const_spec((D_MODEL, D_MODEL)), _const_spec((1, D_MODEL))],
        out_specs=[proj(D_MODEL), mix(D_MODEL), mix(D_MODEL)] + [mix(D_MODEL)] * 6,
        out_shape=[jax.ShapeDtypeStruct((t_len, D_MODEL), F32),
                   jax.ShapeDtypeStruct((t_len, D_MODEL), F32), jax.ShapeDtypeStruct((t_len, D_MODEL), F32)]
        + [jax.ShapeDtypeStruct((t_len, D_MODEL), BF16)] * 6,
        scratch_shapes=[pltpu.VMEM((tm, D_IN), F32), pltpu.VMEM((tm, D_IN), F32),
                        pltpu.VMEM((tm + HALO, D_MODEL), F32)],
        compiler_params=pltpu.CompilerParams(dimension_semantics=("arbitrary",), vmem_limit_bytes=VMEM_LIMIT),
    )(after, xn, x, win_g, b_in, wpool, pool_scale, lng, lnb, ws, bsp_t, wproj, wout, g1post)


def _mlp(h1, target, g2pre, g2post, w1_g, w2):
    t_len = h1.shape[0]
    tm = TM
    nt = t_len // tm

    def body(h1_ref, tgt_ref, g2pre_ref, g2post_ref, w1_ref, w2_ref,
             hn_ref, f_ref, df1_ref, df2_ref, dh1_ref, dg2post_ref, dg2pre_ref, loss_ref, f1_scr):
        i = pl.program_id(0)

        @pl.when(i == 0)
        def _():
            dg2post_ref[...] = jnp.zeros_like(dg2post_ref)
            dg2pre_ref[...] = jnp.zeros_like(dg2pre_ref)
            loss_ref[...] = jnp.zeros_like(loss_ref)

        h = h1_ref[...]
        r2 = lax.rsqrt(_mean(h * h) + EPS)
        nh = h * r2
        hnb = (nh * g2pre_ref[...]).astype(BF16)
        hn_ref[...] = hnb.T
        for k in range(N_DEV):
            f1_scr[:, k * FF_SHARD:(k + 1) * FF_SHARD] = _dot(hnb, w1_ref[:, k * FF_SHARD:(k + 1) * FF_SHARD])
        r = jnp.maximum(f1_scr[...], 0.0)
        fb = (r * r).astype(BF16)
        f_ref[...] = fb.T
        f2 = _dot(fb, w2_ref[...])
        rf = lax.rsqrt(_mean(f2 * f2) + EPS)
        nf = f2 * rf
        diff = h + nf * g2post_ref[...] - tgt_ref[...]
        loss_ref[...] += (0.5 / D_MODEL) * jnp.sum(diff * diff)
        dout = diff * (1.0 / D_MODEL)
        dg2post_ref[...] += _colsum(dout * nf)
        dn = dout * g2post_ref[...]
        df2b = (rf * (dn - nf * _mean(dn * nf))).astype(BF16)
        df2_ref[...] = df2b
        df = _dot_nt(df2b, w2_ref[...])
        df1b = (df * (2.0 * jnp.maximum(f1_scr[...], 0.0))).astype(BF16)
        df1_ref[...] = df1b
        dhn = _dot_nt(df1b, w1_ref[...])
        dg2pre_ref[...] += _colsum(dhn * nh)
        dnh = dhn * g2pre_ref[...]
        dh1_ref[...] = dout + r2 * (dnh - nh * _mean(dnh * nh))

    tok = lambda w: pl.BlockSpec((tm, w), lambda i: (i, 0))
    return pl.pallas_call(
        body, name="mlp_fwd_bwd", grid=(nt,),
        in_specs=[tok(D_MODEL), tok(D_MODEL), _const_spec((1, D_MODEL)), _const_spec((1, D_MODEL)),
                  _const_spec((D_MODEL, D_FF)), _const_spec((D_FF, D_MODEL))],
        out_specs=[pl.BlockSpec((D_MODEL, tm), lambda i: (0, i)), pl.BlockSpec((D_FF, tm), lambda i: (0, i)),
                   tok(D_FF), tok(D_MODEL), tok(D_MODEL),
                   _acc_spec((1, D_MODEL)), _acc_spec((1, D_MODEL)), _acc_spec((1, 128))],
        out_shape=[jax.ShapeDtypeStruct((D_MODEL, t_len), BF16), jax.ShapeDtypeStruct((D_FF, t_len), BF16),
                   jax.ShapeDtypeStruct((t_len, D_FF), BF16), jax.ShapeDtypeStruct((t_len, D_MODEL), BF16),
                   jax.ShapeDtypeStruct((t_len, D_MODEL), F32), jax.ShapeDtypeStruct((1, D_MODEL), F32),
                   jax.ShapeDtypeStruct((1, D_MODEL), F32), jax.ShapeDtypeStruct((1, 128), F32)],
        scratch_shapes=[pltpu.VMEM((tm, D_FF), F32)],
        compiler_params=pltpu.CompilerParams(dimension_semantics=("arbitrary",), vmem_limit_bytes=VMEM_LIMIT),
    )(h1, target, g2pre, g2post, w1_g, w2)


def _bwd_mix(after, dh1, y, z, saved, wpool, pool_scale, lng, lnb, ws, bsp_t, wproj, wout, g1post):
    t_len = y.shape[0]
    tm = TM_BWD
    nt = t_len // tm
    nblk = tm // SGU_BLOCK

    def body(after_ref, dh1_ref, y_ref, z_ref, zh_ref, u_ref, gpu_ref, xhat_ref, gpv_ref, sa_ref, sb_ref,
             wpool_ref, ps_ref, lng_ref, lnb_ref, ws_ref,
             bsp_ref, wproj_ref, wout_ref, g1post_ref,
             dz_ref, mg_ref, dy_ref, dwpool_ref, dwproj_ref, dws_ref, dbsp_ref, dg1post_ref, dps_ref,
             dlng_ref, dlnb_ref, dbin_ref, pbuf, qbuf):
        del after_ref
        i = pl.program_id(0)
        ti = nt - 1 - i

        @pl.when(i == 0)
        def _():
            for ref in (dwpool_ref, dwproj_ref, dws_ref, dbsp_ref, dg1post_ref, dps_ref, dlng_ref, dlnb_ref,
                        dbin_ref):
                ref[...] = jnp.zeros_like(ref)
            qbuf[tm:tm + HALO, :] = jnp.zeros((HALO, D_MODEL), F32)

        pbuf[0:HALO, :] = jnp.where(ti > 0, zh_ref[...], 0.0)
        pbuf[HALO:, :] = z_ref[...]
        pooled = _pool_fwd(pbuf, ti, tm)
        pooled_b = [p.astype(BF16) for p in pooled]
        a_raw = jnp.concatenate([_dot(pooled_b[g], wpool_ref[g]) for g in range(N_GROUPS)], axis=1)
        wsm = _masked_ws(ws_ref)
        u = u_ref[...].astype(F32)
        xhat = xhat_ref[...].astype(F32)
        vb = (xhat * lng_ref[...] + lnb_ref[...]).astype(BF16)
        sv_heads = [_spatial_head(wsm[h], vb, bsp_ref, h, nblk) for h in range(N_HEADS)]
        gated_b = [(u[:, h * HEAD:(h + 1) * HEAD] * sv_heads[h]).astype(BF16) for h in range(N_HEADS)]
        bbr = jnp.concatenate([_dot(gated_b[h], wproj_ref[h]) for h in range(N_HEADS)], axis=1)
        sa = sa_ref[...].astype(F32)
        sb = sb_ref[...].astype(F32)
        a = a_raw * ps_ref[...]
        mg_ref[...] = (sa * a + sb * bbr).astype(BF16).T

        dh = dh1_ref[...]
        yv = y_ref[...]
        ry = lax.rsqrt(_mean(yv * yv) + EPS)
        ny = yv * ry
        dg1post_ref[...] += _colsum(dh * ny)
        dn = dh * g1post_ref[...]
        dyb = (ry * (dn - ny * _mean(dn * ny))).astype(BF16)
        dy_ref[...] = dyb
        dmg = _dot_nt(dyb, wout_ref[...])

        da = dmg * sa
        dbbr = dmg * sb
        dzga = dmg * a * sa * (1.0 - sa)
        dzgb = dmg * bbr * sb * (1.0 - sb)
        dz_ref[:, 3 * D_MODEL:4 * D_MODEL] = dzga.astype(BF16)
        dz_ref[:, 4 * D_MODEL:5 * D_MODEL] = dzgb.astype(BF16)
        dbin_ref[:, 3 * D_MODEL:4 * D_MODEL] += _colsum(dzga)
        dbin_ref[:, 4 * D_MODEL:5 * D_MODEL] += _colsum(dzgb)

        dps_ref[...] += _colsum(da * a_raw)
        da_raw_b = (da * ps_ref[...]).astype(BF16)
        pos = lax.broadcasted_iota(jnp.int32, (tm, 1), 0) + ti * tm + 1
        dpooled = []
        for g, w in enumerate(WINDOWS):
            cols = slice(g * GROUP, (g + 1) * GROUP)
            dwpool_ref[g] += _dot_tn(pooled_b[g], da_raw_b[:, cols])
            dp = _dot_nt(da_raw_b[:, cols], wpool_ref[g])
            dpooled.append(dp)
            qbuf[0:tm, cols] = dp * (1.0 / jnp.minimum(pos, w).astype(F32))
        n_ext = tm + HALO
        dzp = []
        for g, w in enumerate(WINDOWS):
            e = qbuf[:, g * GROUP:(g + 1) * GROUP]
            s, sh = e, 1
            while sh < w:
                s = s + pltpu.roll(s, n_ext - sh, 0)
                sh *= 2
            dzp.append(s[0:tm] - dpooled[g])
        qbuf[tm:tm + HALO, :] = qbuf[0:HALO, :]
        dzp = jnp.concatenate(dzp, axis=1)
        dz_ref[:, 0:D_MODEL] = dzp.astype(BF16)
        dbin_ref[:, 0:D_MODEL] += _colsum(dzp)

        dv_heads = []
        du_heads = []
        for h in range(N_HEADS):
            cols = slice(h * HEAD, (h + 1) * HEAD)
            dbbr_b = dbbr[:, cols].astype(BF16)
            dwproj_ref[h] += _dot_tn(gated_b[h], dbbr_b)
            dgated = _dot_nt(dbbr_b, wproj_ref[h])
            du_heads.append(dgated * sv_heads[h])
            dsv = dgated * u[:, cols]
            dsv_b = dsv.astype(BF16)
            rows = []
            for n in range(nblk):
                blk = slice(n * SGU_BLOCK, (n + 1) * SGU_BLOCK)
                rows.append(_dot_tn(wsm[h], dsv_b[blk]))
                dws_ref[h] += _dot_nt(dsv_b[blk], vb[blk, cols])
                dbsp_ref[:, h:h + 1] += jnp.sum(dsv[blk], axis=1, keepdims=True)
            dv_heads.append(jnp.concatenate(rows, axis=0))
        dzu = jnp.concatenate(du_heads, axis=1) * gpu_ref[...].astype(F32)
        dz_ref[:, D_MODEL:2 * D_MODEL] = dzu.astype(BF16)
        dbin_ref[:, D_MODEL:2 * D_MODEL] += _colsum(dzu)
        dv = jnp.concatenate(dv_heads, axis=1)
        dlng_ref[...] += _colsum(dv * xhat)
        dlnb_ref[...] += _colsum(dv)
        dxh = dv * lng_ref[...]
        dzv = (dxh - _mean(dxh) - xhat * _mean(dxh * xhat)) * gpv_ref[...].astype(F32)
        dz_ref[:, 2 * D_MODEL:3 * D_MODEL] = dzv.astype(BF16)
        dbin_ref[:, 2 * D_MODEL:3 * D_MODEL] += _colsum(dzv)

        @pl.when(i == nt - 1)
        def _():
            ri = lax.broadcasted_iota(jnp.int32, (SGU_BLOCK, SGU_BLOCK), 0) // CHUNK
            ci = lax.broadcasted_iota(jnp.int32, (SGU_BLOCK, SGU_BLOCK), 1) // CHUNK
            for h in range(N_HEADS):
                dws_ref[h] = jnp.where(ri >= ci, dws_ref[h], 0.0)

    tok = lambda w: pl.BlockSpec((tm, w), lambda i: (nt - 1 - i, 0))
    halo = pl.BlockSpec((HALO, D_MODEL), lambda i: (jnp.maximum((nt - 1 - i) * (tm // HALO) - 1, 0), 0))
    return pl.pallas_call(
        body, name="bwd_mix", grid=(nt,),
        in_specs=[_ANY, tok(D_MODEL), tok(D_MODEL), tok(D_MODEL), halo] + [tok(D_MODEL)] * 6
        + [_const_spec((N_GROUPS, GROUP, GROUP)),
                  _const_spec((1, D_MODEL)), _const_spec((1, D_MODEL)), _const_spec((1, D_MODEL)),
                  _const_spec((N_HEADS, SGU_BLOCK, SGU_BLOCK)), _const_spec((SGU_BLOCK, N_HEADS)),
                  _const_spec((N_HEADS, HEAD, HEAD)), _const_spec((D_MODEL, D_MODEL)), _const_spec((1, D_MODEL))],
        out_specs=[tok(D_IN), pl.BlockSpec((D_MODEL, tm), lambda i: (0, nt - 1 - i)), tok(D_MODEL),
                   _acc_spec((N_GROUPS, GROUP, GROUP)), _acc_spec((N_HEADS, HEAD, HEAD)),
                   _acc_spec((N_HEADS, SGU_BLOCK, SGU_BLOCK)), _acc_spec((SGU_BLOCK, N_HEADS)),
                   _acc_spec((1, D_MODEL)), _acc_spec((1, D_MODEL)), _acc_spec((1, D_MODEL)), _acc_spec((1, D_MODEL)),
                   _acc_spec((1, D_IN))],
        out_shape=[jax.ShapeDtypeStruct((t_len, D_IN), BF16),
                   jax.ShapeDtypeStruct((D_MODEL, t_len), BF16), jax.ShapeDtypeStruct((t_len, D_MODEL), BF16),
                   jax.ShapeDtypeStruct((N_GROUPS, GROUP, GROUP), F32), jax.ShapeDtypeStruct((N_HEADS, HEAD, HEAD), F32),
                   jax.ShapeDtypeStruct((N_HEADS, SGU_BLOCK, SGU_BLOCK), F32),
                   jax.ShapeDtypeStruct((SGU_BLOCK, N_HEADS), F32),
                   jax.ShapeDtypeStruct((1, D_MODEL), F32), jax.ShapeDtypeStruct((1, D_MODEL), F32),
                   jax.ShapeDtypeStruct((1, D_MODEL), F32), jax.ShapeDtypeStruct((1, D_MODEL), F32),
                   jax.ShapeDtypeStruct((1, D_IN), F32)],
        scratch_shapes=[pltpu.VMEM((tm + HALO, D_MODEL), F32), pltpu.VMEM((tm + HALO, D_MODEL), F32)],
        compiler_params=pltpu.CompilerParams(dimension_semantics=("arbitrary",), vmem_limit_bytes=VMEM_LIMIT),
    )(after, dh1, y, z, z, *saved, wpool, pool_scale, lng, lnb, ws, bsp_t, wproj, wout, g1post)


def _bwd_in(after, dz_b, x, dh1, g1pre, win_g):
    t_len = x.shape[0]
    tm = TM_IN
    nt = t_len // tm

    def body(after_ref, dz_ref, x_ref, dh1_ref, g1_ref, win_ref, dx_ref, dg1pre_ref):
        del after_ref

        @pl.when(pl.program_id(0) == 0)
        def _():
            dg1pre_ref[...] = jnp.zeros_like(dg1pre_ref)

        dxn = _dot_nt(dz_ref[...], win_ref[...])
        xv = x_ref[...]
        r1 = lax.rsqrt(_mean(xv * xv) + EPS)
        nx = xv * r1
        dg1pre_ref[...] += _colsum(dxn * nx)
        dnx = dxn * g1_ref[...]
        dx_ref[...] = r1 * (dnx - nx * _mean(dnx * nx)) + dh1_ref[...]

    tok = lambda w: pl.BlockSpec((tm, w), lambda i: (i, 0))
    return pl.pallas_call(
        body, name="bwd_in", grid=(nt,),
        in_specs=[_ANY, tok(D_IN), tok(D_MODEL), tok(D_MODEL), _const_spec((1, D_MODEL)),
                  _const_spec((D_MODEL, D_IN))],
        out_specs=[tok(D_MODEL), _acc_spec((1, D_MODEL))],
        out_shape=[jax.ShapeDtypeStruct((t_len, D_MODEL), F32), jax.ShapeDtypeStruct((1, D_MODEL), F32)],
        compiler_params=pltpu.CompilerParams(dimension_semantics=("arbitrary",), vmem_limit_bytes=VMEM_LIMIT),
    )(after, dz_b, x, dh1, g1pre, win_g)


def _owner_of_slot(s):
    return 4 * ((s // 2) % 2) + 2 * (s % 2) + s // 4


def _wgrad(name, a_t, b, shard, a_sliced):
    t_len = b.shape[0]
    am = shard if a_sliced else a_t.shape[0]
    bn = b.shape[1] if a_sliced else shard

    def body(a_ref, b_ref, o_ref):
        o_ref[...] = _dot(a_ref[...], b_ref[...]).astype(BF16)

    if a_sliced:
        in_specs = [pl.BlockSpec((am, t_len), lambda s: (_owner_of_slot(s), 0)),
                    pl.BlockSpec((t_len, bn), lambda s: (0, 0), pipeline_mode=pl.Buffered(1))]
    else:
        in_specs = [pl.BlockSpec((am, t_len), lambda s: (0, 0), pipeline_mode=pl.Buffered(1)),
                    pl.BlockSpec((t_len, bn), lambda s: (0, _owner_of_slot(s)))]
    return pl.pallas_call(
        body, name=name, grid=(N_DEV,), in_specs=in_specs,
        out_specs=pl.BlockSpec((None, am, bn), lambda s: (s, 0, 0)),
        out_shape=jax.ShapeDtypeStruct((N_DEV, am, bn), BF16),
        compiler_params=pltpu.CompilerParams(dimension_semantics=("arbitrary",), vmem_limit_bytes=VMEM_LIMIT),
    )(a_t, b)


def _wgrad_in(xn_t, dz_b):
    t_len = dz_b.shape[0]

    def body(a_ref, b_ref, o_ref):
        res = _dot(a_ref[...], b_ref[...])
        o_ref[0] = res[:, 0:IN_SHARD].astype(BF16)
        o_ref[1] = res[:, IN_SHARD:2 * IN_SHARD].astype(BF16)

    out = pl.pallas_call(
        body, name="wgrad_in", grid=(N_CHIPS,),
        in_specs=[pl.BlockSpec((D_MODEL, t_len), lambda q: (0, 0), pipeline_mode=pl.Buffered(1)),
                  pl.BlockSpec((t_len, 2 * IN_SHARD), lambda q: (0, q))],
        out_specs=pl.BlockSpec((2, None, D_MODEL, IN_SHARD), lambda q: (0, q, 0, 0)),
        out_shape=jax.ShapeDtypeStruct((2, N_CHIPS, D_MODEL, IN_SHARD), BF16),
        compiler_params=pltpu.CompilerParams(dimension_semantics=("arbitrary",), vmem_limit_bytes=VMEM_LIMIT),
    )(xn_t, dz_b)
    return out.reshape(N_DEV, D_MODEL, IN_SHARD)


def _wgrad_out(mg_t, dy_b):
    t_len = dy_b.shape[0]

    def body(a_ref, b_ref, o_ref):
        res = _dot(a_ref[...], b_ref[...])
        o_ref[0] = res[0:OUT_SHARD].astype(BF16)
        o_ref[1] = res[OUT_SHARD:2 * OUT_SHARD].astype(BF16)

    out = pl.pallas_call(
        body, name="wgrad_out", grid=(N_CHIPS,),
        in_specs=[pl.BlockSpec((2 * OUT_SHARD, t_len), lambda q: (q, 0)),
                  pl.BlockSpec((t_len, D_MODEL), lambda q: (0, 0), pipeline_mode=pl.Buffered(1))],
        out_specs=pl.BlockSpec((2, None, OUT_SHARD, D_MODEL), lambda q: (0, q, 0, 0)),
        out_shape=jax.ShapeDtypeStruct((2, N_CHIPS, OUT_SHARD, D_MODEL), BF16),
        compiler_params=pltpu.CompilerParams(dimension_semantics=("arbitrary",), vmem_limit_bytes=VMEM_LIMIT),
    )(mg_t, dy_b)
    return out.reshape(N_DEV, OUT_SHARD, D_MODEL)


def _coords():
    return lax.axis_index("x"), lax.axis_index("y"), lax.axis_index("c")


_ANY = pl.BlockSpec(memory_space=pl.ANY)


def _pair_exchange(name, bigs, smalls):
    nb, n = len(bigs), len(bigs) + len(smalls)

    def body(*refs):
        src, dst = refs[:n], refs[n:2 * n]
        send_sems, recv_sems = refs[2 * n:]
        x, y, c = _coords()
        copies = []
        for j in range(n):
            s = src[j].at[pl.ds(4 * (1 - c), 4)] if j < nb else src[j]
            cp = pltpu.make_async_remote_copy(src_ref=s, dst_ref=dst[j], send_sem=send_sems.at[j],
                                              recv_sem=recv_sems.at[j], device_id=(x, y, 1 - c), device_id_type=MESH)
            cp.start()
            copies.append(cp)
        for cp in copies:
            cp.wait()

    return pl.pallas_call(
        body, name=name,
        in_specs=[_ANY] * n, out_specs=[_ANY] * n,
        out_shape=[jax.ShapeDtypeStruct((4,) + b.shape[1:], b.dtype) for b in bigs]
        + [jax.ShapeDtypeStruct(s.shape, s.dtype) for s in smalls],
        scratch_shapes=[pltpu.SemaphoreType.DMA((n,)), pltpu.SemaphoreType.DMA((n,))],
    )(*bigs, *smalls)


_HBM = pl.BlockSpec(memory_space=pltpu.HBM)
_SEM = pl.BlockSpec(memory_space=pltpu.SEMAPHORE)
_VMEM = pl.BlockSpec(memory_space=pltpu.VMEM)
_EFFECT = pltpu.SideEffectType.DATAFLOW_SIDE_EFFECTING
_TOKEN = jax.ShapeDtypeStruct((8, 128), F32)


def _in_hbm(a):
    return pltpu.with_memory_space_constraint(a, pltpu.HBM)


def _split_call(name, body, n_sems_out, arrays, sems_in=(), after=None):
    na, ns = len(arrays), len(sems_in)
    has_after = after is not None

    def kernel_body(*refs):
        arr = refs[:na]
        s_in = refs[na:na + ns]
        outs = refs[na + ns + has_after:]
        body(arr, s_in, outs[:n_sems_out])
        outs[-1][...] = jnp.zeros((8, 128), F32)

    out_shape = ([pltpu.SemaphoreType.DMA(())] * n_sems_out + [pltpu.HBM(a.shape, a.dtype) for a in arrays] + [_TOKEN])
    res = pl.pallas_call(
        kernel_body, name=name, out_shape=out_shape,
        in_specs=[_HBM] * na + [_SEM] * ns + [_ANY] * has_after,
        out_specs=[_SEM] * n_sems_out + [_HBM] * na + [_VMEM],
        input_output_aliases={i: n_sems_out + i for i in range(na)},
        compiler_params=pltpu.CompilerParams(has_side_effects=_EFFECT),
    )(*[_in_hbm(a) for a in arrays], *sems_in, *([after] if has_after else []))
    return list(res[:n_sems_out]), list(res[n_sems_out:n_sems_out + na]), res[-1]


def _wait_bytes_of(ref, send_sem, recv_sem, peer, send=True, recv=True):
    cp = pltpu.make_async_remote_copy(src_ref=ref, dst_ref=ref, send_sem=send_sem, recv_sem=recv_sem,
                                      device_id=peer, device_id_type=MESH)
    if send:
        cp.wait_send()
    if recv:
        cp.wait_recv()


def _gather_behind(tag, shards, after, work, by_columns=()):
    n = len(shards)
    lands = [lax.empty((s.shape[0], N_DEV * s.shape[1]) if j in by_columns else (N_DEV,) + s.shape, s.dtype)
             for j, s in enumerate(shards)]

    def slots(arr, j, first, count=1):
        if j in by_columns:
            cols = shards[j].shape[1]
            return arr[n + j].at[:, pl.ds(pl.multiple_of(first * cols, 128), count * cols)]
        return arr[n + j].at[first] if count == 1 else arr[n + j].at[pl.ds(first, count)]

    def own_slot(arr, j, sem):
        x, y, c = _coords()
        return pltpu.make_async_copy(arr[j], slots(arr, j, 4 * x + 2 * y + c), sem)

    def start(arr, _, sems):
        x, y, c = _coords()
        me = 4 * x + 2 * y + c
        for j in range(n):
            for chip in [(1 - x, y), (x, 1 - y), (1 - x, 1 - y)]:
                pltpu.make_async_remote_copy(src_ref=arr[j], dst_ref=slots(arr, j, me), send_sem=sems[j],
                                             recv_sem=sems[n + j], device_id=(*chip, c), device_id_type=MESH).start()
        for j in range(n):
            pltpu.make_async_remote_copy(src_ref=arr[j], dst_ref=slots(arr, j, me), send_sem=sems[2 * n + j],
                                         recv_sem=sems[3 * n + j], device_id=(x, y, 1 - c),
                                         device_id_type=MESH).start()
            own_slot(arr, j, sems[4 * n + j]).start()

    def middle(arr, s_in, sems):
        x, y, c = _coords()
        sibling = (x, y, 1 - c)
        for j in range(n):
            _wait_bytes_of(slots(arr, j, 0, 3), s_in[j], s_in[n + j], sibling)
            for chip in [(1 - x, y), (x, 1 - y), (1 - x, 1 - y)]:
                slot = slots(arr, j, 4 * chip[0] + 2 * chip[1] + c)
                pltpu.make_async_remote_copy(src_ref=slot, dst_ref=slot, send_sem=sems[j], recv_sem=sems[n + j],
                                             device_id=sibling, device_id_type=MESH).start()

    def finish(arr, s_in, _):
        x, y, c = _coords()
        sibling = (x, y, 1 - c)
        for j in range(n):
            _wait_bytes_of(slots(arr, j, 0, 1), s_in[j], s_in[n + j], sibling)
            own_slot(arr, j, s_in[2 * n + j]).wait()
            _wait_bytes_of(slots(arr, j, 0, 3), s_in[3 * n + j], s_in[4 * n + j], sibling)

    sems, arrays, token = _split_call("gather_%s_start" % tag, start, 5 * n, list(shards) + lands, after=after)
    result = work(token)
    fwd_sems, arrays, token = _split_call("gather_%s_middle" % tag, middle, 2 * n, arrays, sems_in=sems[:2 * n],
                                          after=result[0])
    _, arrays, _ = _split_call("gather_%s_finish" % tag, finish, 0, arrays, sems_in=sems[2 * n:] + fwd_sems,
                               after=token)
    return arrays[n:], result


def _chip_exchange_behind(tag, bigs, smalls, work):
    nb, n = len(bigs), len(bigs) + len(smalls)
    lands = [lax.empty(s.shape, s.dtype) for s in bigs] + [lax.empty((N_CHIPS,) + s.shape, s.dtype) for s in smalls]

    def own_slot(arr, j, sem):
        x, y, _ = _coords()
        q_me = 2 * x + y
        return pltpu.make_async_copy(arr[j].at[q_me] if j < nb else arr[j], arr[n + j].at[q_me], sem)

    def start(arr, _, sems):
        x, y, c = _coords()
        q_me = 2 * x + y
        for j in range(n):
            for peer in [(1 - x, y, c), (x, 1 - y, c), (1 - x, 1 - y, c)]:
                piece = arr[j].at[2 * peer[0] + peer[1]] if j < nb else arr[j]
                pltpu.make_async_remote_copy(src_ref=piece, dst_ref=arr[n + j].at[q_me],
                                             send_sem=sems[j], recv_sem=sems[n + j], device_id=peer,
                                             device_id_type=MESH).start()
            own_slot(arr, j, sems[2 * n + j]).start()

    def finish(arr, s_in, _):
        x, y, c = _coords()
        for j in range(n):
            _wait_bytes_of(arr[n + j].at[pl.ds(0, 3)], s_in[j], s_in[n + j], (x, y, 1 - c))
            own_slot(arr, j, s_in[2 * n + j]).wait()

    sems, arrays, token = _split_call("chip_exchange_%s_start" % tag, start, 3 * n, list(bigs) + list(smalls) + lands)
    result = work(token)
    _, arrays, _ = _split_call("chip_exchange_%s_finish" % tag, finish, 0, arrays, sems_in=sems, after=result[0])
    return arrays[n:], result


def _pair_sum_big(name, c_arr, mine, theirs):
    _, rows, cols = theirs.shape
    tr = rows

    def body(c_ref, a_ref, b_ref, o_ref):
        del c_ref
        o_ref[...] = (a_ref[...].astype(F32) + b_ref[...].astype(F32)).astype(BF16)

    return pl.pallas_call(
        body, name=name,
        grid_spec=pltpu.PrefetchScalarGridSpec(
            num_scalar_prefetch=1, grid=(N_CHIPS, rows // tr),
            in_specs=[pl.BlockSpec((None, tr, cols), lambda q, r, c_ref: (4 * c_ref[0] + q, r, 0)),
                      pl.BlockSpec((None, tr, cols), lambda q, r, c_ref: (q, r, 0))],
            out_specs=pl.BlockSpec((None, tr, cols), lambda q, r, c_ref: (q, r, 0))),
        out_shape=jax.ShapeDtypeStruct(theirs.shape, BF16),
        compiler_params=pltpu.CompilerParams(dimension_semantics=("arbitrary", "arbitrary")),
    )(c_arr, mine, theirs)


def _pair_sum_small(name, c_arr, slotted, slotted_theirs, mine, theirs):
    ns, n = len(slotted), len(mine)

    def body(c_ref, *refs):
        first = 4 * c_ref[0]
        outs = refs[2 * (ns + n):]
        for j in range(ns):
            own = refs[j][pl.ds(first, N_CHIPS)].astype(F32)
            outs[j][...] = (own + refs[ns + j][...].astype(F32)).astype(BF16)
        for j in range(n):
            outs[ns + j][...] = refs[2 * ns + j][...] + refs[2 * ns + n + j][...]

    return pl.pallas_call(
        body, name=name,
        in_specs=[pl.BlockSpec(memory_space=pltpu.SMEM)] + [_VMEM] * (2 * (ns + n)),
        out_shape=[jax.ShapeDtypeStruct(t.shape, BF16) for t in slotted_theirs]
        + [jax.ShapeDtypeStruct(m.shape, m.dtype) for m in mine],
    )(c_arr, *slotted, *slotted_theirs, *mine, *theirs)


def _adamw_math(w, g, m, v):
    m = ADAM_B1 * m + (1.0 - ADAM_B1) * g
    v = ADAM_B2 * v + (1.0 - ADAM_B2) * (g * g)
    m_hat = m / (1.0 - ADAM_B1 ** ADAM_STEP)
    v_hat = v / (1.0 - ADAM_B2 ** ADAM_STEP)
    delta = -ADAM_LR * (m_hat / (jnp.sqrt(v_hat) + ADAM_EPS) + ADAM_WD * w)
    return delta, m, v


def _adamw_big(name, after, chip_sums, w, m, v):
    rows, cols = w.shape
    tr = min(rows, 512)

    def body(after_ref, t_ref, w_ref, m_ref, v_ref, g_out, d_out, m_out, v_out):
        del after_ref
        g = t_ref[0].astype(F32)
        for q in range(1, N_CHIPS):
            g = g + t_ref[q].astype(F32)
        d, mn, vn = _adamw_math(w_ref[...], g, m_ref[...], v_ref[...])
        g_out[...] = g
        d_out[...] = d
        m_out[...] = mn
        v_out[...] = vn

    blk = pl.BlockSpec((tr, cols), lambda r: (r, 0))
    return pl.pallas_call(
        body, name=name, grid=(rows // tr,),
        in_specs=[_ANY, pl.BlockSpec((N_CHIPS, tr, cols), lambda r: (0, r, 0)), blk, blk, blk],
        out_specs=[blk] * 4,
        out_shape=[jax.ShapeDtypeStruct((rows, cols), F32)] * 4,
        compiler_params=pltpu.CompilerParams(dimension_semantics=("arbitrary",), vmem_limit_bytes=VMEM_LIMIT),
    )(after, chip_sums, w, m, v)


SMALL_ROWS = ("loss", "norm1_pre_g", "pool_scale", "sgu_ln_g", "sgu_ln_b", "norm1_post_g", "norm2_pre_g",
              "norm2_post_g")


N_ROW_PARAMS = len(SMALL_ROWS) - 1


def _adamw_small(u_rows, u_others, params):
    n, n_oth = len(params), len(u_others)

    def body(*refs):
        urow_ref, others = refs[0], refs[1:1 + n_oth]
        wmv = refs[1 + n_oth:1 + n_oth + 3 * n]
        loss_ref = refs[1 + n_oth + 3 * n]
        outs = refs[2 + n_oth + 3 * n:]

        def total(ref, idx):
            g = ref[(0,) + idx].astype(F32)
            for q in range(1, N_CHIPS):
                g = g + ref[(q,) + idx].astype(F32)
            return g

        loss_ref[...] = total(urow_ref, (slice(0, 1), slice(None)))
        for p in range(n):
            if p < N_ROW_PARAMS:
                g = total(urow_ref, (slice(p + 1, p + 2), slice(None)))
            else:
                g = total(others[p - N_ROW_PARAMS], (slice(None), slice(None)))
            d, mn, vn = _adamw_math(wmv[3 * p][...], g, wmv[3 * p + 1][...], wmv[3 * p + 2][...])
            outs[4 * p][...] = g
            outs[4 * p + 1][...] = d
            outs[4 * p + 2][...] = mn
            outs[4 * p + 3][...] = vn

    flat = [a for p in params for a in p]
    out_shape = [jax.ShapeDtypeStruct((1, D_MODEL), F32)]
    for w, _, _ in params:
        out_shape += [jax.ShapeDtypeStruct(w.shape, F32)] * 4
    return pl.pallas_call(body, name="adamw_small", out_shape=out_shape)(u_rows, *u_others, *flat)


def _slots_of_rows(full):
    owners = [_owner_of_slot(s) for s in range(N_DEV)]
    parts = full.reshape(N_GROUPS, N_DEV, PG_SHARD, GROUP)
    return jnp.stack([parts[:, o] for o in owners]).reshape(N_DEV, N_GROUPS * PG_SHARD, GROUP).astype(BF16)


def kernel(x, norm1_pre_g, w_in, b_in, w_pool, pool_scale, sgu_ln_g, sgu_ln_b, w_spatial, b_spatial, w_sgu_proj, w_out, norm1_post_g, norm2_pre_g, w_ff1, w_ff2, norm2_post_g, loss_target, m_norm1_pre_g, m_w_in, m_b_in, m_w_pool, m_pool_scale, m_sgu_ln_g, m_sgu_ln_b, m_w_spatial, m_b_spatial, m_w_sgu_proj, m_w_out, m_norm1_post_g, m_norm2_pre_g, m_w_ff1, m_w_ff2, m_norm2_post_g, v_norm1_pre_g, v_w_in, v_b_in, v_w_pool, v_pool_scale, v_sgu_ln_g, v_sgu_ln_b, v_w_spatial, v_b_spatial, v_w_sgu_proj, v_w_out, v_norm1_post_g, v_norm2_pre_g, v_w_ff1, v_w_ff2, v_norm2_post_g):
    t_len = x.shape[1]
    row = lambda a: a.reshape(1, -1)
    x2 = x.reshape(t_len, D_MODEL)
    tgt2 = loss_target.reshape(t_len, D_MODEL)
    pg2 = lambda a: a.reshape(N_GROUPS * PG_SHARD, GROUP)

    def prework(token):
        return _prenorm(token, x2, row(norm1_pre_g))

    (win_f, g_pool, g_proj, g_out), (xn, xn_b) = _gather_behind(
        "mix", [w_in.astype(BF16), pg2(w_pool).astype(BF16), pg2(w_sgu_proj).astype(BF16), w_out.astype(BF16)],
        None, prework, by_columns=(0,))
    regroup = lambda g: g.reshape(N_DEV, N_GROUPS, PG_SHARD, GROUP).transpose(1, 0, 2, 3).reshape(N_GROUPS, GROUP, GROUP)
    wpool_f, wproj_f = regroup(g_pool), regroup(g_proj)
    wout_f = g_out.reshape(D_MODEL, D_MODEL)
    bsp_t = b_spatial.T

    def forward(token):
        z, y, h1, *saved = _fwd_mix(token, xn, x2, win_f, row(b_in), wpool_f, row(pool_scale),
                                    row(sgu_ln_g), row(sgu_ln_b), w_spatial, bsp_t, wproj_f, wout_f,
                                    row(norm1_post_g))
        return h1, z, y, saved

    (g_ff1, g_ff2), (h1, z, y, saved) = _gather_behind("ff", [w_ff1.astype(BF16), w_ff2.astype(BF16)], win_f,
                                                       forward, by_columns=(0,))
    w2_f = g_ff2.reshape(D_FF, D_MODEL)
    hn_b, f_b, df1_b, df2_b, dh1, dg2post, dg2pre, loss_p = _mlp(h1, tgt2, row(norm2_pre_g), row(norm2_post_g),
                                                               g_ff1, w2_f)
    p_ff2 = _wgrad("wgrad_ff2", f_b, df2_b, FF_SHARD, True)
    p_ff1 = _wgrad("wgrad_ff1", hn_b, df1_b, FF_SHARD, False)
    c_arr = lax.axis_index("c").astype(jnp.int32).reshape(1)
    ff_parts = [p_ff1, p_ff2]
    got_ff = _pair_exchange("pair_exchange_ff", ff_parts, [])
    chip_ff = [_pair_sum_big("pair_sum_" + nm, c_arr, b, r) for nm, b, r in zip(("ff1", "ff2"), ff_parts, got_ff)]

    def backward_mix(token):
        return _bwd_mix(token, dh1, y, z, saved, wpool_f, row(pool_scale), row(sgu_ln_g), row(sgu_ln_b), w_spatial,
                        bsp_t, wproj_f, wout_f, row(norm1_post_g))

    summed_ff, (dz_b, mg_b, dy_b, dwpool, dwproj, dws, dbsp_t, dg1post, dps, dlng, dlnb,
                dbin) = _chip_exchange_behind("ff", chip_ff, [], backward_mix)

    p_in = _wgrad_in(xn_b, dz_b)
    bigs = [p_in, _slots_of_rows(dwpool), _slots_of_rows(dwproj)]
    rows = jnp.concatenate([jnp.broadcast_to(loss_p[:, 0:1], (1, D_MODEL)), dps, dlng, dlnb, dg1post, dg2pre, dg2post],
                           axis=0)
    smalls = [rows, dbin, dws.reshape(N_HEADS * SGU_BLOCK, SGU_BLOCK), dbsp_t.T]
    got = _pair_exchange("pair_exchange_in", bigs, smalls)
    chip_in = _pair_sum_big("pair_sum_in", c_arr, p_in, got[0])
    sums = _pair_sum_small("pair_sum_small", c_arr, bigs[1:], got[1:3], smalls, got[3:])
    chip_bigs, chip_smalls = [chip_in] + list(sums[:2]), sums[2:]

    def backward_rest(token):
        dx, dg1pre = _bwd_in(token, dz_b, x2, dh1, row(norm1_pre_g), win_f)
        p_out = _wgrad_out(mg_b, dy_b)
        got_out = _pair_exchange("pair_exchange_out", [p_out], [dg1pre])
        chip_out, chip_g1pre = _pair_sum_small("pair_sum_out", c_arr, [p_out], got_out[:1], [dg1pre], got_out[1:])
        upd_ff = [_adamw_big("adamw_" + nm, p_out, t, *p) for nm, t, p in
                  (("ff1", summed_ff[0], (w_ff1, m_w_ff1, v_w_ff1)), ("ff2", summed_ff[1], (w_ff2, m_w_ff2, v_w_ff2)))]
        return upd_ff[1][0], dx, chip_out, chip_g1pre, upd_ff

    summed_in, (_, dx, chip_out, chip_g1pre, upd_ff) = _chip_exchange_behind("in", chip_bigs, chip_smalls,
                                                                             backward_rest)

    def update_in(token):
        res = _adamw_big("adamw_in", token, summed_in[0], w_in, m_w_in, v_w_in)
        return res[0], res

    summed_out, (_, upd_in) = _chip_exchange_behind("out", [chip_out], [chip_g1pre], update_in)
    big = {"in": upd_in, "ff1": upd_ff[0], "ff2": upd_ff[1]}
    u_rows = jnp.concatenate([summed_in[3][:, 0:1], summed_out[1], summed_in[3][:, 1:]], axis=1)
    ws2 = lambda a: a.reshape(N_HEADS * SGU_BLOCK, SGU_BLOCK)
    small_params = [(row(norm1_pre_g), row(m_norm1_pre_g), row(v_norm1_pre_g)),
                    (row(pool_scale), row(m_pool_scale), row(v_pool_scale)),
                    (row(sgu_ln_g), row(m_sgu_ln_g), row(v_sgu_ln_g)),
                    (row(sgu_ln_b), row(m_sgu_ln_b), row(v_sgu_ln_b)),
                    (row(norm1_post_g), row(m_norm1_post_g), row(v_norm1_post_g)),
                    (row(norm2_pre_g), row(m_norm2_pre_g), row(v_norm2_pre_g)),
                    (row(norm2_post_g), row(m_norm2_post_g), row(v_norm2_post_g)),
                    (row(b_in), row(m_b_in), row(v_b_in)),
                    (ws2(w_spatial), ws2(m_w_spatial), ws2(v_w_spatial)),
                    (b_spatial, m_b_spatial, v_b_spatial),
                    (pg2(w_pool), pg2(m_w_pool), pg2(v_w_pool)),
                    (pg2(w_sgu_proj), pg2(m_w_sgu_proj), pg2(v_w_sgu_proj)),
                    (w_out, m_w_out, v_w_out)]
    small_out = _adamw_small(u_rows, list(summed_in[4:]) + [summed_in[1], summed_in[2], summed_out[0]], small_params)
    loss = small_out[0][0, 0]
    small_names = SMALL_ROWS[1:] + ("b_in", "w_spatial", "b_spatial", "w_pool", "w_sgu_proj", "w_out")
    small = {nm: small_out[1 + 4 * p:5 + 4 * p] for p, nm in enumerate(small_names)}

    shapes = {"norm1_pre_g": norm1_pre_g.shape, "w_in": w_in.shape, "b_in": b_in.shape, "w_pool": w_pool.shape,
              "pool_scale": pool_scale.shape, "sgu_ln_g": sgu_ln_g.shape, "sgu_ln_b": sgu_ln_b.shape,
              "w_spatial": w_spatial.shape, "b_spatial": b_spatial.shape, "w_sgu_proj": w_sgu_proj.shape,
              "w_out": w_out.shape, "norm1_post_g": norm1_post_g.shape, "norm2_pre_g": norm2_pre_g.shape,
              "w_ff1": w_ff1.shape, "w_ff2": w_ff2.shape, "norm2_post_g": norm2_post_g.shape}
    source = {"w_in": big["in"], "w_ff1": big["ff1"], "w_ff2": big["ff2"], **small}
    order = list(shapes)
    outs = [loss, dx.reshape(x.shape)]
    for kind in range(4):
        outs += [source[nm][kind].reshape(shapes[nm]) for nm in order]
    return tuple(outs)
```

```python
import math

import jax
import jax.numpy as jnp
from jax import lax
from jax.experimental import pallas as pl
from jax.experimental.pallas import tpu as pltpu

F32, BF16 = jnp.float32, jnp.bfloat16
MESH = pl.DeviceIdType.MESH

D_MODEL = 1024
D_IN = 5120
D_FF = 4096
N_DEV = 8
N_CHIPS = 4
WINDOWS = (2, 4, 8, 16)
N_GROUPS = 4
GROUP = 256
HALO = 16
SGU_BLOCK = 128
N_HEADS = 4
HEAD = 256
CHUNK = 64
EPS = 1e-6
IN_SHARD = D_IN // N_DEV
FF_SHARD = D_FF // N_DEV
OUT_SHARD = D_MODEL // N_DEV
PG_SHARD = GROUP // N_DEV

ADAM_LR, ADAM_B1, ADAM_B2, ADAM_EPS, ADAM_WD, ADAM_STEP = 0.001, 0.9, 0.999, 1e-08, 0.01, 10

VMEM_LIMIT = 56 * 1024 * 1024
TM = 256
TM_BWD = 256
TM_IN = 512
PROJ_COLS = 512
GELU_C0 = math.sqrt(2.0 / math.pi)
GELU_C1 = 0.044715


def _dot(a, b):
    return jnp.dot(a, b, preferred_element_type=F32)


def _dot_nt(a, b):
    return lax.dot_general(a, b, (((1,), (1,)), ((), ())), preferred_element_type=F32)


def _dot_tn(a, b):
    return lax.dot_general(a, b, (((0,), (0,)), ((), ())), preferred_element_type=F32)


def _gelu(x):
    t = jnp.tanh(GELU_C0 * (x + GELU_C1 * (x * x * x)))
    return 0.5 * x * (1.0 + t), t


def _gelu_grad(x, t):
    return 0.5 * (1.0 + t) + 0.5 * x * (1.0 - t * t) * (GELU_C0 * (1.0 + 3.0 * GELU_C1 * (x * x)))


def _sigmoid(x):
    return 1.0 / (1.0 + jnp.exp(-x))


def _mean(x):
    return jnp.mean(x, axis=-1, keepdims=True)


def _colsum(x):
    return jnp.sum(x, axis=0, keepdims=True)


def _const_spec(shape):
    nd = len(shape)
    return pl.BlockSpec(shape, lambda *_: (0,) * nd, pipeline_mode=pl.Buffered(1))


def _acc_spec(shape):
    nd = len(shape)
    return pl.BlockSpec(shape, lambda *_: (0,) * nd)


def _masked_ws(ws_ref):
    ri = lax.broadcasted_iota(jnp.int32, (SGU_BLOCK, SGU_BLOCK), 0) // CHUNK
    ci = lax.broadcasted_iota(jnp.int32, (SGU_BLOCK, SGU_BLOCK), 1) // CHUNK
    return [jnp.where(ri >= ci, ws_ref[h], 0.0).astype(BF16) for h in range(N_HEADS)]


def _pool_fwd(pbuf, tile_idx, tm):
    pos = lax.broadcasted_iota(jnp.int32, (tm, 1), 0) + tile_idx * tm + 1
    pooled = []
    for g, w in enumerate(WINDOWS):
        e = pbuf[:, g * GROUP:(g + 1) * GROUP]
        s, sh = e, 1
        while sh < w:
            s = s + pltpu.roll(s, sh, 0)
            sh *= 2
        inv = 1.0 / jnp.minimum(pos, w).astype(F32)
        pooled.append(s[HALO:] * inv - e[HALO:])
    return pooled


def _spatial_head(ws_h, vb, bsp_ref, h, nblk):
    return jnp.concatenate(
        [_dot(ws_h, vb[n * SGU_BLOCK:(n + 1) * SGU_BLOCK, h * HEAD:(h + 1) * HEAD]) + bsp_ref[:, h:h + 1]
         for n in range(nblk)], axis=0)


def _prenorm(after, x, g1pre):
    t_len = x.shape[0]
    tm = TM_IN

    def body(after_ref, x_ref, g_ref, xn_ref, xnt_ref):
        del after_ref
        xv = x_ref[...]
        xnb = (xv * lax.rsqrt(_mean(xv * xv) + EPS) * g_ref[...]).astype(BF16)
        xn_ref[...] = xnb
        xnt_ref[...] = xnb.T

    return pl.pallas_call(
        body, name="prenorm", grid=(t_len // tm,),
        in_specs=[_ANY, pl.BlockSpec((tm, D_MODEL), lambda i: (i, 0)), _const_spec((1, D_MODEL))],
        out_specs=[pl.BlockSpec((tm, D_MODEL), lambda i: (i, 0)), pl.BlockSpec((D_MODEL, tm), lambda i: (0, i))],
        out_shape=[jax.ShapeDtypeStruct((t_len, D_MODEL), BF16), jax.ShapeDtypeStruct((D_MODEL, t_len), BF16)],
        compiler_params=pltpu.CompilerParams(dimension_semantics=("arbitrary",)),
    )(after, x, g1pre)


def _fwd_mix(after, xn, x, win_g, b_in, wpool, pool_scale, lng, lnb, ws, bsp_t, wproj, wout, g1post):
    t_len = x.shape[0]
    tm = TM
    nt = t_len // tm

    def body(after_ref, xn_ref, xb_ref, win_ref, bin_ref, wpool_ref, ps_ref, lng_ref, lnb_ref, ws_ref, bsp_ref,
             wproj_ref, wout_ref, g1post_ref, z_ref, y_ref, h1_ref, u_ref, gpu_ref, xhat_ref, gpv_ref, sa_ref,
             sb_ref, zcur, znext, pbuf):
        del after_ref
        s = pl.program_id(0)

        @pl.when(s == 0)
        def _():
            znext[...] = jnp.zeros((tm, D_IN), F32)
            pbuf[...] = jnp.zeros((tm + HALO, D_MODEL), F32)

        zcur[...] = znext[...]
        xnb = xn_ref[...]

        def project(p):
            cols = slice(p * PROJ_COLS, (p + 1) * PROJ_COLS)
            zp = _dot(xnb, win_ref[:, cols]) + bin_ref[:, cols]
            if (p + 1) * PROJ_COLS <= D_MODEL:
                z_ref[:, cols] = zp
            znext[:, cols] = zp

        project(0)
        pbuf[0:HALO, :] = jnp.where(s <= 1, 0.0, pbuf[0:HALO, :])
        pbuf[HALO:, :] = zcur[:, 0:D_MODEL]
        pooled = _pool_fwd(pbuf, jnp.maximum(s - 1, 0), tm)
        pbuf[0:HALO, :] = pbuf[tm:tm + HALO, :]
        a = jnp.concatenate([_dot(pooled[g].astype(BF16), wpool_ref[g]) for g in range(N_GROUPS)], axis=1)
        a = a * ps_ref[...]
        project(1)
        zu = zcur[:, D_MODEL:2 * D_MODEL]
        u, tu = _gelu(zu)
        u_ref[...] = u.astype(BF16)
        gpu_ref[...] = _gelu_grad(zu, tu).astype(BF16)
        project(2)
        zv = zcur[:, 2 * D_MODEL:3 * D_MODEL]
        gv, tv = _gelu(zv)
        xc = gv - _mean(gv)
        rln = lax.rsqrt(_mean(xc * xc) + EPS)
        xhat = xc * rln
        xhat_ref[...] = xhat.astype(BF16)
        gpv_ref[...] = (_gelu_grad(zv, tv) * rln).astype(BF16)
        vb = (xhat * lng_ref[...] + lnb_ref[...]).astype(BF16)
        wsm = _masked_ws(ws_ref)
        bbr = []
        for h in range(N_HEADS):
            project(3 + h)
            sv = _spatial_head(wsm[h], vb, bsp_ref, h, tm // SGU_BLOCK)
            bbr.append(_dot((u[:, h * HEAD:(h + 1) * HEAD] * sv).astype(BF16), wproj_ref[h]))
        bbr = jnp.concatenate(bbr, axis=1)
        project(7)
        sa = _sigmoid(zcur[:, 3 * D_MODEL:4 * D_MODEL])
        sb = _sigmoid(zcur[:, 4 * D_MODEL:5 * D_MODEL])
        sa_ref[...] = sa.astype(BF16)
        sb_ref[...] = sb.astype(BF16)
        project(8)
        yv = _dot((sa * a + sb * bbr).astype(BF16), wout_ref[...])
        y_ref[...] = yv
        project(9)
        ry = lax.rsqrt(_mean(yv * yv) + EPS)
        h1_ref[...] = xb_ref[...] + yv * ry * g1post_ref[...]

    proj = lambda w: pl.BlockSpec((tm, w), lambda s: (jnp.minimum(s, nt - 1), 0))
    mix = lambda w: pl.BlockSpec((tm, w), lambda s: (jnp.maximum(s - 1, 0), 0))
    return pl.pallas_call(
        body, name="fwd_mix", grid=(nt + 1,),
        in_specs=[_ANY, proj(D_MODEL), mix(D_MODEL),
                  _const_spec((D_MODEL, D_IN)),
                  _const_spec((1, D_IN)), _const_spec((N_GROUPS, GROUP, GROUP)), _const_spec((1, D_MODEL)),
                  _const_spec((1, D_MODEL)), _const_spec((1, D_MODEL)),
                  _const_spec((N_HEADS, SGU_BLOCK, SGU_BLOCK)), _const_spec((SGU_BLOCK, N_HEADS)),
                  _const_spec((N_HEADS, HEAD, HEAD)), _const_spec((D_MODEL, D_MODEL)), _const_spec((1, D_MODEL))],
        out_specs=[proj(D_MODEL), mix(D_MODEL), mix(D_MODEL)] + [mix(D_MODEL)] * 6,
        out_shape=[jax.ShapeDtypeStruct((t_len, D_MODEL), F32),
                   jax.ShapeDtypeStruct((t_len, D_MODEL), F32), jax.ShapeDtypeStruct((t_len, D_MODEL), F32)]
        + [jax.ShapeDtypeStruct((t_len, D_MODEL), BF16)] * 6,
        scratch_shapes=[pltpu.VMEM((tm, D_IN), F32), pltpu.VMEM((tm, D_IN), F32),
                        pltpu.VMEM((tm + HALO, D_MODEL), F32)],
        compiler_params=pltpu.CompilerParams(dimension_semantics=("arbitrary",), vmem_limit_bytes=VMEM_LIMIT),
    )(after, xn, x, win_g, b_in, wpool, pool_scale, lng, lnb, ws, bsp_t, wproj, wout, g1post)


def _mlp(h1, target, g2pre, g2post, w1_g, w2):
    t_len = h1.shape[0]
    tm = TM
    nt = t_len // tm

    def body(h1_ref, tgt_ref, g2pre_ref, g2post_ref, w1_ref, w2_ref,
             hn_ref, f_ref, df1_ref, df2_ref, dh1_ref, dg2post_ref, dg2pre_ref, loss_ref, f1_scr):
        i = pl.program_id(0)

        @pl.when(i == 0)
        def _():
            dg2post_ref[...] = jnp.zeros_like(dg2post_ref)
            dg2pre_ref[...] = jnp.zeros_like(dg2pre_ref)
            loss_ref[...] = jnp.zeros_like(loss_ref)

        h = h1_ref[...]
        r2 = lax.rsqrt(_mean(h * h) + EPS)
        nh = h * r2
        hnb = (nh * g2pre_ref[...]).astype(BF16)
        hn_ref[...] = hnb.T
        for k in range(N_DEV):
            f1_scr[:, k * FF_SHARD:(k + 1) * FF_SHARD] = _dot(hnb, w1_ref[:, k * FF_SHARD:(k + 1) * FF_SHARD])
        r = jnp.maximum(f1_scr[...], 0.0)
        fb = (r * r).astype(BF16)
        f_ref[...] = fb.T
        f2 = _dot(fb, w2_ref[...])
        rf = lax.rsqrt(_mean(f2 * f2) + EPS)
        nf = f2 * rf
        diff = h + nf * g2post_ref[...] - tgt_ref[...]
        loss_ref[...] += (0.5 / D_MODEL) * jnp.sum(diff * diff)
        dout = diff * (1.0 / D_MODEL)
        dg2post_ref[...] += _colsum(dout * nf)
        dn = dout * g2post_ref[...]
        df2b = (rf * (dn - nf * _mean(dn * nf))).astype(BF16)
        df2_ref[...] = df2b
        df = _dot_nt(df2b, w2_ref[...])
        df1b = (df * (2.0 * jnp.maximum(f1_scr[...], 0.0))).astype(BF16)
        df1_ref[...] = df1b
        dhn = _dot_nt(df1b, w1_ref[...])
        dg2pre_ref[...] += _colsum(dhn * nh)
        dnh = dhn * g2pre_ref[...]
        dh1_ref[...] = dout + r2 * (dnh - nh * _mean(dnh * nh))

    tok = lambda w: pl.BlockSpec((tm, w), lambda i: (i, 0))
    return pl.pallas_call(
        body, name="mlp_fwd_bwd", grid=(nt,),
        in_specs=[tok(D_MODEL), tok(D_MODEL), _const_spec((1, D_MODEL)), _const_spec((1, D_MODEL)),
                  _const_spec((D_MODEL, D_FF)), _const_spec((D_FF, D_MODEL))],
        out_specs=[pl.BlockSpec((D_MODEL, tm), lambda i: (0, i)), pl.BlockSpec((D_FF, tm), lambda i: (0, i)),
                   tok(D_FF), tok(D_MODEL), tok(D_MODEL),
                   _acc_spec((1, D_MODEL)), _acc_spec((1, D_MODEL)), _acc_spec((1, 128))],
        out_shape=[jax.ShapeDtypeStruct((D_MODEL, t_len), BF16), jax.ShapeDtypeStruct((D_FF, t_len), BF16),
                   jax.ShapeDtypeStruct((t_len, D_FF), BF16), jax.ShapeDtypeStruct((t_len, D_MODEL), BF16),
                   jax.ShapeDtypeStruct((t_len, D_MODEL), F32), jax.ShapeDtypeStruct((1, D_MODEL), F32),
                   jax.ShapeDtypeStruct((1, D_MODEL), F32), jax.ShapeDtypeStruct((1, 128), F32)],
        scratch_shapes=[pltpu.VMEM((tm, D_FF), F32)],
        compiler_params=pltpu.CompilerParams(dimension_semantics=("arbitrary",), vmem_limit_bytes=VMEM_LIMIT),
    )(h1, target, g2pre, g2post, w1_g, w2)


def _bwd_mix(after, dh1, y, z, saved, wpool, pool_scale, lng, lnb, ws, bsp_t, wproj, wout, g1post):
    t_len = y.shape[0]
    tm = TM_BWD
    nt = t_len // tm
    nblk = tm // SGU_BLOCK

    def body(after_ref, dh1_ref, y_ref, z_ref, zh_ref, u_ref, gpu_ref, xhat_ref, gpv_ref, sa_ref, sb_ref,
             wpool_ref, ps_ref, lng_ref, lnb_ref, ws_ref,
             bsp_ref, wproj_ref, wout_ref, g1post_ref,
             dz_ref, mg_ref, dy_ref, dwpool_ref, dwproj_ref, dws_ref, dbsp_ref, dg1post_ref, dps_ref,
             dlng_ref, dlnb_ref, dbin_ref, pbuf, qbuf):
        del after_ref
        i = pl.program_id(0)
        ti = nt - 1 - i

        @pl.when(i == 0)
        def _():
            for ref in (dwpool_ref, dwproj_ref, dws_ref, dbsp_ref, dg1post_ref, dps_ref, dlng_ref, dlnb_ref,
                        dbin_ref):
                ref[...] = jnp.zeros_like(ref)
            qbuf[tm:tm + HALO, :] = jnp.zeros((HALO, D_MODEL), F32)

        pbuf[0:HALO, :] = jnp.where(ti > 0, zh_ref[...], 0.0)
        pbuf[HALO:, :] = z_ref[...]
        pooled = _pool_fwd(pbuf, ti, tm)
        pooled_b = [p.astype(BF16) for p in pooled]
        a_raw = jnp.concatenate([_dot(pooled_b[g], wpool_ref[g]) for g in range(N_GROUPS)], axis=1)
        wsm = _masked_ws(ws_ref)
        u = u_ref[...].astype(F32)
        xhat = xhat_ref[...].astype(F32)
        vb = (xhat * lng_ref[...] + lnb_ref[...]).astype(BF16)
        sv_heads = [_spatial_head(wsm[h], vb, bsp_ref, h, nblk) for h in range(N_HEADS)]
        gated_b = [(u[:, h * HEAD:(h + 1) * HEAD] * sv_heads[h]).astype(BF16) for h in range(N_HEADS)]
        bbr = jnp.concatenate([_dot(gated_b[h], wproj_ref[h]) for h in range(N_HEADS)], axis=1)
        sa = sa_ref[...].astype(F32)
        sb = sb_ref[...].astype(F32)
        a = a_raw * ps_ref[...]
        mg_ref[...] = (sa * a + sb * bbr).astype(BF16).T

        dh = dh1_ref[...]
        yv = y_ref[...]
        ry = lax.rsqrt(_mean(yv * yv) + EPS)
        ny = yv * ry
        dg1post_ref[...] += _colsum(dh * ny)
        dn = dh * g1post_ref[...]
        dyb = (ry * (dn - ny * _mean(dn * ny))).astype(BF16)
        dy_ref[...] = dyb
        dmg = _dot_nt(dyb, wout_ref[...])

        da = dmg * sa
        dbbr = dmg * sb
        dzga = dmg * a * sa * (1.0 - sa)
        dzgb = dmg * bbr * sb * (1.0 - sb)
        dz_ref[:, 3 * D_MODEL:4 * D_MODEL] = dzga.astype(BF16)
        dz_ref[:, 4 * D_MODEL:5 * D_MODEL] = dzgb.astype(BF16)
        dbin_ref[:, 3 * D_MODEL:4 * D_MODEL] += _colsum(dzga)
        dbin_ref[:, 4 * D_MODEL:5 * D_MODEL] += _colsum(dzgb)

        dps_ref[...] += _colsum(da * a_raw)
        da_raw_b = (da * ps_ref[...]).astype(BF16)
        pos = lax.broadcasted_iota(jnp.int32, (tm, 1), 0) + ti * tm + 1
        dpooled = []
        for g, w in enumerate(WINDOWS):
            cols = slice(g * GROUP, (g + 1) * GROUP)
            dwpool_ref[g] += _dot_tn(pooled_b[g], da_raw_b[:, cols])
            dp = _dot_nt(da_raw_b[:, cols], wpool_ref[g])
            dpooled.append(dp)
            qbuf[0:tm, cols] = dp * (1.0 / jnp.minimum(pos, w).astype(F32))
        n_ext = tm + HALO
        dzp = []
        for g, w in enumerate(WINDOWS):
            e = qbuf[:, g * GROUP:(g + 1) * GROUP]
            s, sh = e, 1
            while sh < w:
                s = s + pltpu.roll(s, n_ext - sh, 0)
                sh *= 2
            dzp.append(s[0:tm] - dpooled[g])
        qbuf[tm:tm + HALO, :] = qbuf[0:HALO, :]
        dzp = jnp.concatenate(dzp, axis=1)
        dz_ref[:, 0:D_MODEL] = dzp.astype(BF16)
        dbin_ref[:, 0:D_MODEL] += _colsum(dzp)

        dv_heads = []
        du_heads = []
        for h in range(N_HEADS):
            cols = slice(h * HEAD, (h + 1) * HEAD)
            dbbr_b = dbbr[:, cols].astype(BF16)
            dwproj_ref[h] += _dot_tn(gated_b[h], dbbr_b)
            dgated = _dot_nt(dbbr_b, wproj_ref[h])
            du_heads.append(dgated * sv_heads[h])
            dsv = dgated * u[:, cols]
            dsv_b = dsv.astype(BF16)
            rows = []
            for n in range(nblk):
                blk = slice(n * SGU_BLOCK, (n + 1) * SGU_BLOCK)
                rows.append(_dot_tn(wsm[h], dsv_b[blk]))
                dws_ref[h] += _dot_nt(dsv_b[blk], vb[blk, cols])
                dbsp_ref[:, h:h + 1] += jnp.sum(dsv[blk], axis=1, keepdims=True)
            dv_heads.append(jnp.concatenate(rows, axis=0))
        dzu = jnp.concatenate(du_heads, axis=1) * gpu_ref[...].astype(F32)
        dz_ref[:, D_MODEL:2 * D_MODEL] = dzu.astype(BF16)
        dbin_ref[:, D_MODEL:2 * D_MODEL] += _colsum(dzu)
        dv = jnp.concatenate(dv_heads, axis=1)
        dlng_ref[...] += _colsum(dv * xhat)
        dlnb_ref[...] += _colsum(dv)
        dxh = dv * lng_ref[...]
        dzv = (dxh - _mean(dxh) - xhat * _mean(dxh * xhat)) * gpv_ref[...].astype(F32)
        dz_ref[:, 2 * D_MODEL:3 * D_MODEL] = dzv.astype(BF16)
        dbin_ref[:, 2 * D_MODEL:3 * D_MODEL] += _colsum(dzv)

        @pl.when(i == nt - 1)
        def _():
            ri = lax.broadcasted_iota(jnp.int32, (SGU_BLOCK, SGU_BLOCK), 0) // CHUNK
            ci = lax.broadcasted_iota(jnp.int32, (SGU_BLOCK, SGU_BLOCK), 1) // CHUNK
            for h in range(N_HEADS):
                dws_ref[h] = jnp.where(ri >= ci, dws_ref[h], 0.0)

    tok = lambda w: pl.BlockSpec((tm, w), lambda i: (nt - 1 - i, 0))
    halo = pl.BlockSpec((HALO, D_MODEL), lambda i: (jnp.maximum((nt - 1 - i) * (tm // HALO) - 1, 0), 0))
    return pl.pallas_call(
        body, name="bwd_mix", grid=(nt,),
        in_specs=[_ANY, tok(D_MODEL), tok(D_MODEL), tok(D_MODEL), halo] + [tok(D_MODEL)] * 6
        + [_const_spec((N_GROUPS, GROUP, GROUP)),
                  _const_spec((1, D_MODEL)), _const_spec((1, D_MODEL)), _const_spec((1, D_MODEL)),
                  _const_spec((N_HEADS, SGU_BLOCK, SGU_BLOCK)), _const_spec((SGU_BLOCK, N_HEADS)),
                  _const_spec((N_HEADS, HEAD, HEAD)), _const_spec((D_MODEL, D_MODEL)), _const_spec((1, D_MODEL))],
        out_specs=[tok(D_IN), pl.BlockSpec((D_MODEL, tm), lambda i: (0, nt - 1 - i)), tok(D_MODEL),
                   _acc_spec((N_GROUPS, GROUP, GROUP)), _acc_spec((N_HEADS, HEAD, HEAD)),
                   _acc_spec((N_HEADS, SGU_BLOCK, SGU_BLOCK)), _acc_spec((SGU_BLOCK, N_HEADS)),
                   _acc_spec((1, D_MODEL)), _acc_spec((1, D_MODEL)), _acc_spec((1, D_MODEL)), _acc_spec((1, D_MODEL)),
                   _acc_spec((1, D_IN))],
        out_shape=[jax.ShapeDtypeStruct((t_len, D_IN), BF16),
                   jax.ShapeDtypeStruct((D_MODEL, t_len), BF16), jax.ShapeDtypeStruct((t_len, D_MODEL), BF16),
                   jax.ShapeDtypeStruct((N_GROUPS, GROUP, GROUP), F32), jax.ShapeDtypeStruct((N_HEADS, HEAD, HEAD), F32),
                   jax.ShapeDtypeStruct((N_HEADS, SGU_BLOCK, SGU_BLOCK), F32),
                   jax.ShapeDtypeStruct((SGU_BLOCK, N_HEADS), F32),
                   jax.ShapeDtypeStruct((1, D_MODEL), F32), jax.ShapeDtypeStruct((1, D_MODEL), F32),
                   jax.ShapeDtypeStruct((1, D_MODEL), F32), jax.ShapeDtypeStruct((1, D_MODEL), F32),
                   jax.ShapeDtypeStruct((1, D_IN), F32)],
        scratch_shapes=[pltpu.VMEM((tm + HALO, D_MODEL), F32), pltpu.VMEM((tm + HALO, D_MODEL), F32)],
        compiler_params=pltpu.CompilerParams(dimension_semantics=("arbitrary",), vmem_limit_bytes=VMEM_LIMIT),
    )(after, dh1, y, z, z, *saved, wpool, pool_scale, lng, lnb, ws, bsp_t, wproj, wout, g1post)


def _bwd_in(after, dz_b, x, dh1, g1pre, win_g):
    t_len = x.shape[0]
    tm = TM_IN
    nt = t_len // tm

    def body(after_ref, dz_ref, x_ref, dh1_ref, g1_ref, win_ref, dx_ref, dg1pre_ref):
        del after_ref

        @pl.when(pl.program_id(0) == 0)
        def _():
            dg1pre_ref[...] = jnp.zeros_like(dg1pre_ref)

        dxn = _dot_nt(dz_ref[...], win_ref[...])
        xv = x_ref[...]
        r1 = lax.rsqrt(_mean(xv * xv) + EPS)
        nx = xv * r1
        dg1pre_ref[...] += _colsum(dxn * nx)
        dnx = dxn * g1_ref[...]
        dx_ref[...] = r1 * (dnx - nx * _mean(dnx * nx)) + dh1_ref[...]

    tok = lambda w: pl.BlockSpec((tm, w), lambda i: (i, 0))
    return pl.pallas_call(
        body, name="bwd_in", grid=(nt,),
        in_specs=[_ANY, tok(D_IN), tok(D_MODEL), tok(D_MODEL), _const_spec((1, D_MODEL)),
                  _const_spec((D_MODEL, D_IN))],
        out_specs=[tok(D_MODEL), _acc_spec((1, D_MODEL))],
        out_shape=[jax.ShapeDtypeStruct((t_len, D_MODEL), F32), jax.ShapeDtypeStruct((1, D_MODEL), F32)],
        compiler_params=pltpu.CompilerParams(dimension_semantics=("arbitrary",), vmem_limit_bytes=VMEM_LIMIT),
    )(after, dz_b, x, dh1, g1pre, win_g)


def _owner_of_slot(s):
    return 4 * ((s // 2) % 2) + 2 * (s % 2) + s // 4


def _wgrad(name, a_t, b, shard, a_sliced):
    t_len = b.shape[0]
    am = shard if a_sliced else a_t.shape[0]
    bn = b.shape[1] if a_sliced else shard

    def body(a_ref, b_ref, o_ref):
        o_ref[...] = _dot(a_ref[...], b_ref[...]).astype(BF16)

    if a_sliced:
        in_specs = [pl.BlockSpec((am, t_len), lambda s: (_owner_of_slot(s), 0)),
                    pl.BlockSpec((t_len, bn), lambda s: (0, 0), pipeline_mode=pl.Buffered(1))]
    else:
        in_specs = [pl.BlockSpec((am, t_len), lambda s: (0, 0), pipeline_mode=pl.Buffered(1)),
                    pl.BlockSpec((t_len, bn), lambda s: (0, _owner_of_slot(s)))]
    return pl.pallas_call(
        body, name=name, grid=(N_DEV,), in_specs=in_specs,
        out_specs=pl.BlockSpec((None, am, bn), lambda s: (s, 0, 0)),
        out_shape=jax.ShapeDtypeStruct((N_DEV, am, bn), BF16),
        compiler_params=pltpu.CompilerParams(dimension_semantics=("arbitrary",), vmem_limit_bytes=VMEM_LIMIT),
    )(a_t, b)


def _wgrad_in(xn_t, dz_b):
    t_len = dz_b.shape[0]

    def body(a_ref, b_ref, o_ref):
        res = _dot(a_ref[...], b_ref[...])
        o_ref[0] = res[:, 0:IN_SHARD].astype(BF16)
        o_ref[1] = res[:, IN_SHARD:2 * IN_SHARD].astype(BF16)

    out = pl.pallas_call(
        body, name="wgrad_in", grid=(N_CHIPS,),
        in_specs=[pl.BlockSpec((D_MODEL, t_len), lambda q: (0, 0), pipeline_mode=pl.Buffered(1)),
                  pl.BlockSpec((t_len, 2 * IN_SHARD), lambda q: (0, q))],
        out_specs=pl.BlockSpec((2, None, D_MODEL, IN_SHARD), lambda q: (0, q, 0, 0)),
        out_shape=jax.ShapeDtypeStruct((2, N_CHIPS, D_MODEL, IN_SHARD), BF16),
        compiler_params=pltpu.CompilerParams(dimension_semantics=("arbitrary",), vmem_limit_bytes=VMEM_LIMIT),
    )(xn_t, dz_b)
    return out.reshape(N_DEV, D_MODEL, IN_SHARD)


def _wgrad_out(mg_t, dy_b):
    t_len = dy_b.shape[0]

    def body(a_ref, b_ref, o_ref):
        res = _dot(a_ref[...], b_ref[...])
        o_ref[0] = res[0:OUT_SHARD].astype(BF16)
        o_ref[1] = res[OUT_SHARD:2 * OUT_SHARD].astype(BF16)

    out = pl.pallas_call(
        body, name="wgrad_out", grid=(N_CHIPS,),
        in_specs=[pl.BlockSpec((2 * OUT_SHARD, t_len), lambda q: (q, 0)),
                  pl.BlockSpec((t_len, D_MODEL), lambda q: (0, 0), pipeline_mode=pl.Buffered(1))],
        out_specs=pl.BlockSpec((2, None, OUT_SHARD, D_MODEL), lambda q: (0, q, 0, 0)),
        out_shape=jax.ShapeDtypeStruct((2, N_CHIPS, OUT_SHARD, D_MODEL), BF16),
        compiler_params=pltpu.CompilerParams(dimension_semantics=("arbitrary",), vmem_limit_bytes=VMEM_LIMIT),
    )(mg_t, dy_b)
    return out.reshape(N_DEV, OUT_SHARD, D_MODEL)


def _coords():
    return lax.axis_index("x"), lax.axis_index("y"), lax.axis_index("c")


_ANY = pl.BlockSpec(memory_space=pl.ANY)


def _pair_exchange(name, bigs, smalls):
    nb, n = len(bigs), len(bigs) + len(smalls)

    def body(*refs):
        src, dst = refs[:n], refs[n:2 * n]
        send_sems, recv_sems = refs[2 * n:]
        x, y, c = _coords()
        copies = []
        for j in range(n):
            s = src[j].at[pl.ds(4 * (1 - c), 4)] if j < nb else src[j]
            cp = pltpu.make_async_remote_copy(src_ref=s, dst_ref=dst[j], send_sem=send_sems.at[j],
                                              recv_sem=recv_sems.at[j], device_id=(x, y, 1 - c), device_id_type=MESH)
            cp.start()
            copies.append(cp)
        for cp in copies:
            cp.wait()

    return pl.pallas_call(
        body, name=name,
        in_specs=[_ANY] * n, out_specs=[_ANY] * n,
        out_shape=[jax.ShapeDtypeStruct((4,) + b.shape[1:], b.dtype) for b in bigs]
        + [jax.ShapeDtypeStruct(s.shape, s.dtype) for s in smalls],
        scratch_shapes=[pltpu.SemaphoreType.DMA((n,)), pltpu.SemaphoreType.DMA((n,))],
    )(*bigs, *smalls)


_HBM = pl.BlockSpec(memory_space=pltpu.HBM)
_SEM = pl.BlockSpec(memory_space=pltpu.SEMAPHORE)
_VMEM = pl.BlockSpec(memory_space=pltpu.VMEM)
_EFFECT = pltpu.SideEffectType.DATAFLOW_SIDE_EFFECTING
_TOKEN = jax.ShapeDtypeStruct((8, 128), F32)


def _in_hbm(a):
    return pltpu.with_memory_space_constraint(a, pltpu.HBM)


def _split_call(name, body, n_sems_out, arrays, sems_in=(), after=None):
    na, ns = len(arrays), len(sems_in)
    has_after = after is not None

    def kernel_body(*refs):
        arr = refs[:na]
        s_in = refs[na:na + ns]
        outs = refs[na + ns + has_after:]
        body(arr, s_in, outs[:n_sems_out])
        outs[-1][...] = jnp.zeros((8, 128), F32)

    out_shape = ([pltpu.SemaphoreType.DMA(())] * n_sems_out + [pltpu.HBM(a.shape, a.dtype) for a in arrays] + [_TOKEN])
    res = pl.pallas_call(
        kernel_body, name=name, out_shape=out_shape,
        in_specs=[_HBM] * na + [_SEM] * ns + [_ANY] * has_after,
        out_specs=[_SEM] * n_sems_out + [_HBM] * na + [_VMEM],
        input_output_aliases={i: n_sems_out + i for i in range(na)},
        compiler_params=pltpu.CompilerParams(has_side_effects=_EFFECT),
    )(*[_in_hbm(a) for a in arrays], *sems_in, *([after] if has_after else []))
    return list(res[:n_sems_out]), list(res[n_sems_out:n_sems_out + na]), res[-1]


def _wait_bytes_of(ref, send_sem, recv_sem, peer, send=True, recv=True):
    cp = pltpu.make_async_remote_copy(src_ref=ref, dst_ref=ref, send_sem=send_sem, recv_sem=recv_sem,
                                      device_id=peer, device_id_type=MESH)
    if send:
        cp.wait_send()
    if recv:
        cp.wait_recv()


def _gather_behind(tag, shards, after, work, by_columns=()):
    n = len(shards)
    lands = [lax.empty((s.shape[0], N_DEV * s.shape[1]) if j in by_columns else (N_DEV,) + s.shape, s.dtype)
             for j, s in enumerate(shards)]

    def slots(arr, j, first, count=1):
        if j in by_columns:
            cols = shards[j].shape[1]
            return arr[n + j].at[:, pl.ds(pl.multiple_of(first * cols, 128), count * cols)]
        return arr[n + j].at[first] if count == 1 else arr[n + j].at[pl.ds(first, count)]

    def own_slot(arr, j, sem):
        x, y, c = _coords()
        return pltpu.make_async_copy(arr[j], slots(arr, j, 4 * x + 2 * y + c), sem)

    def start(arr, _, sems):
        x, y, c = _coords()
        me = 4 * x + 2 * y + c
        for j in range(n):
            for chip in [(1 - x, y), (x, 1 - y), (1 - x, 1 - y)]:
                pltpu.make_async_remote_copy(src_ref=arr[j], dst_ref=slots(arr, j, me), send_sem=sems[j],
                                             recv_sem=sems[n + j], device_id=(*chip, c), device_id_type=MESH).start()
        for j in range(n):
            pltpu.make_async_remote_copy(src_ref=arr[j], dst_ref=slots(arr, j, me), send_sem=sems[2 * n + j],
                                         recv_sem=sems[3 * n + j], device_id=(x, y, 1 - c),
                                         device_id_type=MESH).start()
            own_slot(arr, j, sems[4 * n + j]).start()

    def middle(arr, s_in, sems):
        x, y, c = _coords()
        sibling = (x, y, 1 - c)
        for j in range(n):
            _wait_bytes_of(slots(arr, j, 0, 3), s_in[j], s_in[n + j], sibling)
            for chip in [(1 - x, y), (x, 1 - y), (1 - x, 1 - y)]:
                slot = slots(arr, j, 4 * chip[0] + 2 * chip[1] + c)
                pltpu.make_async_remote_copy(src_ref=slot, dst_ref=slot, send_sem=sems[j], recv_sem=sems[n + j],
                                             device_id=sibling, device_id_type=MESH).start()

    def finish(arr, s_in, _):
        x, y, c = _coords()
        sibling = (x, y, 1 - c)
        for j in range(n):
            _wait_bytes_of(slots(arr, j, 0, 1), s_in[j], s_in[n + j], sibling)
            own_slot(arr, j, s_in[2 * n + j]).wait()
            _wait_bytes_of(slots(arr, j, 0, 3), s_in[3 * n + j], s_in[4 * n + j], sibling)

    sems, arrays, token = _split_call("gather_%s_start" % tag, start, 5 * n, list(shards) + lands, after=after)
    result = work(token)
    fwd_sems, arrays, token = _split_call("gather_%s_middle" % tag, middle, 2 * n, arrays, sems_in=sems[:2 * n],
                                          after=result[0])
    _, arrays, _ = _split_call("gather_%s_finish" % tag, finish, 0, arrays, sems_in=sems[2 * n:] + fwd_sems,
                               after=token)
    return arrays[n:], result


def _chip_exchange_behind(tag, bigs, smalls, work):
    nb, n = len(bigs), len(bigs) + len(smalls)
    lands = [lax.empty(s.shape, s.dtype) for s in bigs] + [lax.empty((N_CHIPS,) + s.shape, s.dtype) for s in smalls]

    def own_slot(arr, j, sem):
        x, y, _ = _coords()
        q_me = 2 * x + y
        return pltpu.make_async_copy(arr[j].at[q_me] if j < nb else arr[j], arr[n + j].at[q_me], sem)

    def start(arr, _, sems):
        x, y, c = _coords()
        q_me = 2 * x + y
        for j in range(n):
            for peer in [(1 - x, y, c), (x, 1 - y, c), (1 - x, 1 - y, c)]:
                piece = arr[j].at[2 * peer[0] + peer[1]] if j < nb else arr[j]
                pltpu.make_async_remote_copy(src_ref=piece, dst_ref=arr[n + j].at[q_me],
                                             send_sem=sems[j], recv_sem=sems[n + j], device_id=peer,
                                             device_id_type=MESH).start()
            own_slot(arr, j, sems[2 * n + j]).start()

    def finish(arr, s_in, _):
        x, y, c = _coords()
        for j in range(n):
            _wait_bytes_of(arr[n + j].at[pl.ds(0, 3)], s_in[j], s_in[n + j], (x, y, 1 - c))
            own_slot(arr, j, s_in[2 * n + j]).wait()

    sems, arrays, token = _split_call("chip_exchange_%s_start" % tag, start, 3 * n, list(bigs) + list(smalls) + lands)
    result = work(token)
    _, arrays, _ = _split_call("chip_exchange_%s_finish" % tag, finish, 0, arrays, sems_in=sems, after=result[0])
    return arrays[n:], result


def _pair_sum_big(name, c_arr, mine, theirs):
    _, rows, cols = theirs.shape
    tr = rows

    def body(c_ref, a_ref, b_ref, o_ref):
        del c_ref
        o_ref[...] = (a_ref[...].astype(F32) + b_ref[...].astype(F32)).astype(BF16)

    return pl.pallas_call(
        body, name=name,
        grid_spec=pltpu.PrefetchScalarGridSpec(
            num_scalar_prefetch=1, grid=(N_CHIPS, rows // tr),
            in_specs=[pl.BlockSpec((None, tr, cols), lambda q, r, c_ref: (4 * c_ref[0] + q, r, 0)),
                      pl.BlockSpec((None, tr, cols), lambda q, r, c_ref: (q, r, 0))],
            out_specs=pl.BlockSpec((None, tr, cols), lambda q, r, c_ref: (q, r, 0))),
        out_shape=jax.ShapeDtypeStruct(theirs.shape, BF16),
        compiler_params=pltpu.CompilerParams(dimension_semantics=("arbitrary", "arbitrary")),
    )(c_arr, mine, theirs)


def _pair_sum_small(name, c_arr, slotted, slotted_theirs, mine, theirs):
    ns, n = len(slotted), len(mine)

    def body(c_ref, *refs):
        first = 4 * c_ref[0]
        outs = refs[2 * (ns + n):]
        for j in range(ns):
            own = refs[j][pl.ds(first, N_CHIPS)].astype(F32)
            outs[j][...] = (own + refs[ns + j][...].astype(F32)).astype(BF16)
        for j in range(n):
            outs[ns + j][...] = refs[2 * ns + j][...] + refs[2 * ns + n + j][...]

    return pl.pallas_call(
        body, name=name,
        in_specs=[pl.BlockSpec(memory_space=pltpu.SMEM)] + [_VMEM] * (2 * (ns + n)),
        out_shape=[jax.ShapeDtypeStruct(t.shape, BF16) for t in slotted_theirs]
        + [jax.ShapeDtypeStruct(m.shape, m.dtype) for m in mine],
    )(c_arr, *slotted, *slotted_theirs, *mine, *theirs)


def _adamw_math(w, g, m, v):
    m = ADAM_B1 * m + (1.0 - ADAM_B1) * g
    v = ADAM_B2 * v + (1.0 - ADAM_B2) * (g * g)
    m_hat = m / (1.0 - ADAM_B1 ** ADAM_STEP)
    v_hat = v / (1.0 - ADAM_B2 ** ADAM_STEP)
    delta = -ADAM_LR * (m_hat / (jnp.sqrt(v_hat) + ADAM_EPS) + ADAM_WD * w)
    return delta, m, v


def _adamw_big(name, after, chip_sums, w, m, v):
    rows, cols = w.shape
    tr = min(rows, 512)

    def body(after_ref, t_ref, w_ref, m_ref, v_ref, g_out, d_out, m_out, v_out):
        del after_ref
        g = t_ref[0].astype(F32)
        for q in range(1, N_CHIPS):
            g = g + t_ref[q].astype(F32)
        d, mn, vn = _adamw_math(w_ref[...], g, m_ref[...], v_ref[...])
        g_out[...] = g
        d_out[...] = d
        m_out[...] = mn
        v_out[...] = vn

    blk = pl.BlockSpec((tr, cols), lambda r: (r, 0))
    return pl.pallas_call(
        body, name=name, grid=(rows // tr,),
        in_specs=[_ANY, pl.BlockSpec((N_CHIPS, tr, cols), lambda r: (0, r, 0)), blk, blk, blk],
        out_specs=[blk] * 4,
        out_shape=[jax.ShapeDtypeStruct((rows, cols), F32)] * 4,
        compiler_params=pltpu.CompilerParams(dimension_semantics=("arbitrary",), vmem_limit_bytes=VMEM_LIMIT),
    )(after, chip_sums, w, m, v)


SMALL_ROWS = ("loss", "pool_scale", "sgu_ln_g", "sgu_ln_b", "norm1_post_g", "norm2_pre_g", "norm2_post_g")


N_ROW_PARAMS = len(SMALL_ROWS) - 1


def _adamw_small(u_rows, u_others, params):
    n, n_oth = len(params), len(u_others)

    def body(*refs):
        urow_ref, others = refs[0], refs[1:1 + n_oth]
        wmv = refs[1 + n_oth:1 + n_oth + 3 * n]
        loss_ref = refs[1 + n_oth + 3 * n]
        outs = refs[2 + n_oth + 3 * n:]

        def total(ref, idx):
            g = ref[(0,) + idx].astype(F32)
            for q in range(1, N_CHIPS):
                g = g + ref[(q,) + idx].astype(F32)
            return g

        loss_ref[...] = total(urow_ref, (slice(0, 1), slice(None)))
        for p in range(n):
            if p < N_ROW_PARAMS:
                g = total(urow_ref, (slice(p + 1, p + 2), slice(None)))
            else:
                g = total(others[p - N_ROW_PARAMS], (slice(None), slice(None)))
            d, mn, vn = _adamw_math(wmv[3 * p][...], g, wmv[3 * p + 1][...], wmv[3 * p + 2][...])
            outs[4 * p][...] = g
            outs[4 * p + 1][...] = d
            outs[4 * p + 2][...] = mn
            outs[4 * p + 3][...] = vn

    flat = [a for p in params for a in p]
    out_shape = [jax.ShapeDtypeStruct((1, D_MODEL), F32)]
    for w, _, _ in params:
        out_shape += [jax.ShapeDtypeStruct(w.shape, F32)] * 4
    return pl.pallas_call(body, name="adamw_small", out_shape=out_shape)(u_rows, *u_others, *flat)


def _to_bf16(arrays):
    n = len(arrays)

    def body(*refs):
        for j in range(n):
            refs[n + j][...] = refs[j][...].astype(BF16)

    return pl.pallas_call(
        body, name="shards_to_bf16", out_shape=[jax.ShapeDtypeStruct(a.shape, BF16) for a in arrays],
    )(*arrays)


def _slots_of_rows(full):
    owners = [_owner_of_slot(s) for s in range(N_DEV)]
    parts = full.reshape(N_GROUPS, N_DEV, PG_SHARD, GROUP)
    return jnp.stack([parts[:, o] for o in owners]).reshape(N_DEV, N_GROUPS * PG_SHARD, GROUP).astype(BF16)


def kernel(x, norm1_pre_g, w_in, b_in, w_pool, pool_scale, sgu_ln_g, sgu_ln_b, w_spatial, b_spatial, w_sgu_proj, w_out, norm1_post_g, norm2_pre_g, w_ff1, w_ff2, norm2_post_g, loss_target, m_norm1_pre_g, m_w_in, m_b_in, m_w_pool, m_pool_scale, m_sgu_ln_g, m_sgu_ln_b, m_w_spatial, m_b_spatial, m_w_sgu_proj, m_w_out, m_norm1_post_g, m_norm2_pre_g, m_w_ff1, m_w_ff2, m_norm2_post_g, v_norm1_pre_g, v_w_in, v_b_in, v_w_pool, v_pool_scale, v_sgu_ln_g, v_sgu_ln_b, v_w_spatial, v_b_spatial, v_w_sgu_proj, v_w_out, v_norm1_post_g, v_norm2_pre_g, v_w_ff1, v_w_ff2, v_norm2_post_g):
    t_len = x.shape[1]
    row = lambda a: a.reshape(1, -1)
    x2 = x.reshape(t_len, D_MODEL)
    tgt2 = loss_target.reshape(t_len, D_MODEL)
    pg2 = lambda a: a.reshape(N_GROUPS * PG_SHARD, GROUP)

    shards_b = _to_bf16([w_in, pg2(w_pool), pg2(w_sgu_proj), w_out, w_ff1, w_ff2])

    def prework(token):
        return _prenorm(token, x2, row(norm1_pre_g))

    (win_f, g_pool, g_proj, g_out), (xn, xn_b) = _gather_behind("mix", shards_b[:4], None, prework, by_columns=(0,))
    regroup = lambda g: g.reshape(N_DEV, N_GROUPS, PG_SHARD, GROUP).transpose(1, 0, 2, 3).reshape(N_GROUPS, GROUP, GROUP)
    wpool_f, wproj_f = regroup(g_pool), regroup(g_proj)
    wout_f = g_out.reshape(D_MODEL, D_MODEL)
    bsp_t = b_spatial.T

    def forward(token):
        z, y, h1, *saved = _fwd_mix(token, xn, x2, win_f, row(b_in), wpool_f, row(pool_scale),
                                    row(sgu_ln_g), row(sgu_ln_b), w_spatial, bsp_t, wproj_f, wout_f,
                                    row(norm1_post_g))
        return h1, z, y, saved

    (g_ff1, g_ff2), (h1, z, y, saved) = _gather_behind("ff", shards_b[4:], win_f,
                                                       forward, by_columns=(0,))
    w2_f = g_ff2.reshape(D_FF, D_MODEL)
    hn_b, f_b, df1_b, df2_b, dh1, dg2post, dg2pre, loss_p = _mlp(h1, tgt2, row(norm2_pre_g), row(norm2_post_g),
                                                               g_ff1, w2_f)
    p_ff2 = _wgrad("wgrad_ff2", f_b, df2_b, FF_SHARD, True)
    p_ff1 = _wgrad("wgrad_ff1", hn_b, df1_b, FF_SHARD, False)
    c_arr = lax.axis_index("c").astype(jnp.int32).reshape(1)
    ff_parts = [p_ff1, p_ff2]
    got_ff = _pair_exchange("pair_exchange_ff", ff_parts, [])
    chip_ff = [_pair_sum_big("pair_sum_" + nm, c_arr, b, r) for nm, b, r in zip(("ff1", "ff2"), ff_parts, got_ff)]

    def backward_mix(token):
        return _bwd_mix(token, dh1, y, z, saved, wpool_f, row(pool_scale), row(sgu_ln_g), row(sgu_ln_b), w_spatial,
                        bsp_t, wproj_f, wout_f, row(norm1_post_g))

    summed_ff, (dz_b, mg_b, dy_b, dwpool, dwproj, dws, dbsp_t, dg1post, dps, dlng, dlnb,
                dbin) = _chip_exchange_behind("ff", chip_ff, [], backward_mix)

    p_in = _wgrad_in(xn_b, dz_b)
    bigs = [p_in, _slots_of_rows(dwpool), _slots_of_rows(dwproj)]
    rows = jnp.concatenate([jnp.broadcast_to(loss_p[:, 0:1], (1, D_MODEL)), dps, dlng, dlnb, dg1post, dg2pre, dg2post],
                           axis=0)
    smalls = [rows, dbin, dws.reshape(N_HEADS * SGU_BLOCK, SGU_BLOCK), dbsp_t.T]
    got = _pair_exchange("pair_exchange_in", bigs, smalls)
    chip_in = _pair_sum_big("pair_sum_in", c_arr, p_in, got[0])
    sums = _pair_sum_small("pair_sum_small", c_arr, bigs[1:], got[1:3], smalls, got[3:])
    chip_bigs, chip_smalls = [chip_in] + list(sums[:2]), sums[2:]

    def backward_rest(token):
        dx, dg1pre = _bwd_in(token, dz_b, x2, dh1, row(norm1_pre_g), win_f)
        p_out = _wgrad_out(mg_b, dy_b)
        got_out = _pair_exchange("pair_exchange_out", [p_out], [dg1pre])
        chip_out, chip_g1pre = _pair_sum_small("pair_sum_out", c_arr, [p_out], got_out[:1], [dg1pre], got_out[1:])
        upd_ff = [_adamw_big("adamw_" + nm, p_out, t, *p) for nm, t, p in
                  (("ff1", summed_ff[0], (w_ff1, m_w_ff1, v_w_ff1)), ("ff2", summed_ff[1], (w_ff2, m_w_ff2, v_w_ff2)))]
        return upd_ff[1][0], dx, chip_out, chip_g1pre, upd_ff

    summed_in, (_, dx, chip_out, chip_g1pre, upd_ff) = _chip_exchange_behind("in", chip_bigs, chip_smalls,
                                                                             backward_rest)

    def update_in(token):
        res = _adamw_big("adamw_in", token, summed_in[0], w_in, m_w_in, v_w_in)
        return res[0], res

    summed_out, (_, upd_in) = _chip_exchange_behind("out", [chip_out], [chip_g1pre], update_in)
    big = {"in": upd_in, "ff1": upd_ff[0], "ff2": upd_ff[1]}
    ws2 = lambda a: a.reshape(N_HEADS * SGU_BLOCK, SGU_BLOCK)
    small_params = [(row(pool_scale), row(m_pool_scale), row(v_pool_scale)),
                    (row(sgu_ln_g), row(m_sgu_ln_g), row(v_sgu_ln_g)),
                    (row(sgu_ln_b), row(m_sgu_ln_b), row(v_sgu_ln_b)),
                    (row(norm1_post_g), row(m_norm1_post_g), row(v_norm1_post_g)),
                    (row(norm2_pre_g), row(m_norm2_pre_g), row(v_norm2_pre_g)),
                    (row(norm2_post_g), row(m_norm2_post_g), row(v_norm2_post_g)),
                    (row(norm1_pre_g), row(m_norm1_pre_g), row(v_norm1_pre_g)),
                    (row(b_in), row(m_b_in), row(v_b_in)),
                    (ws2(w_spatial), ws2(m_w_spatial), ws2(v_w_spatial)),
                    (b_spatial, m_b_spatial, v_b_spatial),
                    (pg2(w_pool), pg2(m_w_pool), pg2(v_w_pool)),
                    (pg2(w_sgu_proj), pg2(m_w_sgu_proj), pg2(v_w_sgu_proj)),
                    (w_out, m_w_out, v_w_out)]
    small_out = _adamw_small(summed_in[3], [summed_out[1]] + list(summed_in[4:])
                             + [summed_in[1], summed_in[2], summed_out[0]], small_params)
    loss = small_out[0][0, 0]
    small_names = SMALL_ROWS[1:] + ("norm1_pre_g", "b_in", "w_spatial", "b_spatial", "w_pool", "w_sgu_proj", "w_out")
    small = {nm: small_out[1 + 4 * p:5 + 4 * p] for p, nm in enumerate(small_names)}

    shapes = {"norm1_pre_g": norm1_pre_g.shape, "w_in": w_in.shape, "b_in": b_in.shape, "w_pool": w_pool.shape,
              "pool_scale": pool_scale.shape, "sgu_ln_g": sgu_ln_g.shape, "sgu_ln_b": sgu_ln_b.shape,
              "w_spatial": w_spatial.shape, "b_spatial": b_spatial.shape, "w_sgu_proj": w_sgu_proj.shape,
              "w_out": w_out.shape, "norm1_post_g": norm1_post_g.shape, "norm2_pre_g": norm2_pre_g.shape,
              "w_ff1": w_ff1.shape, "w_ff2": w_ff2.shape, "norm2_post_g": norm2_post_g.shape}
    source = {"w_in": big["in"], "w_ff1": big["ff1"], "w_ff2": big["ff2"], **small}
    order = list(shapes)
    outs = [loss, dx.reshape(x.shape)]
    for kind in range(4):
        outs += [source[nm][kind].reshape(shapes[nm]) for nm in order]
    return tuple(outs)
```

```python
import math

import jax
import jax.numpy as jnp
from jax import lax
from jax.experimental import pallas as pl
from jax.experimental.pallas import tpu as pltpu

F32, BF16 = jnp.float32, jnp.bfloat16
MESH = pl.DeviceIdType.MESH

D_MODEL = 1024
D_IN = 5120
D_FF = 4096
N_DEV = 8
N_CHIPS = 4
WINDOWS = (2, 4, 8, 16)
N_GROUPS = 4
GROUP = 256
HALO = 16
SGU_BLOCK = 128
N_HEADS = 4
HEAD = 256
CHUNK = 64
EPS = 1e-6
IN_SHARD = D_IN // N_DEV
FF_SHARD = D_FF // N_DEV
OUT_SHARD = D_MODEL // N_DEV
PG_SHARD = GROUP // N_DEV

ADAM_LR, ADAM_B1, ADAM_B2, ADAM_EPS, ADAM_WD, ADAM_STEP = 0.001, 0.9, 0.999, 1e-08, 0.01, 10

VMEM_LIMIT = 56 * 1024 * 1024
TM = 256
TM_BWD = 256
TM_IN = 512
PROJ_COLS = 512
GELU_C0 = math.sqrt(2.0 / math.pi)
GELU_C1 = 0.044715


def _dot(a, b):
    return jnp.dot(a, b, preferred_element_type=F32)


def _dot_nt(a, b):
    return lax.dot_general(a, b, (((1,), (1,)), ((), ())), preferred_element_type=F32)


def _dot_tn(a, b):
    return lax.dot_general(a, b, (((0,), (0,)), ((), ())), preferred_element_type=F32)


def _gelu(x):
    t = jnp.tanh(GELU_C0 * (x + GELU_C1 * (x * x * x)))
    return 0.5 * x * (1.0 + t), t


def _gelu_grad(x, t):
    return 0.5 * (1.0 + t) + 0.5 * x * (1.0 - t * t) * (GELU_C0 * (1.0 + 3.0 * GELU_C1 * (x * x)))


def _sigmoid(x):
    return 1.0 / (1.0 + jnp.exp(-x))


def _mean(x):
    return jnp.mean(x, axis=-1, keepdims=True)


def _colsum(x):
    return jnp.sum(x, axis=0, keepdims=True)


def _const_spec(shape):
    nd = len(shape)
    return pl.BlockSpec(shape, lambda *_: (0,) * nd, pipeline_mode=pl.Buffered(1))


def _acc_spec(shape):
    nd = len(shape)
    return pl.BlockSpec(shape, lambda *_: (0,) * nd)


def _masked_ws(ws_ref):
    ri = lax.broadcasted_iota(jnp.int32, (SGU_BLOCK, SGU_BLOCK), 0) // CHUNK
    ci = lax.broadcasted_iota(jnp.int32, (SGU_BLOCK, SGU_BLOCK), 1) // CHUNK
    return [jnp.where(ri >= ci, ws_ref[h], 0.0).astype(BF16) for h in range(N_HEADS)]


def _pool_fwd(pbuf, tile_idx, tm):
    pos = lax.broadcasted_iota(jnp.int32, (tm, 1), 0) + tile_idx * tm + 1
    pooled = []
    for g, w in enumerate(WINDOWS):
        e = pbuf[:, g * GROUP:(g + 1) * GROUP]
        s, sh = e, 1
        while sh < w:
            s = s + pltpu.roll(s, sh, 0)
            sh *= 2
        inv = 1.0 / jnp.minimum(pos, w).astype(F32)
        pooled.append(s[HALO:] * inv - e[HALO:])
    return pooled


def _spatial_head(ws_h, vb, bsp_ref, h, nblk):
    return jnp.concatenate(
        [_dot(ws_h, vb[n * SGU_BLOCK:(n + 1) * SGU_BLOCK, h * HEAD:(h + 1) * HEAD]) + bsp_ref[:, h:h + 1]
         for n in range(nblk)], axis=0)


def _prenorm(after, x, g1pre):
    t_len = x.shape[0]
    tm = TM_IN

    def body(after_ref, x_ref, g_ref, xn_ref, xnt_ref):
        del after_ref
        xv = x_ref[...]
        xnb = (xv * lax.rsqrt(_mean(xv * xv) + EPS) * g_ref[...]).astype(BF16)
        xn_ref[...] = xnb
        xnt_ref[...] = xnb.T

    return pl.pallas_call(
        body, name="prenorm", grid=(t_len // tm,),
        in_specs=[_ANY, pl.BlockSpec((tm, D_MODEL), lambda i: (i, 0)), _const_spec((1, D_MODEL))],
        out_specs=[pl.BlockSpec((tm, D_MODEL), lambda i: (i, 0)), pl.BlockSpec((D_MODEL, tm), lambda i: (0, i))],
        out_shape=[jax.ShapeDtypeStruct((t_len, D_MODEL), BF16), jax.ShapeDtypeStruct((D_MODEL, t_len), BF16)],
        compiler_params=pltpu.CompilerParams(dimension_semantics=("arbitrary",)),
    )(after, x, g1pre)


def _fwd_mix(after, xn, x, win_g, b_in, wpool, pool_scale, lng, lnb, ws, bsp_t, wproj, wout, g1post):
    t_len = x.shape[0]
    tm = TM
    nt = t_len // tm

    def body(after_ref, xn_ref, xb_ref, win_ref, bin_ref, wpool_ref, ps_ref, lng_ref, lnb_ref, ws_ref, bsp_ref,
             wproj_ref, wout_ref, g1post_ref, z_ref, y_ref, h1_ref, u_ref, gpu_ref, xhat_ref, gpv_ref, sa_ref,
             sb_ref, zcur, znext, pbuf):
        del after_ref
        s = pl.program_id(0)

        @pl.when(s == 0)
        def _():
            znext[...] = jnp.zeros((tm, D_IN), F32)
            pbuf[...] = jnp.zeros((tm + HALO, D_MODEL), F32)

        zcur[...] = znext[...]
        xnb = xn_ref[...]

        def project(p):
            cols = slice(p * PROJ_COLS, (p + 1) * PROJ_COLS)
            zp = _dot(xnb, win_ref[:, cols]) + bin_ref[:, cols]
            if (p + 1) * PROJ_COLS <= D_MODEL:
                z_ref[:, cols] = zp
            znext[:, cols] = zp

        project(0)
        pbuf[0:HALO, :] = jnp.where(s <= 1, 0.0, pbuf[0:HALO, :])
        pbuf[HALO:, :] = zcur[:, 0:D_MODEL]
        pooled = _pool_fwd(pbuf, jnp.maximum(s - 1, 0), tm)
        pbuf[0:HALO, :] = pbuf[tm:tm + HALO, :]
        a = jnp.concatenate([_dot(pooled[g].astype(BF16), wpool_ref[g]) for g in range(N_GROUPS)], axis=1)
        a = a * ps_ref[...]
        project(1)
        zu = zcur[:, D_MODEL:2 * D_MODEL]
        u, tu = _gelu(zu)
        u_ref[...] = u.astype(BF16)
        gpu_ref[...] = _gelu_grad(zu, tu).astype(BF16)
        project(2)
        zv = zcur[:, 2 * D_MODEL:3 * D_MODEL]
        gv, tv = _gelu(zv)
        xc = gv - _mean(gv)
        rln = lax.rsqrt(_mean(xc * xc) + EPS)
        xhat = xc * rln
        xhat_ref[...] = xhat.astype(BF16)
        gpv_ref[...] = (_gelu_grad(zv, tv) * rln).astype(BF16)
        vb = (xhat * lng_ref[...] + lnb_ref[...]).astype(BF16)
        wsm = _masked_ws(ws_ref)
        bbr = []
        for h in range(N_HEADS):
            project(3 + h)
            sv = _spatial_head(wsm[h], vb, bsp_ref, h, tm // SGU_BLOCK)
            bbr.append(_dot((u[:, h * HEAD:(h + 1) * HEAD] * sv).astype(BF16), wproj_ref[h]))
        bbr = jnp.concatenate(bbr, axis=1)
        project(7)
        sa = _sigmoid(zcur[:, 3 * D_MODEL:4 * D_MODEL])
        sb = _sigmoid(zcur[:, 4 * D_MODEL:5 * D_MODEL])
        sa_ref[...] = sa.astype(BF16)
        sb_ref[...] = sb.astype(BF16)
        project(8)
        yv = _dot((sa * a + sb * bbr).astype(BF16), wout_ref[...])
        y_ref[...] = yv
        project(9)
        ry = lax.rsqrt(_mean(yv * yv) + EPS)
        h1_ref[...] = xb_ref[...] + yv * ry * g1post_ref[...]

    proj = lambda w: pl.BlockSpec((tm, w), lambda s: (jnp.minimum(s, nt - 1), 0))
    mix = lambda w: pl.BlockSpec((tm, w), lambda s: (jnp.maximum(s - 1, 0), 0))
    return pl.pallas_call(
        body, name="fwd_mix", grid=(nt + 1,),
        in_specs=[_ANY, proj(D_MODEL), mix(D_MODEL),
                  _const_spec((D_MODEL, D_IN)),
                  _const_spec((1, D_IN)), _const_spec((N_GROUPS, GROUP, GROUP)), _const_spec((1, D_MODEL)),
                  _const_spec((1, D_MODEL)), _const_spec((1, D_MODEL)),
                  _const_spec((N_HEADS, SGU_BLOCK, SGU_BLOCK)), _const_spec((SGU_BLOCK, N_HEADS)),
                  _const_spec((N_HEADS, HEAD, HEAD)), _const_spec((D_MODEL, D_MODEL)), _const_spec((1, D_MODEL))],
        out_specs=[proj(D_MODEL), mix(D_MODEL), mix(D_MODEL)] + [mix(D_MODEL)] * 6,
        out_shape=[jax.ShapeDtypeStruct((t_len, D_MODEL), F32),
                   jax.ShapeDtypeStruct((t_len, D_MODEL), F32), jax.ShapeDtypeStruct((t_len, D_MODEL), F32)]
        + [jax.ShapeDtypeStruct((t_len, D_MODEL), BF16)] * 6,
        scratch_shapes=[pltpu.VMEM((tm, D_IN), F32), pltpu.VMEM((tm, D_IN), F32),
                        pltpu.VMEM((tm + HALO, D_MODEL), F32)],
        compiler_params=pltpu.CompilerParams(dimension_semantics=("arbitrary",), vmem_limit_bytes=VMEM_LIMIT),
    )(after, xn, x, win_g, b_in, wpool, pool_scale, lng, lnb, ws, bsp_t, wproj, wout, g1post)


def _mlp(h1, target, g2pre, g2post, w1_g, w2):
    t_len = h1.shape[0]
    tm = TM
    nt = t_len // tm

    def body(h1_ref, tgt_ref, g2pre_ref, g2post_ref, w1_ref, w2_ref,
             hn_ref, f_ref, df1_ref, df2_ref, dh1_ref, dg2post_ref, dg2pre_ref, loss_ref, f1_scr):
        i = pl.program_id(0)

        @pl.when(i == 0)
        def _():
            dg2post_ref[...] = jnp.zeros_like(dg2post_ref)
            dg2pre_ref[...] = jnp.zeros_like(dg2pre_ref)
            loss_ref[...] = jnp.zeros_like(loss_ref)

        h = h1_ref[...]
        r2 = lax.rsqrt(_mean(h * h) + EPS)
        nh = h * r2
        hnb = (nh * g2pre_ref[...]).astype(BF16)
        hn_ref[...] = hnb.T
        for k in range(N_DEV):
            f1_scr[:, k * FF_SHARD:(k + 1) * FF_SHARD] = _dot(hnb, w1_ref[:, k * FF_SHARD:(k + 1) * FF_SHARD])
        r = jnp.maximum(f1_scr[...], 0.0)
        fb = (r * r).astype(BF16)
        f_ref[...] = fb.T
        f2 = _dot(fb, w2_ref[...])
        rf = lax.rsqrt(_mean(f2 * f2) + EPS)
        nf = f2 * rf
        diff = h + nf * g2post_ref[...] - tgt_ref[...]
        loss_ref[...] += (0.5 / D_MODEL) * jnp.sum(diff * diff)
        dout = diff * (1.0 / D_MODEL)
        dg2post_ref[...] += _colsum(dout * nf)
        dn = dout * g2post_ref[...]
        df2b = (rf * (dn - nf * _mean(dn * nf))).astype(BF16)
        df2_ref[...] = df2b
        df = _dot_nt(df2b, w2_ref[...])
        df1b = (df * (2.0 * jnp.maximum(f1_scr[...], 0.0))).astype(BF16)
        df1_ref[...] = df1b
        dhn = _dot_nt(df1b, w1_ref[...])
        dg2pre_ref[...] += _colsum(dhn * nh)
        dnh = dhn * g2pre_ref[...]
        dh1_ref[...] = dout + r2 * (dnh - nh * _mean(dnh * nh))

    tok = lambda w: pl.BlockSpec((tm, w), lambda i: (i, 0))
    return pl.pallas_call(
        body, name="mlp_fwd_bwd", grid=(nt,),
        in_specs=[tok(D_MODEL), tok(D_MODEL), _const_spec((1, D_MODEL)), _const_spec((1, D_MODEL)),
                  _const_spec((D_MODEL, D_FF)), _const_spec((D_FF, D_MODEL))],
        out_specs=[pl.BlockSpec((D_MODEL, tm), lambda i: (0, i)), pl.BlockSpec((D_FF, tm), lambda i: (0, i)),
                   tok(D_FF), tok(D_MODEL), tok(D_MODEL),
                   _acc_spec((1, D_MODEL)), _acc_spec((1, D_MODEL)), _acc_spec((1, 128))],
        out_shape=[jax.ShapeDtypeStruct((D_MODEL, t_len), BF16), jax.ShapeDtypeStruct((D_FF, t_len), BF16),
                   jax.ShapeDtypeStruct((t_len, D_FF), BF16), jax.ShapeDtypeStruct((t_len, D_MODEL), BF16),
                   jax.ShapeDtypeStruct((t_len, D_MODEL), F32), jax.ShapeDtypeStruct((1, D_MODEL), F32),
                   jax.ShapeDtypeStruct((1, D_MODEL), F32), jax.ShapeDtypeStruct((1, 128), F32)],
        scratch_shapes=[pltpu.VMEM((tm, D_FF), F32)],
        compiler_params=pltpu.CompilerParams(dimension_semantics=("arbitrary",), vmem_limit_bytes=VMEM_LIMIT),
    )(h1, target, g2pre, g2post, w1_g, w2)


def _bwd_mix(after, dh1, y, z, saved, wpool, pool_scale, lng, lnb, ws, bsp_t, wproj, wout, g1post):
    t_len = y.shape[0]
    tm = TM_BWD
    nt = t_len // tm
    nblk = tm // SGU_BLOCK

    def body(after_ref, dh1_ref, y_ref, z_ref, zh_ref, u_ref, gpu_ref, xhat_ref, gpv_ref, sa_ref, sb_ref,
             wpool_ref, ps_ref, lng_ref, lnb_ref, ws_ref,
             bsp_ref, wproj_ref, wout_ref, g1post_ref,
             dz_ref, mg_ref, dy_ref, dwpool_ref, dwproj_ref, dws_ref, dbsp_ref, dg1post_ref, dps_ref,
             dlng_ref, dlnb_ref, dbin_ref, pbuf, qbuf):
        del after_ref
        i = pl.program_id(0)
        ti = nt - 1 - i

        @pl.when(i == 0)
        def _():
            for ref in (dwpool_ref, dwproj_ref, dws_ref, dbsp_ref, dg1post_ref, dps_ref, dlng_ref, dlnb_ref,
                        dbin_ref):
                ref[...] = jnp.zeros_like(ref)
            qbuf[tm:tm + HALO, :] = jnp.zeros((HALO, D_MODEL), F32)

        pbuf[0:HALO, :] = jnp.where(ti > 0, zh_ref[...], 0.0)
        pbuf[HALO:, :] = z_ref[...]
        pooled = _pool_fwd(pbuf, ti, tm)
        pooled_b = [p.astype(BF16) for p in pooled]
        a_raw = jnp.concatenate([_dot(pooled_b[g], wpool_ref[g]) for g in range(N_GROUPS)], axis=1)
        wsm = _masked_ws(ws_ref)
        u = u_ref[...].astype(F32)
        xhat = xhat_ref[...].astype(F32)
        vb = (xhat * lng_ref[...] + lnb_ref[...]).astype(BF16)
        sv_heads = [_spatial_head(wsm[h], vb, bsp_ref, h, nblk) for h in range(N_HEADS)]
        gated_b = [(u[:, h * HEAD:(h + 1) * HEAD] * sv_heads[h]).astype(BF16) for h in range(N_HEADS)]
        bbr = jnp.concatenate([_dot(gated_b[h], wproj_ref[h]) for h in range(N_HEADS)], axis=1)
        sa = sa_ref[...].astype(F32)
        sb = sb_ref[...].astype(F32)
        a = a_raw * ps_ref[...]
        mg_ref[...] = (sa * a + sb * bbr).astype(BF16).T

        dh = dh1_ref[...]
        yv = y_ref[...]
        ry = lax.rsqrt(_mean(yv * yv) + EPS)
        ny = yv * ry
        dg1post_ref[...] += _colsum(dh * ny)
        dn = dh * g1post_ref[...]
        dyb = (ry * (dn - ny * _mean(dn * ny))).astype(BF16)
        dy_ref[...] = dyb
        dmg = _dot_nt(dyb, wout_ref[...])

        da = dmg * sa
        dbbr = dmg * sb
        dzga = dmg * a * sa * (1.0 - sa)
        dzgb = dmg * bbr * sb * (1.0 - sb)
        dz_ref[:, 3 * D_MODEL:4 * D_MODEL] = dzga.astype(BF16)
        dz_ref[:, 4 * D_MODEL:5 * D_MODEL] = dzgb.astype(BF16)
        dbin_ref[:, 3 * D_MODEL:4 * D_MODEL] += _colsum(dzga)
        dbin_ref[:, 4 * D_MODEL:5 * D_MODEL] += _colsum(dzgb)

        dps_ref[...] += _colsum(da * a_raw)
        da_raw_b = (da * ps_ref[...]).astype(BF16)
        pos = lax.broadcasted_iota(jnp.int32, (tm, 1), 0) + ti * tm + 1
        dpooled = []
        for g, w in enumerate(WINDOWS):
            cols = slice(g * GROUP, (g + 1) * GROUP)
            dwpool_ref[g] += _dot_tn(pooled_b[g], da_raw_b[:, cols])
            dp = _dot_nt(da_raw_b[:, cols], wpool_ref[g])
            dpooled.append(dp)
            qbuf[0:tm, cols] = dp * (1.0 / jnp.minimum(pos, w).astype(F32))
        n_ext = tm + HALO
        dzp = []
        for g, w in enumerate(WINDOWS):
            e = qbuf[:, g * GROUP:(g + 1) * GROUP]
            s, sh = e, 1
            while sh < w:
                s = s + pltpu.roll(s, n_ext - sh, 0)
                sh *= 2
            dzp.append(s[0:tm] - dpooled[g])
        qbuf[tm:tm + HALO, :] = qbuf[0:HALO, :]
        dzp = jnp.concatenate(dzp, axis=1)
        dz_ref[:, 0:D_MODEL] = dzp.astype(BF16)
        dbin_ref[:, 0:D_MODEL] += _colsum(dzp)

        dv_heads = []
        du_heads = []
        for h in range(N_HEADS):
            cols = slice(h * HEAD, (h + 1) * HEAD)
            dbbr_b = dbbr[:, cols].astype(BF16)
            dwproj_ref[h] += _dot_tn(gated_b[h], dbbr_b)
            dgated = _dot_nt(dbbr_b, wproj_ref[h])
            du_heads.append(dgated * sv_heads[h])
            dsv = dgated * u[:, cols]
            dsv_b = dsv.astype(BF16)
            rows = []
            for n in range(nblk):
                blk = slice(n * SGU_BLOCK, (n + 1) * SGU_BLOCK)
                rows.append(_dot_tn(wsm[h], dsv_b[blk]))
                dws_ref[h] += _dot_nt(dsv_b[blk], vb[blk, cols])
                dbsp_ref[:, h:h + 1] += jnp.sum(dsv[blk], axis=1, keepdims=True)
            dv_heads.append(jnp.concatenate(rows, axis=0))
        dzu = jnp.concatenate(du_heads, axis=1) * gpu_ref[...].astype(F32)
        dz_ref[:, D_MODEL:2 * D_MODEL] = dzu.astype(BF16)
        dbin_ref[:, D_MODEL:2 * D_MODEL] += _colsum(dzu)
        dv = jnp.concatenate(dv_heads, axis=1)
        dlng_ref[...] += _colsum(dv * xhat)
        dlnb_ref[...] += _colsum(dv)
        dxh = dv * lng_ref[...]
        dzv = (dxh - _mean(dxh) - xhat * _mean(dxh * xhat)) * gpv_ref[...].astype(F32)
        dz_ref[:, 2 * D_MODEL:3 * D_MODEL] = dzv.astype(BF16)
        dbin_ref[:, 2 * D_MODEL:3 * D_MODEL] += _colsum(dzv)

        @pl.when(i == nt - 1)
        def _():
            ri = lax.broadcasted_iota(jnp.int32, (SGU_BLOCK, SGU_BLOCK), 0) // CHUNK
            ci = lax.broadcasted_iota(jnp.int32, (SGU_BLOCK, SGU_BLOCK), 1) // CHUNK
            for h in range(N_HEADS):
                dws_ref[h] = jnp.where(ri >= ci, dws_ref[h], 0.0)

    tok = lambda w: pl.BlockSpec((tm, w), lambda i: (nt - 1 - i, 0))
    halo = pl.BlockSpec((HALO, D_MODEL), lambda i: (jnp.maximum((nt - 1 - i) * (tm // HALO) - 1, 0), 0))
    return pl.pallas_call(
        body, name="bwd_mix", grid=(nt,),
        in_specs=[_ANY, tok(D_MODEL), tok(D_MODEL), tok(D_MODEL), halo] + [tok(D_MODEL)] * 6
        + [_const_spec((N_GROUPS, GROUP, GROUP)),
                  _const_spec((1, D_MODEL)), _const_spec((1, D_MODEL)), _const_spec((1, D_MODEL)),
                  _const_spec((N_HEADS, SGU_BLOCK, SGU_BLOCK)), _const_spec((SGU_BLOCK, N_HEADS)),
                  _const_spec((N_HEADS, HEAD, HEAD)), _const_spec((D_MODEL, D_MODEL)), _const_spec((1, D_MODEL))],
        out_specs=[tok(D_IN), pl.BlockSpec((D_MODEL, tm), lambda i: (0, nt - 1 - i)), tok(D_MODEL),
                   _acc_spec((N_GROUPS, GROUP, GROUP)), _acc_spec((N_HEADS, HEAD, HEAD)),
                   _acc_spec((N_HEADS, SGU_BLOCK, SGU_BLOCK)), _acc_spec((SGU_BLOCK, N_HEADS)),
                   _acc_spec((1, D_MODEL)), _acc_spec((1, D_MODEL)), _acc_spec((1, D_MODEL)), _acc_spec((1, D_MODEL)),
                   _acc_spec((1, D_IN))],
        out_shape=[jax.ShapeDtypeStruct((t_len, D_IN), BF16),
                   jax.ShapeDtypeStruct((D_MODEL, t_len), BF16), jax.ShapeDtypeStruct((t_len, D_MODEL), BF16),
                   jax.ShapeDtypeStruct((N_GROUPS, GROUP, GROUP), F32), jax.ShapeDtypeStruct((N_HEADS, HEAD, HEAD), F32),
                   jax.ShapeDtypeStruct((N_HEADS, SGU_BLOCK, SGU_BLOCK), F32),
                   jax.ShapeDtypeStruct((SGU_BLOCK, N_HEADS), F32),
                   jax.ShapeDtypeStruct((1, D_MODEL), F32), jax.ShapeDtypeStruct((1, D_MODEL), F32),
                   jax.ShapeDtypeStruct((1, D_MODEL), F32), jax.ShapeDtypeStruct((1, D_MODEL), F32),
                   jax.ShapeDtypeStruct((1, D_IN), F32)],
        scratch_shapes=[pltpu.VMEM((tm + HALO, D_MODEL), F32), pltpu.VMEM((tm + HALO, D_MODEL), F32)],
        compiler_params=pltpu.CompilerParams(dimension_semantics=("arbitrary",), vmem_limit_bytes=VMEM_LIMIT),
    )(after, dh1, y, z, z, *saved, wpool, pool_scale, lng, lnb, ws, bsp_t, wproj, wout, g1post)


def _bwd_in(after, dz_b, x, dh1, g1pre, win_g):
    t_len = x.shape[0]
    tm = TM_IN
    nt = t_len // tm

    def body(after_ref, dz_ref, x_ref, dh1_ref, g1_ref, win_ref, dx_ref, dg1pre_ref):
        del after_ref

        @pl.when(pl.program_id(0) == 0)
        def _():
            dg1pre_ref[...] = jnp.zeros_like(dg1pre_ref)

        dxn = _dot_nt(dz_ref[...], win_ref[...])
        xv = x_ref[...]
        r1 = lax.rsqrt(_mean(xv * xv) + EPS)
        nx = xv * r1
        dg1pre_ref[...] += _colsum(dxn * nx)
        dnx = dxn * g1_ref[...]
        dx_ref[...] = r1 * (dnx - nx * _mean(dnx * nx)) + dh1_ref[...]

    tok = lambda w: pl.BlockSpec((tm, w), lambda i: (i, 0))
    return pl.pallas_call(
        body, name="bwd_in", grid=(nt,),
        in_specs=[_ANY, tok(D_IN), tok(D_MODEL), tok(D_MODEL), _const_spec((1, D_MODEL)),
                  _const_spec((D_MODEL, D_IN))],
        out_specs=[tok(D_MODEL), _acc_spec((1, D_MODEL))],
        out_shape=[jax.ShapeDtypeStruct((t_len, D_MODEL), F32), jax.ShapeDtypeStruct((1, D_MODEL), F32)],
        compiler_params=pltpu.CompilerParams(dimension_semantics=("arbitrary",), vmem_limit_bytes=VMEM_LIMIT),
    )(after, dz_b, x, dh1, g1pre, win_g)


def _owner_of_slot(s):
    return 4 * ((s // 2) % 2) + 2 * (s % 2) + s // 4


def _wgrad(name, a_t, b, shard, a_sliced):
    t_len = b.shape[0]
    am = shard if a_sliced else a_t.shape[0]
    bn = b.shape[1] if a_sliced else shard

    def body(a_ref, b_ref, o_ref):
        o_ref[...] = _dot(a_ref[...], b_ref[...]).astype(BF16)

    if a_sliced:
        in_specs = [pl.BlockSpec((am, t_len), lambda s: (_owner_of_slot(s), 0)),
                    pl.BlockSpec((t_len, bn), lambda s: (0, 0), pipeline_mode=pl.Buffered(1))]
    else:
        in_specs = [pl.BlockSpec((am, t_len), lambda s: (0, 0), pipeline_mode=pl.Buffered(1)),
                    pl.BlockSpec((t_len, bn), lambda s: (0, _owner_of_slot(s)))]
    return pl.pallas_call(
        body, name=name, grid=(N_DEV,), in_specs=in_specs,
        out_specs=pl.BlockSpec((None, am, bn), lambda s: (s, 0, 0)),
        out_shape=jax.ShapeDtypeStruct((N_DEV, am, bn), BF16),
        compiler_params=pltpu.CompilerParams(dimension_semantics=("arbitrary",), vmem_limit_bytes=VMEM_LIMIT),
    )(a_t, b)


def _wgrad_in(xn_t, dz_b):
    t_len = dz_b.shape[0]

    def body(a_ref, b_ref, o_ref):
        res = _dot(a_ref[...], b_ref[...])
        o_ref[0] = res[:, 0:IN_SHARD].astype(BF16)
        o_ref[1] = res[:, IN_SHARD:2 * IN_SHARD].astype(BF16)

    out = pl.pallas_call(
        body, name="wgrad_in", grid=(N_CHIPS,),
        in_specs=[pl.BlockSpec((D_MODEL, t_len), lambda q: (0, 0), pipeline_mode=pl.Buffered(1)),
                  pl.BlockSpec((t_len, 2 * IN_SHARD), lambda q: (0, q))],
        out_specs=pl.BlockSpec((2, None, D_MODEL, IN_SHARD), lambda q: (0, q, 0, 0)),
        out_shape=jax.ShapeDtypeStruct((2, N_CHIPS, D_MODEL, IN_SHARD), BF16),
        compiler_params=pltpu.CompilerParams(dimension_semantics=("arbitrary",), vmem_limit_bytes=VMEM_LIMIT),
    )(xn_t, dz_b)
    return out.reshape(N_DEV, D_MODEL, IN_SHARD)


def _wgrad_out(mg_t, dy_b):
    t_len = dy_b.shape[0]

    def body(a_ref, b_ref, o_ref):
        res = _dot(a_ref[...], b_ref[...])
        o_ref[0] = res[0:OUT_SHARD].astype(BF16)
        o_ref[1] = res[OUT_SHARD:2 * OUT_SHARD].astype(BF16)

    out = pl.pallas_call(
        body, name="wgrad_out", grid=(N_CHIPS,),
        in_specs=[pl.BlockSpec((2 * OUT_SHARD, t_len), lambda q: (q, 0)),
                  pl.BlockSpec((t_len, D_MODEL), lambda q: (0, 0), pipeline_mode=pl.Buffered(1))],
        out_specs=pl.BlockSpec((2, None, OUT_SHARD, D_MODEL), lambda q: (0, q, 0, 0)),
        out_shape=jax.ShapeDtypeStruct((2, N_CHIPS, OUT_SHARD, D_MODEL), BF16),
        compiler_params=pltpu.CompilerParams(dimension_semantics=("arbitrary",), vmem_limit_bytes=VMEM_LIMIT),
    )(mg_t, dy_b)
    return out.reshape(N_DEV, OUT_SHARD, D_MODEL)


def _coords():
    return lax.axis_index("x"), lax.axis_index("y"), lax.axis_index("c")


_ANY = pl.BlockSpec(memory_space=pl.ANY)


def _pair_exchange(name, bigs, smalls):
    nb, n = len(bigs), len(bigs) + len(smalls)

    def body(*refs):
        src, dst = refs[:n], refs[n:2 * n]
        send_sems, recv_sems = refs[2 * n:]
        x, y, c = _coords()
        copies = []
        for j in range(n):
            s = src[j].at[pl.ds(4 * (1 - c), 4)] if j < nb else src[j]
            cp = pltpu.make_async_remote_copy(src_ref=s, dst_ref=dst[j], send_sem=send_sems.at[j],
                                              recv_sem=recv_sems.at[j], device_id=(x, y, 1 - c), device_id_type=MESH)
            cp.start()
            copies.append(cp)
        for cp in copies:
            cp.wait()

    return pl.pallas_call(
        body, name=name,
        in_specs=[_ANY] * n, out_specs=[_ANY] * n,
        out_shape=[jax.ShapeDtypeStruct((4,) + b.shape[1:], b.dtype) for b in bigs]
        + [jax.ShapeDtypeStruct(s.shape, s.dtype) for s in smalls],
        scratch_shapes=[pltpu.SemaphoreType.DMA((n,)), pltpu.SemaphoreType.DMA((n,))],
    )(*bigs, *smalls)


_HBM = pl.BlockSpec(memory_space=pltpu.HBM)
_SEM = pl.BlockSpec(memory_space=pltpu.SEMAPHORE)
_VMEM = pl.BlockSpec(memory_space=pltpu.VMEM)
_EFFECT = pltpu.SideEffectType.DATAFLOW_SIDE_EFFECTING
_TOKEN = jax.ShapeDtypeStruct((8, 128), F32)


def _in_hbm(a):
    return pltpu.with_memory_space_constraint(a, pltpu.HBM)


def _split_call(name, body, n_sems_out, arrays, sems_in=(), after=None):
    na, ns = len(arrays), len(sems_in)
    has_after = after is not None

    def kernel_body(*refs):
        arr = refs[:na]
        s_in = refs[na:na + ns]
        outs = refs[na + ns + has_after:]
        body(arr, s_in, outs[:n_sems_out])
        outs[-1][...] = jnp.zeros((8, 128), F32)

    out_shape = ([pltpu.SemaphoreType.DMA(())] * n_sems_out + [pltpu.HBM(a.shape, a.dtype) for a in arrays] + [_TOKEN])
    res = pl.pallas_call(
        kernel_body, name=name, out_shape=out_shape,
        in_specs=[_HBM] * na + [_SEM] * ns + [_ANY] * has_after,
        out_specs=[_SEM] * n_sems_out + [_HBM] * na + [_VMEM],
        input_output_aliases={i: n_sems_out + i for i in range(na)},
        compiler_params=pltpu.CompilerParams(has_side_effects=_EFFECT),
    )(*[_in_hbm(a) for a in arrays], *sems_in, *([after] if has_after else []))
    return list(res[:n_sems_out]), list(res[n_sems_out:n_sems_out + na]), res[-1]


def _wait_bytes_of(ref, send_sem, recv_sem, peer, send=True, recv=True):
    cp = pltpu.make_async_remote_copy(src_ref=ref, dst_ref=ref, send_sem=send_sem, recv_sem=recv_sem,
                                      device_id=peer, device_id_type=MESH)
    if send:
        cp.wait_send()
    if recv:
        cp.wait_recv()


def _gather_behind(tag, shards, after, work, by_columns=()):
    n = len(shards)
    lands = [lax.empty((s.shape[0], N_DEV * s.shape[1]) if j in by_columns else (N_DEV,) + s.shape, s.dtype)
             for j, s in enumerate(shards)]

    def slots(arr, j, first, count=1):
        if j in by_columns:
            cols = shards[j].shape[1]
            return arr[n + j].at[:, pl.ds(pl.multiple_of(first * cols, 128), count * cols)]
        return arr[n + j].at[first] if count == 1 else arr[n + j].at[pl.ds(first, count)]

    def own_slot(arr, j, sem):
        x, y, c = _coords()
        return pltpu.make_async_copy(arr[j], slots(arr, j, 4 * x + 2 * y + c), sem)

    def start(arr, _, sems):
        x, y, c = _coords()
        me = 4 * x + 2 * y + c
        for j in range(n):
            for chip in [(1 - x, y), (x, 1 - y), (1 - x, 1 - y)]:
                pltpu.make_async_remote_copy(src_ref=arr[j], dst_ref=slots(arr, j, me), send_sem=sems[j],
                                             recv_sem=sems[n + j], device_id=(*chip, c), device_id_type=MESH).start()
        for j in range(n):
            pltpu.make_async_remote_copy(src_ref=arr[j], dst_ref=slots(arr, j, me), send_sem=sems[2 * n + j],
                                         recv_sem=sems[3 * n + j], device_id=(x, y, 1 - c),
                                         device_id_type=MESH).start()
            own_slot(arr, j, sems[4 * n + j]).start()

    def middle(arr, s_in, sems):
        x, y, c = _coords()
        sibling = (x, y, 1 - c)
        for j in range(n):
            _wait_bytes_of(slots(arr, j, 0, 3), s_in[j], s_in[n + j], sibling)
            for chip in [(1 - x, y), (x, 1 - y), (1 - x, 1 - y)]:
                slot = slots(arr, j, 4 * chip[0] + 2 * chip[1] + c)
                pltpu.make_async_remote_copy(src_ref=slot, dst_ref=slot, send_sem=sems[j], recv_sem=sems[n + j],
                                             device_id=sibling, device_id_type=MESH).start()

    def finish(arr, s_in, _):
        x, y, c = _coords()
        sibling = (x, y, 1 - c)
        for j in range(n):
            _wait_bytes_of(slots(arr, j, 0, 1), s_in[j], s_in[n + j], sibling)
            own_slot(arr, j, s_in[2 * n + j]).wait()
            _wait_bytes_of(slots(arr, j, 0, 3), s_in[3 * n + j], s_in[4 * n + j], sibling)

    sems, arrays, token = _split_call("gather_%s_start" % tag, start, 5 * n, list(shards) + lands, after=after)
    result = work(token)
    fwd_sems, arrays, token = _split_call("gather_%s_middle" % tag, middle, 2 * n, arrays, sems_in=sems[:2 * n],
                                          after=result[0])
    _, arrays, _ = _split_call("gather_%s_finish" % tag, finish, 0, arrays, sems_in=sems[2 * n:] + fwd_sems,
                               after=token)
    return arrays[n:], result


def _chip_exchange_behind(tag, bigs, smalls, work):
    nb, n = len(bigs), len(bigs) + len(smalls)
    lands = [lax.empty(s.shape, s.dtype) for s in bigs] + [lax.empty((N_CHIPS,) + s.shape, s.dtype) for s in smalls]

    def own_slot(arr, j, sem):
        x, y, _ = _coords()
        q_me = 2 * x + y
        return pltpu.make_async_copy(arr[j].at[q_me] if j < nb else arr[j], arr[n + j].at[q_me], sem)

    def start(arr, _, sems):
        x, y, c = _coords()
        q_me = 2 * x + y
        for j in range(n):
            for peer in [(1 - x, y, c), (x, 1 - y, c), (1 - x, 1 - y, c)]:
                piece = arr[j].at[2 * peer[0] + peer[1]] if j < nb else arr[j]
                pltpu.make_async_remote_copy(src_ref=piece, dst_ref=arr[n + j].at[q_me],
                                             send_sem=sems[j], recv_sem=sems[n + j], device_id=peer,
                                             device_id_type=MESH).start()
            own_slot(arr, j, sems[2 * n + j]).start()

    def finish(arr, s_in, _):
        x, y, c = _coords()
        for j in range(n):
            _wait_bytes_of(arr[n + j].at[pl.ds(0, 3)], s_in[j], s_in[n + j], (x, y, 1 - c))
            own_slot(arr, j, s_in[2 * n + j]).wait()

    sems, arrays, token = _split_call("chip_exchange_%s_start" % tag, start, 3 * n, list(bigs) + list(smalls) + lands)
    result = work(token)
    _, arrays, _ = _split_call("chip_exchange_%s_finish" % tag, finish, 0, arrays, sems_in=sems, after=result[0])
    return arrays[n:], result


def _pair_sum_big(name, c_arr, mine, theirs):
    n = len(mine)

    def body(c_ref, *refs):
        del c_ref
        for j in range(n):
            refs[2 * n + j][...] = (refs[j][...].astype(F32) + refs[n + j][...].astype(F32)).astype(BF16)

    own = lambda t: pl.BlockSpec((None,) + t.shape[1:], lambda q, c_ref: (4 * c_ref[0] + q, 0, 0))
    slot = lambda t: pl.BlockSpec((None,) + t.shape[1:], lambda q, c_ref: (q, 0, 0))
    return pl.pallas_call(
        body, name=name,
        grid_spec=pltpu.PrefetchScalarGridSpec(
            num_scalar_prefetch=1, grid=(N_CHIPS,),
            in_specs=[own(t) for t in theirs] + [slot(t) for t in theirs],
            out_specs=[slot(t) for t in theirs]),
        out_shape=[jax.ShapeDtypeStruct(t.shape, BF16) for t in theirs],
        compiler_params=pltpu.CompilerParams(dimension_semantics=("arbitrary",)),
    )(c_arr, *mine, *theirs)


def _pair_sum_small(name, c_arr, slotted, slotted_theirs, mine, theirs):
    ns, n = len(slotted), len(mine)

    def body(c_ref, *refs):
        first = 4 * c_ref[0]
        outs = refs[2 * (ns + n):]
        for j in range(ns):
            own = refs[j][pl.ds(first, N_CHIPS)].astype(F32)
            outs[j][...] = (own + refs[ns + j][...].astype(F32)).astype(BF16)
        for j in range(n):
            outs[ns + j][...] = refs[2 * ns + j][...] + refs[2 * ns + n + j][...]

    return pl.pallas_call(
        body, name=name,
        in_specs=[pl.BlockSpec(memory_space=pltpu.SMEM)] + [_VMEM] * (2 * (ns + n)),
        out_shape=[jax.ShapeDtypeStruct(t.shape, BF16) for t in slotted_theirs]
        + [jax.ShapeDtypeStruct(m.shape, m.dtype) for m in mine],
    )(c_arr, *slotted, *slotted_theirs, *mine, *theirs)


def _adamw_math(w, g, m, v):
    m = ADAM_B1 * m + (1.0 - ADAM_B1) * g
    v = ADAM_B2 * v + (1.0 - ADAM_B2) * (g * g)
    m_hat = m / (1.0 - ADAM_B1 ** ADAM_STEP)
    v_hat = v / (1.0 - ADAM_B2 ** ADAM_STEP)
    delta = -ADAM_LR * (m_hat / (jnp.sqrt(v_hat) + ADAM_EPS) + ADAM_WD * w)
    return delta, m, v


ADAMW_STEPS = 2


def _adamw_big(name, after, chip_sums, params):
    n = len(params)

    def body(after_ref, *refs):
        del after_ref
        outs = refs[4 * n:]
        for j in range(n):
            t_ref, w_ref, m_ref, v_ref = refs[4 * j:4 * j + 4]
            g = t_ref[0].astype(F32)
            for q in range(1, N_CHIPS):
                g = g + t_ref[q].astype(F32)
            d, mn, vn = _adamw_math(w_ref[...], g, m_ref[...], v_ref[...])
            for out, val in zip(outs[4 * j:4 * j + 4], (g, d, mn, vn)):
                out[...] = val

    in_specs, out_specs, out_shape, operands = [_ANY], [], [], [after]
    for t, (w, m, v) in zip(chip_sums, params):
        rows, cols = w.shape
        tr = rows // ADAMW_STEPS
        blk = pl.BlockSpec((tr, cols), lambda r: (r, 0))
        in_specs += [pl.BlockSpec((N_CHIPS, tr, cols), lambda r: (0, r, 0)), blk, blk, blk]
        out_specs += [blk] * 4
        out_shape += [jax.ShapeDtypeStruct((rows, cols), F32)] * 4
        operands += [t, w, m, v]
    res = pl.pallas_call(
        body, name=name, grid=(ADAMW_STEPS,), in_specs=in_specs, out_specs=out_specs, out_shape=out_shape,
        compiler_params=pltpu.CompilerParams(dimension_semantics=("arbitrary",), vmem_limit_bytes=VMEM_LIMIT),
    )(*operands)
    return [res[4 * j:4 * j + 4] for j in range(n)]


SMALL_ROWS = ("loss", "pool_scale", "sgu_ln_g", "sgu_ln_b", "norm1_post_g", "norm2_pre_g", "norm2_post_g")


N_ROW_PARAMS = len(SMALL_ROWS) - 1


def _adamw_small(u_rows, u_others, params):
    n, n_oth = len(params), len(u_others)

    def body(*refs):
        urow_ref, others = refs[0], refs[1:1 + n_oth]
        wmv = refs[1 + n_oth:1 + n_oth + 3 * n]
        loss_ref = refs[1 + n_oth + 3 * n]
        outs = refs[2 + n_oth + 3 * n:]

        def total(ref, idx):
            g = ref[(0,) + idx].astype(F32)
            for q in range(1, N_CHIPS):
                g = g + ref[(q,) + idx].astype(F32)
            return g

        loss_ref[...] = total(urow_ref, (slice(0, 1), slice(None)))
        for p in range(n):
            if p < N_ROW_PARAMS:
                g = total(urow_ref, (slice(p + 1, p + 2), slice(None)))
            else:
                g = total(others[p - N_ROW_PARAMS], (slice(None), slice(None)))
            d, mn, vn = _adamw_math(wmv[3 * p][...], g, wmv[3 * p + 1][...], wmv[3 * p + 2][...])
            outs[4 * p][...] = g
            outs[4 * p + 1][...] = d
            outs[4 * p + 2][...] = mn
            outs[4 * p + 3][...] = vn

    flat = [a for p in params for a in p]
    out_shape = [jax.ShapeDtypeStruct((1, D_MODEL), F32)]
    for w, _, _ in params:
        out_shape += [jax.ShapeDtypeStruct(w.shape, F32)] * 4
    return pl.pallas_call(body, name="adamw_small", out_shape=out_shape)(u_rows, *u_others, *flat)


def _to_bf16(arrays):
    n = len(arrays)

    def body(*refs):
        for j in range(n):
            refs[n + j][...] = refs[j][...].astype(BF16)

    return pl.pallas_call(
        body, name="shards_to_bf16", out_shape=[jax.ShapeDtypeStruct(a.shape, BF16) for a in arrays],
    )(*arrays)


def _slots_of_rows(full):
    owners = [_owner_of_slot(s) for s in range(N_DEV)]
    parts = full.reshape(N_GROUPS, N_DEV, PG_SHARD, GROUP)
    return jnp.stack([parts[:, o] for o in owners]).reshape(N_DEV, N_GROUPS * PG_SHARD, GROUP).astype(BF16)


def kernel(x, norm1_pre_g, w_in, b_in, w_pool, pool_scale, sgu_ln_g, sgu_ln_b, w_spatial, b_spatial, w_sgu_proj, w_out, norm1_post_g, norm2_pre_g, w_ff1, w_ff2, norm2_post_g, loss_target, m_norm1_pre_g, m_w_in, m_b_in, m_w_pool, m_pool_scale, m_sgu_ln_g, m_sgu_ln_b, m_w_spatial, m_b_spatial, m_w_sgu_proj, m_w_out, m_norm1_post_g, m_norm2_pre_g, m_w_ff1, m_w_ff2, m_norm2_post_g, v_norm1_pre_g, v_w_in, v_b_in, v_w_pool, v_pool_scale, v_sgu_ln_g, v_sgu_ln_b, v_w_spatial, v_b_spatial, v_w_sgu_proj, v_w_out, v_norm1_post_g, v_norm2_pre_g, v_w_ff1, v_w_ff2, v_norm2_post_g):
    t_len = x.shape[1]
    row = lambda a: a.reshape(1, -1)
    x2 = x.reshape(t_len, D_MODEL)
    tgt2 = loss_target.reshape(t_len, D_MODEL)
    pg2 = lambda a: a.reshape(N_GROUPS * PG_SHARD, GROUP)

    shards_b = _to_bf16([w_in, pg2(w_pool), pg2(w_sgu_proj), w_out, w_ff1, w_ff2])

    def prework(token):
        return _prenorm(token, x2, row(norm1_pre_g))

    (win_f, g_pool, g_proj, g_out), (xn, xn_b) = _gather_behind("mix", shards_b[:4], None, prework, by_columns=(0,))
    regroup = lambda g: g.reshape(N_DEV, N_GROUPS, PG_SHARD, GROUP).transpose(1, 0, 2, 3).reshape(N_GROUPS, GROUP, GROUP)
    wpool_f, wproj_f = regroup(g_pool), regroup(g_proj)
    wout_f = g_out.reshape(D_MODEL, D_MODEL)
    bsp_t = b_spatial.T

    def forward(token):
        z, y, h1, *saved = _fwd_mix(token, xn, x2, win_f, row(b_in), wpool_f, row(pool_scale),
                                    row(sgu_ln_g), row(sgu_ln_b), w_spatial, bsp_t, wproj_f, wout_f,
                                    row(norm1_post_g))
        return h1, z, y, saved

    (g_ff1, g_ff2), (h1, z, y, saved) = _gather_behind("ff", shards_b[4:], win_f,
                                                       forward, by_columns=(0,))
    w2_f = g_ff2.reshape(D_FF, D_MODEL)
    hn_b, f_b, df1_b, df2_b, dh1, dg2post, dg2pre, loss_p = _mlp(h1, tgt2, row(norm2_pre_g), row(norm2_post_g),
                                                               g_ff1, w2_f)
    p_ff2 = _wgrad("wgrad_ff2", f_b, df2_b, FF_SHARD, True)
    p_ff1 = _wgrad("wgrad_ff1", hn_b, df1_b, FF_SHARD, False)
    c_arr = lax.axis_index("c").astype(jnp.int32).reshape(1)
    ff_parts = [p_ff1, p_ff2]
    got_ff = _pair_exchange("pair_exchange_ff", ff_parts, [])
    chip_ff = _pair_sum_big("pair_sum_ff", c_arr, ff_parts, got_ff)

    def backward_mix(token):
        return _bwd_mix(token, dh1, y, z, saved, wpool_f, row(pool_scale), row(sgu_ln_g), row(sgu_ln_b), w_spatial,
                        bsp_t, wproj_f, wout_f, row(norm1_post_g))

    summed_ff, (dz_b, mg_b, dy_b, dwpool, dwproj, dws, dbsp_t, dg1post, dps, dlng, dlnb,
                dbin) = _chip_exchange_behind("ff", chip_ff, [], backward_mix)

    p_in = _wgrad_in(xn_b, dz_b)
    bigs = [p_in, _slots_of_rows(dwpool), _slots_of_rows(dwproj)]
    rows = jnp.concatenate([jnp.broadcast_to(loss_p[:, 0:1], (1, D_MODEL)), dps, dlng, dlnb, dg1post, dg2pre, dg2post],
                           axis=0)
    smalls = [rows, dbin, dws.reshape(N_HEADS * SGU_BLOCK, SGU_BLOCK), dbsp_t.T]
    got = _pair_exchange("pair_exchange_in", bigs, smalls)
    chip_in = _pair_sum_big("pair_sum_in", c_arr, [p_in], got[:1])
    sums = _pair_sum_small("pair_sum_small", c_arr, bigs[1:], got[1:3], smalls, got[3:])
    chip_bigs, chip_smalls = list(chip_in) + list(sums[:2]), sums[2:]

    def backward_rest(token):
        dx, dg1pre = _bwd_in(token, dz_b, x2, dh1, row(norm1_pre_g), win_f)
        p_out = _wgrad_out(mg_b, dy_b)
        got_out = _pair_exchange("pair_exchange_out", [p_out], [dg1pre])
        chip_out, chip_g1pre = _pair_sum_small("pair_sum_out", c_arr, [p_out], got_out[:1], [dg1pre], got_out[1:])
        upd_ff = _adamw_big("adamw_ff", p_out, summed_ff, [(w_ff1, m_w_ff1, v_w_ff1), (w_ff2, m_w_ff2, v_w_ff2)])
        return upd_ff[1][0], dx, chip_out, chip_g1pre, upd_ff

    summed_in, (_, dx, chip_out, chip_g1pre, upd_ff) = _chip_exchange_behind("in", chip_bigs, chip_smalls,
                                                                             backward_rest)

    def update_in(token):
        res = _adamw_big("adamw_in", token, summed_in[:1], [(w_in, m_w_in, v_w_in)])[0]
        return res[0], res

    summed_out, (_, upd_in) = _chip_exchange_behind("out", [chip_out], [chip_g1pre], update_in)
    big = {"in": upd_in, "ff1": upd_ff[0], "ff2": upd_ff[1]}
    ws2 = lambda a: a.reshape(N_HEADS * SGU_BLOCK, SGU_BLOCK)
    small_params = [(row(pool_scale), row(m_pool_scale), row(v_pool_scale)),
                    (row(sgu_ln_g), row(m_sgu_ln_g), row(v_sgu_ln_g)),
                    (row(sgu_ln_b), row(m_sgu_ln_b), row(v_sgu_ln_b)),
                    (row(norm1_post_g), row(m_norm1_post_g), row(v_norm1_post_g)),
                    (row(norm2_pre_g), row(m_norm2_pre_g), row(v_norm2_pre_g)),
                    (row(norm2_post_g), row(m_norm2_post_g), row(v_norm2_post_g)),
                    (row(norm1_pre_g), row(m_norm1_pre_g), row(v_norm1_pre_g)),
                    (row(b_in), row(m_b_in), row(v_b_in)),
                    (ws2(w_spatial), ws2(m_w_spatial), ws2(v_w_spatial)),
                    (b_spatial, m_b_spatial, v_b_spatial),
                    (pg2(w_pool), pg2(m_w_pool), pg2(v_w_pool)),
                    (pg2(w_sgu_proj), pg2(m_w_sgu_proj), pg2(v_w_sgu_proj)),
                    (w_out, m_w_out, v_w_out)]
    small_out = _adamw_small(summed_in[3], [summed_out[1]] + list(summed_in[4:])
                             + [summed_in[1], summed_in[2], summed_out[0]], small_params)
    loss = small_out[0][0, 0]
    small_names = SMALL_ROWS[1:] + ("norm1_pre_g", "b_in", "w_spatial", "b_spatial", "w_pool", "w_sgu_proj", "w_out")
    small = {nm: small_out[1 + 4 * p:5 + 4 * p] for p, nm in enumerate(small_names)}

    shapes = {"norm1_pre_g": norm1_pre_g.shape, "w_in": w_in.shape, "b_in": b_in.shape, "w_pool": w_pool.shape,
              "pool_scale": pool_scale.shape, "sgu_ln_g": sgu_ln_g.shape, "sgu_ln_b": sgu_ln_b.shape,
              "w_spatial": w_spatial.shape, "b_spatial": b_spatial.shape, "w_sgu_proj": w_sgu_proj.shape,
              "w_out": w_out.shape, "norm1_post_g": norm1_post_g.shape, "norm2_pre_g": norm2_pre_g.shape,
              "w_ff1": w_ff1.shape, "w_ff2": w_ff2.shape, "norm2_post_g": norm2_post_g.shape}
    source = {"w_in": big["in"], "w_ff1": big["ff1"], "w_ff2": big["ff2"], **small}
    order = list(shapes)
    outs = [loss, dx.reshape(x.shape)]
    for kind in range(4):
        outs += [source[nm][kind].reshape(shapes[nm]) for nm in order]
    return tuple(outs)
```

```python
import math

import jax
import jax.numpy as jnp
from jax import lax
from jax.experimental import pallas as pl
from jax.experimental.pallas import tpu as pltpu

F32, BF16 = jnp.float32, jnp.bfloat16
MESH = pl.DeviceIdType.MESH

D_MODEL = 1024
D_IN = 5120
D_FF = 4096
N_DEV = 8
N_CHIPS = 4
WINDOWS = (2, 4, 8, 16)
N_GROUPS = 4
GROUP = 256
HALO = 16
SGU_BLOCK = 128
N_HEADS = 4
HEAD = 256
CHUNK = 64
EPS = 1e-6
IN_SHARD = D_IN // N_DEV
FF_SHARD = D_FF // N_DEV
OUT_SHARD = D_MODEL // N_DEV
PG_SHARD = GROUP // N_DEV

ADAM_LR, ADAM_B1, ADAM_B2, ADAM_EPS, ADAM_WD, ADAM_STEP = 0.001, 0.9, 0.999, 1e-08, 0.01, 10

VMEM_LIMIT = 56 * 1024 * 1024
TM = 256
TM_BWD = 256
TM_IN = 512
PROJ_COLS = 512
GELU_C0 = math.sqrt(2.0 / math.pi)
GELU_C1 = 0.044715


def _dot(a, b):
    return jnp.dot(a, b, preferred_element_type=F32)


def _dot_nt(a, b):
    return lax.dot_general(a, b, (((1,), (1,)), ((), ())), preferred_element_type=F32)


def _dot_tn(a, b):
    return lax.dot_general(a, b, (((0,), (0,)), ((), ())), preferred_element_type=F32)


def _gelu(x):
    t = jnp.tanh(GELU_C0 * (x + GELU_C1 * (x * x * x)))
    return 0.5 * x * (1.0 + t), t


def _gelu_grad(x, t):
    return 0.5 * (1.0 + t) + 0.5 * x * (1.0 - t * t) * (GELU_C0 * (1.0 + 3.0 * GELU_C1 * (x * x)))


def _sigmoid(x):
    return 1.0 / (1.0 + jnp.exp(-x))


def _mean(x):
    return jnp.mean(x, axis=-1, keepdims=True)


def _colsum(x):
    return jnp.sum(x, axis=0, keepdims=True)


def _const_spec(shape):
    nd = len(shape)
    return pl.BlockSpec(shape, lambda *_: (0,) * nd, pipeline_mode=pl.Buffered(1))


def _acc_spec(shape):
    nd = len(shape)
    return pl.BlockSpec(shape, lambda *_: (0,) * nd)


def _masked_ws(ws_ref):
    ri = lax.broadcasted_iota(jnp.int32, (SGU_BLOCK, SGU_BLOCK), 0) // CHUNK
    ci = lax.broadcasted_iota(jnp.int32, (SGU_BLOCK, SGU_BLOCK), 1) // CHUNK
    return [jnp.where(ri >= ci, ws_ref[h], 0.0).astype(BF16) for h in range(N_HEADS)]


def _pool_fwd(pbuf, tile_idx, tm):
    pos = lax.broadcasted_iota(jnp.int32, (tm, 1), 0) + tile_idx * tm + 1
    pooled = []
    for g, w in enumerate(WINDOWS):
        e = pbuf[:, g * GROUP:(g + 1) * GROUP]
        s, sh = e, 1
        while sh < w:
            s = s + pltpu.roll(s, sh, 0)
            sh *= 2
        inv = 1.0 / jnp.minimum(pos, w).astype(F32)
        pooled.append(s[HALO:] * inv - e[HALO:])
    return pooled


def _spatial_head(ws_h, vb, bsp_ref, h, nblk):
    return jnp.concatenate(
        [_dot(ws_h, vb[n * SGU_BLOCK:(n + 1) * SGU_BLOCK, h * HEAD:(h + 1) * HEAD]) + bsp_ref[:, h:h + 1]
         for n in range(nblk)], axis=0)


def _prenorm(after, x, g1pre):
    t_len = x.shape[0]
    tm = TM_IN

    def body(after_ref, x_ref, g_ref, xn_ref, xnt_ref):
        del after_ref
        xv = x_ref[...]
        xnb = (xv * lax.rsqrt(_mean(xv * xv) + EPS) * g_ref[...]).astype(BF16)
        xn_ref[...] = xnb
        xnt_ref[...] = xnb.T

    return pl.pallas_call(
        body, name="prenorm", grid=(t_len // tm,),
        in_specs=[_ANY, pl.BlockSpec((tm, D_MODEL), lambda i: (i, 0)), _const_spec((1, D_MODEL))],
        out_specs=[pl.BlockSpec((tm, D_MODEL), lambda i: (i, 0)), pl.BlockSpec((D_MODEL, tm), lambda i: (0, i))],
        out_shape=[jax.ShapeDtypeStruct((t_len, D_MODEL), BF16), jax.ShapeDtypeStruct((D_MODEL, t_len), BF16)],
        compiler_params=pltpu.CompilerParams(dimension_semantics=("arbitrary",)),
    )(after, x, g1pre)


def _fwd_mix(after, xn, x, win_g, b_in, wpool, pool_scale, lng, lnb, ws, bsp_t, wproj, wout, g1post):
    t_len = x.shape[0]
    tm = TM
    nt = t_len // tm

    def body(after_ref, xn_ref, xb_ref, win_ref, bin_ref, wpool_ref, ps_ref, lng_ref, lnb_ref, ws_ref, bsp_ref,
             wproj_ref, wout_ref, g1post_ref, z_ref, y_ref, h1_ref, u_ref, gpu_ref, xhat_ref, gpv_ref, sa_ref,
             sb_ref, zcur, znext, pbuf):
        del after_ref
        s = pl.program_id(0)

        @pl.when(s == 0)
        def _():
            znext[...] = jnp.zeros((tm, D_IN), F32)
            pbuf[...] = jnp.zeros((tm + HALO, D_MODEL), F32)

        zcur[...] = znext[...]
        xnb = xn_ref[...]

        def project(p):
            cols = slice(p * PROJ_COLS, (p + 1) * PROJ_COLS)
            zp = _dot(xnb, win_ref[:, cols]) + bin_ref[:, cols]
            if (p + 1) * PROJ_COLS <= D_MODEL:
                z_ref[:, cols] = zp
            znext[:, cols] = zp

        project(0)
        pbuf[0:HALO, :] = jnp.where(s <= 1, 0.0, pbuf[0:HALO, :])
        pbuf[HALO:, :] = zcur[:, 0:D_MODEL]
        pooled = _pool_fwd(pbuf, jnp.maximum(s - 1, 0), tm)
        pbuf[0:HALO, :] = pbuf[tm:tm + HALO, :]
        a = jnp.concatenate([_dot(pooled[g].astype(BF16), wpool_ref[g]) for g in range(N_GROUPS)], axis=1)
        a = a * ps_ref[...]
        project(1)
        zu = zcur[:, D_MODEL:2 * D_MODEL]
        u, tu = _gelu(zu)
        u_ref[...] = u.astype(BF16)
        gpu_ref[...] = _gelu_grad(zu, tu).astype(BF16)
        project(2)
        zv = zcur[:, 2 * D_MODEL:3 * D_MODEL]
        gv, tv = _gelu(zv)
        xc = gv - _mean(gv)
        rln = lax.rsqrt(_mean(xc * xc) + EPS)
        xhat = xc * rln
        xhat_ref[...] = xhat.astype(BF16)
        gpv_ref[...] = (_gelu_grad(zv, tv) * rln).astype(BF16)
        vb = (xhat * lng_ref[...] + lnb_ref[...]).astype(BF16)
        wsm = _masked_ws(ws_ref)
        bbr = []
        for h in range(N_HEADS):
            project(3 + h)
            sv = _spatial_head(wsm[h], vb, bsp_ref, h, tm // SGU_BLOCK)
            bbr.append(_dot((u[:, h * HEAD:(h + 1) * HEAD] * sv).astype(BF16), wproj_ref[h]))
        bbr = jnp.concatenate(bbr, axis=1)
        project(7)
        sa = _sigmoid(zcur[:, 3 * D_MODEL:4 * D_MODEL])
        sb = _sigmoid(zcur[:, 4 * D_MODEL:5 * D_MODEL])
        sa_ref[...] = sa.astype(BF16)
        sb_ref[...] = sb.astype(BF16)
        project(8)
        yv = _dot((sa * a + sb * bbr).astype(BF16), wout_ref[...])
        y_ref[...] = yv
        project(9)
        ry = lax.rsqrt(_mean(yv * yv) + EPS)
        h1_ref[...] = xb_ref[...] + yv * ry * g1post_ref[...]

    proj = lambda w: pl.BlockSpec((tm, w), lambda s: (jnp.minimum(s, nt - 1), 0))
    mix = lambda w: pl.BlockSpec((tm, w), lambda s: (jnp.maximum(s - 1, 0), 0))
    return pl.pallas_call(
        body, name="fwd_mix", grid=(nt + 1,),
        in_specs=[_ANY, proj(D_MODEL), mix(D_MODEL),
                  _const_spec((D_MODEL, D_IN)),
                  _const_spec((1, D_IN)), _const_spec((N_GROUPS, GROUP, GROUP)), _const_spec((1, D_MODEL)),
                  _const_spec((1, D_MODEL)), _const_spec((1, D_MODEL)),
                  _const_spec((N_HEADS, SGU_BLOCK, SGU_BLOCK)), _const_spec((SGU_BLOCK, N_HEADS)),
                  _const_spec((N_HEADS, HEAD, HEAD)), _const_spec((D_MODEL, D_MODEL)), _const_spec((1, D_MODEL))],
        out_specs=[proj(D_MODEL), mix(D_MODEL), mix(D_MODEL)] + [mix(D_MODEL)] * 6,
        out_shape=[jax.ShapeDtypeStruct((t_len, D_MODEL), F32),
                   jax.ShapeDtypeStruct((t_len, D_MODEL), F32), jax.ShapeDtypeStruct((t_len, D_MODEL), F32)]
        + [jax.ShapeDtypeStruct((t_len, D_MODEL), BF16)] * 6,
        scratch_shapes=[pltpu.VMEM((tm, D_IN), F32), pltpu.VMEM((tm, D_IN), F32),
                        pltpu.VMEM((tm + HALO, D_MODEL), F32)],
        compiler_params=pltpu.CompilerParams(dimension_semantics=("arbitrary",), vmem_limit_bytes=VMEM_LIMIT),
    )(after, xn, x, win_g, b_in, wpool, pool_scale, lng, lnb, ws, bsp_t, wproj, wout, g1post)


def _mlp(h1, target, g2pre, g2post, w1_g, w2):
    t_len = h1.shape[0]
    tm = TM
    nt = t_len // tm

    def body(h1_ref, tgt_ref, g2pre_ref, g2post_ref, w1_ref, w2_ref,
             hn_ref, f_ref, df1_ref, df2_ref, dh1_ref, dg2post_ref, dg2pre_ref, loss_ref, f1_scr):
        i = pl.program_id(0)

        @pl.when(i == 0)
        def _():
            dg2post_ref[...] = jnp.zeros_like(dg2post_ref)
            dg2pre_ref[...] = jnp.zeros_like(dg2pre_ref)
            loss_ref[...] = jnp.zeros_like(loss_ref)

        h = h1_ref[...]
        r2 = lax.rsqrt(_mean(h * h) + EPS)
        nh = h * r2
        hnb = (nh * g2pre_ref[...]).astype(BF16)
        hn_ref[...] = hnb.T
        for k in range(N_DEV):
            f1_scr[:, k * FF_SHARD:(k + 1) * FF_SHARD] = _dot(hnb, w1_ref[:, k * FF_SHARD:(k + 1) * FF_SHARD])
        r = jnp.maximum(f1_scr[...], 0.0)
        fb = (r * r).astype(BF16)
        f_ref[...] = fb.T
        f2 = _dot(fb, w2_ref[...])
        rf = lax.rsqrt(_mean(f2 * f2) + EPS)
        nf = f2 * rf
        diff = h + nf * g2post_ref[...] - tgt_ref[...]
        loss_ref[...] += (0.5 / D_MODEL) * jnp.sum(diff * diff)
        dout = diff * (1.0 / D_MODEL)
        dg2post_ref[...] += _colsum(dout * nf)
        dn = dout * g2post_ref[...]
        df2b = (rf * (dn - nf * _mean(dn * nf))).astype(BF16)
        df2_ref[...] = df2b
        df = _dot_nt(df2b, w2_ref[...])
        df1b = (df * (2.0 * jnp.maximum(f1_scr[...], 0.0))).astype(BF16)
        df1_ref[...] = df1b
        dhn = _dot_nt(df1b, w1_ref[...])
        dg2pre_ref[...] += _colsum(dhn * nh)
        dnh = dhn * g2pre_ref[...]
        dh1_ref[...] = dout + r2 * (dnh - nh * _mean(dnh * nh))

    tok = lambda w: pl.BlockSpec((tm, w), lambda i: (i, 0))
    return pl.pallas_call(
        body, name="mlp_fwd_bwd", grid=(nt,),
        in_specs=[tok(D_MODEL), tok(D_MODEL), _const_spec((1, D_MODEL)), _const_spec((1, D_MODEL)),
                  _const_spec((D_MODEL, D_FF)), _const_spec((D_FF, D_MODEL))],
        out_specs=[pl.BlockSpec((D_MODEL, tm), lambda i: (0, i)), pl.BlockSpec((D_FF, tm), lambda i: (0, i)),
                   tok(D_FF), tok(D_MODEL), tok(D_MODEL),
                   _acc_spec((1, D_MODEL)), _acc_spec((1, D_MODEL)), _acc_spec((1, 128))],
        out_shape=[jax.ShapeDtypeStruct((D_MODEL, t_len), BF16), jax.ShapeDtypeStruct((D_FF, t_len), BF16),
                   jax.ShapeDtypeStruct((t_len, D_FF), BF16), jax.ShapeDtypeStruct((t_len, D_MODEL), BF16),
                   jax.ShapeDtypeStruct((t_len, D_MODEL), F32), jax.ShapeDtypeStruct((1, D_MODEL), F32),
                   jax.ShapeDtypeStruct((1, D_MODEL), F32), jax.ShapeDtypeStruct((1, 128), F32)],
        scratch_shapes=[pltpu.VMEM((tm, D_FF), F32)],
        compiler_params=pltpu.CompilerParams(dimension_semantics=("arbitrary",), vmem_limit_bytes=VMEM_LIMIT),
    )(h1, target, g2pre, g2post, w1_g, w2)


def _bwd_mix(after, dh1, y, z, saved, wpool, pool_scale, lng, lnb, ws, bsp_t, wproj, wout, g1post):
    t_len = y.shape[0]
    tm = TM_BWD
    nt = t_len // tm
    nblk = tm // SGU_BLOCK

    def body(after_ref, dh1_ref, y_ref, z_ref, zh_ref, u_ref, gpu_ref, xhat_ref, gpv_ref, sa_ref, sb_ref,
             wpool_ref, ps_ref, lng_ref, lnb_ref, ws_ref,
             bsp_ref, wproj_ref, wout_ref, g1post_ref,
             dz_ref, mg_ref, dy_ref, dwpool_ref, dwproj_ref, dws_ref, dbsp_ref, dg1post_ref, dps_ref,
             dlng_ref, dlnb_ref, dbin_ref, pbuf, qbuf):
        del after_ref
        i = pl.program_id(0)
        ti = nt - 1 - i

        @pl.when(i == 0)
        def _():
            for ref in (dwpool_ref, dwproj_ref, dws_ref, dbsp_ref, dg1post_ref, dps_ref, dlng_ref, dlnb_ref,
                        dbin_ref):
                ref[...] = jnp.zeros_like(ref)
            qbuf[tm:tm + HALO, :] = jnp.zeros((HALO, D_MODEL), F32)

        pbuf[0:HALO, :] = jnp.where(ti > 0, zh_ref[...], 0.0)
        pbuf[HALO:, :] = z_ref[...]
        pooled = _pool_fwd(pbuf, ti, tm)
        pooled_b = [p.astype(BF16) for p in pooled]
        a_raw = jnp.concatenate([_dot(pooled_b[g], wpool_ref[g]) for g in range(N_GROUPS)], axis=1)
        wsm = _masked_ws(ws_ref)
        u = u_ref[...].astype(F32)
        xhat = xhat_ref[...].astype(F32)
        vb = (xhat * lng_ref[...] + lnb_ref[...]).astype(BF16)
        sv_heads = [_spatial_head(wsm[h], vb, bsp_ref, h, nblk) for h in range(N_HEADS)]
        gated_b = [(u[:, h * HEAD:(h + 1) * HEAD] * sv_heads[h]).astype(BF16) for h in range(N_HEADS)]
        bbr = jnp.concatenate([_dot(gated_b[h], wproj_ref[h]) for h in range(N_HEADS)], axis=1)
        sa = sa_ref[...].astype(F32)
        sb = sb_ref[...].astype(F32)
        a = a_raw * ps_ref[...]
        mg_ref[...] = (sa * a + sb * bbr).astype(BF16).T

        dh = dh1_ref[...]
        yv = y_ref[...]
        ry = lax.rsqrt(_mean(yv * yv) + EPS)
        ny = yv * ry
        dg1post_ref[...] += _colsum(dh * ny)
        dn = dh * g1post_ref[...]
        dyb = (ry * (dn - ny * _mean(dn * ny))).astype(BF16)
        dy_ref[...] = dyb
        dmg = _dot_nt(dyb, wout_ref[...])

        da = dmg * sa
        dbbr = dmg * sb
        dzga = dmg * a * sa * (1.0 - sa)
        dzgb = dmg * bbr * sb * (1.0 - sb)
        dz_ref[:, 3 * D_MODEL:4 * D_MODEL] = dzga.astype(BF16)
        dz_ref[:, 4 * D_MODEL:5 * D_MODEL] = dzgb.astype(BF16)
        dbin_ref[:, 3 * D_MODEL:4 * D_MODEL] += _colsum(dzga)
        dbin_ref[:, 4 * D_MODEL:5 * D_MODEL] += _colsum(dzgb)

        dps_ref[...] += _colsum(da * a_raw)
        da_raw_b = (da * ps_ref[...]).astype(BF16)
        pos = lax.broadcasted_iota(jnp.int32, (tm, 1), 0) + ti * tm + 1
        dpooled = []
        for g, w in enumerate(WINDOWS):
            cols = slice(g * GROUP, (g + 1) * GROUP)
            dwpool_ref[g] += _dot_tn(pooled_b[g], da_raw_b[:, cols])
            dp = _dot_nt(da_raw_b[:, cols], wpool_ref[g])
            dpooled.append(dp)
            qbuf[0:tm, cols] = dp * (1.0 / jnp.minimum(pos, w).astype(F32))
        n_ext = tm + HALO
        dzp = []
        for g, w in enumerate(WINDOWS):
            e = qbuf[:, g * GROUP:(g + 1) * GROUP]
            s, sh = e, 1
            while sh < w:
                s = s + pltpu.roll(s, n_ext - sh, 0)
                sh *= 2
            dzp.append(s[0:tm] - dpooled[g])
        qbuf[tm:tm + HALO, :] = qbuf[0:HALO, :]
        dzp = jnp.concatenate(dzp, axis=1)
        dz_ref[:, 0:D_MODEL] = dzp.astype(BF16)
        dbin_ref[:, 0:D_MODEL] += _colsum(dzp)

        dv_heads = []
        du_heads = []
        for h in range(N_HEADS):
            cols = slice(h * HEAD, (h + 1) * HEAD)
            dbbr_b = dbbr[:, cols].astype(BF16)
            dwproj_ref[h] += _dot_tn(gated_b[h], dbbr_b)
            dgated = _dot_nt(dbbr_b, wproj_ref[h])
            du_heads.append(dgated * sv_heads[h])
            dsv = dgated * u[:, cols]
            dsv_b = dsv.astype(BF16)
            rows = []
            for n in range(nblk):
                blk = slice(n * SGU_BLOCK, (n + 1) * SGU_BLOCK)
                rows.append(_dot_tn(wsm[h], dsv_b[blk]))
                dws_ref[h] += _dot_nt(dsv_b[blk], vb[blk, cols])
                dbsp_ref[:, h:h + 1] += jnp.sum(dsv[blk], axis=1, keepdims=True)
            dv_heads.append(jnp.concatenate(rows, axis=0))
        dzu = jnp.concatenate(du_heads, axis=1) * gpu_ref[...].astype(F32)
        dz_ref[:, D_MODEL:2 * D_MODEL] = dzu.astype(BF16)
        dbin_ref[:, D_MODEL:2 * D_MODEL] += _colsum(dzu)
        dv = jnp.concatenate(dv_heads, axis=1)
        dlng_ref[...] += _colsum(dv * xhat)
        dlnb_ref[...] += _colsum(dv)
        dxh = dv * lng_ref[...]
        dzv = (dxh - _mean(dxh) - xhat * _mean(dxh * xhat)) * gpv_ref[...].astype(F32)
        dz_ref[:, 2 * D_MODEL:3 * D_MODEL] = dzv.astype(BF16)
        dbin_ref[:, 2 * D_MODEL:3 * D_MODEL] += _colsum(dzv)

        @pl.when(i == nt - 1)
        def _():
            ri = lax.broadcasted_iota(jnp.int32, (SGU_BLOCK, SGU_BLOCK), 0) // CHUNK
            ci = lax.broadcasted_iota(jnp.int32, (SGU_BLOCK, SGU_BLOCK), 1) // CHUNK
            for h in range(N_HEADS):
                dws_ref[h] = jnp.where(ri >= ci, dws_ref[h], 0.0)

    tok = lambda w: pl.BlockSpec((tm, w), lambda i: (nt - 1 - i, 0))
    halo = pl.BlockSpec((HALO, D_MODEL), lambda i: (jnp.maximum((nt - 1 - i) * (tm // HALO) - 1, 0), 0))
    return pl.pallas_call(
        body, name="bwd_mix", grid=(nt,),
        in_specs=[_ANY, tok(D_MODEL), tok(D_MODEL), tok(D_MODEL), halo] + [tok(D_MODEL)] * 6
        + [_const_spec((N_GROUPS, GROUP, GROUP)),
                  _const_spec((1, D_MODEL)), _const_spec((1, D_MODEL)), _const_spec((1, D_MODEL)),
                  _const_spec((N_HEADS, SGU_BLOCK, SGU_BLOCK)), _const_spec((SGU_BLOCK, N_HEADS)),
                  _const_spec((N_HEADS, HEAD, HEAD)), _const_spec((D_MODEL, D_MODEL)), _const_spec((1, D_MODEL))],
        out_specs=[tok(D_IN), pl.BlockSpec((D_MODEL, tm), lambda i: (0, nt - 1 - i)), tok(D_MODEL),
                   _acc_spec((N_GROUPS, GROUP, GROUP)), _acc_spec((N_HEADS, HEAD, HEAD)),
                   _acc_spec((N_HEADS, SGU_BLOCK, SGU_BLOCK)), _acc_spec((SGU_BLOCK, N_HEADS)),
                   _acc_spec((1, D_MODEL)), _acc_spec((1, D_MODEL)), _acc_spec((1, D_MODEL)), _acc_spec((1, D_MODEL)),
                   _acc_spec((1, D_IN))],
        out_shape=[jax.ShapeDtypeStruct((t_len, D_IN), BF16),
                   jax.ShapeDtypeStruct((D_MODEL, t_len), BF16), jax.ShapeDtypeStruct((t_len, D_MODEL), BF16),
                   jax.ShapeDtypeStruct((N_GROUPS, GROUP, GROUP), F32), jax.ShapeDtypeStruct((N_HEADS, HEAD, HEAD), F32),
                   jax.ShapeDtypeStruct((N_HEADS, SGU_BLOCK, SGU_BLOCK), F32),
                   jax.ShapeDtypeStruct((SGU_BLOCK, N_HEADS), F32),
                   jax.ShapeDtypeStruct((1, D_MODEL), F32), jax.ShapeDtypeStruct((1, D_MODEL), F32),
                   jax.ShapeDtypeStruct((1, D_MODEL), F32), jax.ShapeDtypeStruct((1, D_MODEL), F32),
                   jax.ShapeDtypeStruct((1, D_IN), F32)],
        scratch_shapes=[pltpu.VMEM((tm + HALO, D_MODEL), F32), pltpu.VMEM((tm + HALO, D_MODEL), F32)],
        compiler_params=pltpu.CompilerParams(dimension_semantics=("arbitrary",), vmem_limit_bytes=VMEM_LIMIT),
    )(after, dh1, y, z, z, *saved, wpool, pool_scale, lng, lnb, ws, bsp_t, wproj, wout, g1post)


def _bwd_in(after, dz_b, x, dh1, g1pre, win_g):
    t_len = x.shape[0]
    tm = TM_IN
    nt = t_len // tm

    def body(after_ref, dz_ref, x_ref, dh1_ref, g1_ref, win_ref, dx_ref, dg1pre_ref):
        del after_ref

        @pl.when(pl.program_id(0) == 0)
        def _():
            dg1pre_ref[...] = jnp.zeros_like(dg1pre_ref)

        dxn = _dot_nt(dz_ref[...], win_ref[...])
        xv = x_ref[...]
        r1 = lax.rsqrt(_mean(xv * xv) + EPS)
        nx = xv * r1
        dg1pre_ref[...] += _colsum(dxn * nx)
        dnx = dxn * g1_ref[...]
        dx_ref[...] = r1 * (dnx - nx * _mean(dnx * nx)) + dh1_ref[...]

    tok = lambda w: pl.BlockSpec((tm, w), lambda i: (i, 0))
    return pl.pallas_call(
        body, name="bwd_in", grid=(nt,),
        in_specs=[_ANY, tok(D_IN), tok(D_MODEL), tok(D_MODEL), _const_spec((1, D_MODEL)),
                  _const_spec((D_MODEL, D_IN))],
        out_specs=[tok(D_MODEL), _acc_spec((1, D_MODEL))],
        out_shape=[jax.ShapeDtypeStruct((t_len, D_MODEL), F32), jax.ShapeDtypeStruct((1, D_MODEL), F32)],
        compiler_params=pltpu.CompilerParams(dimension_semantics=("arbitrary",), vmem_limit_bytes=VMEM_LIMIT),
    )(after, dz_b, x, dh1, g1pre, win_g)


def _owner_of_slot(s):
    return 4 * ((s // 2) % 2) + 2 * (s % 2) + s // 4


def _wgrad_ff(f_t, df2_b, hn_t, df1_b):
    t_len = df2_b.shape[0]

    def body(f_ref, df2_ref, hn_ref, df1_ref, o2_ref, o1_ref):
        o2_ref[...] = _dot(f_ref[...], df2_ref[...]).astype(BF16)
        o1_ref[...] = _dot(hn_ref[...], df1_ref[...]).astype(BF16)

    return pl.pallas_call(
        body, name="wgrad_ff", grid=(N_DEV,),
        in_specs=[pl.BlockSpec((FF_SHARD, t_len), lambda s: (_owner_of_slot(s), 0)),
                  pl.BlockSpec((t_len, D_MODEL), lambda s: (0, 0), pipeline_mode=pl.Buffered(1)),
                  pl.BlockSpec((D_MODEL, t_len), lambda s: (0, 0), pipeline_mode=pl.Buffered(1)),
                  pl.BlockSpec((t_len, FF_SHARD), lambda s: (0, _owner_of_slot(s)))],
        out_specs=[pl.BlockSpec((None, FF_SHARD, D_MODEL), lambda s: (s, 0, 0)),
                   pl.BlockSpec((None, D_MODEL, FF_SHARD), lambda s: (s, 0, 0))],
        out_shape=[jax.ShapeDtypeStruct((N_DEV, FF_SHARD, D_MODEL), BF16),
                   jax.ShapeDtypeStruct((N_DEV, D_MODEL, FF_SHARD), BF16)],
        compiler_params=pltpu.CompilerParams(dimension_semantics=("arbitrary",), vmem_limit_bytes=VMEM_LIMIT),
    )(f_t, df2_b, hn_t, df1_b)


def _wgrad_in(xn_t, dz_b):
    t_len = dz_b.shape[0]

    def body(a_ref, b_ref, o_ref):
        res = _dot(a_ref[...], b_ref[...])
        o_ref[0] = res[:, 0:IN_SHARD].astype(BF16)
        o_ref[1] = res[:, IN_SHARD:2 * IN_SHARD].astype(BF16)

    out = pl.pallas_call(
        body, name="wgrad_in", grid=(N_CHIPS,),
        in_specs=[pl.BlockSpec((D_MODEL, t_len), lambda q: (0, 0), pipeline_mode=pl.Buffered(1)),
                  pl.BlockSpec((t_len, 2 * IN_SHARD), lambda q: (0, q))],
        out_specs=pl.BlockSpec((2, None, D_MODEL, IN_SHARD), lambda q: (0, q, 0, 0)),
        out_shape=jax.ShapeDtypeStruct((2, N_CHIPS, D_MODEL, IN_SHARD), BF16),
        compiler_params=pltpu.CompilerParams(dimension_semantics=("arbitrary",), vmem_limit_bytes=VMEM_LIMIT),
    )(xn_t, dz_b)
    return out.reshape(N_DEV, D_MODEL, IN_SHARD)


def _wgrad_out(mg_t, dy_b):
    t_len = dy_b.shape[0]

    def body(a_ref, b_ref, o_ref):
        res = _dot(a_ref[...], b_ref[...])
        o_ref[0] = res[0:OUT_SHARD].astype(BF16)
        o_ref[1] = res[OUT_SHARD:2 * OUT_SHARD].astype(BF16)

    out = pl.pallas_call(
        body, name="wgrad_out", grid=(N_CHIPS,),
        in_specs=[pl.BlockSpec((2 * OUT_SHARD, t_len), lambda q: (q, 0)),
                  pl.BlockSpec((t_len, D_MODEL), lambda q: (0, 0), pipeline_mode=pl.Buffered(1))],
        out_specs=pl.BlockSpec((2, None, OUT_SHARD, D_MODEL), lambda q: (0, q, 0, 0)),
        out_shape=jax.ShapeDtypeStruct((2, N_CHIPS, OUT_SHARD, D_MODEL), BF16),
        compiler_params=pltpu.CompilerParams(dimension_semantics=("arbitrary",), vmem_limit_bytes=VMEM_LIMIT),
    )(mg_t, dy_b)
    return out.reshape(N_DEV, OUT_SHARD, D_MODEL)


def _coords():
    return lax.axis_index("x"), lax.axis_index("y"), lax.axis_index("c")


_ANY = pl.BlockSpec(memory_space=pl.ANY)


def _pair_exchange(name, bigs, smalls):
    nb, n = len(bigs), len(bigs) + len(smalls)

    def body(*refs):
        src, dst = refs[:n], refs[n:2 * n]
        send_sems, recv_sems = refs[2 * n:]
        x, y, c = _coords()
        copies = []
        for j in range(n):
            s = src[j].at[pl.ds(4 * (1 - c), 4)] if j < nb else src[j]
            cp = pltpu.make_async_remote_copy(src_ref=s, dst_ref=dst[j], send_sem=send_sems.at[j],
                                              recv_sem=recv_sems.at[j], device_id=(x, y, 1 - c), device_id_type=MESH)
            cp.start()
            copies.append(cp)
        for cp in copies:
            cp.wait()

    return pl.pallas_call(
        body, name=name,
        in_specs=[_ANY] * n, out_specs=[_ANY] * n,
        out_shape=[jax.ShapeDtypeStruct((4,) + b.shape[1:], b.dtype) for b in bigs]
        + [jax.ShapeDtypeStruct(s.shape, s.dtype) for s in smalls],
        scratch_shapes=[pltpu.SemaphoreType.DMA((n,)), pltpu.SemaphoreType.DMA((n,))],
    )(*bigs, *smalls)


_HBM = pl.BlockSpec(memory_space=pltpu.HBM)
_SEM = pl.BlockSpec(memory_space=pltpu.SEMAPHORE)
_VMEM = pl.BlockSpec(memory_space=pltpu.VMEM)
_EFFECT = pltpu.SideEffectType.DATAFLOW_SIDE_EFFECTING
_TOKEN = jax.ShapeDtypeStruct((8, 128), F32)


def _in_hbm(a):
    return pltpu.with_memory_space_constraint(a, pltpu.HBM)


def _split_call(name, body, n_sems_out, arrays, sems_in=(), after=None):
    na, ns = len(arrays), len(sems_in)
    has_after = after is not None

    def kernel_body(*refs):
        arr = refs[:na]
        s_in = refs[na:na + ns]
        outs = refs[na + ns + has_after:]
        body(arr, s_in, outs[:n_sems_out])
        outs[-1][...] = jnp.zeros((8, 128), F32)

    out_shape = ([pltpu.SemaphoreType.DMA(())] * n_sems_out + [pltpu.HBM(a.shape, a.dtype) for a in arrays] + [_TOKEN])
    res = pl.pallas_call(
        kernel_body, name=name, out_shape=out_shape,
        in_specs=[_HBM] * na + [_SEM] * ns + [_ANY] * has_after,
        out_specs=[_SEM] * n_sems_out + [_HBM] * na + [_VMEM],
        input_output_aliases={i: n_sems_out + i for i in range(na)},
        compiler_params=pltpu.CompilerParams(has_side_effects=_EFFECT),
    )(*[_in_hbm(a) for a in arrays], *sems_in, *([after] if has_after else []))
    return list(res[:n_sems_out]), list(res[n_sems_out:n_sems_out + na]), res[-1]


def _wait_bytes_of(ref, send_sem, recv_sem, peer, send=True, recv=True):
    cp = pltpu.make_async_remote_copy(src_ref=ref, dst_ref=ref, send_sem=send_sem, recv_sem=recv_sem,
                                      device_id=peer, device_id_type=MESH)
    if send:
        cp.wait_send()
    if recv:
        cp.wait_recv()


def _gather_behind(tag, shards, after, work, by_columns=()):
    n = len(shards)
    lands = [lax.empty((s.shape[0], N_DEV * s.shape[1]) if j in by_columns else (N_DEV,) + s.shape, s.dtype)
             for j, s in enumerate(shards)]

    def slots(arr, j, first, count=1):
        if j in by_columns:
            cols = shards[j].shape[1]
            return arr[n + j].at[:, pl.ds(pl.multiple_of(first * cols, 128), count * cols)]
        return arr[n + j].at[first] if count == 1 else arr[n + j].at[pl.ds(first, count)]

    def own_slot(arr, j, sem):
        x, y, c = _coords()
        return pltpu.make_async_copy(arr[j], slots(arr, j, 4 * x + 2 * y + c), sem)

    def start(arr, _, sems):
        x, y, c = _coords()
        me = 4 * x + 2 * y + c
        for j in range(n):
            for chip in [(1 - x, y), (x, 1 - y), (1 - x, 1 - y)]:
                pltpu.make_async_remote_copy(src_ref=arr[j], dst_ref=slots(arr, j, me), send_sem=sems[j],
                                             recv_sem=sems[n + j], device_id=(*chip, c), device_id_type=MESH).start()
        for j in range(n):
            pltpu.make_async_remote_copy(src_ref=arr[j], dst_ref=slots(arr, j, me), send_sem=sems[2 * n + j],
                                         recv_sem=sems[3 * n + j], device_id=(x, y, 1 - c),
                                         device_id_type=MESH).start()
            own_slot(arr, j, sems[4 * n + j]).start()

    def middle(arr, s_in, sems):
        x, y, c = _coords()
        sibling = (x, y, 1 - c)
        for j in range(n):
            _wait_bytes_of(slots(arr, j, 0, 3), s_in[j], s_in[n + j], sibling)
            for chip in [(1 - x, y), (x, 1 - y), (1 - x, 1 - y)]:
                slot = slots(arr, j, 4 * chip[0] + 2 * chip[1] + c)
                pltpu.make_async_remote_copy(src_ref=slot, dst_ref=slot, send_sem=sems[j], recv_sem=sems[n + j],
                                             device_id=sibling, device_id_type=MESH).start()

    def finish(arr, s_in, _):
        x, y, c = _coords()
        sibling = (x, y, 1 - c)
        for j in range(n):
            _wait_bytes_of(slots(arr, j, 0, 1), s_in[j], s_in[n + j], sibling)
            own_slot(arr, j, s_in[2 * n + j]).wait()
            _wait_bytes_of(slots(arr, j, 0, 3), s_in[3 * n + j], s_in[4 * n + j], sibling)

    sems, arrays, token = _split_call("gather_%s_start" % tag, start, 5 * n, list(shards) + lands, after=after)
    result = work(token)
    fwd_sems, arrays, token = _split_call("gather_%s_middle" % tag, middle, 2 * n, arrays, sems_in=sems[:2 * n],
                                          after=result[0])
    _, arrays, _ = _split_call("gather_%s_finish" % tag, finish, 0, arrays, sems_in=sems[2 * n:] + fwd_sems,
                               after=token)
    return arrays[n:], result


def _chip_exchange_behind(tag, bigs, smalls, work):
    nb, n = len(bigs), len(bigs) + len(smalls)
    lands = [lax.empty(s.shape, s.dtype) for s in bigs] + [lax.empty((N_CHIPS,) + s.shape, s.dtype) for s in smalls]

    def own_slot(arr, j, sem):
        x, y, _ = _coords()
        q_me = 2 * x + y
        return pltpu.make_async_copy(arr[j].at[q_me] if j < nb else arr[j], arr[n + j].at[q_me], sem)

    def start(arr, _, sems):
        x, y, c = _coords()
        q_me = 2 * x + y
        for j in range(n):
            for peer in [(1 - x, y, c), (x, 1 - y, c), (1 - x, 1 - y, c)]:
                piece = arr[j].at[2 * peer[0] + peer[1]] if j < nb else arr[j]
                pltpu.make_async_remote_copy(src_ref=piece, dst_ref=arr[n + j].at[q_me],
                                             send_sem=sems[j], recv_sem=sems[n + j], device_id=peer,
                                             device_id_type=MESH).start()
            own_slot(arr, j, sems[2 * n + j]).start()

    def finish(arr, s_in, _):
        x, y, c = _coords()
        for j in range(n):
            _wait_bytes_of(arr[n + j].at[pl.ds(0, 3)], s_in[j], s_in[n + j], (x, y, 1 - c))
            own_slot(arr, j, s_in[2 * n + j]).wait()

    sems, arrays, token = _split_call("chip_exchange_%s_start" % tag, start, 3 * n, list(bigs) + list(smalls) + lands)
    result = work(token)
    _, arrays, _ = _split_call("chip_exchange_%s_finish" % tag, finish, 0, arrays, sems_in=sems, after=result[0])
    return arrays[n:], result


def _pair_sum_big(name, c_arr, mine, theirs, whole_mine=(), whole_theirs=()):
    n, nw = len(mine), len(whole_mine)

    def body(c_ref, *refs):
        del c_ref
        ins, outs = refs[:2 * (n + nw)], refs[2 * (n + nw):]
        for j in range(n):
            outs[j][...] = (ins[j][...].astype(F32) + ins[n + j][...].astype(F32)).astype(BF16)

        @pl.when(pl.program_id(0) == 0)
        def _():
            for j in range(nw):
                outs[n + j][...] = ins[2 * n + j][...] + ins[2 * n + nw + j][...]

    own = lambda t: pl.BlockSpec((None,) + t.shape[1:], lambda q, c_ref: (4 * c_ref[0] + q, 0, 0))
    slot = lambda t: pl.BlockSpec((None,) + t.shape[1:], lambda q, c_ref: (q, 0, 0))
    whole = lambda t: pl.BlockSpec(t.shape, lambda q, c_ref: (0,) * t.ndim)
    return pl.pallas_call(
        body, name=name,
        grid_spec=pltpu.PrefetchScalarGridSpec(
            num_scalar_prefetch=1, grid=(N_CHIPS,),
            in_specs=[own(t) for t in theirs] + [slot(t) for t in theirs]
            + [whole(t) for t in whole_mine] + [whole(t) for t in whole_theirs],
            out_specs=[slot(t) for t in theirs] + [whole(t) for t in whole_mine]),
        out_shape=[jax.ShapeDtypeStruct(t.shape, BF16) for t in theirs]
        + [jax.ShapeDtypeStruct(t.shape, t.dtype) for t in whole_mine],
        compiler_params=pltpu.CompilerParams(dimension_semantics=("arbitrary",)),
    )(c_arr, *mine, *theirs, *whole_mine, *whole_theirs)


def _adamw_math(w, g, m, v):
    m = ADAM_B1 * m + (1.0 - ADAM_B1) * g
    v = ADAM_B2 * v + (1.0 - ADAM_B2) * (g * g)
    m_hat = m / (1.0 - ADAM_B1 ** ADAM_STEP)
    v_hat = v / (1.0 - ADAM_B2 ** ADAM_STEP)
    delta = -ADAM_LR * (m_hat / (jnp.sqrt(v_hat) + ADAM_EPS) + ADAM_WD * w)
    return delta, m, v


ADAMW_STEPS = 2


def _adamw_big(name, after, chip_sums, params):
    n = len(params)

    def body(after_ref, *refs):
        del after_ref
        outs = refs[4 * n:]
        for j in range(n):
            t_ref, w_ref, m_ref, v_ref = refs[4 * j:4 * j + 4]
            g = t_ref[0].astype(F32)
            for q in range(1, N_CHIPS):
                g = g + t_ref[q].astype(F32)
            d, mn, vn = _adamw_math(w_ref[...], g, m_ref[...], v_ref[...])
            for out, val in zip(outs[4 * j:4 * j + 4], (g, d, mn, vn)):
                out[...] = val

    in_specs, out_specs, out_shape, operands = [_ANY], [], [], [after]
    for t, (w, m, v) in zip(chip_sums, params):
        rows, cols = w.shape
        tr = rows // ADAMW_STEPS
        blk = pl.BlockSpec((tr, cols), lambda r: (r, 0))
        in_specs += [pl.BlockSpec((N_CHIPS, tr, cols), lambda r: (0, r, 0)), blk, blk, blk]
        out_specs += [blk] * 4
        out_shape += [jax.ShapeDtypeStruct((rows, cols), F32)] * 4
        operands += [t, w, m, v]
    res = pl.pallas_call(
        body, name=name, grid=(ADAMW_STEPS,), in_specs=in_specs, out_specs=out_specs, out_shape=out_shape,
        compiler_params=pltpu.CompilerParams(dimension_semantics=("arbitrary",), vmem_limit_bytes=VMEM_LIMIT),
    )(*operands)
    return [res[4 * j:4 * j + 4] for j in range(n)]


SMALL_ROWS = ("loss", "pool_scale", "sgu_ln_g", "sgu_ln_b", "norm1_post_g", "norm2_pre_g", "norm2_post_g")


N_ROW_PARAMS = len(SMALL_ROWS) - 1


def _adamw_small(u_rows, u_others, params):
    n, n_oth = len(params), len(u_others)

    def body(*refs):
        urow_ref, others = refs[0], refs[1:1 + n_oth]
        wmv = refs[1 + n_oth:1 + n_oth + 3 * n]
        loss_ref = refs[1 + n_oth + 3 * n]
        outs = refs[2 + n_oth + 3 * n:]

        def total(ref, idx):
            g = ref[(0,) + idx].astype(F32)
            for q in range(1, N_CHIPS):
                g = g + ref[(q,) + idx].astype(F32)
            return g

        loss_ref[...] = total(urow_ref, (slice(0, 1), slice(None)))
        for p in range(n):
            if p < N_ROW_PARAMS:
                g = total(urow_ref, (slice(p + 1, p + 2), slice(None)))
            else:
                g = total(others[p - N_ROW_PARAMS], (slice(None), slice(None)))
            d, mn, vn = _adamw_math(wmv[3 * p][...], g, wmv[3 * p + 1][...], wmv[3 * p + 2][...])
            outs[4 * p][...] = g
            outs[4 * p + 1][...] = d
            outs[4 * p + 2][...] = mn
            outs[4 * p + 3][...] = vn

    flat = [a for p in params for a in p]
    out_shape = [jax.ShapeDtypeStruct((1, D_MODEL), F32)]
    for w, _, _ in params:
        out_shape += [jax.ShapeDtypeStruct(w.shape, F32)] * 4
    return pl.pallas_call(body, name="adamw_small", out_shape=out_shape)(u_rows, *u_others, *flat)


def _to_bf16(arrays):
    n = len(arrays)

    def body(*refs):
        for j in range(n):
            refs[n + j][...] = refs[j][...].astype(BF16)

    return pl.pallas_call(
        body, name="shards_to_bf16", out_shape=[jax.ShapeDtypeStruct(a.shape, BF16) for a in arrays],
    )(*arrays)


def _slots_of_rows(full):
    owners = [_owner_of_slot(s) for s in range(N_DEV)]
    parts = full.reshape(N_GROUPS, N_DEV, PG_SHARD, GROUP)
    return jnp.stack([parts[:, o] for o in owners]).reshape(N_DEV, N_GROUPS * PG_SHARD, GROUP).astype(BF16)


def kernel(x, norm1_pre_g, w_in, b_in, w_pool, pool_scale, sgu_ln_g, sgu_ln_b, w_spatial, b_spatial, w_sgu_proj, w_out, norm1_post_g, norm2_pre_g, w_ff1, w_ff2, norm2_post_g, loss_target, m_norm1_pre_g, m_w_in, m_b_in, m_w_pool, m_pool_scale, m_sgu_ln_g, m_sgu_ln_b, m_w_spatial, m_b_spatial, m_w_sgu_proj, m_w_out, m_norm1_post_g, m_norm2_pre_g, m_w_ff1, m_w_ff2, m_norm2_post_g, v_norm1_pre_g, v_w_in, v_b_in, v_w_pool, v_pool_scale, v_sgu_ln_g, v_sgu_ln_b, v_w_spatial, v_b_spatial, v_w_sgu_proj, v_w_out, v_norm1_post_g, v_norm2_pre_g, v_w_ff1, v_w_ff2, v_norm2_post_g):
    t_len = x.shape[1]
    row = lambda a: a.reshape(1, -1)
    x2 = x.reshape(t_len, D_MODEL)
    tgt2 = loss_target.reshape(t_len, D_MODEL)
    pg2 = lambda a: a.reshape(N_GROUPS * PG_SHARD, GROUP)

    shards_b = _to_bf16([w_in, pg2(w_pool), pg2(w_sgu_proj), w_out, w_ff1, w_ff2])

    def prework(token):
        return _prenorm(token, x2, row(norm1_pre_g))

    (win_f, g_pool, g_proj, g_out), (xn, xn_b) = _gather_behind("mix", shards_b[:4], None, prework, by_columns=(0,))
    regroup = lambda g: g.reshape(N_DEV, N_GROUPS, PG_SHARD, GROUP).transpose(1, 0, 2, 3).reshape(N_GROUPS, GROUP, GROUP)
    wpool_f, wproj_f = regroup(g_pool), regroup(g_proj)
    wout_f = g_out.reshape(D_MODEL, D_MODEL)
    bsp_t = b_spatial.T

    def forward(token):
        z, y, h1, *saved = _fwd_mix(token, xn, x2, win_f, row(b_in), wpool_f, row(pool_scale),
                                    row(sgu_ln_g), row(sgu_ln_b), w_spatial, bsp_t, wproj_f, wout_f,
                                    row(norm1_post_g))
        return h1, z, y, saved

    (g_ff1, g_ff2), (h1, z, y, saved) = _gather_behind("ff", shards_b[4:], win_f,
                                                       forward, by_columns=(0,))
    w2_f = g_ff2.reshape(D_FF, D_MODEL)
    hn_b, f_b, df1_b, df2_b, dh1, dg2post, dg2pre, loss_p = _mlp(h1, tgt2, row(norm2_pre_g), row(norm2_post_g),
                                                               g_ff1, w2_f)
    p_ff2, p_ff1 = _wgrad_ff(f_b, df2_b, hn_b, df1_b)
    c_arr = lax.axis_index("c").astype(jnp.int32).reshape(1)
    ff_parts = [p_ff1, p_ff2]
    got_ff = _pair_exchange("pair_exchange_ff", ff_parts, [])
    chip_ff = _pair_sum_big("pair_sum_ff", c_arr, ff_parts, got_ff)

    def backward_mix(token):
        return _bwd_mix(token, dh1, y, z, saved, wpool_f, row(pool_scale), row(sgu_ln_g), row(sgu_ln_b), w_spatial,
                        bsp_t, wproj_f, wout_f, row(norm1_post_g))

    summed_ff, (dz_b, mg_b, dy_b, dwpool, dwproj, dws, dbsp_t, dg1post, dps, dlng, dlnb,
                dbin) = _chip_exchange_behind("ff", chip_ff, [], backward_mix)

    p_in = _wgrad_in(xn_b, dz_b)
    bigs = [p_in, _slots_of_rows(dwpool), _slots_of_rows(dwproj)]
    rows = jnp.concatenate([jnp.broadcast_to(loss_p[:, 0:1], (1, D_MODEL)), dps, dlng, dlnb, dg1post, dg2pre, dg2post],
                           axis=0)
    smalls = [rows, dbin, dws.reshape(N_HEADS * SGU_BLOCK, SGU_BLOCK), dbsp_t.T]
    got = _pair_exchange("pair_exchange_in", bigs, smalls)
    sums = _pair_sum_big("pair_sum_in", c_arr, bigs, got[:3], smalls, got[3:])
    chip_bigs, chip_smalls = sums[:3], sums[3:]

    def backward_rest(token):
        dx, dg1pre = _bwd_in(token, dz_b, x2, dh1, row(norm1_pre_g), win_f)
        p_out = _wgrad_out(mg_b, dy_b)
        got_out = _pair_exchange("pair_exchange_out", [p_out], [dg1pre])
        chip_out, chip_g1pre = _pair_sum_big("pair_sum_out", c_arr, [p_out], got_out[:1], [dg1pre], got_out[1:])
        upd_ff = _adamw_big("adamw_ff", p_out, summed_ff, [(w_ff1, m_w_ff1, v_w_ff1), (w_ff2, m_w_ff2, v_w_ff2)])
        return upd_ff[1][0], dx, chip_out, chip_g1pre, upd_ff

    summed_in, (_, dx, chip_out, chip_g1pre, upd_ff) = _chip_exchange_behind("in", chip_bigs, chip_smalls,
                                                                             backward_rest)

    def update_in(token):
        res = _adamw_big("adamw_in", token, summed_in[:1], [(w_in, m_w_in, v_w_in)])[0]
        return res[0], res

    summed_out, (_, upd_in) = _chip_exchange_behind("out", [chip_out], [chip_g1pre], update_in)
    big = {"in": upd_in, "ff1": upd_ff[0], "ff2": upd_ff[1]}
    ws2 = lambda a: a.reshape(N_HEADS * SGU_BLOCK, SGU_BLOCK)
    small_params = [(row(pool_scale), row(m_pool_scale), row(v_pool_scale)),
                    (row(sgu_ln_g), row(m_sgu_ln_g), row(v_sgu_ln_g)),
                    (row(sgu_ln_b), row(m_sgu_ln_b), row(v_sgu_ln_b)),
                    (row(norm1_post_g), row(m_norm1_post_g), row(v_norm1_post_g)),
                    (row(norm2_pre_g), row(m_norm2_pre_g), row(v_norm2_pre_g)),
                    (row(norm2_post_g), row(m_norm2_post_g), row(v_norm2_post_g)),
                    (row(norm1_pre_g), row(m_norm1_pre_g), row(v_norm1_pre_g)),
                    (row(b_in), row(m_b_in), row(v_b_in)),
                    (ws2(w_spatial), ws2(m_w_spatial), ws2(v_w_spatial)),
                    (b_spatial, m_b_spatial, v_b_spatial),
                    (pg2(w_pool), pg2(m_w_pool), pg2(v_w_pool)),
                    (pg2(w_sgu_proj), pg2(m_w_sgu_proj), pg2(v_w_sgu_proj)),
                    (w_out, m_w_out, v_w_out)]
    small_out = _adamw_small(summed_in[3], [summed_out[1]] + list(summed_in[4:])
                             + [summed_in[1], summed_in[2], summed_out[0]], small_params)
    loss = small_out[0][0, 0]
    small_names = SMALL_ROWS[1:] + ("norm1_pre_g", "b_in", "w_spatial", "b_spatial", "w_pool", "w_sgu_proj", "w_out")
    small = {nm: small_out[1 + 4 * p:5 + 4 * p] for p, nm in enumerate(small_names)}

    shapes = {"norm1_pre_g": norm1_pre_g.shape, "w_in": w_in.shape, "b_in": b_in.shape, "w_pool": w_pool.shape,
              "pool_scale": pool_scale.shape, "sgu_ln_g": sgu_ln_g.shape, "sgu_ln_b": sgu_ln_b.shape,
              "w_spatial": w_spatial.shape, "b_spatial": b_spatial.shape, "w_sgu_proj": w_sgu_proj.shape,
              "w_out": w_out.shape, "norm1_post_g": norm1_post_g.shape, "norm2_pre_g": norm2_pre_g.shape,
              "w_ff1": w_ff1.shape, "w_ff2": w_ff2.shape, "norm2_post_g": norm2_post_g.shape}
    source = {"w_in": big["in"], "w_ff1": big["ff1"], "w_ff2": big["ff2"], **small}
    order = list(shapes)
    outs = [loss, dx.reshape(x.shape)]
    for kind in range(4):
        outs += [source[nm][kind].reshape(shapes[nm]) for nm in order]
    return tuple(outs)
```

```python
import math

import jax
import jax.numpy as jnp
from jax import lax
from jax.experimental import pallas as pl
from jax.experimental.pallas import tpu as pltpu

F32, BF16 = jnp.float32, jnp.bfloat16
MESH = pl.DeviceIdType.MESH

D_MODEL = 1024
D_IN = 5120
D_FF = 4096
N_DEV = 8
N_CHIPS = 4
WINDOWS = (2, 4, 8, 16)
N_GROUPS = 4
GROUP = 256
HALO = 16
SGU_BLOCK = 128
N_HEADS = 4
HEAD = 256
CHUNK = 64
EPS = 1e-6
IN_SHARD = D_IN // N_DEV
FF_SHARD = D_FF // N_DEV
OUT_SHARD = D_MODEL // N_DEV
PG_SHARD = GROUP // N_DEV

ADAM_LR, ADAM_B1, ADAM_B2, ADAM_EPS, ADAM_WD, ADAM_STEP = 0.001, 0.9, 0.999, 1e-08, 0.01, 10

VMEM_LIMIT = 56 * 1024 * 1024
TM = 256
TM_BWD = 256
TM_IN = 512
PROJ_COLS = 512
GELU_C0 = math.sqrt(2.0 / math.pi)
GELU_C1 = 0.044715


def _dot(a, b):
    return jnp.dot(a, b, preferred_element_type=F32)


def _dot_nt(a, b):
    return lax.dot_general(a, b, (((1,), (1,)), ((), ())), preferred_element_type=F32)


def _dot_tn(a, b):
    return lax.dot_general(a, b, (((0,), (0,)), ((), ())), preferred_element_type=F32)


def _gelu(x):
    t = jnp.tanh(GELU_C0 * (x + GELU_C1 * (x * x * x)))
    return 0.5 * x * (1.0 + t), t


def _gelu_grad(x, t):
    return 0.5 * (1.0 + t) + 0.5 * x * (1.0 - t * t) * (GELU_C0 * (1.0 + 3.0 * GELU_C1 * (x * x)))


def _sigmoid(x):
    return 1.0 / (1.0 + jnp.exp(-x))


def _mean(x):
    return jnp.mean(x, axis=-1, keepdims=True)


def _colsum(x):
    return jnp.sum(x, axis=0, keepdims=True)


def _const_spec(shape):
    nd = len(shape)
    return pl.BlockSpec(shape, lambda *_: (0,) * nd, pipeline_mode=pl.Buffered(1))


def _acc_spec(shape):
    nd = len(shape)
    return pl.BlockSpec(shape, lambda *_: (0,) * nd)


def _masked_ws(ws_ref):
    ri = lax.broadcasted_iota(jnp.int32, (SGU_BLOCK, SGU_BLOCK), 0) // CHUNK
    ci = lax.broadcasted_iota(jnp.int32, (SGU_BLOCK, SGU_BLOCK), 1) // CHUNK
    return [jnp.where(ri >= ci, ws_ref[h], 0.0).astype(BF16) for h in range(N_HEADS)]


def _pool_fwd(pbuf, tile_idx, tm):
    pos = lax.broadcasted_iota(jnp.int32, (tm, 1), 0) + tile_idx * tm + 1
    pooled = []
    for g, w in enumerate(WINDOWS):
        e = pbuf[:, g * GROUP:(g + 1) * GROUP]
        s, sh = e, 1
        while sh < w:
            s = s + pltpu.roll(s, sh, 0)
            sh *= 2
        inv = 1.0 / jnp.minimum(pos, w).astype(F32)
        pooled.append(s[HALO:] * inv - e[HALO:])
    return pooled


def _spatial_head(ws_h, vb, bsp_ref, h, nblk):
    return jnp.concatenate(
        [_dot(ws_h, vb[n * SGU_BLOCK:(n + 1) * SGU_BLOCK, h * HEAD:(h + 1) * HEAD]) + bsp_ref[:, h:h + 1]
         for n in range(nblk)], axis=0)


def _prenorm(after, x, g1pre):
    t_len = x.shape[0]
    tm = TM_IN

    def body(after_ref, x_ref, g_ref, xn_ref, xnt_ref):
        del after_ref
        xv = x_ref[...]
        xnb = (xv * lax.rsqrt(_mean(xv * xv) + EPS) * g_ref[...]).astype(BF16)
        xn_ref[...] = xnb
        xnt_ref[...] = xnb.T

    return pl.pallas_call(
        body, name="prenorm", grid=(t_len // tm,),
        in_specs=[_ANY, pl.BlockSpec((tm, D_MODEL), lambda i: (i, 0)), _const_spec((1, D_MODEL))],
        out_specs=[pl.BlockSpec((tm, D_MODEL), lambda i: (i, 0)), pl.BlockSpec((D_MODEL, tm), lambda i: (0, i))],
        out_shape=[jax.ShapeDtypeStruct((t_len, D_MODEL), BF16), jax.ShapeDtypeStruct((D_MODEL, t_len), BF16)],
        compiler_params=pltpu.CompilerParams(dimension_semantics=("arbitrary",)),
    )(after, x, g1pre)


def _fwd_mix(after, xn, x, win_g, b_in, wpool, pool_scale, lng, lnb, ws, bsp_t, wproj, wout, g1post):
    t_len = x.shape[0]
    tm = TM
    nt = t_len // tm

    def body(after_ref, xn_ref, xb_ref, win_ref, bin_ref, wpool_ref, ps_ref, lng_ref, lnb_ref, ws_ref, bsp_ref,
             wproj_ref, wout_ref, g1post_ref, z_ref, y_ref, h1_ref, u_ref, gpu_ref, xhat_ref, gpv_ref, sa_ref,
             sb_ref, zcur, znext, pbuf):
        del after_ref
        s = pl.program_id(0)

        @pl.when(s == 0)
        def _():
            znext[...] = jnp.zeros((tm, D_IN), F32)
            pbuf[...] = jnp.zeros((tm + HALO, D_MODEL), F32)

        zcur[...] = znext[...]
        xnb = xn_ref[...]

        def project(p):
            cols = slice(p * PROJ_COLS, (p + 1) * PROJ_COLS)
            zp = _dot(xnb, win_ref[:, cols]) + bin_ref[:, cols]
            if (p + 1) * PROJ_COLS <= D_MODEL:
                z_ref[:, cols] = zp
            znext[:, cols] = zp

        project(0)
        pbuf[0:HALO, :] = jnp.where(s <= 1, 0.0, pbuf[0:HALO, :])
        pbuf[HALO:, :] = zcur[:, 0:D_MODEL]
        pooled = _pool_fwd(pbuf, jnp.maximum(s - 1, 0), tm)
        pbuf[0:HALO, :] = pbuf[tm:tm + HALO, :]
        a = jnp.concatenate([_dot(pooled[g].astype(BF16), wpool_ref[g]) for g in range(N_GROUPS)], axis=1)
        a = a * ps_ref[...]
        project(1)
        zu = zcur[:, D_MODEL:2 * D_MODEL]
        u, tu = _gelu(zu)
        u_ref[...] = u.astype(BF16)
        gpu_ref[...] = _gelu_grad(zu, tu).astype(BF16)
        project(2)
        zv = zcur[:, 2 * D_MODEL:3 * D_MODEL]
        gv, tv = _gelu(zv)
        xc = gv - _mean(gv)
        rln = lax.rsqrt(_mean(xc * xc) + EPS)
        xhat = xc * rln
        xhat_ref[...] = xhat.astype(BF16)
        gpv_ref[...] = (_gelu_grad(zv, tv) * rln).astype(BF16)
        vb = (xhat * lng_ref[...] + lnb_ref[...]).astype(BF16)
        wsm = _masked_ws(ws_ref)
        bbr = []
        for h in range(N_HEADS):
            project(3 + h)
            sv = _spatial_head(wsm[h], vb, bsp_ref, h, tm // SGU_BLOCK)
            bbr.append(_dot((u[:, h * HEAD:(h + 1) * HEAD] * sv).astype(BF16), wproj_ref[h]))
        bbr = jnp.concatenate(bbr, axis=1)
        project(7)
        sa = _sigmoid(zcur[:, 3 * D_MODEL:4 * D_MODEL])
        sb = _sigmoid(zcur[:, 4 * D_MODEL:5 * D_MODEL])
        sa_ref[...] = sa.astype(BF16)
        sb_ref[...] = sb.astype(BF16)
        project(8)
        yv = _dot((sa * a + sb * bbr).astype(BF16), wout_ref[...])
        y_ref[...] = yv
        project(9)
        ry = lax.rsqrt(_mean(yv * yv) + EPS)
        h1_ref[...] = xb_ref[...] + yv * ry * g1post_ref[...]

    proj = lambda w: pl.BlockSpec((tm, w), lambda s: (jnp.minimum(s, nt - 1), 0))
    mix = lambda w: pl.BlockSpec((tm, w), lambda s: (jnp.maximum(s - 1, 0), 0))
    return pl.pallas_call(
        body, name="fwd_mix", grid=(nt + 1,),
        in_specs=[_ANY, proj(D_MODEL), mix(D_MODEL),
                  _const_spec((D_MODEL, D_IN)),
                  _const_spec((1, D_IN)), _const_spec((N_GROUPS, GROUP, GROUP)), _const_spec((1, D_MODEL)),
                  _const_spec((1, D_MODEL)), _const_spec((1, D_MODEL)),
                  _const_spec((N_HEADS, SGU_BLOCK, SGU_BLOCK)), _const_spec((SGU_BLOCK, N_HEADS)),
                  _const_spec((N_HEADS, HEAD, HEAD)), _const_spec((D_MODEL, D_MODEL)), _const_spec((1, D_MODEL))],
        out_specs=[proj(D_MODEL), mix(D_MODEL), mix(D_MODEL)] + [mix(D_MODEL)] * 6,
        out_shape=[jax.ShapeDtypeStruct((t_len, D_MODEL), F32),
                   jax.ShapeDtypeStruct((t_len, D_MODEL), F32), jax.ShapeDtypeStruct((t_len, D_MODEL), F32)]
        + [jax.ShapeDtypeStruct((t_len, D_MODEL), BF16)] * 6,
        scratch_shapes=[pltpu.VMEM((tm, D_IN), F32), pltpu.VMEM((tm, D_IN), F32),
                        pltpu.VMEM((tm + HALO, D_MODEL), F32)],
        compiler_params=pltpu.CompilerParams(dimension_semantics=("arbitrary",), vmem_limit_bytes=VMEM_LIMIT),
    )(after, xn, x, win_g, b_in, wpool, pool_scale, lng, lnb, ws, bsp_t, wproj, wout, g1post)


def _mlp(h1, target, g2pre, g2post, w1_g, w2):
    t_len = h1.shape[0]
    tm = TM
    nt = t_len // tm

    def body(h1_ref, tgt_ref, g2pre_ref, g2post_ref, w1_ref, w2_ref,
             hn_ref, f_ref, df1_ref, df2_ref, dh1_ref, dg2post_ref, dg2pre_ref, loss_ref, f1_scr):
        i = pl.program_id(0)

        @pl.when(i == 0)
        def _():
            dg2post_ref[...] = jnp.zeros_like(dg2post_ref)
            dg2pre_ref[...] = jnp.zeros_like(dg2pre_ref)
            loss_ref[...] = jnp.zeros_like(loss_ref)

        h = h1_ref[...]
        r2 = lax.rsqrt(_mean(h * h) + EPS)
        nh = h * r2
        hnb = (nh * g2pre_ref[...]).astype(BF16)
        hn_ref[...] = hnb.T
        for k in range(N_DEV):
            f1_scr[:, k * FF_SHARD:(k + 1) * FF_SHARD] = _dot(hnb, w1_ref[:, k * FF_SHARD:(k + 1) * FF_SHARD])
        r = jnp.maximum(f1_scr[...], 0.0)
        fb = (r * r).astype(BF16)
        f_ref[...] = fb.T
        f2 = _dot(fb, w2_ref[...])
        rf = lax.rsqrt(_mean(f2 * f2) + EPS)
        nf = f2 * rf
        diff = h + nf * g2post_ref[...] - tgt_ref[...]
        loss_ref[...] += (0.5 / D_MODEL) * jnp.sum(diff * diff)
        dout = diff * (1.0 / D_MODEL)
        dg2post_ref[...] += _colsum(dout * nf)
        dn = dout * g2post_ref[...]
        df2b = (rf * (dn - nf * _mean(dn * nf))).astype(BF16)
        df2_ref[...] = df2b
        df = _dot_nt(df2b, w2_ref[...])
        df1b = (df * (2.0 * jnp.maximum(f1_scr[...], 0.0))).astype(BF16)
        df1_ref[...] = df1b
        dhn = _dot_nt(df1b, w1_ref[...])
        dg2pre_ref[...] += _colsum(dhn * nh)
        dnh = dhn * g2pre_ref[...]
        dh1_ref[...] = dout + r2 * (dnh - nh * _mean(dnh * nh))

    tok = lambda w: pl.BlockSpec((tm, w), lambda i: (i, 0))
    return pl.pallas_call(
        body, name="mlp_fwd_bwd", grid=(nt,),
        in_specs=[tok(D_MODEL), tok(D_MODEL), _const_spec((1, D_MODEL)), _const_spec((1, D_MODEL)),
                  _const_spec((D_MODEL, D_FF)), _const_spec((D_FF, D_MODEL))],
        out_specs=[pl.BlockSpec((D_MODEL, tm), lambda i: (0, i)), pl.BlockSpec((D_FF, tm), lambda i: (0, i)),
                   tok(D_FF), tok(D_MODEL), tok(D_MODEL),
                   _acc_spec((1, D_MODEL)), _acc_spec((1, D_MODEL)), _acc_spec((1, 128))],
        out_shape=[jax.ShapeDtypeStruct((D_MODEL, t_len), BF16), jax.ShapeDtypeStruct((D_FF, t_len), BF16),
                   jax.ShapeDtypeStruct((t_len, D_FF), BF16), jax.ShapeDtypeStruct((t_len, D_MODEL), BF16),
                   jax.ShapeDtypeStruct((t_len, D_MODEL), F32), jax.ShapeDtypeStruct((1, D_MODEL), F32),
                   jax.ShapeDtypeStruct((1, D_MODEL), F32), jax.ShapeDtypeStruct((1, 128), F32)],
        scratch_shapes=[pltpu.VMEM((tm, D_FF), F32)],
        compiler_params=pltpu.CompilerParams(dimension_semantics=("arbitrary",), vmem_limit_bytes=VMEM_LIMIT),
    )(h1, target, g2pre, g2post, w1_g, w2)


def _bwd_mix(after, dh1, y, z, saved, wpool, pool_scale, lng, lnb, ws, bsp_t, wproj, wout, g1post):
    t_len = y.shape[0]
    tm = TM_BWD
    nt = t_len // tm
    nblk = tm // SGU_BLOCK

    def body(after_ref, dh1_ref, y_ref, z_ref, zh_ref, u_ref, gpu_ref, xhat_ref, gpv_ref, sa_ref, sb_ref,
             wpool_ref, ps_ref, lng_ref, lnb_ref, ws_ref,
             bsp_ref, wproj_ref, wout_ref, g1post_ref,
             dz_ref, mg_ref, dy_ref, ppool_ref, pproj_ref, dws_ref, dbsp_ref, dg1post_ref, dps_ref,
             dlng_ref, dlnb_ref, dbin_ref, pbuf, qbuf, dwpool_ref, dwproj_ref):
        del after_ref
        i = pl.program_id(0)
        ti = nt - 1 - i

        @pl.when(i == 0)
        def _():
            for ref in (dwpool_ref, dwproj_ref, dws_ref, dbsp_ref, dg1post_ref, dps_ref, dlng_ref, dlnb_ref,
                        dbin_ref):
                ref[...] = jnp.zeros_like(ref)
            qbuf[tm:tm + HALO, :] = jnp.zeros((HALO, D_MODEL), F32)

        pbuf[0:HALO, :] = jnp.where(ti > 0, zh_ref[...], 0.0)
        pbuf[HALO:, :] = z_ref[...]
        pooled = _pool_fwd(pbuf, ti, tm)
        pooled_b = [p.astype(BF16) for p in pooled]
        a_raw = jnp.concatenate([_dot(pooled_b[g], wpool_ref[g]) for g in range(N_GROUPS)], axis=1)
        wsm = _masked_ws(ws_ref)
        u = u_ref[...].astype(F32)
        xhat = xhat_ref[...].astype(F32)
        vb = (xhat * lng_ref[...] + lnb_ref[...]).astype(BF16)
        sv_heads = [_spatial_head(wsm[h], vb, bsp_ref, h, nblk) for h in range(N_HEADS)]
        gated_b = [(u[:, h * HEAD:(h + 1) * HEAD] * sv_heads[h]).astype(BF16) for h in range(N_HEADS)]
        bbr = jnp.concatenate([_dot(gated_b[h], wproj_ref[h]) for h in range(N_HEADS)], axis=1)
        sa = sa_ref[...].astype(F32)
        sb = sb_ref[...].astype(F32)
        a = a_raw * ps_ref[...]
        mg_ref[...] = (sa * a + sb * bbr).astype(BF16).T

        dh = dh1_ref[...]
        yv = y_ref[...]
        ry = lax.rsqrt(_mean(yv * yv) + EPS)
        ny = yv * ry
        dg1post_ref[...] += _colsum(dh * ny)
        dn = dh * g1post_ref[...]
        dyb = (ry * (dn - ny * _mean(dn * ny))).astype(BF16)
        dy_ref[...] = dyb
        dmg = _dot_nt(dyb, wout_ref[...])

        da = dmg * sa
        dbbr = dmg * sb
        dzga = dmg * a * sa * (1.0 - sa)
        dzgb = dmg * bbr * sb * (1.0 - sb)
        dz_ref[:, 3 * D_MODEL:4 * D_MODEL] = dzga.astype(BF16)
        dz_ref[:, 4 * D_MODEL:5 * D_MODEL] = dzgb.astype(BF16)
        dbin_ref[:, 3 * D_MODEL:4 * D_MODEL] += _colsum(dzga)
        dbin_ref[:, 4 * D_MODEL:5 * D_MODEL] += _colsum(dzgb)

        dps_ref[...] += _colsum(da * a_raw)
        da_raw_b = (da * ps_ref[...]).astype(BF16)
        pos = lax.broadcasted_iota(jnp.int32, (tm, 1), 0) + ti * tm + 1
        dpooled = []
        for g, w in enumerate(WINDOWS):
            cols = slice(g * GROUP, (g + 1) * GROUP)
            dwpool_ref[g] += _dot_tn(pooled_b[g], da_raw_b[:, cols])
            dp = _dot_nt(da_raw_b[:, cols], wpool_ref[g])
            dpooled.append(dp)
            qbuf[0:tm, cols] = dp * (1.0 / jnp.minimum(pos, w).astype(F32))
        n_ext = tm + HALO
        dzp = []
        for g, w in enumerate(WINDOWS):
            e = qbuf[:, g * GROUP:(g + 1) * GROUP]
            s, sh = e, 1
            while sh < w:
                s = s + pltpu.roll(s, n_ext - sh, 0)
                sh *= 2
            dzp.append(s[0:tm] - dpooled[g])
        qbuf[tm:tm + HALO, :] = qbuf[0:HALO, :]
        dzp = jnp.concatenate(dzp, axis=1)
        dz_ref[:, 0:D_MODEL] = dzp.astype(BF16)
        dbin_ref[:, 0:D_MODEL] += _colsum(dzp)

        dv_heads = []
        du_heads = []
        for h in range(N_HEADS):
            cols = slice(h * HEAD, (h + 1) * HEAD)
            dbbr_b = dbbr[:, cols].astype(BF16)
            dwproj_ref[h] += _dot_tn(gated_b[h], dbbr_b)
            dgated = _dot_nt(dbbr_b, wproj_ref[h])
            du_heads.append(dgated * sv_heads[h])
            dsv = dgated * u[:, cols]
            dsv_b = dsv.astype(BF16)
            rows = []
            for n in range(nblk):
                blk = slice(n * SGU_BLOCK, (n + 1) * SGU_BLOCK)
                rows.append(_dot_tn(wsm[h], dsv_b[blk]))
                dws_ref[h] += _dot_nt(dsv_b[blk], vb[blk, cols])
                dbsp_ref[:, h:h + 1] += jnp.sum(dsv[blk], axis=1, keepdims=True)
            dv_heads.append(jnp.concatenate(rows, axis=0))
        dzu = jnp.concatenate(du_heads, axis=1) * gpu_ref[...].astype(F32)
        dz_ref[:, D_MODEL:2 * D_MODEL] = dzu.astype(BF16)
        dbin_ref[:, D_MODEL:2 * D_MODEL] += _colsum(dzu)
        dv = jnp.concatenate(dv_heads, axis=1)
        dlng_ref[...] += _colsum(dv * xhat)
        dlnb_ref[...] += _colsum(dv)
        dxh = dv * lng_ref[...]
        dzv = (dxh - _mean(dxh) - xhat * _mean(dxh * xhat)) * gpv_ref[...].astype(F32)
        dz_ref[:, 2 * D_MODEL:3 * D_MODEL] = dzv.astype(BF16)
        dbin_ref[:, 2 * D_MODEL:3 * D_MODEL] += _colsum(dzv)

        @pl.when(i == nt - 1)
        def _():
            ri = lax.broadcasted_iota(jnp.int32, (SGU_BLOCK, SGU_BLOCK), 0) // CHUNK
            ci = lax.broadcasted_iota(jnp.int32, (SGU_BLOCK, SGU_BLOCK), 1) // CHUNK
            for h in range(N_HEADS):
                dws_ref[h] = jnp.where(ri >= ci, dws_ref[h], 0.0)
            for slot in range(N_DEV):
                own = slice(_owner_of_slot(slot) * PG_SHARD, (_owner_of_slot(slot) + 1) * PG_SHARD)
                for g in range(N_GROUPS):
                    rows = slice(g * PG_SHARD, (g + 1) * PG_SHARD)
                    ppool_ref[slot, rows, :] = dwpool_ref[g, own, :].astype(BF16)
                    pproj_ref[slot, rows, :] = dwproj_ref[g, own, :].astype(BF16)

    tok = lambda w: pl.BlockSpec((tm, w), lambda i: (nt - 1 - i, 0))
    halo = pl.BlockSpec((HALO, D_MODEL), lambda i: (jnp.maximum((nt - 1 - i) * (tm // HALO) - 1, 0), 0))
    return pl.pallas_call(
        body, name="bwd_mix", grid=(nt,),
        in_specs=[_ANY, tok(D_MODEL), tok(D_MODEL), tok(D_MODEL), halo] + [tok(D_MODEL)] * 6
        + [_const_spec((N_GROUPS, GROUP, GROUP)),
                  _const_spec((1, D_MODEL)), _const_spec((1, D_MODEL)), _const_spec((1, D_MODEL)),
                  _const_spec((N_HEADS, SGU_BLOCK, SGU_BLOCK)), _const_spec((SGU_BLOCK, N_HEADS)),
                  _const_spec((N_HEADS, HEAD, HEAD)), _const_spec((D_MODEL, D_MODEL)), _const_spec((1, D_MODEL))],
        out_specs=[tok(D_IN), pl.BlockSpec((D_MODEL, tm), lambda i: (0, nt - 1 - i)), tok(D_MODEL),
                   _acc_spec((N_DEV, N_GROUPS * PG_SHARD, GROUP)), _acc_spec((N_DEV, N_HEADS * PG_SHARD, HEAD)),
                   _acc_spec((N_HEADS, SGU_BLOCK, SGU_BLOCK)), _acc_spec((SGU_BLOCK, N_HEADS)),
                   _acc_spec((1, D_MODEL)), _acc_spec((1, D_MODEL)), _acc_spec((1, D_MODEL)), _acc_spec((1, D_MODEL)),
                   _acc_spec((1, D_IN))],
        out_shape=[jax.ShapeDtypeStruct((t_len, D_IN), BF16),
                   jax.ShapeDtypeStruct((D_MODEL, t_len), BF16), jax.ShapeDtypeStruct((t_len, D_MODEL), BF16),
                   jax.ShapeDtypeStruct((N_DEV, N_GROUPS * PG_SHARD, GROUP), BF16),
                   jax.ShapeDtypeStruct((N_DEV, N_HEADS * PG_SHARD, HEAD), BF16),
                   jax.ShapeDtypeStruct((N_HEADS, SGU_BLOCK, SGU_BLOCK), F32),
                   jax.ShapeDtypeStruct((SGU_BLOCK, N_HEADS), F32),
                   jax.ShapeDtypeStruct((1, D_MODEL), F32), jax.ShapeDtypeStruct((1, D_MODEL), F32),
                   jax.ShapeDtypeStruct((1, D_MODEL), F32), jax.ShapeDtypeStruct((1, D_MODEL), F32),
                   jax.ShapeDtypeStruct((1, D_IN), F32)],
        scratch_shapes=[pltpu.VMEM((tm + HALO, D_MODEL), F32), pltpu.VMEM((tm + HALO, D_MODEL), F32),
                        pltpu.VMEM((N_GROUPS, GROUP, GROUP), F32), pltpu.VMEM((N_HEADS, HEAD, HEAD), F32)],
        compiler_params=pltpu.CompilerParams(dimension_semantics=("arbitrary",), vmem_limit_bytes=VMEM_LIMIT),
    )(after, dh1, y, z, z, *saved, wpool, pool_scale, lng, lnb, ws, bsp_t, wproj, wout, g1post)


def _bwd_in(after, dz_b, x, dh1, g1pre, win_g):
    t_len = x.shape[0]
    tm = TM_IN
    nt = t_len // tm

    def body(after_ref, dz_ref, x_ref, dh1_ref, g1_ref, win_ref, dx_ref, dg1pre_ref):
        del after_ref

        @pl.when(pl.program_id(0) == 0)
        def _():
            dg1pre_ref[...] = jnp.zeros_like(dg1pre_ref)

        dxn = _dot_nt(dz_ref[...], win_ref[...])
        xv = x_ref[...]
        r1 = lax.rsqrt(_mean(xv * xv) + EPS)
        nx = xv * r1
        dg1pre_ref[...] += _colsum(dxn * nx)
        dnx = dxn * g1_ref[...]
        dx_ref[...] = r1 * (dnx - nx * _mean(dnx * nx)) + dh1_ref[...]

    tok = lambda w: pl.BlockSpec((tm, w), lambda i: (i, 0))
    return pl.pallas_call(
        body, name="bwd_in", grid=(nt,),
        in_specs=[_ANY, tok(D_IN), tok(D_MODEL), tok(D_MODEL), _const_spec((1, D_MODEL)),
                  _const_spec((D_MODEL, D_IN))],
        out_specs=[tok(D_MODEL), _acc_spec((1, D_MODEL))],
        out_shape=[jax.ShapeDtypeStruct((t_len, D_MODEL), F32), jax.ShapeDtypeStruct((1, D_MODEL), F32)],
        compiler_params=pltpu.CompilerParams(dimension_semantics=("arbitrary",), vmem_limit_bytes=VMEM_LIMIT),
    )(after, dz_b, x, dh1, g1pre, win_g)


def _owner_of_slot(s):
    return 4 * ((s // 2) % 2) + 2 * (s % 2) + s // 4


def _wgrad_ff(f_t, df2_b, hn_t, df1_b):
    t_len = df2_b.shape[0]

    def body(f_ref, df2_ref, hn_ref, df1_ref, o2_ref, o1_ref):
        o2_ref[...] = _dot(f_ref[...], df2_ref[...]).astype(BF16)
        o1_ref[...] = _dot(hn_ref[...], df1_ref[...]).astype(BF16)

    return pl.pallas_call(
        body, name="wgrad_ff", grid=(N_DEV,),
        in_specs=[pl.BlockSpec((FF_SHARD, t_len), lambda s: (_owner_of_slot(s), 0)),
                  pl.BlockSpec((t_len, D_MODEL), lambda s: (0, 0), pipeline_mode=pl.Buffered(1)),
                  pl.BlockSpec((D_MODEL, t_len), lambda s: (0, 0), pipeline_mode=pl.Buffered(1)),
                  pl.BlockSpec((t_len, FF_SHARD), lambda s: (0, _owner_of_slot(s)))],
        out_specs=[pl.BlockSpec((None, FF_SHARD, D_MODEL), lambda s: (s, 0, 0)),
                   pl.BlockSpec((None, D_MODEL, FF_SHARD), lambda s: (s, 0, 0))],
        out_shape=[jax.ShapeDtypeStruct((N_DEV, FF_SHARD, D_MODEL), BF16),
                   jax.ShapeDtypeStruct((N_DEV, D_MODEL, FF_SHARD), BF16)],
        compiler_params=pltpu.CompilerParams(dimension_semantics=("arbitrary",), vmem_limit_bytes=VMEM_LIMIT),
    )(f_t, df2_b, hn_t, df1_b)


def _wgrad_in(xn_t, dz_b):
    t_len = dz_b.shape[0]

    def body(a_ref, b_ref, o_ref):
        res = _dot(a_ref[...], b_ref[...])
        o_ref[0] = res[:, 0:IN_SHARD].astype(BF16)
        o_ref[1] = res[:, IN_SHARD:2 * IN_SHARD].astype(BF16)

    out = pl.pallas_call(
        body, name="wgrad_in", grid=(N_CHIPS,),
        in_specs=[pl.BlockSpec((D_MODEL, t_len), lambda q: (0, 0), pipeline_mode=pl.Buffered(1)),
                  pl.BlockSpec((t_len, 2 * IN_SHARD), lambda q: (0, q))],
        out_specs=pl.BlockSpec((2, None, D_MODEL, IN_SHARD), lambda q: (0, q, 0, 0)),
        out_shape=jax.ShapeDtypeStruct((2, N_CHIPS, D_MODEL, IN_SHARD), BF16),
        compiler_params=pltpu.CompilerParams(dimension_semantics=("arbitrary",), vmem_limit_bytes=VMEM_LIMIT),
    )(xn_t, dz_b)
    return out.reshape(N_DEV, D_MODEL, IN_SHARD)


def _wgrad_out(mg_t, dy_b):
    t_len = dy_b.shape[0]

    def body(a_ref, b_ref, o_ref):
        res = _dot(a_ref[...], b_ref[...])
        o_ref[0] = res[0:OUT_SHARD].astype(BF16)
        o_ref[1] = res[OUT_SHARD:2 * OUT_SHARD].astype(BF16)

    out = pl.pallas_call(
        body, name="wgrad_out", grid=(N_CHIPS,),
        in_specs=[pl.BlockSpec((2 * OUT_SHARD, t_len), lambda q: (q, 0)),
                  pl.BlockSpec((t_len, D_MODEL), lambda q: (0, 0), pipeline_mode=pl.Buffered(1))],
        out_specs=pl.BlockSpec((2, None, OUT_SHARD, D_MODEL), lambda q: (0, q, 0, 0)),
        out_shape=jax.ShapeDtypeStruct((2, N_CHIPS, OUT_SHARD, D_MODEL), BF16),
        compiler_params=pltpu.CompilerParams(dimension_semantics=("arbitrary",), vmem_limit_bytes=VMEM_LIMIT),
    )(mg_t, dy_b)
    return out.reshape(N_DEV, OUT_SHARD, D_MODEL)


def _coords():
    return lax.axis_index("x"), lax.axis_index("y"), lax.axis_index("c")


_ANY = pl.BlockSpec(memory_space=pl.ANY)


def _pair_exchange(name, bigs, smalls):
    nb, n = len(bigs), len(bigs) + len(smalls)

    def body(*refs):
        src, dst = refs[:n], refs[n:2 * n]
        send_sems, recv_sems = refs[2 * n:]
        x, y, c = _coords()
        copies = []
        for j in range(n):
            s = src[j].at[pl.ds(4 * (1 - c), 4)] if j < nb else src[j]
            cp = pltpu.make_async_remote_copy(src_ref=s, dst_ref=dst[j], send_sem=send_sems.at[j],
                                              recv_sem=recv_sems.at[j], device_id=(x, y, 1 - c), device_id_type=MESH)
            cp.start()
            copies.append(cp)
        for cp in copies:
            cp.wait()

    return pl.pallas_call(
        body, name=name,
        in_specs=[_ANY] * n, out_specs=[_ANY] * n,
        out_shape=[jax.ShapeDtypeStruct((4,) + b.shape[1:], b.dtype) for b in bigs]
        + [jax.ShapeDtypeStruct(s.shape, s.dtype) for s in smalls],
        scratch_shapes=[pltpu.SemaphoreType.DMA((n,)), pltpu.SemaphoreType.DMA((n,))],
    )(*bigs, *smalls)


_HBM = pl.BlockSpec(memory_space=pltpu.HBM)
_SEM = pl.BlockSpec(memory_space=pltpu.SEMAPHORE)
_VMEM = pl.BlockSpec(memory_space=pltpu.VMEM)
_EFFECT = pltpu.SideEffectType.DATAFLOW_SIDE_EFFECTING
_TOKEN = jax.ShapeDtypeStruct((8, 128), F32)


def _in_hbm(a):
    return pltpu.with_memory_space_constraint(a, pltpu.HBM)


def _split_call(name, body, n_sems_out, arrays, sems_in=(), after=None):
    na, ns = len(arrays), len(sems_in)
    has_after = after is not None

    def kernel_body(*refs):
        arr = refs[:na]
        s_in = refs[na:na + ns]
        outs = refs[na + ns + has_after:]
        body(arr, s_in, outs[:n_sems_out])
        outs[-1][...] = jnp.zeros((8, 128), F32)

    out_shape = ([pltpu.SemaphoreType.DMA(())] * n_sems_out + [pltpu.HBM(a.shape, a.dtype) for a in arrays] + [_TOKEN])
    res = pl.pallas_call(
        kernel_body, name=name, out_shape=out_shape,
        in_specs=[_HBM] * na + [_SEM] * ns + [_ANY] * has_after,
        out_specs=[_SEM] * n_sems_out + [_HBM] * na + [_VMEM],
        input_output_aliases={i: n_sems_out + i for i in range(na)},
        compiler_params=pltpu.CompilerParams(has_side_effects=_EFFECT),
    )(*[_in_hbm(a) for a in arrays], *sems_in, *([after] if has_after else []))
    return list(res[:n_sems_out]), list(res[n_sems_out:n_sems_out + na]), res[-1]


def _wait_bytes_of(ref, send_sem, recv_sem, peer, send=True, recv=True):
    cp = pltpu.make_async_remote_copy(src_ref=ref, dst_ref=ref, send_sem=send_sem, recv_sem=recv_sem,
                                      device_id=peer, device_id_type=MESH)
    if send:
        cp.wait_send()
    if recv:
        cp.wait_recv()


def _gather_behind(tag, shards, after, work, by_columns=(), by_rows=()):
    n = len(shards)

    def land_shape(j, s):
        if j in by_columns:
            return (s.shape[0], N_DEV * s.shape[1])
        if j in by_rows:
            return (s.shape[0], N_DEV * s.shape[1], s.shape[2])
        return (N_DEV,) + s.shape

    lands = [lax.empty(land_shape(j, s), s.dtype) for j, s in enumerate(shards)]

    def slots(arr, j, first, count=1):
        if j in by_columns:
            cols = shards[j].shape[1]
            return arr[n + j].at[:, pl.ds(pl.multiple_of(first * cols, 128), count * cols)]
        if j in by_rows:
            rows = shards[j].shape[1]
            return arr[n + j].at[:, pl.ds(pl.multiple_of(first * rows, 16), count * rows), :]
        return arr[n + j].at[first] if count == 1 else arr[n + j].at[pl.ds(first, count)]

    def own_slot(arr, j, sem):
        x, y, c = _coords()
        return pltpu.make_async_copy(arr[j], slots(arr, j, 4 * x + 2 * y + c), sem)

    def start(arr, _, sems):
        x, y, c = _coords()
        me = 4 * x + 2 * y + c
        for j in range(n):
            for chip in [(1 - x, y), (x, 1 - y), (1 - x, 1 - y)]:
                pltpu.make_async_remote_copy(src_ref=arr[j], dst_ref=slots(arr, j, me), send_sem=sems[j],
                                             recv_sem=sems[n + j], device_id=(*chip, c), device_id_type=MESH).start()
        for j in range(n):
            pltpu.make_async_remote_copy(src_ref=arr[j], dst_ref=slots(arr, j, me), send_sem=sems[2 * n + j],
                                         recv_sem=sems[3 * n + j], device_id=(x, y, 1 - c),
                                         device_id_type=MESH).start()
            own_slot(arr, j, sems[4 * n + j]).start()

    def middle(arr, s_in, sems):
        x, y, c = _coords()
        sibling = (x, y, 1 - c)
        for j in range(n):
            _wait_bytes_of(slots(arr, j, 0, 3), s_in[j], s_in[n + j], sibling)
            for chip in [(1 - x, y), (x, 1 - y), (1 - x, 1 - y)]:
                slot = slots(arr, j, 4 * chip[0] + 2 * chip[1] + c)
                pltpu.make_async_remote_copy(src_ref=slot, dst_ref=slot, send_sem=sems[j], recv_sem=sems[n + j],
                                             device_id=sibling, device_id_type=MESH).start()

    def finish(arr, s_in, _):
        x, y, c = _coords()
        sibling = (x, y, 1 - c)
        for j in range(n):
            _wait_bytes_of(slots(arr, j, 0, 1), s_in[j], s_in[n + j], sibling)
            own_slot(arr, j, s_in[2 * n + j]).wait()
            _wait_bytes_of(slots(arr, j, 0, 3), s_in[3 * n + j], s_in[4 * n + j], sibling)

    sems, arrays, token = _split_call("gather_%s_start" % tag, start, 5 * n, list(shards) + lands, after=after)
    result = work(token)
    fwd_sems, arrays, token = _split_call("gather_%s_middle" % tag, middle, 2 * n, arrays, sems_in=sems[:2 * n],
                                          after=result[0])
    _, arrays, _ = _split_call("gather_%s_finish" % tag, finish, 0, arrays, sems_in=sems[2 * n:] + fwd_sems,
                               after=token)
    return arrays[n:], result


def _chip_exchange_behind(tag, bigs, smalls, work):
    nb, n = len(bigs), len(bigs) + len(smalls)
    lands = [lax.empty(s.shape, s.dtype) for s in bigs] + [lax.empty((N_CHIPS,) + s.shape, s.dtype) for s in smalls]

    def own_slot(arr, j, sem):
        x, y, _ = _coords()
        q_me = 2 * x + y
        return pltpu.make_async_copy(arr[j].at[q_me] if j < nb else arr[j], arr[n + j].at[q_me], sem)

    def start(arr, _, sems):
        x, y, c = _coords()
        q_me = 2 * x + y
        for j in range(n):
            for peer in [(1 - x, y, c), (x, 1 - y, c), (1 - x, 1 - y, c)]:
                piece = arr[j].at[2 * peer[0] + peer[1]] if j < nb else arr[j]
                pltpu.make_async_remote_copy(src_ref=piece, dst_ref=arr[n + j].at[q_me],
                                             send_sem=sems[j], recv_sem=sems[n + j], device_id=peer,
                                             device_id_type=MESH).start()
            own_slot(arr, j, sems[2 * n + j]).start()

    def finish(arr, s_in, _):
        x, y, c = _coords()
        for j in range(n):
            _wait_bytes_of(arr[n + j].at[pl.ds(0, 3)], s_in[j], s_in[n + j], (x, y, 1 - c))
            own_slot(arr, j, s_in[2 * n + j]).wait()

    sems, arrays, token = _split_call("chip_exchange_%s_start" % tag, start, 3 * n, list(bigs) + list(smalls) + lands)
    result = work(token)
    _, arrays, _ = _split_call("chip_exchange_%s_finish" % tag, finish, 0, arrays, sems_in=sems, after=result[0])
    return arrays[n:], result


def _pair_sum_big(name, c_arr, mine, theirs, whole_mine=(), whole_theirs=()):
    n, nw = len(mine), len(whole_mine)

    def body(c_ref, *refs):
        del c_ref
        ins, outs = refs[:2 * (n + nw)], refs[2 * (n + nw):]
        for j in range(n):
            outs[j][...] = (ins[j][...].astype(F32) + ins[n + j][...].astype(F32)).astype(BF16)

        @pl.when(pl.program_id(0) == 0)
        def _():
            for j in range(nw):
                outs[n + j][...] = ins[2 * n + j][...] + ins[2 * n + nw + j][...]

    own = lambda t: pl.BlockSpec((None,) + t.shape[1:], lambda q, c_ref: (4 * c_ref[0] + q, 0, 0))
    slot = lambda t: pl.BlockSpec((None,) + t.shape[1:], lambda q, c_ref: (q, 0, 0))
    whole = lambda t: pl.BlockSpec(t.shape, lambda q, c_ref: (0,) * t.ndim)
    return pl.pallas_call(
        body, name=name,
        grid_spec=pltpu.PrefetchScalarGridSpec(
            num_scalar_prefetch=1, grid=(N_CHIPS,),
            in_specs=[own(t) for t in theirs] + [slot(t) for t in theirs]
            + [whole(t) for t in whole_mine] + [whole(t) for t in whole_theirs],
            out_specs=[slot(t) for t in theirs] + [whole(t) for t in whole_mine]),
        out_shape=[jax.ShapeDtypeStruct(t.shape, BF16) for t in theirs]
        + [jax.ShapeDtypeStruct(t.shape, t.dtype) for t in whole_mine],
        compiler_params=pltpu.CompilerParams(dimension_semantics=("arbitrary",)),
    )(c_arr, *mine, *theirs, *whole_mine, *whole_theirs)


def _adamw_math(w, g, m, v):
    m = ADAM_B1 * m + (1.0 - ADAM_B1) * g
    v = ADAM_B2 * v + (1.0 - ADAM_B2) * (g * g)
    m_hat = m / (1.0 - ADAM_B1 ** ADAM_STEP)
    v_hat = v / (1.0 - ADAM_B2 ** ADAM_STEP)
    delta = -ADAM_LR * (m_hat / (jnp.sqrt(v_hat) + ADAM_EPS) + ADAM_WD * w)
    return delta, m, v


ADAMW_STEPS = 2


def _adamw_big(name, after, chip_sums, params):
    n = len(params)

    def body(after_ref, *refs):
        del after_ref
        outs = refs[4 * n:]
        for j in range(n):
            t_ref, w_ref, m_ref, v_ref = refs[4 * j:4 * j + 4]
            g = t_ref[0].astype(F32)
            for q in range(1, N_CHIPS):
                g = g + t_ref[q].astype(F32)
            d, mn, vn = _adamw_math(w_ref[...], g, m_ref[...], v_ref[...])
            for out, val in zip(outs[4 * j:4 * j + 4], (g, d, mn, vn)):
                out[...] = val

    in_specs, out_specs, out_shape, operands = [_ANY], [], [], [after]
    for t, (w, m, v) in zip(chip_sums, params):
        rows, cols = w.shape
        tr = rows // ADAMW_STEPS
        blk = pl.BlockSpec((tr, cols), lambda r: (r, 0))
        in_specs += [pl.BlockSpec((N_CHIPS, tr, cols), lambda r: (0, r, 0)), blk, blk, blk]
        out_specs += [blk] * 4
        out_shape += [jax.ShapeDtypeStruct((rows, cols), F32)] * 4
        operands += [t, w, m, v]
    res = pl.pallas_call(
        body, name=name, grid=(ADAMW_STEPS,), in_specs=in_specs, out_specs=out_specs, out_shape=out_shape,
        compiler_params=pltpu.CompilerParams(dimension_semantics=("arbitrary",), vmem_limit_bytes=VMEM_LIMIT),
    )(*operands)
    return [res[4 * j:4 * j + 4] for j in range(n)]


SMALL_ROWS = ("loss", "pool_scale", "sgu_ln_g", "sgu_ln_b", "norm1_post_g", "norm2_pre_g", "norm2_post_g")


N_ROW_PARAMS = len(SMALL_ROWS) - 1


def _adamw_small(u_rows, u_others, params):
    n, n_oth = len(params), len(u_others)

    def body(*refs):
        urow_ref, others = refs[0], refs[1:1 + n_oth]
        wmv = refs[1 + n_oth:1 + n_oth + 3 * n]
        loss_ref = refs[1 + n_oth + 3 * n]
        outs = refs[2 + n_oth + 3 * n:]

        def total(ref, idx):
            g = ref[(0,) + idx].astype(F32)
            for q in range(1, N_CHIPS):
                g = g + ref[(q,) + idx].astype(F32)
            return g

        loss_ref[...] = total(urow_ref, (slice(0, 1), slice(None)))
        for p in range(n):
            if p < N_ROW_PARAMS:
                g = total(urow_ref, (slice(p + 1, p + 2), slice(None)))
            else:
                g = total(others[p - N_ROW_PARAMS], (slice(None), slice(None)))
            d, mn, vn = _adamw_math(wmv[3 * p][...], g, wmv[3 * p + 1][...], wmv[3 * p + 2][...])
            outs[4 * p][...] = g
            outs[4 * p + 1][...] = d
            outs[4 * p + 2][...] = mn
            outs[4 * p + 3][...] = vn

    flat = [a for p in params for a in p]
    out_shape = [jax.ShapeDtypeStruct((1, D_MODEL), F32)]
    for w, _, _ in params:
        out_shape += [jax.ShapeDtypeStruct(w.shape, F32)] * 4
    return pl.pallas_call(body, name="adamw_small", out_shape=out_shape)(u_rows, *u_others, *flat)


def _to_bf16(arrays):
    n = len(arrays)

    def body(*refs):
        for j in range(n):
            refs[n + j][...] = refs[j][...].astype(BF16)

    return pl.pallas_call(
        body, name="shards_to_bf16", out_shape=[jax.ShapeDtypeStruct(a.shape, BF16) for a in arrays],
    )(*arrays)


def kernel(x, norm1_pre_g, w_in, b_in, w_pool, pool_scale, sgu_ln_g, sgu_ln_b, w_spatial, b_spatial, w_sgu_proj, w_out, norm1_post_g, norm2_pre_g, w_ff1, w_ff2, norm2_post_g, loss_target, m_norm1_pre_g, m_w_in, m_b_in, m_w_pool, m_pool_scale, m_sgu_ln_g, m_sgu_ln_b, m_w_spatial, m_b_spatial, m_w_sgu_proj, m_w_out, m_norm1_post_g, m_norm2_pre_g, m_w_ff1, m_w_ff2, m_norm2_post_g, v_norm1_pre_g, v_w_in, v_b_in, v_w_pool, v_pool_scale, v_sgu_ln_g, v_sgu_ln_b, v_w_spatial, v_b_spatial, v_w_sgu_proj, v_w_out, v_norm1_post_g, v_norm2_pre_g, v_w_ff1, v_w_ff2, v_norm2_post_g):
    t_len = x.shape[1]
    row = lambda a: a.reshape(1, -1)
    x2 = x.reshape(t_len, D_MODEL)
    tgt2 = loss_target.reshape(t_len, D_MODEL)
    pg2 = lambda a: a.reshape(N_GROUPS * PG_SHARD, GROUP)

    shards_b = _to_bf16([w_in, w_pool, w_sgu_proj, w_out, w_ff1, w_ff2])

    def prework(token):
        return _prenorm(token, x2, row(norm1_pre_g))

    (win_f, wpool_f, wproj_f, g_out), (xn, xn_b) = _gather_behind("mix", shards_b[:4], None, prework,
                                                                  by_columns=(0,), by_rows=(1, 2))
    wout_f = g_out.reshape(D_MODEL, D_MODEL)
    bsp_t = b_spatial.T

    def forward(token):
        z, y, h1, *saved = _fwd_mix(token, xn, x2, win_f, row(b_in), wpool_f, row(pool_scale),
                                    row(sgu_ln_g), row(sgu_ln_b), w_spatial, bsp_t, wproj_f, wout_f,
                                    row(norm1_post_g))
        return h1, z, y, saved

    (g_ff1, g_ff2), (h1, z, y, saved) = _gather_behind("ff", shards_b[4:], win_f,
                                                       forward, by_columns=(0,))
    w2_f = g_ff2.reshape(D_FF, D_MODEL)
    hn_b, f_b, df1_b, df2_b, dh1, dg2post, dg2pre, loss_p = _mlp(h1, tgt2, row(norm2_pre_g), row(norm2_post_g),
                                                               g_ff1, w2_f)
    p_ff2, p_ff1 = _wgrad_ff(f_b, df2_b, hn_b, df1_b)
    c_arr = lax.axis_index("c").astype(jnp.int32).reshape(1)
    ff_parts = [p_ff1, p_ff2]
    got_ff = _pair_exchange("pair_exchange_ff", ff_parts, [])
    chip_ff = _pair_sum_big("pair_sum_ff", c_arr, ff_parts, got_ff)

    def backward_mix(token):
        return _bwd_mix(token, dh1, y, z, saved, wpool_f, row(pool_scale), row(sgu_ln_g), row(sgu_ln_b), w_spatial,
                        bsp_t, wproj_f, wout_f, row(norm1_post_g))

    summed_ff, (dz_b, mg_b, dy_b, p_pool, p_proj, dws, dbsp_t, dg1post, dps, dlng, dlnb,
                dbin) = _chip_exchange_behind("ff", chip_ff, [], backward_mix)

    p_in = _wgrad_in(xn_b, dz_b)
    bigs = [p_in, p_pool, p_proj]
    rows = jnp.concatenate([jnp.broadcast_to(loss_p[:, 0:1], (1, D_MODEL)), dps, dlng, dlnb, dg1post, dg2pre, dg2post],
                           axis=0)
    smalls = [rows, dbin, dws.reshape(N_HEADS * SGU_BLOCK, SGU_BLOCK), dbsp_t.T]
    got = _pair_exchange("pair_exchange_in", bigs, smalls)
    sums = _pair_sum_big("pair_sum_in", c_arr, bigs, got[:3], smalls, got[3:])
    chip_bigs, chip_smalls = sums[:3], sums[3:]

    def backward_rest(token):
        dx, dg1pre = _bwd_in(token, dz_b, x2, dh1, row(norm1_pre_g), win_f)
        p_out = _wgrad_out(mg_b, dy_b)
        got_out = _pair_exchange("pair_exchange_out", [p_out], [dg1pre])
        chip_out, chip_g1pre = _pair_sum_big("pair_sum_out", c_arr, [p_out], got_out[:1], [dg1pre], got_out[1:])
        upd_ff = _adamw_big("adamw_ff", p_out, summed_ff, [(w_ff1, m_w_ff1, v_w_ff1), (w_ff2, m_w_ff2, v_w_ff2)])
        return upd_ff[1][0], dx, chip_out, chip_g1pre, upd_ff

    summed_in, (_, dx, chip_out, chip_g1pre, upd_ff) = _chip_exchange_behind("in", chip_bigs, chip_smalls,
                                                                             backward_rest)

    def update_in(token):
        res = _adamw_big("adamw_in", token, summed_in[:1], [(w_in, m_w_in, v_w_in)])[0]
        return res[0], res

    summed_out, (_, upd_in) = _chip_exchange_behind("out", [chip_out], [chip_g1pre], update_in)
    big = {"in": upd_in, "ff1": upd_ff[0], "ff2": upd_ff[1]}
    ws2 = lambda a: a.reshape(N_HEADS * SGU_BLOCK, SGU_BLOCK)
    small_params = [(row(pool_scale), row(m_pool_scale), row(v_pool_scale)),
                    (row(sgu_ln_g), row(m_sgu_ln_g), row(v_sgu_ln_g)),
                    (row(sgu_ln_b), row(m_sgu_ln_b), row(v_sgu_ln_b)),
                    (row(norm1_post_g), row(m_norm1_post_g), row(v_norm1_post_g)),
                    (row(norm2_pre_g), row(m_norm2_pre_g), row(v_norm2_pre_g)),
                    (row(norm2_post_g), row(m_norm2_post_g), row(v_norm2_post_g)),
                    (row(norm1_pre_g), row(m_norm1_pre_g), row(v_norm1_pre_g)),
                    (row(b_in), row(m_b_in), row(v_b_in)),
                    (ws2(w_spatial), ws2(m_w_spatial), ws2(v_w_spatial)),
                    (b_spatial, m_b_spatial, v_b_spatial),
                    (pg2(w_pool), pg2(m_w_pool), pg2(v_w_pool)),
                    (pg2(w_sgu_proj), pg2(m_w_sgu_proj), pg2(v_w_sgu_proj)),
                    (w_out, m_w_out, v_w_out)]
    small_out = _adamw_small(summed_in[3], [summed_out[1]] + list(summed_in[4:])
                             + [summed_in[1], summed_in[2], summed_out[0]], small_params)
    loss = small_out[0][0, 0]
    small_names = SMALL_ROWS[1:] + ("norm1_pre_g", "b_in", "w_spatial", "b_spatial", "w_pool", "w_sgu_proj", "w_out")
    small = {nm: small_out[1 + 4 * p:5 + 4 * p] for p, nm in enumerate(small_names)}

    shapes = {"norm1_pre_g": norm1_pre_g.shape, "w_in": w_in.shape, "b_in": b_in.shape, "w_pool": w_pool.shape,
              "pool_scale": pool_scale.shape, "sgu_ln_g": sgu_ln_g.shape, "sgu_ln_b": sgu_ln_b.shape,
              "w_spatial": w_spatial.shape, "b_spatial": b_spatial.shape, "w_sgu_proj": w_sgu_proj.shape,
              "w_out": w_out.shape, "norm1_post_g": norm1_post_g.shape, "norm2_pre_g": norm2_pre_g.shape,
              "w_ff1": w_ff1.shape, "w_ff2": w_ff2.shape, "norm2_post_g": norm2_post_g.shape}
    source = {"w_in": big["in"], "w_ff1": big["ff1"], "w_ff2": big["ff2"], **small}
    order = list(shapes)
    outs = [loss, dx.reshape(x.shape)]
    for kind in range(4):
        outs += [source[nm][kind].reshape(shapes[nm]) for nm in order]
    return tuple(outs)
```

```python
import math

import jax
import jax.numpy as jnp
from jax import lax
from jax.experimental import pallas as pl
from jax.experimental.pallas import tpu as pltpu

F32, BF16 = jnp.float32, jnp.bfloat16
MESH = pl.DeviceIdType.MESH

D_MODEL = 1024
D_IN = 5120
D_FF = 4096
N_DEV = 8
N_CHIPS = 4
WINDOWS = (2, 4, 8, 16)
N_GROUPS = 4
GROUP = 256
HALO = 16
SGU_BLOCK = 128
N_HEADS = 4
HEAD = 256
CHUNK = 64
EPS = 1e-6
IN_SHARD = D_IN // N_DEV
FF_SHARD = D_FF // N_DEV
OUT_SHARD = D_MODEL // N_DEV
PG_SHARD = GROUP // N_DEV

ADAM_LR, ADAM_B1, ADAM_B2, ADAM_EPS, ADAM_WD, ADAM_STEP = 0.001, 0.9, 0.999, 1e-08, 0.01, 10

VMEM_LIMIT = 56 * 1024 * 1024
TM = 256
TM_BWD = 256
TM_IN = 512
PROJ_COLS = 512
GELU_C0 = math.sqrt(2.0 / math.pi)
GELU_C1 = 0.044715


def _dot(a, b):
    return jnp.dot(a, b, preferred_element_type=F32)


def _dot_nt(a, b):
    return lax.dot_general(a, b, (((1,), (1,)), ((), ())), preferred_element_type=F32)


def _dot_tn(a, b):
    return lax.dot_general(a, b, (((0,), (0,)), ((), ())), preferred_element_type=F32)


def _gelu(x):
    t = jnp.tanh(GELU_C0 * (x + GELU_C1 * (x * x * x)))
    return 0.5 * x * (1.0 + t), t


def _gelu_grad(x, t):
    return 0.5 * (1.0 + t) + 0.5 * x * (1.0 - t * t) * (GELU_C0 * (1.0 + 3.0 * GELU_C1 * (x * x)))


def _sigmoid(x):
    return 1.0 / (1.0 + jnp.exp(-x))


def _mean(x):
    return jnp.mean(x, axis=-1, keepdims=True)


def _colsum(x):
    return jnp.sum(x, axis=0, keepdims=True)


def _const_spec(shape):
    nd = len(shape)
    return pl.BlockSpec(shape, lambda *_: (0,) * nd, pipeline_mode=pl.Buffered(1))


def _acc_spec(shape):
    nd = len(shape)
    return pl.BlockSpec(shape, lambda *_: (0,) * nd)


def _masked_ws(ws_ref):
    ri = lax.broadcasted_iota(jnp.int32, (SGU_BLOCK, SGU_BLOCK), 0) // CHUNK
    ci = lax.broadcasted_iota(jnp.int32, (SGU_BLOCK, SGU_BLOCK), 1) // CHUNK
    return [jnp.where(ri >= ci, ws_ref[h], 0.0).astype(BF16) for h in range(N_HEADS)]


def _pool_fwd(pbuf, tile_idx, tm):
    pos = lax.broadcasted_iota(jnp.int32, (tm, 1), 0) + tile_idx * tm + 1
    pooled = []
    for g, w in enumerate(WINDOWS):
        e = pbuf[:, g * GROUP:(g + 1) * GROUP]
        s, sh = e, 1
        while sh < w:
            s = s + pltpu.roll(s, sh, 0)
            sh *= 2
        inv = 1.0 / jnp.minimum(pos, w).astype(F32)
        pooled.append(s[HALO:] * inv - e[HALO:])
    return pooled


def _spatial_head(ws_h, vb, bsp_ref, h, nblk):
    return jnp.concatenate(
        [_dot(ws_h, vb[n * SGU_BLOCK:(n + 1) * SGU_BLOCK, h * HEAD:(h + 1) * HEAD]) + bsp_ref[:, h:h + 1]
         for n in range(nblk)], axis=0)


def _prenorm(after, x, g1pre):
    t_len = x.shape[0]
    tm = TM_IN

    def body(after_ref, x_ref, g_ref, xn_ref, xnt_ref):
        del after_ref
        xv = x_ref[...]
        xnb = (xv * lax.rsqrt(_mean(xv * xv) + EPS) * g_ref[...]).astype(BF16)
        xn_ref[...] = xnb
        xnt_ref[...] = xnb.T

    return pl.pallas_call(
        body, name="prenorm", grid=(t_len // tm,),
        in_specs=[_ANY, pl.BlockSpec((tm, D_MODEL), lambda i: (i, 0)), _const_spec((1, D_MODEL))],
        out_specs=[pl.BlockSpec((tm, D_MODEL), lambda i: (i, 0)), pl.BlockSpec((D_MODEL, tm), lambda i: (0, i))],
        out_shape=[jax.ShapeDtypeStruct((t_len, D_MODEL), BF16), jax.ShapeDtypeStruct((D_MODEL, t_len), BF16)],
        compiler_params=pltpu.CompilerParams(dimension_semantics=("arbitrary",)),
    )(after, x, g1pre)


def _fwd_mix(after, xn, x, win_g, b_in, wpool, pool_scale, lng, lnb, ws, bsp_t, wproj, wout, g1post):
    t_len = x.shape[0]
    tm = TM
    nt = t_len // tm

    def body(after_ref, xn_ref, xb_ref, win_ref, bin_ref, wpool_ref, ps_ref, lng_ref, lnb_ref, ws_ref, bsp_ref,
             wproj_ref, wout_ref, g1post_ref, z_ref, y_ref, h1_ref, u_ref, gpu_ref, xhat_ref, gpv_ref, sa_ref,
             sb_ref, zcur, znext, pbuf):
        del after_ref
        s = pl.program_id(0)

        @pl.when(s == 0)
        def _():
            znext[...] = jnp.zeros((tm, D_IN), F32)
            pbuf[...] = jnp.zeros((tm + HALO, D_MODEL), F32)

        zcur[...] = znext[...]
        xnb = xn_ref[...]

        def project(p):
            cols = slice(p * PROJ_COLS, (p + 1) * PROJ_COLS)
            zp = _dot(xnb, win_ref[:, cols]) + bin_ref[:, cols]
            if (p + 1) * PROJ_COLS <= D_MODEL:
                z_ref[:, cols] = zp
            znext[:, cols] = zp

        project(0)
        pbuf[0:HALO, :] = jnp.where(s <= 1, 0.0, pbuf[0:HALO, :])
        pbuf[HALO:, :] = zcur[:, 0:D_MODEL]
        pooled = _pool_fwd(pbuf, jnp.maximum(s - 1, 0), tm)
        pbuf[0:HALO, :] = pbuf[tm:tm + HALO, :]
        a = jnp.concatenate([_dot(pooled[g].astype(BF16), wpool_ref[g]) for g in range(N_GROUPS)], axis=1)
        a = a * ps_ref[...]
        project(1)
        zu = zcur[:, D_MODEL:2 * D_MODEL]
        u, tu = _gelu(zu)
        u_ref[...] = u.astype(BF16)
        gpu_ref[...] = _gelu_grad(zu, tu).astype(BF16)
        project(2)
        zv = zcur[:, 2 * D_MODEL:3 * D_MODEL]
        gv, tv = _gelu(zv)
        xc = gv - _mean(gv)
        rln = lax.rsqrt(_mean(xc * xc) + EPS)
        xhat = xc * rln
        xhat_ref[...] = xhat.astype(BF16)
        gpv_ref[...] = (_gelu_grad(zv, tv) * rln).astype(BF16)
        vb = (xhat * lng_ref[...] + lnb_ref[...]).astype(BF16)
        wsm = _masked_ws(ws_ref)
        bbr = []
        for h in range(N_HEADS):
            project(3 + h)
            sv = _spatial_head(wsm[h], vb, bsp_ref, h, tm // SGU_BLOCK)
            bbr.append(_dot((u[:, h * HEAD:(h + 1) * HEAD] * sv).astype(BF16), wproj_ref[h]))
        bbr = jnp.concatenate(bbr, axis=1)
        project(7)
        sa = _sigmoid(zcur[:, 3 * D_MODEL:4 * D_MODEL])
        sb = _sigmoid(zcur[:, 4 * D_MODEL:5 * D_MODEL])
        sa_ref[...] = sa.astype(BF16)
        sb_ref[...] = sb.astype(BF16)
        project(8)
        yv = _dot((sa * a + sb * bbr).astype(BF16), wout_ref[...])
        y_ref[...] = yv
        project(9)
        ry = lax.rsqrt(_mean(yv * yv) + EPS)
        h1_ref[...] = xb_ref[...] + yv * ry * g1post_ref[...]

    proj = lambda w: pl.BlockSpec((tm, w), lambda s: (jnp.minimum(s, nt - 1), 0))
    mix = lambda w: pl.BlockSpec((tm, w), lambda s: (jnp.maximum(s - 1, 0), 0))
    return pl.pallas_call(
        body, name="fwd_mix", grid=(nt + 1,),
        in_specs=[_ANY, proj(D_MODEL), mix(D_MODEL),
                  _const_spec((D_MODEL, D_IN)),
                  _const_spec((1, D_IN)), _const_spec((N_GROUPS, GROUP, GROUP)), _const_spec((1, D_MODEL)),
                  _const_spec((1, D_MODEL)), _const_spec((1, D_MODEL)),
                  _const_spec((N_HEADS, SGU_BLOCK, SGU_BLOCK)), _const_spec((SGU_BLOCK, N_HEADS)),
                  _const_spec((N_HEADS, HEAD, HEAD)), _const_spec((D_MODEL, D_MODEL)), _const_spec((1, D_MODEL))],
        out_specs=[proj(D_MODEL), mix(D_MODEL), mix(D_MODEL)] + [mix(D_MODEL)] * 6,
        out_shape=[jax.ShapeDtypeStruct((t_len, D_MODEL), F32),
                   jax.ShapeDtypeStruct((t_len, D_MODEL), F32), jax.ShapeDtypeStruct((t_len, D_MODEL), F32)]
        + [jax.ShapeDtypeStruct((t_len, D_MODEL), BF16)] * 6,
        scratch_shapes=[pltpu.VMEM((tm, D_IN), F32), pltpu.VMEM((tm, D_IN), F32),
                        pltpu.VMEM((tm + HALO, D_MODEL), F32)],
        compiler_params=pltpu.CompilerParams(dimension_semantics=("arbitrary",), vmem_limit_bytes=VMEM_LIMIT),
    )(after, xn, x, win_g, b_in, wpool, pool_scale, lng, lnb, ws, bsp_t, wproj, wout, g1post)


def _mlp(h1, target, g2pre, g2post, w1_g, w2):
    t_len = h1.shape[0]
    tm = TM
    nt = t_len // tm

    def body(h1_ref, tgt_ref, g2pre_ref, g2post_ref, w1_ref, w2_ref,
             hn_ref, f_ref, df1_ref, df2_ref, dh1_ref, dg2post_ref, dg2pre_ref, loss_ref, f1_scr):
        i = pl.program_id(0)

        @pl.when(i == 0)
        def _():
            dg2post_ref[...] = jnp.zeros_like(dg2post_ref)
            dg2pre_ref[...] = jnp.zeros_like(dg2pre_ref)
            loss_ref[...] = jnp.zeros_like(loss_ref)

        h = h1_ref[...]
        r2 = lax.rsqrt(_mean(h * h) + EPS)
        nh = h * r2
        hnb = (nh * g2pre_ref[...]).astype(BF16)
        hn_ref[...] = hnb.T
        for k in range(N_DEV):
            f1_scr[:, k * FF_SHARD:(k + 1) * FF_SHARD] = _dot(hnb, w1_ref[:, k * FF_SHARD:(k + 1) * FF_SHARD])
        r = jnp.maximum(f1_scr[...], 0.0)
        fb = (r * r).astype(BF16)
        f_ref[...] = fb.T
        f2 = _dot(fb, w2_ref[...])
        rf = lax.rsqrt(_mean(f2 * f2) + EPS)
        nf = f2 * rf
        diff = h + nf * g2post_ref[...] - tgt_ref[...]
        loss_ref[...] += (0.5 / D_MODEL) * jnp.sum(diff * diff)
        dout = diff * (1.0 / D_MODEL)
        dg2post_ref[...] += _colsum(dout * nf)
        dn = dout * g2post_ref[...]
        df2b = (rf * (dn - nf * _mean(dn * nf))).astype(BF16)
        df2_ref[...] = df2b
        df = _dot_nt(df2b, w2_ref[...])
        df1b = (df * (2.0 * jnp.maximum(f1_scr[...], 0.0))).astype(BF16)
        df1_ref[...] = df1b
        dhn = _dot_nt(df1b, w1_ref[...])
        dg2pre_ref[...] += _colsum(dhn * nh)
        dnh = dhn * g2pre_ref[...]
        dh1_ref[...] = dout + r2 * (dnh - nh * _mean(dnh * nh))

    tok = lambda w: pl.BlockSpec((tm, w), lambda i: (i, 0))
    return pl.pallas_call(
        body, name="mlp_fwd_bwd", grid=(nt,),
        in_specs=[tok(D_MODEL), tok(D_MODEL), _const_spec((1, D_MODEL)), _const_spec((1, D_MODEL)),
                  _const_spec((D_MODEL, D_FF)), _const_spec((D_FF, D_MODEL))],
        out_specs=[pl.BlockSpec((D_MODEL, tm), lambda i: (0, i)), pl.BlockSpec((D_FF, tm), lambda i: (0, i)),
                   tok(D_FF), tok(D_MODEL), tok(D_MODEL),
                   _acc_spec((1, D_MODEL)), _acc_spec((1, D_MODEL)), _acc_spec((1, 128))],
        out_shape=[jax.ShapeDtypeStruct((D_MODEL, t_len), BF16), jax.ShapeDtypeStruct((D_FF, t_len), BF16),
                   jax.ShapeDtypeStruct((t_len, D_FF), BF16), jax.ShapeDtypeStruct((t_len, D_MODEL), BF16),
                   jax.ShapeDtypeStruct((t_len, D_MODEL), F32), jax.ShapeDtypeStruct((1, D_MODEL), F32),
                   jax.ShapeDtypeStruct((1, D_MODEL), F32), jax.ShapeDtypeStruct((1, 128), F32)],
        scratch_shapes=[pltpu.VMEM((tm, D_FF), F32)],
        compiler_params=pltpu.CompilerParams(dimension_semantics=("arbitrary",), vmem_limit_bytes=VMEM_LIMIT),
    )(h1, target, g2pre, g2post, w1_g, w2)


def _bwd_mix(after, dh1, y, z, saved, wpool, pool_scale, lng, lnb, ws, bsp_t, wproj, wout, g1post):
    t_len = y.shape[0]
    tm = TM_BWD
    nt = t_len // tm
    nblk = tm // SGU_BLOCK

    def body(after_ref, dh1_ref, y_ref, z_ref, zh_ref, u_ref, gpu_ref, xhat_ref, gpv_ref, sa_ref, sb_ref,
             wpool_ref, ps_ref, lng_ref, lnb_ref, ws_ref,
             bsp_ref, wproj_ref, wout_ref, g1post_ref,
             dz_ref, mg_ref, dy_ref, ppool_ref, pproj_ref, dws_ref, dbsp_ref, dg1post_ref, dps_ref,
             dlng_ref, dlnb_ref, dbin_ref, pbuf, qbuf, dwpool_ref, dwproj_ref):
        del after_ref
        i = pl.program_id(0)
        ti = nt - 1 - i

        @pl.when(i == 0)
        def _():
            for ref in (dwpool_ref, dwproj_ref, dws_ref, dbsp_ref, dg1post_ref, dps_ref, dlng_ref, dlnb_ref,
                        dbin_ref):
                ref[...] = jnp.zeros_like(ref)
            qbuf[tm:tm + HALO, :] = jnp.zeros((HALO, D_MODEL), F32)

        pbuf[0:HALO, :] = jnp.where(ti > 0, zh_ref[...], 0.0)
        pbuf[HALO:, :] = z_ref[...]
        pooled = _pool_fwd(pbuf, ti, tm)
        pooled_b = [p.astype(BF16) for p in pooled]
        a_raw = jnp.concatenate([_dot(pooled_b[g], wpool_ref[g]) for g in range(N_GROUPS)], axis=1)
        wsm = _masked_ws(ws_ref)
        u = u_ref[...].astype(F32)
        xhat = xhat_ref[...].astype(F32)
        vb = (xhat * lng_ref[...] + lnb_ref[...]).astype(BF16)
        sv_heads = [_spatial_head(wsm[h], vb, bsp_ref, h, nblk) for h in range(N_HEADS)]
        gated_b = [(u[:, h * HEAD:(h + 1) * HEAD] * sv_heads[h]).astype(BF16) for h in range(N_HEADS)]
        bbr = jnp.concatenate([_dot(gated_b[h], wproj_ref[h]) for h in range(N_HEADS)], axis=1)
        sa = sa_ref[...].astype(F32)
        sb = sb_ref[...].astype(F32)
        a = a_raw * ps_ref[...]
        mg_ref[...] = (sa * a + sb * bbr).astype(BF16).T

        dh = dh1_ref[...]
        yv = y_ref[...]
        ry = lax.rsqrt(_mean(yv * yv) + EPS)
        ny = yv * ry
        dg1post_ref[...] += _colsum(dh * ny)
        dn = dh * g1post_ref[...]
        dyb = (ry * (dn - ny * _mean(dn * ny))).astype(BF16)
        dy_ref[...] = dyb
        dmg = _dot_nt(dyb, wout_ref[...])

        da = dmg * sa
        dbbr = dmg * sb
        dzga = dmg * a * sa * (1.0 - sa)
        dzgb = dmg * bbr * sb * (1.0 - sb)
        dz_ref[:, 3 * D_MODEL:4 * D_MODEL] = dzga.astype(BF16)
        dz_ref[:, 4 * D_MODEL:5 * D_MODEL] = dzgb.astype(BF16)
        dbin_ref[:, 3 * D_MODEL:4 * D_MODEL] += _colsum(dzga)
        dbin_ref[:, 4 * D_MODEL:5 * D_MODEL] += _colsum(dzgb)

        dps_ref[...] += _colsum(da * a_raw)
        da_raw_b = (da * ps_ref[...]).astype(BF16)
        pos = lax.broadcasted_iota(jnp.int32, (tm, 1), 0) + ti * tm + 1
        dpooled = []
        for g, w in enumerate(WINDOWS):
            cols = slice(g * GROUP, (g + 1) * GROUP)
            dwpool_ref[g] += _dot_tn(pooled_b[g], da_raw_b[:, cols])
            dp = _dot_nt(da_raw_b[:, cols], wpool_ref[g])
            dpooled.append(dp)
            qbuf[0:tm, cols] = dp * (1.0 / jnp.minimum(pos, w).astype(F32))
        n_ext = tm + HALO
        dzp = []
        for g, w in enumerate(WINDOWS):
            e = qbuf[:, g * GROUP:(g + 1) * GROUP]
            s, sh = e, 1
            while sh < w:
                s = s + pltpu.roll(s, n_ext - sh, 0)
                sh *= 2
            dzp.append(s[0:tm] - dpooled[g])
        qbuf[tm:tm + HALO, :] = qbuf[0:HALO, :]
        dzp = jnp.concatenate(dzp, axis=1)
        dz_ref[:, 0:D_MODEL] = dzp.astype(BF16)
        dbin_ref[:, 0:D_MODEL] += _colsum(dzp)

        dv_heads = []
        du_heads = []
        for h in range(N_HEADS):
            cols = slice(h * HEAD, (h + 1) * HEAD)
            dbbr_b = dbbr[:, cols].astype(BF16)
            dwproj_ref[h] += _dot_tn(gated_b[h], dbbr_b)
            dgated = _dot_nt(dbbr_b, wproj_ref[h])
            du_heads.append(dgated * sv_heads[h])
            dsv = dgated * u[:, cols]
            dsv_b = dsv.astype(BF16)
            rows = []
            for n in range(nblk):
                blk = slice(n * SGU_BLOCK, (n + 1) * SGU_BLOCK)
                rows.append(_dot_tn(wsm[h], dsv_b[blk]))
                dws_ref[h] += _dot_nt(dsv_b[blk], vb[blk, cols])
                dbsp_ref[:, h:h + 1] += jnp.sum(dsv[blk], axis=1, keepdims=True)
            dv_heads.append(jnp.concatenate(rows, axis=0))
        dzu = jnp.concatenate(du_heads, axis=1) * gpu_ref[...].astype(F32)
        dz_ref[:, D_MODEL:2 * D_MODEL] = dzu.astype(BF16)
        dbin_ref[:, D_MODEL:2 * D_MODEL] += _colsum(dzu)
        dv = jnp.concatenate(dv_heads, axis=1)
        dlng_ref[...] += _colsum(dv * xhat)
        dlnb_ref[...] += _colsum(dv)
        dxh = dv * lng_ref[...]
        dzv = (dxh - _mean(dxh) - xhat * _mean(dxh * xhat)) * gpv_ref[...].astype(F32)
        dz_ref[:, 2 * D_MODEL:3 * D_MODEL] = dzv.astype(BF16)
        dbin_ref[:, 2 * D_MODEL:3 * D_MODEL] += _colsum(dzv)

        @pl.when(i == nt - 1)
        def _():
            ri = lax.broadcasted_iota(jnp.int32, (SGU_BLOCK, SGU_BLOCK), 0) // CHUNK
            ci = lax.broadcasted_iota(jnp.int32, (SGU_BLOCK, SGU_BLOCK), 1) // CHUNK
            for h in range(N_HEADS):
                dws_ref[h] = jnp.where(ri >= ci, dws_ref[h], 0.0)
            for slot in range(N_DEV):
                own = slice(_owner_of_slot(slot) * PG_SHARD, (_owner_of_slot(slot) + 1) * PG_SHARD)
                for g in range(N_GROUPS):
                    rows = slice(g * PG_SHARD, (g + 1) * PG_SHARD)
                    ppool_ref[slot, rows, :] = dwpool_ref[g, own, :].astype(BF16)
                    pproj_ref[slot, rows, :] = dwproj_ref[g, own, :].astype(BF16)

    tok = lambda w: pl.BlockSpec((tm, w), lambda i: (nt - 1 - i, 0))
    halo = pl.BlockSpec((HALO, D_MODEL), lambda i: (jnp.maximum((nt - 1 - i) * (tm // HALO) - 1, 0), 0))
    return pl.pallas_call(
        body, name="bwd_mix", grid=(nt,),
        in_specs=[_ANY, tok(D_MODEL), tok(D_MODEL), tok(D_MODEL), halo] + [tok(D_MODEL)] * 6
        + [_const_spec((N_GROUPS, GROUP, GROUP)),
                  _const_spec((1, D_MODEL)), _const_spec((1, D_MODEL)), _const_spec((1, D_MODEL)),
                  _const_spec((N_HEADS, SGU_BLOCK, SGU_BLOCK)), _const_spec((SGU_BLOCK, N_HEADS)),
                  _const_spec((N_HEADS, HEAD, HEAD)), _const_spec((D_MODEL, D_MODEL)), _const_spec((1, D_MODEL))],
        out_specs=[tok(D_IN), pl.BlockSpec((D_MODEL, tm), lambda i: (0, nt - 1 - i)), tok(D_MODEL),
                   _acc_spec((N_DEV, N_GROUPS * PG_SHARD, GROUP)), _acc_spec((N_DEV, N_HEADS * PG_SHARD, HEAD)),
                   _acc_spec((N_HEADS, SGU_BLOCK, SGU_BLOCK)), _acc_spec((SGU_BLOCK, N_HEADS)),
                   _acc_spec((1, D_MODEL)), _acc_spec((1, D_MODEL)), _acc_spec((1, D_MODEL)), _acc_spec((1, D_MODEL)),
                   _acc_spec((1, D_IN))],
        out_shape=[jax.ShapeDtypeStruct((t_len, D_IN), BF16),
                   jax.ShapeDtypeStruct((D_MODEL, t_len), BF16), jax.ShapeDtypeStruct((t_len, D_MODEL), BF16),
                   jax.ShapeDtypeStruct((N_DEV, N_GROUPS * PG_SHARD, GROUP), BF16),
                   jax.ShapeDtypeStruct((N_DEV, N_HEADS * PG_SHARD, HEAD), BF16),
                   jax.ShapeDtypeStruct((N_HEADS, SGU_BLOCK, SGU_BLOCK), F32),
                   jax.ShapeDtypeStruct((SGU_BLOCK, N_HEADS), F32),
                   jax.ShapeDtypeStruct((1, D_MODEL), F32), jax.ShapeDtypeStruct((1, D_MODEL), F32),
                   jax.ShapeDtypeStruct((1, D_MODEL), F32), jax.ShapeDtypeStruct((1, D_MODEL), F32),
                   jax.ShapeDtypeStruct((1, D_IN), F32)],
        scratch_shapes=[pltpu.VMEM((tm + HALO, D_MODEL), F32), pltpu.VMEM((tm + HALO, D_MODEL), F32),
                        pltpu.VMEM((N_GROUPS, GROUP, GROUP), F32), pltpu.VMEM((N_HEADS, HEAD, HEAD), F32)],
        compiler_params=pltpu.CompilerParams(dimension_semantics=("arbitrary",), vmem_limit_bytes=VMEM_LIMIT),
    )(after, dh1, y, z, z, *saved, wpool, pool_scale, lng, lnb, ws, bsp_t, wproj, wout, g1post)


def _bwd_in(after, dz_b, x, dh1, g1pre, win_g):
    t_len = x.shape[0]
    tm = TM_IN
    nt = t_len // tm

    def body(after_ref, dz_ref, x_ref, dh1_ref, g1_ref, win_ref, dx_ref, dg1pre_ref):
        del after_ref

        @pl.when(pl.program_id(0) == 0)
        def _():
            dg1pre_ref[...] = jnp.zeros_like(dg1pre_ref)

        dxn = _dot_nt(dz_ref[...], win_ref[...])
        xv = x_ref[...]
        r1 = lax.rsqrt(_mean(xv * xv) + EPS)
        nx = xv * r1
        dg1pre_ref[...] += _colsum(dxn * nx)
        dnx = dxn * g1_ref[...]
        dx_ref[...] = r1 * (dnx - nx * _mean(dnx * nx)) + dh1_ref[...]

    tok = lambda w: pl.BlockSpec((tm, w), lambda i: (i, 0))
    return pl.pallas_call(
        body, name="bwd_in", grid=(nt,),
        in_specs=[_ANY, tok(D_IN), tok(D_MODEL), tok(D_MODEL), _const_spec((1, D_MODEL)),
                  _const_spec((D_MODEL, D_IN))],
        out_specs=[tok(D_MODEL), _acc_spec((1, D_MODEL))],
        out_shape=[jax.ShapeDtypeStruct((t_len, D_MODEL), F32), jax.ShapeDtypeStruct((1, D_MODEL), F32)],
        compiler_params=pltpu.CompilerParams(dimension_semantics=("arbitrary",), vmem_limit_bytes=VMEM_LIMIT),
    )(after, dz_b, x, dh1, g1pre, win_g)


def _owner_of_slot(s):
    return 4 * ((s // 2) % 2) + 2 * (s % 2) + s // 4


def _wgrad_ff(c_arr, f_t, df2_b, hn_t, df1_b):
    t_len = df2_b.shape[0]

    def body(c_ref, f_ref, df2_ref, hn_ref, df1_ref, o2_ref, o1_ref, got2_ref, got1_ref, buf2, buf1, send_sems,
             recv_sems):
        del c_ref
        s = pl.program_id(0)
        x, y, c = _coords()

        def copies(k):
            return [pltpu.make_async_remote_copy(src_ref=buf.at[k], dst_ref=got.at[k], send_sem=send_sems.at[j, k],
                                                 recv_sem=recv_sems.at[j, k], device_id=(x, y, 1 - c),
                                                 device_id_type=MESH)
                    for j, (buf, got) in enumerate(((buf2, got2_ref), (buf1, got1_ref)))]

        @pl.when(s < N_CHIPS)
        def _():
            buf2[s] = _dot(f_ref[...], df2_ref[...]).astype(BF16)
            buf1[s] = _dot(hn_ref[...], df1_ref[...]).astype(BF16)
            for cp in copies(s):
                cp.start()

        @pl.when(s >= N_CHIPS)
        def _():
            o2_ref[...] = _dot(f_ref[...], df2_ref[...]).astype(BF16)
            o1_ref[...] = _dot(hn_ref[...], df1_ref[...]).astype(BF16)

        @pl.when(s == N_DEV - 1)
        def _():
            for k in range(N_CHIPS):
                for cp in copies(k):
                    cp.wait()

    owner = lambda s, c_ref: 2 * (s % N_CHIPS) + jnp.where(s < N_CHIPS, 1 - c_ref[0], c_ref[0])
    mine = lambda s, c_ref: (jnp.maximum(s - N_CHIPS, 0), 0, 0)
    return pl.pallas_call(
        body, name="wgrad_ff",
        grid_spec=pltpu.PrefetchScalarGridSpec(
            num_scalar_prefetch=1, grid=(N_DEV,),
            in_specs=[pl.BlockSpec((FF_SHARD, t_len), lambda s, c_ref: (owner(s, c_ref), 0)),
                      pl.BlockSpec((t_len, D_MODEL), lambda s, c_ref: (0, 0), pipeline_mode=pl.Buffered(1)),
                      pl.BlockSpec((D_MODEL, t_len), lambda s, c_ref: (0, 0), pipeline_mode=pl.Buffered(1)),
                      pl.BlockSpec((t_len, FF_SHARD), lambda s, c_ref: (0, owner(s, c_ref)))],
            out_specs=[pl.BlockSpec((None, FF_SHARD, D_MODEL), mine),
                       pl.BlockSpec((None, D_MODEL, FF_SHARD), mine), _ANY, _ANY],
            scratch_shapes=[pltpu.VMEM((N_CHIPS, FF_SHARD, D_MODEL), BF16),
                            pltpu.VMEM((N_CHIPS, D_MODEL, FF_SHARD), BF16),
                            pltpu.SemaphoreType.DMA((2, N_CHIPS)), pltpu.SemaphoreType.DMA((2, N_CHIPS))]),
        out_shape=[jax.ShapeDtypeStruct((N_CHIPS, FF_SHARD, D_MODEL), BF16),
                   jax.ShapeDtypeStruct((N_CHIPS, D_MODEL, FF_SHARD), BF16)] * 2,
        compiler_params=pltpu.CompilerParams(dimension_semantics=("arbitrary",), vmem_limit_bytes=VMEM_LIMIT),
    )(c_arr, f_t, df2_b, hn_t, df1_b)


def _wgrad_in(xn_t, dz_b):
    t_len = dz_b.shape[0]

    def body(a_ref, b_ref, o_ref):
        res = _dot(a_ref[...], b_ref[...])
        o_ref[0] = res[:, 0:IN_SHARD].astype(BF16)
        o_ref[1] = res[:, IN_SHARD:2 * IN_SHARD].astype(BF16)

    out = pl.pallas_call(
        body, name="wgrad_in", grid=(N_CHIPS,),
        in_specs=[pl.BlockSpec((D_MODEL, t_len), lambda q: (0, 0), pipeline_mode=pl.Buffered(1)),
                  pl.BlockSpec((t_len, 2 * IN_SHARD), lambda q: (0, q))],
        out_specs=pl.BlockSpec((2, None, D_MODEL, IN_SHARD), lambda q: (0, q, 0, 0)),
        out_shape=jax.ShapeDtypeStruct((2, N_CHIPS, D_MODEL, IN_SHARD), BF16),
        compiler_params=pltpu.CompilerParams(dimension_semantics=("arbitrary",), vmem_limit_bytes=VMEM_LIMIT),
    )(xn_t, dz_b)
    return out.reshape(N_DEV, D_MODEL, IN_SHARD)


def _wgrad_out(mg_t, dy_b):
    t_len = dy_b.shape[0]

    def body(a_ref, b_ref, o_ref):
        res = _dot(a_ref[...], b_ref[...])
        o_ref[0] = res[0:OUT_SHARD].astype(BF16)
        o_ref[1] = res[OUT_SHARD:2 * OUT_SHARD].astype(BF16)

    out = pl.pallas_call(
        body, name="wgrad_out", grid=(N_CHIPS,),
        in_specs=[pl.BlockSpec((2 * OUT_SHARD, t_len), lambda q: (q, 0)),
                  pl.BlockSpec((t_len, D_MODEL), lambda q: (0, 0), pipeline_mode=pl.Buffered(1))],
        out_specs=pl.BlockSpec((2, None, OUT_SHARD, D_MODEL), lambda q: (0, q, 0, 0)),
        out_shape=jax.ShapeDtypeStruct((2, N_CHIPS, OUT_SHARD, D_MODEL), BF16),
        compiler_params=pltpu.CompilerParams(dimension_semantics=("arbitrary",), vmem_limit_bytes=VMEM_LIMIT),
    )(mg_t, dy_b)
    return out.reshape(N_DEV, OUT_SHARD, D_MODEL)


def _coords():
    return lax.axis_index("x"), lax.axis_index("y"), lax.axis_index("c")


_ANY = pl.BlockSpec(memory_space=pl.ANY)


def _pair_exchange(name, bigs, smalls):
    nb, n = len(bigs), len(bigs) + len(smalls)

    def body(*refs):
        src, dst = refs[:n], refs[n:2 * n]
        send_sems, recv_sems = refs[2 * n:]
        x, y, c = _coords()
        copies = []
        for j in range(n):
            s = src[j].at[pl.ds(4 * (1 - c), 4)] if j < nb else src[j]
            cp = pltpu.make_async_remote_copy(src_ref=s, dst_ref=dst[j], send_sem=send_sems.at[j],
                                              recv_sem=recv_sems.at[j], device_id=(x, y, 1 - c), device_id_type=MESH)
            cp.start()
            copies.append(cp)
        for cp in copies:
            cp.wait()

    return pl.pallas_call(
        body, name=name,
        in_specs=[_ANY] * n, out_specs=[_ANY] * n,
        out_shape=[jax.ShapeDtypeStruct((4,) + b.shape[1:], b.dtype) for b in bigs]
        + [jax.ShapeDtypeStruct(s.shape, s.dtype) for s in smalls],
        scratch_shapes=[pltpu.SemaphoreType.DMA((n,)), pltpu.SemaphoreType.DMA((n,))],
    )(*bigs, *smalls)


_HBM = pl.BlockSpec(memory_space=pltpu.HBM)
_SEM = pl.BlockSpec(memory_space=pltpu.SEMAPHORE)
_VMEM = pl.BlockSpec(memory_space=pltpu.VMEM)
_EFFECT = pltpu.SideEffectType.DATAFLOW_SIDE_EFFECTING
_TOKEN = jax.ShapeDtypeStruct((8, 128), F32)


def _in_hbm(a):
    return pltpu.with_memory_space_constraint(a, pltpu.HBM)


def _split_call(name, body, n_sems_out, arrays, sems_in=(), after=None):
    na, ns = len(arrays), len(sems_in)
    has_after = after is not None

    def kernel_body(*refs):
        arr = refs[:na]
        s_in = refs[na:na + ns]
        outs = refs[na + ns + has_after:]
        body(arr, s_in, outs[:n_sems_out])
        outs[-1][...] = jnp.zeros((8, 128), F32)

    out_shape = ([pltpu.SemaphoreType.DMA(())] * n_sems_out + [pltpu.HBM(a.shape, a.dtype) for a in arrays] + [_TOKEN])
    res = pl.pallas_call(
        kernel_body, name=name, out_shape=out_shape,
        in_specs=[_HBM] * na + [_SEM] * ns + [_ANY] * has_after,
        out_specs=[_SEM] * n_sems_out + [_HBM] * na + [_VMEM],
        input_output_aliases={i: n_sems_out + i for i in range(na)},
        compiler_params=pltpu.CompilerParams(has_side_effects=_EFFECT),
    )(*[_in_hbm(a) for a in arrays], *sems_in, *([after] if has_after else []))
    return list(res[:n_sems_out]), list(res[n_sems_out:n_sems_out + na]), res[-1]


def _wait_bytes_of(ref, send_sem, recv_sem, peer, send=True, recv=True):
    cp = pltpu.make_async_remote_copy(src_ref=ref, dst_ref=ref, send_sem=send_sem, recv_sem=recv_sem,
                                      device_id=peer, device_id_type=MESH)
    if send:
        cp.wait_send()
    if recv:
        cp.wait_recv()


def _gather_behind(tag, shards, after, work, by_columns=(), by_rows=()):
    n = len(shards)

    def land_shape(j, s):
        if j in by_columns:
            return (s.shape[0], N_DEV * s.shape[1])
        if j in by_rows:
            return (s.shape[0], N_DEV * s.shape[1], s.shape[2])
        return (N_DEV,) + s.shape

    lands = [lax.empty(land_shape(j, s), s.dtype) for j, s in enumerate(shards)]

    def slots(arr, j, first, count=1):
        if j in by_columns:
            cols = shards[j].shape[1]
            return arr[n + j].at[:, pl.ds(pl.multiple_of(first * cols, 128), count * cols)]
        if j in by_rows:
            rows = shards[j].shape[1]
            return arr[n + j].at[:, pl.ds(pl.multiple_of(first * rows, 16), count * rows), :]
        return arr[n + j].at[first] if count == 1 else arr[n + j].at[pl.ds(first, count)]

    def own_slot(arr, j, sem):
        x, y, c = _coords()
        return pltpu.make_async_copy(arr[j], slots(arr, j, 4 * x + 2 * y + c), sem)

    def start(arr, _, sems):
        x, y, c = _coords()
        me = 4 * x + 2 * y + c
        for j in range(n):
            for chip in [(1 - x, y), (x, 1 - y), (1 - x, 1 - y)]:
                pltpu.make_async_remote_copy(src_ref=arr[j], dst_ref=slots(arr, j, me), send_sem=sems[j],
                                             recv_sem=sems[n + j], device_id=(*chip, c), device_id_type=MESH).start()
        for j in range(n):
            pltpu.make_async_remote_copy(src_ref=arr[j], dst_ref=slots(arr, j, me), send_sem=sems[2 * n + j],
                                         recv_sem=sems[3 * n + j], device_id=(x, y, 1 - c),
                                         device_id_type=MESH).start()
            own_slot(arr, j, sems[4 * n + j]).start()

    def middle(arr, s_in, sems):
        x, y, c = _coords()
        sibling = (x, y, 1 - c)
        for j in range(n):
            _wait_bytes_of(slots(arr, j, 0, 3), s_in[j], s_in[n + j], sibling)
            for chip in [(1 - x, y), (x, 1 - y), (1 - x, 1 - y)]:
                slot = slots(arr, j, 4 * chip[0] + 2 * chip[1] + c)
                pltpu.make_async_remote_copy(src_ref=slot, dst_ref=slot, send_sem=sems[j], recv_sem=sems[n + j],
                                             device_id=sibling, device_id_type=MESH).start()

    def finish(arr, s_in, _):
        x, y, c = _coords()
        sibling = (x, y, 1 - c)
        for j in range(n):
            _wait_bytes_of(slots(arr, j, 0, 1), s_in[j], s_in[n + j], sibling)
            own_slot(arr, j, s_in[2 * n + j]).wait()
            _wait_bytes_of(slots(arr, j, 0, 3), s_in[3 * n + j], s_in[4 * n + j], sibling)

    sems, arrays, token = _split_call("gather_%s_start" % tag, start, 5 * n, list(shards) + lands, after=after)
    result = work(token)
    fwd_sems, arrays, token = _split_call("gather_%s_middle" % tag, middle, 2 * n, arrays, sems_in=sems[:2 * n],
                                          after=result[0])
    _, arrays, _ = _split_call("gather_%s_finish" % tag, finish, 0, arrays, sems_in=sems[2 * n:] + fwd_sems,
                               after=token)
    return arrays[n:], result


def _chip_exchange_behind(tag, bigs, smalls, work):
    nb, n = len(bigs), len(bigs) + len(smalls)
    lands = [lax.empty(s.shape, s.dtype) for s in bigs] + [lax.empty((N_CHIPS,) + s.shape, s.dtype) for s in smalls]

    def own_slot(arr, j, sem):
        x, y, _ = _coords()
        q_me = 2 * x + y
        return pltpu.make_async_copy(arr[j].at[q_me] if j < nb else arr[j], arr[n + j].at[q_me], sem)

    def start(arr, _, sems):
        x, y, c = _coords()
        q_me = 2 * x + y
        for j in range(n):
            for peer in [(1 - x, y, c), (x, 1 - y, c), (1 - x, 1 - y, c)]:
                piece = arr[j].at[2 * peer[0] + peer[1]] if j < nb else arr[j]
                pltpu.make_async_remote_copy(src_ref=piece, dst_ref=arr[n + j].at[q_me],
                                             send_sem=sems[j], recv_sem=sems[n + j], device_id=peer,
                                             device_id_type=MESH).start()
            own_slot(arr, j, sems[2 * n + j]).start()

    def finish(arr, s_in, _):
        x, y, c = _coords()
        for j in range(n):
            _wait_bytes_of(arr[n + j].at[pl.ds(0, 3)], s_in[j], s_in[n + j], (x, y, 1 - c))
            own_slot(arr, j, s_in[2 * n + j]).wait()

    sems, arrays, token = _split_call("chip_exchange_%s_start" % tag, start, 3 * n, list(bigs) + list(smalls) + lands)
    result = work(token)
    _, arrays, _ = _split_call("chip_exchange_%s_finish" % tag, finish, 0, arrays, sems_in=sems, after=result[0])
    return arrays[n:], result


def _pair_sum_big(name, c_arr, mine, theirs, whole_mine=(), whole_theirs=()):
    n, nw = len(mine), len(whole_mine)

    def body(c_ref, *refs):
        del c_ref
        ins, outs = refs[:2 * (n + nw)], refs[2 * (n + nw):]
        for j in range(n):
            outs[j][...] = (ins[j][...].astype(F32) + ins[n + j][...].astype(F32)).astype(BF16)

        @pl.when(pl.program_id(0) == 0)
        def _():
            for j in range(nw):
                outs[n + j][...] = ins[2 * n + j][...] + ins[2 * n + nw + j][...]

    slot = lambda t: pl.BlockSpec((None,) + t.shape[1:], lambda q, c_ref: (q, 0, 0))
    own = lambda t: slot(t) if t.shape[0] == N_CHIPS else pl.BlockSpec(
        (None,) + t.shape[1:], lambda q, c_ref: (4 * c_ref[0] + q, 0, 0))
    whole = lambda t: pl.BlockSpec(t.shape, lambda q, c_ref: (0,) * t.ndim)
    return pl.pallas_call(
        body, name=name,
        grid_spec=pltpu.PrefetchScalarGridSpec(
            num_scalar_prefetch=1, grid=(N_CHIPS,),
            in_specs=[own(t) for t in mine] + [slot(t) for t in theirs]
            + [whole(t) for t in whole_mine] + [whole(t) for t in whole_theirs],
            out_specs=[slot(t) for t in theirs] + [whole(t) for t in whole_mine]),
        out_shape=[jax.ShapeDtypeStruct(t.shape, BF16) for t in theirs]
        + [jax.ShapeDtypeStruct(t.shape, t.dtype) for t in whole_mine],
        compiler_params=pltpu.CompilerParams(dimension_semantics=("arbitrary",)),
    )(c_arr, *mine, *theirs, *whole_mine, *whole_theirs)


def _adamw_math(w, g, m, v):
    m = ADAM_B1 * m + (1.0 - ADAM_B1) * g
    v = ADAM_B2 * v + (1.0 - ADAM_B2) * (g * g)
    m_hat = m / (1.0 - ADAM_B1 ** ADAM_STEP)
    v_hat = v / (1.0 - ADAM_B2 ** ADAM_STEP)
    delta = -ADAM_LR * (m_hat / (jnp.sqrt(v_hat) + ADAM_EPS) + ADAM_WD * w)
    return delta, m, v


ADAMW_STEPS = 2


def _adamw_big(name, after, chip_sums, params):
    n = len(params)

    def body(after_ref, *refs):
        del after_ref
        outs = refs[4 * n:]
        for j in range(n):
            t_ref, w_ref, m_ref, v_ref = refs[4 * j:4 * j + 4]
            g = t_ref[0].astype(F32)
            for q in range(1, N_CHIPS):
                g = g + t_ref[q].astype(F32)
            d, mn, vn = _adamw_math(w_ref[...], g, m_ref[...], v_ref[...])
            for out, val in zip(outs[4 * j:4 * j + 4], (g, d, mn, vn)):
                out[...] = val

    in_specs, out_specs, out_shape, operands = [_ANY], [], [], [after]
    for t, (w, m, v) in zip(chip_sums, params):
        rows, cols = w.shape
        tr = rows // ADAMW_STEPS
        blk = pl.BlockSpec((tr, cols), lambda r: (r, 0))
        in_specs += [pl.BlockSpec((N_CHIPS, tr, cols), lambda r: (0, r, 0)), blk, blk, blk]
        out_specs += [blk] * 4
        out_shape += [jax.ShapeDtypeStruct((rows, cols), F32)] * 4
        operands += [t, w, m, v]
    res = pl.pallas_call(
        body, name=name, grid=(ADAMW_STEPS,), in_specs=in_specs, out_specs=out_specs, out_shape=out_shape,
        compiler_params=pltpu.CompilerParams(dimension_semantics=("arbitrary",), vmem_limit_bytes=VMEM_LIMIT),
    )(*operands)
    return [res[4 * j:4 * j + 4] for j in range(n)]


SMALL_ROWS = ("loss", "pool_scale", "sgu_ln_g", "sgu_ln_b", "norm1_post_g", "norm2_pre_g", "norm2_post_g")


N_ROW_PARAMS = len(SMALL_ROWS) - 1


def _adamw_small(u_rows, u_others, params):
    n, n_oth = len(params), len(u_others)

    def body(*refs):
        urow_ref, others = refs[0], refs[1:1 + n_oth]
        wmv = refs[1 + n_oth:1 + n_oth + 3 * n]
        loss_ref = refs[1 + n_oth + 3 * n]
        outs = refs[2 + n_oth + 3 * n:]

        def total(ref, idx):
            g = ref[(0,) + idx].astype(F32)
            for q in range(1, N_CHIPS):
                g = g + ref[(q,) + idx].astype(F32)
            return g

        loss_ref[...] = total(urow_ref, (slice(0, 1), slice(None)))
        for p in range(n):
            if p < N_ROW_PARAMS:
                g = total(urow_ref, (slice(p + 1, p + 2), slice(None)))
            else:
                g = total(others[p - N_ROW_PARAMS], (slice(None), slice(None)))
            d, mn, vn = _adamw_math(wmv[3 * p][...], g, wmv[3 * p + 1][...], wmv[3 * p + 2][...])
            outs[4 * p][...] = g
            outs[4 * p + 1][...] = d
            outs[4 * p + 2][...] = mn
            outs[4 * p + 3][...] = vn

    flat = [a for p in params for a in p]
    out_shape = [jax.ShapeDtypeStruct((1, D_MODEL), F32)]
    for w, _, _ in params:
        out_shape += [jax.ShapeDtypeStruct(w.shape, F32)] * 4
    return pl.pallas_call(body, name="adamw_small", out_shape=out_shape)(u_rows, *u_others, *flat)


def _to_bf16(arrays):
    n = len(arrays)

    def body(*refs):
        for j in range(n):
            refs[n + j][...] = refs[j][...].astype(BF16)

    return pl.pallas_call(
        body, name="shards_to_bf16", out_shape=[jax.ShapeDtypeStruct(a.shape, BF16) for a in arrays],
    )(*arrays)


def kernel(x, norm1_pre_g, w_in, b_in, w_pool, pool_scale, sgu_ln_g, sgu_ln_b, w_spatial, b_spatial, w_sgu_proj, w_out, norm1_post_g, norm2_pre_g, w_ff1, w_ff2, norm2_post_g, loss_target, m_norm1_pre_g, m_w_in, m_b_in, m_w_pool, m_pool_scale, m_sgu_ln_g, m_sgu_ln_b, m_w_spatial, m_b_spatial, m_w_sgu_proj, m_w_out, m_norm1_post_g, m_norm2_pre_g, m_w_ff1, m_w_ff2, m_norm2_post_g, v_norm1_pre_g, v_w_in, v_b_in, v_w_pool, v_pool_scale, v_sgu_ln_g, v_sgu_ln_b, v_w_spatial, v_b_spatial, v_w_sgu_proj, v_w_out, v_norm1_post_g, v_norm2_pre_g, v_w_ff1, v_w_ff2, v_norm2_post_g):
    t_len = x.shape[1]
    row = lambda a: a.reshape(1, -1)
    x2 = x.reshape(t_len, D_MODEL)
    tgt2 = loss_target.reshape(t_len, D_MODEL)
    pg2 = lambda a: a.reshape(N_GROUPS * PG_SHARD, GROUP)

    shards_b = _to_bf16([w_in, w_pool, w_sgu_proj, w_out, w_ff1, w_ff2])

    def prework(token):
        return _prenorm(token, x2, row(norm1_pre_g))

    (win_f, wpool_f, wproj_f, g_out), (xn, xn_b) = _gather_behind("mix", shards_b[:4], None, prework,
                                                                  by_columns=(0,), by_rows=(1, 2))
    wout_f = g_out.reshape(D_MODEL, D_MODEL)
    bsp_t = b_spatial.T

    def forward(token):
        z, y, h1, *saved = _fwd_mix(token, xn, x2, win_f, row(b_in), wpool_f, row(pool_scale),
                                    row(sgu_ln_g), row(sgu_ln_b), w_spatial, bsp_t, wproj_f, wout_f,
                                    row(norm1_post_g))
        return h1, z, y, saved

    (g_ff1, g_ff2), (h1, z, y, saved) = _gather_behind("ff", shards_b[4:], win_f,
                                                       forward, by_columns=(0,))
    w2_f = g_ff2.reshape(D_FF, D_MODEL)
    hn_b, f_b, df1_b, df2_b, dh1, dg2post, dg2pre, loss_p = _mlp(h1, tgt2, row(norm2_pre_g), row(norm2_post_g),
                                                               g_ff1, w2_f)
    c_arr = lax.axis_index("c").astype(jnp.int32).reshape(1)
    p_ff2, p_ff1, got_ff2, got_ff1 = _wgrad_ff(c_arr, f_b, df2_b, hn_b, df1_b)
    chip_ff = _pair_sum_big("pair_sum_ff", c_arr, [p_ff1, p_ff2], [got_ff1, got_ff2])

    def backward_mix(token):
        return _bwd_mix(token, dh1, y, z, saved, wpool_f, row(pool_scale), row(sgu_ln_g), row(sgu_ln_b), w_spatial,
                        bsp_t, wproj_f, wout_f, row(norm1_post_g))

    summed_ff, (dz_b, mg_b, dy_b, p_pool, p_proj, dws, dbsp_t, dg1post, dps, dlng, dlnb,
                dbin) = _chip_exchange_behind("ff", chip_ff, [], backward_mix)

    p_in = _wgrad_in(xn_b, dz_b)
    bigs = [p_in, p_pool, p_proj]
    rows = jnp.concatenate([jnp.broadcast_to(loss_p[:, 0:1], (1, D_MODEL)), dps, dlng, dlnb, dg1post, dg2pre, dg2post],
                           axis=0)
    smalls = [rows, dbin, dws.reshape(N_HEADS * SGU_BLOCK, SGU_BLOCK), dbsp_t.T]
    got = _pair_exchange("pair_exchange_in", bigs, smalls)
    sums = _pair_sum_big("pair_sum_in", c_arr, bigs, got[:3], smalls, got[3:])
    chip_bigs, chip_smalls = sums[:3], sums[3:]

    def backward_rest(token):
        dx, dg1pre = _bwd_in(token, dz_b, x2, dh1, row(norm1_pre_g), win_f)
        p_out = _wgrad_out(mg_b, dy_b)
        got_out = _pair_exchange("pair_exchange_out", [p_out], [dg1pre])
        chip_out, chip_g1pre = _pair_sum_big("pair_sum_out", c_arr, [p_out], got_out[:1], [dg1pre], got_out[1:])
        upd_ff = _adamw_big("adamw_ff", p_out, summed_ff, [(w_ff1, m_w_ff1, v_w_ff1), (w_ff2, m_w_ff2, v_w_ff2)])
        return upd_ff[1][0], dx, chip_out, chip_g1pre, upd_ff

    summed_in, (_, dx, chip_out, chip_g1pre, upd_ff) = _chip_exchange_behind("in", chip_bigs, chip_smalls,
                                                                             backward_rest)

    def update_in(token):
        res = _adamw_big("adamw_in", token, summed_in[:1], [(w_in, m_w_in, v_w_in)])[0]
        return res[0], res

    summed_out, (_, upd_in) = _chip_exchange_behind("out", [chip_out], [chip_g1pre], update_in)
    big = {"in": upd_in, "ff1": upd_ff[0], "ff2": upd_ff[1]}
    ws2 = lambda a: a.reshape(N_HEADS * SGU_BLOCK, SGU_BLOCK)
    small_params = [(row(pool_scale), row(m_pool_scale), row(v_pool_scale)),
                    (row(sgu_ln_g), row(m_sgu_ln_g), row(v_sgu_ln_g)),
                    (row(sgu_ln_b), row(m_sgu_ln_b), row(v_sgu_ln_b)),
                    (row(norm1_post_g), row(m_norm1_post_g), row(v_norm1_post_g)),
                    (row(norm2_pre_g), row(m_norm2_pre_g), row(v_norm2_pre_g)),
                    (row(norm2_post_g), row(m_norm2_post_g), row(v_norm2_post_g)),
                    (row(norm1_pre_g), row(m_norm1_pre_g), row(v_norm1_pre_g)),
                    (row(b_in), row(m_b_in), row(v_b_in)),
                    (ws2(w_spatial), ws2(m_w_spatial), ws2(v_w_spatial)),
                    (b_spatial, m_b_spatial, v_b_spatial),
                    (pg2(w_pool), pg2(m_w_pool), pg2(v_w_pool)),
                    (pg2(w_sgu_proj), pg2(m_w_sgu_proj), pg2(v_w_sgu_proj)),
                    (w_out, m_w_out, v_w_out)]
    small_out = _adamw_small(summed_in[3], [summed_out[1]] + list(summed_in[4:])
                             + [summed_in[1], summed_in[2], summed_out[0]], small_params)
    loss = small_out[0][0, 0]
    small_names = SMALL_ROWS[1:] + ("norm1_pre_g", "b_in", "w_spatial", "b_spatial", "w_pool", "w_sgu_proj", "w_out")
    small = {nm: small_out[1 + 4 * p:5 + 4 * p] for p, nm in enumerate(small_names)}

    shapes = {"norm1_pre_g": norm1_pre_g.shape, "w_in": w_in.shape, "b_in": b_in.shape, "w_pool": w_pool.shape,
              "pool_scale": pool_scale.shape, "sgu_ln_g": sgu_ln_g.shape, "sgu_ln_b": sgu_ln_b.shape,
              "w_spatial": w_spatial.shape, "b_spatial": b_spatial.shape, "w_sgu_proj": w_sgu_proj.shape,
              "w_out": w_out.shape, "norm1_post_g": norm1_post_g.shape, "norm2_pre_g": norm2_pre_g.shape,
              "w_ff1": w_ff1.shape, "w_ff2": w_ff2.shape, "norm2_post_g": norm2_post_g.shape}
    source = {"w_in": big["in"], "w_ff1": big["ff1"], "w_ff2": big["ff2"], **small}
    order = list(shapes)
    outs = [loss, dx.reshape(x.shape)]
    for kind in range(4):
        outs += [source[nm][kind].reshape(shapes[nm]) for nm in order]
    return tuple(outs)
```

```python
import math

import jax
import jax.numpy as jnp
from jax import lax
from jax.experimental import pallas as pl
from jax.experimental.pallas import tpu as pltpu

F32, BF16 = jnp.float32, jnp.bfloat16
MESH = pl.DeviceIdType.MESH

D_MODEL = 1024
D_IN = 5120
D_FF = 4096
N_DEV = 8
N_CHIPS = 4
WINDOWS = (2, 4, 8, 16)
N_GROUPS = 4
GROUP = 256
HALO = 16
SGU_BLOCK = 128
N_HEADS = 4
HEAD = 256
CHUNK = 64
EPS = 1e-6
IN_SHARD = D_IN // N_DEV
FF_SHARD = D_FF // N_DEV
OUT_SHARD = D_MODEL // N_DEV
PG_SHARD = GROUP // N_DEV

ADAM_LR, ADAM_B1, ADAM_B2, ADAM_EPS, ADAM_WD, ADAM_STEP = 0.001, 0.9, 0.999, 1e-08, 0.01, 10

VMEM_LIMIT = 56 * 1024 * 1024
TM = 256
TM_BWD = 256
TM_IN = 512
PROJ_COLS = 512
GELU_C0 = math.sqrt(2.0 / math.pi)
GELU_C1 = 0.044715


def _dot(a, b):
    return jnp.dot(a, b, preferred_element_type=F32)


def _dot_nt(a, b):
    return lax.dot_general(a, b, (((1,), (1,)), ((), ())), preferred_element_type=F32)


def _dot_tn(a, b):
    return lax.dot_general(a, b, (((0,), (0,)), ((), ())), preferred_element_type=F32)


def _gelu(x):
    t = jnp.tanh(GELU_C0 * (x + GELU_C1 * (x * x * x)))
    return 0.5 * x * (1.0 + t), t


def _gelu_grad(x, t):
    return 0.5 * (1.0 + t) + 0.5 * x * (1.0 - t * t) * (GELU_C0 * (1.0 + 3.0 * GELU_C1 * (x * x)))


def _sigmoid(x):
    return 1.0 / (1.0 + jnp.exp(-x))


def _mean(x):
    return jnp.mean(x, axis=-1, keepdims=True)


def _colsum(x):
    return jnp.sum(x, axis=0, keepdims=True)


def _const_spec(shape):
    nd = len(shape)
    return pl.BlockSpec(shape, lambda *_: (0,) * nd, pipeline_mode=pl.Buffered(1))


def _acc_spec(shape):
    nd = len(shape)
    return pl.BlockSpec(shape, lambda *_: (0,) * nd)


def _masked_ws(ws_ref):
    ri = lax.broadcasted_iota(jnp.int32, (SGU_BLOCK, SGU_BLOCK), 0) // CHUNK
    ci = lax.broadcasted_iota(jnp.int32, (SGU_BLOCK, SGU_BLOCK), 1) // CHUNK
    return [jnp.where(ri >= ci, ws_ref[h], 0.0).astype(BF16) for h in range(N_HEADS)]


def _pool_fwd(pbuf, tile_idx, tm):
    pos = lax.broadcasted_iota(jnp.int32, (tm, 1), 0) + tile_idx * tm + 1
    pooled = []
    for g, w in enumerate(WINDOWS):
        e = pbuf[:, g * GROUP:(g + 1) * GROUP]
        s, sh = e, 1
        while sh < w:
            s = s + pltpu.roll(s, sh, 0)
            sh *= 2
        inv = 1.0 / jnp.minimum(pos, w).astype(F32)
        pooled.append(s[HALO:] * inv - e[HALO:])
    return pooled


def _spatial_head(ws_h, vb, bsp_ref, h, nblk):
    return jnp.concatenate(
        [_dot(ws_h, vb[n * SGU_BLOCK:(n + 1) * SGU_BLOCK, h * HEAD:(h + 1) * HEAD]) + bsp_ref[:, h:h + 1]
         for n in range(nblk)], axis=0)


def _prenorm(after, x, g1pre):
    t_len = x.shape[0]
    tm = TM_IN

    def body(after_ref, x_ref, g_ref, xn_ref, xnt_ref):
        del after_ref
        xv = x_ref[...]
        xnb = (xv * lax.rsqrt(_mean(xv * xv) + EPS) * g_ref[...]).astype(BF16)
        xn_ref[...] = xnb
        xnt_ref[...] = xnb.T

    return pl.pallas_call(
        body, name="prenorm", grid=(t_len // tm,),
        in_specs=[_ANY, pl.BlockSpec((tm, D_MODEL), lambda i: (i, 0)), _const_spec((1, D_MODEL))],
        out_specs=[pl.BlockSpec((tm, D_MODEL), lambda i: (i, 0)), pl.BlockSpec((D_MODEL, tm), lambda i: (0, i))],
        out_shape=[jax.ShapeDtypeStruct((t_len, D_MODEL), BF16), jax.ShapeDtypeStruct((D_MODEL, t_len), BF16)],
        compiler_params=pltpu.CompilerParams(dimension_semantics=("arbitrary",)),
    )(after, x, g1pre)


def _fwd_mix(after, xn, x, win_g, b_in, wpool, pool_scale, lng, lnb, ws, bsp_t, wproj, wout, g1post):
    t_len = x.shape[0]
    tm = TM
    nt = t_len // tm

    def body(after_ref, xn_ref, xb_ref, win_ref, bin_ref, wpool_ref, ps_ref, lng_ref, lnb_ref, ws_ref, bsp_ref,
             wproj_ref, wout_ref, g1post_ref, z_ref, y_ref, h1_ref, u_ref, gpu_ref, xhat_ref, gpv_ref, sa_ref,
             sb_ref, zcur, znext, pbuf):
        del after_ref
        s = pl.program_id(0)

        @pl.when(s == 0)
        def _():
            znext[...] = jnp.zeros((tm, D_IN), F32)
            pbuf[...] = jnp.zeros((tm + HALO, D_MODEL), F32)

        zcur[...] = znext[...]
        xnb = xn_ref[...]

        def project(p):
            cols = slice(p * PROJ_COLS, (p + 1) * PROJ_COLS)
            zp = _dot(xnb, win_ref[:, cols]) + bin_ref[:, cols]
            if (p + 1) * PROJ_COLS <= D_MODEL:
                z_ref[:, cols] = zp
            znext[:, cols] = zp

        project(0)
        pbuf[0:HALO, :] = jnp.where(s <= 1, 0.0, pbuf[0:HALO, :])
        pbuf[HALO:, :] = zcur[:, 0:D_MODEL]
        pooled = _pool_fwd(pbuf, jnp.maximum(s - 1, 0), tm)
        pbuf[0:HALO, :] = pbuf[tm:tm + HALO, :]
        a = jnp.concatenate([_dot(pooled[g].astype(BF16), wpool_ref[g]) for g in range(N_GROUPS)], axis=1)
        a = a * ps_ref[...]
        project(1)
        zu = zcur[:, D_MODEL:2 * D_MODEL]
        u, tu = _gelu(zu)
        u_ref[...] = u.astype(BF16)
        gpu_ref[...] = _gelu_grad(zu, tu).astype(BF16)
        project(2)
        zv = zcur[:, 2 * D_MODEL:3 * D_MODEL]
        gv, tv = _gelu(zv)
        xc = gv - _mean(gv)
        rln = lax.rsqrt(_mean(xc * xc) + EPS)
        xhat = xc * rln
        xhat_ref[...] = xhat.astype(BF16)
        gpv_ref[...] = (_gelu_grad(zv, tv) * rln).astype(BF16)
        vb = (xhat * lng_ref[...] + lnb_ref[...]).astype(BF16)
        wsm = _masked_ws(ws_ref)
        bbr = []
        for h in range(N_HEADS):
            project(3 + h)
            sv = _spatial_head(wsm[h], vb, bsp_ref, h, tm // SGU_BLOCK)
            bbr.append(_dot((u[:, h * HEAD:(h + 1) * HEAD] * sv).astype(BF16), wproj_ref[h]))
        bbr = jnp.concatenate(bbr, axis=1)
        project(7)
        sa = _sigmoid(zcur[:, 3 * D_MODEL:4 * D_MODEL])
        sb = _sigmoid(zcur[:, 4 * D_MODEL:5 * D_MODEL])
        sa_ref[...] = sa.astype(BF16)
        sb_ref[...] = sb.astype(BF16)
        project(8)
        yv = _dot((sa * a + sb * bbr).astype(BF16), wout_ref[...])
        y_ref[...] = yv
        project(9)
        ry = lax.rsqrt(_mean(yv * yv) + EPS)
        h1_ref[...] = xb_ref[...] + yv * ry * g1post_ref[...]

    proj = lambda w: pl.BlockSpec((tm, w), lambda s: (jnp.minimum(s, nt - 1), 0))
    mix = lambda w: pl.BlockSpec((tm, w), lambda s: (jnp.maximum(s - 1, 0), 0))
    return pl.pallas_call(
        body, name="fwd_mix", grid=(nt + 1,),
        in_specs=[_ANY, proj(D_MODEL), mix(D_MODEL),
                  _const_spec((D_MODEL, D_IN)),
                  _const_spec((1, D_IN)), _const_spec((N_GROUPS, GROUP, GROUP)), _const_spec((1, D_MODEL)),
                  _const_spec((1, D_MODEL)), _const_spec((1, D_MODEL)),
                  _const_spec((N_HEADS, SGU_BLOCK, SGU_BLOCK)), _const_spec((SGU_BLOCK, N_HEADS)),
                  _const_spec((N_HEADS, HEAD, HEAD)), _const_spec((D_MODEL, D_MODEL)), _const_spec((1, D_MODEL))],
        out_specs=[proj(D_MODEL), mix(D_MODEL), mix(D_MODEL)] + [mix(D_MODEL)] * 6,
        out_shape=[jax.ShapeDtypeStruct((t_len, D_MODEL), F32),
                   jax.ShapeDtypeStruct((t_len, D_MODEL), F32), jax.ShapeDtypeStruct((t_len, D_MODEL), F32)]
        + [jax.ShapeDtypeStruct((t_len, D_MODEL), BF16)] * 6,
        scratch_shapes=[pltpu.VMEM((tm, D_IN), F32), pltpu.VMEM((tm, D_IN), F32),
                        pltpu.VMEM((tm + HALO, D_MODEL), F32)],
        compiler_params=pltpu.CompilerParams(dimension_semantics=("arbitrary",), vmem_limit_bytes=VMEM_LIMIT),
    )(after, xn, x, win_g, b_in, wpool, pool_scale, lng, lnb, ws, bsp_t, wproj, wout, g1post)


def _mlp(h1, target, g2pre, g2post, w1_g, w2):
    t_len = h1.shape[0]
    tm = TM
    nt = t_len // tm

    def body(h1_ref, tgt_ref, g2pre_ref, g2post_ref, w1_ref, w2_ref,
             hn_ref, f_ref, df1_ref, df2_ref, dh1_ref, dg2post_ref, dg2pre_ref, loss_ref, f1_scr):
        i = pl.program_id(0)

        @pl.when(i == 0)
        def _():
            dg2post_ref[...] = jnp.zeros_like(dg2post_ref)
            dg2pre_ref[...] = jnp.zeros_like(dg2pre_ref)
            loss_ref[...] = jnp.zeros_like(loss_ref)

        h = h1_ref[...]
        r2 = lax.rsqrt(_mean(h * h) + EPS)
        nh = h * r2
        hnb = (nh * g2pre_ref[...]).astype(BF16)
        hn_ref[...] = hnb.T
        for k in range(N_DEV):
            f1_scr[:, k * FF_SHARD:(k + 1) * FF_SHARD] = _dot(hnb, w1_ref[:, k * FF_SHARD:(k + 1) * FF_SHARD])
        r = jnp.maximum(f1_scr[...], 0.0)
        fb = (r * r).astype(BF16)
        f_ref[...] = fb.T
        f2 = _dot(fb, w2_ref[...])
        rf = lax.rsqrt(_mean(f2 * f2) + EPS)
        nf = f2 * rf
        diff = h + nf * g2post_ref[...] - tgt_ref[...]
        loss_ref[...] += (0.5 / D_MODEL) * jnp.sum(diff * diff)
        dout = diff * (1.0 / D_MODEL)
        dg2post_ref[...] += _colsum(dout * nf)
        dn = dout * g2post_ref[...]
        df2b = (rf * (dn - nf * _mean(dn * nf))).astype(BF16)
        df2_ref[...] = df2b
        df = _dot_nt(df2b, w2_ref[...])
        df1b = (df * (2.0 * jnp.maximum(f1_scr[...], 0.0))).astype(BF16)
        df1_ref[...] = df1b
        dhn = _dot_nt(df1b, w1_ref[...])
        dg2pre_ref[...] += _colsum(dhn * nh)
        dnh = dhn * g2pre_ref[...]
        dh1_ref[...] = dout + r2 * (dnh - nh * _mean(dnh * nh))

    tok = lambda w: pl.BlockSpec((tm, w), lambda i: (i, 0))
    return pl.pallas_call(
        body, name="mlp_fwd_bwd", grid=(nt,),
        in_specs=[tok(D_MODEL), tok(D_MODEL), _const_spec((1, D_MODEL)), _const_spec((1, D_MODEL)),
                  _const_spec((D_MODEL, D_FF)), _const_spec((D_FF, D_MODEL))],
        out_specs=[pl.BlockSpec((D_MODEL, tm), lambda i: (0, i)), pl.BlockSpec((D_FF, tm), lambda i: (0, i)),
                   tok(D_FF), tok(D_MODEL), tok(D_MODEL),
                   _acc_spec((1, D_MODEL)), _acc_spec((1, D_MODEL)), _acc_spec((1, 128))],
        out_shape=[jax.ShapeDtypeStruct((D_MODEL, t_len), BF16), jax.ShapeDtypeStruct((D_FF, t_len), BF16),
                   jax.ShapeDtypeStruct((t_len, D_FF), BF16), jax.ShapeDtypeStruct((t_len, D_MODEL), BF16),
                   jax.ShapeDtypeStruct((t_len, D_MODEL), F32), jax.ShapeDtypeStruct((1, D_MODEL), F32),
                   jax.ShapeDtypeStruct((1, D_MODEL), F32), jax.ShapeDtypeStruct((1, 128), F32)],
        scratch_shapes=[pltpu.VMEM((tm, D_FF), F32)],
        compiler_params=pltpu.CompilerParams(dimension_semantics=("arbitrary",), vmem_limit_bytes=VMEM_LIMIT),
    )(h1, target, g2pre, g2post, w1_g, w2)


def _bwd_mix(after, dh1, y, z, saved, wpool, pool_scale, lng, lnb, ws, bsp_t, wproj, wout, g1post):
    t_len = y.shape[0]
    tm = TM_BWD
    nt = t_len // tm
    nblk = tm // SGU_BLOCK

    def body(after_ref, dh1_ref, y_ref, z_ref, zh_ref, u_ref, gpu_ref, xhat_ref, gpv_ref, sa_ref, sb_ref,
             wpool_ref, ps_ref, lng_ref, lnb_ref, ws_ref,
             bsp_ref, wproj_ref, wout_ref, g1post_ref,
             dz_ref, mg_ref, dy_ref, ppool_ref, pproj_ref, dws_ref, dbsp_ref, dg1post_ref, dps_ref,
             dlng_ref, dlnb_ref, dbin_ref, pbuf, qbuf, dwpool_ref, dwproj_ref):
        del after_ref
        i = pl.program_id(0)
        ti = nt - 1 - i

        @pl.when(i == 0)
        def _():
            for ref in (dwpool_ref, dwproj_ref, dws_ref, dbsp_ref, dg1post_ref, dps_ref, dlng_ref, dlnb_ref,
                        dbin_ref):
                ref[...] = jnp.zeros_like(ref)
            qbuf[tm:tm + HALO, :] = jnp.zeros((HALO, D_MODEL), F32)

        pbuf[0:HALO, :] = jnp.where(ti > 0, zh_ref[...], 0.0)
        pbuf[HALO:, :] = z_ref[...]
        pooled = _pool_fwd(pbuf, ti, tm)
        pooled_b = [p.astype(BF16) for p in pooled]
        a_raw = jnp.concatenate([_dot(pooled_b[g], wpool_ref[g]) for g in range(N_GROUPS)], axis=1)
        wsm = _masked_ws(ws_ref)
        u = u_ref[...].astype(F32)
        xhat = xhat_ref[...].astype(F32)
        vb = (xhat * lng_ref[...] + lnb_ref[...]).astype(BF16)
        sv_heads = [_spatial_head(wsm[h], vb, bsp_ref, h, nblk) for h in range(N_HEADS)]
        gated_b = [(u[:, h * HEAD:(h + 1) * HEAD] * sv_heads[h]).astype(BF16) for h in range(N_HEADS)]
        bbr = jnp.concatenate([_dot(gated_b[h], wproj_ref[h]) for h in range(N_HEADS)], axis=1)
        sa = sa_ref[...].astype(F32)
        sb = sb_ref[...].astype(F32)
        a = a_raw * ps_ref[...]
        mg_ref[...] = (sa * a + sb * bbr).astype(BF16).T

        dh = dh1_ref[...]
        yv = y_ref[...]
        ry = lax.rsqrt(_mean(yv * yv) + EPS)
        ny = yv * ry
        dg1post_ref[...] += _colsum(dh * ny)
        dn = dh * g1post_ref[...]
        dyb = (ry * (dn - ny * _mean(dn * ny))).astype(BF16)
        dy_ref[...] = dyb
        dmg = _dot_nt(dyb, wout_ref[...])

        da = dmg * sa
        dbbr = dmg * sb
        dzga = dmg * a * sa * (1.0 - sa)
        dzgb = dmg * bbr * sb * (1.0 - sb)
        dz_ref[:, 3 * D_MODEL:4 * D_MODEL] = dzga.astype(BF16)
        dz_ref[:, 4 * D_MODEL:5 * D_MODEL] = dzgb.astype(BF16)
        dbin_ref[:, 3 * D_MODEL:4 * D_MODEL] += _colsum(dzga)
        dbin_ref[:, 4 * D_MODEL:5 * D_MODEL] += _colsum(dzgb)

        dps_ref[...] += _colsum(da * a_raw)
        da_raw_b = (da * ps_ref[...]).astype(BF16)
        pos = lax.broadcasted_iota(jnp.int32, (tm, 1), 0) + ti * tm + 1
        dpooled = []
        for g, w in enumerate(WINDOWS):
            cols = slice(g * GROUP, (g + 1) * GROUP)
            dwpool_ref[g] += _dot_tn(pooled_b[g], da_raw_b[:, cols])
            dp = _dot_nt(da_raw_b[:, cols], wpool_ref[g])
            dpooled.append(dp)
            qbuf[0:tm, cols] = dp * (1.0 / jnp.minimum(pos, w).astype(F32))
        n_ext = tm + HALO
        dzp = []
        for g, w in enumerate(WINDOWS):
            e = qbuf[:, g * GROUP:(g + 1) * GROUP]
            s, sh = e, 1
            while sh < w:
                s = s + pltpu.roll(s, n_ext - sh, 0)
                sh *= 2
            dzp.append(s[0:tm] - dpooled[g])
        qbuf[tm:tm + HALO, :] = qbuf[0:HALO, :]
        dzp = jnp.concatenate(dzp, axis=1)
        dz_ref[:, 0:D_MODEL] = dzp.astype(BF16)
        dbin_ref[:, 0:D_MODEL] += _colsum(dzp)

        dv_heads = []
        du_heads = []
        for h in range(N_HEADS):
            cols = slice(h * HEAD, (h + 1) * HEAD)
            dbbr_b = dbbr[:, cols].astype(BF16)
            dwproj_ref[h] += _dot_tn(gated_b[h], dbbr_b)
            dgated = _dot_nt(dbbr_b, wproj_ref[h])
            du_heads.append(dgated * sv_heads[h])
            dsv = dgated * u[:, cols]
            dsv_b = dsv.astype(BF16)
            rows = []
            for n in range(nblk):
                blk = slice(n * SGU_BLOCK, (n + 1) * SGU_BLOCK)
                rows.append(_dot_tn(wsm[h], dsv_b[blk]))
                dws_ref[h] += _dot_nt(dsv_b[blk], vb[blk, cols])
                dbsp_ref[:, h:h + 1] += jnp.sum(dsv[blk], axis=1, keepdims=True)
            dv_heads.append(jnp.concatenate(rows, axis=0))
        dzu = jnp.concatenate(du_heads, axis=1) * gpu_ref[...].astype(F32)
        dz_ref[:, D_MODEL:2 * D_MODEL] = dzu.astype(BF16)
        dbin_ref[:, D_MODEL:2 * D_MODEL] += _colsum(dzu)
        dv = jnp.concatenate(dv_heads, axis=1)
        dlng_ref[...] += _colsum(dv * xhat)
        dlnb_ref[...] += _colsum(dv)
        dxh = dv * lng_ref[...]
        dzv = (dxh - _mean(dxh) - xhat * _mean(dxh * xhat)) * gpv_ref[...].astype(F32)
        dz_ref[:, 2 * D_MODEL:3 * D_MODEL] = dzv.astype(BF16)
        dbin_ref[:, 2 * D_MODEL:3 * D_MODEL] += _colsum(dzv)

        @pl.when(i == nt - 1)
        def _():
            ri = lax.broadcasted_iota(jnp.int32, (SGU_BLOCK, SGU_BLOCK), 0) // CHUNK
            ci = lax.broadcasted_iota(jnp.int32, (SGU_BLOCK, SGU_BLOCK), 1) // CHUNK
            for h in range(N_HEADS):
                dws_ref[h] = jnp.where(ri >= ci, dws_ref[h], 0.0)
            for slot in range(N_DEV):
                own = slice(_owner_of_slot(slot) * PG_SHARD, (_owner_of_slot(slot) + 1) * PG_SHARD)
                for g in range(N_GROUPS):
                    rows = slice(g * PG_SHARD, (g + 1) * PG_SHARD)
                    ppool_ref[slot, rows, :] = dwpool_ref[g, own, :].astype(BF16)
                    pproj_ref[slot, rows, :] = dwproj_ref[g, own, :].astype(BF16)

    tok = lambda w: pl.BlockSpec((tm, w), lambda i: (nt - 1 - i, 0))
    halo = pl.BlockSpec((HALO, D_MODEL), lambda i: (jnp.maximum((nt - 1 - i) * (tm // HALO) - 1, 0), 0))
    return pl.pallas_call(
        body, name="bwd_mix", grid=(nt,),
        in_specs=[_ANY, tok(D_MODEL), tok(D_MODEL), tok(D_MODEL), halo] + [tok(D_MODEL)] * 6
        + [_const_spec((N_GROUPS, GROUP, GROUP)),
                  _const_spec((1, D_MODEL)), _const_spec((1, D_MODEL)), _const_spec((1, D_MODEL)),
                  _const_spec((N_HEADS, SGU_BLOCK, SGU_BLOCK)), _const_spec((SGU_BLOCK, N_HEADS)),
                  _const_spec((N_HEADS, HEAD, HEAD)), _const_spec((D_MODEL, D_MODEL)), _const_spec((1, D_MODEL))],
        out_specs=[tok(D_IN), pl.BlockSpec((D_MODEL, tm), lambda i: (0, nt - 1 - i)), tok(D_MODEL),
                   _acc_spec((N_DEV, N_GROUPS * PG_SHARD, GROUP)), _acc_spec((N_DEV, N_HEADS * PG_SHARD, HEAD)),
                   _acc_spec((N_HEADS, SGU_BLOCK, SGU_BLOCK)), _acc_spec((SGU_BLOCK, N_HEADS)),
                   _acc_spec((1, D_MODEL)), _acc_spec((1, D_MODEL)), _acc_spec((1, D_MODEL)), _acc_spec((1, D_MODEL)),
                   _acc_spec((1, D_IN))],
        out_shape=[jax.ShapeDtypeStruct((t_len, D_IN), BF16),
                   jax.ShapeDtypeStruct((D_MODEL, t_len), BF16), jax.ShapeDtypeStruct((t_len, D_MODEL), BF16),
                   jax.ShapeDtypeStruct((N_DEV, N_GROUPS * PG_SHARD, GROUP), BF16),
                   jax.ShapeDtypeStruct((N_DEV, N_HEADS * PG_SHARD, HEAD), BF16),
                   jax.ShapeDtypeStruct((N_HEADS, SGU_BLOCK, SGU_BLOCK), F32),
                   jax.ShapeDtypeStruct((SGU_BLOCK, N_HEADS), F32),
                   jax.ShapeDtypeStruct((1, D_MODEL), F32), jax.ShapeDtypeStruct((1, D_MODEL), F32),
                   jax.ShapeDtypeStruct((1, D_MODEL), F32), jax.ShapeDtypeStruct((1, D_MODEL), F32),
                   jax.ShapeDtypeStruct((1, D_IN), F32)],
        scratch_shapes=[pltpu.VMEM((tm + HALO, D_MODEL), F32), pltpu.VMEM((tm + HALO, D_MODEL), F32),
                        pltpu.VMEM((N_GROUPS, GROUP, GROUP), F32), pltpu.VMEM((N_HEADS, HEAD, HEAD), F32)],
        compiler_params=pltpu.CompilerParams(dimension_semantics=("arbitrary",), vmem_limit_bytes=VMEM_LIMIT),
    )(after, dh1, y, z, z, *saved, wpool, pool_scale, lng, lnb, ws, bsp_t, wproj, wout, g1post)


def _bwd_in(after, dz_b, x, dh1, g1pre, win_g):
    t_len = x.shape[0]
    tm = TM_IN
    nt = t_len // tm

    def body(after_ref, dz_ref, x_ref, dh1_ref, g1_ref, win_ref, dx_ref, dg1pre_ref):
        del after_ref

        @pl.when(pl.program_id(0) == 0)
        def _():
            dg1pre_ref[...] = jnp.zeros_like(dg1pre_ref)

        dxn = _dot_nt(dz_ref[...], win_ref[...])
        xv = x_ref[...]
        r1 = lax.rsqrt(_mean(xv * xv) + EPS)
        nx = xv * r1
        dg1pre_ref[...] += _colsum(dxn * nx)
        dnx = dxn * g1_ref[...]
        dx_ref[...] = r1 * (dnx - nx * _mean(dnx * nx)) + dh1_ref[...]

    tok = lambda w: pl.BlockSpec((tm, w), lambda i: (i, 0))
    return pl.pallas_call(
        body, name="bwd_in", grid=(nt,),
        in_specs=[_ANY, tok(D_IN), tok(D_MODEL), tok(D_MODEL), _const_spec((1, D_MODEL)),
                  _const_spec((D_MODEL, D_IN))],
        out_specs=[tok(D_MODEL), _acc_spec((1, D_MODEL))],
        out_shape=[jax.ShapeDtypeStruct((t_len, D_MODEL), F32), jax.ShapeDtypeStruct((1, D_MODEL), F32)],
        compiler_params=pltpu.CompilerParams(dimension_semantics=("arbitrary",), vmem_limit_bytes=VMEM_LIMIT),
    )(after, dz_b, x, dh1, g1pre, win_g)


def _owner_of_slot(s):
    return 4 * ((s // 2) % 2) + 2 * (s % 2) + s // 4


def _wgrad_ff(c_arr, f_t, df2_b, hn_t, df1_b):
    t_len = df2_b.shape[0]

    def body(c_ref, f_ref, df2_ref, hn_ref, df1_ref, o2_ref, o1_ref, got2_ref, got1_ref, buf2, buf1, send_sems,
             recv_sems):
        del c_ref
        s = pl.program_id(0)
        x, y, c = _coords()

        def copies(k):
            return [pltpu.make_async_remote_copy(src_ref=buf.at[k], dst_ref=got.at[k], send_sem=send_sems.at[j, k],
                                                 recv_sem=recv_sems.at[j, k], device_id=(x, y, 1 - c),
                                                 device_id_type=MESH)
                    for j, (buf, got) in enumerate(((buf2, got2_ref), (buf1, got1_ref)))]

        @pl.when(s < N_CHIPS)
        def _():
            buf2[s] = _dot(f_ref[...], df2_ref[...]).astype(BF16)
            buf1[s] = _dot(hn_ref[...], df1_ref[...]).astype(BF16)
            for cp in copies(s):
                cp.start()

        @pl.when(s >= N_CHIPS)
        def _():
            o2_ref[...] = _dot(f_ref[...], df2_ref[...]).astype(BF16)
            o1_ref[...] = _dot(hn_ref[...], df1_ref[...]).astype(BF16)

        @pl.when(s == N_DEV - 1)
        def _():
            for k in range(N_CHIPS):
                for cp in copies(k):
                    cp.wait()

    owner = lambda s, c_ref: 2 * (s % N_CHIPS) + jnp.where(s < N_CHIPS, 1 - c_ref[0], c_ref[0])
    mine = lambda s, c_ref: (jnp.maximum(s - N_CHIPS, 0), 0, 0)
    return pl.pallas_call(
        body, name="wgrad_ff",
        grid_spec=pltpu.PrefetchScalarGridSpec(
            num_scalar_prefetch=1, grid=(N_DEV,),
            in_specs=[pl.BlockSpec((FF_SHARD, t_len), lambda s, c_ref: (owner(s, c_ref), 0)),
                      pl.BlockSpec((t_len, D_MODEL), lambda s, c_ref: (0, 0), pipeline_mode=pl.Buffered(1)),
                      pl.BlockSpec((D_MODEL, t_len), lambda s, c_ref: (0, 0), pipeline_mode=pl.Buffered(1)),
                      pl.BlockSpec((t_len, FF_SHARD), lambda s, c_ref: (0, owner(s, c_ref)))],
            out_specs=[pl.BlockSpec((None, FF_SHARD, D_MODEL), mine),
                       pl.BlockSpec((None, D_MODEL, FF_SHARD), mine), _ANY, _ANY],
            scratch_shapes=[pltpu.VMEM((N_CHIPS, FF_SHARD, D_MODEL), BF16),
                            pltpu.VMEM((N_CHIPS, D_MODEL, FF_SHARD), BF16),
                            pltpu.SemaphoreType.DMA((2, N_CHIPS)), pltpu.SemaphoreType.DMA((2, N_CHIPS))]),
        out_shape=[jax.ShapeDtypeStruct((N_CHIPS, FF_SHARD, D_MODEL), BF16),
                   jax.ShapeDtypeStruct((N_CHIPS, D_MODEL, FF_SHARD), BF16)] * 2,
        compiler_params=pltpu.CompilerParams(dimension_semantics=("arbitrary",), vmem_limit_bytes=VMEM_LIMIT),
    )(c_arr, f_t, df2_b, hn_t, df1_b)


def _wgrad_pair(name, a, b, in_specs, halves, slot_shape, bigs, smalls):
    nb, nx = len(bigs), len(bigs) + len(smalls)

    def body(a_ref, b_ref, *refs):
        extra_src, (mine_ref, got_ref), extra_dst = refs[:nx], refs[nx:nx + 2], refs[nx + 2:2 * nx + 2]
        both, keep_sems, send_sems, recv_sems, extra_send, extra_recv = refs[2 * nx + 2:]
        q = pl.program_id(0)
        x, y, c = _coords()
        to_sibling = dict(device_id=(x, y, 1 - c), device_id_type=MESH)

        def slot_copies(k):
            return [pltpu.make_async_copy(both.at[k, c], mine_ref.at[k], keep_sems.at[k]),
                    pltpu.make_async_remote_copy(src_ref=both.at[k, 1 - c], dst_ref=got_ref.at[k],
                                                 send_sem=send_sems.at[k], recv_sem=recv_sems.at[k], **to_sibling)]

        def extra_copies():
            return [pltpu.make_async_remote_copy(
                src_ref=extra_src[j].at[pl.ds(N_CHIPS * (1 - c), N_CHIPS)] if j < nb else extra_src[j],
                dst_ref=extra_dst[j], send_sem=extra_send.at[j], recv_sem=extra_recv.at[j], **to_sibling)
                for j in range(nx)]

        @pl.when(q == 0)
        def _():
            for cp in extra_copies():
                cp.start()

        half0, half1 = halves(_dot(a_ref[...], b_ref[...]))
        both[q, 0] = half0.astype(BF16)
        both[q, 1] = half1.astype(BF16)
        for cp in slot_copies(q):
            cp.start()

        @pl.when(q == N_CHIPS - 1)
        def _():
            for k in range(N_CHIPS):
                for cp in slot_copies(k):
                    cp.wait()
            for cp in extra_copies():
                cp.wait()

    slots = jax.ShapeDtypeStruct((N_CHIPS,) + slot_shape, BF16)
    return pl.pallas_call(
        body, name=name, grid=(N_CHIPS,),
        in_specs=list(in_specs) + [_ANY] * nx,
        out_specs=[_ANY] * (2 + nx),
        out_shape=[slots, slots] + [jax.ShapeDtypeStruct((N_CHIPS,) + t.shape[1:], t.dtype) for t in bigs]
        + [jax.ShapeDtypeStruct(t.shape, t.dtype) for t in smalls],
        scratch_shapes=[pltpu.VMEM((N_CHIPS, 2) + slot_shape, BF16)] + [pltpu.SemaphoreType.DMA((N_CHIPS,))] * 3
        + [pltpu.SemaphoreType.DMA((max(nx, 1),))] * 2,
        compiler_params=pltpu.CompilerParams(dimension_semantics=("arbitrary",), vmem_limit_bytes=VMEM_LIMIT),
    )(a, b, *bigs, *smalls)


def _wgrad_in(xn_t, dz_b, bigs, smalls):
    t_len = dz_b.shape[0]
    return _wgrad_pair("wgrad_in", xn_t, dz_b,
                       [pl.BlockSpec((D_MODEL, t_len), lambda q: (0, 0), pipeline_mode=pl.Buffered(1)),
                        pl.BlockSpec((t_len, 2 * IN_SHARD), lambda q: (0, q))],
                       lambda res: (res[:, 0:IN_SHARD], res[:, IN_SHARD:2 * IN_SHARD]), (D_MODEL, IN_SHARD),
                       bigs, smalls)


def _wgrad_out(mg_t, dy_b, smalls):
    t_len = dy_b.shape[0]
    return _wgrad_pair("wgrad_out", mg_t, dy_b,
                       [pl.BlockSpec((2 * OUT_SHARD, t_len), lambda q: (q, 0)),
                        pl.BlockSpec((t_len, D_MODEL), lambda q: (0, 0), pipeline_mode=pl.Buffered(1))],
                       lambda res: (res[0:OUT_SHARD], res[OUT_SHARD:2 * OUT_SHARD]), (OUT_SHARD, D_MODEL),
                       [], smalls)


def _coords():
    return lax.axis_index("x"), lax.axis_index("y"), lax.axis_index("c")


_ANY = pl.BlockSpec(memory_space=pl.ANY)


_HBM = pl.BlockSpec(memory_space=pltpu.HBM)
_SEM = pl.BlockSpec(memory_space=pltpu.SEMAPHORE)
_VMEM = pl.BlockSpec(memory_space=pltpu.VMEM)
_EFFECT = pltpu.SideEffectType.DATAFLOW_SIDE_EFFECTING
_TOKEN = jax.ShapeDtypeStruct((8, 128), F32)


def _in_hbm(a):
    return pltpu.with_memory_space_constraint(a, pltpu.HBM)


def _split_call(name, body, n_sems_out, arrays, sems_in=(), after=None):
    na, ns = len(arrays), len(sems_in)
    has_after = after is not None

    def kernel_body(*refs):
        arr = refs[:na]
        s_in = refs[na:na + ns]
        outs = refs[na + ns + has_after:]
        body(arr, s_in, outs[:n_sems_out])
        outs[-1][...] = jnp.zeros((8, 128), F32)

    out_shape = ([pltpu.SemaphoreType.DMA(())] * n_sems_out + [pltpu.HBM(a.shape, a.dtype) for a in arrays] + [_TOKEN])
    res = pl.pallas_call(
        kernel_body, name=name, out_shape=out_shape,
        in_specs=[_HBM] * na + [_SEM] * ns + [_ANY] * has_after,
        out_specs=[_SEM] * n_sems_out + [_HBM] * na + [_VMEM],
        input_output_aliases={i: n_sems_out + i for i in range(na)},
        compiler_params=pltpu.CompilerParams(has_side_effects=_EFFECT),
    )(*[_in_hbm(a) for a in arrays], *sems_in, *([after] if has_after else []))
    return list(res[:n_sems_out]), list(res[n_sems_out:n_sems_out + na]), res[-1]


def _wait_bytes_of(ref, send_sem, recv_sem, peer, send=True, recv=True):
    cp = pltpu.make_async_remote_copy(src_ref=ref, dst_ref=ref, send_sem=send_sem, recv_sem=recv_sem,
                                      device_id=peer, device_id_type=MESH)
    if send:
        cp.wait_send()
    if recv:
        cp.wait_recv()


def _gather_behind(tag, shards, after, work, by_columns=(), by_rows=()):
    n = len(shards)

    def land_shape(j, s):
        if j in by_columns:
            return (s.shape[0], N_DEV * s.shape[1])
        if j in by_rows:
            return (s.shape[0], N_DEV * s.shape[1], s.shape[2])
        return (N_DEV,) + s.shape

    lands = [lax.empty(land_shape(j, s), s.dtype) for j, s in enumerate(shards)]

    def slots(arr, j, first, count=1):
        if j in by_columns:
            cols = shards[j].shape[1]
            return arr[n + j].at[:, pl.ds(pl.multiple_of(first * cols, 128), count * cols)]
        if j in by_rows:
            rows = shards[j].shape[1]
            return arr[n + j].at[:, pl.ds(pl.multiple_of(first * rows, 16), count * rows), :]
        return arr[n + j].at[first] if count == 1 else arr[n + j].at[pl.ds(first, count)]

    def own_slot(arr, j, sem):
        x, y, c = _coords()
        return pltpu.make_async_copy(arr[j], slots(arr, j, 4 * x + 2 * y + c), sem)

    def start(arr, _, sems):
        x, y, c = _coords()
        me = 4 * x + 2 * y + c
        for j in range(n):
            for chip in [(1 - x, y), (x, 1 - y), (1 - x, 1 - y)]:
                pltpu.make_async_remote_copy(src_ref=arr[j], dst_ref=slots(arr, j, me), send_sem=sems[j],
                                             recv_sem=sems[n + j], device_id=(*chip, c), device_id_type=MESH).start()
        for j in range(n):
            pltpu.make_async_remote_copy(src_ref=arr[j], dst_ref=slots(arr, j, me), send_sem=sems[2 * n + j],
                                         recv_sem=sems[3 * n + j], device_id=(x, y, 1 - c),
                                         device_id_type=MESH).start()
            own_slot(arr, j, sems[4 * n + j]).start()

    def middle(arr, s_in, sems):
        x, y, c = _coords()
        sibling = (x, y, 1 - c)
        for j in range(n):
            _wait_bytes_of(slots(arr, j, 0, 3), s_in[j], s_in[n + j], sibling)
            for chip in [(1 - x, y), (x, 1 - y), (1 - x, 1 - y)]:
                slot = slots(arr, j, 4 * chip[0] + 2 * chip[1] + c)
                pltpu.make_async_remote_copy(src_ref=slot, dst_ref=slot, send_sem=sems[j], recv_sem=sems[n + j],
                                             device_id=sibling, device_id_type=MESH).start()

    def finish(arr, s_in, _):
        x, y, c = _coords()
        sibling = (x, y, 1 - c)
        for j in range(n):
            _wait_bytes_of(slots(arr, j, 0, 1), s_in[j], s_in[n + j], sibling)
            own_slot(arr, j, s_in[2 * n + j]).wait()
            _wait_bytes_of(slots(arr, j, 0, 3), s_in[3 * n + j], s_in[4 * n + j], sibling)

    sems, arrays, token = _split_call("gather_%s_start" % tag, start, 5 * n, list(shards) + lands, after=after)
    result = work(token)
    fwd_sems, arrays, token = _split_call("gather_%s_middle" % tag, middle, 2 * n, arrays, sems_in=sems[:2 * n],
                                          after=result[0])
    _, arrays, _ = _split_call("gather_%s_finish" % tag, finish, 0, arrays, sems_in=sems[2 * n:] + fwd_sems,
                               after=token)
    return arrays[n:], result


def _chip_exchange_behind(tag, bigs, smalls, work):
    nb, n = len(bigs), len(bigs) + len(smalls)
    lands = [lax.empty(s.shape, s.dtype) for s in bigs] + [lax.empty((N_CHIPS,) + s.shape, s.dtype) for s in smalls]

    def own_slot(arr, j, sem):
        x, y, _ = _coords()
        q_me = 2 * x + y
        return pltpu.make_async_copy(arr[j].at[q_me] if j < nb else arr[j], arr[n + j].at[q_me], sem)

    def start(arr, _, sems):
        x, y, c = _coords()
        q_me = 2 * x + y
        for j in range(n):
            for peer in [(1 - x, y, c), (x, 1 - y, c), (1 - x, 1 - y, c)]:
                piece = arr[j].at[2 * peer[0] + peer[1]] if j < nb else arr[j]
                pltpu.make_async_remote_copy(src_ref=piece, dst_ref=arr[n + j].at[q_me],
                                             send_sem=sems[j], recv_sem=sems[n + j], device_id=peer,
                                             device_id_type=MESH).start()
            own_slot(arr, j, sems[2 * n + j]).start()

    def finish(arr, s_in, _):
        x, y, c = _coords()
        for j in range(n):
            _wait_bytes_of(arr[n + j].at[pl.ds(0, 3)], s_in[j], s_in[n + j], (x, y, 1 - c))
            own_slot(arr, j, s_in[2 * n + j]).wait()

    sems, arrays, token = _split_call("chip_exchange_%s_start" % tag, start, 3 * n, list(bigs) + list(smalls) + lands)
    result = work(token)
    _, arrays, _ = _split_call("chip_exchange_%s_finish" % tag, finish, 0, arrays, sems_in=sems, after=result[0])
    return arrays[n:], result


def _pair_sum_big(name, c_arr, mine, theirs, whole_mine=(), whole_theirs=()):
    n, nw = len(mine), len(whole_mine)

    def body(c_ref, *refs):
        del c_ref
        ins, outs = refs[:2 * (n + nw)], refs[2 * (n + nw):]
        for j in range(n):
            outs[j][...] = (ins[j][...].astype(F32) + ins[n + j][...].astype(F32)).astype(BF16)

        @pl.when(pl.program_id(0) == 0)
        def _():
            for j in range(nw):
                outs[n + j][...] = ins[2 * n + j][...] + ins[2 * n + nw + j][...]

    slot = lambda t: pl.BlockSpec((None,) + t.shape[1:], lambda q, c_ref: (q, 0, 0))
    own = lambda t: slot(t) if t.shape[0] == N_CHIPS else pl.BlockSpec(
        (None,) + t.shape[1:], lambda q, c_ref: (4 * c_ref[0] + q, 0, 0))
    whole = lambda t: pl.BlockSpec(t.shape, lambda q, c_ref: (0,) * t.ndim)
    return pl.pallas_call(
        body, name=name,
        grid_spec=pltpu.PrefetchScalarGridSpec(
            num_scalar_prefetch=1, grid=(N_CHIPS,),
            in_specs=[own(t) for t in mine] + [slot(t) for t in theirs]
            + [whole(t) for t in whole_mine] + [whole(t) for t in whole_theirs],
            out_specs=[slot(t) for t in theirs] + [whole(t) for t in whole_mine]),
        out_shape=[jax.ShapeDtypeStruct(t.shape, BF16) for t in theirs]
        + [jax.ShapeDtypeStruct(t.shape, t.dtype) for t in whole_mine],
        compiler_params=pltpu.CompilerParams(dimension_semantics=("arbitrary",)),
    )(c_arr, *mine, *theirs, *whole_mine, *whole_theirs)


def _adamw_math(w, g, m, v):
    m = ADAM_B1 * m + (1.0 - ADAM_B1) * g
    v = ADAM_B2 * v + (1.0 - ADAM_B2) * (g * g)
    m_hat = m / (1.0 - ADAM_B1 ** ADAM_STEP)
    v_hat = v / (1.0 - ADAM_B2 ** ADAM_STEP)
    delta = -ADAM_LR * (m_hat / (jnp.sqrt(v_hat) + ADAM_EPS) + ADAM_WD * w)
    return delta, m, v


ADAMW_STEPS = 2


def _adamw_big(name, after, chip_sums, params):
    n = len(params)

    def body(after_ref, *refs):
        del after_ref
        outs = refs[4 * n:]
        for j in range(n):
            t_ref, w_ref, m_ref, v_ref = refs[4 * j:4 * j + 4]
            g = t_ref[0].astype(F32)
            for q in range(1, N_CHIPS):
                g = g + t_ref[q].astype(F32)
            d, mn, vn = _adamw_math(w_ref[...], g, m_ref[...], v_ref[...])
            for out, val in zip(outs[4 * j:4 * j + 4], (g, d, mn, vn)):
                out[...] = val

    in_specs, out_specs, out_shape, operands = [_ANY], [], [], [after]
    for t, (w, m, v) in zip(chip_sums, params):
        rows, cols = w.shape
        tr = rows // ADAMW_STEPS
        blk = pl.BlockSpec((tr, cols), lambda r: (r, 0))
        in_specs += [pl.BlockSpec((N_CHIPS, tr, cols), lambda r: (0, r, 0)), blk, blk, blk]
        out_specs += [blk] * 4
        out_shape += [jax.ShapeDtypeStruct((rows, cols), F32)] * 4
        operands += [t, w, m, v]
    res = pl.pallas_call(
        body, name=name, grid=(ADAMW_STEPS,), in_specs=in_specs, out_specs=out_specs, out_shape=out_shape,
        compiler_params=pltpu.CompilerParams(dimension_semantics=("arbitrary",), vmem_limit_bytes=VMEM_LIMIT),
    )(*operands)
    return [res[4 * j:4 * j + 4] for j in range(n)]


SMALL_ROWS = ("loss", "pool_scale", "sgu_ln_g", "sgu_ln_b", "norm1_post_g", "norm2_pre_g", "norm2_post_g")


N_ROW_PARAMS = len(SMALL_ROWS) - 1


def _adamw_small(u_rows, u_others, params):
    n, n_oth = len(params), len(u_others)

    def body(*refs):
        urow_ref, others = refs[0], refs[1:1 + n_oth]
        wmv = refs[1 + n_oth:1 + n_oth + 3 * n]
        loss_ref = refs[1 + n_oth + 3 * n]
        outs = refs[2 + n_oth + 3 * n:]

        def total(ref, idx):
            g = ref[(0,) + idx].astype(F32)
            for q in range(1, N_CHIPS):
                g = g + ref[(q,) + idx].astype(F32)
            return g

        loss_ref[...] = total(urow_ref, (slice(0, 1), slice(None)))
        for p in range(n):
            if p < N_ROW_PARAMS:
                g = total(urow_ref, (slice(p + 1, p + 2), slice(None)))
            else:
                g = total(others[p - N_ROW_PARAMS], (slice(None), slice(None)))
            d, mn, vn = _adamw_math(wmv[3 * p][...], g, wmv[3 * p + 1][...], wmv[3 * p + 2][...])
            outs[4 * p][...] = g
            outs[4 * p + 1][...] = d
            outs[4 * p + 2][...] = mn
            outs[4 * p + 3][...] = vn

    flat = [a for p in params for a in p]
    out_shape = [jax.ShapeDtypeStruct((1, D_MODEL), F32)]
    for w, _, _ in params:
        out_shape += [jax.ShapeDtypeStruct(w.shape, F32)] * 4
    return pl.pallas_call(body, name="adamw_small", out_shape=out_shape)(u_rows, *u_others, *flat)


def _to_bf16(arrays):
    n = len(arrays)

    def body(*refs):
        for j in range(n):
            refs[n + j][...] = refs[j][...].astype(BF16)

    return pl.pallas_call(
        body, name="shards_to_bf16", out_shape=[jax.ShapeDtypeStruct(a.shape, BF16) for a in arrays],
    )(*arrays)


def kernel(x, norm1_pre_g, w_in, b_in, w_pool, pool_scale, sgu_ln_g, sgu_ln_b, w_spatial, b_spatial, w_sgu_proj, w_out, norm1_post_g, norm2_pre_g, w_ff1, w_ff2, norm2_post_g, loss_target, m_norm1_pre_g, m_w_in, m_b_in, m_w_pool, m_pool_scale, m_sgu_ln_g, m_sgu_ln_b, m_w_spatial, m_b_spatial, m_w_sgu_proj, m_w_out, m_norm1_post_g, m_norm2_pre_g, m_w_ff1, m_w_ff2, m_norm2_post_g, v_norm1_pre_g, v_w_in, v_b_in, v_w_pool, v_pool_scale, v_sgu_ln_g, v_sgu_ln_b, v_w_spatial, v_b_spatial, v_w_sgu_proj, v_w_out, v_norm1_post_g, v_norm2_pre_g, v_w_ff1, v_w_ff2, v_norm2_post_g):
    t_len = x.shape[1]
    row = lambda a: a.reshape(1, -1)
    x2 = x.reshape(t_len, D_MODEL)
    tgt2 = loss_target.reshape(t_len, D_MODEL)
    pg2 = lambda a: a.reshape(N_GROUPS * PG_SHARD, GROUP)

    shards_b = _to_bf16([w_in, w_pool, w_sgu_proj, w_out, w_ff1, w_ff2])

    def prework(token):
        return _prenorm(token, x2, row(norm1_pre_g))

    (win_f, wpool_f, wproj_f, g_out), (xn, xn_b) = _gather_behind("mix", shards_b[:4], None, prework,
                                                                  by_columns=(0,), by_rows=(1, 2))
    wout_f = g_out.reshape(D_MODEL, D_MODEL)
    bsp_t = b_spatial.T

    def forward(token):
        z, y, h1, *saved = _fwd_mix(token, xn, x2, win_f, row(b_in), wpool_f, row(pool_scale),
                                    row(sgu_ln_g), row(sgu_ln_b), w_spatial, bsp_t, wproj_f, wout_f,
                                    row(norm1_post_g))
        return h1, z, y, saved

    (g_ff1, g_ff2), (h1, z, y, saved) = _gather_behind("ff", shards_b[4:], win_f,
                                                       forward, by_columns=(0,))
    w2_f = g_ff2.reshape(D_FF, D_MODEL)
    hn_b, f_b, df1_b, df2_b, dh1, dg2post, dg2pre, loss_p = _mlp(h1, tgt2, row(norm2_pre_g), row(norm2_post_g),
                                                               g_ff1, w2_f)
    c_arr = lax.axis_index("c").astype(jnp.int32).reshape(1)
    p_ff2, p_ff1, got_ff2, got_ff1 = _wgrad_ff(c_arr, f_b, df2_b, hn_b, df1_b)
    chip_ff = _pair_sum_big("pair_sum_ff", c_arr, [p_ff1, p_ff2], [got_ff1, got_ff2])

    def backward_mix(token):
        return _bwd_mix(token, dh1, y, z, saved, wpool_f, row(pool_scale), row(sgu_ln_g), row(sgu_ln_b), w_spatial,
                        bsp_t, wproj_f, wout_f, row(norm1_post_g))

    summed_ff, (dz_b, mg_b, dy_b, p_pool, p_proj, dws, dbsp_t, dg1post, dps, dlng, dlnb,
                dbin) = _chip_exchange_behind("ff", chip_ff, [], backward_mix)

    rows = jnp.concatenate([jnp.broadcast_to(loss_p[:, 0:1], (1, D_MODEL)), dps, dlng, dlnb, dg1post, dg2pre, dg2post],
                           axis=0)
    smalls = [rows, dbin, dws.reshape(N_HEADS * SGU_BLOCK, SGU_BLOCK), dbsp_t.T]
    p_in, *got = _wgrad_in(xn_b, dz_b, [p_pool, p_proj], smalls)
    sums = _pair_sum_big("pair_sum_in", c_arr, [p_in, p_pool, p_proj], got[:3], smalls, got[3:])
    chip_bigs, chip_smalls = sums[:3], sums[3:]

    def backward_rest(token):
        dx, dg1pre = _bwd_in(token, dz_b, x2, dh1, row(norm1_pre_g), win_f)
        p_out, *got_out = _wgrad_out(mg_b, dy_b, [dg1pre])
        chip_out, chip_g1pre = _pair_sum_big("pair_sum_out", c_arr, [p_out], got_out[:1], [dg1pre], got_out[1:])
        upd_ff = _adamw_big("adamw_ff", p_out, summed_ff, [(w_ff1, m_w_ff1, v_w_ff1), (w_ff2, m_w_ff2, v_w_ff2)])
        return upd_ff[1][0], dx, chip_out, chip_g1pre, upd_ff

    summed_in, (_, dx, chip_out, chip_g1pre, upd_ff) = _chip_exchange_behind("in", chip_bigs, chip_smalls,
                                                                             backward_rest)

    def update_in(token):
        res = _adamw_big("adamw_in", token, summed_in[:1], [(w_in, m_w_in, v_w_in)])[0]
        return res[0], res

    summed_out, (_, upd_in) = _chip_exchange_behind("out", [chip_out], [chip_g1pre], update_in)
    big = {"in": upd_in, "ff1": upd_ff[0], "ff2": upd_ff[1]}
    ws2 = lambda a: a.reshape(N_HEADS * SGU_BLOCK, SGU_BLOCK)
    small_params = [(row(pool_scale), row(m_pool_scale), row(v_pool_scale)),
                    (row(sgu_ln_g), row(m_sgu_ln_g), row(v_sgu_ln_g)),
                    (row(sgu_ln_b), row(m_sgu_ln_b), row(v_sgu_ln_b)),
                    (row(norm1_post_g), row(m_norm1_post_g), row(v_norm1_post_g)),
                    (row(norm2_pre_g), row(m_norm2_pre_g), row(v_norm2_pre_g)),
                    (row(norm2_post_g), row(m_norm2_post_g), row(v_norm2_post_g)),
                    (row(norm1_pre_g), row(m_norm1_pre_g), row(v_norm1_pre_g)),
                    (row(b_in), row(m_b_in), row(v_b_in)),
                    (ws2(w_spatial), ws2(m_w_spatial), ws2(v_w_spatial)),
                    (b_spatial, m_b_spatial, v_b_spatial),
                    (pg2(w_pool), pg2(m_w_pool), pg2(v_w_pool)),
                    (pg2(w_sgu_proj), pg2(m_w_sgu_proj), pg2(v_w_sgu_proj)),
                    (w_out, m_w_out, v_w_out)]
    small_out = _adamw_small(summed_in[3], [summed_out[1]] + list(summed_in[4:])
                             + [summed_in[1], summed_in[2], summed_out[0]], small_params)
    loss = small_out[0][0, 0]
    small_names = SMALL_ROWS[1:] + ("norm1_pre_g", "b_in", "w_spatial", "b_spatial", "w_pool", "w_sgu_proj", "w_out")
    small = {nm: small_out[1 + 4 * p:5 + 4 * p] for p, nm in enumerate(small_names)}

    shapes = {"norm1_pre_g": norm1_pre_g.shape, "w_in": w_in.shape, "b_in": b_in.shape, "w_pool": w_pool.shape,
              "pool_scale": pool_scale.shape, "sgu_ln_g": sgu_ln_g.shape, "sgu_ln_b": sgu_ln_b.shape,
              "w_spatial": w_spatial.shape, "b_spatial": b_spatial.shape, "w_sgu_proj": w_sgu_proj.shape,
              "w_out": w_out.shape, "norm1_post_g": norm1_post_g.shape, "norm2_pre_g": norm2_pre_g.shape,
              "w_ff1": w_ff1.shape, "w_ff2": w_ff2.shape, "norm2_post_g": norm2_post_g.shape}
    source = {"w_in": big["in"], "w_ff1": big["ff1"], "w_ff2": big["ff2"], **small}
    order = list(shapes)
    outs = [loss, dx.reshape(x.shape)]
    for kind in range(4):
        outs += [source[nm][kind].reshape(shapes[nm]) for nm in order]
    return tuple(outs)
```

```python
import math

import jax
import jax.numpy as jnp
from jax import lax
from jax.experimental import pallas as pl
from jax.experimental.pallas import tpu as pltpu

F32, BF16 = jnp.float32, jnp.bfloat16
MESH = pl.DeviceIdType.MESH

D_MODEL = 1024
D_IN = 5120
D_FF = 4096
N_DEV = 8
N_CHIPS = 4
WINDOWS = (2, 4, 8, 16)
N_GROUPS = 4
GROUP = 256
HALO = 16
SGU_BLOCK = 128
N_HEADS = 4
HEAD = 256
CHUNK = 64
EPS = 1e-6
IN_SHARD = D_IN // N_DEV
FF_SHARD = D_FF // N_DEV
OUT_SHARD = D_MODEL // N_DEV
PG_SHARD = GROUP // N_DEV

ADAM_LR, ADAM_B1, ADAM_B2, ADAM_EPS, ADAM_WD, ADAM_STEP = 0.001, 0.9, 0.999, 1e-08, 0.01, 10

VMEM_LIMIT = 56 * 1024 * 1024
TM = 256
TM_BWD = 256
TM_IN = 512
PROJ_COLS = 512
GELU_C0 = math.sqrt(2.0 / math.pi)
GELU_C1 = 0.044715


def _dot(a, b):
    return jnp.dot(a, b, preferred_element_type=F32)


def _dot_nt(a, b):
    return lax.dot_general(a, b, (((1,), (1,)), ((), ())), preferred_element_type=F32)


def _dot_tn(a, b):
    return lax.dot_general(a, b, (((0,), (0,)), ((), ())), preferred_element_type=F32)


def _gelu(x):
    t = jnp.tanh(GELU_C0 * (x + GELU_C1 * (x * x * x)))
    return 0.5 * x * (1.0 + t), t


def _gelu_grad(x, t):
    return 0.5 * (1.0 + t) + 0.5 * x * (1.0 - t * t) * (GELU_C0 * (1.0 + 3.0 * GELU_C1 * (x * x)))


def _sigmoid(x):
    return 1.0 / (1.0 + jnp.exp(-x))


def _mean(x):
    return jnp.mean(x, axis=-1, keepdims=True)


def _colsum(x):
    return jnp.sum(x, axis=0, keepdims=True)


def _const_spec(shape):
    nd = len(shape)
    return pl.BlockSpec(shape, lambda *_: (0,) * nd, pipeline_mode=pl.Buffered(1))


def _acc_spec(shape):
    nd = len(shape)
    return pl.BlockSpec(shape, lambda *_: (0,) * nd)


def _masked_ws(ws_ref):
    ri = lax.broadcasted_iota(jnp.int32, (SGU_BLOCK, SGU_BLOCK), 0) // CHUNK
    ci = lax.broadcasted_iota(jnp.int32, (SGU_BLOCK, SGU_BLOCK), 1) // CHUNK
    return [jnp.where(ri >= ci, ws_ref[h], 0.0).astype(BF16) for h in range(N_HEADS)]


def _pool_fwd(pbuf, tile_idx, tm):
    pos = lax.broadcasted_iota(jnp.int32, (tm, 1), 0) + tile_idx * tm + 1
    pooled = []
    for g, w in enumerate(WINDOWS):
        e = pbuf[:, g * GROUP:(g + 1) * GROUP]
        s, sh = e, 1
        while sh < w:
            s = s + pltpu.roll(s, sh, 0)
            sh *= 2
        inv = 1.0 / jnp.minimum(pos, w).astype(F32)
        pooled.append(s[HALO:] * inv - e[HALO:])
    return pooled


def _spatial_head(ws_h, vb, bsp_ref, h, nblk):
    return jnp.concatenate(
        [_dot(ws_h, vb[n * SGU_BLOCK:(n + 1) * SGU_BLOCK, h * HEAD:(h + 1) * HEAD]) + bsp_ref[:, h:h + 1]
         for n in range(nblk)], axis=0)


def _prenorm(after, x, g1pre):
    t_len = x.shape[0]
    tm = TM_IN

    def body(after_ref, x_ref, g_ref, xn_ref, xnt_ref):
        del after_ref
        xv = x_ref[...]
        xnb = (xv * lax.rsqrt(_mean(xv * xv) + EPS) * g_ref[...]).astype(BF16)
        xn_ref[...] = xnb
        xnt_ref[...] = xnb.T

    return pl.pallas_call(
        body, name="prenorm", grid=(t_len // tm,),
        in_specs=[_ANY, pl.BlockSpec((tm, D_MODEL), lambda i: (i, 0)), _const_spec((1, D_MODEL))],
        out_specs=[pl.BlockSpec((tm, D_MODEL), lambda i: (i, 0)), pl.BlockSpec((D_MODEL, tm), lambda i: (0, i))],
        out_shape=[jax.ShapeDtypeStruct((t_len, D_MODEL), BF16), jax.ShapeDtypeStruct((D_MODEL, t_len), BF16)],
        compiler_params=pltpu.CompilerParams(dimension_semantics=("arbitrary",)),
    )(after, x, g1pre)


def _fwd_mix(after, xn, x, win_g, b_in, wpool, pool_scale, lng, lnb, ws, bsp_t, wproj, wout, g1post):
    t_len = x.shape[0]
    tm = TM
    nt = t_len // tm

    def body(after_ref, xn_ref, xb_ref, win_ref, bin_ref, wpool_ref, ps_ref, lng_ref, lnb_ref, ws_ref, bsp_ref,
             wproj_ref, wout_ref, g1post_ref, z_ref, y_ref, h1_ref, u_ref, gpu_ref, xhat_ref, gpv_ref, sa_ref,
             sb_ref, zcur, znext, pbuf):
        del after_ref
        s = pl.program_id(0)

        @pl.when(s == 0)
        def _():
            znext[...] = jnp.zeros((tm, D_IN), F32)
            pbuf[...] = jnp.zeros((tm + HALO, D_MODEL), F32)

        zcur[...] = znext[...]
        xnb = xn_ref[...]

        def project(p):
            cols = slice(p * PROJ_COLS, (p + 1) * PROJ_COLS)
            zp = _dot(xnb, win_ref[:, cols]) + bin_ref[:, cols]
            if (p + 1) * PROJ_COLS <= D_MODEL:
                z_ref[:, cols] = zp
            znext[:, cols] = zp

        project(0)
        pbuf[0:HALO, :] = jnp.where(s <= 1, 0.0, pbuf[0:HALO, :])
        pbuf[HALO:, :] = zcur[:, 0:D_MODEL]
        pooled = _pool_fwd(pbuf, jnp.maximum(s - 1, 0), tm)
        pbuf[0:HALO, :] = pbuf[tm:tm + HALO, :]
        a = jnp.concatenate([_dot(pooled[g].astype(BF16), wpool_ref[g]) for g in range(N_GROUPS)], axis=1)
        a = a * ps_ref[...]
        project(1)
        zu = zcur[:, D_MODEL:2 * D_MODEL]
        u, tu = _gelu(zu)
        u_ref[...] = u.astype(BF16)
        gpu_ref[...] = _gelu_grad(zu, tu).astype(BF16)
        project(2)
        zv = zcur[:, 2 * D_MODEL:3 * D_MODEL]
        gv, tv = _gelu(zv)
        xc = gv - _mean(gv)
        rln = lax.rsqrt(_mean(xc * xc) + EPS)
        xhat = xc * rln
        xhat_ref[...] = xhat.astype(BF16)
        gpv_ref[...] = (_gelu_grad(zv, tv) * rln).astype(BF16)
        vb = (xhat * lng_ref[...] + lnb_ref[...]).astype(BF16)
        wsm = _masked_ws(ws_ref)
        bbr = []
        for h in range(N_HEADS):
            project(3 + h)
            sv = _spatial_head(wsm[h], vb, bsp_ref, h, tm // SGU_BLOCK)
            bbr.append(_dot((u[:, h * HEAD:(h + 1) * HEAD] * sv).astype(BF16), wproj_ref[h]))
        bbr = jnp.concatenate(bbr, axis=1)
        project(7)
        sa = _sigmoid(zcur[:, 3 * D_MODEL:4 * D_MODEL])
        sb = _sigmoid(zcur[:, 4 * D_MODEL:5 * D_MODEL])
        sa_ref[...] = sa.astype(BF16)
        sb_ref[...] = sb.astype(BF16)
        project(8)
        yv = _dot((sa * a + sb * bbr).astype(BF16), wout_ref[...])
        y_ref[...] = yv
        project(9)
        ry = lax.rsqrt(_mean(yv * yv) + EPS)
        h1_ref[...] = xb_ref[...] + yv * ry * g1post_ref[...]

    proj = lambda w: pl.BlockSpec((tm, w), lambda s: (jnp.minimum(s, nt - 1), 0))
    mix = lambda w: pl.BlockSpec((tm, w), lambda s: (jnp.maximum(s - 1, 0), 0))
    return pl.pallas_call(
        body, name="fwd_mix", grid=(nt + 1,),
        in_specs=[_ANY, proj(D_MODEL), mix(D_MODEL),
                  _const_spec((D_MODEL, D_IN)),
                  _const_spec((1, D_IN)), _const_spec((N_GROUPS, GROUP, GROUP)), _const_spec((1, D_MODEL)),
                  _const_spec((1, D_MODEL)), _const_spec((1, D_MODEL)),
                  _const_spec((N_HEADS, SGU_BLOCK, SGU_BLOCK)), _const_spec((SGU_BLOCK, N_HEADS)),
                  _const_spec((N_HEADS, HEAD, HEAD)), _const_spec((D_MODEL, D_MODEL)), _const_spec((1, D_MODEL))],
        out_specs=[proj(D_MODEL), mix(D_MODEL), mix(D_MODEL)] + [mix(D_MODEL)] * 6,
        out_shape=[jax.ShapeDtypeStruct((t_len, D_MODEL), F32),
                   jax.ShapeDtypeStruct((t_len, D_MODEL), F32), jax.ShapeDtypeStruct((t_len, D_MODEL), F32)]
        + [jax.ShapeDtypeStruct((t_len, D_MODEL), BF16)] * 6,
        scratch_shapes=[pltpu.VMEM((tm, D_IN), F32), pltpu.VMEM((tm, D_IN), F32),
                        pltpu.VMEM((tm + HALO, D_MODEL), F32)],
        compiler_params=pltpu.CompilerParams(dimension_semantics=("arbitrary",), vmem_limit_bytes=VMEM_LIMIT),
    )(after, xn, x, win_g, b_in, wpool, pool_scale, lng, lnb, ws, bsp_t, wproj, wout, g1post)


def _mlp(h1, target, g2pre, g2post, w1_g, w2):
    t_len = h1.shape[0]
    tm = TM
    nt = t_len // tm

    def body(h1_ref, tgt_ref, g2pre_ref, g2post_ref, w1_ref, w2_ref,
             hn_ref, f_ref, df1_ref, df2_ref, dh1_ref, dg2post_ref, dg2pre_ref, loss_ref, f1_scr):
        i = pl.program_id(0)

        @pl.when(i == 0)
        def _():
            dg2post_ref[...] = jnp.zeros_like(dg2post_ref)
            dg2pre_ref[...] = jnp.zeros_like(dg2pre_ref)
            loss_ref[...] = jnp.zeros_like(loss_ref)

        h = h1_ref[...]
        r2 = lax.rsqrt(_mean(h * h) + EPS)
        nh = h * r2
        hnb = (nh * g2pre_ref[...]).astype(BF16)
        hn_ref[...] = hnb.T
        for k in range(N_DEV):
            f1_scr[:, k * FF_SHARD:(k + 1) * FF_SHARD] = _dot(hnb, w1_ref[:, k * FF_SHARD:(k + 1) * FF_SHARD])
        r = jnp.maximum(f1_scr[...], 0.0)
        fb = (r * r).astype(BF16)
        f_ref[...] = fb.T
        f2 = _dot(fb, w2_ref[...])
        rf = lax.rsqrt(_mean(f2 * f2) + EPS)
        nf = f2 * rf
        diff = h + nf * g2post_ref[...] - tgt_ref[...]
        loss_ref[...] += (0.5 / D_MODEL) * jnp.sum(diff * diff)
        dout = diff * (1.0 / D_MODEL)
        dg2post_ref[...] += _colsum(dout * nf)
        dn = dout * g2post_ref[...]
        df2b = (rf * (dn - nf * _mean(dn * nf))).astype(BF16)
        df2_ref[...] = df2b
        df = _dot_nt(df2b, w2_ref[...])
        df1b = (df * (2.0 * jnp.maximum(f1_scr[...], 0.0))).astype(BF16)
        df1_ref[...] = df1b
        dhn = _dot_nt(df1b, w1_ref[...])
        dg2pre_ref[...] += _colsum(dhn * nh)
        dnh = dhn * g2pre_ref[...]
        dh1_ref[...] = dout + r2 * (dnh - nh * _mean(dnh * nh))

    tok = lambda w: pl.BlockSpec((tm, w), lambda i: (i, 0))
    return pl.pallas_call(
        body, name="mlp_fwd_bwd", grid=(nt,),
        in_specs=[tok(D_MODEL), tok(D_MODEL), _const_spec((1, D_MODEL)), _const_spec((1, D_MODEL)),
                  _const_spec((D_MODEL, D_FF)), _const_spec((D_FF, D_MODEL))],
        out_specs=[pl.BlockSpec((D_MODEL, tm), lambda i: (0, i)), pl.BlockSpec((D_FF, tm), lambda i: (0, i)),
                   tok(D_FF), tok(D_MODEL), tok(D_MODEL),
                   _acc_spec((1, D_MODEL)), _acc_spec((1, D_MODEL)), _acc_spec((1, 128))],
        out_shape=[jax.ShapeDtypeStruct((D_MODEL, t_len), BF16), jax.ShapeDtypeStruct((D_FF, t_len), BF16),
                   jax.ShapeDtypeStruct((t_len, D_FF), BF16), jax.ShapeDtypeStruct((t_len, D_MODEL), BF16),
                   jax.ShapeDtypeStruct((t_len, D_MODEL), F32), jax.ShapeDtypeStruct((1, D_MODEL), F32),
                   jax.ShapeDtypeStruct((1, D_MODEL), F32), jax.ShapeDtypeStruct((1, 128), F32)],
        scratch_shapes=[pltpu.VMEM((tm, D_FF), F32)],
        compiler_params=pltpu.CompilerParams(dimension_semantics=("arbitrary",), vmem_limit_bytes=VMEM_LIMIT),
    )(h1, target, g2pre, g2post, w1_g, w2)


def _bwd_mix(after, dh1, y, z, saved, wpool, pool_scale, lng, lnb, ws, bsp_t, wproj, wout, g1post):
    t_len = y.shape[0]
    tm = TM_BWD
    nt = t_len // tm
    nblk = tm // SGU_BLOCK

    def body(after_ref, dh1_ref, y_ref, z_ref, zh_ref, u_ref, gpu_ref, xhat_ref, gpv_ref, sa_ref, sb_ref,
             wpool_ref, ps_ref, lng_ref, lnb_ref, ws_ref,
             bsp_ref, wproj_ref, wout_ref, g1post_ref,
             dz_ref, mg_ref, dy_ref, ppool_ref, pproj_ref, dws_ref, dbsp_ref, dg1post_ref, dps_ref,
             dlng_ref, dlnb_ref, dbin_ref, pbuf, qbuf, dwpool_ref, dwproj_ref):
        del after_ref
        i = pl.program_id(0)
        ti = nt - 1 - i

        @pl.when(i == 0)
        def _():
            for ref in (dwpool_ref, dwproj_ref, dws_ref, dbsp_ref, dg1post_ref, dps_ref, dlng_ref, dlnb_ref,
                        dbin_ref):
                ref[...] = jnp.zeros_like(ref)
            qbuf[tm:tm + HALO, :] = jnp.zeros((HALO, D_MODEL), F32)

        pbuf[0:HALO, :] = jnp.where(ti > 0, zh_ref[...], 0.0)
        pbuf[HALO:, :] = z_ref[...]
        pooled = _pool_fwd(pbuf, ti, tm)
        pooled_b = [p.astype(BF16) for p in pooled]
        a_raw = jnp.concatenate([_dot(pooled_b[g], wpool_ref[g]) for g in range(N_GROUPS)], axis=1)
        wsm = _masked_ws(ws_ref)
        u = u_ref[...].astype(F32)
        xhat = xhat_ref[...].astype(F32)
        vb = (xhat * lng_ref[...] + lnb_ref[...]).astype(BF16)
        sv_heads = [_spatial_head(wsm[h], vb, bsp_ref, h, nblk) for h in range(N_HEADS)]
        gated_b = [(u[:, h * HEAD:(h + 1) * HEAD] * sv_heads[h]).astype(BF16) for h in range(N_HEADS)]
        bbr = jnp.concatenate([_dot(gated_b[h], wproj_ref[h]) for h in range(N_HEADS)], axis=1)
        sa = sa_ref[...].astype(F32)
        sb = sb_ref[...].astype(F32)
        a = a_raw * ps_ref[...]
        mg_ref[...] = (sa * a + sb * bbr).astype(BF16).T

        dh = dh1_ref[...]
        yv = y_ref[...]
        ry = lax.rsqrt(_mean(yv * yv) + EPS)
        ny = yv * ry
        dg1post_ref[...] += _colsum(dh * ny)
        dn = dh * g1post_ref[...]
        dyb = (ry * (dn - ny * _mean(dn * ny))).astype(BF16)
        dy_ref[...] = dyb
        dmg = _dot_nt(dyb, wout_ref[...])

        da = dmg * sa
        dbbr = dmg * sb
        dzga = dmg * a * sa * (1.0 - sa)
        dzgb = dmg * bbr * sb * (1.0 - sb)
        dz_ref[:, 3 * D_MODEL:4 * D_MODEL] = dzga.astype(BF16)
        dz_ref[:, 4 * D_MODEL:5 * D_MODEL] = dzgb.astype(BF16)
        dbin_ref[:, 3 * D_MODEL:4 * D_MODEL] += _colsum(dzga)
        dbin_ref[:, 4 * D_MODEL:5 * D_MODEL] += _colsum(dzgb)

        dps_ref[...] += _colsum(da * a_raw)
        da_raw_b = (da * ps_ref[...]).astype(BF16)
        pos = lax.broadcasted_iota(jnp.int32, (tm, 1), 0) + ti * tm + 1
        dpooled = []
        for g, w in enumerate(WINDOWS):
            cols = slice(g * GROUP, (g + 1) * GROUP)
            dwpool_ref[g] += _dot_tn(pooled_b[g], da_raw_b[:, cols])
            dp = _dot_nt(da_raw_b[:, cols], wpool_ref[g])
            dpooled.append(dp)
            qbuf[0:tm, cols] = dp * (1.0 / jnp.minimum(pos, w).astype(F32))
        n_ext = tm + HALO
        dzp = []
        for g, w in enumerate(WINDOWS):
            e = qbuf[:, g * GROUP:(g + 1) * GROUP]
            s, sh = e, 1
            while sh < w:
                s = s + pltpu.roll(s, n_ext - sh, 0)
                sh *= 2
            dzp.append(s[0:tm] - dpooled[g])
        qbuf[tm:tm + HALO, :] = qbuf[0:HALO, :]
        dzp = jnp.concatenate(dzp, axis=1)
        dz_ref[:, 0:D_MODEL] = dzp.astype(BF16)
        dbin_ref[:, 0:D_MODEL] += _colsum(dzp)

        dv_heads = []
        du_heads = []
        for h in range(N_HEADS):
            cols = slice(h * HEAD, (h + 1) * HEAD)
            dbbr_b = dbbr[:, cols].astype(BF16)
            dwproj_ref[h] += _dot_tn(gated_b[h], dbbr_b)
            dgated = _dot_nt(dbbr_b, wproj_ref[h])
            du_heads.append(dgated * sv_heads[h])
            dsv = dgated * u[:, cols]
            dsv_b = dsv.astype(BF16)
            rows = []
            for n in range(nblk):
                blk = slice(n * SGU_BLOCK, (n + 1) * SGU_BLOCK)
                rows.append(_dot_tn(wsm[h], dsv_b[blk]))
                dws_ref[h] += _dot_nt(dsv_b[blk], vb[blk, cols])
                dbsp_ref[:, h:h + 1] += jnp.sum(dsv[blk], axis=1, keepdims=True)
            dv_heads.append(jnp.concatenate(rows, axis=0))
        dzu = jnp.concatenate(du_heads, axis=1) * gpu_ref[...].astype(F32)
        dz_ref[:, D_MODEL:2 * D_MODEL] = dzu.astype(BF16)
        dbin_ref[:, D_MODEL:2 * D_MODEL] += _colsum(dzu)
        dv = jnp.concatenate(dv_heads, axis=1)
        dlng_ref[...] += _colsum(dv * xhat)
        dlnb_ref[...] += _colsum(dv)
        dxh = dv * lng_ref[...]
        dzv = (dxh - _mean(dxh) - xhat * _mean(dxh * xhat)) * gpv_ref[...].astype(F32)
        dz_ref[:, 2 * D_MODEL:3 * D_MODEL] = dzv.astype(BF16)
        dbin_ref[:, 2 * D_MODEL:3 * D_MODEL] += _colsum(dzv)

        @pl.when(i == nt - 1)
        def _():
            ri = lax.broadcasted_iota(jnp.int32, (SGU_BLOCK, SGU_BLOCK), 0) // CHUNK
            ci = lax.broadcasted_iota(jnp.int32, (SGU_BLOCK, SGU_BLOCK), 1) // CHUNK
            for h in range(N_HEADS):
                dws_ref[h] = jnp.where(ri >= ci, dws_ref[h], 0.0)
            for slot in range(N_DEV):
                own = slice(_owner_of_slot(slot) * PG_SHARD, (_owner_of_slot(slot) + 1) * PG_SHARD)
                for g in range(N_GROUPS):
                    rows = slice(g * PG_SHARD, (g + 1) * PG_SHARD)
                    ppool_ref[slot, rows, :] = dwpool_ref[g, own, :].astype(BF16)
                    pproj_ref[slot, rows, :] = dwproj_ref[g, own, :].astype(BF16)

    tok = lambda w: pl.BlockSpec((tm, w), lambda i: (nt - 1 - i, 0))
    halo = pl.BlockSpec((HALO, D_MODEL), lambda i: (jnp.maximum((nt - 1 - i) * (tm // HALO) - 1, 0), 0))
    return pl.pallas_call(
        body, name="bwd_mix", grid=(nt,),
        in_specs=[_ANY, tok(D_MODEL), tok(D_MODEL), tok(D_MODEL), halo] + [tok(D_MODEL)] * 6
        + [_const_spec((N_GROUPS, GROUP, GROUP)),
                  _const_spec((1, D_MODEL)), _const_spec((1, D_MODEL)), _const_spec((1, D_MODEL)),
                  _const_spec((N_HEADS, SGU_BLOCK, SGU_BLOCK)), _const_spec((SGU_BLOCK, N_HEADS)),
                  _const_spec((N_HEADS, HEAD, HEAD)), _const_spec((D_MODEL, D_MODEL)), _const_spec((1, D_MODEL))],
        out_specs=[tok(D_IN), pl.BlockSpec((D_MODEL, tm), lambda i: (0, nt - 1 - i)), tok(D_MODEL),
                   _acc_spec((N_DEV, N_GROUPS * PG_SHARD, GROUP)), _acc_spec((N_DEV, N_HEADS * PG_SHARD, HEAD)),
                   _acc_spec((N_HEADS, SGU_BLOCK, SGU_BLOCK)), _acc_spec((SGU_BLOCK, N_HEADS)),
                   _acc_spec((1, D_MODEL)), _acc_spec((1, D_MODEL)), _acc_spec((1, D_MODEL)), _acc_spec((1, D_MODEL)),
                   _acc_spec((1, D_IN))],
        out_shape=[jax.ShapeDtypeStruct((t_len, D_IN), BF16),
                   jax.ShapeDtypeStruct((D_MODEL, t_len), BF16), jax.ShapeDtypeStruct((t_len, D_MODEL), BF16),
                   jax.ShapeDtypeStruct((N_DEV, N_GROUPS * PG_SHARD, GROUP), BF16),
                   jax.ShapeDtypeStruct((N_DEV, N_HEADS * PG_SHARD, HEAD), BF16),
                   jax.ShapeDtypeStruct((N_HEADS, SGU_BLOCK, SGU_BLOCK), F32),
                   jax.ShapeDtypeStruct((SGU_BLOCK, N_HEADS), F32),
                   jax.ShapeDtypeStruct((1, D_MODEL), F32), jax.ShapeDtypeStruct((1, D_MODEL), F32),
                   jax.ShapeDtypeStruct((1, D_MODEL), F32), jax.ShapeDtypeStruct((1, D_MODEL), F32),
                   jax.ShapeDtypeStruct((1, D_IN), F32)],
        scratch_shapes=[pltpu.VMEM((tm + HALO, D_MODEL), F32), pltpu.VMEM((tm + HALO, D_MODEL), F32),
                        pltpu.VMEM((N_GROUPS, GROUP, GROUP), F32), pltpu.VMEM((N_HEADS, HEAD, HEAD), F32)],
        compiler_params=pltpu.CompilerParams(dimension_semantics=("arbitrary",), vmem_limit_bytes=VMEM_LIMIT),
    )(after, dh1, y, z, z, *saved, wpool, pool_scale, lng, lnb, ws, bsp_t, wproj, wout, g1post)


def _bwd_in(after, dz_b, x, dh1, g1pre, win_g):
    t_len = x.shape[0]
    tm = TM_IN
    nt = t_len // tm

    def body(after_ref, dz_ref, x_ref, dh1_ref, g1_ref, win_ref, dx_ref, dg1pre_ref):
        del after_ref

        @pl.when(pl.program_id(0) == 0)
        def _():
            dg1pre_ref[...] = jnp.zeros_like(dg1pre_ref)

        dxn = _dot_nt(dz_ref[...], win_ref[...])
        xv = x_ref[...]
        r1 = lax.rsqrt(_mean(xv * xv) + EPS)
        nx = xv * r1
        dg1pre_ref[...] += _colsum(dxn * nx)
        dnx = dxn * g1_ref[...]
        dx_ref[...] = r1 * (dnx - nx * _mean(dnx * nx)) + dh1_ref[...]

    tok = lambda w: pl.BlockSpec((tm, w), lambda i: (i, 0))
    return pl.pallas_call(
        body, name="bwd_in", grid=(nt,),
        in_specs=[_ANY, tok(D_IN), tok(D_MODEL), tok(D_MODEL), _const_spec((1, D_MODEL)),
                  _const_spec((D_MODEL, D_IN))],
        out_specs=[tok(D_MODEL), _acc_spec((1, D_MODEL))],
        out_shape=[jax.ShapeDtypeStruct((t_len, D_MODEL), F32), jax.ShapeDtypeStruct((1, D_MODEL), F32)],
        compiler_params=pltpu.CompilerParams(dimension_semantics=("arbitrary",), vmem_limit_bytes=VMEM_LIMIT),
    )(after, dz_b, x, dh1, g1pre, win_g)


def _owner_of_slot(s):
    return 4 * ((s // 2) % 2) + 2 * (s % 2) + s // 4


def _wgrad_ff(c_arr, f_t, df2_b, hn_t, df1_b):
    t_len = df2_b.shape[0]
    half_shard = FF_SHARD // 2

    def body(c_ref, f_ref, df2_ref, hn_ref, df1_ref, o2_ref, o1_ref, buf2, buf1, land2, land1, send_sems, recv_sems):
        del c_ref
        s = pl.program_id(0)
        k, h = (s // 2) % N_CHIPS, s % 2
        x, y, c = _coords()

        def copies(k):
            return [pltpu.make_async_remote_copy(src_ref=buf.at[k], dst_ref=land.at[k], send_sem=send_sems.at[j, k],
                                                 recv_sem=recv_sems.at[j, k], device_id=(x, y, 1 - c),
                                                 device_id_type=MESH)
                    for j, (buf, land) in enumerate(((buf2, land2), (buf1, land1)))]

        @pl.when(s < N_DEV)
        def _():
            buf2[k, h] = _dot(f_ref[...], df2_ref[...]).astype(BF16)
            buf1[k, h] = _dot(hn_ref[...], df1_ref[...]).astype(BF16)

            @pl.when(h == 1)
            def _():
                for cp in copies(k):
                    cp.start()

        @pl.when(s >= N_DEV)
        def _():
            @pl.when(h == 0)
            def _():
                for cp in copies(k):
                    cp.wait_recv()

            o2_ref[...] = (_dot(f_ref[...], df2_ref[...]) + land2[k, h].astype(F32)).astype(BF16)
            o1_ref[...] = (_dot(hn_ref[...], df1_ref[...]) + land1[k, h].astype(F32)).astype(BF16)

        @pl.when(s == 2 * N_DEV - 1)
        def _():
            for k_sent in range(N_CHIPS):
                for cp in copies(k_sent):
                    cp.wait_send()

    def part(s, c_ref):
        owner = 2 * ((s // 2) % N_CHIPS) + jnp.where(s < N_DEV, 1 - c_ref[0], c_ref[0])
        return 2 * owner + s % 2

    mine_slot = lambda s: jnp.maximum(s // 2 - N_CHIPS, 0)
    mine_half = lambda s: jnp.where(s >= N_DEV, s % 2, 0)
    return pl.pallas_call(
        body, name="wgrad_ff",
        grid_spec=pltpu.PrefetchScalarGridSpec(
            num_scalar_prefetch=1, grid=(2 * N_DEV,),
            in_specs=[pl.BlockSpec((half_shard, t_len), lambda s, c_ref: (part(s, c_ref), 0)),
                      pl.BlockSpec((t_len, D_MODEL), lambda s, c_ref: (0, 0), pipeline_mode=pl.Buffered(1)),
                      pl.BlockSpec((D_MODEL, t_len), lambda s, c_ref: (0, 0), pipeline_mode=pl.Buffered(1)),
                      pl.BlockSpec((t_len, half_shard), lambda s, c_ref: (0, part(s, c_ref)))],
            out_specs=[pl.BlockSpec((None, half_shard, D_MODEL), lambda s, c_ref: (mine_slot(s), mine_half(s), 0)),
                       pl.BlockSpec((None, D_MODEL, half_shard), lambda s, c_ref: (mine_slot(s), 0, mine_half(s)))],
            scratch_shapes=[pltpu.VMEM((N_CHIPS, 2, half_shard, D_MODEL), BF16),
                            pltpu.VMEM((N_CHIPS, 2, D_MODEL, half_shard), BF16)] * 2
            + [pltpu.SemaphoreType.DMA((2, N_CHIPS))] * 2),
        out_shape=[jax.ShapeDtypeStruct((N_CHIPS, FF_SHARD, D_MODEL), BF16),
                   jax.ShapeDtypeStruct((N_CHIPS, D_MODEL, FF_SHARD), BF16)],
        compiler_params=pltpu.CompilerParams(dimension_semantics=("arbitrary",), vmem_limit_bytes=VMEM_LIMIT),
    )(c_arr, f_t, df2_b, hn_t, df1_b)


def _wgrad_pair(name, a, b, in_specs, halves, slot_shape, bigs, smalls):
    nb, nx = len(bigs), len(bigs) + len(smalls)

    def body(a_ref, b_ref, *refs):
        extra_src, (mine_ref, got_ref), extra_dst = refs[:nx], refs[nx:nx + 2], refs[nx + 2:2 * nx + 2]
        both, keep_sems, send_sems, recv_sems, extra_send, extra_recv = refs[2 * nx + 2:]
        q = pl.program_id(0)
        x, y, c = _coords()
        to_sibling = dict(device_id=(x, y, 1 - c), device_id_type=MESH)

        def slot_copies(k):
            return [pltpu.make_async_copy(both.at[k, c], mine_ref.at[k], keep_sems.at[k]),
                    pltpu.make_async_remote_copy(src_ref=both.at[k, 1 - c], dst_ref=got_ref.at[k],
                                                 send_sem=send_sems.at[k], recv_sem=recv_sems.at[k], **to_sibling)]

        def extra_copies():
            return [pltpu.make_async_remote_copy(
                src_ref=extra_src[j].at[pl.ds(N_CHIPS * (1 - c), N_CHIPS)] if j < nb else extra_src[j],
                dst_ref=extra_dst[j], send_sem=extra_send.at[j], recv_sem=extra_recv.at[j], **to_sibling)
                for j in range(nx)]

        @pl.when(q == 0)
        def _():
            for cp in extra_copies():
                cp.start()

        half0, half1 = halves(_dot(a_ref[...], b_ref[...]))
        both[q, 0] = half0.astype(BF16)
        both[q, 1] = half1.astype(BF16)
        for cp in slot_copies(q):
            cp.start()

        @pl.when(q == N_CHIPS - 1)
        def _():
            for k in range(N_CHIPS):
                for cp in slot_copies(k):
                    cp.wait()
            for cp in extra_copies():
                cp.wait()

    slots = jax.ShapeDtypeStruct((N_CHIPS,) + slot_shape, BF16)
    return pl.pallas_call(
        body, name=name, grid=(N_CHIPS,),
        in_specs=list(in_specs) + [_ANY] * nx,
        out_specs=[_ANY] * (2 + nx),
        out_shape=[slots, slots] + [jax.ShapeDtypeStruct((N_CHIPS,) + t.shape[1:], t.dtype) for t in bigs]
        + [jax.ShapeDtypeStruct(t.shape, t.dtype) for t in smalls],
        scratch_shapes=[pltpu.VMEM((N_CHIPS, 2) + slot_shape, BF16)] + [pltpu.SemaphoreType.DMA((N_CHIPS,))] * 3
        + [pltpu.SemaphoreType.DMA((max(nx, 1),))] * 2,
        compiler_params=pltpu.CompilerParams(dimension_semantics=("arbitrary",), vmem_limit_bytes=VMEM_LIMIT),
    )(a, b, *bigs, *smalls)


def _wgrad_in(xn_t, dz_b, bigs, smalls):
    t_len = dz_b.shape[0]
    return _wgrad_pair("wgrad_in", xn_t, dz_b,
                       [pl.BlockSpec((D_MODEL, t_len), lambda q: (0, 0), pipeline_mode=pl.Buffered(1)),
                        pl.BlockSpec((t_len, 2 * IN_SHARD), lambda q: (0, q))],
                       lambda res: (res[:, 0:IN_SHARD], res[:, IN_SHARD:2 * IN_SHARD]), (D_MODEL, IN_SHARD),
                       bigs, smalls)


def _wgrad_out(mg_t, dy_b, smalls):
    t_len = dy_b.shape[0]
    return _wgrad_pair("wgrad_out", mg_t, dy_b,
                       [pl.BlockSpec((2 * OUT_SHARD, t_len), lambda q: (q, 0)),
                        pl.BlockSpec((t_len, D_MODEL), lambda q: (0, 0), pipeline_mode=pl.Buffered(1))],
                       lambda res: (res[0:OUT_SHARD], res[OUT_SHARD:2 * OUT_SHARD]), (OUT_SHARD, D_MODEL),
                       [], smalls)


def _coords():
    return lax.axis_index("x"), lax.axis_index("y"), lax.axis_index("c")


_ANY = pl.BlockSpec(memory_space=pl.ANY)


_HBM = pl.BlockSpec(memory_space=pltpu.HBM)
_SEM = pl.BlockSpec(memory_space=pltpu.SEMAPHORE)
_VMEM = pl.BlockSpec(memory_space=pltpu.VMEM)
_EFFECT = pltpu.SideEffectType.DATAFLOW_SIDE_EFFECTING
_TOKEN = jax.ShapeDtypeStruct((8, 128), F32)


def _in_hbm(a):
    return pltpu.with_memory_space_constraint(a, pltpu.HBM)


def _split_call(name, body, n_sems_out, arrays, sems_in=(), after=None):
    na, ns = len(arrays), len(sems_in)
    has_after = after is not None

    def kernel_body(*refs):
        arr = refs[:na]
        s_in = refs[na:na + ns]
        outs = refs[na + ns + has_after:]
        body(arr, s_in, outs[:n_sems_out])
        outs[-1][...] = jnp.zeros((8, 128), F32)

    out_shape = ([pltpu.SemaphoreType.DMA(())] * n_sems_out + [pltpu.HBM(a.shape, a.dtype) for a in arrays] + [_TOKEN])
    res = pl.pallas_call(
        kernel_body, name=name, out_shape=out_shape,
        in_specs=[_HBM] * na + [_SEM] * ns + [_ANY] * has_after,
        out_specs=[_SEM] * n_sems_out + [_HBM] * na + [_VMEM],
        input_output_aliases={i: n_sems_out + i for i in range(na)},
        compiler_params=pltpu.CompilerParams(has_side_effects=_EFFECT),
    )(*[_in_hbm(a) for a in arrays], *sems_in, *([after] if has_after else []))
    return list(res[:n_sems_out]), list(res[n_sems_out:n_sems_out + na]), res[-1]


def _wait_bytes_of(ref, send_sem, recv_sem, peer, send=True, recv=True):
    cp = pltpu.make_async_remote_copy(src_ref=ref, dst_ref=ref, send_sem=send_sem, recv_sem=recv_sem,
                                      device_id=peer, device_id_type=MESH)
    if send:
        cp.wait_send()
    if recv:
        cp.wait_recv()


def _gather_behind(tag, shards, after, work, by_columns=(), by_rows=()):
    n = len(shards)

    def land_shape(j, s):
        if j in by_columns:
            return (s.shape[0], N_DEV * s.shape[1])
        if j in by_rows:
            return (s.shape[0], N_DEV * s.shape[1], s.shape[2])
        return (N_DEV,) + s.shape

    lands = [lax.empty(land_shape(j, s), s.dtype) for j, s in enumerate(shards)]

    def slots(arr, j, first, count=1):
        if j in by_columns:
            cols = shards[j].shape[1]
            return arr[n + j].at[:, pl.ds(pl.multiple_of(first * cols, 128), count * cols)]
        if j in by_rows:
            rows = shards[j].shape[1]
            return arr[n + j].at[:, pl.ds(pl.multiple_of(first * rows, 16), count * rows), :]
        return arr[n + j].at[first] if count == 1 else arr[n + j].at[pl.ds(first, count)]

    def own_slot(arr, j, sem):
        x, y, c = _coords()
        return pltpu.make_async_copy(arr[j], slots(arr, j, 4 * x + 2 * y + c), sem)

    def start(arr, _, sems):
        x, y, c = _coords()
        me = 4 * x + 2 * y + c
        for j in range(n):
            for chip in [(1 - x, y), (x, 1 - y), (1 - x, 1 - y)]:
                pltpu.make_async_remote_copy(src_ref=arr[j], dst_ref=slots(arr, j, me), send_sem=sems[j],
                                             recv_sem=sems[n + j], device_id=(*chip, c), device_id_type=MESH).start()
        for j in range(n):
            pltpu.make_async_remote_copy(src_ref=arr[j], dst_ref=slots(arr, j, me), send_sem=sems[2 * n + j],
                                         recv_sem=sems[3 * n + j], device_id=(x, y, 1 - c),
                                         device_id_type=MESH).start()
            own_slot(arr, j, sems[4 * n + j]).start()

    def middle(arr, s_in, sems):
        x, y, c = _coords()
        sibling = (x, y, 1 - c)
        for j in range(n):
            _wait_bytes_of(slots(arr, j, 0, 3), s_in[j], s_in[n + j], sibling)
            for chip in [(1 - x, y), (x, 1 - y), (1 - x, 1 - y)]:
                slot = slots(arr, j, 4 * chip[0] + 2 * chip[1] + c)
                pltpu.make_async_remote_copy(src_ref=slot, dst_ref=slot, send_sem=sems[j], recv_sem=sems[n + j],
                                             device_id=sibling, device_id_type=MESH).start()

    def finish(arr, s_in, _):
        x, y, c = _coords()
        sibling = (x, y, 1 - c)
        for j in range(n):
            _wait_bytes_of(slots(arr, j, 0, 1), s_in[j], s_in[n + j], sibling)
            own_slot(arr, j, s_in[2 * n + j]).wait()
            _wait_bytes_of(slots(arr, j, 0, 3), s_in[3 * n + j], s_in[4 * n + j], sibling)

    sems, arrays, token = _split_call("gather_%s_start" % tag, start, 5 * n, list(shards) + lands, after=after)
    result = work(token)
    fwd_sems, arrays, token = _split_call("gather_%s_middle" % tag, middle, 2 * n, arrays, sems_in=sems[:2 * n],
                                          after=result[0])
    _, arrays, _ = _split_call("gather_%s_finish" % tag, finish, 0, arrays, sems_in=sems[2 * n:] + fwd_sems,
                               after=token)
    return arrays[n:], result


def _chip_exchange_behind(tag, bigs, smalls, work):
    nb, n = len(bigs), len(bigs) + len(smalls)
    lands = [lax.empty(s.shape, s.dtype) for s in bigs] + [lax.empty((N_CHIPS,) + s.shape, s.dtype) for s in smalls]

    def own_slot(arr, j, sem):
        x, y, _ = _coords()
        q_me = 2 * x + y
        return pltpu.make_async_copy(arr[j].at[q_me] if j < nb else arr[j], arr[n + j].at[q_me], sem)

    def start(arr, _, sems):
        x, y, c = _coords()
        q_me = 2 * x + y
        for j in range(n):
            for peer in [(1 - x, y, c), (x, 1 - y, c), (1 - x, 1 - y, c)]:
                piece = arr[j].at[2 * peer[0] + peer[1]] if j < nb else arr[j]
                pltpu.make_async_remote_copy(src_ref=piece, dst_ref=arr[n + j].at[q_me],
                                             send_sem=sems[j], recv_sem=sems[n + j], device_id=peer,
                                             device_id_type=MESH).start()
            own_slot(arr, j, sems[2 * n + j]).start()

    def finish(arr, s_in, _):
        x, y, c = _coords()
        for j in range(n):
            _wait_bytes_of(arr[n + j].at[pl.ds(0, 3)], s_in[j], s_in[n + j], (x, y, 1 - c))
            own_slot(arr, j, s_in[2 * n + j]).wait()

    sems, arrays, token = _split_call("chip_exchange_%s_start" % tag, start, 3 * n, list(bigs) + list(smalls) + lands)
    result = work(token)
    _, arrays, _ = _split_call("chip_exchange_%s_finish" % tag, finish, 0, arrays, sems_in=sems, after=result[0])
    return arrays[n:], result


def _pair_sum_big(name, c_arr, mine, theirs, whole_mine=(), whole_theirs=()):
    n, nw = len(mine), len(whole_mine)

    def body(c_ref, *refs):
        del c_ref
        ins, outs = refs[:2 * (n + nw)], refs[2 * (n + nw):]
        for j in range(n):
            outs[j][...] = (ins[j][...].astype(F32) + ins[n + j][...].astype(F32)).astype(BF16)

        @pl.when(pl.program_id(0) == 0)
        def _():
            for j in range(nw):
                outs[n + j][...] = ins[2 * n + j][...] + ins[2 * n + nw + j][...]

    slot = lambda t: pl.BlockSpec((None,) + t.shape[1:], lambda q, c_ref: (q, 0, 0))
    own = lambda t: slot(t) if t.shape[0] == N_CHIPS else pl.BlockSpec(
        (None,) + t.shape[1:], lambda q, c_ref: (4 * c_ref[0] + q, 0, 0))
    whole = lambda t: pl.BlockSpec(t.shape, lambda q, c_ref: (0,) * t.ndim)
    return pl.pallas_call(
        body, name=name,
        grid_spec=pltpu.PrefetchScalarGridSpec(
            num_scalar_prefetch=1, grid=(N_CHIPS,),
            in_specs=[own(t) for t in mine] + [slot(t) for t in theirs]
            + [whole(t) for t in whole_mine] + [whole(t) for t in whole_theirs],
            out_specs=[slot(t) for t in theirs] + [whole(t) for t in whole_mine]),
        out_shape=[jax.ShapeDtypeStruct(t.shape, BF16) for t in theirs]
        + [jax.ShapeDtypeStruct(t.shape, t.dtype) for t in whole_mine],
        compiler_params=pltpu.CompilerParams(dimension_semantics=("arbitrary",)),
    )(c_arr, *mine, *theirs, *whole_mine, *whole_theirs)


def _adamw_math(w, g, m, v):
    m = ADAM_B1 * m + (1.0 - ADAM_B1) * g
    v = ADAM_B2 * v + (1.0 - ADAM_B2) * (g * g)
    m_hat = m / (1.0 - ADAM_B1 ** ADAM_STEP)
    v_hat = v / (1.0 - ADAM_B2 ** ADAM_STEP)
    delta = -ADAM_LR * (m_hat / (jnp.sqrt(v_hat) + ADAM_EPS) + ADAM_WD * w)
    return delta, m, v


ADAMW_STEPS = 2


def _adamw_big(name, after, chip_sums, params):
    n = len(params)

    def body(after_ref, *refs):
        del after_ref
        outs = refs[4 * n:]
        for j in range(n):
            t_ref, w_ref, m_ref, v_ref = refs[4 * j:4 * j + 4]
            g = t_ref[0].astype(F32)
            for q in range(1, N_CHIPS):
                g = g + t_ref[q].astype(F32)
            d, mn, vn = _adamw_math(w_ref[...], g, m_ref[...], v_ref[...])
            for out, val in zip(outs[4 * j:4 * j + 4], (g, d, mn, vn)):
                out[...] = val

    in_specs, out_specs, out_shape, operands = [_ANY], [], [], [after]
    for t, (w, m, v) in zip(chip_sums, params):
        rows, cols = w.shape
        tr = rows // ADAMW_STEPS
        blk = pl.BlockSpec((tr, cols), lambda r: (r, 0))
        in_specs += [pl.BlockSpec((N_CHIPS, tr, cols), lambda r: (0, r, 0)), blk, blk, blk]
        out_specs += [blk] * 4
        out_shape += [jax.ShapeDtypeStruct((rows, cols), F32)] * 4
        operands += [t, w, m, v]
    res = pl.pallas_call(
        body, name=name, grid=(ADAMW_STEPS,), in_specs=in_specs, out_specs=out_specs, out_shape=out_shape,
        compiler_params=pltpu.CompilerParams(dimension_semantics=("arbitrary",), vmem_limit_bytes=VMEM_LIMIT),
    )(*operands)
    return [res[4 * j:4 * j + 4] for j in range(n)]


SMALL_ROWS = ("loss", "pool_scale", "sgu_ln_g", "sgu_ln_b", "norm1_post_g", "norm2_pre_g", "norm2_post_g")


N_ROW_PARAMS = len(SMALL_ROWS) - 1


def _adamw_small(u_rows, u_others, params):
    n, n_oth = len(params), len(u_others)

    def body(*refs):
        urow_ref, others = refs[0], refs[1:1 + n_oth]
        wmv = refs[1 + n_oth:1 + n_oth + 3 * n]
        loss_ref = refs[1 + n_oth + 3 * n]
        outs = refs[2 + n_oth + 3 * n:]

        def total(ref, idx):
            g = ref[(0,) + idx].astype(F32)
            for q in range(1, N_CHIPS):
                g = g + ref[(q,) + idx].astype(F32)
            return g

        loss_ref[...] = total(urow_ref, (slice(0, 1), slice(None)))
        for p in range(n):
            if p < N_ROW_PARAMS:
                g = total(urow_ref, (slice(p + 1, p + 2), slice(None)))
            else:
                g = total(others[p - N_ROW_PARAMS], (slice(None), slice(None)))
            d, mn, vn = _adamw_math(wmv[3 * p][...], g, wmv[3 * p + 1][...], wmv[3 * p + 2][...])
            outs[4 * p][...] = g
            outs[4 * p + 1][...] = d
            outs[4 * p + 2][...] = mn
            outs[4 * p + 3][...] = vn

    flat = [a for p in params for a in p]
    out_shape = [jax.ShapeDtypeStruct((1, D_MODEL), F32)]
    for w, _, _ in params:
        out_shape += [jax.ShapeDtypeStruct(w.shape, F32)] * 4
    return pl.pallas_call(body, name="adamw_small", out_shape=out_shape)(u_rows, *u_others, *flat)


def _to_bf16(arrays):
    n = len(arrays)

    def body(*refs):
        for j in range(n):
            refs[n + j][...] = refs[j][...].astype(BF16)

    return pl.pallas_call(
        body, name="shards_to_bf16", out_shape=[jax.ShapeDtypeStruct(a.shape, BF16) for a in arrays],
    )(*arrays)


def kernel(x, norm1_pre_g, w_in, b_in, w_pool, pool_scale, sgu_ln_g, sgu_ln_b, w_spatial, b_spatial, w_sgu_proj, w_out, norm1_post_g, norm2_pre_g, w_ff1, w_ff2, norm2_post_g, loss_target, m_norm1_pre_g, m_w_in, m_b_in, m_w_pool, m_pool_scale, m_sgu_ln_g, m_sgu_ln_b, m_w_spatial, m_b_spatial, m_w_sgu_proj, m_w_out, m_norm1_post_g, m_norm2_pre_g, m_w_ff1, m_w_ff2, m_norm2_post_g, v_norm1_pre_g, v_w_in, v_b_in, v_w_pool, v_pool_scale, v_sgu_ln_g, v_sgu_ln_b, v_w_spatial, v_b_spatial, v_w_sgu_proj, v_w_out, v_norm1_post_g, v_norm2_pre_g, v_w_ff1, v_w_ff2, v_norm2_post_g):
    t_len = x.shape[1]
    row = lambda a: a.reshape(1, -1)
    x2 = x.reshape(t_len, D_MODEL)
    tgt2 = loss_target.reshape(t_len, D_MODEL)
    pg2 = lambda a: a.reshape(N_GROUPS * PG_SHARD, GROUP)

    shards_b = _to_bf16([w_in, w_pool, w_sgu_proj, w_out, w_ff1, w_ff2])

    def prework(token):
        return _prenorm(token, x2, row(norm1_pre_g))

    (win_f, wpool_f, wproj_f, g_out), (xn, xn_b) = _gather_behind("mix", shards_b[:4], None, prework,
                                                                  by_columns=(0,), by_rows=(1, 2))
    wout_f = g_out.reshape(D_MODEL, D_MODEL)
    bsp_t = b_spatial.T

    def forward(token):
        z, y, h1, *saved = _fwd_mix(token, xn, x2, win_f, row(b_in), wpool_f, row(pool_scale),
                                    row(sgu_ln_g), row(sgu_ln_b), w_spatial, bsp_t, wproj_f, wout_f,
                                    row(norm1_post_g))
        return h1, z, y, saved

    (g_ff1, g_ff2), (h1, z, y, saved) = _gather_behind("ff", shards_b[4:], win_f,
                                                       forward, by_columns=(0,))
    w2_f = g_ff2.reshape(D_FF, D_MODEL)
    hn_b, f_b, df1_b, df2_b, dh1, dg2post, dg2pre, loss_p = _mlp(h1, tgt2, row(norm2_pre_g), row(norm2_post_g),
                                                               g_ff1, w2_f)
    c_arr = lax.axis_index("c").astype(jnp.int32).reshape(1)
    chip_ff2, chip_ff1 = _wgrad_ff(c_arr, f_b, df2_b, hn_b, df1_b)
    chip_ff = [chip_ff1, chip_ff2]

    def backward_mix(token):
        return _bwd_mix(token, dh1, y, z, saved, wpool_f, row(pool_scale), row(sgu_ln_g), row(sgu_ln_b), w_spatial,
                        bsp_t, wproj_f, wout_f, row(norm1_post_g))

    summed_ff, (dz_b, mg_b, dy_b, p_pool, p_proj, dws, dbsp_t, dg1post, dps, dlng, dlnb,
                dbin) = _chip_exchange_behind("ff", chip_ff, [], backward_mix)

    rows = jnp.concatenate([jnp.broadcast_to(loss_p[:, 0:1], (1, D_MODEL)), dps, dlng, dlnb, dg1post, dg2pre, dg2post],
                           axis=0)
    smalls = [rows, dbin, dws.reshape(N_HEADS * SGU_BLOCK, SGU_BLOCK), dbsp_t.T]
    p_in, *got = _wgrad_in(xn_b, dz_b, [p_pool, p_proj], smalls)
    sums = _pair_sum_big("pair_sum_in", c_arr, [p_in, p_pool, p_proj], got[:3], smalls, got[3:])
    chip_bigs, chip_smalls = sums[:3], sums[3:]

    def backward_rest(token):
        dx, dg1pre = _bwd_in(token, dz_b, x2, dh1, row(norm1_pre_g), win_f)
        p_out, *got_out = _wgrad_out(mg_b, dy_b, [dg1pre])
        chip_out, chip_g1pre = _pair_sum_big("pair_sum_out", c_arr, [p_out], got_out[:1], [dg1pre], got_out[1:])
        upd_ff = _adamw_big("adamw_ff", p_out, summed_ff, [(w_ff1, m_w_ff1, v_w_ff1), (w_ff2, m_w_ff2, v_w_ff2)])
        return upd_ff[1][0], dx, chip_out, chip_g1pre, upd_ff

    summed_in, (_, dx, chip_out, chip_g1pre, upd_ff) = _chip_exchange_behind("in", chip_bigs, chip_smalls,
                                                                             backward_rest)

    def update_in(token):
        res = _adamw_big("adamw_in", token, summed_in[:1], [(w_in, m_w_in, v_w_in)])[0]
        return res[0], res

    summed_out, (_, upd_in) = _chip_exchange_behind("out", [chip_out], [chip_g1pre], update_in)
    big = {"in": upd_in, "ff1": upd_ff[0], "ff2": upd_ff[1]}
    ws2 = lambda a: a.reshape(N_HEADS * SGU_BLOCK, SGU_BLOCK)
    small_params = [(row(pool_scale), row(m_pool_scale), row(v_pool_scale)),
                    (row(sgu_ln_g), row(m_sgu_ln_g), row(v_sgu_ln_g)),
                    (row(sgu_ln_b), row(m_sgu_ln_b), row(v_sgu_ln_b)),
                    (row(norm1_post_g), row(m_norm1_post_g), row(v_norm1_post_g)),
                    (row(norm2_pre_g), row(m_norm2_pre_g), row(v_norm2_pre_g)),
                    (row(norm2_post_g), row(m_norm2_post_g), row(v_norm2_post_g)),
                    (row(norm1_pre_g), row(m_norm1_pre_g), row(v_norm1_pre_g)),
                    (row(b_in), row(m_b_in), row(v_b_in)),
                    (ws2(w_spatial), ws2(m_w_spatial), ws2(v_w_spatial)),
                    (b_spatial, m_b_spatial, v_b_spatial),
                    (pg2(w_pool), pg2(m_w_pool), pg2(v_w_pool)),
                    (pg2(w_sgu_proj), pg2(m_w_sgu_proj), pg2(v_w_sgu_proj)),
                    (w_out, m_w_out, v_w_out)]
    small_out = _adamw_small(summed_in[3], [summed_out[1]] + list(summed_in[4:])
                             + [summed_in[1], summed_in[2], summed_out[0]], small_params)
    loss = small_out[0][0, 0]
    small_names = SMALL_ROWS[1:] + ("norm1_pre_g", "b_in", "w_spatial", "b_spatial", "w_pool", "w_sgu_proj", "w_out")
    small = {nm: small_out[1 + 4 * p:5 + 4 * p] for p, nm in enumerate(small_names)}

    shapes = {"norm1_pre_g": norm1_pre_g.shape, "w_in": w_in.shape, "b_in": b_in.shape, "w_pool": w_pool.shape,
              "pool_scale": pool_scale.shape, "sgu_ln_g": sgu_ln_g.shape, "sgu_ln_b": sgu_ln_b.shape,
              "w_spatial": w_spatial.shape, "b_spatial": b_spatial.shape, "w_sgu_proj": w_sgu_proj.shape,
              "w_out": w_out.shape, "norm1_post_g": norm1_post_g.shape, "norm2_pre_g": norm2_pre_g.shape,
              "w_ff1": w_ff1.shape, "w_ff2": w_ff2.shape, "norm2_post_g": norm2_post_g.shape}
    source = {"w_in": big["in"], "w_ff1": big["ff1"], "w_ff2": big["ff2"], **small}
    order = list(shapes)
    outs = [loss, dx.reshape(x.shape)]
    for kind in range(4):
        outs += [source[nm][kind].reshape(shapes[nm]) for nm in order]
    return tuple(outs)
```

```python
import math

import jax
import jax.numpy as jnp
from jax import lax
from jax.experimental import pallas as pl
from jax.experimental.pallas import tpu as pltpu

F32, BF16 = jnp.float32, jnp.bfloat16
MESH = pl.DeviceIdType.MESH

D_MODEL = 1024
D_IN = 5120
D_FF = 4096
N_DEV = 8
N_CHIPS = 4
WINDOWS = (2, 4, 8, 16)
N_GROUPS = 4
GROUP = 256
HALO = 16
SGU_BLOCK = 128
N_HEADS = 4
HEAD = 256
CHUNK = 64
EPS = 1e-6
IN_SHARD = D_IN // N_DEV
FF_SHARD = D_FF // N_DEV
OUT_SHARD = D_MODEL // N_DEV
PG_SHARD = GROUP // N_DEV

ADAM_LR, ADAM_B1, ADAM_B2, ADAM_EPS, ADAM_WD, ADAM_STEP = 0.001, 0.9, 0.999, 1e-08, 0.01, 10

VMEM_LIMIT = 56 * 1024 * 1024
TM = 256
TM_BWD = 256
TM_IN = 512
PROJ_COLS = 512
GELU_C0 = math.sqrt(2.0 / math.pi)
GELU_C1 = 0.044715


def _dot(a, b):
    return jnp.dot(a, b, preferred_element_type=F32)


def _dot_nt(a, b):
    return lax.dot_general(a, b, (((1,), (1,)), ((), ())), preferred_element_type=F32)


def _dot_tn(a, b):
    return lax.dot_general(a, b, (((0,), (0,)), ((), ())), preferred_element_type=F32)


def _gelu(x):
    t = jnp.tanh(GELU_C0 * (x + GELU_C1 * (x * x * x)))
    return 0.5 * x * (1.0 + t), t


def _gelu_grad(x, t):
    return 0.5 * (1.0 + t) + 0.5 * x * (1.0 - t * t) * (GELU_C0 * (1.0 + 3.0 * GELU_C1 * (x * x)))


def _sigmoid(x):
    return 1.0 / (1.0 + jnp.exp(-x))


def _mean(x):
    return jnp.mean(x, axis=-1, keepdims=True)


def _colsum(x):
    return jnp.sum(x, axis=0, keepdims=True)


def _const_spec(shape):
    nd = len(shape)
    return pl.BlockSpec(shape, lambda *_: (0,) * nd, pipeline_mode=pl.Buffered(1))


def _acc_spec(shape):
    nd = len(shape)
    return pl.BlockSpec(shape, lambda *_: (0,) * nd)


def _masked_ws(ws_ref):
    ri = lax.broadcasted_iota(jnp.int32, (SGU_BLOCK, SGU_BLOCK), 0) // CHUNK
    ci = lax.broadcasted_iota(jnp.int32, (SGU_BLOCK, SGU_BLOCK), 1) // CHUNK
    return [jnp.where(ri >= ci, ws_ref[h], 0.0).astype(BF16) for h in range(N_HEADS)]


def _pool_fwd(pbuf, tile_idx, tm):
    pos = lax.broadcasted_iota(jnp.int32, (tm, 1), 0) + tile_idx * tm + 1
    pooled = []
    for g, w in enumerate(WINDOWS):
        e = pbuf[:, g * GROUP:(g + 1) * GROUP]
        s, sh = e, 1
        while sh < w:
            s = s + pltpu.roll(s, sh, 0)
            sh *= 2
        inv = 1.0 / jnp.minimum(pos, w).astype(F32)
        pooled.append(s[HALO:] * inv - e[HALO:])
    return pooled


def _spatial_head(ws_h, vb, bsp_ref, h, nblk):
    return jnp.concatenate(
        [_dot(ws_h, vb[n * SGU_BLOCK:(n + 1) * SGU_BLOCK, h * HEAD:(h + 1) * HEAD]) + bsp_ref[:, h:h + 1]
         for n in range(nblk)], axis=0)


def _prenorm(after, x, g1pre):
    t_len = x.shape[0]
    tm = TM_IN

    def body(after_ref, x_ref, g_ref, xn_ref, xnt_ref):
        del after_ref
        xv = x_ref[...]
        xnb = (xv * lax.rsqrt(_mean(xv * xv) + EPS) * g_ref[...]).astype(BF16)
        xn_ref[...] = xnb
        xnt_ref[...] = xnb.T

    return pl.pallas_call(
        body, name="prenorm", grid=(t_len // tm,),
        in_specs=[_ANY, pl.BlockSpec((tm, D_MODEL), lambda i: (i, 0)), _const_spec((1, D_MODEL))],
        out_specs=[pl.BlockSpec((tm, D_MODEL), lambda i: (i, 0)), pl.BlockSpec((D_MODEL, tm), lambda i: (0, i))],
        out_shape=[jax.ShapeDtypeStruct((t_len, D_MODEL), BF16), jax.ShapeDtypeStruct((D_MODEL, t_len), BF16)],
        compiler_params=pltpu.CompilerParams(dimension_semantics=("arbitrary",)),
    )(after, x, g1pre)


def _fwd_mix(after, xn, x, win_g, b_in, wpool, pool_scale, lng, lnb, ws, bsp_t, wproj, wout, g1post):
    t_len = x.shape[0]
    tm = TM
    nt = t_len // tm

    def body(after_ref, xn_ref, xb_ref, win_ref, bin_ref, wpool_ref, ps_ref, lng_ref, lnb_ref, ws_ref, bsp_ref,
             wproj_ref, wout_ref, g1post_ref, z_ref, y_ref, h1_ref, u_ref, gpu_ref, xhat_ref, gpv_ref, sa_ref,
             sb_ref, zcur, znext, pbuf):
        del after_ref
        s = pl.program_id(0)

        @pl.when(s == 0)
        def _():
            znext[...] = jnp.zeros((tm, D_IN), F32)
            pbuf[...] = jnp.zeros((tm + HALO, D_MODEL), F32)

        zcur[...] = znext[...]
        xnb = xn_ref[...]

        def project(p):
            cols = slice(p * PROJ_COLS, (p + 1) * PROJ_COLS)
            zp = _dot(xnb, win_ref[:, cols]) + bin_ref[:, cols]
            if (p + 1) * PROJ_COLS <= D_MODEL:
                z_ref[:, cols] = zp
            znext[:, cols] = zp

        project(0)
        pbuf[0:HALO, :] = jnp.where(s <= 1, 0.0, pbuf[0:HALO, :])
        pbuf[HALO:, :] = zcur[:, 0:D_MODEL]
        pooled = _pool_fwd(pbuf, jnp.maximum(s - 1, 0), tm)
        pbuf[0:HALO, :] = pbuf[tm:tm + HALO, :]
        a = jnp.concatenate([_dot(pooled[g].astype(BF16), wpool_ref[g]) for g in range(N_GROUPS)], axis=1)
        a = a * ps_ref[...]
        project(1)
        zu = zcur[:, D_MODEL:2 * D_MODEL]
        u, tu = _gelu(zu)
        u_ref[...] = u.astype(BF16)
        gpu_ref[...] = _gelu_grad(zu, tu).astype(BF16)
        project(2)
        zv = zcur[:, 2 * D_MODEL:3 * D_MODEL]
        gv, tv = _gelu(zv)
        xc = gv - _mean(gv)
        rln = lax.rsqrt(_mean(xc * xc) + EPS)
        xhat = xc * rln
        xhat_ref[...] = xhat.astype(BF16)
        gpv_ref[...] = (_gelu_grad(zv, tv) * rln).astype(BF16)
        vb = (xhat * lng_ref[...] + lnb_ref[...]).astype(BF16)
        wsm = _masked_ws(ws_ref)
        bbr = []
        for h in range(N_HEADS):
            project(3 + h)
            sv = _spatial_head(wsm[h], vb, bsp_ref, h, tm // SGU_BLOCK)
            bbr.append(_dot((u[:, h * HEAD:(h + 1) * HEAD] * sv).astype(BF16), wproj_ref[h]))
        bbr = jnp.concatenate(bbr, axis=1)
        project(7)
        sa = _sigmoid(zcur[:, 3 * D_MODEL:4 * D_MODEL])
        sb = _sigmoid(zcur[:, 4 * D_MODEL:5 * D_MODEL])
        sa_ref[...] = sa.astype(BF16)
        sb_ref[...] = sb.astype(BF16)
        project(8)
        yv = _dot((sa * a + sb * bbr).astype(BF16), wout_ref[...])
        y_ref[...] = yv
        project(9)
        ry = lax.rsqrt(_mean(yv * yv) + EPS)
        h1_ref[...] = xb_ref[...] + yv * ry * g1post_ref[...]

    proj = lambda w: pl.BlockSpec((tm, w), lambda s: (jnp.minimum(s, nt - 1), 0))
    mix = lambda w: pl.BlockSpec((tm, w), lambda s: (jnp.maximum(s - 1, 0), 0))
    return pl.pallas_call(
        body, name="fwd_mix", grid=(nt + 1,),
        in_specs=[_ANY, proj(D_MODEL), mix(D_MODEL),
                  _const_spec((D_MODEL, D_IN)),
                  _const_spec((1, D_IN)), _const_spec((N_GROUPS, GROUP, GROUP)), _const_spec((1, D_MODEL)),
                  _const_spec((1, D_MODEL)), _const_spec((1, D_MODEL)),
                  _const_spec((N_HEADS, SGU_BLOCK, SGU_BLOCK)), _const_spec((SGU_BLOCK, N_HEADS)),
                  _const_spec((N_HEADS, HEAD, HEAD)), _const_spec((D_MODEL, D_MODEL)), _const_spec((1, D_MODEL))],
        out_specs=[proj(D_MODEL), mix(D_MODEL), mix(D_MODEL)] + [mix(D_MODEL)] * 6,
        out_shape=[jax.ShapeDtypeStruct((t_len, D_MODEL), F32),
                   jax.ShapeDtypeStruct((t_len, D_MODEL), F32), jax.ShapeDtypeStruct((t_len, D_MODEL), F32)]
        + [jax.ShapeDtypeStruct((t_len, D_MODEL), BF16)] * 6,
        scratch_shapes=[pltpu.VMEM((tm, D_IN), F32), pltpu.VMEM((tm, D_IN), F32),
                        pltpu.VMEM((tm + HALO, D_MODEL), F32)],
        compiler_params=pltpu.CompilerParams(dimension_semantics=("arbitrary",), vmem_limit_bytes=VMEM_LIMIT),
    )(after, xn, x, win_g, b_in, wpool, pool_scale, lng, lnb, ws, bsp_t, wproj, wout, g1post)


def _mlp(h1, target, g2pre, g2post, w1_g, w2):
    t_len = h1.shape[0]
    tm = TM
    nt = t_len // tm

    def body(h1_ref, tgt_ref, g2pre_ref, g2post_ref, w1_ref, w2_ref,
             hn_ref, f_ref, df1_ref, df2_ref, dh1_ref, dg2post_ref, dg2pre_ref, loss_ref, f1_scr):
        i = pl.program_id(0)

        @pl.when(i == 0)
        def _():
            dg2post_ref[...] = jnp.zeros_like(dg2post_ref)
            dg2pre_ref[...] = jnp.zeros_like(dg2pre_ref)
            loss_ref[...] = jnp.zeros_like(loss_ref)

        h = h1_ref[...]
        r2 = lax.rsqrt(_mean(h * h) + EPS)
        nh = h * r2
        hnb = (nh * g2pre_ref[...]).astype(BF16)
        hn_ref[...] = hnb.T
        for k in range(N_DEV):
            f1_scr[:, k * FF_SHARD:(k + 1) * FF_SHARD] = _dot(hnb, w1_ref[:, k * FF_SHARD:(k + 1) * FF_SHARD])
        r = jnp.maximum(f1_scr[...], 0.0)
        fb = (r * r).astype(BF16)
        f_ref[...] = fb.T
        f2 = _dot(fb, w2_ref[...])
        rf = lax.rsqrt(_mean(f2 * f2) + EPS)
        nf = f2 * rf
        diff = h + nf * g2post_ref[...] - tgt_ref[...]
        loss_ref[...] += (0.5 / D_MODEL) * jnp.sum(diff * diff)
        dout = diff * (1.0 / D_MODEL)
        dg2post_ref[...] += _colsum(dout * nf)
        dn = dout * g2post_ref[...]
        df2b = (rf * (dn - nf * _mean(dn * nf))).astype(BF16)
        df2_ref[...] = df2b
        df = _dot_nt(df2b, w2_ref[...])
        df1b = (df * (2.0 * jnp.maximum(f1_scr[...], 0.0))).astype(BF16)
        df1_ref[...] = df1b
        dhn = _dot_nt(df1b, w1_ref[...])
        dg2pre_ref[...] += _colsum(dhn * nh)
        dnh = dhn * g2pre_ref[...]
        dh1_ref[...] = dout + r2 * (dnh - nh * _mean(dnh * nh))

    tok = lambda w: pl.BlockSpec((tm, w), lambda i: (i, 0))
    return pl.pallas_call(
        body, name="mlp_fwd_bwd", grid=(nt,),
        in_specs=[tok(D_MODEL), tok(D_MODEL), _const_spec((1, D_MODEL)), _const_spec((1, D_MODEL)),
                  _const_spec((D_MODEL, D_FF)), _const_spec((D_FF, D_MODEL))],
        out_specs=[pl.BlockSpec((D_MODEL, tm), lambda i: (0, i)), pl.BlockSpec((D_FF, tm), lambda i: (0, i)),
                   tok(D_FF), tok(D_MODEL), tok(D_MODEL),
                   _acc_spec((1, D_MODEL)), _acc_spec((1, D_MODEL)), _acc_spec((1, 128))],
        out_shape=[jax.ShapeDtypeStruct((D_MODEL, t_len), BF16), jax.ShapeDtypeStruct((D_FF, t_len), BF16),
                   jax.ShapeDtypeStruct((t_len, D_FF), BF16), jax.ShapeDtypeStruct((t_len, D_MODEL), BF16),
                   jax.ShapeDtypeStruct((t_len, D_MODEL), F32), jax.ShapeDtypeStruct((1, D_MODEL), F32),
                   jax.ShapeDtypeStruct((1, D_MODEL), F32), jax.ShapeDtypeStruct((1, 128), F32)],
        scratch_shapes=[pltpu.VMEM((tm, D_FF), F32)],
        compiler_params=pltpu.CompilerParams(dimension_semantics=("arbitrary",), vmem_limit_bytes=VMEM_LIMIT),
    )(h1, target, g2pre, g2post, w1_g, w2)


def _bwd_mix(after, dh1, y, z, saved, wpool, pool_scale, lng, lnb, ws, bsp_t, wproj, wout, g1post):
    t_len = y.shape[0]
    tm = TM_BWD
    nt = t_len // tm
    nblk = tm // SGU_BLOCK

    def body(after_ref, dh1_ref, y_ref, z_ref, zh_ref, u_ref, gpu_ref, xhat_ref, gpv_ref, sa_ref, sb_ref,
             wpool_ref, ps_ref, lng_ref, lnb_ref, ws_ref,
             bsp_ref, wproj_ref, wout_ref, g1post_ref,
             dz_ref, mg_ref, dy_ref, ppool_ref, pproj_ref, dws_ref, dbsp_ref, dg1post_ref, dps_ref,
             dlng_ref, dlnb_ref, dbin_ref, pbuf, qbuf, dwpool_ref, dwproj_ref):
        del after_ref
        i = pl.program_id(0)
        ti = nt - 1 - i

        @pl.when(i == 0)
        def _():
            for ref in (dwpool_ref, dwproj_ref, dws_ref, dbsp_ref, dg1post_ref, dps_ref, dlng_ref, dlnb_ref,
                        dbin_ref):
                ref[...] = jnp.zeros_like(ref)
            qbuf[tm:tm + HALO, :] = jnp.zeros((HALO, D_MODEL), F32)

        pbuf[0:HALO, :] = jnp.where(ti > 0, zh_ref[...], 0.0)
        pbuf[HALO:, :] = z_ref[...]
        pooled = _pool_fwd(pbuf, ti, tm)
        pooled_b = [p.astype(BF16) for p in pooled]
        a_raw = jnp.concatenate([_dot(pooled_b[g], wpool_ref[g]) for g in range(N_GROUPS)], axis=1)
        wsm = _masked_ws(ws_ref)
        u = u_ref[...].astype(F32)
        xhat = xhat_ref[...].astype(F32)
        vb = (xhat * lng_ref[...] + lnb_ref[...]).astype(BF16)
        sv_heads = [_spatial_head(wsm[h], vb, bsp_ref, h, nblk) for h in range(N_HEADS)]
        gated_b = [(u[:, h * HEAD:(h + 1) * HEAD] * sv_heads[h]).astype(BF16) for h in range(N_HEADS)]
        bbr = jnp.concatenate([_dot(gated_b[h], wproj_ref[h]) for h in range(N_HEADS)], axis=1)
        sa = sa_ref[...].astype(F32)
        sb = sb_ref[...].astype(F32)
        a = a_raw * ps_ref[...]
        mg_ref[...] = (sa * a + sb * bbr).astype(BF16).T

        dh = dh1_ref[...]
        yv = y_ref[...]
        ry = lax.rsqrt(_mean(yv * yv) + EPS)
        ny = yv * ry
        dg1post_ref[...] += _colsum(dh * ny)
        dn = dh * g1post_ref[...]
        dyb = (ry * (dn - ny * _mean(dn * ny))).astype(BF16)
        dy_ref[...] = dyb
        dmg = _dot_nt(dyb, wout_ref[...])

        da = dmg * sa
        dbbr = dmg * sb
        dzga = dmg * a * sa * (1.0 - sa)
        dzgb = dmg * bbr * sb * (1.0 - sb)
        dz_ref[:, 3 * D_MODEL:4 * D_MODEL] = dzga.astype(BF16)
        dz_ref[:, 4 * D_MODEL:5 * D_MODEL] = dzgb.astype(BF16)
        dbin_ref[:, 3 * D_MODEL:4 * D_MODEL] += _colsum(dzga)
        dbin_ref[:, 4 * D_MODEL:5 * D_MODEL] += _colsum(dzgb)

        dps_ref[...] += _colsum(da * a_raw)
        da_raw_b = (da * ps_ref[...]).astype(BF16)
        pos = lax.broadcasted_iota(jnp.int32, (tm, 1), 0) + ti * tm + 1
        dpooled = []
        for g, w in enumerate(WINDOWS):
            cols = slice(g * GROUP, (g + 1) * GROUP)
            dwpool_ref[g] += _dot_tn(pooled_b[g], da_raw_b[:, cols])
            dp = _dot_nt(da_raw_b[:, cols], wpool_ref[g])
            dpooled.append(dp)
            qbuf[0:tm, cols] = dp * (1.0 / jnp.minimum(pos, w).astype(F32))
        n_ext = tm + HALO
        dzp = []
        for g, w in enumerate(WINDOWS):
            e = qbuf[:, g * GROUP:(g + 1) * GROUP]
            s, sh = e, 1
            while sh < w:
                s = s + pltpu.roll(s, n_ext - sh, 0)
                sh *= 2
            dzp.append(s[0:tm] - dpooled[g])
        qbuf[tm:tm + HALO, :] = qbuf[0:HALO, :]
        dzp = jnp.concatenate(dzp, axis=1)
        dz_ref[:, 0:D_MODEL] = dzp.astype(BF16)
        dbin_ref[:, 0:D_MODEL] += _colsum(dzp)

        dv_heads = []
        du_heads = []
        for h in range(N_HEADS):
            cols = slice(h * HEAD, (h + 1) * HEAD)
            dbbr_b = dbbr[:, cols].astype(BF16)
            dwproj_ref[h] += _dot_tn(gated_b[h], dbbr_b)
            dgated = _dot_nt(dbbr_b, wproj_ref[h])
            du_heads.append(dgated * sv_heads[h])
            dsv = dgated * u[:, cols]
            dsv_b = dsv.astype(BF16)
            rows = []
            for n in range(nblk):
                blk = slice(n * SGU_BLOCK, (n + 1) * SGU_BLOCK)
                rows.append(_dot_tn(wsm[h], dsv_b[blk]))
                dws_ref[h] += _dot_nt(dsv_b[blk], vb[blk, cols])
                dbsp_ref[:, h:h + 1] += jnp.sum(dsv[blk], axis=1, keepdims=True)
            dv_heads.append(jnp.concatenate(rows, axis=0))
        dzu = jnp.concatenate(du_heads, axis=1) * gpu_ref[...].astype(F32)
        dz_ref[:, D_MODEL:2 * D_MODEL] = dzu.astype(BF16)
        dbin_ref[:, D_MODEL:2 * D_MODEL] += _colsum(dzu)
        dv = jnp.concatenate(dv_heads, axis=1)
        dlng_ref[...] += _colsum(dv * xhat)
        dlnb_ref[...] += _colsum(dv)
        dxh = dv * lng_ref[...]
        dzv = (dxh - _mean(dxh) - xhat * _mean(dxh * xhat)) * gpv_ref[...].astype(F32)
        dz_ref[:, 2 * D_MODEL:3 * D_MODEL] = dzv.astype(BF16)
        dbin_ref[:, 2 * D_MODEL:3 * D_MODEL] += _colsum(dzv)

        @pl.when(i == nt - 1)
        def _():
            ri = lax.broadcasted_iota(jnp.int32, (SGU_BLOCK, SGU_BLOCK), 0) // CHUNK
            ci = lax.broadcasted_iota(jnp.int32, (SGU_BLOCK, SGU_BLOCK), 1) // CHUNK
            for h in range(N_HEADS):
                dws_ref[h] = jnp.where(ri >= ci, dws_ref[h], 0.0)
            for slot in range(N_DEV):
                own = slice(_owner_of_slot(slot) * PG_SHARD, (_owner_of_slot(slot) + 1) * PG_SHARD)
                for g in range(N_GROUPS):
                    rows = slice(g * PG_SHARD, (g + 1) * PG_SHARD)
                    ppool_ref[slot, rows, :] = dwpool_ref[g, own, :].astype(BF16)
                    pproj_ref[slot, rows, :] = dwproj_ref[g, own, :].astype(BF16)

    tok = lambda w: pl.BlockSpec((tm, w), lambda i: (nt - 1 - i, 0))
    halo = pl.BlockSpec((HALO, D_MODEL), lambda i: (jnp.maximum((nt - 1 - i) * (tm // HALO) - 1, 0), 0))
    return pl.pallas_call(
        body, name="bwd_mix", grid=(nt,),
        in_specs=[_ANY, tok(D_MODEL), tok(D_MODEL), tok(D_MODEL), halo] + [tok(D_MODEL)] * 6
        + [_const_spec((N_GROUPS, GROUP, GROUP)),
                  _const_spec((1, D_MODEL)), _const_spec((1, D_MODEL)), _const_spec((1, D_MODEL)),
                  _const_spec((N_HEADS, SGU_BLOCK, SGU_BLOCK)), _const_spec((SGU_BLOCK, N_HEADS)),
                  _const_spec((N_HEADS, HEAD, HEAD)), _const_spec((D_MODEL, D_MODEL)), _const_spec((1, D_MODEL))],
        out_specs=[tok(D_IN), pl.BlockSpec((D_MODEL, tm), lambda i: (0, nt - 1 - i)), tok(D_MODEL),
                   _acc_spec((N_DEV, N_GROUPS * PG_SHARD, GROUP)), _acc_spec((N_DEV, N_HEADS * PG_SHARD, HEAD)),
                   _acc_spec((N_HEADS, SGU_BLOCK, SGU_BLOCK)), _acc_spec((SGU_BLOCK, N_HEADS)),
                   _acc_spec((1, D_MODEL)), _acc_spec((1, D_MODEL)), _acc_spec((1, D_MODEL)), _acc_spec((1, D_MODEL)),
                   _acc_spec((1, D_IN))],
        out_shape=[jax.ShapeDtypeStruct((t_len, D_IN), BF16),
                   jax.ShapeDtypeStruct((D_MODEL, t_len), BF16), jax.ShapeDtypeStruct((t_len, D_MODEL), BF16),
                   jax.ShapeDtypeStruct((N_DEV, N_GROUPS * PG_SHARD, GROUP), BF16),
                   jax.ShapeDtypeStruct((N_DEV, N_HEADS * PG_SHARD, HEAD), BF16),
                   jax.ShapeDtypeStruct((N_HEADS, SGU_BLOCK, SGU_BLOCK), F32),
                   jax.ShapeDtypeStruct((SGU_BLOCK, N_HEADS), F32),
                   jax.ShapeDtypeStruct((1, D_MODEL), F32), jax.ShapeDtypeStruct((1, D_MODEL), F32),
                   jax.ShapeDtypeStruct((1, D_MODEL), F32), jax.ShapeDtypeStruct((1, D_MODEL), F32),
                   jax.ShapeDtypeStruct((1, D_IN), F32)],
        scratch_shapes=[pltpu.VMEM((tm + HALO, D_MODEL), F32), pltpu.VMEM((tm + HALO, D_MODEL), F32),
                        pltpu.VMEM((N_GROUPS, GROUP, GROUP), F32), pltpu.VMEM((N_HEADS, HEAD, HEAD), F32)],
        compiler_params=pltpu.CompilerParams(dimension_semantics=("arbitrary",), vmem_limit_bytes=VMEM_LIMIT),
    )(after, dh1, y, z, z, *saved, wpool, pool_scale, lng, lnb, ws, bsp_t, wproj, wout, g1post)


def _bwd_in(after, dz_b, x, dh1, g1pre, win_g):
    t_len = x.shape[0]
    tm = TM_IN
    nt = t_len // tm

    def body(after_ref, dz_ref, x_ref, dh1_ref, g1_ref, win_ref, dx_ref, dg1pre_ref):
        del after_ref

        @pl.when(pl.program_id(0) == 0)
        def _():
            dg1pre_ref[...] = jnp.zeros_like(dg1pre_ref)

        dxn = _dot_nt(dz_ref[...], win_ref[...])
        xv = x_ref[...]
        r1 = lax.rsqrt(_mean(xv * xv) + EPS)
        nx = xv * r1
        dg1pre_ref[...] += _colsum(dxn * nx)
        dnx = dxn * g1_ref[...]
        dx_ref[...] = r1 * (dnx - nx * _mean(dnx * nx)) + dh1_ref[...]

    tok = lambda w: pl.BlockSpec((tm, w), lambda i: (i, 0))
    return pl.pallas_call(
        body, name="bwd_in", grid=(nt,),
        in_specs=[_ANY, tok(D_IN), tok(D_MODEL), tok(D_MODEL), _const_spec((1, D_MODEL)),
                  _const_spec((D_MODEL, D_IN))],
        out_specs=[tok(D_MODEL), _acc_spec((1, D_MODEL))],
        out_shape=[jax.ShapeDtypeStruct((t_len, D_MODEL), F32), jax.ShapeDtypeStruct((1, D_MODEL), F32)],
        compiler_params=pltpu.CompilerParams(dimension_semantics=("arbitrary",), vmem_limit_bytes=VMEM_LIMIT),
    )(after, dz_b, x, dh1, g1pre, win_g)


def _owner_of_slot(s):
    return 4 * ((s // 2) % 2) + 2 * (s % 2) + s // 4


def _wgrad_ff(c_arr, f_t, df2_b, hn_t, df1_b):
    t_len = df2_b.shape[0]
    half_shard = FF_SHARD // 2

    def body(c_ref, f_ref, df2_ref, hn_ref, df1_ref, o2_ref, o1_ref, buf2, buf1, land2, land1, send_sems, recv_sems):
        del c_ref
        s = pl.program_id(0)
        k, h = (s // 2) % N_CHIPS, s % 2
        x, y, c = _coords()

        def copies(k):
            return [pltpu.make_async_remote_copy(src_ref=buf.at[k], dst_ref=land.at[k], send_sem=send_sems.at[j, k],
                                                 recv_sem=recv_sems.at[j, k], device_id=(x, y, 1 - c),
                                                 device_id_type=MESH)
                    for j, (buf, land) in enumerate(((buf2, land2), (buf1, land1)))]

        @pl.when(s < N_DEV)
        def _():
            buf2[k, h] = _dot(f_ref[...], df2_ref[...]).astype(BF16)
            buf1[k, h] = _dot(hn_ref[...], df1_ref[...]).astype(BF16)

            @pl.when(h == 1)
            def _():
                for cp in copies(k):
                    cp.start()

        @pl.when(s >= N_DEV)
        def _():
            @pl.when(h == 0)
            def _():
                for cp in copies(k):
                    cp.wait_recv()

            o2_ref[...] = (_dot(f_ref[...], df2_ref[...]) + land2[k, h].astype(F32)).astype(BF16)
            o1_ref[...] = (_dot(hn_ref[...], df1_ref[...]) + land1[k, h].astype(F32)).astype(BF16)

        @pl.when(s == 2 * N_DEV - 1)
        def _():
            for k_sent in range(N_CHIPS):
                for cp in copies(k_sent):
                    cp.wait_send()

    def part(s, c_ref):
        owner = 2 * ((s // 2) % N_CHIPS) + jnp.where(s < N_DEV, 1 - c_ref[0], c_ref[0])
        return 2 * owner + s % 2

    mine_slot = lambda s: jnp.maximum(s // 2 - N_CHIPS, 0)
    mine_half = lambda s: jnp.where(s >= N_DEV, s % 2, 0)
    return pl.pallas_call(
        body, name="wgrad_ff",
        grid_spec=pltpu.PrefetchScalarGridSpec(
            num_scalar_prefetch=1, grid=(2 * N_DEV,),
            in_specs=[pl.BlockSpec((half_shard, t_len), lambda s, c_ref: (part(s, c_ref), 0)),
                      pl.BlockSpec((t_len, D_MODEL), lambda s, c_ref: (0, 0), pipeline_mode=pl.Buffered(1)),
                      pl.BlockSpec((D_MODEL, t_len), lambda s, c_ref: (0, 0), pipeline_mode=pl.Buffered(1)),
                      pl.BlockSpec((t_len, half_shard), lambda s, c_ref: (0, part(s, c_ref)))],
            out_specs=[pl.BlockSpec((None, half_shard, D_MODEL), lambda s, c_ref: (mine_slot(s), mine_half(s), 0)),
                       pl.BlockSpec((None, D_MODEL, half_shard), lambda s, c_ref: (mine_slot(s), 0, mine_half(s)))],
            scratch_shapes=[pltpu.VMEM((N_CHIPS, 2, half_shard, D_MODEL), BF16),
                            pltpu.VMEM((N_CHIPS, 2, D_MODEL, half_shard), BF16)] * 2
            + [pltpu.SemaphoreType.DMA((2, N_CHIPS))] * 2),
        out_shape=[jax.ShapeDtypeStruct((N_CHIPS, FF_SHARD, D_MODEL), BF16),
                   jax.ShapeDtypeStruct((N_CHIPS, D_MODEL, FF_SHARD), BF16)],
        compiler_params=pltpu.CompilerParams(dimension_semantics=("arbitrary",), vmem_limit_bytes=VMEM_LIMIT),
    )(c_arr, f_t, df2_b, hn_t, df1_b)


def _wgrad_pair(name, a, b, in_specs, halves, slot_shape, bigs, smalls):
    nb, nx = len(bigs), len(bigs) + len(smalls)

    def body(a_ref, b_ref, *refs):
        extra_in, out_ref, extra_out = refs[:nx], refs[nx], refs[nx + 1:2 * nx + 1]
        both, land = refs[2 * nx + 1:2 * nx + 3]
        extra_land = refs[2 * nx + 3:3 * nx + 3]
        send_sems, recv_sems, extra_send, extra_recv = refs[3 * nx + 3:]
        q = pl.program_id(0)
        x, y, c = _coords()
        to_sibling = dict(device_id=(x, y, 1 - c), device_id_type=MESH)

        def slot_copy(k):
            return pltpu.make_async_remote_copy(src_ref=both.at[k % 2, 1 - c], dst_ref=land.at[k],
                                                send_sem=send_sems.at[k], recv_sem=recv_sems.at[k], **to_sibling)

        def extra_copies():
            return [pltpu.make_async_remote_copy(
                src_ref=extra_in[j].at[pl.ds(N_CHIPS * (1 - c), N_CHIPS)] if j < nb else extra_in[j],
                dst_ref=extra_land[j], send_sem=extra_send.at[j], recv_sem=extra_recv.at[j], **to_sibling)
                for j in range(nx)]

        @pl.when(q == 0)
        def _():
            for cp in extra_copies():
                cp.start()

        @pl.when(q < N_CHIPS)
        def _():
            @pl.when(q >= 2)
            def _():
                slot_copy(q - 2).wait_send()

            half0, half1 = halves(_dot(a_ref[...], b_ref[...]))
            both[q % 2, 0] = half0.astype(BF16)
            both[q % 2, 1] = half1.astype(BF16)
            slot_copy(q).start()

        @pl.when(q >= 1)
        def _():
            k = q - 1
            slot_copy(k).wait_recv()
            out_ref[...] = (both[k % 2, c].astype(F32) + land[k].astype(F32)).astype(BF16)

        @pl.when(q == N_CHIPS)
        def _():
            slot_copy(N_CHIPS - 2).wait_send()
            slot_copy(N_CHIPS - 1).wait_send()
            for cp in extra_copies():
                cp.wait()
            for j in range(nx):
                if j < nb:
                    mine = extra_in[j][pl.ds(N_CHIPS * c, N_CHIPS)]
                    extra_out[j][...] = (mine.astype(F32) + extra_land[j][...].astype(F32)).astype(BF16)
                else:
                    extra_out[j][...] = extra_in[j][...] + extra_land[j][...]

    got_shapes = [(N_CHIPS,) + t.shape[1:] for t in bigs] + [t.shape for t in smalls]
    dtypes = [t.dtype for t in bigs] + [t.dtype for t in smalls]
    return pl.pallas_call(
        body, name=name, grid=(N_CHIPS + 1,),
        in_specs=list(in_specs) + [_const_spec(t.shape) for t in list(bigs) + list(smalls)],
        out_specs=[pl.BlockSpec((None,) + slot_shape, lambda q: (jnp.maximum(q - 1, 0), 0, 0))]
        + [pl.BlockSpec(s, lambda q, nd=len(s): (0,) * nd) for s in got_shapes],
        out_shape=[jax.ShapeDtypeStruct((N_CHIPS,) + slot_shape, BF16)]
        + [jax.ShapeDtypeStruct(s, d) for s, d in zip(got_shapes, dtypes)],
        scratch_shapes=[pltpu.VMEM((2, 2) + slot_shape, BF16), pltpu.VMEM((N_CHIPS,) + slot_shape, BF16)]
        + [pltpu.VMEM(s, d) for s, d in zip(got_shapes, dtypes)]
        + [pltpu.SemaphoreType.DMA((N_CHIPS,))] * 2 + [pltpu.SemaphoreType.DMA((max(nx, 1),))] * 2,
        compiler_params=pltpu.CompilerParams(dimension_semantics=("arbitrary",), vmem_limit_bytes=VMEM_LIMIT),
    )(a, b, *bigs, *smalls)


def _wgrad_in(xn_t, dz_b, bigs, smalls):
    t_len = dz_b.shape[0]
    return _wgrad_pair("wgrad_in", xn_t, dz_b,
                       [pl.BlockSpec((D_MODEL, t_len), lambda q: (0, 0), pipeline_mode=pl.Buffered(1)),
                        pl.BlockSpec((t_len, 2 * IN_SHARD), lambda q: (0, jnp.minimum(q, N_CHIPS - 1)))],
                       lambda res: (res[:, 0:IN_SHARD], res[:, IN_SHARD:2 * IN_SHARD]), (D_MODEL, IN_SHARD),
                       bigs, smalls)


def _wgrad_out(mg_t, dy_b, smalls):
    t_len = dy_b.shape[0]
    return _wgrad_pair("wgrad_out", mg_t, dy_b,
                       [pl.BlockSpec((2 * OUT_SHARD, t_len), lambda q: (jnp.minimum(q, N_CHIPS - 1), 0)),
                        pl.BlockSpec((t_len, D_MODEL), lambda q: (0, 0), pipeline_mode=pl.Buffered(1))],
                       lambda res: (res[0:OUT_SHARD], res[OUT_SHARD:2 * OUT_SHARD]), (OUT_SHARD, D_MODEL),
                       [], smalls)


def _coords():
    return lax.axis_index("x"), lax.axis_index("y"), lax.axis_index("c")


_ANY = pl.BlockSpec(memory_space=pl.ANY)


_HBM = pl.BlockSpec(memory_space=pltpu.HBM)
_SEM = pl.BlockSpec(memory_space=pltpu.SEMAPHORE)
_VMEM = pl.BlockSpec(memory_space=pltpu.VMEM)
_EFFECT = pltpu.SideEffectType.DATAFLOW_SIDE_EFFECTING
_TOKEN = jax.ShapeDtypeStruct((8, 128), F32)


def _in_hbm(a):
    return pltpu.with_memory_space_constraint(a, pltpu.HBM)


def _split_call(name, body, n_sems_out, arrays, sems_in=(), after=None):
    na, ns = len(arrays), len(sems_in)
    has_after = after is not None

    def kernel_body(*refs):
        arr = refs[:na]
        s_in = refs[na:na + ns]
        outs = refs[na + ns + has_after:]
        body(arr, s_in, outs[:n_sems_out])
        outs[-1][...] = jnp.zeros((8, 128), F32)

    out_shape = ([pltpu.SemaphoreType.DMA(())] * n_sems_out + [pltpu.HBM(a.shape, a.dtype) for a in arrays] + [_TOKEN])
    res = pl.pallas_call(
        kernel_body, name=name, out_shape=out_shape,
        in_specs=[_HBM] * na + [_SEM] * ns + [_ANY] * has_after,
        out_specs=[_SEM] * n_sems_out + [_HBM] * na + [_VMEM],
        input_output_aliases={i: n_sems_out + i for i in range(na)},
        compiler_params=pltpu.CompilerParams(has_side_effects=_EFFECT),
    )(*[_in_hbm(a) for a in arrays], *sems_in, *([after] if has_after else []))
    return list(res[:n_sems_out]), list(res[n_sems_out:n_sems_out + na]), res[-1]


def _wait_bytes_of(ref, send_sem, recv_sem, peer, send=True, recv=True):
    cp = pltpu.make_async_remote_copy(src_ref=ref, dst_ref=ref, send_sem=send_sem, recv_sem=recv_sem,
                                      device_id=peer, device_id_type=MESH)
    if send:
        cp.wait_send()
    if recv:
        cp.wait_recv()


def _gather_behind(tag, shards, after, work, by_columns=(), by_rows=()):
    n = len(shards)

    def land_shape(j, s):
        if j in by_columns:
            return (s.shape[0], N_DEV * s.shape[1])
        if j in by_rows:
            return (s.shape[0], N_DEV * s.shape[1], s.shape[2])
        return (N_DEV,) + s.shape

    lands = [lax.empty(land_shape(j, s), s.dtype) for j, s in enumerate(shards)]

    def slots(arr, j, first, count=1):
        if j in by_columns:
            cols = shards[j].shape[1]
            return arr[n + j].at[:, pl.ds(pl.multiple_of(first * cols, 128), count * cols)]
        if j in by_rows:
            rows = shards[j].shape[1]
            return arr[n + j].at[:, pl.ds(pl.multiple_of(first * rows, 16), count * rows), :]
        return arr[n + j].at[first] if count == 1 else arr[n + j].at[pl.ds(first, count)]

    def own_slot(arr, j, sem):
        x, y, c = _coords()
        return pltpu.make_async_copy(arr[j], slots(arr, j, 4 * x + 2 * y + c), sem)

    def start(arr, _, sems):
        x, y, c = _coords()
        me = 4 * x + 2 * y + c
        for j in range(n):
            for chip in [(1 - x, y), (x, 1 - y), (1 - x, 1 - y)]:
                pltpu.make_async_remote_copy(src_ref=arr[j], dst_ref=slots(arr, j, me), send_sem=sems[j],
                                             recv_sem=sems[n + j], device_id=(*chip, c), device_id_type=MESH).start()
        for j in range(n):
            pltpu.make_async_remote_copy(src_ref=arr[j], dst_ref=slots(arr, j, me), send_sem=sems[2 * n + j],
                                         recv_sem=sems[3 * n + j], device_id=(x, y, 1 - c),
                                         device_id_type=MESH).start()
            own_slot(arr, j, sems[4 * n + j]).start()

    def middle(arr, s_in, sems):
        x, y, c = _coords()
        sibling = (x, y, 1 - c)
        for j in range(n):
            _wait_bytes_of(slots(arr, j, 0, 3), s_in[j], s_in[n + j], sibling)
            for chip in [(1 - x, y), (x, 1 - y), (1 - x, 1 - y)]:
                slot = slots(arr, j, 4 * chip[0] + 2 * chip[1] + c)
                pltpu.make_async_remote_copy(src_ref=slot, dst_ref=slot, send_sem=sems[j], recv_sem=sems[n + j],
                                             device_id=sibling, device_id_type=MESH).start()

    def finish(arr, s_in, _):
        x, y, c = _coords()
        sibling = (x, y, 1 - c)
        for j in range(n):
            _wait_bytes_of(slots(arr, j, 0, 1), s_in[j], s_in[n + j], sibling)
            own_slot(arr, j, s_in[2 * n + j]).wait()
            _wait_bytes_of(slots(arr, j, 0, 3), s_in[3 * n + j], s_in[4 * n + j], sibling)

    sems, arrays, token = _split_call("gather_%s_start" % tag, start, 5 * n, list(shards) + lands, after=after)
    result = work(token)
    fwd_sems, arrays, token = _split_call("gather_%s_middle" % tag, middle, 2 * n, arrays, sems_in=sems[:2 * n],
                                          after=result[0])
    _, arrays, _ = _split_call("gather_%s_finish" % tag, finish, 0, arrays, sems_in=sems[2 * n:] + fwd_sems,
                               after=token)
    return arrays[n:], result


def _chip_exchange_behind(tag, bigs, smalls, work):
    nb, n = len(bigs), len(bigs) + len(smalls)
    lands = [lax.empty(s.shape, s.dtype) for s in bigs] + [lax.empty((N_CHIPS,) + s.shape, s.dtype) for s in smalls]

    def own_slot(arr, j, sem):
        x, y, _ = _coords()
        q_me = 2 * x + y
        return pltpu.make_async_copy(arr[j].at[q_me] if j < nb else arr[j], arr[n + j].at[q_me], sem)

    def start(arr, _, sems):
        x, y, c = _coords()
        q_me = 2 * x + y
        for j in range(n):
            for peer in [(1 - x, y, c), (x, 1 - y, c), (1 - x, 1 - y, c)]:
                piece = arr[j].at[2 * peer[0] + peer[1]] if j < nb else arr[j]
                pltpu.make_async_remote_copy(src_ref=piece, dst_ref=arr[n + j].at[q_me],
                                             send_sem=sems[j], recv_sem=sems[n + j], device_id=peer,
                                             device_id_type=MESH).start()
            own_slot(arr, j, sems[2 * n + j]).start()

    def finish(arr, s_in, _):
        x, y, c = _coords()
        for j in range(n):
            _wait_bytes_of(arr[n + j].at[pl.ds(0, 3)], s_in[j], s_in[n + j], (x, y, 1 - c))
            own_slot(arr, j, s_in[2 * n + j]).wait()

    sems, arrays, token = _split_call("chip_exchange_%s_start" % tag, start, 3 * n, list(bigs) + list(smalls) + lands)
    result = work(token)
    _, arrays, _ = _split_call("chip_exchange_%s_finish" % tag, finish, 0, arrays, sems_in=sems, after=result[0])
    return arrays[n:], result


def _adamw_math(w, g, m, v):
    m = ADAM_B1 * m + (1.0 - ADAM_B1) * g
    v = ADAM_B2 * v + (1.0 - ADAM_B2) * (g * g)
    m_hat = m / (1.0 - ADAM_B1 ** ADAM_STEP)
    v_hat = v / (1.0 - ADAM_B2 ** ADAM_STEP)
    delta = -ADAM_LR * (m_hat / (jnp.sqrt(v_hat) + ADAM_EPS) + ADAM_WD * w)
    return delta, m, v


ADAMW_STEPS = 2


def _adamw_big(name, after, chip_sums, params):
    n = len(params)

    def body(after_ref, *refs):
        del after_ref
        outs = refs[4 * n:]
        for j in range(n):
            t_ref, w_ref, m_ref, v_ref = refs[4 * j:4 * j + 4]
            g = t_ref[0].astype(F32)
            for q in range(1, N_CHIPS):
                g = g + t_ref[q].astype(F32)
            d, mn, vn = _adamw_math(w_ref[...], g, m_ref[...], v_ref[...])
            for out, val in zip(outs[4 * j:4 * j + 4], (g, d, mn, vn)):
                out[...] = val

    in_specs, out_specs, out_shape, operands = [_ANY], [], [], [after]
    for t, (w, m, v) in zip(chip_sums, params):
        rows, cols = w.shape
        tr = rows // ADAMW_STEPS
        blk = pl.BlockSpec((tr, cols), lambda r: (r, 0))
        in_specs += [pl.BlockSpec((N_CHIPS, tr, cols), lambda r: (0, r, 0)), blk, blk, blk]
        out_specs += [blk] * 4
        out_shape += [jax.ShapeDtypeStruct((rows, cols), F32)] * 4
        operands += [t, w, m, v]
    res = pl.pallas_call(
        body, name=name, grid=(ADAMW_STEPS,), in_specs=in_specs, out_specs=out_specs, out_shape=out_shape,
        compiler_params=pltpu.CompilerParams(dimension_semantics=("arbitrary",), vmem_limit_bytes=VMEM_LIMIT),
    )(*operands)
    return [res[4 * j:4 * j + 4] for j in range(n)]


SMALL_ROWS = ("loss", "pool_scale", "sgu_ln_g", "sgu_ln_b", "norm1_post_g", "norm2_pre_g", "norm2_post_g")


N_ROW_PARAMS = len(SMALL_ROWS) - 1


def _adamw_small(u_rows, u_others, params):
    n, n_oth = len(params), len(u_others)

    def body(*refs):
        urow_ref, others = refs[0], refs[1:1 + n_oth]
        wmv = refs[1 + n_oth:1 + n_oth + 3 * n]
        loss_ref = refs[1 + n_oth + 3 * n]
        outs = refs[2 + n_oth + 3 * n:]

        def total(ref, idx):
            g = ref[(0,) + idx].astype(F32)
            for q in range(1, N_CHIPS):
                g = g + ref[(q,) + idx].astype(F32)
            return g

        loss_ref[...] = total(urow_ref, (slice(0, 1), slice(None)))
        for p in range(n):
            if p < N_ROW_PARAMS:
                g = total(urow_ref, (slice(p + 1, p + 2), slice(None)))
            else:
                g = total(others[p - N_ROW_PARAMS], (slice(None), slice(None)))
            d, mn, vn = _adamw_math(wmv[3 * p][...], g, wmv[3 * p + 1][...], wmv[3 * p + 2][...])
            outs[4 * p][...] = g
            outs[4 * p + 1][...] = d
            outs[4 * p + 2][...] = mn
            outs[4 * p + 3][...] = vn

    flat = [a for p in params for a in p]
    out_shape = [jax.ShapeDtypeStruct((1, D_MODEL), F32)]
    for w, _, _ in params:
        out_shape += [jax.ShapeDtypeStruct(w.shape, F32)] * 4
    return pl.pallas_call(body, name="adamw_small", out_shape=out_shape)(u_rows, *u_others, *flat)


def _to_bf16(arrays):
    n = len(arrays)

    def body(*refs):
        for j in range(n):
            refs[n + j][...] = refs[j][...].astype(BF16)

    return pl.pallas_call(
        body, name="shards_to_bf16", out_shape=[jax.ShapeDtypeStruct(a.shape, BF16) for a in arrays],
    )(*arrays)


def kernel(x, norm1_pre_g, w_in, b_in, w_pool, pool_scale, sgu_ln_g, sgu_ln_b, w_spatial, b_spatial, w_sgu_proj, w_out, norm1_post_g, norm2_pre_g, w_ff1, w_ff2, norm2_post_g, loss_target, m_norm1_pre_g, m_w_in, m_b_in, m_w_pool, m_pool_scale, m_sgu_ln_g, m_sgu_ln_b, m_w_spatial, m_b_spatial, m_w_sgu_proj, m_w_out, m_norm1_post_g, m_norm2_pre_g, m_w_ff1, m_w_ff2, m_norm2_post_g, v_norm1_pre_g, v_w_in, v_b_in, v_w_pool, v_pool_scale, v_sgu_ln_g, v_sgu_ln_b, v_w_spatial, v_b_spatial, v_w_sgu_proj, v_w_out, v_norm1_post_g, v_norm2_pre_g, v_w_ff1, v_w_ff2, v_norm2_post_g):
    t_len = x.shape[1]
    row = lambda a: a.reshape(1, -1)
    x2 = x.reshape(t_len, D_MODEL)
    tgt2 = loss_target.reshape(t_len, D_MODEL)
    pg2 = lambda a: a.reshape(N_GROUPS * PG_SHARD, GROUP)

    shards_b = _to_bf16([w_in, w_pool, w_sgu_proj, w_out, w_ff1, w_ff2])

    def prework(token):
        return _prenorm(token, x2, row(norm1_pre_g))

    (win_f, wpool_f, wproj_f, g_out), (xn, xn_b) = _gather_behind("mix", shards_b[:4], None, prework,
                                                                  by_columns=(0,), by_rows=(1, 2))
    wout_f = g_out.reshape(D_MODEL, D_MODEL)
    bsp_t = b_spatial.T

    def forward(token):
        z, y, h1, *saved = _fwd_mix(token, xn, x2, win_f, row(b_in), wpool_f, row(pool_scale),
                                    row(sgu_ln_g), row(sgu_ln_b), w_spatial, bsp_t, wproj_f, wout_f,
                                    row(norm1_post_g))
        return h1, z, y, saved

    (g_ff1, g_ff2), (h1, z, y, saved) = _gather_behind("ff", shards_b[4:], win_f,
                                                       forward, by_columns=(0,))
    w2_f = g_ff2.reshape(D_FF, D_MODEL)
    hn_b, f_b, df1_b, df2_b, dh1, dg2post, dg2pre, loss_p = _mlp(h1, tgt2, row(norm2_pre_g), row(norm2_post_g),
                                                               g_ff1, w2_f)
    c_arr = lax.axis_index("c").astype(jnp.int32).reshape(1)
    chip_ff2, chip_ff1 = _wgrad_ff(c_arr, f_b, df2_b, hn_b, df1_b)
    chip_ff = [chip_ff1, chip_ff2]

    def backward_mix(token):
        return _bwd_mix(token, dh1, y, z, saved, wpool_f, row(pool_scale), row(sgu_ln_g), row(sgu_ln_b), w_spatial,
                        bsp_t, wproj_f, wout_f, row(norm1_post_g))

    summed_ff, (dz_b, mg_b, dy_b, p_pool, p_proj, dws, dbsp_t, dg1post, dps, dlng, dlnb,
                dbin) = _chip_exchange_behind("ff", chip_ff, [], backward_mix)

    rows = jnp.concatenate([jnp.broadcast_to(loss_p[:, 0:1], (1, D_MODEL)), dps, dlng, dlnb, dg1post, dg2pre, dg2post],
                           axis=0)
    smalls = [rows, dbin, dws.reshape(N_HEADS * SGU_BLOCK, SGU_BLOCK), dbsp_t.T]
    sums = _wgrad_in(xn_b, dz_b, [p_pool, p_proj], smalls)
    chip_bigs, chip_smalls = sums[:3], sums[3:]

    def backward_rest(token):
        dx, dg1pre = _bwd_in(token, dz_b, x2, dh1, row(norm1_pre_g), win_f)
        chip_out, chip_g1pre = _wgrad_out(mg_b, dy_b, [dg1pre])
        upd_ff = _adamw_big("adamw_ff", chip_out, summed_ff, [(w_ff1, m_w_ff1, v_w_ff1), (w_ff2, m_w_ff2, v_w_ff2)])
        return upd_ff[1][0], dx, chip_out, chip_g1pre, upd_ff

    summed_in, (_, dx, chip_out, chip_g1pre, upd_ff) = _chip_exchange_behind("in", chip_bigs, chip_smalls,
                                                                             backward_rest)

    def update_in(token):
        res = _adamw_big("adamw_in", token, summed_in[:1], [(w_in, m_w_in, v_w_in)])[0]
        return res[0], res

    summed_out, (_, upd_in) = _chip_exchange_behind("out", [chip_out], [chip_g1pre], update_in)
    big = {"in": upd_in, "ff1": upd_ff[0], "ff2": upd_ff[1]}
    ws2 = lambda a: a.reshape(N_HEADS * SGU_BLOCK, SGU_BLOCK)
    small_params = [(row(pool_scale), row(m_pool_scale), row(v_pool_scale)),
                    (row(sgu_ln_g), row(m_sgu_ln_g), row(v_sgu_ln_g)),
                    (row(sgu_ln_b), row(m_sgu_ln_b), row(v_sgu_ln_b)),
                    (row(norm1_post_g), row(m_norm1_post_g), row(v_norm1_post_g)),
                    (row(norm2_pre_g), row(m_norm2_pre_g), row(v_norm2_pre_g)),
                    (row(norm2_post_g), row(m_norm2_post_g), row(v_norm2_post_g)),
                    (row(norm1_pre_g), row(m_norm1_pre_g), row(v_norm1_pre_g)),
                    (row(b_in), row(m_b_in), row(v_b_in)),
                    (ws2(w_spatial), ws2(m_w_spatial), ws2(v_w_spatial)),
                    (b_spatial, m_b_spatial, v_b_spatial),
                    (pg2(w_pool), pg2(m_w_pool), pg2(v_w_pool)),
                    (pg2(w_sgu_proj), pg2(m_w_sgu_proj), pg2(v_w_sgu_proj)),
                    (w_out, m_w_out, v_w_out)]
    small_out = _adamw_small(summed_in[3], [summed_out[1]] + list(summed_in[4:])
                             + [summed_in[1], summed_in[2], summed_out[0]], small_params)
    loss = small_out[0][0, 0]
    small_names = SMALL_ROWS[1:] + ("norm1_pre_g", "b_in", "w_spatial", "b_spatial", "w_pool", "w_sgu_proj", "w_out")
    small = {nm: small_out[1 + 4 * p:5 + 4 * p] for p, nm in enumerate(small_names)}

    shapes = {"norm1_pre_g": norm1_pre_g.shape, "w_in": w_in.shape, "b_in": b_in.shape, "w_pool": w_pool.shape,
              "pool_scale": pool_scale.shape, "sgu_ln_g": sgu_ln_g.shape, "sgu_ln_b": sgu_ln_b.shape,
              "w_spatial": w_spatial.shape, "b_spatial": b_spatial.shape, "w_sgu_proj": w_sgu_proj.shape,
              "w_out": w_out.shape, "norm1_post_g": norm1_post_g.shape, "norm2_pre_g": norm2_pre_g.shape,
              "w_ff1": w_ff1.shape, "w_ff2": w_ff2.shape, "norm2_post_g": norm2_post_g.shape}
    source = {"w_in": big["in"], "w_ff1": big["ff1"], "w_ff2": big["ff2"], **small}
    order = list(shapes)
    outs = [loss, dx.reshape(x.shape)]
    for kind in range(4):
        outs += [source[nm][kind].reshape(shapes[nm]) for nm in order]
    return tuple(outs)
```

```python
import math

import jax
import jax.numpy as jnp
from jax import lax
from jax.experimental import pallas as pl
from jax.experimental.pallas import tpu as pltpu

F32, BF16 = jnp.float32, jnp.bfloat16
MESH = pl.DeviceIdType.MESH

D_MODEL = 1024
D_IN = 5120
D_FF = 4096
N_DEV = 8
N_CHIPS = 4
WINDOWS = (2, 4, 8, 16)
N_GROUPS = 4
GROUP = 256
HALO = 16
SGU_BLOCK = 128
N_HEADS = 4
HEAD = 256
CHUNK = 64
EPS = 1e-6
IN_SHARD = D_IN // N_DEV
FF_SHARD = D_FF // N_DEV
OUT_SHARD = D_MODEL // N_DEV
PG_SHARD = GROUP // N_DEV

ADAM_LR, ADAM_B1, ADAM_B2, ADAM_EPS, ADAM_WD, ADAM_STEP = 0.001, 0.9, 0.999, 1e-08, 0.01, 10

VMEM_LIMIT = 56 * 1024 * 1024
TM = 256
TM_BWD = 256
TM_IN = 512
PROJ_COLS = 512
GELU_C0 = math.sqrt(2.0 / math.pi)
GELU_C1 = 0.044715


def _dot(a, b):
    return jnp.dot(a, b, preferred_element_type=F32)


def _dot_nt(a, b):
    return lax.dot_general(a, b, (((1,), (1,)), ((), ())), preferred_element_type=F32)


def _dot_tn(a, b):
    return lax.dot_general(a, b, (((0,), (0,)), ((), ())), preferred_element_type=F32)


def _gelu(x):
    t = jnp.tanh(GELU_C0 * (x + GELU_C1 * (x * x * x)))
    return 0.5 * x * (1.0 + t), t


def _gelu_grad(x, t):
    return 0.5 * (1.0 + t) + 0.5 * x * (1.0 - t * t) * (GELU_C0 * (1.0 + 3.0 * GELU_C1 * (x * x)))


def _sigmoid(x):
    return 1.0 / (1.0 + jnp.exp(-x))


def _mean(x):
    return jnp.mean(x, axis=-1, keepdims=True)


def _colsum(x):
    return jnp.sum(x, axis=0, keepdims=True)


def _const_spec(shape):
    nd = len(shape)
    return pl.BlockSpec(shape, lambda *_: (0,) * nd, pipeline_mode=pl.Buffered(1))


def _acc_spec(shape):
    nd = len(shape)
    return pl.BlockSpec(shape, lambda *_: (0,) * nd)


def _masked_ws(ws_ref):
    ri = lax.broadcasted_iota(jnp.int32, (SGU_BLOCK, SGU_BLOCK), 0) // CHUNK
    ci = lax.broadcasted_iota(jnp.int32, (SGU_BLOCK, SGU_BLOCK), 1) // CHUNK
    return [jnp.where(ri >= ci, ws_ref[h], 0.0).astype(BF16) for h in range(N_HEADS)]


def _pool_fwd(pbuf, tile_idx, tm):
    pos = lax.broadcasted_iota(jnp.int32, (tm, 1), 0) + tile_idx * tm + 1
    pooled = []
    for g, w in enumerate(WINDOWS):
        e = pbuf[:, g * GROUP:(g + 1) * GROUP]
        s, sh = e, 1
        while sh < w:
            s = s + pltpu.roll(s, sh, 0)
            sh *= 2
        inv = 1.0 / jnp.minimum(pos, w).astype(F32)
        pooled.append(s[HALO:] * inv - e[HALO:])
    return pooled


def _spatial_head(ws_h, vb, bsp_ref, h, nblk):
    return jnp.concatenate(
        [_dot(ws_h, vb[n * SGU_BLOCK:(n + 1) * SGU_BLOCK, h * HEAD:(h + 1) * HEAD]) + bsp_ref[:, h:h + 1]
         for n in range(nblk)], axis=0)


def _prenorm(after, x, g1pre):
    t_len = x.shape[0]
    tm = TM_IN

    def body(after_ref, x_ref, g_ref, xn_ref, xnt_ref):
        del after_ref
        xv = x_ref[...]
        xnb = (xv * lax.rsqrt(_mean(xv * xv) + EPS) * g_ref[...]).astype(BF16)
        xn_ref[...] = xnb
        xnt_ref[...] = xnb.T

    return pl.pallas_call(
        body, name="prenorm", grid=(t_len // tm,),
        in_specs=[_ANY, pl.BlockSpec((tm, D_MODEL), lambda i: (i, 0)), _const_spec((1, D_MODEL))],
        out_specs=[pl.BlockSpec((tm, D_MODEL), lambda i: (i, 0)), pl.BlockSpec((D_MODEL, tm), lambda i: (0, i))],
        out_shape=[jax.ShapeDtypeStruct((t_len, D_MODEL), BF16), jax.ShapeDtypeStruct((D_MODEL, t_len), BF16)],
        compiler_params=pltpu.CompilerParams(dimension_semantics=("arbitrary",)),
    )(after, x, g1pre)


def _fwd_mix(after, xn, x, win_g, b_in, wpool, pool_scale, lng, lnb, ws, bsp_t, wproj, wout, g1post):
    t_len = x.shape[0]
    tm = TM
    nt = t_len // tm

    def body(after_ref, xn_ref, xb_ref, win_ref, bin_ref, wpool_ref, ps_ref, lng_ref, lnb_ref, ws_ref, bsp_ref,
             wproj_ref, wout_ref, g1post_ref, z_ref, y_ref, h1_ref, u_ref, gpu_ref, xhat_ref, gpv_ref, sa_ref,
             sb_ref, zcur, znext, pbuf):
        del after_ref
        s = pl.program_id(0)

        @pl.when(s == 0)
        def _():
            znext[...] = jnp.zeros((tm, D_IN), F32)
            pbuf[...] = jnp.zeros((tm + HALO, D_MODEL), F32)

        zcur[...] = znext[...]
        xnb = xn_ref[...]

        def project(p):
            cols = slice(p * PROJ_COLS, (p + 1) * PROJ_COLS)
            zp = _dot(xnb, win_ref[:, cols]) + bin_ref[:, cols]
            if (p + 1) * PROJ_COLS <= D_MODEL:
                z_ref[:, cols] = zp
            znext[:, cols] = zp

        project(0)
        pbuf[0:HALO, :] = jnp.where(s <= 1, 0.0, pbuf[0:HALO, :])
        pbuf[HALO:, :] = zcur[:, 0:D_MODEL]
        pooled = _pool_fwd(pbuf, jnp.maximum(s - 1, 0), tm)
        pbuf[0:HALO, :] = pbuf[tm:tm + HALO, :]
        a = jnp.concatenate([_dot(pooled[g].astype(BF16), wpool_ref[g]) for g in range(N_GROUPS)], axis=1)
        a = a * ps_ref[...]
        project(1)
        zu = zcur[:, D_MODEL:2 * D_MODEL]
        u, tu = _gelu(zu)
        u_ref[...] = u.astype(BF16)
        gpu_ref[...] = _gelu_grad(zu, tu).astype(BF16)
        project(2)
        zv = zcur[:, 2 * D_MODEL:3 * D_MODEL]
        gv, tv = _gelu(zv)
        xc = gv - _mean(gv)
        rln = lax.rsqrt(_mean(xc * xc) + EPS)
        xhat = xc * rln
        xhat_ref[...] = xhat.astype(BF16)
        gpv_ref[...] = (_gelu_grad(zv, tv) * rln).astype(BF16)
        vb = (xhat * lng_ref[...] + lnb_ref[...]).astype(BF16)
        wsm = _masked_ws(ws_ref)
        bbr = []
        for h in range(N_HEADS):
            project(3 + h)
            sv = _spatial_head(wsm[h], vb, bsp_ref, h, tm // SGU_BLOCK)
            bbr.append(_dot((u[:, h * HEAD:(h + 1) * HEAD] * sv).astype(BF16), wproj_ref[h]))
        bbr = jnp.concatenate(bbr, axis=1)
        project(7)
        sa = _sigmoid(zcur[:, 3 * D_MODEL:4 * D_MODEL])
        sb = _sigmoid(zcur[:, 4 * D_MODEL:5 * D_MODEL])
        sa_ref[...] = sa.astype(BF16)
        sb_ref[...] = sb.astype(BF16)
        project(8)
        yv = _dot((sa * a + sb * bbr).astype(BF16), wout_ref[...])
        y_ref[...] = yv
        project(9)
        ry = lax.rsqrt(_mean(yv * yv) + EPS)
        h1_ref[...] = xb_ref[...] + yv * ry * g1post_ref[...]

    proj = lambda w: pl.BlockSpec((tm, w), lambda s: (jnp.minimum(s, nt - 1), 0))
    mix = lambda w: pl.BlockSpec((tm, w), lambda s: (jnp.maximum(s - 1, 0), 0))
    return pl.pallas_call(
        body, name="fwd_mix", grid=(nt + 1,),
        in_specs=[_ANY, proj(D_MODEL), mix(D_MODEL),
                  _const_spec((D_MODEL, D_IN)),
                  _const_spec((1, D_IN)), _const_spec((N_GROUPS, GROUP, GROUP)), _const_spec((1, D_MODEL)),
                  _const_spec((1, D_MODEL)), _const_spec((1, D_MODEL)),
                  _const_spec((N_HEADS, SGU_BLOCK, SGU_BLOCK)), _const_spec((SGU_BLOCK, N_HEADS)),
                  _const_spec((N_HEADS, HEAD, HEAD)), _const_spec((D_MODEL, D_MODEL)), _const_spec((1, D_MODEL))],
        out_specs=[proj(D_MODEL), mix(D_MODEL), mix(D_MODEL)] + [mix(D_MODEL)] * 6,
        out_shape=[jax.ShapeDtypeStruct((t_len, D_MODEL), F32),
                   jax.ShapeDtypeStruct((t_len, D_MODEL), F32), jax.ShapeDtypeStruct((t_len, D_MODEL), F32)]
        + [jax.ShapeDtypeStruct((t_len, D_MODEL), BF16)] * 6,
        scratch_shapes=[pltpu.VMEM((tm, D_IN), F32), pltpu.VMEM((tm, D_IN), F32),
                        pltpu.VMEM((tm + HALO, D_MODEL), F32)],
        compiler_params=pltpu.CompilerParams(dimension_semantics=("arbitrary",), vmem_limit_bytes=VMEM_LIMIT),
    )(after, xn, x, win_g, b_in, wpool, pool_scale, lng, lnb, ws, bsp_t, wproj, wout, g1post)


def _mlp(h1, target, g2pre, g2post, w1_g, w2):
    t_len = h1.shape[0]
    tm = TM
    nt = t_len // tm

    def body(h1_ref, tgt_ref, g2pre_ref, g2post_ref, w1_ref, w2_ref,
             hn_ref, f_ref, df1_ref, df2_ref, dh1_ref, dg2post_ref, dg2pre_ref, loss_ref, f1_scr):
        i = pl.program_id(0)

        @pl.when(i == 0)
        def _():
            dg2post_ref[...] = jnp.zeros_like(dg2post_ref)
            dg2pre_ref[...] = jnp.zeros_like(dg2pre_ref)
            loss_ref[...] = jnp.zeros_like(loss_ref)

        h = h1_ref[...]
        r2 = lax.rsqrt(_mean(h * h) + EPS)
        nh = h * r2
        hnb = (nh * g2pre_ref[...]).astype(BF16)
        hn_ref[...] = hnb.T
        for k in range(N_DEV):
            f1_scr[:, k * FF_SHARD:(k + 1) * FF_SHARD] = _dot(hnb, w1_ref[:, k * FF_SHARD:(k + 1) * FF_SHARD])
        r = jnp.maximum(f1_scr[...], 0.0)
        fb = (r * r).astype(BF16)
        f_ref[...] = fb.T
        f2 = _dot(fb, w2_ref[...])
        rf = lax.rsqrt(_mean(f2 * f2) + EPS)
        nf = f2 * rf
        diff = h + nf * g2post_ref[...] - tgt_ref[...]
        loss_ref[...] += (0.5 / D_MODEL) * jnp.sum(diff * diff)
        dout = diff * (1.0 / D_MODEL)
        dg2post_ref[...] += _colsum(dout * nf)
        dn = dout * g2post_ref[...]
        df2b = (rf * (dn - nf * _mean(dn * nf))).astype(BF16)
        df2_ref[...] = df2b
        df = _dot_nt(df2b, w2_ref[...])
        df1b = (df * (2.0 * jnp.maximum(f1_scr[...], 0.0))).astype(BF16)
        df1_ref[...] = df1b
        dhn = _dot_nt(df1b, w1_ref[...])
        dg2pre_ref[...] += _colsum(dhn * nh)
        dnh = dhn * g2pre_ref[...]
        dh1_ref[...] = dout + r2 * (dnh - nh * _mean(dnh * nh))

    tok = lambda w: pl.BlockSpec((tm, w), lambda i: (i, 0))
    return pl.pallas_call(
        body, name="mlp_fwd_bwd", grid=(nt,),
        in_specs=[tok(D_MODEL), tok(D_MODEL), _const_spec((1, D_MODEL)), _const_spec((1, D_MODEL)),
                  _const_spec((D_MODEL, D_FF)), _const_spec((D_FF, D_MODEL))],
        out_specs=[pl.BlockSpec((D_MODEL, tm), lambda i: (0, i)), pl.BlockSpec((D_FF, tm), lambda i: (0, i)),
                   tok(D_FF), tok(D_MODEL), tok(D_MODEL),
                   _acc_spec((1, D_MODEL)), _acc_spec((1, D_MODEL)), _acc_spec((1, 128))],
        out_shape=[jax.ShapeDtypeStruct((D_MODEL, t_len), BF16), jax.ShapeDtypeStruct((D_FF, t_len), BF16),
                   jax.ShapeDtypeStruct((t_len, D_FF), BF16), jax.ShapeDtypeStruct((t_len, D_MODEL), BF16),
                   jax.ShapeDtypeStruct((t_len, D_MODEL), F32), jax.ShapeDtypeStruct((1, D_MODEL), F32),
                   jax.ShapeDtypeStruct((1, D_MODEL), F32), jax.ShapeDtypeStruct((1, 128), F32)],
        scratch_shapes=[pltpu.VMEM((tm, D_FF), F32)],
        compiler_params=pltpu.CompilerParams(dimension_semantics=("arbitrary",), vmem_limit_bytes=VMEM_LIMIT),
    )(h1, target, g2pre, g2post, w1_g, w2)


def _bwd_mix(after, dh1, y, z, saved, wpool, pool_scale, lng, lnb, ws, bsp_t, wproj, wout, g1post):
    t_len = y.shape[0]
    tm = TM_BWD
    nt = t_len // tm
    nblk = tm // SGU_BLOCK

    def body(after_ref, dh1_ref, y_ref, z_ref, zh_ref, u_ref, gpu_ref, xhat_ref, gpv_ref, sa_ref, sb_ref,
             wpool_ref, ps_ref, lng_ref, lnb_ref, ws_ref,
             bsp_ref, wproj_ref, wout_ref, g1post_ref,
             dz_ref, mg_ref, dy_ref, ppool_ref, pproj_ref, dws_ref, dbsp_ref, dg1post_ref, dps_ref,
             dlng_ref, dlnb_ref, dbin_ref, pbuf, qbuf, dwpool_ref, dwproj_ref):
        del after_ref
        i = pl.program_id(0)
        ti = nt - 1 - i

        @pl.when(i == 0)
        def _():
            for ref in (dwpool_ref, dwproj_ref, dws_ref, dbsp_ref, dg1post_ref, dps_ref, dlng_ref, dlnb_ref,
                        dbin_ref):
                ref[...] = jnp.zeros_like(ref)
            qbuf[tm:tm + HALO, :] = jnp.zeros((HALO, D_MODEL), F32)

        pbuf[0:HALO, :] = jnp.where(ti > 0, zh_ref[...], 0.0)
        pbuf[HALO:, :] = z_ref[...]
        pooled = _pool_fwd(pbuf, ti, tm)
        pooled_b = [p.astype(BF16) for p in pooled]
        a_raw = jnp.concatenate([_dot(pooled_b[g], wpool_ref[g]) for g in range(N_GROUPS)], axis=1)
        wsm = _masked_ws(ws_ref)
        u = u_ref[...].astype(F32)
        xhat = xhat_ref[...].astype(F32)
        vb = (xhat * lng_ref[...] + lnb_ref[...]).astype(BF16)
        sv_heads = [_spatial_head(wsm[h], vb, bsp_ref, h, nblk) for h in range(N_HEADS)]
        gated_b = [(u[:, h * HEAD:(h + 1) * HEAD] * sv_heads[h]).astype(BF16) for h in range(N_HEADS)]
        bbr = jnp.concatenate([_dot(gated_b[h], wproj_ref[h]) for h in range(N_HEADS)], axis=1)
        sa = sa_ref[...].astype(F32)
        sb = sb_ref[...].astype(F32)
        a = a_raw * ps_ref[...]
        mg_ref[...] = (sa * a + sb * bbr).astype(BF16).T

        dh = dh1_ref[...]
        yv = y_ref[...]
        ry = lax.rsqrt(_mean(yv * yv) + EPS)
        ny = yv * ry
        dg1post_ref[...] += _colsum(dh * ny)
        dn = dh * g1post_ref[...]
        dyb = (ry * (dn - ny * _mean(dn * ny))).astype(BF16)
        dy_ref[...] = dyb
        dmg = _dot_nt(dyb, wout_ref[...])

        da = dmg * sa
        dbbr = dmg * sb
        dzga = dmg * a * sa * (1.0 - sa)
        dzgb = dmg * bbr * sb * (1.0 - sb)
        dz_ref[:, 3 * D_MODEL:4 * D_MODEL] = dzga.astype(BF16)
        dz_ref[:, 4 * D_MODEL:5 * D_MODEL] = dzgb.astype(BF16)
        dbin_ref[:, 3 * D_MODEL:4 * D_MODEL] += _colsum(dzga)
        dbin_ref[:, 4 * D_MODEL:5 * D_MODEL] += _colsum(dzgb)

        dps_ref[...] += _colsum(da * a_raw)
        da_raw_b = (da * ps_ref[...]).astype(BF16)
        pos = lax.broadcasted_iota(jnp.int32, (tm, 1), 0) + ti * tm + 1
        dpooled = []
        for g, w in enumerate(WINDOWS):
            cols = slice(g * GROUP, (g + 1) * GROUP)
            dwpool_ref[g] += _dot_tn(pooled_b[g], da_raw_b[:, cols])
            dp = _dot_nt(da_raw_b[:, cols], wpool_ref[g])
            dpooled.append(dp)
            qbuf[0:tm, cols] = dp * (1.0 / jnp.minimum(pos, w).astype(F32))
        n_ext = tm + HALO
        dzp = []
        for g, w in enumerate(WINDOWS):
            e = qbuf[:, g * GROUP:(g + 1) * GROUP]
            s, sh = e, 1
            while sh < w:
                s = s + pltpu.roll(s, n_ext - sh, 0)
                sh *= 2
            dzp.append(s[0:tm] - dpooled[g])
        qbuf[tm:tm + HALO, :] = qbuf[0:HALO, :]
        dzp = jnp.concatenate(dzp, axis=1)
        dz_ref[:, 0:D_MODEL] = dzp.astype(BF16)
        dbin_ref[:, 0:D_MODEL] += _colsum(dzp)

        dv_heads = []
        du_heads = []
        for h in range(N_HEADS):
            cols = slice(h * HEAD, (h + 1) * HEAD)
            dbbr_b = dbbr[:, cols].astype(BF16)
            dwproj_ref[h] += _dot_tn(gated_b[h], dbbr_b)
            dgated = _dot_nt(dbbr_b, wproj_ref[h])
            du_heads.append(dgated * sv_heads[h])
            dsv = dgated * u[:, cols]
            dsv_b = dsv.astype(BF16)
            rows = []
            for n in range(nblk):
                blk = slice(n * SGU_BLOCK, (n + 1) * SGU_BLOCK)
                rows.append(_dot_tn(wsm[h], dsv_b[blk]))
                dws_ref[h] += _dot_nt(dsv_b[blk], vb[blk, cols])
                dbsp_ref[:, h:h + 1] += jnp.sum(dsv[blk], axis=1, keepdims=True)
            dv_heads.append(jnp.concatenate(rows, axis=0))
        dzu = jnp.concatenate(du_heads, axis=1) * gpu_ref[...].astype(F32)
        dz_ref[:, D_MODEL:2 * D_MODEL] = dzu.astype(BF16)
        dbin_ref[:, D_MODEL:2 * D_MODEL] += _colsum(dzu)
        dv = jnp.concatenate(dv_heads, axis=1)
        dlng_ref[...] += _colsum(dv * xhat)
        dlnb_ref[...] += _colsum(dv)
        dxh = dv * lng_ref[...]
        dzv = (dxh - _mean(dxh) - xhat * _mean(dxh * xhat)) * gpv_ref[...].astype(F32)
        dz_ref[:, 2 * D_MODEL:3 * D_MODEL] = dzv.astype(BF16)
        dbin_ref[:, 2 * D_MODEL:3 * D_MODEL] += _colsum(dzv)

        @pl.when(i == nt - 1)
        def _():
            ri = lax.broadcasted_iota(jnp.int32, (SGU_BLOCK, SGU_BLOCK), 0) // CHUNK
            ci = lax.broadcasted_iota(jnp.int32, (SGU_BLOCK, SGU_BLOCK), 1) // CHUNK
            for h in range(N_HEADS):
                dws_ref[h] = jnp.where(ri >= ci, dws_ref[h], 0.0)
            for slot in range(N_DEV):
                own = slice(_owner_of_slot(slot) * PG_SHARD, (_owner_of_slot(slot) + 1) * PG_SHARD)
                for g in range(N_GROUPS):
                    rows = slice(g * PG_SHARD, (g + 1) * PG_SHARD)
                    ppool_ref[slot, rows, :] = dwpool_ref[g, own, :].astype(BF16)
                    pproj_ref[slot, rows, :] = dwproj_ref[g, own, :].astype(BF16)

    tok = lambda w: pl.BlockSpec((tm, w), lambda i: (nt - 1 - i, 0))
    halo = pl.BlockSpec((HALO, D_MODEL), lambda i: (jnp.maximum((nt - 1 - i) * (tm // HALO) - 1, 0), 0))
    return pl.pallas_call(
        body, name="bwd_mix", grid=(nt,),
        in_specs=[_ANY, tok(D_MODEL), tok(D_MODEL), tok(D_MODEL), halo] + [tok(D_MODEL)] * 6
        + [_const_spec((N_GROUPS, GROUP, GROUP)),
                  _const_spec((1, D_MODEL)), _const_spec((1, D_MODEL)), _const_spec((1, D_MODEL)),
                  _const_spec((N_HEADS, SGU_BLOCK, SGU_BLOCK)), _const_spec((SGU_BLOCK, N_HEADS)),
                  _const_spec((N_HEADS, HEAD, HEAD)), _const_spec((D_MODEL, D_MODEL)), _const_spec((1, D_MODEL))],
        out_specs=[tok(D_IN), pl.BlockSpec((D_MODEL, tm), lambda i: (0, nt - 1 - i)), tok(D_MODEL),
                   _acc_spec((N_DEV, N_GROUPS * PG_SHARD, GROUP)), _acc_spec((N_DEV, N_HEADS * PG_SHARD, HEAD)),
                   _acc_spec((N_HEADS, SGU_BLOCK, SGU_BLOCK)), _acc_spec((SGU_BLOCK, N_HEADS)),
                   _acc_spec((1, D_MODEL)), _acc_spec((1, D_MODEL)), _acc_spec((1, D_MODEL)), _acc_spec((1, D_MODEL)),
                   _acc_spec((1, D_IN))],
        out_shape=[jax.ShapeDtypeStruct((t_len, D_IN), BF16),
                   jax.ShapeDtypeStruct((D_MODEL, t_len), BF16), jax.ShapeDtypeStruct((t_len, D_MODEL), BF16),
                   jax.ShapeDtypeStruct((N_DEV, N_GROUPS * PG_SHARD, GROUP), BF16),
                   jax.ShapeDtypeStruct((N_DEV, N_HEADS * PG_SHARD, HEAD), BF16),
                   jax.ShapeDtypeStruct((N_HEADS, SGU_BLOCK, SGU_BLOCK), F32),
                   jax.ShapeDtypeStruct((SGU_BLOCK, N_HEADS), F32),
                   jax.ShapeDtypeStruct((1, D_MODEL), F32), jax.ShapeDtypeStruct((1, D_MODEL), F32),
                   jax.ShapeDtypeStruct((1, D_MODEL), F32), jax.ShapeDtypeStruct((1, D_MODEL), F32),
                   jax.ShapeDtypeStruct((1, D_IN), F32)],
        scratch_shapes=[pltpu.VMEM((tm + HALO, D_MODEL), F32), pltpu.VMEM((tm + HALO, D_MODEL), F32),
                        pltpu.VMEM((N_GROUPS, GROUP, GROUP), F32), pltpu.VMEM((N_HEADS, HEAD, HEAD), F32)],
        compiler_params=pltpu.CompilerParams(dimension_semantics=("arbitrary",), vmem_limit_bytes=VMEM_LIMIT),
    )(after, dh1, y, z, z, *saved, wpool, pool_scale, lng, lnb, ws, bsp_t, wproj, wout, g1post)


def _bwd_in(after, dz_b, x, dh1, g1pre, win_g):
    t_len = x.shape[0]
    tm = TM_IN
    nt = t_len // tm

    def body(after_ref, dz_ref, x_ref, dh1_ref, g1_ref, win_ref, dx_ref, dg1pre_ref):
        del after_ref

        @pl.when(pl.program_id(0) == 0)
        def _():
            dg1pre_ref[...] = jnp.zeros_like(dg1pre_ref)

        dxn = _dot_nt(dz_ref[...], win_ref[...])
        xv = x_ref[...]
        r1 = lax.rsqrt(_mean(xv * xv) + EPS)
        nx = xv * r1
        dg1pre_ref[...] += _colsum(dxn * nx)
        dnx = dxn * g1_ref[...]
        dx_ref[...] = r1 * (dnx - nx * _mean(dnx * nx)) + dh1_ref[...]

    tok = lambda w: pl.BlockSpec((tm, w), lambda i: (i, 0))
    return pl.pallas_call(
        body, name="bwd_in", grid=(nt,),
        in_specs=[_ANY, tok(D_IN), tok(D_MODEL), tok(D_MODEL), _const_spec((1, D_MODEL)),
                  _const_spec((D_MODEL, D_IN))],
        out_specs=[tok(D_MODEL), _acc_spec((1, D_MODEL))],
        out_shape=[jax.ShapeDtypeStruct((t_len, D_MODEL), F32), jax.ShapeDtypeStruct((1, D_MODEL), F32)],
        compiler_params=pltpu.CompilerParams(dimension_semantics=("arbitrary",), vmem_limit_bytes=VMEM_LIMIT),
    )(after, dz_b, x, dh1, g1pre, win_g)


def _owner_of_slot(s):
    return 4 * ((s // 2) % 2) + 2 * (s % 2) + s // 4


def _wgrad_ff(c_arr, f_t, df2_b, hn_t, df1_b):
    t_len = df2_b.shape[0]
    half_shard = FF_SHARD // 2

    def body(c_ref, f_ref, df2_ref, hn_ref, df1_ref, o2_ref, o1_ref, buf2, buf1, land2, land1, send_sems, recv_sems):
        del c_ref
        s = pl.program_id(0)
        k, h = (s // 2) % N_CHIPS, s % 2
        x, y, c = _coords()

        def copies(k):
            return [pltpu.make_async_remote_copy(src_ref=buf.at[k], dst_ref=land.at[k], send_sem=send_sems.at[j, k],
                                                 recv_sem=recv_sems.at[j, k], device_id=(x, y, 1 - c),
                                                 device_id_type=MESH)
                    for j, (buf, land) in enumerate(((buf2, land2), (buf1, land1)))]

        @pl.when(s < N_DEV)
        def _():
            buf2[k, h] = _dot(f_ref[...], df2_ref[...]).astype(BF16)
            buf1[k, h] = _dot(hn_ref[...], df1_ref[...]).astype(BF16)

            @pl.when(h == 1)
            def _():
                for cp in copies(k):
                    cp.start()

        @pl.when(s >= N_DEV)
        def _():
            @pl.when(h == 0)
            def _():
                for cp in copies(k):
                    cp.wait_recv()

            o2_ref[...] = (_dot(f_ref[...], df2_ref[...]) + land2[k, h].astype(F32)).astype(BF16)
            o1_ref[...] = (_dot(hn_ref[...], df1_ref[...]) + land1[k, h].astype(F32)).astype(BF16)

        @pl.when(s == 2 * N_DEV - 1)
        def _():
            for k_sent in range(N_CHIPS):
                for cp in copies(k_sent):
                    cp.wait_send()

    def part(s, c_ref):
        owner = 2 * ((s // 2) % N_CHIPS) + jnp.where(s < N_DEV, 1 - c_ref[0], c_ref[0])
        return 2 * owner + s % 2

    mine_slot = lambda s: jnp.maximum(s // 2 - N_CHIPS, 0)
    mine_half = lambda s: jnp.where(s >= N_DEV, s % 2, 0)
    return pl.pallas_call(
        body, name="wgrad_ff",
        grid_spec=pltpu.PrefetchScalarGridSpec(
            num_scalar_prefetch=1, grid=(2 * N_DEV,),
            in_specs=[pl.BlockSpec((half_shard, t_len), lambda s, c_ref: (part(s, c_ref), 0)),
                      pl.BlockSpec((t_len, D_MODEL), lambda s, c_ref: (0, 0), pipeline_mode=pl.Buffered(1)),
                      pl.BlockSpec((D_MODEL, t_len), lambda s, c_ref: (0, 0), pipeline_mode=pl.Buffered(1)),
                      pl.BlockSpec((t_len, half_shard), lambda s, c_ref: (0, part(s, c_ref)))],
            out_specs=[pl.BlockSpec((None, half_shard, D_MODEL), lambda s, c_ref: (mine_slot(s), mine_half(s), 0)),
                       pl.BlockSpec((None, D_MODEL, half_shard), lambda s, c_ref: (mine_slot(s), 0, mine_half(s)))],
            scratch_shapes=[pltpu.VMEM((N_CHIPS, 2, half_shard, D_MODEL), BF16),
                            pltpu.VMEM((N_CHIPS, 2, D_MODEL, half_shard), BF16)] * 2
            + [pltpu.SemaphoreType.DMA((2, N_CHIPS))] * 2),
        out_shape=[jax.ShapeDtypeStruct((N_CHIPS, FF_SHARD, D_MODEL), BF16),
                   jax.ShapeDtypeStruct((N_CHIPS, D_MODEL, FF_SHARD), BF16)],
        compiler_params=pltpu.CompilerParams(dimension_semantics=("arbitrary",), vmem_limit_bytes=VMEM_LIMIT),
    )(c_arr, f_t, df2_b, hn_t, df1_b)


def _wgrad_pair(name, a, b, in_specs, halves, slot_shape, bigs, smalls):
    nb, nx = len(bigs), len(bigs) + len(smalls)

    def body(a_ref, b_ref, *refs):
        extra_in, out_ref, extra_out = refs[:nx], refs[nx], refs[nx + 1:2 * nx + 1]
        both, land = refs[2 * nx + 1:2 * nx + 3]
        extra_land = refs[2 * nx + 3:3 * nx + 3]
        send_sems, recv_sems, extra_send, extra_recv = refs[3 * nx + 3:]
        q = pl.program_id(0)
        x, y, c = _coords()
        to_sibling = dict(device_id=(x, y, 1 - c), device_id_type=MESH)

        def slot_copy(k):
            return pltpu.make_async_remote_copy(src_ref=both.at[k % 2, 1 - c], dst_ref=land.at[k],
                                                send_sem=send_sems.at[k], recv_sem=recv_sems.at[k], **to_sibling)

        def extra_copies():
            return [pltpu.make_async_remote_copy(
                src_ref=extra_in[j].at[pl.ds(N_CHIPS * (1 - c), N_CHIPS)] if j < nb else extra_in[j],
                dst_ref=extra_land[j], send_sem=extra_send.at[j], recv_sem=extra_recv.at[j], **to_sibling)
                for j in range(nx)]

        @pl.when(q == 0)
        def _():
            for cp in extra_copies():
                cp.start()

        @pl.when(q < N_CHIPS)
        def _():
            @pl.when(q >= 2)
            def _():
                slot_copy(q - 2).wait_send()

            half0, half1 = halves(_dot(a_ref[...], b_ref[...]))
            both[q % 2, 0] = half0.astype(BF16)
            both[q % 2, 1] = half1.astype(BF16)
            slot_copy(q).start()

        @pl.when(q >= 1)
        def _():
            k = q - 1
            slot_copy(k).wait_recv()
            out_ref[...] = (both[k % 2, c].astype(F32) + land[k].astype(F32)).astype(BF16)

        @pl.when(q == N_CHIPS)
        def _():
            slot_copy(N_CHIPS - 2).wait_send()
            slot_copy(N_CHIPS - 1).wait_send()
            for cp in extra_copies():
                cp.wait()
            for j in range(nx):
                if j < nb:
                    mine = extra_in[j][pl.ds(N_CHIPS * c, N_CHIPS)]
                    extra_out[j][...] = (mine.astype(F32) + extra_land[j][...].astype(F32)).astype(BF16)
                else:
                    extra_out[j][...] = extra_in[j][...] + extra_land[j][...]

    got_shapes = [(N_CHIPS,) + t.shape[1:] for t in bigs] + [t.shape for t in smalls]
    dtypes = [t.dtype for t in bigs] + [t.dtype for t in smalls]
    return pl.pallas_call(
        body, name=name, grid=(N_CHIPS + 1,),
        in_specs=list(in_specs) + [_const_spec(t.shape) for t in list(bigs) + list(smalls)],
        out_specs=[pl.BlockSpec((None,) + slot_shape, lambda q: (jnp.maximum(q - 1, 0), 0, 0))]
        + [pl.BlockSpec(s, lambda q, nd=len(s): (0,) * nd) for s in got_shapes],
        out_shape=[jax.ShapeDtypeStruct((N_CHIPS,) + slot_shape, BF16)]
        + [jax.ShapeDtypeStruct(s, d) for s, d in zip(got_shapes, dtypes)],
        scratch_shapes=[pltpu.VMEM((2, 2) + slot_shape, BF16), pltpu.VMEM((N_CHIPS,) + slot_shape, BF16)]
        + [pltpu.VMEM(s, d) for s, d in zip(got_shapes, dtypes)]
        + [pltpu.SemaphoreType.DMA((N_CHIPS,))] * 2 + [pltpu.SemaphoreType.DMA((max(nx, 1),))] * 2,
        compiler_params=pltpu.CompilerParams(dimension_semantics=("arbitrary",), vmem_limit_bytes=VMEM_LIMIT),
    )(a, b, *bigs, *smalls)


def _wgrad_in(xn_t, dz_b, bigs, smalls):
    t_len = dz_b.shape[0]
    return _wgrad_pair("wgrad_in", xn_t, dz_b,
                       [pl.BlockSpec((D_MODEL, t_len), lambda q: (0, 0), pipeline_mode=pl.Buffered(1)),
                        pl.BlockSpec((t_len, 2 * IN_SHARD), lambda q: (0, jnp.minimum(q, N_CHIPS - 1)))],
                       lambda res: (res[:, 0:IN_SHARD], res[:, IN_SHARD:2 * IN_SHARD]), (D_MODEL, IN_SHARD),
                       bigs, smalls)


def _wgrad_out(mg_t, dy_b, smalls):
    t_len = dy_b.shape[0]
    return _wgrad_pair("wgrad_out", mg_t, dy_b,
                       [pl.BlockSpec((2 * OUT_SHARD, t_len), lambda q: (jnp.minimum(q, N_CHIPS - 1), 0)),
                        pl.BlockSpec((t_len, D_MODEL), lambda q: (0, 0), pipeline_mode=pl.Buffered(1))],
                       lambda res: (res[0:OUT_SHARD], res[OUT_SHARD:2 * OUT_SHARD]), (OUT_SHARD, D_MODEL),
                       [], smalls)


def _coords():
    return lax.axis_index("x"), lax.axis_index("y"), lax.axis_index("c")


_ANY = pl.BlockSpec(memory_space=pl.ANY)


_HBM = pl.BlockSpec(memory_space=pltpu.HBM)
_SEM = pl.BlockSpec(memory_space=pltpu.SEMAPHORE)
_VMEM = pl.BlockSpec(memory_space=pltpu.VMEM)
_EFFECT = pltpu.SideEffectType.DATAFLOW_SIDE_EFFECTING
_TOKEN = jax.ShapeDtypeStruct((8, 128), F32)


def _in_hbm(a):
    return pltpu.with_memory_space_constraint(a, pltpu.HBM)


def _split_call(name, body, n_sems_out, arrays, sems_in=(), after=None):
    na, ns = len(arrays), len(sems_in)
    has_after = after is not None

    def kernel_body(*refs):
        arr = refs[:na]
        s_in = refs[na:na + ns]
        outs = refs[na + ns + has_after:]
        body(arr, s_in, outs[:n_sems_out])
        outs[-1][...] = jnp.zeros((8, 128), F32)

    out_shape = ([pltpu.SemaphoreType.DMA(())] * n_sems_out + [pltpu.HBM(a.shape, a.dtype) for a in arrays] + [_TOKEN])
    res = pl.pallas_call(
        kernel_body, name=name, out_shape=out_shape,
        in_specs=[_HBM] * na + [_SEM] * ns + [_ANY] * has_after,
        out_specs=[_SEM] * n_sems_out + [_HBM] * na + [_VMEM],
        input_output_aliases={i: n_sems_out + i for i in range(na)},
        compiler_params=pltpu.CompilerParams(has_side_effects=_EFFECT),
    )(*[_in_hbm(a) for a in arrays], *sems_in, *([after] if has_after else []))
    return list(res[:n_sems_out]), list(res[n_sems_out:n_sems_out + na]), res[-1]


def _wait_bytes_of(ref, send_sem, recv_sem, peer, send=True, recv=True):
    cp = pltpu.make_async_remote_copy(src_ref=ref, dst_ref=ref, send_sem=send_sem, recv_sem=recv_sem,
                                      device_id=peer, device_id_type=MESH)
    if send:
        cp.wait_send()
    if recv:
        cp.wait_recv()


def _gather_behind(tag, shards, after, work, by_columns=(), by_rows=()):
    n = len(shards)

    def land_shape(j, s):
        if j in by_columns:
            return (s.shape[0], N_DEV * s.shape[1])
        if j in by_rows:
            return (s.shape[0], N_DEV * s.shape[1], s.shape[2])
        return (N_DEV,) + s.shape

    lands = [lax.empty(land_shape(j, s), s.dtype) for j, s in enumerate(shards)]

    def slots(arr, j, first, count=1):
        if j in by_columns:
            cols = shards[j].shape[1]
            return arr[n + j].at[:, pl.ds(pl.multiple_of(first * cols, 128), count * cols)]
        if j in by_rows:
            rows = shards[j].shape[1]
            return arr[n + j].at[:, pl.ds(pl.multiple_of(first * rows, 16), count * rows), :]
        return arr[n + j].at[first] if count == 1 else arr[n + j].at[pl.ds(first, count)]

    def own_slot(arr, j, sem):
        x, y, c = _coords()
        return pltpu.make_async_copy(arr[j], slots(arr, j, 4 * x + 2 * y + c), sem)

    def start(arr, _, sems):
        x, y, c = _coords()
        me = 4 * x + 2 * y + c
        for j in range(n):
            for chip in [(1 - x, y), (x, 1 - y), (1 - x, 1 - y)]:
                pltpu.make_async_remote_copy(src_ref=arr[j], dst_ref=slots(arr, j, me), send_sem=sems[j],
                                             recv_sem=sems[n + j], device_id=(*chip, c), device_id_type=MESH).start()
        for j in range(n):
            pltpu.make_async_remote_copy(src_ref=arr[j], dst_ref=slots(arr, j, me), send_sem=sems[2 * n + j],
                                         recv_sem=sems[3 * n + j], device_id=(x, y, 1 - c),
                                         device_id_type=MESH).start()
            own_slot(arr, j, sems[4 * n + j]).start()

    def middle(arr, s_in, sems):
        x, y, c = _coords()
        sibling = (x, y, 1 - c)
        for j in range(n):
            _wait_bytes_of(slots(arr, j, 0, 3), s_in[j], s_in[n + j], sibling)
            for chip in [(1 - x, y), (x, 1 - y), (1 - x, 1 - y)]:
                slot = slots(arr, j, 4 * chip[0] + 2 * chip[1] + c)
                pltpu.make_async_remote_copy(src_ref=slot, dst_ref=slot, send_sem=sems[j], recv_sem=sems[n + j],
                                             device_id=sibling, device_id_type=MESH).start()

    def finish(arr, s_in, _):
        x, y, c = _coords()
        sibling = (x, y, 1 - c)
        for j in range(n):
            _wait_bytes_of(slots(arr, j, 0, 1), s_in[j], s_in[n + j], sibling)
            own_slot(arr, j, s_in[2 * n + j]).wait()
            _wait_bytes_of(slots(arr, j, 0, 3), s_in[3 * n + j], s_in[4 * n + j], sibling)

    sems, arrays, token = _split_call("gather_%s_start" % tag, start, 5 * n, list(shards) + lands, after=after)
    result = work(token)
    fwd_sems, arrays, token = _split_call("gather_%s_middle" % tag, middle, 2 * n, arrays, sems_in=sems[:2 * n],
                                          after=result[0])
    _, arrays, _ = _split_call("gather_%s_finish" % tag, finish, 0, arrays, sems_in=sems[2 * n:] + fwd_sems,
                               after=token)
    return arrays[n:], result


def _chip_exchange_behind(tag, bigs, smalls, work):
    nb, n = len(bigs), len(bigs) + len(smalls)
    lands = [lax.empty(s.shape, s.dtype) for s in bigs] + [lax.empty((N_CHIPS,) + s.shape, s.dtype) for s in smalls]

    def own_slot(arr, j, sem):
        x, y, _ = _coords()
        q_me = 2 * x + y
        return pltpu.make_async_copy(arr[j].at[q_me] if j < nb else arr[j], arr[n + j].at[q_me], sem)

    def start(arr, _, sems):
        x, y, c = _coords()
        q_me = 2 * x + y
        for j in range(n):
            for peer in [(1 - x, y, c), (x, 1 - y, c), (1 - x, 1 - y, c)]:
                piece = arr[j].at[2 * peer[0] + peer[1]] if j < nb else arr[j]
                pltpu.make_async_remote_copy(src_ref=piece, dst_ref=arr[n + j].at[q_me],
                                             send_sem=sems[j], recv_sem=sems[n + j], device_id=peer,
                                             device_id_type=MESH).start()
            own_slot(arr, j, sems[2 * n + j]).start()

    def finish(arr, s_in, _):
        x, y, c = _coords()
        for j in range(n):
            _wait_bytes_of(arr[n + j].at[pl.ds(0, 3)], s_in[j], s_in[n + j], (x, y, 1 - c))
            own_slot(arr, j, s_in[2 * n + j]).wait()

    sems, arrays, token = _split_call("chip_exchange_%s_start" % tag, start, 3 * n, list(bigs) + list(smalls) + lands)
    result = work(token)
    _, arrays, _ = _split_call("chip_exchange_%s_finish" % tag, finish, 0, arrays, sems_in=sems, after=result[0])
    return arrays[n:], result


def _adamw_math(w, g, m, v):
    m = ADAM_B1 * m + (1.0 - ADAM_B1) * g
    v = ADAM_B2 * v + (1.0 - ADAM_B2) * (g * g)
    m_hat = m / (1.0 - ADAM_B1 ** ADAM_STEP)
    v_hat = v / (1.0 - ADAM_B2 ** ADAM_STEP)
    delta = -ADAM_LR * (m_hat / (jnp.sqrt(v_hat) + ADAM_EPS) + ADAM_WD * w)
    return delta, m, v


ADAMW_STEPS = 4


def _adamw_big(name, after, chip_sums, params):
    n = len(params)

    def body(after_ref, *refs):
        del after_ref
        outs = refs[4 * n:]
        for j in range(n):
            t_ref, w_ref, m_ref, v_ref = refs[4 * j:4 * j + 4]
            g = t_ref[0].astype(F32)
            for q in range(1, N_CHIPS):
                g = g + t_ref[q].astype(F32)
            d, mn, vn = _adamw_math(w_ref[...], g, m_ref[...], v_ref[...])
            for out, val in zip(outs[4 * j:4 * j + 4], (g, d, mn, vn)):
                out[...] = val

    in_specs, out_specs, out_shape, operands = [_ANY], [], [], [after]
    for t, (w, m, v) in zip(chip_sums, params):
        rows, cols = w.shape
        tr = rows // ADAMW_STEPS
        blk = pl.BlockSpec((tr, cols), lambda r: (r, 0))
        in_specs += [pl.BlockSpec((N_CHIPS, tr, cols), lambda r: (0, r, 0)), blk, blk, blk]
        out_specs += [blk] * 4
        out_shape += [jax.ShapeDtypeStruct((rows, cols), F32)] * 4
        operands += [t, w, m, v]
    res = pl.pallas_call(
        body, name=name, grid=(ADAMW_STEPS,), in_specs=in_specs, out_specs=out_specs, out_shape=out_shape,
        compiler_params=pltpu.CompilerParams(dimension_semantics=("arbitrary",), vmem_limit_bytes=VMEM_LIMIT),
    )(*operands)
    return [res[4 * j:4 * j + 4] for j in range(n)]


SMALL_ROWS = ("loss", "pool_scale", "sgu_ln_g", "sgu_ln_b", "norm1_post_g", "norm2_pre_g", "norm2_post_g")


N_ROW_PARAMS = len(SMALL_ROWS) - 1


def _adamw_small(u_rows, u_others, params):
    n, n_oth = len(params), len(u_others)

    def body(*refs):
        urow_ref, others = refs[0], refs[1:1 + n_oth]
        wmv = refs[1 + n_oth:1 + n_oth + 3 * n]
        loss_ref = refs[1 + n_oth + 3 * n]
        outs = refs[2 + n_oth + 3 * n:]

        def total(ref, idx):
            g = ref[(0,) + idx].astype(F32)
            for q in range(1, N_CHIPS):
                g = g + ref[(q,) + idx].astype(F32)
            return g

        loss_ref[...] = total(urow_ref, (slice(0, 1), slice(None)))
        for p in range(n):
            if p < N_ROW_PARAMS:
                g = total(urow_ref, (slice(p + 1, p + 2), slice(None)))
            else:
                g = total(others[p - N_ROW_PARAMS], (slice(None), slice(None)))
            d, mn, vn = _adamw_math(wmv[3 * p][...], g, wmv[3 * p + 1][...], wmv[3 * p + 2][...])
            outs[4 * p][...] = g
            outs[4 * p + 1][...] = d
            outs[4 * p + 2][...] = mn
            outs[4 * p + 3][...] = vn

    flat = [a for p in params for a in p]
    out_shape = [jax.ShapeDtypeStruct((1, D_MODEL), F32)]
    for w, _, _ in params:
        out_shape += [jax.ShapeDtypeStruct(w.shape, F32)] * 4
    return pl.pallas_call(body, name="adamw_small", out_shape=out_shape)(u_rows, *u_others, *flat)


def _to_bf16(arrays):
    n = len(arrays)

    def body(*refs):
        for j in range(n):
            refs[n + j][...] = refs[j][...].astype(BF16)

    return pl.pallas_call(
        body, name="shards_to_bf16", out_shape=[jax.ShapeDtypeStruct(a.shape, BF16) for a in arrays],
    )(*arrays)


def kernel(x, norm1_pre_g, w_in, b_in, w_pool, pool_scale, sgu_ln_g, sgu_ln_b, w_spatial, b_spatial, w_sgu_proj, w_out, norm1_post_g, norm2_pre_g, w_ff1, w_ff2, norm2_post_g, loss_target, m_norm1_pre_g, m_w_in, m_b_in, m_w_pool, m_pool_scale, m_sgu_ln_g, m_sgu_ln_b, m_w_spatial, m_b_spatial, m_w_sgu_proj, m_w_out, m_norm1_post_g, m_norm2_pre_g, m_w_ff1, m_w_ff2, m_norm2_post_g, v_norm1_pre_g, v_w_in, v_b_in, v_w_pool, v_pool_scale, v_sgu_ln_g, v_sgu_ln_b, v_w_spatial, v_b_spatial, v_w_sgu_proj, v_w_out, v_norm1_post_g, v_norm2_pre_g, v_w_ff1, v_w_ff2, v_norm2_post_g):
    t_len = x.shape[1]
    row = lambda a: a.reshape(1, -1)
    x2 = x.reshape(t_len, D_MODEL)
    tgt2 = loss_target.reshape(t_len, D_MODEL)
    pg2 = lambda a: a.reshape(N_GROUPS * PG_SHARD, GROUP)

    shards_b = _to_bf16([w_in, w_pool, w_sgu_proj, w_out, w_ff1, w_ff2])

    def prework(token):
        return _prenorm(token, x2, row(norm1_pre_g))

    (win_f, wpool_f, wproj_f, g_out), (xn, xn_b) = _gather_behind("mix", shards_b[:4], None, prework,
                                                                  by_columns=(0,), by_rows=(1, 2))
    wout_f = g_out.reshape(D_MODEL, D_MODEL)
    bsp_t = b_spatial.T

    def forward(token):
        z, y, h1, *saved = _fwd_mix(token, xn, x2, win_f, row(b_in), wpool_f, row(pool_scale),
                                    row(sgu_ln_g), row(sgu_ln_b), w_spatial, bsp_t, wproj_f, wout_f,
                                    row(norm1_post_g))
        return h1, z, y, saved

    (g_ff1, g_ff2), (h1, z, y, saved) = _gather_behind("ff", shards_b[4:], win_f,
                                                       forward, by_columns=(0,))
    w2_f = g_ff2.reshape(D_FF, D_MODEL)
    hn_b, f_b, df1_b, df2_b, dh1, dg2post, dg2pre, loss_p = _mlp(h1, tgt2, row(norm2_pre_g), row(norm2_post_g),
                                                               g_ff1, w2_f)
    c_arr = lax.axis_index("c").astype(jnp.int32).reshape(1)
    chip_ff2, chip_ff1 = _wgrad_ff(c_arr, f_b, df2_b, hn_b, df1_b)
    chip_ff = [chip_ff1, chip_ff2]

    def backward_mix(token):
        return _bwd_mix(token, dh1, y, z, saved, wpool_f, row(pool_scale), row(sgu_ln_g), row(sgu_ln_b), w_spatial,
                        bsp_t, wproj_f, wout_f, row(norm1_post_g))

    summed_ff, (dz_b, mg_b, dy_b, p_pool, p_proj, dws, dbsp_t, dg1post, dps, dlng, dlnb,
                dbin) = _chip_exchange_behind("ff", chip_ff, [], backward_mix)

    rows = jnp.concatenate([jnp.broadcast_to(loss_p[:, 0:1], (1, D_MODEL)), dps, dlng, dlnb, dg1post, dg2pre, dg2post],
                           axis=0)
    smalls = [rows, dbin, dws.reshape(N_HEADS * SGU_BLOCK, SGU_BLOCK), dbsp_t.T]
    sums = _wgrad_in(xn_b, dz_b, [p_pool, p_proj], smalls)
    chip_bigs, chip_smalls = sums[:3], sums[3:]

    def backward_rest(token):
        dx, dg1pre = _bwd_in(token, dz_b, x2, dh1, row(norm1_pre_g), win_f)
        chip_out, chip_g1pre = _wgrad_out(mg_b, dy_b, [dg1pre])
        return chip_out, dx, chip_g1pre

    summed_in, (chip_out, dx, chip_g1pre) = _chip_exchange_behind("in", chip_bigs, chip_smalls, backward_rest)

    def update_big(token):
        res = _adamw_big("adamw_big", token, list(summed_ff) + list(summed_in[:1]),
                         [(w_ff1, m_w_ff1, v_w_ff1), (w_ff2, m_w_ff2, v_w_ff2), (w_in, m_w_in, v_w_in)])
        return res[2][0], res

    summed_out, (_, upd_big) = _chip_exchange_behind("out", [chip_out], [chip_g1pre], update_big)
    big = {"in": upd_big[2], "ff1": upd_big[0], "ff2": upd_big[1]}
    ws2 = lambda a: a.reshape(N_HEADS * SGU_BLOCK, SGU_BLOCK)
    small_params = [(row(pool_scale), row(m_pool_scale), row(v_pool_scale)),
                    (row(sgu_ln_g), row(m_sgu_ln_g), row(v_sgu_ln_g)),
                    (row(sgu_ln_b), row(m_sgu_ln_b), row(v_sgu_ln_b)),
                    (row(norm1_post_g), row(m_norm1_post_g), row(v_norm1_post_g)),
                    (row(norm2_pre_g), row(m_norm2_pre_g), row(v_norm2_pre_g)),
                    (row(norm2_post_g), row(m_norm2_post_g), row(v_norm2_post_g)),
                    (row(norm1_pre_g), row(m_norm1_pre_g), row(v_norm1_pre_g)),
                    (row(b_in), row(m_b_in), row(v_b_in)),
                    (ws2(w_spatial), ws2(m_w_spatial), ws2(v_w_spatial)),
                    (b_spatial, m_b_spatial, v_b_spatial),
                    (pg2(w_pool), pg2(m_w_pool), pg2(v_w_pool)),
                    (pg2(w_sgu_proj), pg2(m_w_sgu_proj), pg2(v_w_sgu_proj)),
                    (w_out, m_w_out, v_w_out)]
    small_out = _adamw_small(summed_in[3], [summed_out[1]] + list(summed_in[4:])
                             + [summed_in[1], summed_in[2], summed_out[0]], small_params)
    loss = small_out[0][0, 0]
    small_names = SMALL_ROWS[1:] + ("norm1_pre_g", "b_in", "w_spatial", "b_spatial", "w_pool", "w_sgu_proj", "w_out")
    small = {nm: small_out[1 + 4 * p:5 + 4 * p] for p, nm in enumerate(small_names)}

    shapes = {"norm1_pre_g": norm1_pre_g.shape, "w_in": w_in.shape, "b_in": b_in.shape, "w_pool": w_pool.shape,
              "pool_scale": pool_scale.shape, "sgu_ln_g": sgu_ln_g.shape, "sgu_ln_b": sgu_ln_b.shape,
              "w_spatial": w_spatial.shape, "b_spatial": b_spatial.shape, "w_sgu_proj": w_sgu_proj.shape,
              "w_out": w_out.shape, "norm1_post_g": norm1_post_g.shape, "norm2_pre_g": norm2_pre_g.shape,
              "w_ff1": w_ff1.shape, "w_ff2": w_ff2.shape, "norm2_post_g": norm2_post_g.shape}
    source = {"w_in": big["in"], "w_ff1": big["ff1"], "w_ff2": big["ff2"], **small}
    order = list(shapes)
    outs = [loss, dx.reshape(x.shape)]
    for kind in range(4):
        outs += [source[nm][kind].reshape(shapes[nm]) for nm in order]
    return tuple(outs)
```

```python
import math

import jax
import jax.numpy as jnp
from jax import lax
from jax.experimental import pallas as pl
from jax.experimental.pallas import tpu as pltpu

F32, BF16 = jnp.float32, jnp.bfloat16
MESH = pl.DeviceIdType.MESH

D_MODEL = 1024
D_IN = 5120
D_FF = 4096
N_DEV = 8
N_CHIPS = 4
WINDOWS = (2, 4, 8, 16)
N_GROUPS = 4
GROUP = 256
HALO = 16
SGU_BLOCK = 128
N_HEADS = 4
HEAD = 256
CHUNK = 64
EPS = 1e-6
IN_SHARD = D_IN // N_DEV
FF_SHARD = D_FF // N_DEV
OUT_SHARD = D_MODEL // N_DEV
PG_SHARD = GROUP // N_DEV

ADAM_LR, ADAM_B1, ADAM_B2, ADAM_EPS, ADAM_WD, ADAM_STEP = 0.001, 0.9, 0.999, 1e-08, 0.01, 10

VMEM_LIMIT = 56 * 1024 * 1024
TM = 256
TM_BWD = 256
TM_IN = 512
PROJ_COLS = 512
GELU_C0 = math.sqrt(2.0 / math.pi)
GELU_C1 = 0.044715


def _dot(a, b):
    return jnp.dot(a, b, preferred_element_type=F32)


def _dot_nt(a, b):
    return lax.dot_general(a, b, (((1,), (1,)), ((), ())), preferred_element_type=F32)


def _dot_tn(a, b):
    return lax.dot_general(a, b, (((0,), (0,)), ((), ())), preferred_element_type=F32)


def _gelu(x):
    t = jnp.tanh(GELU_C0 * (x + GELU_C1 * (x * x * x)))
    return 0.5 * x * (1.0 + t), t


def _gelu_grad(x, t):
    return 0.5 * (1.0 + t) + 0.5 * x * (1.0 - t * t) * (GELU_C0 * (1.0 + 3.0 * GELU_C1 * (x * x)))


def _sigmoid(x):
    return 1.0 / (1.0 + jnp.exp(-x))


def _mean(x):
    return jnp.mean(x, axis=-1, keepdims=True)


def _colsum(x):
    return jnp.sum(x, axis=0, keepdims=True)


def _const_spec(shape):
    nd = len(shape)
    return pl.BlockSpec(shape, lambda *_: (0,) * nd, pipeline_mode=pl.Buffered(1))


def _acc_spec(shape):
    nd = len(shape)
    return pl.BlockSpec(shape, lambda *_: (0,) * nd)


def _masked_ws(ws_ref):
    ri = lax.broadcasted_iota(jnp.int32, (SGU_BLOCK, SGU_BLOCK), 0) // CHUNK
    ci = lax.broadcasted_iota(jnp.int32, (SGU_BLOCK, SGU_BLOCK), 1) // CHUNK
    return [jnp.where(ri >= ci, ws_ref[h], 0.0).astype(BF16) for h in range(N_HEADS)]


def _pool_fwd(pbuf, tile_idx, tm):
    pos = lax.broadcasted_iota(jnp.int32, (tm, 1), 0) + tile_idx * tm + 1
    pooled = []
    for g, w in enumerate(WINDOWS):
        e = pbuf[:, g * GROUP:(g + 1) * GROUP]
        s, sh = e, 1
        while sh < w:
            s = s + pltpu.roll(s, sh, 0)
            sh *= 2
        inv = 1.0 / jnp.minimum(pos, w).astype(F32)
        pooled.append(s[HALO:] * inv - e[HALO:])
    return pooled


def _spatial_head(ws_h, vb, bsp_ref, h, nblk):
    return jnp.concatenate(
        [_dot(ws_h, vb[n * SGU_BLOCK:(n + 1) * SGU_BLOCK, h * HEAD:(h + 1) * HEAD]) + bsp_ref[:, h:h + 1]
         for n in range(nblk)], axis=0)


def _prenorm(after, x, g1pre):
    t_len = x.shape[0]
    tm = TM_IN

    def body(after_ref, x_ref, g_ref, xn_ref, xnt_ref):
        del after_ref
        xv = x_ref[...]
        xnb = (xv * lax.rsqrt(_mean(xv * xv) + EPS) * g_ref[...]).astype(BF16)
        xn_ref[...] = xnb
        xnt_ref[...] = xnb.T

    return pl.pallas_call(
        body, name="prenorm", grid=(t_len // tm,),
        in_specs=[_ANY, pl.BlockSpec((tm, D_MODEL), lambda i: (i, 0)), _const_spec((1, D_MODEL))],
        out_specs=[pl.BlockSpec((tm, D_MODEL), lambda i: (i, 0)), pl.BlockSpec((D_MODEL, tm), lambda i: (0, i))],
        out_shape=[jax.ShapeDtypeStruct((t_len, D_MODEL), BF16), jax.ShapeDtypeStruct((D_MODEL, t_len), BF16)],
        compiler_params=pltpu.CompilerParams(dimension_semantics=("arbitrary",)),
    )(after, x, g1pre)


def _fwd_mix(after, xn, x, win_g, b_in, wpool, pool_scale, lng, lnb, ws, bsp_t, wproj, wout, g1post):
    t_len = x.shape[0]
    tm = TM
    nt = t_len // tm

    def body(after_ref, xn_ref, xb_ref, win_ref, bin_ref, wpool_ref, ps_ref, lng_ref, lnb_ref, ws_ref, bsp_ref,
             wproj_ref, wout_ref, g1post_ref, z_ref, y_ref, h1_ref, u_ref, gpu_ref, xhat_ref, gpv_ref, sa_ref,
             sb_ref, zcur, znext, pbuf):
        del after_ref
        s = pl.program_id(0)

        @pl.when(s == 0)
        def _():
            znext[...] = jnp.zeros((tm, D_IN), F32)
            pbuf[...] = jnp.zeros((tm + HALO, D_MODEL), F32)

        zcur[...] = znext[...]
        xnb = xn_ref[...]

        def project(p):
            cols = slice(p * PROJ_COLS, (p + 1) * PROJ_COLS)
            zp = _dot(xnb, win_ref[:, cols]) + bin_ref[:, cols]
            if (p + 1) * PROJ_COLS <= D_MODEL:
                z_ref[:, cols] = zp
            znext[:, cols] = zp

        project(0)
        pbuf[0:HALO, :] = jnp.where(s <= 1, 0.0, pbuf[0:HALO, :])
        pbuf[HALO:, :] = zcur[:, 0:D_MODEL]
        pooled = _pool_fwd(pbuf, jnp.maximum(s - 1, 0), tm)
        pbuf[0:HALO, :] = pbuf[tm:tm + HALO, :]
        a = jnp.concatenate([_dot(pooled[g].astype(BF16), wpool_ref[g]) for g in range(N_GROUPS)], axis=1)
        a = a * ps_ref[...]
        project(1)
        zu = zcur[:, D_MODEL:2 * D_MODEL]
        u, tu = _gelu(zu)
        u_ref[...] = u.astype(BF16)
        gpu_ref[...] = _gelu_grad(zu, tu).astype(BF16)
        project(2)
        zv = zcur[:, 2 * D_MODEL:3 * D_MODEL]
        gv, tv = _gelu(zv)
        xc = gv - _mean(gv)
        rln = lax.rsqrt(_mean(xc * xc) + EPS)
        xhat = xc * rln
        xhat_ref[...] = xhat.astype(BF16)
        gpv_ref[...] = (_gelu_grad(zv, tv) * rln).astype(BF16)
        vb = (xhat * lng_ref[...] + lnb_ref[...]).astype(BF16)
        wsm = _masked_ws(ws_ref)
        bbr = []
        for h in range(N_HEADS):
            project(3 + h)
            sv = _spatial_head(wsm[h], vb, bsp_ref, h, tm // SGU_BLOCK)
            bbr.append(_dot((u[:, h * HEAD:(h + 1) * HEAD] * sv).astype(BF16), wproj_ref[h]))
        bbr = jnp.concatenate(bbr, axis=1)
        project(7)
        sa = _sigmoid(zcur[:, 3 * D_MODEL:4 * D_MODEL])
        sb = _sigmoid(zcur[:, 4 * D_MODEL:5 * D_MODEL])
        sa_ref[...] = sa.astype(BF16)
        sb_ref[...] = sb.astype(BF16)
        project(8)
        yv = _dot((sa * a + sb * bbr).astype(BF16), wout_ref[...])
        y_ref[...] = yv
        project(9)
        ry = lax.rsqrt(_mean(yv * yv) + EPS)
        h1_ref[...] = xb_ref[...] + yv * ry * g1post_ref[...]

    proj = lambda w: pl.BlockSpec((tm, w), lambda s: (jnp.minimum(s, nt - 1), 0))
    mix = lambda w: pl.BlockSpec((tm, w), lambda s: (jnp.maximum(s - 1, 0), 0))
    return pl.pallas_call(
        body, name="fwd_mix", grid=(nt + 1,),
        in_specs=[_ANY, proj(D_MODEL), mix(D_MODEL),
                  _const_spec((D_MODEL, D_IN)),
                  _const_spec((1, D_IN)), _const_spec((N_GROUPS, GROUP, GROUP)), _const_spec((1, D_MODEL)),
                  _const_spec((1, D_MODEL)), _const_spec((1, D_MODEL)),
                  _const_spec((N_HEADS, SGU_BLOCK, SGU_BLOCK)), _const_spec((SGU_BLOCK, N_HEADS)),
                  _const_spec((N_HEADS, HEAD, HEAD)), _const_spec((D_MODEL, D_MODEL)), _const_spec((1, D_MODEL))],
        out_specs=[proj(D_MODEL), mix(D_MODEL), mix(D_MODEL)] + [mix(D_MODEL)] * 6,
        out_shape=[jax.ShapeDtypeStruct((t_len, D_MODEL), F32),
                   jax.ShapeDtypeStruct((t_len, D_MODEL), F32), jax.ShapeDtypeStruct((t_len, D_MODEL), F32)]
        + [jax.ShapeDtypeStruct((t_len, D_MODEL), BF16)] * 6,
        scratch_shapes=[pltpu.VMEM((tm, D_IN), F32), pltpu.VMEM((tm, D_IN), F32),
                        pltpu.VMEM((tm + HALO, D_MODEL), F32)],
        compiler_params=pltpu.CompilerParams(dimension_semantics=("arbitrary",), vmem_limit_bytes=VMEM_LIMIT),
    )(after, xn, x, win_g, b_in, wpool, pool_scale, lng, lnb, ws, bsp_t, wproj, wout, g1post)


def _mlp(h1, target, g2pre, g2post, w1_g, w2):
    t_len = h1.shape[0]
    tm = TM
    nt = t_len // tm

    def body(h1_ref, tgt_ref, g2pre_ref, g2post_ref, w1_ref, w2_ref,
             hn_ref, f_ref, df1_ref, df2_ref, dh1_ref, dg2post_ref, dg2pre_ref, loss_ref, f1_scr):
        i = pl.program_id(0)

        @pl.when(i == 0)
        def _():
            dg2post_ref[...] = jnp.zeros_like(dg2post_ref)
            dg2pre_ref[...] = jnp.zeros_like(dg2pre_ref)
            loss_ref[...] = jnp.zeros_like(loss_ref)

        h = h1_ref[...]
        r2 = lax.rsqrt(_mean(h * h) + EPS)
        nh = h * r2
        hnb = (nh * g2pre_ref[...]).astype(BF16)
        hn_ref[...] = hnb.T
        for k in range(N_DEV):
            f1_scr[:, k * FF_SHARD:(k + 1) * FF_SHARD] = _dot(hnb, w1_ref[:, k * FF_SHARD:(k + 1) * FF_SHARD])
        r = jnp.maximum(f1_scr[...], 0.0)
        fb = (r * r).astype(BF16)
        f_ref[...] = fb.T
        f2 = _dot(fb, w2_ref[...])
        rf = lax.rsqrt(_mean(f2 * f2) + EPS)
        nf = f2 * rf
        diff = h + nf * g2post_ref[...] - tgt_ref[...]
        loss_ref[...] += (0.5 / D_MODEL) * jnp.sum(diff * diff)
        dout = diff * (1.0 / D_MODEL)
        dg2post_ref[...] += _colsum(dout * nf)
        dn = dout * g2post_ref[...]
        df2b = (rf * (dn - nf * _mean(dn * nf))).astype(BF16)
        df2_ref[...] = df2b
        df = _dot_nt(df2b, w2_ref[...])
        df1b = (df * (2.0 * jnp.maximum(f1_scr[...], 0.0))).astype(BF16)
        df1_ref[...] = df1b
        dhn = _dot_nt(df1b, w1_ref[...])
        dg2pre_ref[...] += _colsum(dhn * nh)
        dnh = dhn * g2pre_ref[...]
        dh1_ref[...] = dout + r2 * (dnh - nh * _mean(dnh * nh))

    tok = lambda w: pl.BlockSpec((tm, w), lambda i: (i, 0))
    return pl.pallas_call(
        body, name="mlp_fwd_bwd", grid=(nt,),
        in_specs=[tok(D_MODEL), tok(D_MODEL), _const_spec((1, D_MODEL)), _const_spec((1, D_MODEL)),
                  _const_spec((D_MODEL, D_FF)), _const_spec((D_FF, D_MODEL))],
        out_specs=[pl.BlockSpec((D_MODEL, tm), lambda i: (0, i)), pl.BlockSpec((D_FF, tm), lambda i: (0, i)),
                   tok(D_FF), tok(D_MODEL), tok(D_MODEL),
                   _acc_spec((1, D_MODEL)), _acc_spec((1, D_MODEL)), _acc_spec((1, 128))],
        out_shape=[jax.ShapeDtypeStruct((D_MODEL, t_len), BF16), jax.ShapeDtypeStruct((D_FF, t_len), BF16),
                   jax.ShapeDtypeStruct((t_len, D_FF), BF16), jax.ShapeDtypeStruct((t_len, D_MODEL), BF16),
                   jax.ShapeDtypeStruct((t_len, D_MODEL), F32), jax.ShapeDtypeStruct((1, D_MODEL), F32),
                   jax.ShapeDtypeStruct((1, D_MODEL), F32), jax.ShapeDtypeStruct((1, 128), F32)],
        scratch_shapes=[pltpu.VMEM((tm, D_FF), F32)],
        compiler_params=pltpu.CompilerParams(dimension_semantics=("arbitrary",), vmem_limit_bytes=VMEM_LIMIT),
    )(h1, target, g2pre, g2post, w1_g, w2)


def _bwd_mix(after, dh1, y, z, saved, wpool, pool_scale, lng, lnb, ws, bsp_t, wproj, wout, g1post):
    t_len = y.shape[0]
    tm = TM_BWD
    nt = t_len // tm
    nblk = tm // SGU_BLOCK

    def body(after_ref, dh1_ref, y_ref, z_ref, zh_ref, u_ref, gpu_ref, xhat_ref, gpv_ref, sa_ref, sb_ref,
             wpool_ref, ps_ref, lng_ref, lnb_ref, ws_ref,
             bsp_ref, wproj_ref, wout_ref, g1post_ref,
             dz_ref, mg_ref, dy_ref, ppool_ref, pproj_ref, dws_ref, dbsp_ref, dg1post_ref, dps_ref,
             dlng_ref, dlnb_ref, dbin_ref, pbuf, qbuf, dwpool_ref, dwproj_ref):
        del after_ref
        i = pl.program_id(0)
        ti = nt - 1 - i

        @pl.when(i == 0)
        def _():
            for ref in (dwpool_ref, dwproj_ref, dws_ref, dbsp_ref, dg1post_ref, dps_ref, dlng_ref, dlnb_ref,
                        dbin_ref):
                ref[...] = jnp.zeros_like(ref)
            qbuf[tm:tm + HALO, :] = jnp.zeros((HALO, D_MODEL), F32)

        pbuf[0:HALO, :] = jnp.where(ti > 0, zh_ref[...], 0.0)
        pbuf[HALO:, :] = z_ref[...]
        pooled = _pool_fwd(pbuf, ti, tm)
        pooled_b = [p.astype(BF16) for p in pooled]
        a_raw = jnp.concatenate([_dot(pooled_b[g], wpool_ref[g]) for g in range(N_GROUPS)], axis=1)
        wsm = _masked_ws(ws_ref)
        u = u_ref[...].astype(F32)
        xhat = xhat_ref[...].astype(F32)
        vb = (xhat * lng_ref[...] + lnb_ref[...]).astype(BF16)
        sv_heads = [_spatial_head(wsm[h], vb, bsp_ref, h, nblk) for h in range(N_HEADS)]
        gated_b = [(u[:, h * HEAD:(h + 1) * HEAD] * sv_heads[h]).astype(BF16) for h in range(N_HEADS)]
        bbr = jnp.concatenate([_dot(gated_b[h], wproj_ref[h]) for h in range(N_HEADS)], axis=1)
        sa = sa_ref[...].astype(F32)
        sb = sb_ref[...].astype(F32)
        a = a_raw * ps_ref[...]
        mg_ref[...] = (sa * a + sb * bbr).astype(BF16).T

        dh = dh1_ref[...]
        yv = y_ref[...]
        ry = lax.rsqrt(_mean(yv * yv) + EPS)
        ny = yv * ry
        dg1post_ref[...] += _colsum(dh * ny)
        dn = dh * g1post_ref[...]
        dyb = (ry * (dn - ny * _mean(dn * ny))).astype(BF16)
        dy_ref[...] = dyb
        dmg = _dot_nt(dyb, wout_ref[...])

        da = dmg * sa
        dbbr = dmg * sb
        dzga = dmg * a * sa * (1.0 - sa)
        dzgb = dmg * bbr * sb * (1.0 - sb)
        dz_ref[:, 3 * D_MODEL:4 * D_MODEL] = dzga.astype(BF16)
        dz_ref[:, 4 * D_MODEL:5 * D_MODEL] = dzgb.astype(BF16)
        dbin_ref[:, 3 * D_MODEL:4 * D_MODEL] += _colsum(dzga)
        dbin_ref[:, 4 * D_MODEL:5 * D_MODEL] += _colsum(dzgb)

        dps_ref[...] += _colsum(da * a_raw)
        da_raw_b = (da * ps_ref[...]).astype(BF16)
        pos = lax.broadcasted_iota(jnp.int32, (tm, 1), 0) + ti * tm + 1
        dpooled = []
        for g, w in enumerate(WINDOWS):
            cols = slice(g * GROUP, (g + 1) * GROUP)
            dwpool_ref[g] += _dot_tn(pooled_b[g], da_raw_b[:, cols])
            dp = _dot_nt(da_raw_b[:, cols], wpool_ref[g])
            dpooled.append(dp)
            qbuf[0:tm, cols] = dp * (1.0 / jnp.minimum(pos, w).astype(F32))
        n_ext = tm + HALO
        dzp = []
        for g, w in enumerate(WINDOWS):
            e = qbuf[:, g * GROUP:(g + 1) * GROUP]
            s, sh = e, 1
            while sh < w:
                s = s + pltpu.roll(s, n_ext - sh, 0)
                sh *= 2
            dzp.append(s[0:tm] - dpooled[g])
        qbuf[tm:tm + HALO, :] = qbuf[0:HALO, :]
        dzp = jnp.concatenate(dzp, axis=1)
        dz_ref[:, 0:D_MODEL] = dzp.astype(BF16)
        dbin_ref[:, 0:D_MODEL] += _colsum(dzp)

        dv_heads = []
        du_heads = []
        for h in range(N_HEADS):
            cols = slice(h * HEAD, (h + 1) * HEAD)
            dbbr_b = dbbr[:, cols].astype(BF16)
            dwproj_ref[h] += _dot_tn(gated_b[h], dbbr_b)
            dgated = _dot_nt(dbbr_b, wproj_ref[h])
            du_heads.append(dgated * sv_heads[h])
            dsv = dgated * u[:, cols]
            dsv_b = dsv.astype(BF16)
            rows = []
            for n in range(nblk):
                blk = slice(n * SGU_BLOCK, (n + 1) * SGU_BLOCK)
                rows.append(_dot_tn(wsm[h], dsv_b[blk]))
                dws_ref[h] += _dot_nt(dsv_b[blk], vb[blk, cols])
                dbsp_ref[:, h:h + 1] += jnp.sum(dsv[blk], axis=1, keepdims=True)
            dv_heads.append(jnp.concatenate(rows, axis=0))
        dzu = jnp.concatenate(du_heads, axis=1) * gpu_ref[...].astype(F32)
        dz_ref[:, D_MODEL:2 * D_MODEL] = dzu.astype(BF16)
        dbin_ref[:, D_MODEL:2 * D_MODEL] += _colsum(dzu)
        dv = jnp.concatenate(dv_heads, axis=1)
        dlng_ref[...] += _colsum(dv * xhat)
        dlnb_ref[...] += _colsum(dv)
        dxh = dv * lng_ref[...]
        dzv = (dxh - _mean(dxh) - xhat * _mean(dxh * xhat)) * gpv_ref[...].astype(F32)
        dz_ref[:, 2 * D_MODEL:3 * D_MODEL] = dzv.astype(BF16)
        dbin_ref[:, 2 * D_MODEL:3 * D_MODEL] += _colsum(dzv)

        @pl.when(i == nt - 1)
        def _():
            ri = lax.broadcasted_iota(jnp.int32, (SGU_BLOCK, SGU_BLOCK), 0) // CHUNK
            ci = lax.broadcasted_iota(jnp.int32, (SGU_BLOCK, SGU_BLOCK), 1) // CHUNK
            for h in range(N_HEADS):
                dws_ref[h] = jnp.where(ri >= ci, dws_ref[h], 0.0)
            for slot in range(N_DEV):
                own = slice(_owner_of_slot(slot) * PG_SHARD, (_owner_of_slot(slot) + 1) * PG_SHARD)
                for g in range(N_GROUPS):
                    rows = slice(g * PG_SHARD, (g + 1) * PG_SHARD)
                    ppool_ref[slot, rows, :] = dwpool_ref[g, own, :].astype(BF16)
                    pproj_ref[slot, rows, :] = dwproj_ref[g, own, :].astype(BF16)

    tok = lambda w: pl.BlockSpec((tm, w), lambda i: (nt - 1 - i, 0))
    halo = pl.BlockSpec((HALO, D_MODEL), lambda i: (jnp.maximum((nt - 1 - i) * (tm // HALO) - 1, 0), 0))
    return pl.pallas_call(
        body, name="bwd_mix", grid=(nt,),
        in_specs=[_ANY, tok(D_MODEL), tok(D_MODEL), tok(D_MODEL), halo] + [tok(D_MODEL)] * 6
        + [_const_spec((N_GROUPS, GROUP, GROUP)),
                  _const_spec((1, D_MODEL)), _const_spec((1, D_MODEL)), _const_spec((1, D_MODEL)),
                  _const_spec((N_HEADS, SGU_BLOCK, SGU_BLOCK)), _const_spec((SGU_BLOCK, N_HEADS)),
                  _const_spec((N_HEADS, HEAD, HEAD)), _const_spec((D_MODEL, D_MODEL)), _const_spec((1, D_MODEL))],
        out_specs=[tok(D_IN), pl.BlockSpec((D_MODEL, tm), lambda i: (0, nt - 1 - i)), tok(D_MODEL),
                   _acc_spec((N_DEV, N_GROUPS * PG_SHARD, GROUP)), _acc_spec((N_DEV, N_HEADS * PG_SHARD, HEAD)),
                   _acc_spec((N_HEADS, SGU_BLOCK, SGU_BLOCK)), _acc_spec((SGU_BLOCK, N_HEADS)),
                   _acc_spec((1, D_MODEL)), _acc_spec((1, D_MODEL)), _acc_spec((1, D_MODEL)), _acc_spec((1, D_MODEL)),
                   _acc_spec((1, D_IN))],
        out_shape=[jax.ShapeDtypeStruct((t_len, D_IN), BF16),
                   jax.ShapeDtypeStruct((D_MODEL, t_len), BF16), jax.ShapeDtypeStruct((t_len, D_MODEL), BF16),
                   jax.ShapeDtypeStruct((N_DEV, N_GROUPS * PG_SHARD, GROUP), BF16),
                   jax.ShapeDtypeStruct((N_DEV, N_HEADS * PG_SHARD, HEAD), BF16),
                   jax.ShapeDtypeStruct((N_HEADS, SGU_BLOCK, SGU_BLOCK), F32),
                   jax.ShapeDtypeStruct((SGU_BLOCK, N_HEADS), F32),
                   jax.ShapeDtypeStruct((1, D_MODEL), F32), jax.ShapeDtypeStruct((1, D_MODEL), F32),
                   jax.ShapeDtypeStruct((1, D_MODEL), F32), jax.ShapeDtypeStruct((1, D_MODEL), F32),
                   jax.ShapeDtypeStruct((1, D_IN), F32)],
        scratch_shapes=[pltpu.VMEM((tm + HALO, D_MODEL), F32), pltpu.VMEM((tm + HALO, D_MODEL), F32),
                        pltpu.VMEM((N_GROUPS, GROUP, GROUP), F32), pltpu.VMEM((N_HEADS, HEAD, HEAD), F32)],
        compiler_params=pltpu.CompilerParams(dimension_semantics=("arbitrary",), vmem_limit_bytes=VMEM_LIMIT),
    )(after, dh1, y, z, z, *saved, wpool, pool_scale, lng, lnb, ws, bsp_t, wproj, wout, g1post)


def _bwd_in(after, dz_b, x, dh1, g1pre, win_g):
    t_len = x.shape[0]
    tm = TM_IN
    nt = t_len // tm

    def body(after_ref, dz_ref, x_ref, dh1_ref, g1_ref, win_ref, dx_ref, dg1pre_ref):
        del after_ref

        @pl.when(pl.program_id(0) == 0)
        def _():
            dg1pre_ref[...] = jnp.zeros_like(dg1pre_ref)

        dxn = _dot_nt(dz_ref[...], win_ref[...])
        xv = x_ref[...]
        r1 = lax.rsqrt(_mean(xv * xv) + EPS)
        nx = xv * r1
        dg1pre_ref[...] += _colsum(dxn * nx)
        dnx = dxn * g1_ref[...]
        dx_ref[...] = r1 * (dnx - nx * _mean(dnx * nx)) + dh1_ref[...]

    tok = lambda w: pl.BlockSpec((tm, w), lambda i: (i, 0))
    return pl.pallas_call(
        body, name="bwd_in", grid=(nt,),
        in_specs=[_ANY, tok(D_IN), tok(D_MODEL), tok(D_MODEL), _const_spec((1, D_MODEL)),
                  _const_spec((D_MODEL, D_IN))],
        out_specs=[tok(D_MODEL), _acc_spec((1, D_MODEL))],
        out_shape=[jax.ShapeDtypeStruct((t_len, D_MODEL), F32), jax.ShapeDtypeStruct((1, D_MODEL), F32)],
        compiler_params=pltpu.CompilerParams(dimension_semantics=("arbitrary",), vmem_limit_bytes=VMEM_LIMIT),
    )(after, dz_b, x, dh1, g1pre, win_g)


def _owner_of_slot(s):
    return 4 * ((s // 2) % 2) + 2 * (s % 2) + s // 4


def _wgrad_ff(c_arr, f_t, df2_b, hn_t, df1_b):
    t_len = df2_b.shape[0]
    half_shard = FF_SHARD // 2

    def body(c_ref, f_ref, df2_ref, hn_ref, df1_ref, o2_ref, o1_ref, buf2, buf1, land2, land1, send_sems, recv_sems):
        del c_ref
        s = pl.program_id(0)
        k, h = (s // 2) % N_CHIPS, s % 2
        x, y, c = _coords()

        def copies(k):
            return [pltpu.make_async_remote_copy(src_ref=buf.at[k], dst_ref=land.at[k], send_sem=send_sems.at[j, k],
                                                 recv_sem=recv_sems.at[j, k], device_id=(x, y, 1 - c),
                                                 device_id_type=MESH)
                    for j, (buf, land) in enumerate(((buf2, land2), (buf1, land1)))]

        @pl.when(s < N_DEV)
        def _():
            buf2[k, h] = _dot(f_ref[...], df2_ref[...]).astype(BF16)
            buf1[k, h] = _dot(hn_ref[...], df1_ref[...]).astype(BF16)

            @pl.when(h == 1)
            def _():
                for cp in copies(k):
                    cp.start()

        @pl.when(s >= N_DEV)
        def _():
            @pl.when(h == 0)
            def _():
                for cp in copies(k):
                    cp.wait_recv()

            o2_ref[...] = (_dot(f_ref[...], df2_ref[...]) + land2[k, h].astype(F32)).astype(BF16)
            o1_ref[...] = (_dot(hn_ref[...], df1_ref[...]) + land1[k, h].astype(F32)).astype(BF16)

        @pl.when(s == 2 * N_DEV - 1)
        def _():
            for k_sent in range(N_CHIPS):
                for cp in copies(k_sent):
                    cp.wait_send()

    def part(s, c_ref):
        owner = 2 * ((s // 2) % N_CHIPS) + jnp.where(s < N_DEV, 1 - c_ref[0], c_ref[0])
        return 2 * owner + s % 2

    mine_slot = lambda s: jnp.maximum(s // 2 - N_CHIPS, 0)
    mine_half = lambda s: jnp.where(s >= N_DEV, s % 2, 0)
    return pl.pallas_call(
        body, name="wgrad_ff",
        grid_spec=pltpu.PrefetchScalarGridSpec(
            num_scalar_prefetch=1, grid=(2 * N_DEV,),
            in_specs=[pl.BlockSpec((half_shard, t_len), lambda s, c_ref: (part(s, c_ref), 0)),
                      pl.BlockSpec((t_len, D_MODEL), lambda s, c_ref: (0, 0), pipeline_mode=pl.Buffered(1)),
                      pl.BlockSpec((D_MODEL, t_len), lambda s, c_ref: (0, 0), pipeline_mode=pl.Buffered(1)),
                      pl.BlockSpec((t_len, half_shard), lambda s, c_ref: (0, part(s, c_ref)))],
            out_specs=[pl.BlockSpec((None, half_shard, D_MODEL), lambda s, c_ref: (mine_slot(s), mine_half(s), 0)),
                       pl.BlockSpec((None, D_MODEL, half_shard), lambda s, c_ref: (mine_slot(s), 0, mine_half(s)))],
            scratch_shapes=[pltpu.VMEM((N_CHIPS, 2, half_shard, D_MODEL), BF16),
                            pltpu.VMEM((N_CHIPS, 2, D_MODEL, half_shard), BF16)] * 2
            + [pltpu.SemaphoreType.DMA((2, N_CHIPS))] * 2),
        out_shape=[jax.ShapeDtypeStruct((N_CHIPS, FF_SHARD, D_MODEL), BF16),
                   jax.ShapeDtypeStruct((N_CHIPS, D_MODEL, FF_SHARD), BF16)],
        compiler_params=pltpu.CompilerParams(dimension_semantics=("arbitrary",), vmem_limit_bytes=VMEM_LIMIT),
    )(c_arr, f_t, df2_b, hn_t, df1_b)


def _wgrad_pair(name, a, b, in_specs, halves, slot_shape, bigs, smalls):
    nb, nx = len(bigs), len(bigs) + len(smalls)

    def body(a_ref, b_ref, *refs):
        extra_in, out_ref, extra_out = refs[:nx], refs[nx], refs[nx + 1:2 * nx + 1]
        both, land = refs[2 * nx + 1:2 * nx + 3]
        extra_land = refs[2 * nx + 3:3 * nx + 3]
        send_sems, recv_sems, extra_send, extra_recv = refs[3 * nx + 3:]
        q = pl.program_id(0)
        x, y, c = _coords()
        to_sibling = dict(device_id=(x, y, 1 - c), device_id_type=MESH)

        def slot_copy(k):
            return pltpu.make_async_remote_copy(src_ref=both.at[k % 2, 1 - c], dst_ref=land.at[k],
                                                send_sem=send_sems.at[k], recv_sem=recv_sems.at[k], **to_sibling)

        def extra_copies():
            return [pltpu.make_async_remote_copy(
                src_ref=extra_in[j].at[pl.ds(N_CHIPS * (1 - c), N_CHIPS)] if j < nb else extra_in[j],
                dst_ref=extra_land[j], send_sem=extra_send.at[j], recv_sem=extra_recv.at[j], **to_sibling)
                for j in range(nx)]

        @pl.when(q == 0)
        def _():
            for cp in extra_copies():
                cp.start()

        @pl.when(q < N_CHIPS)
        def _():
            @pl.when(q >= 2)
            def _():
                slot_copy(q - 2).wait_send()

            half0, half1 = halves(_dot(a_ref[...], b_ref[...]))
            both[q % 2, 0] = half0.astype(BF16)
            both[q % 2, 1] = half1.astype(BF16)
            slot_copy(q).start()

        @pl.when(q >= 1)
        def _():
            k = q - 1
            slot_copy(k).wait_recv()
            out_ref[...] = (both[k % 2, c].astype(F32) + land[k].astype(F32)).astype(BF16)

        @pl.when(q == N_CHIPS)
        def _():
            slot_copy(N_CHIPS - 2).wait_send()
            slot_copy(N_CHIPS - 1).wait_send()
            for cp in extra_copies():
                cp.wait()
            for j in range(nx):
                if j < nb:
                    mine = extra_in[j][pl.ds(N_CHIPS * c, N_CHIPS)]
                    extra_out[j][...] = (mine.astype(F32) + extra_land[j][...].astype(F32)).astype(BF16)
                else:
                    extra_out[j][...] = extra_in[j][...] + extra_land[j][...]

    got_shapes = [(N_CHIPS,) + t.shape[1:] for t in bigs] + [t.shape for t in smalls]
    dtypes = [t.dtype for t in bigs] + [t.dtype for t in smalls]
    return pl.pallas_call(
        body, name=name, grid=(N_CHIPS + 1,),
        in_specs=list(in_specs) + [_const_spec(t.shape) for t in list(bigs) + list(smalls)],
        out_specs=[pl.BlockSpec((None,) + slot_shape, lambda q: (jnp.maximum(q - 1, 0), 0, 0))]
        + [pl.BlockSpec(s, lambda q, nd=len(s): (0,) * nd) for s in got_shapes],
        out_shape=[jax.ShapeDtypeStruct((N_CHIPS,) + slot_shape, BF16)]
        + [jax.ShapeDtypeStruct(s, d) for s, d in zip(got_shapes, dtypes)],
        scratch_shapes=[pltpu.VMEM((2, 2) + slot_shape, BF16), pltpu.VMEM((N_CHIPS,) + slot_shape, BF16)]
        + [pltpu.VMEM(s, d) for s, d in zip(got_shapes, dtypes)]
        + [pltpu.SemaphoreType.DMA((N_CHIPS,))] * 2 + [pltpu.SemaphoreType.DMA((max(nx, 1),))] * 2,
        compiler_params=pltpu.CompilerParams(dimension_semantics=("arbitrary",), vmem_limit_bytes=VMEM_LIMIT),
    )(a, b, *bigs, *smalls)


def _wgrad_in(xn_t, dz_b, bigs, smalls):
    t_len = dz_b.shape[0]
    return _wgrad_pair("wgrad_in", xn_t, dz_b,
                       [pl.BlockSpec((D_MODEL, t_len), lambda q: (0, 0), pipeline_mode=pl.Buffered(1)),
                        pl.BlockSpec((t_len, 2 * IN_SHARD), lambda q: (0, jnp.minimum(q, N_CHIPS - 1)))],
                       lambda res: (res[:, 0:IN_SHARD], res[:, IN_SHARD:2 * IN_SHARD]), (D_MODEL, IN_SHARD),
                       bigs, smalls)


def _wgrad_out(mg_t, dy_b, smalls):
    t_len = dy_b.shape[0]
    return _wgrad_pair("wgrad_out", mg_t, dy_b,
                       [pl.BlockSpec((2 * OUT_SHARD, t_len), lambda q: (jnp.minimum(q, N_CHIPS - 1), 0)),
                        pl.BlockSpec((t_len, D_MODEL), lambda q: (0, 0), pipeline_mode=pl.Buffered(1))],
                       lambda res: (res[0:OUT_SHARD], res[OUT_SHARD:2 * OUT_SHARD]), (OUT_SHARD, D_MODEL),
                       [], smalls)


def _coords():
    return lax.axis_index("x"), lax.axis_index("y"), lax.axis_index("c")


_ANY = pl.BlockSpec(memory_space=pl.ANY)


_HBM = pl.BlockSpec(memory_space=pltpu.HBM)
_SEM = pl.BlockSpec(memory_space=pltpu.SEMAPHORE)
_VMEM = pl.BlockSpec(memory_space=pltpu.VMEM)
_EFFECT = pltpu.SideEffectType.DATAFLOW_SIDE_EFFECTING
_TOKEN = jax.ShapeDtypeStruct((8, 128), F32)


def _in_hbm(a):
    return pltpu.with_memory_space_constraint(a, pltpu.HBM)


def _split_call(name, body, n_sems_out, arrays, sems_in=(), after=None):
    na, ns = len(arrays), len(sems_in)
    has_after = after is not None

    def kernel_body(*refs):
        arr = refs[:na]
        s_in = refs[na:na + ns]
        outs = refs[na + ns + has_after:]
        body(arr, s_in, outs[:n_sems_out])
        outs[-1][...] = jnp.zeros((8, 128), F32)

    out_shape = ([pltpu.SemaphoreType.DMA(())] * n_sems_out + [pltpu.HBM(a.shape, a.dtype) for a in arrays] + [_TOKEN])
    res = pl.pallas_call(
        kernel_body, name=name, out_shape=out_shape,
        in_specs=[_HBM] * na + [_SEM] * ns + [_ANY] * has_after,
        out_specs=[_SEM] * n_sems_out + [_HBM] * na + [_VMEM],
        input_output_aliases={i: n_sems_out + i for i in range(na)},
        compiler_params=pltpu.CompilerParams(has_side_effects=_EFFECT),
    )(*[_in_hbm(a) for a in arrays], *sems_in, *([after] if has_after else []))
    return list(res[:n_sems_out]), list(res[n_sems_out:n_sems_out + na]), res[-1]


def _wait_bytes_of(ref, send_sem, recv_sem, peer, send=True, recv=True):
    cp = pltpu.make_async_remote_copy(src_ref=ref, dst_ref=ref, send_sem=send_sem, recv_sem=recv_sem,
                                      device_id=peer, device_id_type=MESH)
    if send:
        cp.wait_send()
    if recv:
        cp.wait_recv()


def _gather_behind(tag, shards, after, work, by_columns=(), by_rows=()):
    n = len(shards)

    def land_shape(j, s):
        if j in by_columns:
            return (s.shape[0], N_DEV * s.shape[1])
        if j in by_rows:
            return (s.shape[0], N_DEV * s.shape[1], s.shape[2])
        return (N_DEV,) + s.shape

    lands = [lax.empty(land_shape(j, s), s.dtype) for j, s in enumerate(shards)]

    def slots(arr, j, first, count=1):
        if j in by_columns:
            cols = shards[j].shape[1]
            return arr[n + j].at[:, pl.ds(pl.multiple_of(first * cols, 128), count * cols)]
        if j in by_rows:
            rows = shards[j].shape[1]
            return arr[n + j].at[:, pl.ds(pl.multiple_of(first * rows, 16), count * rows), :]
        return arr[n + j].at[first] if count == 1 else arr[n + j].at[pl.ds(first, count)]

    def own_slot(arr, j, sem):
        x, y, c = _coords()
        return pltpu.make_async_copy(arr[j], slots(arr, j, 4 * x + 2 * y + c), sem)

    def start(arr, _, sems):
        x, y, c = _coords()
        me = 4 * x + 2 * y + c
        for j in range(n):
            for chip in [(1 - x, y), (x, 1 - y), (1 - x, 1 - y)]:
                pltpu.make_async_remote_copy(src_ref=arr[j], dst_ref=slots(arr, j, me), send_sem=sems[j],
                                             recv_sem=sems[n + j], device_id=(*chip, c), device_id_type=MESH).start()
        for j in range(n):
            pltpu.make_async_remote_copy(src_ref=arr[j], dst_ref=slots(arr, j, me), send_sem=sems[2 * n + j],
                                         recv_sem=sems[3 * n + j], device_id=(x, y, 1 - c),
                                         device_id_type=MESH).start()
            own_slot(arr, j, sems[4 * n + j]).start()

    def middle(arr, s_in, sems):
        x, y, c = _coords()
        sibling = (x, y, 1 - c)
        for j in range(n):
            _wait_bytes_of(slots(arr, j, 0, 3), s_in[j], s_in[n + j], sibling)
            for chip in [(1 - x, y), (x, 1 - y), (1 - x, 1 - y)]:
                slot = slots(arr, j, 4 * chip[0] + 2 * chip[1] + c)
                pltpu.make_async_remote_copy(src_ref=slot, dst_ref=slot, send_sem=sems[j], recv_sem=sems[n + j],
                                             device_id=sibling, device_id_type=MESH).start()

    def finish(arr, s_in, _):
        x, y, c = _coords()
        sibling = (x, y, 1 - c)
        for j in range(n):
            _wait_bytes_of(slots(arr, j, 0, 1), s_in[j], s_in[n + j], sibling)
            own_slot(arr, j, s_in[2 * n + j]).wait()
            _wait_bytes_of(slots(arr, j, 0, 3), s_in[3 * n + j], s_in[4 * n + j], sibling)

    sems, arrays, token = _split_call("gather_%s_start" % tag, start, 5 * n, list(shards) + lands, after=after)
    result = work(token)
    fwd_sems, arrays, token = _split_call("gather_%s_middle" % tag, middle, 2 * n, arrays, sems_in=sems[:2 * n],
                                          after=result[0])
    _, arrays, _ = _split_call("gather_%s_finish" % tag, finish, 0, arrays, sems_in=sems[2 * n:] + fwd_sems,
                               after=token)
    return arrays[n:], result


def _chip_exchange_behind(tag, bigs, smalls, work):
    nb, n = len(bigs), len(bigs) + len(smalls)
    lands = [lax.empty(s.shape, s.dtype) for s in bigs] + [lax.empty((N_CHIPS,) + s.shape, s.dtype) for s in smalls]

    def own_slot(arr, j, sem):
        x, y, _ = _coords()
        q_me = 2 * x + y
        return pltpu.make_async_copy(arr[j].at[q_me] if j < nb else arr[j], arr[n + j].at[q_me], sem)

    def start(arr, _, sems):
        x, y, c = _coords()
        q_me = 2 * x + y
        for j in range(n):
            for peer in [(1 - x, y, c), (x, 1 - y, c), (1 - x, 1 - y, c)]:
                piece = arr[j].at[2 * peer[0] + peer[1]] if j < nb else arr[j]
                pltpu.make_async_remote_copy(src_ref=piece, dst_ref=arr[n + j].at[q_me],
                                             send_sem=sems[j], recv_sem=sems[n + j], device_id=peer,
                                             device_id_type=MESH).start()
            own_slot(arr, j, sems[2 * n + j]).start()

    def finish(arr, s_in, _):
        x, y, c = _coords()
        for j in range(n):
            _wait_bytes_of(arr[n + j].at[pl.ds(0, 3)], s_in[j], s_in[n + j], (x, y, 1 - c))
            own_slot(arr, j, s_in[2 * n + j]).wait()

    sems, arrays, token = _split_call("chip_exchange_%s_start" % tag, start, 3 * n, list(bigs) + list(smalls) + lands)
    result = work(token)
    _, arrays, _ = _split_call("chip_exchange_%s_finish" % tag, finish, 0, arrays, sems_in=sems, after=result[0])
    return arrays[n:], result


def _adamw_math(w, g, m, v):
    m = ADAM_B1 * m + (1.0 - ADAM_B1) * g
    v = ADAM_B2 * v + (1.0 - ADAM_B2) * (g * g)
    m_hat = m / (1.0 - ADAM_B1 ** ADAM_STEP)
    v_hat = v / (1.0 - ADAM_B2 ** ADAM_STEP)
    delta = -ADAM_LR * (m_hat / (jnp.sqrt(v_hat) + ADAM_EPS) + ADAM_WD * w)
    return delta, m, v


ADAMW_STEPS = 4


def _adamw_big(name, after, chip_sums, params):
    n = len(params)

    def body(after_ref, *refs):
        del after_ref
        outs = refs[4 * n:]
        for j in range(n):
            t_ref, w_ref, m_ref, v_ref = refs[4 * j:4 * j + 4]
            g = t_ref[0].astype(F32)
            for q in range(1, N_CHIPS):
                g = g + t_ref[q].astype(F32)
            d, mn, vn = _adamw_math(w_ref[...], g, m_ref[...], v_ref[...])
            for out, val in zip(outs[4 * j:4 * j + 4], (g, d, mn, vn)):
                out[...] = val

    in_specs, out_specs, out_shape, operands = [_ANY], [], [], [after]
    for t, (w, m, v) in zip(chip_sums, params):
        rows, cols = w.shape
        tr = rows // ADAMW_STEPS
        blk = pl.BlockSpec((tr, cols), lambda r: (r, 0))
        in_specs += [pl.BlockSpec((N_CHIPS, tr, cols), lambda r: (0, r, 0)), blk, blk, blk]
        out_specs += [blk] * 4
        out_shape += [jax.ShapeDtypeStruct((rows, cols), F32)] * 4
        operands += [t, w, m, v]
    res = pl.pallas_call(
        body, name=name, grid=(ADAMW_STEPS,), in_specs=in_specs, out_specs=out_specs, out_shape=out_shape,
        compiler_params=pltpu.CompilerParams(dimension_semantics=("arbitrary",), vmem_limit_bytes=VMEM_LIMIT),
    )(*operands)
    return [res[4 * j:4 * j + 4] for j in range(n)]


SMALL_ROWS = ("loss", "pool_scale", "sgu_ln_g", "sgu_ln_b", "norm1_post_g", "norm2_pre_g", "norm2_post_g")


N_ROW_PARAMS = len(SMALL_ROWS) - 1


def _adamw_small(u_rows, u_others, params):
    n, n_oth = len(params), len(u_others)

    def body(*refs):
        urow_ref, others = refs[0], refs[1:1 + n_oth]
        wmv = refs[1 + n_oth:1 + n_oth + 3 * n]
        loss_ref = refs[1 + n_oth + 3 * n]
        outs = refs[2 + n_oth + 3 * n:]

        def total(ref, idx):
            g = ref[(0,) + idx].astype(F32)
            for q in range(1, N_CHIPS):
                g = g + ref[(q,) + idx].astype(F32)
            return g

        loss_ref[...] = total(urow_ref, (slice(0, 1), slice(None)))
        for p in range(n):
            if p < N_ROW_PARAMS:
                g = total(urow_ref, (slice(p + 1, p + 2), slice(None)))
            else:
                g = total(others[p - N_ROW_PARAMS], (slice(None), slice(None)))
            d, mn, vn = _adamw_math(wmv[3 * p][...], g, wmv[3 * p + 1][...], wmv[3 * p + 2][...])
            outs[4 * p][...] = g
            outs[4 * p + 1][...] = d
            outs[4 * p + 2][...] = mn
            outs[4 * p + 3][...] = vn

    flat = [a for p in params for a in p]
    out_shape = [jax.ShapeDtypeStruct((1, D_MODEL), F32)]
    for w, _, _ in params:
        out_shape += [jax.ShapeDtypeStruct(w.shape, F32)] * 4
    return pl.pallas_call(body, name="adamw_small", out_shape=out_shape)(u_rows, *u_others, *flat)


def _to_bf16(name, after, arrays):
    n = len(arrays)

    def body(after_ref, *refs):
        del after_ref
        for j in range(n):
            refs[n + j][...] = refs[j][...].astype(BF16)

    return pl.pallas_call(
        body, name=name, in_specs=[_ANY] + [_VMEM] * n,
        out_shape=[jax.ShapeDtypeStruct(a.shape, BF16) for a in arrays],
    )(after, *arrays)


def kernel(x, norm1_pre_g, w_in, b_in, w_pool, pool_scale, sgu_ln_g, sgu_ln_b, w_spatial, b_spatial, w_sgu_proj, w_out, norm1_post_g, norm2_pre_g, w_ff1, w_ff2, norm2_post_g, loss_target, m_norm1_pre_g, m_w_in, m_b_in, m_w_pool, m_pool_scale, m_sgu_ln_g, m_sgu_ln_b, m_w_spatial, m_b_spatial, m_w_sgu_proj, m_w_out, m_norm1_post_g, m_norm2_pre_g, m_w_ff1, m_w_ff2, m_norm2_post_g, v_norm1_pre_g, v_w_in, v_b_in, v_w_pool, v_pool_scale, v_sgu_ln_g, v_sgu_ln_b, v_w_spatial, v_b_spatial, v_w_sgu_proj, v_w_out, v_norm1_post_g, v_norm2_pre_g, v_w_ff1, v_w_ff2, v_norm2_post_g):
    t_len = x.shape[1]
    row = lambda a: a.reshape(1, -1)
    x2 = x.reshape(t_len, D_MODEL)
    tgt2 = loss_target.reshape(t_len, D_MODEL)
    pg2 = lambda a: a.reshape(N_GROUPS * PG_SHARD, GROUP)

    mix_shards_b = _to_bf16("mix_shards_to_bf16", x2, [w_in, w_pool, w_sgu_proj, w_out])

    def prework(token):
        xn, xn_b = _prenorm(token, x2, row(norm1_pre_g))
        ff_b = _to_bf16("ff_shards_to_bf16", xn, [w_ff1, w_ff2])
        return ff_b[0], xn, xn_b, ff_b

    (win_f, wpool_f, wproj_f, g_out), (_, xn, xn_b, ff_shards_b) = _gather_behind(
        "mix", mix_shards_b, None, prework, by_columns=(0,), by_rows=(1, 2))
    wout_f = g_out.reshape(D_MODEL, D_MODEL)
    bsp_t = b_spatial.T

    def forward(token):
        z, y, h1, *saved = _fwd_mix(token, xn, x2, win_f, row(b_in), wpool_f, row(pool_scale),
                                    row(sgu_ln_g), row(sgu_ln_b), w_spatial, bsp_t, wproj_f, wout_f,
                                    row(norm1_post_g))
        return h1, z, y, saved

    (g_ff1, g_ff2), (h1, z, y, saved) = _gather_behind("ff", ff_shards_b, win_f, forward, by_columns=(0,))
    w2_f = g_ff2.reshape(D_FF, D_MODEL)
    hn_b, f_b, df1_b, df2_b, dh1, dg2post, dg2pre, loss_p = _mlp(h1, tgt2, row(norm2_pre_g), row(norm2_post_g),
                                                               g_ff1, w2_f)
    c_arr = lax.axis_index("c").astype(jnp.int32).reshape(1)
    chip_ff2, chip_ff1 = _wgrad_ff(c_arr, f_b, df2_b, hn_b, df1_b)
    chip_ff = [chip_ff1, chip_ff2]

    def backward_mix(token):
        return _bwd_mix(token, dh1, y, z, saved, wpool_f, row(pool_scale), row(sgu_ln_g), row(sgu_ln_b), w_spatial,
                        bsp_t, wproj_f, wout_f, row(norm1_post_g))

    summed_ff, (dz_b, mg_b, dy_b, p_pool, p_proj, dws, dbsp_t, dg1post, dps, dlng, dlnb,
                dbin) = _chip_exchange_behind("ff", chip_ff, [], backward_mix)

    rows = jnp.concatenate([jnp.broadcast_to(loss_p[:, 0:1], (1, D_MODEL)), dps, dlng, dlnb, dg1post, dg2pre, dg2post],
                           axis=0)
    smalls = [rows, dbin, dws.reshape(N_HEADS * SGU_BLOCK, SGU_BLOCK), dbsp_t.T]
    sums = _wgrad_in(xn_b, dz_b, [p_pool, p_proj], smalls)
    chip_bigs, chip_smalls = sums[:3], sums[3:]

    def backward_rest(token):
        dx, dg1pre = _bwd_in(token, dz_b, x2, dh1, row(norm1_pre_g), win_f)
        chip_out, chip_g1pre = _wgrad_out(mg_b, dy_b, [dg1pre])
        return chip_out, dx, chip_g1pre

    summed_in, (chip_out, dx, chip_g1pre) = _chip_exchange_behind("in", chip_bigs, chip_smalls, backward_rest)

    def update_big(token):
        res = _adamw_big("adamw_big", token, list(summed_ff) + list(summed_in[:1]),
                         [(w_ff1, m_w_ff1, v_w_ff1), (w_ff2, m_w_ff2, v_w_ff2), (w_in, m_w_in, v_w_in)])
        return res[2][0], res

    summed_out, (_, upd_big) = _chip_exchange_behind("out", [chip_out], [chip_g1pre], update_big)
    big = {"in": upd_big[2], "ff1": upd_big[0], "ff2": upd_big[1]}
    ws2 = lambda a: a.reshape(N_HEADS * SGU_BLOCK, SGU_BLOCK)
    small_params = [(row(pool_scale), row(m_pool_scale), row(v_pool_scale)),
                    (row(sgu_ln_g), row(m_sgu_ln_g), row(v_sgu_ln_g)),
                    (row(sgu_ln_b), row(m_sgu_ln_b), row(v_sgu_ln_b)),
                    (row(norm1_post_g), row(m_norm1_post_g), row(v_norm1_post_g)),
                    (row(norm2_pre_g), row(m_norm2_pre_g), row(v_norm2_pre_g)),
                    (row(norm2_post_g), row(m_norm2_post_g), row(v_norm2_post_g)),
                    (row(norm1_pre_g), row(m_norm1_pre_g), row(v_norm1_pre_g)),
                    (row(b_in), row(m_b_in), row(v_b_in)),
                    (ws2(w_spatial), ws2(m_w_spatial), ws2(v_w_spatial)),
                    (b_spatial, m_b_spatial, v_b_spatial),
                    (pg2(w_pool), pg2(m_w_pool), pg2(v_w_pool)),
                    (pg2(w_sgu_proj), pg2(m_w_sgu_proj), pg2(v_w_sgu_proj)),
                    (w_out, m_w_out, v_w_out)]
    small_out = _adamw_small(summed_in[3], [summed_out[1]] + list(summed_in[4:])
                             + [summed_in[1], summed_in[2], summed_out[0]], small_params)
    loss = small_out[0][0, 0]
    small_names = SMALL_ROWS[1:] + ("norm1_pre_g", "b_in", "w_spatial", "b_spatial", "w_pool", "w_sgu_proj", "w_out")
    small = {nm: small_out[1 + 4 * p:5 + 4 * p] for p, nm in enumerate(small_names)}

    shapes = {"norm1_pre_g": norm1_pre_g.shape, "w_in": w_in.shape, "b_in": b_in.shape, "w_pool": w_pool.shape,
              "pool_scale": pool_scale.shape, "sgu_ln_g": sgu_ln_g.shape, "sgu_ln_b": sgu_ln_b.shape,
              "w_spatial": w_spatial.shape, "b_spatial": b_spatial.shape, "w_sgu_proj": w_sgu_proj.shape,
              "w_out": w_out.shape, "norm1_post_g": norm1_post_g.shape, "norm2_pre_g": norm2_pre_g.shape,
              "w_ff1": w_ff1.shape, "w_ff2": w_ff2.shape, "norm2_post_g": norm2_post_g.shape}
    source = {"w_in": big["in"], "w_ff1": big["ff1"], "w_ff2": big["ff2"], **small}
    order = list(shapes)
    outs = [loss, dx.reshape(x.shape)]
    for kind in range(4):
        outs += [source[nm][kind].reshape(shapes[nm]) for nm in order]
    return tuple(outs)
```

```python
import math

import jax
import jax.numpy as jnp
from jax import lax
from jax.experimental import pallas as pl
from jax.experimental.pallas import tpu as pltpu

F32, BF16 = jnp.float32, jnp.bfloat16
MESH = pl.DeviceIdType.MESH

D_MODEL = 1024
D_IN = 5120
D_FF = 4096
N_DEV = 8
N_CHIPS = 4
WINDOWS = (2, 4, 8, 16)
N_GROUPS = 4
GROUP = 256
HALO = 16
SGU_BLOCK = 128
N_HEADS = 4
HEAD = 256
CHUNK = 64
EPS = 1e-6
IN_SHARD = D_IN // N_DEV
FF_SHARD = D_FF // N_DEV
OUT_SHARD = D_MODEL // N_DEV
PG_SHARD = GROUP // N_DEV

ADAM_LR, ADAM_B1, ADAM_B2, ADAM_EPS, ADAM_WD, ADAM_STEP = 0.001, 0.9, 0.999, 1e-08, 0.01, 10

VMEM_LIMIT = 56 * 1024 * 1024
TM = 256
TM_BWD = 256
TM_IN = 512
PROJ_COLS = 512
GELU_C0 = math.sqrt(2.0 / math.pi)
GELU_C1 = 0.044715


def _dot(a, b):
    return jnp.dot(a, b, preferred_element_type=F32)


def _dot_nt(a, b):
    return lax.dot_general(a, b, (((1,), (1,)), ((), ())), preferred_element_type=F32)


def _dot_tn(a, b):
    return lax.dot_general(a, b, (((0,), (0,)), ((), ())), preferred_element_type=F32)


def _gelu(x):
    t = jnp.tanh(GELU_C0 * (x + GELU_C1 * (x * x * x)))
    return 0.5 * x * (1.0 + t), t


def _gelu_grad(x, t):
    return 0.5 * (1.0 + t) + 0.5 * x * (1.0 - t * t) * (GELU_C0 * (1.0 + 3.0 * GELU_C1 * (x * x)))


def _sigmoid(x):
    return 1.0 / (1.0 + jnp.exp(-x))


def _mean(x):
    return jnp.mean(x, axis=-1, keepdims=True)


def _colsum(x):
    return jnp.sum(x, axis=0, keepdims=True)


def _const_spec(shape):
    nd = len(shape)
    return pl.BlockSpec(shape, lambda *_: (0,) * nd, pipeline_mode=pl.Buffered(1))


def _acc_spec(shape):
    nd = len(shape)
    return pl.BlockSpec(shape, lambda *_: (0,) * nd)


def _masked_ws(ws_ref):
    ri = lax.broadcasted_iota(jnp.int32, (SGU_BLOCK, SGU_BLOCK), 0) // CHUNK
    ci = lax.broadcasted_iota(jnp.int32, (SGU_BLOCK, SGU_BLOCK), 1) // CHUNK
    return [jnp.where(ri >= ci, ws_ref[h], 0.0).astype(BF16) for h in range(N_HEADS)]


def _pool_fwd(pbuf, tile_idx, tm):
    pos = lax.broadcasted_iota(jnp.int32, (tm, 1), 0) + tile_idx * tm + 1
    pooled = []
    for g, w in enumerate(WINDOWS):
        e = pbuf[:, g * GROUP:(g + 1) * GROUP]
        s, sh = e, 1
        while sh < w:
            s = s + pltpu.roll(s, sh, 0)
            sh *= 2
        inv = 1.0 / jnp.minimum(pos, w).astype(F32)
        pooled.append(s[HALO:] * inv - e[HALO:])
    return pooled


def _spatial_head(ws_h, vb, bsp_ref, h, nblk):
    return jnp.concatenate(
        [_dot(ws_h, vb[n * SGU_BLOCK:(n + 1) * SGU_BLOCK, h * HEAD:(h + 1) * HEAD]) + bsp_ref[:, h:h + 1]
         for n in range(nblk)], axis=0)


def _prenorm(after, x, g1pre):
    t_len = x.shape[0]
    tm = TM_IN

    def body(after_ref, x_ref, g_ref, xn_ref, xnt_ref):
        del after_ref
        xv = x_ref[...]
        xnb = (xv * lax.rsqrt(_mean(xv * xv) + EPS) * g_ref[...]).astype(BF16)
        xn_ref[...] = xnb
        xnt_ref[...] = xnb.T

    return pl.pallas_call(
        body, name="prenorm", grid=(t_len // tm,),
        in_specs=[_ANY, pl.BlockSpec((tm, D_MODEL), lambda i: (i, 0)), _const_spec((1, D_MODEL))],
        out_specs=[pl.BlockSpec((tm, D_MODEL), lambda i: (i, 0)), pl.BlockSpec((D_MODEL, tm), lambda i: (0, i))],
        out_shape=[jax.ShapeDtypeStruct((t_len, D_MODEL), BF16), jax.ShapeDtypeStruct((D_MODEL, t_len), BF16)],
        compiler_params=pltpu.CompilerParams(dimension_semantics=("arbitrary",)),
    )(after, x, g1pre)


def _fwd_mix(after, xn, x, win_g, b_in, wpool, pool_scale, lng, lnb, ws, bsp_t, wproj, wout, g1post):
    t_len = x.shape[0]
    tm = TM
    nt = t_len // tm

    def body(after_ref, xn_ref, xb_ref, win_ref, bin_ref, wpool_ref, ps_ref, lng_ref, lnb_ref, ws_ref, bsp_ref,
             wproj_ref, wout_ref, g1post_ref, z_ref, y_ref, h1_ref, u_ref, gpu_ref, xhat_ref, gpv_ref, sa_ref,
             sb_ref, zcur, znext, pbuf):
        del after_ref
        s = pl.program_id(0)

        @pl.when(s == 0)
        def _():
            znext[...] = jnp.zeros((tm, D_IN), F32)
            pbuf[...] = jnp.zeros((tm + HALO, D_MODEL), F32)

        zcur[...] = znext[...]
        xnb = xn_ref[...]

        def project(p):
            cols = slice(p * PROJ_COLS, (p + 1) * PROJ_COLS)
            zp = _dot(xnb, win_ref[:, cols]) + bin_ref[:, cols]
            if (p + 1) * PROJ_COLS <= D_MODEL:
                z_ref[:, cols] = zp
            znext[:, cols] = zp

        project(0)
        pbuf[0:HALO, :] = jnp.where(s <= 1, 0.0, pbuf[0:HALO, :])
        pbuf[HALO:, :] = zcur[:, 0:D_MODEL]
        pooled = _pool_fwd(pbuf, jnp.maximum(s - 1, 0), tm)
        pbuf[0:HALO, :] = pbuf[tm:tm + HALO, :]
        a = jnp.concatenate([_dot(pooled[g].astype(BF16), wpool_ref[g]) for g in range(N_GROUPS)], axis=1)
        a = a * ps_ref[...]
        project(1)
        zu = zcur[:, D_MODEL:2 * D_MODEL]
        u, tu = _gelu(zu)
        u_ref[...] = u.astype(BF16)
        gpu_ref[...] = _gelu_grad(zu, tu).astype(BF16)
        project(2)
        zv = zcur[:, 2 * D_MODEL:3 * D_MODEL]
        gv, tv = _gelu(zv)
        xc = gv - _mean(gv)
        rln = lax.rsqrt(_mean(xc * xc) + EPS)
        xhat = xc * rln
        xhat_ref[...] = xhat.astype(BF16)
        gpv_ref[...] = (_gelu_grad(zv, tv) * rln).astype(BF16)
        vb = (xhat * lng_ref[...] + lnb_ref[...]).astype(BF16)
        wsm = _masked_ws(ws_ref)
        bbr = []
        for h in range(N_HEADS):
            project(3 + h)
            sv = _spatial_head(wsm[h], vb, bsp_ref, h, tm // SGU_BLOCK)
            bbr.append(_dot((u[:, h * HEAD:(h + 1) * HEAD] * sv).astype(BF16), wproj_ref[h]))
        bbr = jnp.concatenate(bbr, axis=1)
        project(7)
        sa = _sigmoid(zcur[:, 3 * D_MODEL:4 * D_MODEL])
        sb = _sigmoid(zcur[:, 4 * D_MODEL:5 * D_MODEL])
        sa_ref[...] = sa.astype(BF16)
        sb_ref[...] = sb.astype(BF16)
        project(8)
        yv = _dot((sa * a + sb * bbr).astype(BF16), wout_ref[...])
        y_ref[...] = yv
        project(9)
        ry = lax.rsqrt(_mean(yv * yv) + EPS)
        h1_ref[...] = xb_ref[...] + yv * ry * g1post_ref[...]

    proj = lambda w: pl.BlockSpec((tm, w), lambda s: (jnp.minimum(s, nt - 1), 0))
    mix = lambda w: pl.BlockSpec((tm, w), lambda s: (jnp.maximum(s - 1, 0), 0))
    return pl.pallas_call(
        body, name="fwd_mix", grid=(nt + 1,),
        in_specs=[_ANY, proj(D_MODEL), mix(D_MODEL),
                  _const_spec((D_MODEL, D_IN)),
                  _const_spec((1, D_IN)), _const_spec((N_GROUPS, GROUP, GROUP)), _const_spec((1, D_MODEL)),
                  _const_spec((1, D_MODEL)), _const_spec((1, D_MODEL)),
                  _const_spec((N_HEADS, SGU_BLOCK, SGU_BLOCK)), _const_spec((SGU_BLOCK, N_HEADS)),
                  _const_spec((N_HEADS, HEAD, HEAD)), _const_spec((D_MODEL, D_MODEL)), _const_spec((1, D_MODEL))],
        out_specs=[proj(D_MODEL), mix(D_MODEL), mix(D_MODEL)] + [mix(D_MODEL)] * 6,
        out_shape=[jax.ShapeDtypeStruct((t_len, D_MODEL), F32),
                   jax.ShapeDtypeStruct((t_len, D_MODEL), F32), jax.ShapeDtypeStruct((t_len, D_MODEL), F32)]
        + [jax.ShapeDtypeStruct((t_len, D_MODEL), BF16)] * 6,
        scratch_shapes=[pltpu.VMEM((tm, D_IN), F32), pltpu.VMEM((tm, D_IN), F32),
                        pltpu.VMEM((tm + HALO, D_MODEL), F32)],
        compiler_params=pltpu.CompilerParams(dimension_semantics=("arbitrary",), vmem_limit_bytes=VMEM_LIMIT),
    )(after, xn, x, win_g, b_in, wpool, pool_scale, lng, lnb, ws, bsp_t, wproj, wout, g1post)


def _mlp(h1, target, g2pre, g2post, w1_g, w2):
    t_len = h1.shape[0]
    tm = TM
    nt = t_len // tm

    def body(h1_ref, tgt_ref, g2pre_ref, g2post_ref, w1_ref, w2_ref,
             hn_ref, f_ref, df1_ref, df2_ref, dh1_ref, dg2post_ref, dg2pre_ref, loss_ref, f1_scr):
        i = pl.program_id(0)

        @pl.when(i == 0)
        def _():
            dg2post_ref[...] = jnp.zeros_like(dg2post_ref)
            dg2pre_ref[...] = jnp.zeros_like(dg2pre_ref)
            loss_ref[...] = jnp.zeros_like(loss_ref)

        h = h1_ref[...]
        r2 = lax.rsqrt(_mean(h * h) + EPS)
        nh = h * r2
        hnb = (nh * g2pre_ref[...]).astype(BF16)
        hn_ref[...] = hnb.T
        for k in range(N_DEV):
            f1_scr[:, k * FF_SHARD:(k + 1) * FF_SHARD] = _dot(hnb, w1_ref[:, k * FF_SHARD:(k + 1) * FF_SHARD])
        r = jnp.maximum(f1_scr[...], 0.0)
        fb = (r * r).astype(BF16)
        f_ref[...] = fb.T
        f2 = _dot(fb, w2_ref[...])
        rf = lax.rsqrt(_mean(f2 * f2) + EPS)
        nf = f2 * rf
        diff = h + nf * g2post_ref[...] - tgt_ref[...]
        loss_ref[...] += (0.5 / D_MODEL) * jnp.sum(diff * diff)
        dout = diff * (1.0 / D_MODEL)
        dg2post_ref[...] += _colsum(dout * nf)
        dn = dout * g2post_ref[...]
        df2b = (rf * (dn - nf * _mean(dn * nf))).astype(BF16)
        df2_ref[...] = df2b
        df = _dot_nt(df2b, w2_ref[...])
        df1b = (df * (2.0 * jnp.maximum(f1_scr[...], 0.0))).astype(BF16)
        df1_ref[...] = df1b
        dhn = _dot_nt(df1b, w1_ref[...])
        dg2pre_ref[...] += _colsum(dhn * nh)
        dnh = dhn * g2pre_ref[...]
        dh1_ref[...] = dout + r2 * (dnh - nh * _mean(dnh * nh))

    tok = lambda w: pl.BlockSpec((tm, w), lambda i: (i, 0))
    return pl.pallas_call(
        body, name="mlp_fwd_bwd", grid=(nt,),
        in_specs=[tok(D_MODEL), tok(D_MODEL), _const_spec((1, D_MODEL)), _const_spec((1, D_MODEL)),
                  _const_spec((D_MODEL, D_FF)), _const_spec((D_FF, D_MODEL))],
        out_specs=[pl.BlockSpec((D_MODEL, tm), lambda i: (0, i)), pl.BlockSpec((D_FF, tm), lambda i: (0, i)),
                   tok(D_FF), tok(D_MODEL), tok(D_MODEL),
                   _acc_spec((1, D_MODEL)), _acc_spec((1, D_MODEL)), _acc_spec((1, 128))],
        out_shape=[jax.ShapeDtypeStruct((D_MODEL, t_len), BF16), jax.ShapeDtypeStruct((D_FF, t_len), BF16),
                   jax.ShapeDtypeStruct((t_len, D_FF), BF16), jax.ShapeDtypeStruct((t_len, D_MODEL), BF16),
                   jax.ShapeDtypeStruct((t_len, D_MODEL), F32), jax.ShapeDtypeStruct((1, D_MODEL), F32),
                   jax.ShapeDtypeStruct((1, D_MODEL), F32), jax.ShapeDtypeStruct((1, 128), F32)],
        scratch_shapes=[pltpu.VMEM((tm, D_FF), F32)],
        compiler_params=pltpu.CompilerParams(dimension_semantics=("arbitrary",), vmem_limit_bytes=VMEM_LIMIT),
    )(h1, target, g2pre, g2post, w1_g, w2)


def _bwd_mix(after, dh1, y, z, saved, wpool, pool_scale, lng, lnb, ws, bsp_t, wproj, wout, g1post):
    t_len = y.shape[0]
    tm = TM_BWD
    nt = t_len // tm
    nblk = tm // SGU_BLOCK

    def body(after_ref, dh1_ref, y_ref, z_ref, zh_ref, u_ref, gpu_ref, xhat_ref, gpv_ref, sa_ref, sb_ref,
             wpool_ref, ps_ref, lng_ref, lnb_ref, ws_ref,
             bsp_ref, wproj_ref, wout_ref, g1post_ref,
             dz_ref, mg_ref, dy_ref, ppool_ref, pproj_ref, dws_ref, dbsp_ref, dg1post_ref, dps_ref,
             dlng_ref, dlnb_ref, dbin_ref, pbuf, qbuf, dwpool_ref, dwproj_ref):
        del after_ref
        i = pl.program_id(0)
        ti = nt - 1 - i

        @pl.when(i == 0)
        def _():
            for ref in (dwpool_ref, dwproj_ref, dws_ref, dbsp_ref, dg1post_ref, dps_ref, dlng_ref, dlnb_ref,
                        dbin_ref):
                ref[...] = jnp.zeros_like(ref)
            qbuf[tm:tm + HALO, :] = jnp.zeros((HALO, D_MODEL), F32)

        pbuf[0:HALO, :] = jnp.where(ti > 0, zh_ref[...], 0.0)
        pbuf[HALO:, :] = z_ref[...]
        pooled = _pool_fwd(pbuf, ti, tm)
        pooled_b = [p.astype(BF16) for p in pooled]
        a_raw = jnp.concatenate([_dot(pooled_b[g], wpool_ref[g]) for g in range(N_GROUPS)], axis=1)
        wsm = _masked_ws(ws_ref)
        u = u_ref[...].astype(F32)
        xhat = xhat_ref[...].astype(F32)
        vb = (xhat * lng_ref[...] + lnb_ref[...]).astype(BF16)
        sv_heads = [_spatial_head(wsm[h], vb, bsp_ref, h, nblk) for h in range(N_HEADS)]
        gated_b = [(u[:, h * HEAD:(h + 1) * HEAD] * sv_heads[h]).astype(BF16) for h in range(N_HEADS)]
        bbr = jnp.concatenate([_dot(gated_b[h], wproj_ref[h]) for h in range(N_HEADS)], axis=1)
        sa = sa_ref[...].astype(F32)
        sb = sb_ref[...].astype(F32)
        a = a_raw * ps_ref[...]
        mg_ref[...] = (sa * a + sb * bbr).astype(BF16).T

        dh = dh1_ref[...]
        yv = y_ref[...]
        ry = lax.rsqrt(_mean(yv * yv) + EPS)
        ny = yv * ry
        dg1post_ref[...] += _colsum(dh * ny)
        dn = dh * g1post_ref[...]
        dyb = (ry * (dn - ny * _mean(dn * ny))).astype(BF16)
        dy_ref[...] = dyb
        dmg = _dot_nt(dyb, wout_ref[...])

        da = dmg * sa
        dbbr = dmg * sb
        dzga = dmg * a * sa * (1.0 - sa)
        dzgb = dmg * bbr * sb * (1.0 - sb)
        dz_ref[:, 3 * D_MODEL:4 * D_MODEL] = dzga.astype(BF16)
        dz_ref[:, 4 * D_MODEL:5 * D_MODEL] = dzgb.astype(BF16)
        dbin_ref[:, 3 * D_MODEL:4 * D_MODEL] += _colsum(dzga)
        dbin_ref[:, 4 * D_MODEL:5 * D_MODEL] += _colsum(dzgb)

        dps_ref[...] += _colsum(da * a_raw)
        da_raw_b = (da * ps_ref[...]).astype(BF16)
        pos = lax.broadcasted_iota(jnp.int32, (tm, 1), 0) + ti * tm + 1
        dpooled = []
        for g, w in enumerate(WINDOWS):
            cols = slice(g * GROUP, (g + 1) * GROUP)
            dwpool_ref[g] += _dot_tn(pooled_b[g], da_raw_b[:, cols])
            dp = _dot_nt(da_raw_b[:, cols], wpool_ref[g])
            dpooled.append(dp)
            qbuf[0:tm, cols] = dp * (1.0 / jnp.minimum(pos, w).astype(F32))
        n_ext = tm + HALO
        dzp = []
        for g, w in enumerate(WINDOWS):
            e = qbuf[:, g * GROUP:(g + 1) * GROUP]
            s, sh = e, 1
            while sh < w:
                s = s + pltpu.roll(s, n_ext - sh, 0)
                sh *= 2
            dzp.append(s[0:tm] - dpooled[g])
        qbuf[tm:tm + HALO, :] = qbuf[0:HALO, :]
        dzp = jnp.concatenate(dzp, axis=1)
        dz_ref[:, 0:D_MODEL] = dzp.astype(BF16)
        dbin_ref[:, 0:D_MODEL] += _colsum(dzp)

        dv_heads = []
        du_heads = []
        for h in range(N_HEADS):
            cols = slice(h * HEAD, (h + 1) * HEAD)
            dbbr_b = dbbr[:, cols].astype(BF16)
            dwproj_ref[h] += _dot_tn(gated_b[h], dbbr_b)
            dgated = _dot_nt(dbbr_b, wproj_ref[h])
            du_heads.append(dgated * sv_heads[h])
            dsv = dgated * u[:, cols]
            dsv_b = dsv.astype(BF16)
            rows = []
            for n in range(nblk):
                blk = slice(n * SGU_BLOCK, (n + 1) * SGU_BLOCK)
                rows.append(_dot_tn(wsm[h], dsv_b[blk]))
                dws_ref[h] += _dot_nt(dsv_b[blk], vb[blk, cols])
                dbsp_ref[:, h:h + 1] += jnp.sum(dsv[blk], axis=1, keepdims=True)
            dv_heads.append(jnp.concatenate(rows, axis=0))
        dzu = jnp.concatenate(du_heads, axis=1) * gpu_ref[...].astype(F32)
        dz_ref[:, D_MODEL:2 * D_MODEL] = dzu.astype(BF16)
        dbin_ref[:, D_MODEL:2 * D_MODEL] += _colsum(dzu)
        dv = jnp.concatenate(dv_heads, axis=1)
        dlng_ref[...] += _colsum(dv * xhat)
        dlnb_ref[...] += _colsum(dv)
        dxh = dv * lng_ref[...]
        dzv = (dxh - _mean(dxh) - xhat * _mean(dxh * xhat)) * gpv_ref[...].astype(F32)
        dz_ref[:, 2 * D_MODEL:3 * D_MODEL] = dzv.astype(BF16)
        dbin_ref[:, 2 * D_MODEL:3 * D_MODEL] += _colsum(dzv)

        @pl.when(i == nt - 1)
        def _():
            ri = lax.broadcasted_iota(jnp.int32, (SGU_BLOCK, SGU_BLOCK), 0) // CHUNK
            ci = lax.broadcasted_iota(jnp.int32, (SGU_BLOCK, SGU_BLOCK), 1) // CHUNK
            for h in range(N_HEADS):
                dws_ref[h] = jnp.where(ri >= ci, dws_ref[h], 0.0)
            for slot in range(N_DEV):
                own = slice(_owner_of_slot(slot) * PG_SHARD, (_owner_of_slot(slot) + 1) * PG_SHARD)
                for g in range(N_GROUPS):
                    rows = slice(g * PG_SHARD, (g + 1) * PG_SHARD)
                    ppool_ref[slot, rows, :] = dwpool_ref[g, own, :].astype(BF16)
                    pproj_ref[slot, rows, :] = dwproj_ref[g, own, :].astype(BF16)

    tok = lambda w: pl.BlockSpec((tm, w), lambda i: (nt - 1 - i, 0))
    halo = pl.BlockSpec((HALO, D_MODEL), lambda i: (jnp.maximum((nt - 1 - i) * (tm // HALO) - 1, 0), 0))
    return pl.pallas_call(
        body, name="bwd_mix", grid=(nt,),
        in_specs=[_ANY, tok(D_MODEL), tok(D_MODEL), tok(D_MODEL), halo] + [tok(D_MODEL)] * 6
        + [_const_spec((N_GROUPS, GROUP, GROUP)),
                  _const_spec((1, D_MODEL)), _const_spec((1, D_MODEL)), _const_spec((1, D_MODEL)),
                  _const_spec((N_HEADS, SGU_BLOCK, SGU_BLOCK)), _const_spec((SGU_BLOCK, N_HEADS)),
                  _const_spec((N_HEADS, HEAD, HEAD)), _const_spec((D_MODEL, D_MODEL)), _const_spec((1, D_MODEL))],
        out_specs=[tok(D_IN), pl.BlockSpec((D_MODEL, tm), lambda i: (0, nt - 1 - i)), tok(D_MODEL),
                   _acc_spec((N_DEV, N_GROUPS * PG_SHARD, GROUP)), _acc_spec((N_DEV, N_HEADS * PG_SHARD, HEAD)),
                   _acc_spec((N_HEADS, SGU_BLOCK, SGU_BLOCK)), _acc_spec((SGU_BLOCK, N_HEADS)),
                   _acc_spec((1, D_MODEL)), _acc_spec((1, D_MODEL)), _acc_spec((1, D_MODEL)), _acc_spec((1, D_MODEL)),
                   _acc_spec((1, D_IN))],
        out_shape=[jax.ShapeDtypeStruct((t_len, D_IN), BF16),
                   jax.ShapeDtypeStruct((D_MODEL, t_len), BF16), jax.ShapeDtypeStruct((t_len, D_MODEL), BF16),
                   jax.ShapeDtypeStruct((N_DEV, N_GROUPS * PG_SHARD, GROUP), BF16),
                   jax.ShapeDtypeStruct((N_DEV, N_HEADS * PG_SHARD, HEAD), BF16),
                   jax.ShapeDtypeStruct((N_HEADS, SGU_BLOCK, SGU_BLOCK), F32),
                   jax.ShapeDtypeStruct((SGU_BLOCK, N_HEADS), F32),
                   jax.ShapeDtypeStruct((1, D_MODEL), F32), jax.ShapeDtypeStruct((1, D_MODEL), F32),
                   jax.ShapeDtypeStruct((1, D_MODEL), F32), jax.ShapeDtypeStruct((1, D_MODEL), F32),
                   jax.ShapeDtypeStruct((1, D_IN), F32)],
        scratch_shapes=[pltpu.VMEM((tm + HALO, D_MODEL), F32), pltpu.VMEM((tm + HALO, D_MODEL), F32),
                        pltpu.VMEM((N_GROUPS, GROUP, GROUP), F32), pltpu.VMEM((N_HEADS, HEAD, HEAD), F32)],
        compiler_params=pltpu.CompilerParams(dimension_semantics=("arbitrary",), vmem_limit_bytes=VMEM_LIMIT),
    )(after, dh1, y, z, z, *saved, wpool, pool_scale, lng, lnb, ws, bsp_t, wproj, wout, g1post)


def _bwd_in(after, dz_b, x, dh1, g1pre, win_g):
    t_len = x.shape[0]
    tm = TM_IN
    nt = t_len // tm

    def body(after_ref, dz_ref, x_ref, dh1_ref, g1_ref, win_ref, dx_ref, dg1pre_ref):
        del after_ref

        @pl.when(pl.program_id(0) == 0)
        def _():
            dg1pre_ref[...] = jnp.zeros_like(dg1pre_ref)

        dxn = _dot_nt(dz_ref[...], win_ref[...])
        xv = x_ref[...]
        r1 = lax.rsqrt(_mean(xv * xv) + EPS)
        nx = xv * r1
        dg1pre_ref[...] += _colsum(dxn * nx)
        dnx = dxn * g1_ref[...]
        dx_ref[...] = r1 * (dnx - nx * _mean(dnx * nx)) + dh1_ref[...]

    tok = lambda w: pl.BlockSpec((tm, w), lambda i: (i, 0))
    return pl.pallas_call(
        body, name="bwd_in", grid=(nt,),
        in_specs=[_ANY, tok(D_IN), tok(D_MODEL), tok(D_MODEL), _const_spec((1, D_MODEL)),
                  _const_spec((D_MODEL, D_IN))],
        out_specs=[tok(D_MODEL), _acc_spec((1, D_MODEL))],
        out_shape=[jax.ShapeDtypeStruct((t_len, D_MODEL), F32), jax.ShapeDtypeStruct((1, D_MODEL), F32)],
        compiler_params=pltpu.CompilerParams(dimension_semantics=("arbitrary",), vmem_limit_bytes=VMEM_LIMIT),
    )(after, dz_b, x, dh1, g1pre, win_g)


def _owner_of_slot(s):
    return 4 * ((s // 2) % 2) + 2 * (s % 2) + s // 4


def _wgrad_ff(c_arr, f_t, df2_b, hn_t, df1_b):
    t_len = df2_b.shape[0]
    half_shard = FF_SHARD // 2

    def body(c_ref, f_ref, df2_ref, hn_ref, df1_ref, o2_ref, o1_ref, buf2, buf1, land2, land1, send_sems, recv_sems):
        del c_ref
        s = pl.program_id(0)
        k, h = (s // 2) % N_CHIPS, s % 2
        x, y, c = _coords()

        def copies(k):
            return [pltpu.make_async_remote_copy(src_ref=buf.at[k], dst_ref=land.at[k], send_sem=send_sems.at[j, k],
                                                 recv_sem=recv_sems.at[j, k], device_id=(x, y, 1 - c),
                                                 device_id_type=MESH)
                    for j, (buf, land) in enumerate(((buf2, land2), (buf1, land1)))]

        @pl.when(s < N_DEV)
        def _():
            buf2[k, h] = _dot(f_ref[...], df2_ref[...]).astype(BF16)
            buf1[k, h] = _dot(hn_ref[...], df1_ref[...]).astype(BF16)

            @pl.when(h == 1)
            def _():
                for cp in copies(k):
                    cp.start()

        @pl.when(s >= N_DEV)
        def _():
            @pl.when(h == 0)
            def _():
                for cp in copies(k):
                    cp.wait_recv()

            o2_ref[...] = (_dot(f_ref[...], df2_ref[...]) + land2[k, h].astype(F32)).astype(BF16)
            o1_ref[...] = (_dot(hn_ref[...], df1_ref[...]) + land1[k, h].astype(F32)).astype(BF16)

        @pl.when(s == 2 * N_DEV - 1)
        def _():
            for k_sent in range(N_CHIPS):
                for cp in copies(k_sent):
                    cp.wait_send()

    def part(s, c_ref):
        owner = 2 * ((s // 2) % N_CHIPS) + jnp.where(s < N_DEV, 1 - c_ref[0], c_ref[0])
        return 2 * owner + s % 2

    mine_slot = lambda s: jnp.maximum(s // 2 - N_CHIPS, 0)
    mine_half = lambda s: jnp.where(s >= N_DEV, s % 2, 0)
    return pl.pallas_call(
        body, name="wgrad_ff",
        grid_spec=pltpu.PrefetchScalarGridSpec(
            num_scalar_prefetch=1, grid=(2 * N_DEV,),
            in_specs=[pl.BlockSpec((half_shard, t_len), lambda s, c_ref: (part(s, c_ref), 0)),
                      pl.BlockSpec((t_len, D_MODEL), lambda s, c_ref: (0, 0), pipeline_mode=pl.Buffered(1)),
                      pl.BlockSpec((D_MODEL, t_len), lambda s, c_ref: (0, 0), pipeline_mode=pl.Buffered(1)),
                      pl.BlockSpec((t_len, half_shard), lambda s, c_ref: (0, part(s, c_ref)))],
            out_specs=[pl.BlockSpec((None, half_shard, D_MODEL), lambda s, c_ref: (mine_slot(s), mine_half(s), 0)),
                       pl.BlockSpec((None, D_MODEL, half_shard), lambda s, c_ref: (mine_slot(s), 0, mine_half(s)))],
            scratch_shapes=[pltpu.VMEM((N_CHIPS, 2, half_shard, D_MODEL), BF16),
                            pltpu.VMEM((N_CHIPS, 2, D_MODEL, half_shard), BF16)] * 2
            + [pltpu.SemaphoreType.DMA((2, N_CHIPS))] * 2),
        out_shape=[jax.ShapeDtypeStruct((N_CHIPS, FF_SHARD, D_MODEL), BF16),
                   jax.ShapeDtypeStruct((N_CHIPS, D_MODEL, FF_SHARD), BF16)],
        compiler_params=pltpu.CompilerParams(dimension_semantics=("arbitrary",), vmem_limit_bytes=VMEM_LIMIT),
    )(c_arr, f_t, df2_b, hn_t, df1_b)


def _wgrad_pair(name, a, b, in_specs, halves, slot_shape, bigs, smalls):
    nb, nx = len(bigs), len(bigs) + len(smalls)

    def body(a_ref, b_ref, *refs):
        extra_in, out_ref, extra_out = refs[:nx], refs[nx], refs[nx + 1:2 * nx + 1]
        both, land = refs[2 * nx + 1:2 * nx + 3]
        extra_land = refs[2 * nx + 3:3 * nx + 3]
        send_sems, recv_sems, extra_send, extra_recv = refs[3 * nx + 3:]
        q = pl.program_id(0)
        x, y, c = _coords()
        to_sibling = dict(device_id=(x, y, 1 - c), device_id_type=MESH)

        def slot_copy(k):
            return pltpu.make_async_remote_copy(src_ref=both.at[k % 2, 1 - c], dst_ref=land.at[k],
                                                send_sem=send_sems.at[k], recv_sem=recv_sems.at[k], **to_sibling)

        def extra_copies():
            return [pltpu.make_async_remote_copy(
                src_ref=extra_in[j].at[pl.ds(N_CHIPS * (1 - c), N_CHIPS)] if j < nb else extra_in[j],
                dst_ref=extra_land[j], send_sem=extra_send.at[j], recv_sem=extra_recv.at[j], **to_sibling)
                for j in range(nx)]

        @pl.when(q == 0)
        def _():
            for cp in extra_copies():
                cp.start()

        @pl.when(q < N_CHIPS)
        def _():
            @pl.when(q >= 2)
            def _():
                slot_copy(q - 2).wait_send()

            half0, half1 = halves(_dot(a_ref[...], b_ref[...]))
            both[q % 2, 0] = half0.astype(BF16)
            both[q % 2, 1] = half1.astype(BF16)
            slot_copy(q).start()

        @pl.when(q >= 1)
        def _():
            k = q - 1
            slot_copy(k).wait_recv()
            out_ref[...] = (both[k % 2, c].astype(F32) + land[k].astype(F32)).astype(BF16)

        @pl.when(q == N_CHIPS)
        def _():
            slot_copy(N_CHIPS - 2).wait_send()
            slot_copy(N_CHIPS - 1).wait_send()
            for cp in extra_copies():
                cp.wait()
            for j in range(nx):
                if j < nb:
                    mine = extra_in[j][pl.ds(N_CHIPS * c, N_CHIPS)]
                    extra_out[j][...] = (mine.astype(F32) + extra_land[j][...].astype(F32)).astype(BF16)
                else:
                    extra_out[j][...] = extra_in[j][...] + extra_land[j][...]

    got_shapes = [(N_CHIPS,) + t.shape[1:] for t in bigs] + [t.shape for t in smalls]
    dtypes = [t.dtype for t in bigs] + [t.dtype for t in smalls]
    return pl.pallas_call(
        body, name=name, grid=(N_CHIPS + 1,),
        in_specs=list(in_specs) + [_const_spec(t.shape) for t in list(bigs) + list(smalls)],
        out_specs=[pl.BlockSpec((None,) + slot_shape, lambda q: (jnp.maximum(q - 1, 0), 0, 0))]
        + [pl.BlockSpec(s, lambda q, nd=len(s): (0,) * nd) for s in got_shapes],
        out_shape=[jax.ShapeDtypeStruct((N_CHIPS,) + slot_shape, BF16)]
        + [jax.ShapeDtypeStruct(s, d) for s, d in zip(got_shapes, dtypes)],
        scratch_shapes=[pltpu.VMEM((2, 2) + slot_shape, BF16), pltpu.VMEM((N_CHIPS,) + slot_shape, BF16)]
        + [pltpu.VMEM(s, d) for s, d in zip(got_shapes, dtypes)]
        + [pltpu.SemaphoreType.DMA((N_CHIPS,))] * 2 + [pltpu.SemaphoreType.DMA((max(nx, 1),))] * 2,
        compiler_params=pltpu.CompilerParams(dimension_semantics=("arbitrary",), vmem_limit_bytes=VMEM_LIMIT),
    )(a, b, *bigs, *smalls)


def _wgrad_in(xn_t, dz_b, bigs, smalls):
    t_len = dz_b.shape[0]
    return _wgrad_pair("wgrad_in", xn_t, dz_b,
                       [pl.BlockSpec((D_MODEL, t_len), lambda q: (0, 0), pipeline_mode=pl.Buffered(1)),
                        pl.BlockSpec((t_len, 2 * IN_SHARD), lambda q: (0, jnp.minimum(q, N_CHIPS - 1)))],
                       lambda res: (res[:, 0:IN_SHARD], res[:, IN_SHARD:2 * IN_SHARD]), (D_MODEL, IN_SHARD),
                       bigs, smalls)


def _wgrad_out(mg_t, dy_b, smalls):
    t_len = dy_b.shape[0]
    return _wgrad_pair("wgrad_out", mg_t, dy_b,
                       [pl.BlockSpec((2 * OUT_SHARD, t_len), lambda q: (jnp.minimum(q, N_CHIPS - 1), 0)),
                        pl.BlockSpec((t_len, D_MODEL), lambda q: (0, 0), pipeline_mode=pl.Buffered(1))],
                       lambda res: (res[0:OUT_SHARD], res[OUT_SHARD:2 * OUT_SHARD]), (OUT_SHARD, D_MODEL),
                       [], smalls)


def _coords():
    return lax.axis_index("x"), lax.axis_index("y"), lax.axis_index("c")


_ANY = pl.BlockSpec(memory_space=pl.ANY)


_HBM = pl.BlockSpec(memory_space=pltpu.HBM)
_SEM = pl.BlockSpec(memory_space=pltpu.SEMAPHORE)
_VMEM = pl.BlockSpec(memory_space=pltpu.VMEM)
_EFFECT = pltpu.SideEffectType.DATAFLOW_SIDE_EFFECTING
_TOKEN = jax.ShapeDtypeStruct((8, 128), F32)


def _in_hbm(a):
    return pltpu.with_memory_space_constraint(a, pltpu.HBM)


def _split_call(name, body, n_sems_out, arrays, sems_in=(), after=None):
    na, ns = len(arrays), len(sems_in)
    has_after = after is not None

    def kernel_body(*refs):
        arr = refs[:na]
        s_in = refs[na:na + ns]
        outs = refs[na + ns + has_after:]
        body(arr, s_in, outs[:n_sems_out])
        outs[-1][...] = jnp.zeros((8, 128), F32)

    out_shape = ([pltpu.SemaphoreType.DMA(())] * n_sems_out + [pltpu.HBM(a.shape, a.dtype) for a in arrays] + [_TOKEN])
    res = pl.pallas_call(
        kernel_body, name=name, out_shape=out_shape,
        in_specs=[_HBM] * na + [_SEM] * ns + [_ANY] * has_after,
        out_specs=[_SEM] * n_sems_out + [_HBM] * na + [_VMEM],
        input_output_aliases={i: n_sems_out + i for i in range(na)},
        compiler_params=pltpu.CompilerParams(has_side_effects=_EFFECT),
    )(*[_in_hbm(a) for a in arrays], *sems_in, *([after] if has_after else []))
    return list(res[:n_sems_out]), list(res[n_sems_out:n_sems_out + na]), res[-1]


def _wait_bytes_of(ref, send_sem, recv_sem, peer, send=True, recv=True):
    cp = pltpu.make_async_remote_copy(src_ref=ref, dst_ref=ref, send_sem=send_sem, recv_sem=recv_sem,
                                      device_id=peer, device_id_type=MESH)
    if send:
        cp.wait_send()
    if recv:
        cp.wait_recv()


def _gather_behind(tag, shards, after, work, by_columns=(), by_rows=()):
    n = len(shards)

    def land_shape(j, s):
        if j in by_columns:
            return (s.shape[0], N_DEV * s.shape[1])
        if j in by_rows:
            return (s.shape[0], N_DEV * s.shape[1], s.shape[2])
        return (N_DEV,) + s.shape

    lands = [lax.empty(land_shape(j, s), s.dtype) for j, s in enumerate(shards)]

    def slots(arr, j, first, count=1):
        if j in by_columns:
            cols = shards[j].shape[1]
            return arr[n + j].at[:, pl.ds(pl.multiple_of(first * cols, 128), count * cols)]
        if j in by_rows:
            rows = shards[j].shape[1]
            return arr[n + j].at[:, pl.ds(pl.multiple_of(first * rows, 16), count * rows), :]
        return arr[n + j].at[first] if count == 1 else arr[n + j].at[pl.ds(first, count)]

    def own_slot(arr, j, sem):
        x, y, c = _coords()
        return pltpu.make_async_copy(arr[j], slots(arr, j, 4 * x + 2 * y + c), sem)

    def start(arr, _, sems):
        x, y, c = _coords()
        me = 4 * x + 2 * y + c
        for j in range(n):
            for chip in [(1 - x, y), (x, 1 - y), (1 - x, 1 - y)]:
                pltpu.make_async_remote_copy(src_ref=arr[j], dst_ref=slots(arr, j, me), send_sem=sems[j],
                                             recv_sem=sems[n + j], device_id=(*chip, c), device_id_type=MESH).start()
        for j in range(n):
            pltpu.make_async_remote_copy(src_ref=arr[j], dst_ref=slots(arr, j, me), send_sem=sems[2 * n + j],
                                         recv_sem=sems[3 * n + j], device_id=(x, y, 1 - c),
                                         device_id_type=MESH).start()
            own_slot(arr, j, sems[4 * n + j]).start()

    def middle(arr, s_in, sems):
        x, y, c = _coords()
        sibling = (x, y, 1 - c)
        for j in range(n):
            _wait_bytes_of(slots(arr, j, 0, 3), s_in[j], s_in[n + j], sibling)
            for chip in [(1 - x, y), (x, 1 - y), (1 - x, 1 - y)]:
                slot = slots(arr, j, 4 * chip[0] + 2 * chip[1] + c)
                pltpu.make_async_remote_copy(src_ref=slot, dst_ref=slot, send_sem=sems[j], recv_sem=sems[n + j],
                                             device_id=sibling, device_id_type=MESH).start()

    def finish(arr, s_in, _):
        x, y, c = _coords()
        sibling = (x, y, 1 - c)
        for j in range(n):
            _wait_bytes_of(slots(arr, j, 0, 1), s_in[j], s_in[n + j], sibling)
            own_slot(arr, j, s_in[2 * n + j]).wait()
            _wait_bytes_of(slots(arr, j, 0, 3), s_in[3 * n + j], s_in[4 * n + j], sibling)

    sems, arrays, token = _split_call("gather_%s_start" % tag, start, 5 * n, list(shards) + lands, after=after)
    result = work(token)
    fwd_sems, arrays, token = _split_call("gather_%s_middle" % tag, middle, 2 * n, arrays, sems_in=sems[:2 * n],
                                          after=result[0])
    _, arrays, _ = _split_call("gather_%s_finish" % tag, finish, 0, arrays, sems_in=sems[2 * n:] + fwd_sems,
                               after=token)
    return arrays[n:], result


def _chip_exchange_behind(tag, bigs, smalls, work):
    nb, n = len(bigs), len(bigs) + len(smalls)
    lands = [lax.empty(s.shape, s.dtype) for s in bigs] + [lax.empty((N_CHIPS,) + s.shape, s.dtype) for s in smalls]

    def own_slot(arr, j, sem):
        x, y, _ = _coords()
        q_me = 2 * x + y
        return pltpu.make_async_copy(arr[j].at[q_me] if j < nb else arr[j], arr[n + j].at[q_me], sem)

    def start(arr, _, sems):
        x, y, c = _coords()
        q_me = 2 * x + y
        for j in range(n):
            for peer in [(1 - x, y, c), (x, 1 - y, c), (1 - x, 1 - y, c)]:
                piece = arr[j].at[2 * peer[0] + peer[1]] if j < nb else arr[j]
                pltpu.make_async_remote_copy(src_ref=piece, dst_ref=arr[n + j].at[q_me],
                                             send_sem=sems[j], recv_sem=sems[n + j], device_id=peer,
                                             device_id_type=MESH).start()
            own_slot(arr, j, sems[2 * n + j]).start()

    def finish(arr, s_in, _):
        x, y, c = _coords()
        for j in range(n):
            _wait_bytes_of(arr[n + j].at[pl.ds(0, 3)], s_in[j], s_in[n + j], (x, y, 1 - c))
            own_slot(arr, j, s_in[2 * n + j]).wait()

    sems, arrays, token = _split_call("chip_exchange_%s_start" % tag, start, 3 * n, list(bigs) + list(smalls) + lands)
    result = work(token)
    _, arrays, _ = _split_call("chip_exchange_%s_finish" % tag, finish, 0, arrays, sems_in=sems, after=result[0])
    return arrays[n:], result


def _adamw_math(w, g, m, v):
    m = ADAM_B1 * m + (1.0 - ADAM_B1) * g
    v = ADAM_B2 * v + (1.0 - ADAM_B2) * (g * g)
    m_hat = m / (1.0 - ADAM_B1 ** ADAM_STEP)
    v_hat = v / (1.0 - ADAM_B2 ** ADAM_STEP)
    delta = -ADAM_LR * (m_hat / (jnp.sqrt(v_hat) + ADAM_EPS) + ADAM_WD * w)
    return delta, m, v


ADAMW_STEPS = 4


def _adamw_big(name, after, chip_sums, params):
    n = len(params)

    def body(after_ref, *refs):
        del after_ref
        outs = refs[4 * n:]
        for j in range(n):
            t_ref, w_ref, m_ref, v_ref = refs[4 * j:4 * j + 4]
            g = t_ref[0].astype(F32)
            for q in range(1, N_CHIPS):
                g = g + t_ref[q].astype(F32)
            d, mn, vn = _adamw_math(w_ref[...], g, m_ref[...], v_ref[...])
            for out, val in zip(outs[4 * j:4 * j + 4], (g, d, mn, vn)):
                out[...] = val

    in_specs, out_specs, out_shape, operands = [_ANY], [], [], [after]
    for t, (w, m, v) in zip(chip_sums, params):
        rows, cols = w.shape
        tr = rows // ADAMW_STEPS
        blk = pl.BlockSpec((tr, cols), lambda r: (r, 0))
        in_specs += [pl.BlockSpec((N_CHIPS, tr, cols), lambda r: (0, r, 0)), blk, blk, blk]
        out_specs += [blk] * 4
        out_shape += [jax.ShapeDtypeStruct((rows, cols), F32)] * 4
        operands += [t, w, m, v]
    res = pl.pallas_call(
        body, name=name, grid=(ADAMW_STEPS,), in_specs=in_specs, out_specs=out_specs, out_shape=out_shape,
        compiler_params=pltpu.CompilerParams(dimension_semantics=("arbitrary",), vmem_limit_bytes=VMEM_LIMIT),
    )(*operands)
    return [res[4 * j:4 * j + 4] for j in range(n)]


SMALL_ROWS = ("loss", "pool_scale", "sgu_ln_g", "sgu_ln_b", "norm1_post_g", "norm2_pre_g", "norm2_post_g")


N_ROW_PARAMS = len(SMALL_ROWS) - 1


def _adamw_small(u_rows, u_others, params):
    n, n_oth = len(params), len(u_others)

    def body(*refs):
        urow_ref, others = refs[0], refs[1:1 + n_oth]
        wmv = refs[1 + n_oth:1 + n_oth + 3 * n]
        loss_ref = refs[1 + n_oth + 3 * n]
        outs = refs[2 + n_oth + 3 * n:]

        def total(ref, idx):
            g = ref[(0,) + idx].astype(F32)
            for q in range(1, N_CHIPS):
                g = g + ref[(q,) + idx].astype(F32)
            return g

        loss_ref[...] = total(urow_ref, (slice(0, 1), slice(0, 1)))
        for p in range(n):
            if p < N_ROW_PARAMS:
                g = total(urow_ref, (slice(p + 1, p + 2), slice(None)))
            else:
                g = total(others[p - N_ROW_PARAMS], (slice(None), slice(None)))
            d, mn, vn = _adamw_math(wmv[3 * p][...], g, wmv[3 * p + 1][...], wmv[3 * p + 2][...])
            outs[4 * p][...] = g
            outs[4 * p + 1][...] = d
            outs[4 * p + 2][...] = mn
            outs[4 * p + 3][...] = vn

    flat = [a for p in params for a in p]
    out_shape = [jax.ShapeDtypeStruct((1, 1), F32)]
    for w, _, _ in params:
        out_shape += [jax.ShapeDtypeStruct(w.shape, F32)] * 4
    return pl.pallas_call(body, name="adamw_small", out_shape=out_shape)(u_rows, *u_others, *flat)


def _to_bf16(name, after, arrays):
    n = len(arrays)

    def body(after_ref, *refs):
        del after_ref
        for j in range(n):
            refs[n + j][...] = refs[j][...].astype(BF16)

    return pl.pallas_call(
        body, name=name, in_specs=[_ANY] + [_VMEM] * n,
        out_shape=[jax.ShapeDtypeStruct(a.shape, BF16) for a in arrays],
    )(after, *arrays)


def kernel(x, norm1_pre_g, w_in, b_in, w_pool, pool_scale, sgu_ln_g, sgu_ln_b, w_spatial, b_spatial, w_sgu_proj, w_out, norm1_post_g, norm2_pre_g, w_ff1, w_ff2, norm2_post_g, loss_target, m_norm1_pre_g, m_w_in, m_b_in, m_w_pool, m_pool_scale, m_sgu_ln_g, m_sgu_ln_b, m_w_spatial, m_b_spatial, m_w_sgu_proj, m_w_out, m_norm1_post_g, m_norm2_pre_g, m_w_ff1, m_w_ff2, m_norm2_post_g, v_norm1_pre_g, v_w_in, v_b_in, v_w_pool, v_pool_scale, v_sgu_ln_g, v_sgu_ln_b, v_w_spatial, v_b_spatial, v_w_sgu_proj, v_w_out, v_norm1_post_g, v_norm2_pre_g, v_w_ff1, v_w_ff2, v_norm2_post_g):
    t_len = x.shape[1]
    row = lambda a: a.reshape(1, -1)
    x2 = x.reshape(t_len, D_MODEL)
    tgt2 = loss_target.reshape(t_len, D_MODEL)
    pg2 = lambda a: a.reshape(N_GROUPS * PG_SHARD, GROUP)

    mix_shards_b = _to_bf16("mix_shards_to_bf16", x2, [w_in, w_pool, w_sgu_proj, w_out])

    def prework(token):
        xn, xn_b = _prenorm(token, x2, row(norm1_pre_g))
        ff_b = _to_bf16("ff_shards_to_bf16", xn, [w_ff1, w_ff2])
        return ff_b[0], xn, xn_b, ff_b

    (win_f, wpool_f, wproj_f, g_out), (_, xn, xn_b, ff_shards_b) = _gather_behind(
        "mix", mix_shards_b, None, prework, by_columns=(0,), by_rows=(1, 2))
    wout_f = g_out.reshape(D_MODEL, D_MODEL)
    bsp_t = b_spatial.T

    def forward(token):
        z, y, h1, *saved = _fwd_mix(token, xn, x2, win_f, row(b_in), wpool_f, row(pool_scale),
                                    row(sgu_ln_g), row(sgu_ln_b), w_spatial, bsp_t, wproj_f, wout_f,
                                    row(norm1_post_g))
        return h1, z, y, saved

    (g_ff1, g_ff2), (h1, z, y, saved) = _gather_behind("ff", ff_shards_b, win_f, forward, by_columns=(0,))
    w2_f = g_ff2.reshape(D_FF, D_MODEL)
    hn_b, f_b, df1_b, df2_b, dh1, dg2post, dg2pre, loss_p = _mlp(h1, tgt2, row(norm2_pre_g), row(norm2_post_g),
                                                               g_ff1, w2_f)
    c_arr = lax.axis_index("c").astype(jnp.int32).reshape(1)
    chip_ff2, chip_ff1 = _wgrad_ff(c_arr, f_b, df2_b, hn_b, df1_b)
    chip_ff = [chip_ff1, chip_ff2]

    def backward_mix(token):
        return _bwd_mix(token, dh1, y, z, saved, wpool_f, row(pool_scale), row(sgu_ln_g), row(sgu_ln_b), w_spatial,
                        bsp_t, wproj_f, wout_f, row(norm1_post_g))

    summed_ff, (dz_b, mg_b, dy_b, p_pool, p_proj, dws, dbsp_t, dg1post, dps, dlng, dlnb,
                dbin) = _chip_exchange_behind("ff", chip_ff, [], backward_mix)

    rows = jnp.concatenate([jnp.broadcast_to(loss_p[:, 0:1], (1, D_MODEL)), dps, dlng, dlnb, dg1post, dg2pre, dg2post],
                           axis=0)
    smalls = [rows, dbin, dws.reshape(N_HEADS * SGU_BLOCK, SGU_BLOCK), dbsp_t.T]
    sums = _wgrad_in(xn_b, dz_b, [p_pool, p_proj], smalls)
    chip_bigs, chip_smalls = sums[:3], sums[3:]

    def backward_rest(token):
        dx, dg1pre = _bwd_in(token, dz_b, x2, dh1, row(norm1_pre_g), win_f)
        chip_out, chip_g1pre = _wgrad_out(mg_b, dy_b, [dg1pre])
        return chip_out, dx, chip_g1pre

    summed_in, (chip_out, dx, chip_g1pre) = _chip_exchange_behind("in", chip_bigs, chip_smalls, backward_rest)

    def update_big(token):
        res = _adamw_big("adamw_big", token, list(summed_ff) + list(summed_in[:1]),
                         [(w_ff1, m_w_ff1, v_w_ff1), (w_ff2, m_w_ff2, v_w_ff2), (w_in, m_w_in, v_w_in)])
        return res[2][0], res

    summed_out, (_, upd_big) = _chip_exchange_behind("out", [chip_out], [chip_g1pre], update_big)
    big = {"in": upd_big[2], "ff1": upd_big[0], "ff2": upd_big[1]}
    ws2 = lambda a: a.reshape(N_HEADS * SGU_BLOCK, SGU_BLOCK)
    small_params = [(row(pool_scale), row(m_pool_scale), row(v_pool_scale)),
                    (row(sgu_ln_g), row(m_sgu_ln_g), row(v_sgu_ln_g)),
                    (row(sgu_ln_b), row(m_sgu_ln_b), row(v_sgu_ln_b)),
                    (row(norm1_post_g), row(m_norm1_post_g), row(v_norm1_post_g)),
                    (row(norm2_pre_g), row(m_norm2_pre_g), row(v_norm2_pre_g)),
                    (row(norm2_post_g), row(m_norm2_post_g), row(v_norm2_post_g)),
                    (row(norm1_pre_g), row(m_norm1_pre_g), row(v_norm1_pre_g)),
                    (row(b_in), row(m_b_in), row(v_b_in)),
                    (ws2(w_spatial), ws2(m_w_spatial), ws2(v_w_spatial)),
                    (b_spatial, m_b_spatial, v_b_spatial),
                    (pg2(w_pool), pg2(m_w_pool), pg2(v_w_pool)),
                    (pg2(w_sgu_proj), pg2(m_w_sgu_proj), pg2(v_w_sgu_proj)),
                    (w_out, m_w_out, v_w_out)]
    small_out = _adamw_small(summed_in[3], [summed_out[1]] + list(summed_in[4:])
                             + [summed_in[1], summed_in[2], summed_out[0]], small_params)
    loss = small_out[0].reshape(())
    small_names = SMALL_ROWS[1:] + ("norm1_pre_g", "b_in", "w_spatial", "b_spatial", "w_pool", "w_sgu_proj", "w_out")
    small = {nm: small_out[1 + 4 * p:5 + 4 * p] for p, nm in enumerate(small_names)}

    shapes = {"norm1_pre_g": norm1_pre_g.shape, "w_in": w_in.shape, "b_in": b_in.shape, "w_pool": w_pool.shape,
              "pool_scale": pool_scale.shape, "sgu_ln_g": sgu_ln_g.shape, "sgu_ln_b": sgu_ln_b.shape,
              "w_spatial": w_spatial.shape, "b_spatial": b_spatial.shape, "w_sgu_proj": w_sgu_proj.shape,
              "w_out": w_out.shape, "norm1_post_g": norm1_post_g.shape, "norm2_pre_g": norm2_pre_g.shape,
              "w_ff1": w_ff1.shape, "w_ff2": w_ff2.shape, "norm2_post_g": norm2_post_g.shape}
    source = {"w_in": big["in"], "w_ff1": big["ff1"], "w_ff2": big["ff2"], **small}
    order = list(shapes)
    outs = [loss, dx.reshape(x.shape)]
    for kind in range(4):
        outs += [source[nm][kind].reshape(shapes[nm]) for nm in order]
    return tuple(outs)
```

```python
import math

import jax
import jax.numpy as jnp
from jax import lax
from jax.experimental import pallas as pl
from jax.experimental.pallas import tpu as pltpu

F32, BF16 = jnp.float32, jnp.bfloat16
MESH = pl.DeviceIdType.MESH

D_MODEL = 1024
D_IN = 5120
D_FF = 4096
N_DEV = 8
N_CHIPS = 4
WINDOWS = (2, 4, 8, 16)
N_GROUPS = 4
GROUP = 256
HALO = 16
SGU_BLOCK = 128
N_HEADS = 4
HEAD = 256
CHUNK = 64
EPS = 1e-6
IN_SHARD = D_IN // N_DEV
FF_SHARD = D_FF // N_DEV
OUT_SHARD = D_MODEL // N_DEV
PG_SHARD = GROUP // N_DEV

ADAM_LR, ADAM_B1, ADAM_B2, ADAM_EPS, ADAM_WD, ADAM_STEP = 0.001, 0.9, 0.999, 1e-08, 0.01, 10

VMEM_LIMIT = 56 * 1024 * 1024
TM = 256
TM_BWD = 256
TM_IN = 512
PROJ_COLS = 512
GELU_C0 = math.sqrt(2.0 / math.pi)
GELU_C1 = 0.044715


def _dot(a, b):
    return jnp.dot(a, b, preferred_element_type=F32)


def _dot_nt(a, b):
    return lax.dot_general(a, b, (((1,), (1,)), ((), ())), preferred_element_type=F32)


def _dot_tn(a, b):
    return lax.dot_general(a, b, (((0,), (0,)), ((), ())), preferred_element_type=F32)


def _gelu(x):
    t = jnp.tanh(GELU_C0 * (x + GELU_C1 * (x * x * x)))
    return 0.5 * x * (1.0 + t), t


def _gelu_grad(x, t):
    return 0.5 * (1.0 + t) + 0.5 * x * (1.0 - t * t) * (GELU_C0 * (1.0 + 3.0 * GELU_C1 * (x * x)))


def _sigmoid(x):
    return 1.0 / (1.0 + jnp.exp(-x))


def _mean(x):
    return jnp.mean(x, axis=-1, keepdims=True)


def _colsum(x):
    return jnp.sum(x, axis=0, keepdims=True)


def _const_spec(shape):
    nd = len(shape)
    return pl.BlockSpec(shape, lambda *_: (0,) * nd, pipeline_mode=pl.Buffered(1))


def _acc_spec(shape):
    nd = len(shape)
    return pl.BlockSpec(shape, lambda *_: (0,) * nd)


def _masked_ws(ws_ref):
    ri = lax.broadcasted_iota(jnp.int32, (SGU_BLOCK, SGU_BLOCK), 0) // CHUNK
    ci = lax.broadcasted_iota(jnp.int32, (SGU_BLOCK, SGU_BLOCK), 1) // CHUNK
    return [jnp.where(ri >= ci, ws_ref[h], 0.0).astype(BF16) for h in range(N_HEADS)]


def _pool_fwd(pbuf, tile_idx, tm):
    pos = lax.broadcasted_iota(jnp.int32, (tm, 1), 0) + tile_idx * tm + 1
    pooled = []
    for g, w in enumerate(WINDOWS):
        e = pbuf[:, g * GROUP:(g + 1) * GROUP]
        s, sh = e, 1
        while sh < w:
            s = s + pltpu.roll(s, sh, 0)
            sh *= 2
        inv = 1.0 / jnp.minimum(pos, w).astype(F32)
        pooled.append(s[HALO:] * inv - e[HALO:])
    return pooled


def _spatial_head(ws_h, vb, bsp_ref, h, nblk):
    return jnp.concatenate(
        [_dot(ws_h, vb[n * SGU_BLOCK:(n + 1) * SGU_BLOCK, h * HEAD:(h + 1) * HEAD]) + bsp_ref[:, h:h + 1]
         for n in range(nblk)], axis=0)


def _prenorm(after, x, g1pre):
    t_len = x.shape[0]
    tm = TM_IN

    def body(after_ref, x_ref, g_ref, xn_ref, xnt_ref):
        del after_ref
        xv = x_ref[...]
        xnb = (xv * lax.rsqrt(_mean(xv * xv) + EPS) * g_ref[...]).astype(BF16)
        xn_ref[...] = xnb
        xnt_ref[...] = xnb.T

    return pl.pallas_call(
        body, name="prenorm", grid=(t_len // tm,),
        in_specs=[_ANY, pl.BlockSpec((tm, D_MODEL), lambda i: (i, 0)), _const_spec((1, D_MODEL))],
        out_specs=[pl.BlockSpec((tm, D_MODEL), lambda i: (i, 0)), pl.BlockSpec((D_MODEL, tm), lambda i: (0, i))],
        out_shape=[jax.ShapeDtypeStruct((t_len, D_MODEL), BF16), jax.ShapeDtypeStruct((D_MODEL, t_len), BF16)],
        compiler_params=pltpu.CompilerParams(dimension_semantics=("arbitrary",)),
    )(after, x, g1pre)


def _fwd_mix(after, xn, x, win_g, b_in, wpool, pool_scale, lng, lnb, ws, bsp_t, wproj, wout, g1post, hop=None):
    t_len = x.shape[0]
    tm = TM
    nt = t_len // tm
    hop_arrays, hop_sems, hop_body = hop if hop is not None else ((), (), None)
    na, ns = len(hop_arrays), len(hop_sems)
    n_in, n_out = 14, 9

    def body(*refs):
        (_, xn_ref, xb_ref, win_ref, bin_ref, wpool_ref, ps_ref, lng_ref, lnb_ref, ws_ref, bsp_ref,
         wproj_ref, wout_ref, g1post_ref) = refs[:n_in]
        hop_in, hop_sems_in = refs[n_in:n_in + na], refs[n_in + na:n_in + na + ns]
        outs = refs[n_in + na + ns:]
        z_ref, y_ref, h1_ref, u_ref, gpu_ref, xhat_ref, gpv_ref, sa_ref, sb_ref = outs[:n_out]
        hop_sems_out = outs[n_out:n_out + ns]
        zcur, znext, pbuf = refs[-3:]
        s = pl.program_id(0)

        if hop_body is not None:
            @pl.when(s == nt - 1)
            def _():
                hop_body(hop_in, hop_sems_in, hop_sems_out)

        @pl.when(s == 0)
        def _():
            znext[...] = jnp.zeros((tm, D_IN), F32)
            pbuf[...] = jnp.zeros((tm + HALO, D_MODEL), F32)

        zcur[...] = znext[...]
        xnb = xn_ref[...]

        def project(p):
            cols = slice(p * PROJ_COLS, (p + 1) * PROJ_COLS)
            zp = _dot(xnb, win_ref[:, cols]) + bin_ref[:, cols]
            if (p + 1) * PROJ_COLS <= D_MODEL:
                z_ref[:, cols] = zp
            znext[:, cols] = zp

        project(0)
        pbuf[0:HALO, :] = jnp.where(s <= 1, 0.0, pbuf[0:HALO, :])
        pbuf[HALO:, :] = zcur[:, 0:D_MODEL]
        pooled = _pool_fwd(pbuf, jnp.maximum(s - 1, 0), tm)
        pbuf[0:HALO, :] = pbuf[tm:tm + HALO, :]
        a = jnp.concatenate([_dot(pooled[g].astype(BF16), wpool_ref[g]) for g in range(N_GROUPS)], axis=1)
        a = a * ps_ref[...]
        project(1)
        zu = zcur[:, D_MODEL:2 * D_MODEL]
        u, tu = _gelu(zu)
        u_ref[...] = u.astype(BF16)
        gpu_ref[...] = _gelu_grad(zu, tu).astype(BF16)
        project(2)
        zv = zcur[:, 2 * D_MODEL:3 * D_MODEL]
        gv, tv = _gelu(zv)
        xc = gv - _mean(gv)
        rln = lax.rsqrt(_mean(xc * xc) + EPS)
        xhat = xc * rln
        xhat_ref[...] = xhat.astype(BF16)
        gpv_ref[...] = (_gelu_grad(zv, tv) * rln).astype(BF16)
        vb = (xhat * lng_ref[...] + lnb_ref[...]).astype(BF16)
        wsm = _masked_ws(ws_ref)
        bbr = []
        for h in range(N_HEADS):
            project(3 + h)
            sv = _spatial_head(wsm[h], vb, bsp_ref, h, tm // SGU_BLOCK)
            bbr.append(_dot((u[:, h * HEAD:(h + 1) * HEAD] * sv).astype(BF16), wproj_ref[h]))
        bbr = jnp.concatenate(bbr, axis=1)
        project(7)
        sa = _sigmoid(zcur[:, 3 * D_MODEL:4 * D_MODEL])
        sb = _sigmoid(zcur[:, 4 * D_MODEL:5 * D_MODEL])
        sa_ref[...] = sa.astype(BF16)
        sb_ref[...] = sb.astype(BF16)
        project(8)
        yv = _dot((sa * a + sb * bbr).astype(BF16), wout_ref[...])
        y_ref[...] = yv
        project(9)
        ry = lax.rsqrt(_mean(yv * yv) + EPS)
        h1_ref[...] = xb_ref[...] + yv * ry * g1post_ref[...]

    proj = lambda w: pl.BlockSpec((tm, w), lambda s: (jnp.minimum(s, nt - 1), 0))
    mix = lambda w: pl.BlockSpec((tm, w), lambda s: (jnp.maximum(s - 1, 0), 0))
    res = pl.pallas_call(
        body, name="fwd_mix", grid=(nt + 1,),
        in_specs=[_ANY, proj(D_MODEL), mix(D_MODEL),
                  _const_spec((D_MODEL, D_IN)),
                  _const_spec((1, D_IN)), _const_spec((N_GROUPS, GROUP, GROUP)), _const_spec((1, D_MODEL)),
                  _const_spec((1, D_MODEL)), _const_spec((1, D_MODEL)),
                  _const_spec((N_HEADS, SGU_BLOCK, SGU_BLOCK)), _const_spec((SGU_BLOCK, N_HEADS)),
                  _const_spec((N_HEADS, HEAD, HEAD)), _const_spec((D_MODEL, D_MODEL)), _const_spec((1, D_MODEL))]
        + [_HBM] * na + [_SEM] * ns,
        out_specs=[proj(D_MODEL), mix(D_MODEL), mix(D_MODEL)] + [mix(D_MODEL)] * 6 + [_SEM] * ns + [_HBM] * na,
        out_shape=[jax.ShapeDtypeStruct((t_len, D_MODEL), F32),
                   jax.ShapeDtypeStruct((t_len, D_MODEL), F32), jax.ShapeDtypeStruct((t_len, D_MODEL), F32)]
        + [jax.ShapeDtypeStruct((t_len, D_MODEL), BF16)] * 6
        + [pltpu.SemaphoreType.DMA(())] * ns + [pltpu.HBM(a.shape, a.dtype) for a in hop_arrays],
        scratch_shapes=[pltpu.VMEM((tm, D_IN), F32), pltpu.VMEM((tm, D_IN), F32),
                        pltpu.VMEM((tm + HALO, D_MODEL), F32)],
        input_output_aliases={n_in + i: n_out + ns + i for i in range(na)},
        compiler_params=pltpu.CompilerParams(dimension_semantics=("arbitrary",), vmem_limit_bytes=VMEM_LIMIT,
                                             **({"has_side_effects": _EFFECT} if hop is not None else {})),
    )(after, xn, x, win_g, b_in, wpool, pool_scale, lng, lnb, ws, bsp_t, wproj, wout, g1post,
      *[_in_hbm(a) for a in hop_arrays], *hop_sems)
    if hop is None:
        return res
    return list(res[:n_out]) + [list(res[n_out:n_out + ns]), list(res[n_out + ns:])]


def _mlp(h1, target, g2pre, g2post, w1_g, w2):
    t_len = h1.shape[0]
    tm = TM
    nt = t_len // tm

    def body(h1_ref, tgt_ref, g2pre_ref, g2post_ref, w1_ref, w2_ref,
             hn_ref, f_ref, df1_ref, df2_ref, dh1_ref, dg2post_ref, dg2pre_ref, loss_ref, f1_scr):
        i = pl.program_id(0)

        @pl.when(i == 0)
        def _():
            dg2post_ref[...] = jnp.zeros_like(dg2post_ref)
            dg2pre_ref[...] = jnp.zeros_like(dg2pre_ref)
            loss_ref[...] = jnp.zeros_like(loss_ref)

        h = h1_ref[...]
        r2 = lax.rsqrt(_mean(h * h) + EPS)
        nh = h * r2
        hnb = (nh * g2pre_ref[...]).astype(BF16)
        hn_ref[...] = hnb.T
        for k in range(N_DEV):
            f1_scr[:, k * FF_SHARD:(k + 1) * FF_SHARD] = _dot(hnb, w1_ref[:, k * FF_SHARD:(k + 1) * FF_SHARD])
        r = jnp.maximum(f1_scr[...], 0.0)
        fb = (r * r).astype(BF16)
        f_ref[...] = fb.T
        f2 = _dot(fb, w2_ref[...])
        rf = lax.rsqrt(_mean(f2 * f2) + EPS)
        nf = f2 * rf
        diff = h + nf * g2post_ref[...] - tgt_ref[...]
        loss_ref[...] += (0.5 / D_MODEL) * jnp.sum(diff * diff)
        dout = diff * (1.0 / D_MODEL)
        dg2post_ref[...] += _colsum(dout * nf)
        dn = dout * g2post_ref[...]
        df2b = (rf * (dn - nf * _mean(dn * nf))).astype(BF16)
        df2_ref[...] = df2b
        df = _dot_nt(df2b, w2_ref[...])
        df1b = (df * (2.0 * jnp.maximum(f1_scr[...], 0.0))).astype(BF16)
        df1_ref[...] = df1b
        dhn = _dot_nt(df1b, w1_ref[...])
        dg2pre_ref[...] += _colsum(dhn * nh)
        dnh = dhn * g2pre_ref[...]
        dh1_ref[...] = dout + r2 * (dnh - nh * _mean(dnh * nh))

    tok = lambda w: pl.BlockSpec((tm, w), lambda i: (i, 0))
    return pl.pallas_call(
        body, name="mlp_fwd_bwd", grid=(nt,),
        in_specs=[tok(D_MODEL), tok(D_MODEL), _const_spec((1, D_MODEL)), _const_spec((1, D_MODEL)),
                  _const_spec((D_MODEL, D_FF)), _const_spec((D_FF, D_MODEL))],
        out_specs=[pl.BlockSpec((D_MODEL, tm), lambda i: (0, i)), pl.BlockSpec((D_FF, tm), lambda i: (0, i)),
                   tok(D_FF), tok(D_MODEL), tok(D_MODEL),
                   _acc_spec((1, D_MODEL)), _acc_spec((1, D_MODEL)), _acc_spec((1, 128))],
        out_shape=[jax.ShapeDtypeStruct((D_MODEL, t_len), BF16), jax.ShapeDtypeStruct((D_FF, t_len), BF16),
                   jax.ShapeDtypeStruct((t_len, D_FF), BF16), jax.ShapeDtypeStruct((t_len, D_MODEL), BF16),
                   jax.ShapeDtypeStruct((t_len, D_MODEL), F32), jax.ShapeDtypeStruct((1, D_MODEL), F32),
                   jax.ShapeDtypeStruct((1, D_MODEL), F32), jax.ShapeDtypeStruct((1, 128), F32)],
        scratch_shapes=[pltpu.VMEM((tm, D_FF), F32)],
        compiler_params=pltpu.CompilerParams(dimension_semantics=("arbitrary",), vmem_limit_bytes=VMEM_LIMIT),
    )(h1, target, g2pre, g2post, w1_g, w2)


def _bwd_mix(after, dh1, y, z, saved, wpool, pool_scale, lng, lnb, ws, bsp_t, wproj, wout, g1post):
    t_len = y.shape[0]
    tm = TM_BWD
    nt = t_len // tm
    nblk = tm // SGU_BLOCK

    def body(after_ref, dh1_ref, y_ref, z_ref, zh_ref, u_ref, gpu_ref, xhat_ref, gpv_ref, sa_ref, sb_ref,
             wpool_ref, ps_ref, lng_ref, lnb_ref, ws_ref,
             bsp_ref, wproj_ref, wout_ref, g1post_ref,
             dz_ref, mg_ref, dy_ref, ppool_ref, pproj_ref, dws_ref, dbsp_ref, dg1post_ref, dps_ref,
             dlng_ref, dlnb_ref, dbin_ref, pbuf, qbuf, dwpool_ref, dwproj_ref):
        del after_ref
        i = pl.program_id(0)
        ti = nt - 1 - i

        @pl.when(i == 0)
        def _():
            for ref in (dwpool_ref, dwproj_ref, dws_ref, dbsp_ref, dg1post_ref, dps_ref, dlng_ref, dlnb_ref,
                        dbin_ref):
                ref[...] = jnp.zeros_like(ref)
            qbuf[tm:tm + HALO, :] = jnp.zeros((HALO, D_MODEL), F32)

        pbuf[0:HALO, :] = jnp.where(ti > 0, zh_ref[...], 0.0)
        pbuf[HALO:, :] = z_ref[...]
        pooled = _pool_fwd(pbuf, ti, tm)
        pooled_b = [p.astype(BF16) for p in pooled]
        a_raw = jnp.concatenate([_dot(pooled_b[g], wpool_ref[g]) for g in range(N_GROUPS)], axis=1)
        wsm = _masked_ws(ws_ref)
        u = u_ref[...].astype(F32)
        xhat = xhat_ref[...].astype(F32)
        vb = (xhat * lng_ref[...] + lnb_ref[...]).astype(BF16)
        sv_heads = [_spatial_head(wsm[h], vb, bsp_ref, h, nblk) for h in range(N_HEADS)]
        gated_b = [(u[:, h * HEAD:(h + 1) * HEAD] * sv_heads[h]).astype(BF16) for h in range(N_HEADS)]
        bbr = jnp.concatenate([_dot(gated_b[h], wproj_ref[h]) for h in range(N_HEADS)], axis=1)
        sa = sa_ref[...].astype(F32)
        sb = sb_ref[...].astype(F32)
        a = a_raw * ps_ref[...]
        mg_ref[...] = (sa * a + sb * bbr).astype(BF16).T

        dh = dh1_ref[...]
        yv = y_ref[...]
        ry = lax.rsqrt(_mean(yv * yv) + EPS)
        ny = yv * ry
        dg1post_ref[...] += _colsum(dh * ny)
        dn = dh * g1post_ref[...]
        dyb = (ry * (dn - ny * _mean(dn * ny))).astype(BF16)
        dy_ref[...] = dyb
        dmg = _dot_nt(dyb, wout_ref[...])

        da = dmg * sa
        dbbr = dmg * sb
        dzga = dmg * a * sa * (1.0 - sa)
        dzgb = dmg * bbr * sb * (1.0 - sb)
        dz_ref[:, 3 * D_MODEL:4 * D_MODEL] = dzga.astype(BF16)
        dz_ref[:, 4 * D_MODEL:5 * D_MODEL] = dzgb.astype(BF16)
        dbin_ref[:, 3 * D_MODEL:4 * D_MODEL] += _colsum(dzga)
        dbin_ref[:, 4 * D_MODEL:5 * D_MODEL] += _colsum(dzgb)

        dps_ref[...] += _colsum(da * a_raw)
        da_raw_b = (da * ps_ref[...]).astype(BF16)
        pos = lax.broadcasted_iota(jnp.int32, (tm, 1), 0) + ti * tm + 1
        dpooled = []
        for g, w in enumerate(WINDOWS):
            cols = slice(g * GROUP, (g + 1) * GROUP)
            dwpool_ref[g] += _dot_tn(pooled_b[g], da_raw_b[:, cols])
            dp = _dot_nt(da_raw_b[:, cols], wpool_ref[g])
            dpooled.append(dp)
            qbuf[0:tm, cols] = dp * (1.0 / jnp.minimum(pos, w).astype(F32))
        n_ext = tm + HALO
        dzp = []
        for g, w in enumerate(WINDOWS):
            e = qbuf[:, g * GROUP:(g + 1) * GROUP]
            s, sh = e, 1
            while sh < w:
                s = s + pltpu.roll(s, n_ext - sh, 0)
                sh *= 2
            dzp.append(s[0:tm] - dpooled[g])
        qbuf[tm:tm + HALO, :] = qbuf[0:HALO, :]
        dzp = jnp.concatenate(dzp, axis=1)
        dz_ref[:, 0:D_MODEL] = dzp.astype(BF16)
        dbin_ref[:, 0:D_MODEL] += _colsum(dzp)

        dv_heads = []
        du_heads = []
        for h in range(N_HEADS):
            cols = slice(h * HEAD, (h + 1) * HEAD)
            dbbr_b = dbbr[:, cols].astype(BF16)
            dwproj_ref[h] += _dot_tn(gated_b[h], dbbr_b)
            dgated = _dot_nt(dbbr_b, wproj_ref[h])
            du_heads.append(dgated * sv_heads[h])
            dsv = dgated * u[:, cols]
            dsv_b = dsv.astype(BF16)
            rows = []
            for n in range(nblk):
                blk = slice(n * SGU_BLOCK, (n + 1) * SGU_BLOCK)
                rows.append(_dot_tn(wsm[h], dsv_b[blk]))
                dws_ref[h] += _dot_nt(dsv_b[blk], vb[blk, cols])
                dbsp_ref[:, h:h + 1] += jnp.sum(dsv[blk], axis=1, keepdims=True)
            dv_heads.append(jnp.concatenate(rows, axis=0))
        dzu = jnp.concatenate(du_heads, axis=1) * gpu_ref[...].astype(F32)
        dz_ref[:, D_MODEL:2 * D_MODEL] = dzu.astype(BF16)
        dbin_ref[:, D_MODEL:2 * D_MODEL] += _colsum(dzu)
        dv = jnp.concatenate(dv_heads, axis=1)
        dlng_ref[...] += _colsum(dv * xhat)
        dlnb_ref[...] += _colsum(dv)
        dxh = dv * lng_ref[...]
        dzv = (dxh - _mean(dxh) - xhat * _mean(dxh * xhat)) * gpv_ref[...].astype(F32)
        dz_ref[:, 2 * D_MODEL:3 * D_MODEL] = dzv.astype(BF16)
        dbin_ref[:, 2 * D_MODEL:3 * D_MODEL] += _colsum(dzv)

        @pl.when(i == nt - 1)
        def _():
            ri = lax.broadcasted_iota(jnp.int32, (SGU_BLOCK, SGU_BLOCK), 0) // CHUNK
            ci = lax.broadcasted_iota(jnp.int32, (SGU_BLOCK, SGU_BLOCK), 1) // CHUNK
            for h in range(N_HEADS):
                dws_ref[h] = jnp.where(ri >= ci, dws_ref[h], 0.0)
            for slot in range(N_DEV):
                own = slice(_owner_of_slot(slot) * PG_SHARD, (_owner_of_slot(slot) + 1) * PG_SHARD)
                for g in range(N_GROUPS):
                    rows = slice(g * PG_SHARD, (g + 1) * PG_SHARD)
                    ppool_ref[slot, rows, :] = dwpool_ref[g, own, :].astype(BF16)
                    pproj_ref[slot, rows, :] = dwproj_ref[g, own, :].astype(BF16)

    tok = lambda w: pl.BlockSpec((tm, w), lambda i: (nt - 1 - i, 0))
    halo = pl.BlockSpec((HALO, D_MODEL), lambda i: (jnp.maximum((nt - 1 - i) * (tm // HALO) - 1, 0), 0))
    return pl.pallas_call(
        body, name="bwd_mix", grid=(nt,),
        in_specs=[_ANY, tok(D_MODEL), tok(D_MODEL), tok(D_MODEL), halo] + [tok(D_MODEL)] * 6
        + [_const_spec((N_GROUPS, GROUP, GROUP)),
                  _const_spec((1, D_MODEL)), _const_spec((1, D_MODEL)), _const_spec((1, D_MODEL)),
                  _const_spec((N_HEADS, SGU_BLOCK, SGU_BLOCK)), _const_spec((SGU_BLOCK, N_HEADS)),
                  _const_spec((N_HEADS, HEAD, HEAD)), _const_spec((D_MODEL, D_MODEL)), _const_spec((1, D_MODEL))],
        out_specs=[tok(D_IN), pl.BlockSpec((D_MODEL, tm), lambda i: (0, nt - 1 - i)), tok(D_MODEL),
                   _acc_spec((N_DEV, N_GROUPS * PG_SHARD, GROUP)), _acc_spec((N_DEV, N_HEADS * PG_SHARD, HEAD)),
                   _acc_spec((N_HEADS, SGU_BLOCK, SGU_BLOCK)), _acc_spec((SGU_BLOCK, N_HEADS)),
                   _acc_spec((1, D_MODEL)), _acc_spec((1, D_MODEL)), _acc_spec((1, D_MODEL)), _acc_spec((1, D_MODEL)),
                   _acc_spec((1, D_IN))],
        out_shape=[jax.ShapeDtypeStruct((t_len, D_IN), BF16),
                   jax.ShapeDtypeStruct((D_MODEL, t_len), BF16), jax.ShapeDtypeStruct((t_len, D_MODEL), BF16),
                   jax.ShapeDtypeStruct((N_DEV, N_GROUPS * PG_SHARD, GROUP), BF16),
                   jax.ShapeDtypeStruct((N_DEV, N_HEADS * PG_SHARD, HEAD), BF16),
                   jax.ShapeDtypeStruct((N_HEADS, SGU_BLOCK, SGU_BLOCK), F32),
                   jax.ShapeDtypeStruct((SGU_BLOCK, N_HEADS), F32),
                   jax.ShapeDtypeStruct((1, D_MODEL), F32), jax.ShapeDtypeStruct((1, D_MODEL), F32),
                   jax.ShapeDtypeStruct((1, D_MODEL), F32), jax.ShapeDtypeStruct((1, D_MODEL), F32),
                   jax.ShapeDtypeStruct((1, D_IN), F32)],
        scratch_shapes=[pltpu.VMEM((tm + HALO, D_MODEL), F32), pltpu.VMEM((tm + HALO, D_MODEL), F32),
                        pltpu.VMEM((N_GROUPS, GROUP, GROUP), F32), pltpu.VMEM((N_HEADS, HEAD, HEAD), F32)],
        compiler_params=pltpu.CompilerParams(dimension_semantics=("arbitrary",), vmem_limit_bytes=VMEM_LIMIT),
    )(after, dh1, y, z, z, *saved, wpool, pool_scale, lng, lnb, ws, bsp_t, wproj, wout, g1post)


def _bwd_in(after, dz_b, x, dh1, g1pre, win_g):
    t_len = x.shape[0]
    tm = TM_IN
    nt = t_len // tm

    def body(after_ref, dz_ref, x_ref, dh1_ref, g1_ref, win_ref, dx_ref, dg1pre_ref):
        del after_ref

        @pl.when(pl.program_id(0) == 0)
        def _():
            dg1pre_ref[...] = jnp.zeros_like(dg1pre_ref)

        dxn = _dot_nt(dz_ref[...], win_ref[...])
        xv = x_ref[...]
        r1 = lax.rsqrt(_mean(xv * xv) + EPS)
        nx = xv * r1
        dg1pre_ref[...] += _colsum(dxn * nx)
        dnx = dxn * g1_ref[...]
        dx_ref[...] = r1 * (dnx - nx * _mean(dnx * nx)) + dh1_ref[...]

    tok = lambda w: pl.BlockSpec((tm, w), lambda i: (i, 0))
    return pl.pallas_call(
        body, name="bwd_in", grid=(nt,),
        in_specs=[_ANY, tok(D_IN), tok(D_MODEL), tok(D_MODEL), _const_spec((1, D_MODEL)),
                  _const_spec((D_MODEL, D_IN))],
        out_specs=[tok(D_MODEL), _acc_spec((1, D_MODEL))],
        out_shape=[jax.ShapeDtypeStruct((t_len, D_MODEL), F32), jax.ShapeDtypeStruct((1, D_MODEL), F32)],
        compiler_params=pltpu.CompilerParams(dimension_semantics=("arbitrary",), vmem_limit_bytes=VMEM_LIMIT),
    )(after, dz_b, x, dh1, g1pre, win_g)


def _owner_of_slot(s):
    return 4 * ((s // 2) % 2) + 2 * (s % 2) + s // 4


def _wgrad_ff(c_arr, f_t, df2_b, hn_t, df1_b):
    t_len = df2_b.shape[0]
    half_shard = FF_SHARD // 2

    def body(c_ref, f_ref, df2_ref, hn_ref, df1_ref, o2_ref, o1_ref, buf2, buf1, land2, land1, send_sems, recv_sems):
        del c_ref
        s = pl.program_id(0)
        k, h = (s // 2) % N_CHIPS, s % 2
        x, y, c = _coords()

        def copies(k):
            return [pltpu.make_async_remote_copy(src_ref=buf.at[k], dst_ref=land.at[k], send_sem=send_sems.at[j, k],
                                                 recv_sem=recv_sems.at[j, k], device_id=(x, y, 1 - c),
                                                 device_id_type=MESH)
                    for j, (buf, land) in enumerate(((buf2, land2), (buf1, land1)))]

        @pl.when(s < N_DEV)
        def _():
            buf2[k, h] = _dot(f_ref[...], df2_ref[...]).astype(BF16)
            buf1[k, h] = _dot(hn_ref[...], df1_ref[...]).astype(BF16)

            @pl.when(h == 1)
            def _():
                for cp in copies(k):
                    cp.start()

        @pl.when(s >= N_DEV)
        def _():
            @pl.when(h == 0)
            def _():
                for cp in copies(k):
                    cp.wait_recv()

            o2_ref[...] = (_dot(f_ref[...], df2_ref[...]) + land2[k, h].astype(F32)).astype(BF16)
            o1_ref[...] = (_dot(hn_ref[...], df1_ref[...]) + land1[k, h].astype(F32)).astype(BF16)

        @pl.when(s == 2 * N_DEV - 1)
        def _():
            for k_sent in range(N_CHIPS):
                for cp in copies(k_sent):
                    cp.wait_send()

    def part(s, c_ref):
        owner = 2 * ((s // 2) % N_CHIPS) + jnp.where(s < N_DEV, 1 - c_ref[0], c_ref[0])
        return 2 * owner + s % 2

    mine_slot = lambda s: jnp.maximum(s // 2 - N_CHIPS, 0)
    mine_half = lambda s: jnp.where(s >= N_DEV, s % 2, 0)
    return pl.pallas_call(
        body, name="wgrad_ff",
        grid_spec=pltpu.PrefetchScalarGridSpec(
            num_scalar_prefetch=1, grid=(2 * N_DEV,),
            in_specs=[pl.BlockSpec((half_shard, t_len), lambda s, c_ref: (part(s, c_ref), 0)),
                      pl.BlockSpec((t_len, D_MODEL), lambda s, c_ref: (0, 0), pipeline_mode=pl.Buffered(1)),
                      pl.BlockSpec((D_MODEL, t_len), lambda s, c_ref: (0, 0), pipeline_mode=pl.Buffered(1)),
                      pl.BlockSpec((t_len, half_shard), lambda s, c_ref: (0, part(s, c_ref)))],
            out_specs=[pl.BlockSpec((None, half_shard, D_MODEL), lambda s, c_ref: (mine_slot(s), mine_half(s), 0)),
                       pl.BlockSpec((None, D_MODEL, half_shard), lambda s, c_ref: (mine_slot(s), 0, mine_half(s)))],
            scratch_shapes=[pltpu.VMEM((N_CHIPS, 2, half_shard, D_MODEL), BF16),
                            pltpu.VMEM((N_CHIPS, 2, D_MODEL, half_shard), BF16)] * 2
            + [pltpu.SemaphoreType.DMA((2, N_CHIPS))] * 2),
        out_shape=[jax.ShapeDtypeStruct((N_CHIPS, FF_SHARD, D_MODEL), BF16),
                   jax.ShapeDtypeStruct((N_CHIPS, D_MODEL, FF_SHARD), BF16)],
        compiler_params=pltpu.CompilerParams(dimension_semantics=("arbitrary",), vmem_limit_bytes=VMEM_LIMIT),
    )(c_arr, f_t, df2_b, hn_t, df1_b)


def _wgrad_pair(name, a, b, in_specs, halves, slot_shape, bigs, smalls):
    nb, nx = len(bigs), len(bigs) + len(smalls)

    def body(a_ref, b_ref, *refs):
        extra_in, out_ref, extra_out = refs[:nx], refs[nx], refs[nx + 1:2 * nx + 1]
        both, land = refs[2 * nx + 1:2 * nx + 3]
        extra_land = refs[2 * nx + 3:3 * nx + 3]
        send_sems, recv_sems, extra_send, extra_recv = refs[3 * nx + 3:]
        q = pl.program_id(0)
        x, y, c = _coords()
        to_sibling = dict(device_id=(x, y, 1 - c), device_id_type=MESH)

        def slot_copy(k):
            return pltpu.make_async_remote_copy(src_ref=both.at[k % 2, 1 - c], dst_ref=land.at[k],
                                                send_sem=send_sems.at[k], recv_sem=recv_sems.at[k], **to_sibling)

        def extra_copies():
            return [pltpu.make_async_remote_copy(
                src_ref=extra_in[j].at[pl.ds(N_CHIPS * (1 - c), N_CHIPS)] if j < nb else extra_in[j],
                dst_ref=extra_land[j], send_sem=extra_send.at[j], recv_sem=extra_recv.at[j], **to_sibling)
                for j in range(nx)]

        @pl.when(q == 0)
        def _():
            for cp in extra_copies():
                cp.start()

        @pl.when(q < N_CHIPS)
        def _():
            @pl.when(q >= 2)
            def _():
                slot_copy(q - 2).wait_send()

            half0, half1 = halves(_dot(a_ref[...], b_ref[...]))
            both[q % 2, 0] = half0.astype(BF16)
            both[q % 2, 1] = half1.astype(BF16)
            slot_copy(q).start()

        @pl.when(q >= 1)
        def _():
            k = q - 1
            slot_copy(k).wait_recv()
            out_ref[...] = (both[k % 2, c].astype(F32) + land[k].astype(F32)).astype(BF16)

        @pl.when(q == N_CHIPS)
        def _():
            slot_copy(N_CHIPS - 2).wait_send()
            slot_copy(N_CHIPS - 1).wait_send()
            for cp in extra_copies():
                cp.wait()
            for j in range(nx):
                if j < nb:
                    mine = extra_in[j][pl.ds(N_CHIPS * c, N_CHIPS)]
                    extra_out[j][...] = (mine.astype(F32) + extra_land[j][...].astype(F32)).astype(BF16)
                else:
                    extra_out[j][...] = extra_in[j][...] + extra_land[j][...]

    got_shapes = [(N_CHIPS,) + t.shape[1:] for t in bigs] + [t.shape for t in smalls]
    dtypes = [t.dtype for t in bigs] + [t.dtype for t in smalls]
    return pl.pallas_call(
        body, name=name, grid=(N_CHIPS + 1,),
        in_specs=list(in_specs) + [_const_spec(t.shape) for t in list(bigs) + list(smalls)],
        out_specs=[pl.BlockSpec((None,) + slot_shape, lambda q: (jnp.maximum(q - 1, 0), 0, 0))]
        + [pl.BlockSpec(s, lambda q, nd=len(s): (0,) * nd) for s in got_shapes],
        out_shape=[jax.ShapeDtypeStruct((N_CHIPS,) + slot_shape, BF16)]
        + [jax.ShapeDtypeStruct(s, d) for s, d in zip(got_shapes, dtypes)],
        scratch_shapes=[pltpu.VMEM((2, 2) + slot_shape, BF16), pltpu.VMEM((N_CHIPS,) + slot_shape, BF16)]
        + [pltpu.VMEM(s, d) for s, d in zip(got_shapes, dtypes)]
        + [pltpu.SemaphoreType.DMA((N_CHIPS,))] * 2 + [pltpu.SemaphoreType.DMA((max(nx, 1),))] * 2,
        compiler_params=pltpu.CompilerParams(dimension_semantics=("arbitrary",), vmem_limit_bytes=VMEM_LIMIT),
    )(a, b, *bigs, *smalls)


def _wgrad_in(xn_t, dz_b, bigs, smalls):
    t_len = dz_b.shape[0]
    return _wgrad_pair("wgrad_in", xn_t, dz_b,
                       [pl.BlockSpec((D_MODEL, t_len), lambda q: (0, 0), pipeline_mode=pl.Buffered(1)),
                        pl.BlockSpec((t_len, 2 * IN_SHARD), lambda q: (0, jnp.minimum(q, N_CHIPS - 1)))],
                       lambda res: (res[:, 0:IN_SHARD], res[:, IN_SHARD:2 * IN_SHARD]), (D_MODEL, IN_SHARD),
                       bigs, smalls)


def _wgrad_out(mg_t, dy_b, smalls):
    t_len = dy_b.shape[0]
    return _wgrad_pair("wgrad_out", mg_t, dy_b,
                       [pl.BlockSpec((2 * OUT_SHARD, t_len), lambda q: (jnp.minimum(q, N_CHIPS - 1), 0)),
                        pl.BlockSpec((t_len, D_MODEL), lambda q: (0, 0), pipeline_mode=pl.Buffered(1))],
                       lambda res: (res[0:OUT_SHARD], res[OUT_SHARD:2 * OUT_SHARD]), (OUT_SHARD, D_MODEL),
                       [], smalls)


def _coords():
    return lax.axis_index("x"), lax.axis_index("y"), lax.axis_index("c")


_ANY = pl.BlockSpec(memory_space=pl.ANY)


_HBM = pl.BlockSpec(memory_space=pltpu.HBM)
_SEM = pl.BlockSpec(memory_space=pltpu.SEMAPHORE)
_VMEM = pl.BlockSpec(memory_space=pltpu.VMEM)
_EFFECT = pltpu.SideEffectType.DATAFLOW_SIDE_EFFECTING
_TOKEN = jax.ShapeDtypeStruct((8, 128), F32)


def _in_hbm(a):
    return pltpu.with_memory_space_constraint(a, pltpu.HBM)


def _split_call(name, body, n_sems_out, arrays, sems_in=(), after=None):
    na, ns = len(arrays), len(sems_in)
    has_after = after is not None

    def kernel_body(*refs):
        arr = refs[:na]
        s_in = refs[na:na + ns]
        outs = refs[na + ns + has_after:]
        body(arr, s_in, outs[:n_sems_out])
        outs[-1][...] = jnp.zeros((8, 128), F32)

    out_shape = ([pltpu.SemaphoreType.DMA(())] * n_sems_out + [pltpu.HBM(a.shape, a.dtype) for a in arrays] + [_TOKEN])
    res = pl.pallas_call(
        kernel_body, name=name, out_shape=out_shape,
        in_specs=[_HBM] * na + [_SEM] * ns + [_ANY] * has_after,
        out_specs=[_SEM] * n_sems_out + [_HBM] * na + [_VMEM],
        input_output_aliases={i: n_sems_out + i for i in range(na)},
        compiler_params=pltpu.CompilerParams(has_side_effects=_EFFECT),
    )(*[_in_hbm(a) for a in arrays], *sems_in, *([after] if has_after else []))
    return list(res[:n_sems_out]), list(res[n_sems_out:n_sems_out + na]), res[-1]


def _wait_bytes_of(ref, send_sem, recv_sem, peer, send=True, recv=True):
    cp = pltpu.make_async_remote_copy(src_ref=ref, dst_ref=ref, send_sem=send_sem, recv_sem=recv_sem,
                                      device_id=peer, device_id_type=MESH)
    if send:
        cp.wait_send()
    if recv:
        cp.wait_recv()


def _gather_behind(tag, shards, after, work, by_columns=(), by_rows=(), middle_inside_work=False):
    n = len(shards)

    def land_shape(j, s):
        if j in by_columns:
            return (s.shape[0], N_DEV * s.shape[1])
        if j in by_rows:
            return (s.shape[0], N_DEV * s.shape[1], s.shape[2])
        return (N_DEV,) + s.shape

    lands = [lax.empty(land_shape(j, s), s.dtype) for j, s in enumerate(shards)]

    def slots(arr, j, first, count=1):
        if j in by_columns:
            cols = shards[j].shape[1]
            return arr[n + j].at[:, pl.ds(pl.multiple_of(first * cols, 128), count * cols)]
        if j in by_rows:
            rows = shards[j].shape[1]
            return arr[n + j].at[:, pl.ds(pl.multiple_of(first * rows, 16), count * rows), :]
        return arr[n + j].at[first] if count == 1 else arr[n + j].at[pl.ds(first, count)]

    def own_slot(arr, j, sem):
        x, y, c = _coords()
        return pltpu.make_async_copy(arr[j], slots(arr, j, 4 * x + 2 * y + c), sem)

    def start(arr, _, sems):
        x, y, c = _coords()
        me = 4 * x + 2 * y + c
        for j in range(n):
            for chip in [(1 - x, y), (x, 1 - y), (1 - x, 1 - y)]:
                pltpu.make_async_remote_copy(src_ref=arr[j], dst_ref=slots(arr, j, me), send_sem=sems[j],
                                             recv_sem=sems[n + j], device_id=(*chip, c), device_id_type=MESH).start()
        for j in range(n):
            pltpu.make_async_remote_copy(src_ref=arr[j], dst_ref=slots(arr, j, me), send_sem=sems[2 * n + j],
                                         recv_sem=sems[3 * n + j], device_id=(x, y, 1 - c),
                                         device_id_type=MESH).start()
            own_slot(arr, j, sems[4 * n + j]).start()

    def middle(arr, s_in, sems):
        x, y, c = _coords()
        sibling = (x, y, 1 - c)
        for j in range(n):
            _wait_bytes_of(slots(arr, j, 0, 3), s_in[j], s_in[n + j], sibling)
            for chip in [(1 - x, y), (x, 1 - y), (1 - x, 1 - y)]:
                slot = slots(arr, j, 4 * chip[0] + 2 * chip[1] + c)
                pltpu.make_async_remote_copy(src_ref=slot, dst_ref=slot, send_sem=sems[j], recv_sem=sems[n + j],
                                             device_id=sibling, device_id_type=MESH).start()

    def finish(arr, s_in, _):
        x, y, c = _coords()
        sibling = (x, y, 1 - c)
        for j in range(n):
            _wait_bytes_of(slots(arr, j, 0, 1), s_in[j], s_in[n + j], sibling)
            own_slot(arr, j, s_in[2 * n + j]).wait()
            _wait_bytes_of(slots(arr, j, 0, 3), s_in[3 * n + j], s_in[4 * n + j], sibling)

    sems, arrays, token = _split_call("gather_%s_start" % tag, start, 5 * n, list(shards) + lands, after=after)
    if middle_inside_work:
        *result, fwd_sems, arrays = work(token, (arrays, sems[:2 * n], middle))
        token = result[0]
    else:
        result = work(token)
        fwd_sems, arrays, token = _split_call("gather_%s_middle" % tag, middle, 2 * n, arrays, sems_in=sems[:2 * n],
                                              after=result[0])
    _, arrays, _ = _split_call("gather_%s_finish" % tag, finish, 0, arrays, sems_in=sems[2 * n:] + fwd_sems,
                               after=token)
    return arrays[n:], result


def _chip_exchange_behind(tag, bigs, smalls, work):
    nb, n = len(bigs), len(bigs) + len(smalls)
    lands = [lax.empty(s.shape, s.dtype) for s in bigs] + [lax.empty((N_CHIPS,) + s.shape, s.dtype) for s in smalls]

    def own_slot(arr, j, sem):
        x, y, _ = _coords()
        q_me = 2 * x + y
        return pltpu.make_async_copy(arr[j].at[q_me] if j < nb else arr[j], arr[n + j].at[q_me], sem)

    def start(arr, _, sems):
        x, y, c = _coords()
        q_me = 2 * x + y
        for j in range(n):
            for peer in [(1 - x, y, c), (x, 1 - y, c), (1 - x, 1 - y, c)]:
                piece = arr[j].at[2 * peer[0] + peer[1]] if j < nb else arr[j]
                pltpu.make_async_remote_copy(src_ref=piece, dst_ref=arr[n + j].at[q_me],
                                             send_sem=sems[j], recv_sem=sems[n + j], device_id=peer,
                                             device_id_type=MESH).start()
            own_slot(arr, j, sems[2 * n + j]).start()

    def finish(arr, s_in, _):
        x, y, c = _coords()
        for j in range(n):
            _wait_bytes_of(arr[n + j].at[pl.ds(0, 3)], s_in[j], s_in[n + j], (x, y, 1 - c))
            own_slot(arr, j, s_in[2 * n + j]).wait()

    sems, arrays, token = _split_call("chip_exchange_%s_start" % tag, start, 3 * n, list(bigs) + list(smalls) + lands)
    result = work(token)
    _, arrays, _ = _split_call("chip_exchange_%s_finish" % tag, finish, 0, arrays, sems_in=sems, after=result[0])
    return arrays[n:], result


def _adamw_math(w, g, m, v):
    m = ADAM_B1 * m + (1.0 - ADAM_B1) * g
    v = ADAM_B2 * v + (1.0 - ADAM_B2) * (g * g)
    m_hat = m / (1.0 - ADAM_B1 ** ADAM_STEP)
    v_hat = v / (1.0 - ADAM_B2 ** ADAM_STEP)
    delta = -ADAM_LR * (m_hat / (jnp.sqrt(v_hat) + ADAM_EPS) + ADAM_WD * w)
    return delta, m, v


ADAMW_STEPS = 4


def _adamw_big(name, after, chip_sums, params):
    n = len(params)

    def body(after_ref, *refs):
        del after_ref
        outs = refs[4 * n:]
        for j in range(n):
            t_ref, w_ref, m_ref, v_ref = refs[4 * j:4 * j + 4]
            g = t_ref[0].astype(F32)
            for q in range(1, N_CHIPS):
                g = g + t_ref[q].astype(F32)
            d, mn, vn = _adamw_math(w_ref[...], g, m_ref[...], v_ref[...])
            for out, val in zip(outs[4 * j:4 * j + 4], (g, d, mn, vn)):
                out[...] = val

    in_specs, out_specs, out_shape, operands = [_ANY], [], [], [after]
    for t, (w, m, v) in zip(chip_sums, params):
        rows, cols = w.shape
        tr = rows // ADAMW_STEPS
        blk = pl.BlockSpec((tr, cols), lambda r: (r, 0))
        in_specs += [pl.BlockSpec((N_CHIPS, tr, cols), lambda r: (0, r, 0)), blk, blk, blk]
        out_specs += [blk] * 4
        out_shape += [jax.ShapeDtypeStruct((rows, cols), F32)] * 4
        operands += [t, w, m, v]
    res = pl.pallas_call(
        body, name=name, grid=(ADAMW_STEPS,), in_specs=in_specs, out_specs=out_specs, out_shape=out_shape,
        compiler_params=pltpu.CompilerParams(dimension_semantics=("arbitrary",), vmem_limit_bytes=VMEM_LIMIT),
    )(*operands)
    return [res[4 * j:4 * j + 4] for j in range(n)]


SMALL_ROWS = ("loss", "pool_scale", "sgu_ln_g", "sgu_ln_b", "norm1_post_g", "norm2_pre_g", "norm2_post_g")


N_ROW_PARAMS = len(SMALL_ROWS) - 1


def _adamw_small(u_rows, u_others, params):
    n, n_oth = len(params), len(u_others)

    def body(*refs):
        urow_ref, others = refs[0], refs[1:1 + n_oth]
        wmv = refs[1 + n_oth:1 + n_oth + 3 * n]
        loss_ref = refs[1 + n_oth + 3 * n]
        outs = refs[2 + n_oth + 3 * n:]

        def total(ref, idx):
            g = ref[(0,) + idx].astype(F32)
            for q in range(1, N_CHIPS):
                g = g + ref[(q,) + idx].astype(F32)
            return g

        loss_ref[...] = total(urow_ref, (slice(0, 1), slice(0, 1)))
        for p in range(n):
            if p < N_ROW_PARAMS:
                g = total(urow_ref, (slice(p + 1, p + 2), slice(None)))
            else:
                g = total(others[p - N_ROW_PARAMS], (slice(None), slice(None)))
            d, mn, vn = _adamw_math(wmv[3 * p][...], g, wmv[3 * p + 1][...], wmv[3 * p + 2][...])
            outs[4 * p][...] = g
            outs[4 * p + 1][...] = d
            outs[4 * p + 2][...] = mn
            outs[4 * p + 3][...] = vn

    flat = [a for p in params for a in p]
    out_shape = [jax.ShapeDtypeStruct((1, 1), F32)]
    for w, _, _ in params:
        out_shape += [jax.ShapeDtypeStruct(w.shape, F32)] * 4
    return pl.pallas_call(body, name="adamw_small", out_shape=out_shape)(u_rows, *u_others, *flat)


def _to_bf16(name, after, arrays):
    n = len(arrays)

    def body(after_ref, *refs):
        del after_ref
        for j in range(n):
            refs[n + j][...] = refs[j][...].astype(BF16)

    return pl.pallas_call(
        body, name=name, in_specs=[_ANY] + [_VMEM] * n,
        out_shape=[jax.ShapeDtypeStruct(a.shape, BF16) for a in arrays],
    )(after, *arrays)


def kernel(x, norm1_pre_g, w_in, b_in, w_pool, pool_scale, sgu_ln_g, sgu_ln_b, w_spatial, b_spatial, w_sgu_proj, w_out, norm1_post_g, norm2_pre_g, w_ff1, w_ff2, norm2_post_g, loss_target, m_norm1_pre_g, m_w_in, m_b_in, m_w_pool, m_pool_scale, m_sgu_ln_g, m_sgu_ln_b, m_w_spatial, m_b_spatial, m_w_sgu_proj, m_w_out, m_norm1_post_g, m_norm2_pre_g, m_w_ff1, m_w_ff2, m_norm2_post_g, v_norm1_pre_g, v_w_in, v_b_in, v_w_pool, v_pool_scale, v_sgu_ln_g, v_sgu_ln_b, v_w_spatial, v_b_spatial, v_w_sgu_proj, v_w_out, v_norm1_post_g, v_norm2_pre_g, v_w_ff1, v_w_ff2, v_norm2_post_g):
    t_len = x.shape[1]
    row = lambda a: a.reshape(1, -1)
    x2 = x.reshape(t_len, D_MODEL)
    tgt2 = loss_target.reshape(t_len, D_MODEL)
    pg2 = lambda a: a.reshape(N_GROUPS * PG_SHARD, GROUP)

    mix_shards_b = _to_bf16("mix_shards_to_bf16", x2, [w_in, w_pool, w_sgu_proj, w_out])

    def prework(token):
        xn, xn_b = _prenorm(token, x2, row(norm1_pre_g))
        ff_b = _to_bf16("ff_shards_to_bf16", xn, [w_ff1, w_ff2])
        return ff_b[0], xn, xn_b, ff_b

    (win_f, wpool_f, wproj_f, g_out), (_, xn, xn_b, ff_shards_b) = _gather_behind(
        "mix", mix_shards_b, None, prework, by_columns=(0,), by_rows=(1, 2))
    wout_f = g_out.reshape(D_MODEL, D_MODEL)
    bsp_t = b_spatial.T

    def forward(token, hop=None):
        z, y, h1, *rest = _fwd_mix(token, xn, x2, win_f, row(b_in), wpool_f, row(pool_scale),
                                   row(sgu_ln_g), row(sgu_ln_b), w_spatial, bsp_t, wproj_f, wout_f,
                                   row(norm1_post_g), hop)
        return (h1, z, y, rest) if hop is None else (h1, z, y, rest[:6], rest[6], rest[7])

    (g_ff1, g_ff2), (h1, z, y, saved) = _gather_behind("ff", ff_shards_b, win_f, forward, by_columns=(0,),
                                                       middle_inside_work=True)
    w2_f = g_ff2.reshape(D_FF, D_MODEL)
    hn_b, f_b, df1_b, df2_b, dh1, dg2post, dg2pre, loss_p = _mlp(h1, tgt2, row(norm2_pre_g), row(norm2_post_g),
                                                               g_ff1, w2_f)
    c_arr = lax.axis_index("c").astype(jnp.int32).reshape(1)
    chip_ff2, chip_ff1 = _wgrad_ff(c_arr, f_b, df2_b, hn_b, df1_b)
    chip_ff = [chip_ff1, chip_ff2]

    def backward_mix(token):
        return _bwd_mix(token, dh1, y, z, saved, wpool_f, row(pool_scale), row(sgu_ln_g), row(sgu_ln_b), w_spatial,
                        bsp_t, wproj_f, wout_f, row(norm1_post_g))

    summed_ff, (dz_b, mg_b, dy_b, p_pool, p_proj, dws, dbsp_t, dg1post, dps, dlng, dlnb,
                dbin) = _chip_exchange_behind("ff", chip_ff, [], backward_mix)

    rows = jnp.concatenate([jnp.broadcast_to(loss_p[:, 0:1], (1, D_MODEL)), dps, dlng, dlnb, dg1post, dg2pre, dg2post],
                           axis=0)
    smalls = [rows, dbin, dws.reshape(N_HEADS * SGU_BLOCK, SGU_BLOCK), dbsp_t.T]
    sums = _wgrad_in(xn_b, dz_b, [p_pool, p_proj], smalls)
    chip_bigs, chip_smalls = sums[:3], sums[3:]

    def backward_rest(token):
        dx, dg1pre = _bwd_in(token, dz_b, x2, dh1, row(norm1_pre_g), win_f)
        chip_out, chip_g1pre = _wgrad_out(mg_b, dy_b, [dg1pre])
        return chip_out, dx, chip_g1pre

    summed_in, (chip_out, dx, chip_g1pre) = _chip_exchange_behind("in", chip_bigs, chip_smalls, backward_rest)

    def update_big(token):
        res = _adamw_big("adamw_big", token, list(summed_ff) + list(summed_in[:1]),
                         [(w_ff1, m_w_ff1, v_w_ff1), (w_ff2, m_w_ff2, v_w_ff2), (w_in, m_w_in, v_w_in)])
        return res[2][0], res

    summed_out, (_, upd_big) = _chip_exchange_behind("out", [chip_out], [chip_g1pre], update_big)
    big = {"in": upd_big[2], "ff1": upd_big[0], "ff2": upd_big[1]}
    ws2 = lambda a: a.reshape(N_HEADS * SGU_BLOCK, SGU_BLOCK)
    small_params = [(row(pool_scale), row(m_pool_scale), row(v_pool_scale)),
                    (row(sgu_ln_g), row(m_sgu_ln_g), row(v_sgu_ln_g)),
                    (row(sgu_ln_b), row(m_sgu_ln_b), row(v_sgu_ln_b)),
                    (row(norm1_post_g), row(m_norm1_post_g), row(v_norm1_post_g)),
                    (row(norm2_pre_g), row(m_norm2_pre_g), row(v_norm2_pre_g)),
                    (row(norm2_post_g), row(m_norm2_post_g), row(v_norm2_post_g)),
                    (row(norm1_pre_g), row(m_norm1_pre_g), row(v_norm1_pre_g)),
                    (row(b_in), row(m_b_in), row(v_b_in)),
                    (ws2(w_spatial), ws2(m_w_spatial), ws2(v_w_spatial)),
                    (b_spatial, m_b_spatial, v_b_spatial),
                    (pg2(w_pool), pg2(m_w_pool), pg2(v_w_pool)),
                    (pg2(w_sgu_proj), pg2(m_w_sgu_proj), pg2(v_w_sgu_proj)),
                    (w_out, m_w_out, v_w_out)]
    small_out = _adamw_small(summed_in[3], [summed_out[1]] + list(summed_in[4:])
                             + [summed_in[1], summed_in[2], summed_out[0]], small_params)
    loss = small_out[0].reshape(())
    small_names = SMALL_ROWS[1:] + ("norm1_pre_g", "b_in", "w_spatial", "b_spatial", "w_pool", "w_sgu_proj", "w_out")
    small = {nm: small_out[1 + 4 * p:5 + 4 * p] for p, nm in enumerate(small_names)}

    shapes = {"norm1_pre_g": norm1_pre_g.shape, "w_in": w_in.shape, "b_in": b_in.shape, "w_pool": w_pool.shape,
              "pool_scale": pool_scale.shape, "sgu_ln_g": sgu_ln_g.shape, "sgu_ln_b": sgu_ln_b.shape,
              "w_spatial": w_spatial.shape, "b_spatial": b_spatial.shape, "w_sgu_proj": w_sgu_proj.shape,
              "w_out": w_out.shape, "norm1_post_g": norm1_post_g.shape, "norm2_pre_g": norm2_pre_g.shape,
              "w_ff1": w_ff1.shape, "w_ff2": w_ff2.shape, "norm2_post_g": norm2_post_g.shape}
    source = {"w_in": big["in"], "w_ff1": big["ff1"], "w_ff2": big["ff2"], **small}
    order = list(shapes)
    outs = [loss, dx.reshape(x.shape)]
    for kind in range(4):
        outs += [source[nm][kind].reshape(shapes[nm]) for nm in order]
    return tuple(outs)
```

```python
import math

import jax
import jax.numpy as jnp
from jax import lax
from jax.experimental import pallas as pl
from jax.experimental.pallas import tpu as pltpu

F32, BF16 = jnp.float32, jnp.bfloat16
MESH = pl.DeviceIdType.MESH

D_MODEL = 1024
D_IN = 5120
D_FF = 4096
N_DEV = 8
N_CHIPS = 4
WINDOWS = (2, 4, 8, 16)
N_GROUPS = 4
GROUP = 256
HALO = 16
SGU_BLOCK = 128
N_HEADS = 4
HEAD = 256
CHUNK = 64
EPS = 1e-6
IN_SHARD = D_IN // N_DEV
FF_SHARD = D_FF // N_DEV
OUT_SHARD = D_MODEL // N_DEV
PG_SHARD = GROUP // N_DEV

ADAM_LR, ADAM_B1, ADAM_B2, ADAM_EPS, ADAM_WD, ADAM_STEP = 0.001, 0.9, 0.999, 1e-08, 0.01, 10

VMEM_LIMIT = 56 * 1024 * 1024
TM = 256
TM_BWD = 256
TM_IN = 512
PROJ_COLS = 512
GELU_C0 = math.sqrt(2.0 / math.pi)
GELU_C1 = 0.044715


def _dot(a, b):
    return jnp.dot(a, b, preferred_element_type=F32)


def _dot_nt(a, b):
    return lax.dot_general(a, b, (((1,), (1,)), ((), ())), preferred_element_type=F32)


def _dot_tn(a, b):
    return lax.dot_general(a, b, (((0,), (0,)), ((), ())), preferred_element_type=F32)


def _gelu(x):
    t = jnp.tanh(GELU_C0 * (x + GELU_C1 * (x * x * x)))
    return 0.5 * x * (1.0 + t), t


def _gelu_grad(x, t):
    return 0.5 * (1.0 + t) + 0.5 * x * (1.0 - t * t) * (GELU_C0 * (1.0 + 3.0 * GELU_C1 * (x * x)))


def _sigmoid(x):
    return 1.0 / (1.0 + jnp.exp(-x))


def _mean(x):
    return jnp.mean(x, axis=-1, keepdims=True)


def _colsum(x):
    return jnp.sum(x, axis=0, keepdims=True)


def _const_spec(shape):
    nd = len(shape)
    return pl.BlockSpec(shape, lambda *_: (0,) * nd, pipeline_mode=pl.Buffered(1))


def _acc_spec(shape):
    nd = len(shape)
    return pl.BlockSpec(shape, lambda *_: (0,) * nd)


def _masked_ws(ws_ref):
    ri = lax.broadcasted_iota(jnp.int32, (SGU_BLOCK, SGU_BLOCK), 0) // CHUNK
    ci = lax.broadcasted_iota(jnp.int32, (SGU_BLOCK, SGU_BLOCK), 1) // CHUNK
    return [jnp.where(ri >= ci, ws_ref[h], 0.0).astype(BF16) for h in range(N_HEADS)]


def _pool_fwd(pbuf, tile_idx, tm):
    pos = lax.broadcasted_iota(jnp.int32, (tm, 1), 0) + tile_idx * tm + 1
    pooled = []
    for g, w in enumerate(WINDOWS):
        e = pbuf[:, g * GROUP:(g + 1) * GROUP]
        s, sh = e, 1
        while sh < w:
            s = s + pltpu.roll(s, sh, 0)
            sh *= 2
        inv = 1.0 / jnp.minimum(pos, w).astype(F32)
        pooled.append(s[HALO:] * inv - e[HALO:])
    return pooled


def _spatial_head(ws_h, vb, bsp_ref, h, nblk):
    return jnp.concatenate(
        [_dot(ws_h, vb[n * SGU_BLOCK:(n + 1) * SGU_BLOCK, h * HEAD:(h + 1) * HEAD]) + bsp_ref[:, h:h + 1]
         for n in range(nblk)], axis=0)


def _prenorm(after, x, g1pre):
    t_len = x.shape[0]
    tm = TM_IN

    def body(after_ref, x_ref, g_ref, xn_ref, xnt_ref):
        del after_ref
        xv = x_ref[...]
        xnb = (xv * lax.rsqrt(_mean(xv * xv) + EPS) * g_ref[...]).astype(BF16)
        xn_ref[...] = xnb
        xnt_ref[...] = xnb.T

    return pl.pallas_call(
        body, name="prenorm", grid=(t_len // tm,),
        in_specs=[_ANY, pl.BlockSpec((tm, D_MODEL), lambda i: (i, 0)), _const_spec((1, D_MODEL))],
        out_specs=[pl.BlockSpec((tm, D_MODEL), lambda i: (i, 0)), pl.BlockSpec((D_MODEL, tm), lambda i: (0, i))],
        out_shape=[jax.ShapeDtypeStruct((t_len, D_MODEL), BF16), jax.ShapeDtypeStruct((D_MODEL, t_len), BF16)],
        compiler_params=pltpu.CompilerParams(dimension_semantics=("arbitrary",)),
    )(after, x, g1pre)


def _fwd_mix(after, xn, x, win_g, b_in, wpool, pool_scale, lng, lnb, ws, bsp_t, wproj, wout, g1post, hop=None):
    t_len = x.shape[0]
    tm = TM
    nt = t_len // tm
    hop_arrays, hop_sems, hop_body = hop if hop is not None else ((), (), None)
    na, ns = len(hop_arrays), len(hop_sems)
    n_in, n_out = 14, 9

    def body(*refs):
        (_, xn_ref, xb_ref, win_ref, bin_ref, wpool_ref, ps_ref, lng_ref, lnb_ref, ws_ref, bsp_ref,
         wproj_ref, wout_ref, g1post_ref) = refs[:n_in]
        hop_in, hop_sems_in = refs[n_in:n_in + na], refs[n_in + na:n_in + na + ns]
        outs = refs[n_in + na + ns:]
        z_ref, y_ref, h1_ref, u_ref, gpu_ref, xhat_ref, gpv_ref, sa_ref, sb_ref = outs[:n_out]
        hop_sems_out = outs[n_out:n_out + ns]
        zcur, znext, pbuf = refs[-3:]
        s = pl.program_id(0)

        if hop_body is not None:
            @pl.when(s == nt - 1)
            def _():
                hop_body(hop_in, hop_sems_in, hop_sems_out)

        @pl.when(s == 0)
        def _():
            znext[...] = jnp.zeros((tm, D_IN), F32)
            pbuf[...] = jnp.zeros((tm + HALO, D_MODEL), F32)

        zcur[...] = znext[...]
        xnb = xn_ref[...]

        def project(p):
            cols = slice(p * PROJ_COLS, (p + 1) * PROJ_COLS)
            zp = _dot(xnb, win_ref[:, cols]) + bin_ref[:, cols]
            if (p + 1) * PROJ_COLS <= D_MODEL:
                z_ref[:, cols] = zp
            znext[:, cols] = zp

        project(0)
        pbuf[0:HALO, :] = jnp.where(s <= 1, 0.0, pbuf[0:HALO, :])
        pbuf[HALO:, :] = zcur[:, 0:D_MODEL]
        pooled = _pool_fwd(pbuf, jnp.maximum(s - 1, 0), tm)
        pbuf[0:HALO, :] = pbuf[tm:tm + HALO, :]
        a = jnp.concatenate([_dot(pooled[g].astype(BF16), wpool_ref[g]) for g in range(N_GROUPS)], axis=1)
        a = a * ps_ref[...]
        project(1)
        zu = zcur[:, D_MODEL:2 * D_MODEL]
        u, tu = _gelu(zu)
        u_ref[...] = u.astype(BF16)
        gpu_ref[...] = _gelu_grad(zu, tu).astype(BF16)
        project(2)
        zv = zcur[:, 2 * D_MODEL:3 * D_MODEL]
        gv, tv = _gelu(zv)
        xc = gv - _mean(gv)
        rln = lax.rsqrt(_mean(xc * xc) + EPS)
        xhat = xc * rln
        xhat_ref[...] = xhat.astype(BF16)
        gpv_ref[...] = (_gelu_grad(zv, tv) * rln).astype(BF16)
        vb = (xhat * lng_ref[...] + lnb_ref[...]).astype(BF16)
        wsm = _masked_ws(ws_ref)
        bbr = []
        for h in range(N_HEADS):
            project(3 + h)
            sv = _spatial_head(wsm[h], vb, bsp_ref, h, tm // SGU_BLOCK)
            bbr.append(_dot((u[:, h * HEAD:(h + 1) * HEAD] * sv).astype(BF16), wproj_ref[h]))
        bbr = jnp.concatenate(bbr, axis=1)
        project(7)
        sa = _sigmoid(zcur[:, 3 * D_MODEL:4 * D_MODEL])
        sb = _sigmoid(zcur[:, 4 * D_MODEL:5 * D_MODEL])
        sa_ref[...] = sa.astype(BF16)
        sb_ref[...] = sb.astype(BF16)
        project(8)
        yv = _dot((sa * a + sb * bbr).astype(BF16), wout_ref[...])
        y_ref[...] = yv
        project(9)
        ry = lax.rsqrt(_mean(yv * yv) + EPS)
        h1_ref[...] = xb_ref[...] + yv * ry * g1post_ref[...]

    proj = lambda w: pl.BlockSpec((tm, w), lambda s: (jnp.minimum(s, nt - 1), 0))
    mix = lambda w: pl.BlockSpec((tm, w), lambda s: (jnp.maximum(s - 1, 0), 0))
    res = pl.pallas_call(
        body, name="fwd_mix", grid=(nt + 1,),
        in_specs=[_ANY, proj(D_MODEL), mix(D_MODEL),
                  _const_spec((D_MODEL, D_IN)),
                  _const_spec((1, D_IN)), _const_spec((N_GROUPS, GROUP, GROUP)), _const_spec((1, D_MODEL)),
                  _const_spec((1, D_MODEL)), _const_spec((1, D_MODEL)),
                  _const_spec((N_HEADS, SGU_BLOCK, SGU_BLOCK)), _const_spec((SGU_BLOCK, N_HEADS)),
                  _const_spec((N_HEADS, HEAD, HEAD)), _const_spec((D_MODEL, D_MODEL)), _const_spec((1, D_MODEL))]
        + [_HBM] * na + [_SEM] * ns,
        out_specs=[proj(D_MODEL), mix(D_MODEL), mix(D_MODEL)] + [mix(D_MODEL)] * 6 + [_SEM] * ns + [_HBM] * na,
        out_shape=[jax.ShapeDtypeStruct((t_len, D_MODEL), F32),
                   jax.ShapeDtypeStruct((t_len, D_MODEL), F32), jax.ShapeDtypeStruct((t_len, D_MODEL), F32)]
        + [jax.ShapeDtypeStruct((t_len, D_MODEL), BF16)] * 6
        + [pltpu.SemaphoreType.DMA(())] * ns + [pltpu.HBM(a.shape, a.dtype) for a in hop_arrays],
        scratch_shapes=[pltpu.VMEM((tm, D_IN), F32), pltpu.VMEM((tm, D_IN), F32),
                        pltpu.VMEM((tm + HALO, D_MODEL), F32)],
        input_output_aliases={n_in + i: n_out + ns + i for i in range(na)},
        compiler_params=pltpu.CompilerParams(dimension_semantics=("arbitrary",), vmem_limit_bytes=VMEM_LIMIT,
                                             **({"has_side_effects": _EFFECT} if hop is not None else {})),
    )(after, xn, x, win_g, b_in, wpool, pool_scale, lng, lnb, ws, bsp_t, wproj, wout, g1post,
      *[_in_hbm(a) for a in hop_arrays], *hop_sems)
    if hop is None:
        return res
    return list(res[:n_out]) + [list(res[n_out:n_out + ns]), list(res[n_out + ns:])]


def _mlp(h1, target, g2pre, g2post, w1_g, w2):
    t_len = h1.shape[0]
    tm = TM
    nt = t_len // tm

    def body(h1_ref, tgt_ref, g2pre_ref, g2post_ref, w1_ref, w2_ref,
             hn_ref, f_ref, df1_ref, df2_ref, dh1_ref, dg2post_ref, dg2pre_ref, loss_ref, f1_scr):
        i = pl.program_id(0)

        @pl.when(i == 0)
        def _():
            dg2post_ref[...] = jnp.zeros_like(dg2post_ref)
            dg2pre_ref[...] = jnp.zeros_like(dg2pre_ref)
            loss_ref[...] = jnp.zeros_like(loss_ref)

        h = h1_ref[...]
        r2 = lax.rsqrt(_mean(h * h) + EPS)
        nh = h * r2
        hnb = (nh * g2pre_ref[...]).astype(BF16)
        hn_ref[...] = hnb.T
        for k in range(N_DEV):
            f1_scr[:, k * FF_SHARD:(k + 1) * FF_SHARD] = _dot(hnb, w1_ref[:, k * FF_SHARD:(k + 1) * FF_SHARD])
        r = jnp.maximum(f1_scr[...], 0.0)
        fb = (r * r).astype(BF16)
        f_ref[...] = fb.T
        f2 = _dot(fb, w2_ref[...])
        rf = lax.rsqrt(_mean(f2 * f2) + EPS)
        nf = f2 * rf
        diff = h + nf * g2post_ref[...] - tgt_ref[...]
        loss_ref[...] += (0.5 / D_MODEL) * jnp.sum(diff * diff)
        dout = diff * (1.0 / D_MODEL)
        dg2post_ref[...] += _colsum(dout * nf)
        dn = dout * g2post_ref[...]
        df2b = (rf * (dn - nf * _mean(dn * nf))).astype(BF16)
        df2_ref[...] = df2b
        df = _dot_nt(df2b, w2_ref[...])
        df1b = (df * (2.0 * jnp.maximum(f1_scr[...], 0.0))).astype(BF16)
        df1_ref[...] = df1b
        dhn = _dot_nt(df1b, w1_ref[...])
        dg2pre_ref[...] += _colsum(dhn * nh)
        dnh = dhn * g2pre_ref[...]
        dh1_ref[...] = dout + r2 * (dnh - nh * _mean(dnh * nh))

    tok = lambda w: pl.BlockSpec((tm, w), lambda i: (i, 0))
    return pl.pallas_call(
        body, name="mlp_fwd_bwd", grid=(nt,),
        in_specs=[tok(D_MODEL), tok(D_MODEL), _const_spec((1, D_MODEL)), _const_spec((1, D_MODEL)),
                  _const_spec((D_MODEL, D_FF)), _const_spec((D_FF, D_MODEL))],
        out_specs=[pl.BlockSpec((D_MODEL, tm), lambda i: (0, i)), pl.BlockSpec((D_FF, tm), lambda i: (0, i)),
                   tok(D_FF), tok(D_MODEL), tok(D_MODEL),
                   _acc_spec((1, D_MODEL)), _acc_spec((1, D_MODEL)), _acc_spec((1, 128))],
        out_shape=[jax.ShapeDtypeStruct((D_MODEL, t_len), BF16), jax.ShapeDtypeStruct((D_FF, t_len), BF16),
                   jax.ShapeDtypeStruct((t_len, D_FF), BF16), jax.ShapeDtypeStruct((t_len, D_MODEL), BF16),
                   jax.ShapeDtypeStruct((t_len, D_MODEL), F32), jax.ShapeDtypeStruct((1, D_MODEL), F32),
                   jax.ShapeDtypeStruct((1, D_MODEL), F32), jax.ShapeDtypeStruct((1, 128), F32)],
        scratch_shapes=[pltpu.VMEM((tm, D_FF), F32)],
        compiler_params=pltpu.CompilerParams(dimension_semantics=("arbitrary",), vmem_limit_bytes=VMEM_LIMIT),
    )(h1, target, g2pre, g2post, w1_g, w2)


def _bwd_mix(after, dh1, y, z, saved, wpool, pool_scale, lng, lnb, ws, bsp_t, wproj, wout, g1post):
    t_len = y.shape[0]
    tm = TM_BWD
    nt = t_len // tm
    nblk = tm // SGU_BLOCK

    def body(after_ref, dh1_ref, y_ref, z_ref, zh_ref, u_ref, gpu_ref, xhat_ref, gpv_ref, sa_ref, sb_ref,
             wpool_ref, ps_ref, lng_ref, lnb_ref, ws_ref,
             bsp_ref, wproj_ref, wout_ref, g1post_ref,
             dz_ref, mg_ref, dy_ref, ppool_ref, pproj_ref, dws_ref, dbsp_ref, dg1post_ref, dps_ref,
             dlng_ref, dlnb_ref, dbin_ref, pbuf, qbuf, dwpool_ref, dwproj_ref):
        del after_ref
        i = pl.program_id(0)
        ti = nt - 1 - i

        @pl.when(i == 0)
        def _():
            for ref in (dwpool_ref, dwproj_ref, dws_ref, dbsp_ref, dg1post_ref, dps_ref, dlng_ref, dlnb_ref,
                        dbin_ref):
                ref[...] = jnp.zeros_like(ref)
            qbuf[tm:tm + HALO, :] = jnp.zeros((HALO, D_MODEL), F32)

        pbuf[0:HALO, :] = jnp.where(ti > 0, zh_ref[...], 0.0)
        pbuf[HALO:, :] = z_ref[...]
        pooled = _pool_fwd(pbuf, ti, tm)
        pooled_b = [p.astype(BF16) for p in pooled]
        a_raw = jnp.concatenate([_dot(pooled_b[g], wpool_ref[g]) for g in range(N_GROUPS)], axis=1)
        wsm = _masked_ws(ws_ref)
        u = u_ref[...].astype(F32)
        xhat = xhat_ref[...].astype(F32)
        vb = (xhat * lng_ref[...] + lnb_ref[...]).astype(BF16)
        sv_heads = [_spatial_head(wsm[h], vb, bsp_ref, h, nblk) for h in range(N_HEADS)]
        gated_b = [(u[:, h * HEAD:(h + 1) * HEAD] * sv_heads[h]).astype(BF16) for h in range(N_HEADS)]
        bbr = jnp.concatenate([_dot(gated_b[h], wproj_ref[h]) for h in range(N_HEADS)], axis=1)
        sa = sa_ref[...].astype(F32)
        sb = sb_ref[...].astype(F32)
        a = a_raw * ps_ref[...]
        mg_ref[...] = (sa * a + sb * bbr).astype(BF16).T

        dh = dh1_ref[...]
        yv = y_ref[...]
        ry = lax.rsqrt(_mean(yv * yv) + EPS)
        ny = yv * ry
        dg1post_ref[...] += _colsum(dh * ny)
        dn = dh * g1post_ref[...]
        dyb = (ry * (dn - ny * _mean(dn * ny))).astype(BF16)
        dy_ref[...] = dyb
        dmg = _dot_nt(dyb, wout_ref[...])

        da = dmg * sa
        dbbr = dmg * sb
        dzga = dmg * a * sa * (1.0 - sa)
        dzgb = dmg * bbr * sb * (1.0 - sb)
        dz_ref[:, 3 * D_MODEL:4 * D_MODEL] = dzga.astype(BF16)
        dz_ref[:, 4 * D_MODEL:5 * D_MODEL] = dzgb.astype(BF16)
        dbin_ref[:, 3 * D_MODEL:4 * D_MODEL] += _colsum(dzga)
        dbin_ref[:, 4 * D_MODEL:5 * D_MODEL] += _colsum(dzgb)

        dps_ref[...] += _colsum(da * a_raw)
        da_raw_b = (da * ps_ref[...]).astype(BF16)
        pos = lax.broadcasted_iota(jnp.int32, (tm, 1), 0) + ti * tm + 1
        dpooled = []
        for g, w in enumerate(WINDOWS):
            cols = slice(g * GROUP, (g + 1) * GROUP)
            dwpool_ref[g] += _dot_tn(pooled_b[g], da_raw_b[:, cols])
            dp = _dot_nt(da_raw_b[:, cols], wpool_ref[g])
            dpooled.append(dp)
            qbuf[0:tm, cols] = dp * (1.0 / jnp.minimum(pos, w).astype(F32))
        n_ext = tm + HALO
        dzp = []
        for g, w in enumerate(WINDOWS):
            e = qbuf[:, g * GROUP:(g + 1) * GROUP]
            s, sh = e, 1
            while sh < w:
                s = s + pltpu.roll(s, n_ext - sh, 0)
                sh *= 2
            dzp.append(s[0:tm] - dpooled[g])
        qbuf[tm:tm + HALO, :] = qbuf[0:HALO, :]
        dzp = jnp.concatenate(dzp, axis=1)
        dz_ref[:, 0:D_MODEL] = dzp.astype(BF16)
        dbin_ref[:, 0:D_MODEL] += _colsum(dzp)

        dv_heads = []
        du_heads = []
        for h in range(N_HEADS):
            cols = slice(h * HEAD, (h + 1) * HEAD)
            dbbr_b = dbbr[:, cols].astype(BF16)
            dwproj_ref[h] += _dot_tn(gated_b[h], dbbr_b)
            dgated = _dot_nt(dbbr_b, wproj_ref[h])
            du_heads.append(dgated * sv_heads[h])
            dsv = dgated * u[:, cols]
            dsv_b = dsv.astype(BF16)
            rows = []
            for n in range(nblk):
                blk = slice(n * SGU_BLOCK, (n + 1) * SGU_BLOCK)
                rows.append(_dot_tn(wsm[h], dsv_b[blk]))
                dws_ref[h] += _dot_nt(dsv_b[blk], vb[blk, cols])
                dbsp_ref[:, h:h + 1] += jnp.sum(dsv[blk], axis=1, keepdims=True)
            dv_heads.append(jnp.concatenate(rows, axis=0))
        dzu = jnp.concatenate(du_heads, axis=1) * gpu_ref[...].astype(F32)
        dz_ref[:, D_MODEL:2 * D_MODEL] = dzu.astype(BF16)
        dbin_ref[:, D_MODEL:2 * D_MODEL] += _colsum(dzu)
        dv = jnp.concatenate(dv_heads, axis=1)
        dlng_ref[...] += _colsum(dv * xhat)
        dlnb_ref[...] += _colsum(dv)
        dxh = dv * lng_ref[...]
        dzv = (dxh - _mean(dxh) - xhat * _mean(dxh * xhat)) * gpv_ref[...].astype(F32)
        dz_ref[:, 2 * D_MODEL:3 * D_MODEL] = dzv.astype(BF16)
        dbin_ref[:, 2 * D_MODEL:3 * D_MODEL] += _colsum(dzv)

        @pl.when(i == nt - 1)
        def _():
            ri = lax.broadcasted_iota(jnp.int32, (SGU_BLOCK, SGU_BLOCK), 0) // CHUNK
            ci = lax.broadcasted_iota(jnp.int32, (SGU_BLOCK, SGU_BLOCK), 1) // CHUNK
            for h in range(N_HEADS):
                dws_ref[h] = jnp.where(ri >= ci, dws_ref[h], 0.0)
            for slot in range(N_DEV):
                own = slice(_owner_of_slot(slot) * PG_SHARD, (_owner_of_slot(slot) + 1) * PG_SHARD)
                for g in range(N_GROUPS):
                    rows = slice(g * PG_SHARD, (g + 1) * PG_SHARD)
                    ppool_ref[slot, rows, :] = dwpool_ref[g, own, :].astype(BF16)
                    pproj_ref[slot, rows, :] = dwproj_ref[g, own, :].astype(BF16)

    tok = lambda w: pl.BlockSpec((tm, w), lambda i: (nt - 1 - i, 0))
    halo = pl.BlockSpec((HALO, D_MODEL), lambda i: (jnp.maximum((nt - 1 - i) * (tm // HALO) - 1, 0), 0))
    return pl.pallas_call(
        body, name="bwd_mix", grid=(nt,),
        in_specs=[_ANY, tok(D_MODEL), tok(D_MODEL), tok(D_MODEL), halo] + [tok(D_MODEL)] * 6
        + [_const_spec((N_GROUPS, GROUP, GROUP)),
                  _const_spec((1, D_MODEL)), _const_spec((1, D_MODEL)), _const_spec((1, D_MODEL)),
                  _const_spec((N_HEADS, SGU_BLOCK, SGU_BLOCK)), _const_spec((SGU_BLOCK, N_HEADS)),
                  _const_spec((N_HEADS, HEAD, HEAD)), _const_spec((D_MODEL, D_MODEL)), _const_spec((1, D_MODEL))],
        out_specs=[tok(D_IN), pl.BlockSpec((D_MODEL, tm), lambda i: (0, nt - 1 - i)), tok(D_MODEL),
                   _acc_spec((N_DEV, N_GROUPS * PG_SHARD, GROUP)), _acc_spec((N_DEV, N_HEADS * PG_SHARD, HEAD)),
                   _acc_spec((N_HEADS, SGU_BLOCK, SGU_BLOCK)), _acc_spec((SGU_BLOCK, N_HEADS)),
                   _acc_spec((1, D_MODEL)), _acc_spec((1, D_MODEL)), _acc_spec((1, D_MODEL)), _acc_spec((1, D_MODEL)),
                   _acc_spec((1, D_IN))],
        out_shape=[jax.ShapeDtypeStruct((t_len, D_IN), BF16),
                   jax.ShapeDtypeStruct((D_MODEL, t_len), BF16), jax.ShapeDtypeStruct((t_len, D_MODEL), BF16),
                   jax.ShapeDtypeStruct((N_DEV, N_GROUPS * PG_SHARD, GROUP), BF16),
                   jax.ShapeDtypeStruct((N_DEV, N_HEADS * PG_SHARD, HEAD), BF16),
                   jax.ShapeDtypeStruct((N_HEADS, SGU_BLOCK, SGU_BLOCK), F32),
                   jax.ShapeDtypeStruct((SGU_BLOCK, N_HEADS), F32),
                   jax.ShapeDtypeStruct((1, D_MODEL), F32), jax.ShapeDtypeStruct((1, D_MODEL), F32),
                   jax.ShapeDtypeStruct((1, D_MODEL), F32), jax.ShapeDtypeStruct((1, D_MODEL), F32),
                   jax.ShapeDtypeStruct((1, D_IN), F32)],
        scratch_shapes=[pltpu.VMEM((tm + HALO, D_MODEL), F32), pltpu.VMEM((tm + HALO, D_MODEL), F32),
                        pltpu.VMEM((N_GROUPS, GROUP, GROUP), F32), pltpu.VMEM((N_HEADS, HEAD, HEAD), F32)],
        compiler_params=pltpu.CompilerParams(dimension_semantics=("arbitrary",), vmem_limit_bytes=VMEM_LIMIT),
    )(after, dh1, y, z, z, *saved, wpool, pool_scale, lng, lnb, ws, bsp_t, wproj, wout, g1post)


def _bwd_in(after, dz_b, x, dh1, g1pre, win_g):
    t_len = x.shape[0]
    tm = TM_IN
    nt = t_len // tm

    def body(after_ref, dz_ref, x_ref, dh1_ref, g1_ref, win_ref, dx_ref, dg1pre_ref):
        del after_ref

        @pl.when(pl.program_id(0) == 0)
        def _():
            dg1pre_ref[...] = jnp.zeros_like(dg1pre_ref)

        dxn = _dot_nt(dz_ref[...], win_ref[...])
        xv = x_ref[...]
        r1 = lax.rsqrt(_mean(xv * xv) + EPS)
        nx = xv * r1
        dg1pre_ref[...] += _colsum(dxn * nx)
        dnx = dxn * g1_ref[...]
        dx_ref[...] = r1 * (dnx - nx * _mean(dnx * nx)) + dh1_ref[...]

    tok = lambda w: pl.BlockSpec((tm, w), lambda i: (i, 0))
    return pl.pallas_call(
        body, name="bwd_in", grid=(nt,),
        in_specs=[_ANY, tok(D_IN), tok(D_MODEL), tok(D_MODEL), _const_spec((1, D_MODEL)),
                  _const_spec((D_MODEL, D_IN))],
        out_specs=[tok(D_MODEL), _acc_spec((1, D_MODEL))],
        out_shape=[jax.ShapeDtypeStruct((t_len, D_MODEL), F32), jax.ShapeDtypeStruct((1, D_MODEL), F32)],
        compiler_params=pltpu.CompilerParams(dimension_semantics=("arbitrary",), vmem_limit_bytes=VMEM_LIMIT),
    )(after, dz_b, x, dh1, g1pre, win_g)


def _owner_of_slot(s):
    return 4 * ((s // 2) % 2) + 2 * (s % 2) + s // 4


def _wgrad_ff(c_arr, f_t, df2_b, hn_t, df1_b):
    t_len = df2_b.shape[0]
    half_shard = FF_SHARD // 2

    def body(c_ref, f_ref, df2_ref, hn_ref, df1_ref, o2_ref, o1_ref, buf2, buf1, land2, land1, send_sems, recv_sems):
        del c_ref
        s = pl.program_id(0)
        k, h = (s // 2) % N_CHIPS, s % 2
        x, y, c = _coords()

        def copies(k):
            return [pltpu.make_async_remote_copy(src_ref=buf.at[k], dst_ref=land.at[k], send_sem=send_sems.at[j, k],
                                                 recv_sem=recv_sems.at[j, k], device_id=(x, y, 1 - c),
                                                 device_id_type=MESH)
                    for j, (buf, land) in enumerate(((buf2, land2), (buf1, land1)))]

        @pl.when(s < N_DEV)
        def _():
            buf2[k, h] = _dot(f_ref[...], df2_ref[...]).astype(BF16)
            buf1[k, h] = _dot(hn_ref[...], df1_ref[...]).astype(BF16)

            @pl.when(h == 1)
            def _():
                for cp in copies(k):
                    cp.start()

        @pl.when(s >= N_DEV)
        def _():
            @pl.when(h == 0)
            def _():
                for cp in copies(k):
                    cp.wait_recv()

            o2_ref[...] = (_dot(f_ref[...], df2_ref[...]) + land2[k, h].astype(F32)).astype(BF16)
            o1_ref[...] = (_dot(hn_ref[...], df1_ref[...]) + land1[k, h].astype(F32)).astype(BF16)

        @pl.when(s == 2 * N_DEV - 1)
        def _():
            for k_sent in range(N_CHIPS):
                for cp in copies(k_sent):
                    cp.wait_send()

    def part(s, c_ref):
        owner = 2 * ((s // 2) % N_CHIPS) + jnp.where(s < N_DEV, 1 - c_ref[0], c_ref[0])
        return 2 * owner + s % 2

    mine_slot = lambda s: jnp.maximum(s // 2 - N_CHIPS, 0)
    mine_half = lambda s: jnp.where(s >= N_DEV, s % 2, 0)
    return pl.pallas_call(
        body, name="wgrad_ff",
        grid_spec=pltpu.PrefetchScalarGridSpec(
            num_scalar_prefetch=1, grid=(2 * N_DEV,),
            in_specs=[pl.BlockSpec((half_shard, t_len), lambda s, c_ref: (part(s, c_ref), 0)),
                      pl.BlockSpec((t_len, D_MODEL), lambda s, c_ref: (0, 0), pipeline_mode=pl.Buffered(1)),
                      pl.BlockSpec((D_MODEL, t_len), lambda s, c_ref: (0, 0), pipeline_mode=pl.Buffered(1)),
                      pl.BlockSpec((t_len, half_shard), lambda s, c_ref: (0, part(s, c_ref)))],
            out_specs=[pl.BlockSpec((None, half_shard, D_MODEL), lambda s, c_ref: (mine_slot(s), mine_half(s), 0)),
                       pl.BlockSpec((None, D_MODEL, half_shard), lambda s, c_ref: (mine_slot(s), 0, mine_half(s)))],
            scratch_shapes=[pltpu.VMEM((N_CHIPS, 2, half_shard, D_MODEL), BF16),
                            pltpu.VMEM((N_CHIPS, 2, D_MODEL, half_shard), BF16)] * 2
            + [pltpu.SemaphoreType.DMA((2, N_CHIPS))] * 2),
        out_shape=[jax.ShapeDtypeStruct((N_CHIPS, FF_SHARD, D_MODEL), BF16),
                   jax.ShapeDtypeStruct((N_CHIPS, D_MODEL, FF_SHARD), BF16)],
        compiler_params=pltpu.CompilerParams(dimension_semantics=("arbitrary",), vmem_limit_bytes=VMEM_LIMIT),
    )(c_arr, f_t, df2_b, hn_t, df1_b)


def _wgrad_pair(name, a, b, in_specs, halves, slot_shape, bigs, smalls):
    nb, nx = len(bigs), len(bigs) + len(smalls)

    def body(a_ref, b_ref, *refs):
        extra_in, out_ref, extra_out = refs[:nx], refs[nx], refs[nx + 1:2 * nx + 1]
        both, land = refs[2 * nx + 1:2 * nx + 3]
        extra_land = refs[2 * nx + 3:3 * nx + 3]
        send_sems, recv_sems, extra_send, extra_recv = refs[3 * nx + 3:]
        q = pl.program_id(0)
        x, y, c = _coords()
        to_sibling = dict(device_id=(x, y, 1 - c), device_id_type=MESH)

        def slot_copy(k):
            return pltpu.make_async_remote_copy(src_ref=both.at[k % 2, 1 - c], dst_ref=land.at[k],
                                                send_sem=send_sems.at[k], recv_sem=recv_sems.at[k], **to_sibling)

        def extra_copies():
            return [pltpu.make_async_remote_copy(
                src_ref=extra_in[j].at[pl.ds(N_CHIPS * (1 - c), N_CHIPS)] if j < nb else extra_in[j],
                dst_ref=extra_land[j], send_sem=extra_send.at[j], recv_sem=extra_recv.at[j], **to_sibling)
                for j in range(nx)]

        @pl.when(q == 0)
        def _():
            for cp in extra_copies():
                cp.start()

        @pl.when(q < N_CHIPS)
        def _():
            @pl.when(q >= 2)
            def _():
                slot_copy(q - 2).wait_send()

            half0, half1 = halves(_dot(a_ref[...], b_ref[...]))
            both[q % 2, 0] = half0.astype(BF16)
            both[q % 2, 1] = half1.astype(BF16)
            slot_copy(q).start()

        @pl.when(q >= 1)
        def _():
            k = q - 1
            slot_copy(k).wait_recv()
            out_ref[...] = (both[k % 2, c].astype(F32) + land[k].astype(F32)).astype(BF16)

        @pl.when(q == N_CHIPS)
        def _():
            slot_copy(N_CHIPS - 2).wait_send()
            slot_copy(N_CHIPS - 1).wait_send()
            for cp in extra_copies():
                cp.wait()
            for j in range(nx):
                if j < nb:
                    mine = extra_in[j][pl.ds(N_CHIPS * c, N_CHIPS)]
                    extra_out[j][...] = (mine.astype(F32) + extra_land[j][...].astype(F32)).astype(BF16)
                else:
                    extra_out[j][...] = extra_in[j][...] + extra_land[j][...]

    got_shapes = [(N_CHIPS,) + t.shape[1:] for t in bigs] + [t.shape for t in smalls]
    dtypes = [t.dtype for t in bigs] + [t.dtype for t in smalls]
    return pl.pallas_call(
        body, name=name, grid=(N_CHIPS + 1,),
        in_specs=list(in_specs) + [_const_spec(t.shape) for t in list(bigs) + list(smalls)],
        out_specs=[pl.BlockSpec((None,) + slot_shape, lambda q: (jnp.maximum(q - 1, 0), 0, 0))]
        + [pl.BlockSpec(s, lambda q, nd=len(s): (0,) * nd) for s in got_shapes],
        out_shape=[jax.ShapeDtypeStruct((N_CHIPS,) + slot_shape, BF16)]
        + [jax.ShapeDtypeStruct(s, d) for s, d in zip(got_shapes, dtypes)],
        scratch_shapes=[pltpu.VMEM((2, 2) + slot_shape, BF16), pltpu.VMEM((N_CHIPS,) + slot_shape, BF16)]
        + [pltpu.VMEM(s, d) for s, d in zip(got_shapes, dtypes)]
        + [pltpu.SemaphoreType.DMA((N_CHIPS,))] * 2 + [pltpu.SemaphoreType.DMA((max(nx, 1),))] * 2,
        compiler_params=pltpu.CompilerParams(dimension_semantics=("arbitrary",), vmem_limit_bytes=VMEM_LIMIT),
    )(a, b, *bigs, *smalls)


def _wgrad_in(xn_t, dz_b, bigs, smalls):
    t_len = dz_b.shape[0]
    return _wgrad_pair("wgrad_in", xn_t, dz_b,
                       [pl.BlockSpec((D_MODEL, t_len), lambda q: (0, 0), pipeline_mode=pl.Buffered(1)),
                        pl.BlockSpec((t_len, 2 * IN_SHARD), lambda q: (0, jnp.minimum(q, N_CHIPS - 1)))],
                       lambda res: (res[:, 0:IN_SHARD], res[:, IN_SHARD:2 * IN_SHARD]), (D_MODEL, IN_SHARD),
                       bigs, smalls)


def _wgrad_out(mg_t, dy_b, smalls):
    t_len = dy_b.shape[0]
    return _wgrad_pair("wgrad_out", mg_t, dy_b,
                       [pl.BlockSpec((2 * OUT_SHARD, t_len), lambda q: (jnp.minimum(q, N_CHIPS - 1), 0)),
                        pl.BlockSpec((t_len, D_MODEL), lambda q: (0, 0), pipeline_mode=pl.Buffered(1))],
                       lambda res: (res[0:OUT_SHARD], res[OUT_SHARD:2 * OUT_SHARD]), (OUT_SHARD, D_MODEL),
                       [], smalls)


def _coords():
    return lax.axis_index("x"), lax.axis_index("y"), lax.axis_index("c")


_ANY = pl.BlockSpec(memory_space=pl.ANY)


_HBM = pl.BlockSpec(memory_space=pltpu.HBM)
_SEM = pl.BlockSpec(memory_space=pltpu.SEMAPHORE)
_VMEM = pl.BlockSpec(memory_space=pltpu.VMEM)
_EFFECT = pltpu.SideEffectType.DATAFLOW_SIDE_EFFECTING
_TOKEN = jax.ShapeDtypeStruct((8, 128), F32)


def _in_hbm(a):
    return pltpu.with_memory_space_constraint(a, pltpu.HBM)


def _split_call(name, body, n_sems_out, arrays, sems_in=(), after=None):
    na, ns = len(arrays), len(sems_in)
    has_after = after is not None

    def kernel_body(*refs):
        arr = refs[:na]
        s_in = refs[na:na + ns]
        outs = refs[na + ns + has_after:]
        body(arr, s_in, outs[:n_sems_out])
        outs[-1][...] = jnp.zeros((8, 128), F32)

    out_shape = ([pltpu.SemaphoreType.DMA(())] * n_sems_out + [pltpu.HBM(a.shape, a.dtype) for a in arrays] + [_TOKEN])
    res = pl.pallas_call(
        kernel_body, name=name, out_shape=out_shape,
        in_specs=[_HBM] * na + [_SEM] * ns + [_ANY] * has_after,
        out_specs=[_SEM] * n_sems_out + [_HBM] * na + [_VMEM],
        input_output_aliases={i: n_sems_out + i for i in range(na)},
        compiler_params=pltpu.CompilerParams(has_side_effects=_EFFECT),
    )(*[_in_hbm(a) for a in arrays], *sems_in, *([after] if has_after else []))
    return list(res[:n_sems_out]), list(res[n_sems_out:n_sems_out + na]), res[-1]


def _wait_bytes_of(ref, send_sem, recv_sem, peer, send=True, recv=True):
    cp = pltpu.make_async_remote_copy(src_ref=ref, dst_ref=ref, send_sem=send_sem, recv_sem=recv_sem,
                                      device_id=peer, device_id_type=MESH)
    if send:
        cp.wait_send()
    if recv:
        cp.wait_recv()


def _gather_behind(tag, shards, after, work, by_columns=(), by_rows=(), middle_inside_work=False):
    n = len(shards)

    def land_shape(j, s):
        if j in by_columns:
            return (s.shape[0], N_DEV * s.shape[1])
        if j in by_rows:
            return (s.shape[0], N_DEV * s.shape[1], s.shape[2])
        return (N_DEV,) + s.shape

    lands = [lax.empty(land_shape(j, s), s.dtype) for j, s in enumerate(shards)]

    def slots(arr, j, first, count=1):
        if j in by_columns:
            cols = shards[j].shape[1]
            return arr[n + j].at[:, pl.ds(pl.multiple_of(first * cols, 128), count * cols)]
        if j in by_rows:
            rows = shards[j].shape[1]
            return arr[n + j].at[:, pl.ds(pl.multiple_of(first * rows, 16), count * rows), :]
        return arr[n + j].at[first] if count == 1 else arr[n + j].at[pl.ds(first, count)]

    def own_slot(arr, j, sem):
        x, y, c = _coords()
        return pltpu.make_async_copy(arr[j], slots(arr, j, 4 * x + 2 * y + c), sem)

    def start(arr, _, sems):
        x, y, c = _coords()
        me = 4 * x + 2 * y + c
        for j in range(n):
            for chip in [(1 - x, y), (x, 1 - y), (1 - x, 1 - y)]:
                pltpu.make_async_remote_copy(src_ref=arr[j], dst_ref=slots(arr, j, me), send_sem=sems[j],
                                             recv_sem=sems[n + j], device_id=(*chip, c), device_id_type=MESH).start()
        for j in range(n):
            pltpu.make_async_remote_copy(src_ref=arr[j], dst_ref=slots(arr, j, me), send_sem=sems[2 * n + j],
                                         recv_sem=sems[3 * n + j], device_id=(x, y, 1 - c),
                                         device_id_type=MESH).start()
            own_slot(arr, j, sems[4 * n + j]).start()

    def middle(arr, s_in, sems):
        x, y, c = _coords()
        sibling = (x, y, 1 - c)
        for j in range(n):
            _wait_bytes_of(slots(arr, j, 0, 3), s_in[j], s_in[n + j], sibling)
            for chip in [(1 - x, y), (x, 1 - y), (1 - x, 1 - y)]:
                slot = slots(arr, j, 4 * chip[0] + 2 * chip[1] + c)
                pltpu.make_async_remote_copy(src_ref=slot, dst_ref=slot, send_sem=sems[j], recv_sem=sems[n + j],
                                             device_id=sibling, device_id_type=MESH).start()

    def finish(arr, s_in, _):
        x, y, c = _coords()
        sibling = (x, y, 1 - c)
        for j in range(n):
            _wait_bytes_of(slots(arr, j, 0, 1), s_in[j], s_in[n + j], sibling)
            own_slot(arr, j, s_in[2 * n + j]).wait()
            _wait_bytes_of(slots(arr, j, 0, 3), s_in[3 * n + j], s_in[4 * n + j], sibling)

    sems, arrays, token = _split_call("gather_%s_start" % tag, start, 5 * n, list(shards) + lands, after=after)
    if middle_inside_work:
        *result, fwd_sems, arrays = work(token, (arrays, sems[:2 * n], middle))
        token = result[0]
    else:
        result = work(token)
        fwd_sems, arrays, token = _split_call("gather_%s_middle" % tag, middle, 2 * n, arrays, sems_in=sems[:2 * n],
                                              after=result[0])
    _, arrays, _ = _split_call("gather_%s_finish" % tag, finish, 0, arrays, sems_in=sems[2 * n:] + fwd_sems,
                               after=token)
    return arrays[n:], result


def _chip_exchange_behind(tag, bigs, smalls, work):
    nb, n = len(bigs), len(bigs) + len(smalls)
    lands = [lax.empty(s.shape, s.dtype) for s in bigs] + [lax.empty((N_CHIPS,) + s.shape, s.dtype) for s in smalls]

    def own_slot(arr, j, sem):
        x, y, _ = _coords()
        q_me = 2 * x + y
        return pltpu.make_async_copy(arr[j].at[q_me] if j < nb else arr[j], arr[n + j].at[q_me], sem)

    def start(arr, _, sems):
        x, y, c = _coords()
        q_me = 2 * x + y
        for j in range(n):
            for peer in [(1 - x, y, c), (x, 1 - y, c), (1 - x, 1 - y, c)]:
                piece = arr[j].at[2 * peer[0] + peer[1]] if j < nb else arr[j]
                pltpu.make_async_remote_copy(src_ref=piece, dst_ref=arr[n + j].at[q_me],
                                             send_sem=sems[j], recv_sem=sems[n + j], device_id=peer,
                                             device_id_type=MESH).start()
            own_slot(arr, j, sems[2 * n + j]).start()

    def finish(arr, s_in, _):
        x, y, c = _coords()
        for j in range(n):
            _wait_bytes_of(arr[n + j].at[pl.ds(0, 3)], s_in[j], s_in[n + j], (x, y, 1 - c))
            own_slot(arr, j, s_in[2 * n + j]).wait()

    sems, arrays, token = _split_call("chip_exchange_%s_start" % tag, start, 3 * n, list(bigs) + list(smalls) + lands)
    result = work(token)
    _, arrays, _ = _split_call("chip_exchange_%s_finish" % tag, finish, 0, arrays, sems_in=sems, after=result[0])
    return arrays[n:], result


def _adamw_math(w, g, m, v):
    m = ADAM_B1 * m + (1.0 - ADAM_B1) * g
    v = ADAM_B2 * v + (1.0 - ADAM_B2) * (g * g)
    m_hat = m / (1.0 - ADAM_B1 ** ADAM_STEP)
    v_hat = v / (1.0 - ADAM_B2 ** ADAM_STEP)
    delta = -ADAM_LR * (m_hat / (jnp.sqrt(v_hat) + ADAM_EPS) + ADAM_WD * w)
    return delta, m, v


ADAMW_STEPS = 4


def _adamw_big(name, after, chip_sums, params):
    n = len(params)

    def body(after_ref, *refs):
        del after_ref
        outs = refs[4 * n:]
        for j in range(n):
            t_ref, w_ref, m_ref, v_ref = refs[4 * j:4 * j + 4]
            g = t_ref[0].astype(F32)
            for q in range(1, N_CHIPS):
                g = g + t_ref[q].astype(F32)
            d, mn, vn = _adamw_math(w_ref[...], g, m_ref[...], v_ref[...])
            for out, val in zip(outs[4 * j:4 * j + 4], (g, d, mn, vn)):
                out[...] = val

    in_specs, out_specs, out_shape, operands = [_ANY], [], [], [after]
    for t, (w, m, v) in zip(chip_sums, params):
        rows, cols = w.shape
        tr = rows // ADAMW_STEPS
        blk = pl.BlockSpec((tr, cols), lambda r: (r, 0))
        in_specs += [pl.BlockSpec((N_CHIPS, tr, cols), lambda r: (0, r, 0)), blk, blk, blk]
        out_specs += [blk] * 4
        out_shape += [jax.ShapeDtypeStruct((rows, cols), F32)] * 4
        operands += [t, w, m, v]
    res = pl.pallas_call(
        body, name=name, grid=(ADAMW_STEPS,), in_specs=in_specs, out_specs=out_specs, out_shape=out_shape,
        compiler_params=pltpu.CompilerParams(dimension_semantics=("arbitrary",), vmem_limit_bytes=VMEM_LIMIT),
    )(*operands)
    return [res[4 * j:4 * j + 4] for j in range(n)]


SMALL_ROWS = ("loss", "pool_scale", "sgu_ln_g", "sgu_ln_b", "norm1_post_g", "norm2_pre_g", "norm2_post_g")


N_ROW_PARAMS = len(SMALL_ROWS) - 1


def _adamw_small(u_rows, u_others, params):
    n, n_oth = len(params), len(u_others)

    def body(*refs):
        urow_ref, others = refs[0], refs[1:1 + n_oth]
        wmv = refs[1 + n_oth:1 + n_oth + 3 * n]
        loss_ref = refs[1 + n_oth + 3 * n]
        outs = refs[2 + n_oth + 3 * n:]

        def total(ref, idx):
            g = ref[(0,) + idx].astype(F32)
            for q in range(1, N_CHIPS):
                g = g + ref[(q,) + idx].astype(F32)
            return g

        loss_ref[...] = total(urow_ref, (slice(0, 1), slice(0, 1)))
        for p in range(n):
            if p < N_ROW_PARAMS:
                g = total(urow_ref, (slice(p + 1, p + 2), slice(None)))
            else:
                g = total(others[p - N_ROW_PARAMS], (slice(None), slice(None)))
            d, mn, vn = _adamw_math(wmv[3 * p][...], g, wmv[3 * p + 1][...], wmv[3 * p + 2][...])
            outs[4 * p][...] = g
            outs[4 * p + 1][...] = d
            outs[4 * p + 2][...] = mn
            outs[4 * p + 3][...] = vn

    flat = [a for p in params for a in p]
    out_shape = [jax.ShapeDtypeStruct((1, 1), F32)]
    for w, _, _ in params:
        out_shape += [jax.ShapeDtypeStruct(w.shape, F32)] * 4
    return pl.pallas_call(body, name="adamw_small", out_shape=out_shape)(u_rows, *u_others, *flat)


def _to_bf16(name, after, arrays, hop=None):
    n = len(arrays)
    hop_arrays, hop_sems, hop_body = hop if hop is not None else ((), (), None)
    na, ns = len(hop_arrays), len(hop_sems)

    def body(after_ref, *refs):
        del after_ref
        outs = refs[n + na + ns:]
        for j in range(n):
            outs[j][...] = refs[j][...].astype(BF16)
        if hop_body is not None:
            hop_body(refs[n:n + na], refs[n + na:n + na + ns], outs[n:n + ns])

    res = pl.pallas_call(
        body, name=name, in_specs=[_ANY] + [_VMEM] * n + [_HBM] * na + [_SEM] * ns,
        out_specs=[_VMEM] * n + [_SEM] * ns + [_HBM] * na,
        out_shape=[jax.ShapeDtypeStruct(a.shape, BF16) for a in arrays] + [pltpu.SemaphoreType.DMA(())] * ns
        + [pltpu.HBM(a.shape, a.dtype) for a in hop_arrays],
        input_output_aliases={1 + n + i: n + ns + i for i in range(na)},
        compiler_params=pltpu.CompilerParams(**({"has_side_effects": _EFFECT} if hop is not None else {})),
    )(after, *arrays, *[_in_hbm(a) for a in hop_arrays], *hop_sems)
    if hop is None:
        return res
    return list(res[:n]), list(res[n:n + ns]), list(res[n + ns:])


def kernel(x, norm1_pre_g, w_in, b_in, w_pool, pool_scale, sgu_ln_g, sgu_ln_b, w_spatial, b_spatial, w_sgu_proj, w_out, norm1_post_g, norm2_pre_g, w_ff1, w_ff2, norm2_post_g, loss_target, m_norm1_pre_g, m_w_in, m_b_in, m_w_pool, m_pool_scale, m_sgu_ln_g, m_sgu_ln_b, m_w_spatial, m_b_spatial, m_w_sgu_proj, m_w_out, m_norm1_post_g, m_norm2_pre_g, m_w_ff1, m_w_ff2, m_norm2_post_g, v_norm1_pre_g, v_w_in, v_b_in, v_w_pool, v_pool_scale, v_sgu_ln_g, v_sgu_ln_b, v_w_spatial, v_b_spatial, v_w_sgu_proj, v_w_out, v_norm1_post_g, v_norm2_pre_g, v_w_ff1, v_w_ff2, v_norm2_post_g):
    t_len = x.shape[1]
    row = lambda a: a.reshape(1, -1)
    x2 = x.reshape(t_len, D_MODEL)
    tgt2 = loss_target.reshape(t_len, D_MODEL)
    pg2 = lambda a: a.reshape(N_GROUPS * PG_SHARD, GROUP)

    mix_shards_b = _to_bf16("mix_shards_to_bf16", x2, [w_in, w_pool, w_sgu_proj, w_out])

    def prework(token, hop=None):
        xn, xn_b = _prenorm(token, x2, row(norm1_pre_g))
        if hop is None:
            ff_b = _to_bf16("ff_shards_to_bf16", xn, [w_ff1, w_ff2])
            return ff_b[0], xn, xn_b, ff_b
        ff_b, fwd_sems, passed = _to_bf16("ff_shards_to_bf16", xn, [w_ff1, w_ff2], hop)
        return ff_b[0], xn, xn_b, ff_b, fwd_sems, passed

    (win_f, wpool_f, wproj_f, g_out), (_, xn, xn_b, ff_shards_b) = _gather_behind(
        "mix", mix_shards_b, None, prework, by_columns=(0,), by_rows=(1, 2), middle_inside_work=True)
    wout_f = g_out.reshape(D_MODEL, D_MODEL)
    bsp_t = b_spatial.T

    def forward(token, hop=None):
        z, y, h1, *rest = _fwd_mix(token, xn, x2, win_f, row(b_in), wpool_f, row(pool_scale),
                                   row(sgu_ln_g), row(sgu_ln_b), w_spatial, bsp_t, wproj_f, wout_f,
                                   row(norm1_post_g), hop)
        return (h1, z, y, rest) if hop is None else (h1, z, y, rest[:6], rest[6], rest[7])

    (g_ff1, g_ff2), (h1, z, y, saved) = _gather_behind("ff", ff_shards_b, win_f, forward, by_columns=(0,),
                                                       middle_inside_work=True)
    w2_f = g_ff2.reshape(D_FF, D_MODEL)
    hn_b, f_b, df1_b, df2_b, dh1, dg2post, dg2pre, loss_p = _mlp(h1, tgt2, row(norm2_pre_g), row(norm2_post_g),
                                                               g_ff1, w2_f)
    c_arr = lax.axis_index("c").astype(jnp.int32).reshape(1)
    chip_ff2, chip_ff1 = _wgrad_ff(c_arr, f_b, df2_b, hn_b, df1_b)
    chip_ff = [chip_ff1, chip_ff2]

    def backward_mix(token):
        return _bwd_mix(token, dh1, y, z, saved, wpool_f, row(pool_scale), row(sgu_ln_g), row(sgu_ln_b), w_spatial,
                        bsp_t, wproj_f, wout_f, row(norm1_post_g))

    summed_ff, (dz_b, mg_b, dy_b, p_pool, p_proj, dws, dbsp_t, dg1post, dps, dlng, dlnb,
                dbin) = _chip_exchange_behind("ff", chip_ff, [], backward_mix)

    rows = jnp.concatenate([jnp.broadcast_to(loss_p[:, 0:1], (1, D_MODEL)), dps, dlng, dlnb, dg1post, dg2pre, dg2post],
                           axis=0)
    smalls = [rows, dbin, dws.reshape(N_HEADS * SGU_BLOCK, SGU_BLOCK), dbsp_t.T]
    sums = _wgrad_in(xn_b, dz_b, [p_pool, p_proj], smalls)
    chip_bigs, chip_smalls = sums[:3], sums[3:]

    def backward_rest(token):
        dx, dg1pre = _bwd_in(token, dz_b, x2, dh1, row(norm1_pre_g), win_f)
        chip_out, chip_g1pre = _wgrad_out(mg_b, dy_b, [dg1pre])
        return chip_out, dx, chip_g1pre

    summed_in, (chip_out, dx, chip_g1pre) = _chip_exchange_behind("in", chip_bigs, chip_smalls, backward_rest)

    def update_big(token):
        res = _adamw_big("adamw_big", token, list(summed_ff) + list(summed_in[:1]),
                         [(w_ff1, m_w_ff1, v_w_ff1), (w_ff2, m_w_ff2, v_w_ff2), (w_in, m_w_in, v_w_in)])
        return res[2][0], res

    summed_out, (_, upd_big) = _chip_exchange_behind("out", [chip_out], [chip_g1pre], update_big)
    big = {"in": upd_big[2], "ff1": upd_big[0], "ff2": upd_big[1]}
    ws2 = lambda a: a.reshape(N_HEADS * SGU_BLOCK, SGU_BLOCK)
    small_params = [(row(pool_scale), row(m_pool_scale), row(v_pool_scale)),
                    (row(sgu_ln_g), row(m_sgu_ln_g), row(v_sgu_ln_g)),
                    (row(sgu_ln_b), row(m_sgu_ln_b), row(v_sgu_ln_b)),
                    (row(norm1_post_g), row(m_norm1_post_g), row(v_norm1_post_g)),
                    (row(norm2_pre_g), row(m_norm2_pre_g), row(v_norm2_pre_g)),
                    (row(norm2_post_g), row(m_norm2_post_g), row(v_norm2_post_g)),
                    (row(norm1_pre_g), row(m_norm1_pre_g), row(v_norm1_pre_g)),
                    (row(b_in), row(m_b_in), row(v_b_in)),
                    (ws2(w_spatial), ws2(m_w_spatial), ws2(v_w_spatial)),
                    (b_spatial, m_b_spatial, v_b_spatial),
                    (pg2(w_pool), pg2(m_w_pool), pg2(v_w_pool)),
                    (pg2(w_sgu_proj), pg2(m_w_sgu_proj), pg2(v_w_sgu_proj)),
                    (w_out, m_w_out, v_w_out)]
    small_out = _adamw_small(summed_in[3], [summed_out[1]] + list(summed_in[4:])
                             + [summed_in[1], summed_in[2], summed_out[0]], small_params)
    loss = small_out[0].reshape(())
    small_names = SMALL_ROWS[1:] + ("norm1_pre_g", "b_in", "w_spatial", "b_spatial", "w_pool", "w_sgu_proj", "w_out")
    small = {nm: small_out[1 + 4 * p:5 + 4 * p] for p, nm in enumerate(small_names)}

    shapes = {"norm1_pre_g": norm1_pre_g.shape, "w_in": w_in.shape, "b_in": b_in.shape, "w_pool": w_pool.shape,
              "pool_scale": pool_scale.shape, "sgu_ln_g": sgu_ln_g.shape, "sgu_ln_b": sgu_ln_b.shape,
              "w_spatial": w_spatial.shape, "b_spatial": b_spatial.shape, "w_sgu_proj": w_sgu_proj.shape,
              "w_out": w_out.shape, "norm1_post_g": norm1_post_g.shape, "norm2_pre_g": norm2_pre_g.shape,
              "w_ff1": w_ff1.shape, "w_ff2": w_ff2.shape, "norm2_post_g": norm2_post_g.shape}
    source = {"w_in": big["in"], "w_ff1": big["ff1"], "w_ff2": big["ff2"], **small}
    order = list(shapes)
    outs = [loss, dx.reshape(x.shape)]
    for kind in range(4):
        outs += [source[nm][kind].reshape(shapes[nm]) for nm in order]
    return tuple(outs)
```
